```python
import math
import jax, jax.numpy as jnp
from jax import lax
import numpy as np

D_MODEL = 1024
BATCH = 8
SEQ = 4096
DEPTH = 4

CHUNK = 64
N_MIXERS = 3
EPS = 1e-6
N_FOX = (DEPTH + 2) // 3
N_S5 = (DEPTH + 1) // 3
N_POOL = DEPTH // 3
FOX_HEADS = 16
FOX_HEAD_DIM = D_MODEL // FOX_HEADS
Q_BLOCK = 128
FOX_IN = 3 * D_MODEL + FOX_HEADS
FORGET_BIAS_CENTER = 3.0
S5_GROUP = 16
S5_GROUPS = D_MODEL // S5_GROUP
S5_STATE = 64
S5_DT_MIN = 1e-3
S5_DT_MAX = 1e-1
POOL_WINDOWS = (2, 4, 8, 16)
POOL_GROUPS = len(POOL_WINDOWS)
POOL_WIDTH = D_MODEL // POOL_GROUPS
D_FF = -(-8 * D_MODEL // (3 * 256)) * 256

kernel_name = "interleaved_fox_s5_pool_hybrid"

F32 = jnp.float32


def rms_norm(x, g):
    xf = x.astype(F32)
    y = xf * lax.rsqrt(jnp.mean(xf * xf, axis=-1, keepdims=True) + EPS)
    return (y * g.astype(F32)).astype(x.dtype)


def forgetting_attention(h, w_in, b_f, w_out):
    B, S, _ = h.shape
    proj = h @ w_in
    q, k, v, f_logit = jnp.split(proj, [D_MODEL, 2 * D_MODEL, 3 * D_MODEL], axis=-1)

    def heads(t):
        return t.reshape(B, S, FOX_HEADS, FOX_HEAD_DIM).transpose(0, 2, 1, 3).astype(F32)

    q, k, v = heads(q), heads(k), heads(v)
    log_f = jax.nn.log_sigmoid(f_logit.astype(F32) + b_f.astype(F32))
    c = jnp.cumsum(log_f, axis=1).transpose(0, 2, 1)
    n_blk = S // Q_BLOCK
    q_blocks = q.reshape(B, FOX_HEADS, n_blk, Q_BLOCK, FOX_HEAD_DIM).transpose(2, 0, 1, 3, 4)
    c_blocks = c.reshape(B, FOX_HEADS, n_blk, Q_BLOCK).transpose(2, 0, 1, 3)
    starts = jnp.arange(n_blk, dtype=jnp.int32) * Q_BLOCK
    key_pos = jnp.arange(S, dtype=jnp.int32)
    scale = FOX_HEAD_DIM ** -0.5

    def one_block(args):
        q_b, c_b, start = args
        logits = jnp.einsum('bhqd,bhkd->bhqk', q_b, k) * scale
        logits = logits + c_b[..., :, None] - c[..., None, :]
        q_pos = start + jnp.arange(Q_BLOCK, dtype=jnp.int32)
        mask = key_pos[None, :] <= q_pos[:, None]
        p = jax.nn.softmax(jnp.where(mask, logits, -jnp.inf), axis=-1)
        return jnp.einsum('bhqk,bhkd->bhqd', p, v)

    out = lax.map(one_block, (q_blocks, c_blocks, starts))
    out = out.transpose(1, 0, 3, 2, 4).reshape(B, S, D_MODEL).astype(h.dtype)
    return out @ w_out


def s5_layer(h, w_in, a_re, a_im, log_dt, b_re, b_im, c_re, c_im, d, w_glu):
    B, S, _ = h.shape
    u = (h @ w_in).astype(F32)
    u_g = u.reshape(B, S, S5_GROUPS, S5_GROUP).astype(jnp.complex64)
    lam = lax.complex(a_re.astype(F32), a_im.astype(F32))
    dt = jnp.exp(log_dt.astype(F32))[:, None]
    lam_bar = jnp.exp(lam * dt)
    b_mat = lax.complex(b_re.astype(F32), b_im.astype(F32))
    b_bar = ((lam_bar - 1.0) / lam)[..., None] * b_mat
    bu = jnp.einsum('gpc,bsgc->bsgp', b_bar, u_g)
    lam_seq = jnp.broadcast_to(lam_bar, bu.shape)

    def combine(left, right):
        a_l, s_l = left
        a_r, s_r = right
        return a_r * a_l, a_r * s_l + s_r

    _, states = lax.associative_scan(combine, (lam_seq, bu), axis=1)
    c_mat = lax.complex(c_re.astype(F32), c_im.astype(F32))
    y = jnp.einsum('gcp,bsgp->bsgc', c_mat, states).real.reshape(B, S, D_MODEL)
    y = y + d.astype(F32) * u
    g = jax.nn.gelu(y).astype(h.dtype)
    val, gate = jnp.split(g @ w_glu, 2, axis=-1)
    return val * jax.nn.sigmoid(gate)


def multiscale_pool(h, w_grp, b_grp, scale):
    B, S, _ = h.shape
    hf = h.astype(F32).reshape(B, S, POOL_GROUPS, POOL_WIDTH)
    count_base = jnp.arange(1, S + 1, dtype=F32)[None, :, None]
    outs = []
    for gi, w in enumerate(POOL_WINDOWS):
        xg = hf[:, :, gi]
        cs = jnp.cumsum(xg, axis=1)
        lag = jnp.pad(cs[:, :-w], ((0, 0), (w, 0), (0, 0)))
        mean = (cs - lag) / jnp.minimum(count_base, float(w))
        outs.append(jnp.einsum('bsc,cd->bsd', mean - xg, w_grp[gi].astype(F32)))
    y = jnp.concatenate(outs, axis=-1) + b_grp.astype(F32)
    return (y * scale.astype(F32)).astype(h.dtype)


def swiglu(h, w_gate_up, w_down):
    g, u = jnp.split(h @ w_gate_up, 2, axis=-1)
    return (jax.nn.silu(g) * u) @ w_down


def _fwd_setup_inputs(seed: int = 0) -> dict:
    key = jax.random.key(seed)
    ks = jax.random.split(key, 24)
    D, H, G, P, C = D_MODEL, FOX_HEADS, S5_GROUPS, S5_STATE, S5_GROUP
    nrm = jax.random.normal
    x = nrm(ks[0], (BATCH, SEQ, D), F32)
    mix_norm_g = 1.0 + 0.05 * nrm(ks[1], (DEPTH, D), F32)
    ffn_norm_g = 1.0 + 0.05 * nrm(ks[2], (DEPTH, D), F32)
    final_norm_g = 1.0 + 0.05 * nrm(ks[3], (D,), F32)
    fox_w_in = nrm(ks[4], (N_FOX, D, FOX_IN), F32) * D ** -0.5
    fox_b_f = FORGET_BIAS_CENTER + 0.5 * nrm(ks[5], (N_FOX, H), F32)
    fox_w_out = nrm(ks[6], (N_FOX, D, D), F32) * D ** -0.5
    s5_w_in = nrm(ks[7], (N_S5, D, D), F32) * D ** -0.5
    s5_a_re = -0.5 + 0.01 * nrm(ks[8], (N_S5, G, P), F32)
    s5_a_im = math.pi * jnp.arange(P, dtype=F32)[None, None, :] + 0.01 * nrm(ks[9], (N_S5, G, P), F32)
    s5_log_dt = jax.random.uniform(ks[10], (N_S5, G), F32, math.log(S5_DT_MIN), math.log(S5_DT_MAX))
    s5_b_re = nrm(ks[11], (N_S5, G, P, C), F32) * (2 * C) ** -0.5
    s5_b_im = nrm(ks[12], (N_S5, G, P, C), F32) * (2 * C) ** -0.5
    s5_c_re = nrm(ks[13], (N_S5, G, C, P), F32) * P ** -0.5
    s5_c_im = nrm(ks[14], (N_S5, G, C, P), F32) * P ** -0.5
    s5_d = nrm(ks[15], (N_S5, D), F32)
    s5_w_glu = nrm(ks[16], (N_S5, D, 2 * D), F32) * D ** -0.5
    pool_w = nrm(ks[17], (N_POOL, POOL_GROUPS, POOL_WIDTH, POOL_WIDTH), F32) * POOL_WIDTH ** -0.5
    pool_b = 0.01 * nrm(ks[18], (N_POOL, D), F32)
    pool_scale = 0.5 + 0.05 * nrm(ks[19], (N_POOL, D), F32)
    ffn_w_gate_up = nrm(ks[20], (DEPTH, D, 2 * D_FF), F32) * D ** -0.5
    ffn_w_down = nrm(ks[21], (DEPTH, D_FF, D), F32) * D_FF ** -0.5
    return {"x": x, "mix_norm_g": mix_norm_g, "ffn_norm_g": ffn_norm_g, "final_norm_g": final_norm_g,
            "fox_w_in": fox_w_in, "fox_b_f": fox_b_f, "fox_w_out": fox_w_out,
            "s5_w_in": s5_w_in, "s5_a_re": s5_a_re, "s5_a_im": s5_a_im, "s5_log_dt": s5_log_dt,
            "s5_b_re": s5_b_re, "s5_b_im": s5_b_im, "s5_c_re": s5_c_re, "s5_c_im": s5_c_im,
            "s5_d": s5_d, "s5_w_glu": s5_w_glu,
            "pool_w": pool_w, "pool_b": pool_b, "pool_scale": pool_scale,
            "ffn_w_gate_up": ffn_w_gate_up, "ffn_w_down": ffn_w_down}


def _fwd_reference(x, mix_norm_g, ffn_norm_g, final_norm_g, fox_w_in, fox_b_f, fox_w_out,
              s5_w_in, s5_a_re, s5_a_im, s5_log_dt, s5_b_re, s5_b_im, s5_c_re, s5_c_im,
              s5_d, s5_w_glu, pool_w, pool_b, pool_scale, ffn_w_gate_up, ffn_w_down):
    for i in range(DEPTH):
        h = rms_norm(x, mix_norm_g[i])
        kind, j = i % N_MIXERS, i // N_MIXERS
        if kind == 0:
            mix = forgetting_attention(h, fox_w_in[j], fox_b_f[j], fox_w_out[j])
        elif kind == 1:
            mix = s5_layer(h, s5_w_in[j], s5_a_re[j], s5_a_im[j], s5_log_dt[j], s5_b_re[j], s5_b_im[j],
                           s5_c_re[j], s5_c_im[j], s5_d[j], s5_w_glu[j])
        else:
            mix = multiscale_pool(h, pool_w[j], pool_b[j], pool_scale[j])
        x = x + mix.astype(x.dtype)
        x = x + swiglu(rms_norm(x, ffn_norm_g[i]), ffn_w_gate_up[i], ffn_w_down[i]).astype(x.dtype)
    return rms_norm(x, final_norm_g)


import jax as _jax
import jax.numpy as _jnp

TWIN_FORMAT = 'train_step'
FWD_PARAMS = ['x', 'mix_norm_g', 'ffn_norm_g', 'final_norm_g', 'fox_w_in', 'fox_b_f', 'fox_w_out', 's5_w_in', 's5_a_re', 's5_a_im', 's5_log_dt', 's5_b_re', 's5_b_im', 's5_c_re', 's5_c_im', 's5_d', 's5_w_glu', 'pool_w', 'pool_b', 'pool_scale', 'ffn_w_gate_up', 'ffn_w_down']
TWIN_WEIGHTS = ['mix_norm_g', 'ffn_norm_g', 'final_norm_g', 'fox_w_in', 'fox_b_f', 'fox_w_out', 's5_w_in', 's5_a_re', 's5_a_im', 's5_log_dt', 's5_b_re', 's5_b_im', 's5_c_re', 's5_c_im', 's5_d', 's5_w_glu', 'pool_w', 'pool_b', 'pool_scale', 'ffn_w_gate_up', 'ffn_w_down']
TWIN_DIFF_INPUT = 'x'
TWIN_INPUTS = ['x', 'mix_norm_g', 'ffn_norm_g', 'final_norm_g', 'fox_w_in', 'fox_b_f', 'fox_w_out', 's5_w_in', 's5_a_re', 's5_a_im', 's5_log_dt', 's5_b_re', 's5_b_im', 's5_c_re', 's5_c_im', 's5_d', 's5_w_glu', 'pool_w', 'pool_b', 'pool_scale', 'ffn_w_gate_up', 'ffn_w_down', 'loss_target', 'm_mix_norm_g', 'm_ffn_norm_g', 'm_final_norm_g', 'm_fox_w_in', 'm_fox_b_f', 'm_fox_w_out', 'm_s5_w_in', 'm_s5_a_re', 'm_s5_a_im', 'm_s5_log_dt', 'm_s5_b_re', 'm_s5_b_im', 'm_s5_c_re', 'm_s5_c_im', 'm_s5_d', 'm_s5_w_glu', 'm_pool_w', 'm_pool_b', 'm_pool_scale', 'm_ffn_w_gate_up', 'm_ffn_w_down', 'v_mix_norm_g', 'v_ffn_norm_g', 'v_final_norm_g', 'v_fox_w_in', 'v_fox_b_f', 'v_fox_w_out', 'v_s5_w_in', 'v_s5_a_re', 'v_s5_a_im', 'v_s5_log_dt', 'v_s5_b_re', 'v_s5_b_im', 'v_s5_c_re', 'v_s5_c_im', 'v_s5_d', 'v_s5_w_glu', 'v_pool_w', 'v_pool_b', 'v_pool_scale', 'v_ffn_w_gate_up', 'v_ffn_w_down']
TWIN_OUTPUTS = ['loss', 'grad_x', 'grad_mix_norm_g', 'grad_ffn_norm_g', 'grad_final_norm_g', 'grad_fox_w_in', 'grad_fox_b_f', 'grad_fox_w_out', 'grad_s5_w_in', 'grad_s5_a_re', 'grad_s5_a_im', 'grad_s5_log_dt', 'grad_s5_b_re', 'grad_s5_b_im', 'grad_s5_c_re', 'grad_s5_c_im', 'grad_s5_d', 'grad_s5_w_glu', 'grad_pool_w', 'grad_pool_b', 'grad_pool_scale', 'grad_ffn_w_gate_up', 'grad_ffn_w_down', 'delta_mix_norm_g', 'delta_ffn_norm_g', 'delta_final_norm_g', 'delta_fox_w_in', 'delta_fox_b_f', 'delta_fox_w_out', 'delta_s5_w_in', 'delta_s5_a_re', 'delta_s5_a_im', 'delta_s5_log_dt', 'delta_s5_b_re', 'delta_s5_b_im', 'delta_s5_c_re', 'delta_s5_c_im', 'delta_s5_d', 'delta_s5_w_glu', 'delta_pool_w', 'delta_pool_b', 'delta_pool_scale', 'delta_ffn_w_gate_up', 'delta_ffn_w_down', 'new_m_mix_norm_g', 'new_m_ffn_norm_g', 'new_m_final_norm_g', 'new_m_fox_w_in', 'new_m_fox_b_f', 'new_m_fox_w_out', 'new_m_s5_w_in', 'new_m_s5_a_re', 'new_m_s5_a_im', 'new_m_s5_log_dt', 'new_m_s5_b_re', 'new_m_s5_b_im', 'new_m_s5_c_re', 'new_m_s5_c_im', 'new_m_s5_d', 'new_m_s5_w_glu', 'new_m_pool_w', 'new_m_pool_b', 'new_m_pool_scale', 'new_m_ffn_w_gate_up', 'new_m_ffn_w_down', 'new_v_mix_norm_g', 'new_v_ffn_norm_g', 'new_v_final_norm_g', 'new_v_fox_w_in', 'new_v_fox_b_f', 'new_v_fox_w_out', 'new_v_s5_w_in', 'new_v_s5_a_re', 'new_v_s5_a_im', 'new_v_s5_log_dt', 'new_v_s5_b_re', 'new_v_s5_b_im', 'new_v_s5_c_re', 'new_v_s5_c_im', 'new_v_s5_d', 'new_v_s5_w_glu', 'new_v_pool_w', 'new_v_pool_b', 'new_v_pool_scale', 'new_v_ffn_w_gate_up', 'new_v_ffn_w_down']
TWIN_LEAF_KINDS = {'loss': 'loss', 'grad_x': 'grad_x', 'grad_mix_norm_g': 'grad_w', 'grad_ffn_norm_g': 'grad_w', 'grad_final_norm_g': 'grad_w', 'grad_fox_w_in': 'grad_w', 'grad_fox_b_f': 'grad_w', 'grad_fox_w_out': 'grad_w', 'grad_s5_w_in': 'grad_w', 'grad_s5_a_re': 'grad_w', 'grad_s5_a_im': 'grad_w', 'grad_s5_log_dt': 'grad_w', 'grad_s5_b_re': 'grad_w', 'grad_s5_b_im': 'grad_w', 'grad_s5_c_re': 'grad_w', 'grad_s5_c_im': 'grad_w', 'grad_s5_d': 'grad_w', 'grad_s5_w_glu': 'grad_w', 'grad_pool_w': 'grad_w', 'grad_pool_b': 'grad_w', 'grad_pool_scale': 'grad_w', 'grad_ffn_w_gate_up': 'grad_w', 'grad_ffn_w_down': 'grad_w', 'delta_mix_norm_g': 'delta_w', 'delta_ffn_norm_g': 'delta_w', 'delta_final_norm_g': 'delta_w', 'delta_fox_w_in': 'delta_w', 'delta_fox_b_f': 'delta_w', 'delta_fox_w_out': 'delta_w', 'delta_s5_w_in': 'delta_w', 'delta_s5_a_re': 'delta_w', 'delta_s5_a_im': 'delta_w', 'delta_s5_log_dt': 'delta_w', 'delta_s5_b_re': 'delta_w', 'delta_s5_b_im': 'delta_w', 'delta_s5_c_re': 'delta_w', 'delta_s5_c_im': 'delta_w', 'delta_s5_d': 'delta_w', 'delta_s5_w_glu': 'delta_w', 'delta_pool_w': 'delta_w', 'delta_pool_b': 'delta_w', 'delta_pool_scale': 'delta_w', 'delta_ffn_w_gate_up': 'delta_w', 'delta_ffn_w_down': 'delta_w', 'new_m_mix_norm_g': 'new_m', 'new_m_ffn_norm_g': 'new_m', 'new_m_final_norm_g': 'new_m', 'new_m_fox_w_in': 'new_m', 'new_m_fox_b_f': 'new_m', 'new_m_fox_w_out': 'new_m', 'new_m_s5_w_in': 'new_m', 'new_m_s5_a_re': 'new_m', 'new_m_s5_a_im': 'new_m', 'new_m_s5_log_dt': 'new_m', 'new_m_s5_b_re': 'new_m', 'new_m_s5_b_im': 'new_m', 'new_m_s5_c_re': 'new_m', 'new_m_s5_c_im': 'new_m', 'new_m_s5_d': 'new_m', 'new_m_s5_w_glu': 'new_m', 'new_m_pool_w': 'new_m', 'new_m_pool_b': 'new_m', 'new_m_pool_scale': 'new_m', 'new_m_ffn_w_gate_up': 'new_m', 'new_m_ffn_w_down': 'new_m', 'new_v_mix_norm_g': 'new_v', 'new_v_ffn_norm_g': 'new_v', 'new_v_final_norm_g': 'new_v', 'new_v_fox_w_in': 'new_v', 'new_v_fox_b_f': 'new_v', 'new_v_fox_w_out': 'new_v', 'new_v_s5_w_in': 'new_v', 'new_v_s5_a_re': 'new_v', 'new_v_s5_a_im': 'new_v', 'new_v_s5_log_dt': 'new_v', 'new_v_s5_b_re': 'new_v', 'new_v_s5_b_im': 'new_v', 'new_v_s5_c_re': 'new_v', 'new_v_s5_c_im': 'new_v', 'new_v_s5_d': 'new_v', 'new_v_s5_w_glu': 'new_v', 'new_v_pool_w': 'new_v', 'new_v_pool_b': 'new_v', 'new_v_pool_scale': 'new_v', 'new_v_ffn_w_gate_up': 'new_v', 'new_v_ffn_w_down': 'new_v'}


def _forward(args):
    return _fwd_reference(*[args[k] for k in FWD_PARAMS])


def _output_shape():
    def fwd():
        inp = _fwd_setup_inputs(0)
        return _fwd_reference(*[inp[k] for k in FWD_PARAMS])
    out = _jax.eval_shape(fwd)
    return out.shape, out.dtype

N_MICROBATCH = 1
ADAM_LR = 0.001
ADAM_B1 = 0.9
ADAM_B2 = 0.999
ADAM_EPS = 1e-08
ADAM_WD = 0.01
ADAM_STEP = 10
PER_EXAMPLE_BATCH_AXIS = {'x': 0, 'loss_target': 0}
SHARED_INPUTS = []
_WEIGHT_DTYPES = {'mix_norm_g': _jnp.float32, 'ffn_norm_g': _jnp.float32, 'final_norm_g': _jnp.float32, 'fox_w_in': _jnp.float32, 'fox_b_f': _jnp.float32, 'fox_w_out': _jnp.float32, 's5_w_in': _jnp.float32, 's5_a_re': _jnp.float32, 's5_a_im': _jnp.float32, 's5_log_dt': _jnp.float32, 's5_b_re': _jnp.float32, 's5_b_im': _jnp.float32, 's5_c_re': _jnp.float32, 's5_c_im': _jnp.float32, 's5_d': _jnp.float32, 's5_w_glu': _jnp.float32, 'pool_w': _jnp.float32, 'pool_b': _jnp.float32, 'pool_scale': _jnp.float32, 'ffn_w_gate_up': _jnp.float32, 'ffn_w_down': _jnp.float32}
MOMENT_SCALE = {'mix_norm_g': 8.151282e-02, 'ffn_norm_g': 1.243523e-01, 'final_norm_g': 3.205092e+01, 'fox_w_in': 5.262061e-02, 'fox_b_f': 2.587371e-01, 'fox_w_out': 6.011087e-02, 's5_w_in': 7.153392e-02, 's5_a_re': 4.263810e-03, 's5_a_im': 4.504258e-03, 's5_log_dt': 1.951263e+00, 's5_b_re': 3.109145e-03, 's5_b_im': 3.105608e-03, 's5_c_re': 4.383268e-03, 's5_c_im': 4.326176e-03, 's5_d': 7.896973e-02, 's5_w_glu': 5.653319e-02, 'pool_w': 6.736178e-02, 'pool_b': 1.303268e-01, 'pool_scale': 3.859768e-01, 'ffn_w_gate_up': 5.379160e-02, 'ffn_w_down': 8.804250e-02}


def _to_microbatches(a, axis):
    t = _jnp.moveaxis(a, axis, 0)
    t = t.reshape((N_MICROBATCH, t.shape[0] // N_MICROBATCH) + t.shape[1:])
    return _jnp.moveaxis(t, 1, axis + 1)


def setup_inputs(seed: int = 0) -> dict:
    inp = _fwd_setup_inputs(seed)
    key = _jax.random.fold_in(_jax.random.key(seed), 7919)
    shape, _ = _output_shape()
    out = dict(inp)
    out["loss_target"] = _jax.random.normal(_jax.random.fold_in(key, 0), shape, _jnp.float32)
    for i, name in enumerate(TWIN_WEIGHTS):
        w = inp[name].astype(_jnp.float32)
        if MOMENT_SCALE is None:
            s = _jnp.sqrt(_jnp.mean(_jnp.square(w)) + 1e-30)
        else:
            s = MOMENT_SCALE[name]
        km, kv = _jax.random.split(_jax.random.fold_in(key, i + 1))
        out[name] = w
        out["m_" + name] = s * _jax.random.normal(km, w.shape, _jnp.float32)
        out["v_" + name] = (s * s) * _jax.random.uniform(kv, w.shape, _jnp.float32, 0.5, 1.5)
    if N_MICROBATCH > 1:
        for name, axis in PER_EXAMPLE_BATCH_AXIS.items():
            out[name] = _to_microbatches(out[name], axis)
    return {'x': out['x'], 'mix_norm_g': out['mix_norm_g'], 'ffn_norm_g': out['ffn_norm_g'], 'final_norm_g': out['final_norm_g'], 'fox_w_in': out['fox_w_in'], 'fox_b_f': out['fox_b_f'], 'fox_w_out': out['fox_w_out'], 's5_w_in': out['s5_w_in'], 's5_a_re': out['s5_a_re'], 's5_a_im': out['s5_a_im'], 's5_log_dt': out['s5_log_dt'], 's5_b_re': out['s5_b_re'], 's5_b_im': out['s5_b_im'], 's5_c_re': out['s5_c_re'], 's5_c_im': out['s5_c_im'], 's5_d': out['s5_d'], 's5_w_glu': out['s5_w_glu'], 'pool_w': out['pool_w'], 'pool_b': out['pool_b'], 'pool_scale': out['pool_scale'], 'ffn_w_gate_up': out['ffn_w_gate_up'], 'ffn_w_down': out['ffn_w_down'], 'loss_target': out['loss_target'], 'm_mix_norm_g': out['m_mix_norm_g'], 'm_ffn_norm_g': out['m_ffn_norm_g'], 'm_final_norm_g': out['m_final_norm_g'], 'm_fox_w_in': out['m_fox_w_in'], 'm_fox_b_f': out['m_fox_b_f'], 'm_fox_w_out': out['m_fox_w_out'], 'm_s5_w_in': out['m_s5_w_in'], 'm_s5_a_re': out['m_s5_a_re'], 'm_s5_a_im': out['m_s5_a_im'], 'm_s5_log_dt': out['m_s5_log_dt'], 'm_s5_b_re': out['m_s5_b_re'], 'm_s5_b_im': out['m_s5_b_im'], 'm_s5_c_re': out['m_s5_c_re'], 'm_s5_c_im': out['m_s5_c_im'], 'm_s5_d': out['m_s5_d'], 'm_s5_w_glu': out['m_s5_w_glu'], 'm_pool_w': out['m_pool_w'], 'm_pool_b': out['m_pool_b'], 'm_pool_scale': out['m_pool_scale'], 'm_ffn_w_gate_up': out['m_ffn_w_gate_up'], 'm_ffn_w_down': out['m_ffn_w_down'], 'v_mix_norm_g': out['v_mix_norm_g'], 'v_ffn_norm_g': out['v_ffn_norm_g'], 'v_final_norm_g': out['v_final_norm_g'], 'v_fox_w_in': out['v_fox_w_in'], 'v_fox_b_f': out['v_fox_b_f'], 'v_fox_w_out': out['v_fox_w_out'], 'v_s5_w_in': out['v_s5_w_in'], 'v_s5_a_re': out['v_s5_a_re'], 'v_s5_a_im': out['v_s5_a_im'], 'v_s5_log_dt': out['v_s5_log_dt'], 'v_s5_b_re': out['v_s5_b_re'], 'v_s5_b_im': out['v_s5_b_im'], 'v_s5_c_re': out['v_s5_c_re'], 'v_s5_c_im': out['v_s5_c_im'], 'v_s5_d': out['v_s5_d'], 'v_s5_w_glu': out['v_s5_w_glu'], 'v_pool_w': out['v_pool_w'], 'v_pool_b': out['v_pool_b'], 'v_pool_scale': out['v_pool_scale'], 'v_ffn_w_gate_up': out['v_ffn_w_gate_up'], 'v_ffn_w_down': out['v_ffn_w_down']}


def _loss(weights, diff, rest, loss_target):
    with _jax.named_scope("forward"):
        args = {**rest, TWIN_DIFF_INPUT: diff, **{k: w.astype(_WEIGHT_DTYPES[k]) for k, w in weights.items()}}
        y = _forward(args)
    with _jax.named_scope("loss_head"):
        err = _jnp.square(y.astype(_jnp.float32) - loss_target)
        return 0.5 * _jnp.sum(_jnp.mean(err, axis=-1)) if err.ndim else 0.5 * err


def _adamw(w, g, m, v):
    m = ADAM_B1 * m + (1.0 - ADAM_B1) * g
    v = ADAM_B2 * v + (1.0 - ADAM_B2) * _jnp.square(g)
    m_hat = m / (1.0 - ADAM_B1 ** ADAM_STEP)
    v_hat = v / (1.0 - ADAM_B2 ** ADAM_STEP)
    delta = -ADAM_LR * (m_hat / (_jnp.sqrt(v_hat) + ADAM_EPS) + ADAM_WD * w)
    return delta, m, v


def reference(x, mix_norm_g, ffn_norm_g, final_norm_g, fox_w_in, fox_b_f, fox_w_out, s5_w_in, s5_a_re, s5_a_im, s5_log_dt, s5_b_re, s5_b_im, s5_c_re, s5_c_im, s5_d, s5_w_glu, pool_w, pool_b, pool_scale, ffn_w_gate_up, ffn_w_down, loss_target, m_mix_norm_g, m_ffn_norm_g, m_final_norm_g, m_fox_w_in, m_fox_b_f, m_fox_w_out, m_s5_w_in, m_s5_a_re, m_s5_a_im, m_s5_log_dt, m_s5_b_re, m_s5_b_im, m_s5_c_re, m_s5_c_im, m_s5_d, m_s5_w_glu, m_pool_w, m_pool_b, m_pool_scale, m_ffn_w_gate_up, m_ffn_w_down, v_mix_norm_g, v_ffn_norm_g, v_final_norm_g, v_fox_w_in, v_fox_b_f, v_fox_w_out, v_s5_w_in, v_s5_a_re, v_s5_a_im, v_s5_log_dt, v_s5_b_re, v_s5_b_im, v_s5_c_re, v_s5_c_im, v_s5_d, v_s5_w_glu, v_pool_w, v_pool_b, v_pool_scale, v_ffn_w_gate_up, v_ffn_w_down):
    given = dict(x=x, mix_norm_g=mix_norm_g, ffn_norm_g=ffn_norm_g, final_norm_g=final_norm_g, fox_w_in=fox_w_in, fox_b_f=fox_b_f, fox_w_out=fox_w_out, s5_w_in=s5_w_in, s5_a_re=s5_a_re, s5_a_im=s5_a_im, s5_log_dt=s5_log_dt, s5_b_re=s5_b_re, s5_b_im=s5_b_im, s5_c_re=s5_c_re, s5_c_im=s5_c_im, s5_d=s5_d, s5_w_glu=s5_w_glu, pool_w=pool_w, pool_b=pool_b, pool_scale=pool_scale, ffn_w_gate_up=ffn_w_gate_up, ffn_w_down=ffn_w_down, loss_target=loss_target, m_mix_norm_g=m_mix_norm_g, m_ffn_norm_g=m_ffn_norm_g, m_final_norm_g=m_final_norm_g, m_fox_w_in=m_fox_w_in, m_fox_b_f=m_fox_b_f, m_fox_w_out=m_fox_w_out, m_s5_w_in=m_s5_w_in, m_s5_a_re=m_s5_a_re, m_s5_a_im=m_s5_a_im, m_s5_log_dt=m_s5_log_dt, m_s5_b_re=m_s5_b_re, m_s5_b_im=m_s5_b_im, m_s5_c_re=m_s5_c_re, m_s5_c_im=m_s5_c_im, m_s5_d=m_s5_d, m_s5_w_glu=m_s5_w_glu, m_pool_w=m_pool_w, m_pool_b=m_pool_b, m_pool_scale=m_pool_scale, m_ffn_w_gate_up=m_ffn_w_gate_up, m_ffn_w_down=m_ffn_w_down, v_mix_norm_g=v_mix_norm_g, v_ffn_norm_g=v_ffn_norm_g, v_final_norm_g=v_final_norm_g, v_fox_w_in=v_fox_w_in, v_fox_b_f=v_fox_b_f, v_fox_w_out=v_fox_w_out, v_s5_w_in=v_s5_w_in, v_s5_a_re=v_s5_a_re, v_s5_a_im=v_s5_a_im, v_s5_log_dt=v_s5_log_dt, v_s5_b_re=v_s5_b_re, v_s5_b_im=v_s5_b_im, v_s5_c_re=v_s5_c_re, v_s5_c_im=v_s5_c_im, v_s5_d=v_s5_d, v_s5_w_glu=v_s5_w_glu, v_pool_w=v_pool_w, v_pool_b=v_pool_b, v_pool_scale=v_pool_scale, v_ffn_w_gate_up=v_ffn_w_gate_up, v_ffn_w_down=v_ffn_w_down)
    weights = {n: given[n] for n in TWIN_WEIGHTS}
    shared = {n: given[n] for n in SHARED_INPUTS}
    per_example = {n: given[n] for n in ['x']}
    grad_fn = _jax.value_and_grad(_loss, argnums=(0, 1))

    def one_microbatch(ex, loss_target):
        ex = dict(ex)
        diff = ex.pop(TWIN_DIFF_INPUT)
        return grad_fn(weights, diff, {**shared, **ex}, loss_target)

    if N_MICROBATCH == 1:
        loss, (grad_w, grad_x) = one_microbatch(per_example, given["loss_target"])
    else:
        def body(carry, xs):
            loss_sum, grad_sum = carry
            l_k, (gw_k, gx_k) = one_microbatch(xs[0], xs[1])
            with _jax.named_scope("update"):
                return (loss_sum + l_k, _jax.tree.map(_jnp.add, grad_sum, gw_k)), gx_k

        init = (_jnp.zeros((), _jnp.float32), _jax.tree.map(_jnp.zeros_like, weights))
        (loss, grad_w), grad_x = _jax.lax.scan(body, init, (per_example, given["loss_target"]))
    with _jax.named_scope("update"):
        delta_w, new_m, new_v = {}, {}, {}
        for n in TWIN_WEIGHTS:
            delta_w[n], new_m[n], new_v[n] = _adamw(weights[n], grad_w[n], given["m_" + n], given["v_" + n])
    return (loss, grad_x, *[grad_w[n] for n in TWIN_WEIGHTS], *[delta_w[n] for n in TWIN_WEIGHTS],
            *[new_m[n] for n in TWIN_WEIGHTS], *[new_v[n] for n in TWIN_WEIGHTS])
```

```python
import functools
import math

import jax
import jax.numpy as jnp
import numpy as np
from jax import lax
from jax.experimental import pallas as pl
from jax.experimental.pallas import tpu as pltpu

F32 = jnp.float32
BF16 = jnp.bfloat16
MXU_DTYPE = jnp.bfloat16

D_MODEL = 1024
DEPTH = 4
EPS = 1e-6
FOX_HEADS = 16
FOX_HEAD_DIM = 64
HEAD_PAIRS = FOX_HEADS // 2
S5_GROUPS = 64
S5_GROUP = 16
S5_STATE = 64
S5_BLOCKS = 8
S5_HALF = 256
POOL_WINDOWS = (2, 4, 8, 16)
POOL_WIDTH = 256
D_FF = 2816
N_CHIPS = 4
N_DEV = 8
LANES = 128
SUBLANES = 8
VMEM_LIMIT = 56 * 1024 * 1024

ADAM_LR = 0.001
ADAM_B1 = 0.9
ADAM_B2 = 0.999
ADAM_EPS = 1e-08
ADAM_WD = 0.01
ADAM_STEP = 10

MESH_AXES = ("x", "y", "c")


def _tile(n, want):
    t = (min(n, want) // LANES) * LANES
    while t >= LANES:
        if n % t == 0:
            return t
        t -= LANES
    return n


def _params(sem=None):
    return pltpu.CompilerParams(dimension_semantics=sem, vmem_limit_bytes=VMEM_LIMIT)


def _mm(a, b, *, name, ta=False, tb=False, out_dtype=F32, add=None, tm=1024, tn=1024, tk=512):
    m, k = (a.shape[1], a.shape[0]) if ta else a.shape
    n = b.shape[0] if tb else b.shape[1]
    assert (b.shape[1] if tb else b.shape[0]) == k, (a.shape, b.shape, ta, tb)
    tm, tn, tk = _tile(m, tm), _tile(n, tn), _tile(k, tk)
    nk = k // tk
    a_spec = pl.BlockSpec((tk, tm), lambda i, j, kk: (kk, i)) if ta else pl.BlockSpec((tm, tk), lambda i, j, kk: (i, kk))
    b_spec = pl.BlockSpec((tn, tk), lambda i, j, kk: (j, kk)) if tb else pl.BlockSpec((tk, tn), lambda i, j, kk: (kk, j))
    o_spec = pl.BlockSpec((tm, tn), lambda i, j, kk: (i, j))
    dims = (((0 if ta else 1,), (1 if tb else 0,)), ((), ()))
    has_add = add is not None

    def body(*refs):
        if has_add:
            a_ref, b_ref, add_ref, o_ref, acc_ref = refs
        else:
            a_ref, b_ref, o_ref, acc_ref = refs
        kk = pl.program_id(2)

        @pl.when(kk == 0)
        def _():
            acc_ref[...] = jnp.zeros_like(acc_ref)

        acc_ref[...] += lax.dot_general(a_ref[...].astype(MXU_DTYPE), b_ref[...].astype(MXU_DTYPE), dims,
                                        preferred_element_type=F32)

        @pl.when(kk == nk - 1)
        def _():
            r = acc_ref[...]
            if has_add:
                r = r + add_ref[...].astype(F32)
            o_ref[...] = r.astype(out_dtype)

    ins = [a, b] + ([add] if has_add else [])
    specs = [a_spec, b_spec] + ([o_spec] if has_add else [])
    return pl.pallas_call(
        body, name=name, grid=(m // tm, n // tn, nk), in_specs=specs, out_specs=o_spec,
        out_shape=jax.ShapeDtypeStruct((m, n), out_dtype), scratch_shapes=[pltpu.VMEM((tm, tn), F32)],
        compiler_params=_params(("parallel", "parallel", "arbitrary")))(*ins)


def _ew(fn, tens, vecs=(), *, outs=(), sums=(), name, tr=256):
    tens = [t if isinstance(t, tuple) else (t, t.shape[1], 0) for t in tens]
    rows = tens[0][0].shape[0]
    tr = min(tr, rows)
    n_t, n_v, n_o, n_s = len(tens), len(vecs), len(outs), len(sums)

    def body(*refs):
        i = pl.program_id(0)
        t_blocks = [r[...] for r in refs[:n_t]]
        v_blocks = [r[...] for r in refs[n_t:n_t + n_v]]
        o_refs = refs[n_t + n_v:n_t + n_v + n_o]
        s_refs = refs[n_t + n_v + n_o:]
        o_vals, s_vals = fn(*t_blocks, *v_blocks)
        for r, v in zip(o_refs, o_vals):
            r[...] = v.astype(r.dtype)
        if n_s:
            @pl.when(i == 0)
            def _():
                for r in s_refs:
                    r[...] = jnp.zeros_like(r)
            for r, v in zip(s_refs, s_vals):
                r[...] += jnp.sum(v.astype(F32), axis=0, keepdims=True)

    in_specs = [pl.BlockSpec((tr, w), functools.partial(lambda i, cb: (i, cb), cb=cb)) for _, w, cb in tens]
    in_specs += [pl.BlockSpec(v.shape, functools.partial(lambda i, nd: (0,) * nd, nd=v.ndim)) for v in vecs]
    out_specs = [pl.BlockSpec((tr, c), lambda i: (i, 0)) for c, _ in outs]
    out_specs += [pl.BlockSpec((1, c), lambda i: (0, 0)) for c in sums]
    out_shape = [jax.ShapeDtypeStruct((rows, c), dt) for c, dt in outs]
    out_shape += [jax.ShapeDtypeStruct((1, c), F32) for c in sums]
    res = pl.pallas_call(
        body, name=name, grid=(rows // tr,), in_specs=in_specs, out_specs=out_specs, out_shape=out_shape,
        compiler_params=_params(("arbitrary",)))(*[t[0] for t in tens], *vecs)
    return res


def _sigmoid(z):
    return 1.0 / (1.0 + jnp.exp(-z))


def _rms_fwd(x, g, name):
    def fn(xb, gb):
        r = lax.rsqrt(jnp.mean(xb * xb, axis=-1, keepdims=True) + EPS)
        return ((xb * r) * gb,), ()
    return _ew(fn, [x], [g.reshape(1, -1)], outs=[(x.shape[1], BF16)], name=name)[0]


def _rms_bwd(x, g, dh, dres, name):
    def fn(xb, dhb, drb, gb):
        r = lax.rsqrt(jnp.mean(xb * xb, axis=-1, keepdims=True) + EPS)
        xh = xb * r
        dhf = dhb.astype(F32)
        dy = dhf * gb
        dx = r * (dy - xh * jnp.mean(dy * xh, axis=-1, keepdims=True))
        return (drb + dx,), (dhf * xh,)
    dx, dg = _ew(fn, [x, dh, dres], [g.reshape(1, -1)], outs=[(x.shape[1], F32)], sums=[x.shape[1]], name=name)
    return dx, dg[0]


def _ffn_fwd(x1, gain, w_gu, w_down):
    h = _rms_fwd(x1, gain, "ffn_norm")
    gu = _mm(h, w_gu, name="ffn_gate_up", out_dtype=BF16)

    def act_fn(gb, ub):
        gf, uf = gb.astype(F32), ub.astype(F32)
        return (gf * _sigmoid(gf) * uf,), ()
    act = _ew(act_fn, [(gu, D_FF, 0), (gu, D_FF, 1)], outs=[(D_FF, BF16)], name="ffn_act")[0]
    x2 = _mm(act, w_down, name="ffn_down", add=x1, tk=1408)
    return x2, (x1, h, gu, act)


def _ffn_bwd(dx2, saved, gain, w_gu, w_down):
    x1, h, gu, act = saved
    dact = _mm(dx2, w_down, tb=True, name="ffn_dact", out_dtype=BF16)
    dw_down = _mm(act, dx2, ta=True, name="ffn_dw_down", out_dtype=BF16, tm=1408)

    def dgu_fn(db, gb, ub):
        df, gf, uf = db.astype(F32), gb.astype(F32), ub.astype(F32)
        sg = _sigmoid(gf)
        silu = gf * sg
        dg = df * uf * (sg * (1.0 + gf * (1.0 - sg)))
        du = df * silu
        return (jnp.concatenate([dg, du], axis=1),), ()
    dgu = _ew(dgu_fn, [dact, (gu, D_FF, 0), (gu, D_FF, 1)], outs=[(2 * D_FF, BF16)], name="ffn_dgu")[0]
    dw_gu = _mm(h, dgu, ta=True, name="ffn_dw_gu", out_dtype=BF16)
    dh = _mm(dgu, w_gu, tb=True, name="ffn_dh")
    dx1, dgain = _rms_bwd(x1, gain, dh, dx2, "ffn_norm_bwd")
    return dx1, dgain, dw_gu, dw_down


def _loss_head(x, gain, target):
    d = x.shape[1]

    def fn(xb, tb, gb):
        r = lax.rsqrt(jnp.mean(xb * xb, axis=-1, keepdims=True) + EPS)
        xh = xb * r
        y = xh * gb
        err = y - tb
        dyv = err * (1.0 / d)
        dyg = dyv * gb
        dx = r * (dyg - xh * jnp.mean(dyg * xh, axis=-1, keepdims=True))
        return (dx,), (0.5 * err * err * (1.0 / d), dyv * xh)
    dx, lsum, dg = _ew(fn, [x, target], [gain.reshape(1, -1)], outs=[(d, F32)], sums=[d, d], name="loss_head")
    return jnp.sum(lsum), dx, dg[0]


ATT_BLOCK = 256
CUM_BLOCK = 512
NEG_INF = -1e30


def _fox_gate_fwd(fl_row, b_col):
    nh, s = fl_row.shape
    tb = min(CUM_BLOCK, s)

    def body(fl_ref, b_ref, z_ref, c_ref):
        upper = (lax.broadcasted_iota(jnp.int32, (tb, tb), 0) <= lax.broadcasted_iota(jnp.int32, (tb, tb), 1)).astype(F32)
        carry = jnp.zeros((nh, 1), F32)
        for blk in range(s // tb):
            z = fl_ref[:, blk * tb:(blk + 1) * tb] + b_ref[...]
            logf = jnp.minimum(z, 0.0) - jnp.log(1.0 + jnp.exp(-jnp.abs(z)))
            cs = jnp.dot(logf, upper, precision=lax.Precision.HIGHEST, preferred_element_type=F32) + carry
            z_ref[:, blk * tb:(blk + 1) * tb] = z
            c_ref[:, blk * tb:(blk + 1) * tb] = cs
            carry = cs[:, tb - 1:tb]

    return pl.pallas_call(body, name="fox_gate_fwd", out_shape=[jax.ShapeDtypeStruct((nh, s), F32)] * 2,
                          compiler_params=_params())(fl_row, b_col)


def _fox_gate_bwd(dc_row, z_row):
    nh, s = dc_row.shape
    tb = min(CUM_BLOCK, s)

    def body(dc_ref, z_ref, dz_ref, db_ref):
        lower = (lax.broadcasted_iota(jnp.int32, (tb, tb), 0) >= lax.broadcasted_iota(jnp.int32, (tb, tb), 1)).astype(F32)
        carry = jnp.zeros((nh, 1), F32)
        db = jnp.zeros((nh, 1), F32)
        for blk in reversed(range(s // tb)):
            dc = dc_ref[:, blk * tb:(blk + 1) * tb]
            rs = jnp.dot(dc, lower, precision=lax.Precision.HIGHEST, preferred_element_type=F32) + carry
            dz = rs * _sigmoid(-z_ref[:, blk * tb:(blk + 1) * tb])
            dz_ref[:, blk * tb:(blk + 1) * tb] = dz
            db = db + jnp.sum(dz, axis=1, keepdims=True)
            carry = rs[:, 0:1]
        db_ref[...] = db

    return pl.pallas_call(body, name="fox_gate_bwd",
                          out_shape=[jax.ShapeDtypeStruct((nh, s), F32), jax.ShapeDtypeStruct((nh, 1), F32)],
                          compiler_params=_params())(dc_row, z_row)


def _head_masks(rows):
    lane = lax.broadcasted_iota(jnp.int32, (rows, LANES), 1)
    return lane < FOX_HEAD_DIM


def _attn_fwd(qkv, c_col, c_row):
    s = qkv.shape[0]
    t = min(ATT_BLOCK, s)
    nq = s // t
    scale = FOX_HEAD_DIM ** -0.5

    def body(q_ref, k_ref, v_ref, cc_ref, cr_ref, o_ref, lse_ref):
        i = pl.program_id(1)
        first = _head_masks(t)
        qs = q_ref[...] * scale
        causal = lax.broadcasted_iota(jnp.int32, (t, t), 0) >= lax.broadcasted_iota(jnp.int32, (t, t), 1)
        outs, lses = [], []
        for hh in range(2):
            qh = jnp.where(first if hh == 0 else ~first, qs, jnp.zeros_like(qs))
            ct = cc_ref[0, :, hh:hh + 1]

            def scores(j, qh=qh, ct=ct, hh=hh):
                start = pl.multiple_of(j * t, t)
                ks = k_ref[pl.ds(start, t), :]
                sc = lax.dot_general(qh, ks, (((1,), (1,)), ((), ())), preferred_element_type=F32)
                return sc + (ct - cr_ref[0, hh:hh + 1, pl.ds(start, t)]), start

            def update(sc, start, carry):
                m, l, acc = carry
                m_new = jnp.maximum(m, jnp.max(sc, axis=1, keepdims=True))
                p = jnp.exp(sc - m_new)
                alpha = jnp.exp(m - m_new)
                l = alpha * l + jnp.sum(p, axis=1, keepdims=True)
                vs = v_ref[pl.ds(start, t), :]
                p_hi = p.astype(MXU_DTYPE)
                p_lo = (p - p_hi.astype(F32)).astype(MXU_DTYPE)
                acc = alpha * acc + (jnp.dot(p_hi, vs, preferred_element_type=F32)
                                     + jnp.dot(p_lo, vs, preferred_element_type=F32))
                return m_new, l, acc

            def step(j, carry):
                sc, start = scores(j)
                return update(sc, start, carry)

            init = (jnp.full((t, 1), NEG_INF, F32), jnp.zeros((t, 1), F32), jnp.zeros((t, LANES), F32))
            carry = lax.fori_loop(0, i, step, init)
            sc, start = scores(i)
            m, l, acc = update(jnp.where(causal, sc, NEG_INF), start, carry)
            outs.append(acc / l)
            lses.append(m + jnp.log(l))
        o_ref[...] = jnp.where(first, outs[0], outs[1]).astype(o_ref.dtype)
        lse_ref[0] = jnp.concatenate(lses, axis=1)

    np_ = HEAD_PAIRS
    return pl.pallas_call(
        body, name="fox_attn_fwd", grid=(np_, nq),
        in_specs=[pl.BlockSpec((t, LANES), lambda p, i: (i, p)),
                  pl.BlockSpec((s, LANES), lambda p, i: (0, np_ + p)),
                  pl.BlockSpec((s, LANES), lambda p, i: (0, 2 * np_ + p)),
                  pl.BlockSpec((1, t, 2), lambda p, i: (p, i, 0)),
                  pl.BlockSpec((1, 2, s), lambda p, i: (p, 0, 0))],
        out_specs=[pl.BlockSpec((t, LANES), lambda p, i: (i, p)),
                   pl.BlockSpec((1, t, 2), lambda p, i: (p, i, 0))],
        out_shape=[jax.ShapeDtypeStruct((s, D_MODEL), F32), jax.ShapeDtypeStruct((np_, s, 2), F32)],
        compiler_params=_params(("parallel", "arbitrary")))(qkv, qkv, qkv, c_col, c_row)


def _attn_bwd(qkv, do, lse, delta, c_col, c_row):
    s = qkv.shape[0]
    t = min(ATT_BLOCK, s)
    nb = s // t
    scale = FOX_HEAD_DIM ** -0.5
    np_ = HEAD_PAIRS

    def body(q_ref, k_ref, v_ref, do_ref, lse_ref, dl_ref, cc_ref, cr_ref, dq_ref, dk_ref, dv_ref, dc_ref):
        j = pl.program_id(1)
        first = _head_masks(t)
        causal = lax.broadcasted_iota(jnp.int32, (t, t), 0) >= lax.broadcasted_iota(jnp.int32, (t, t), 1)
        kb = k_ref[...]
        vb = v_ref[...]

        @pl.when(j == 0)
        def _():
            dq_ref[...] = jnp.zeros_like(dq_ref)

        dks, dvs, dcs = [], [], []
        for hh in range(2):
            mask = first if hh == 0 else ~first
            cs = cr_ref[0, hh:hh + 1, :]

            def step(i, carry, masked, hh=hh, mask=mask, cs=cs):
                dk_acc, dv_acc, dc_acc = carry
                rows = pl.ds(pl.multiple_of(i * t, t), t)
                qs = q_ref[rows, :] * scale
                dob = do_ref[rows, :]
                qh = jnp.where(mask, qs, jnp.zeros_like(qs))
                doh = jnp.where(mask, dob, jnp.zeros_like(dob))
                sc = lax.dot_general(qh, kb, (((1,), (1,)), ((), ())), preferred_element_type=F32)
                sc = sc + (cc_ref[0, rows, hh:hh + 1] - cs)
                p = jnp.exp(sc - lse_ref[0, rows, hh:hh + 1])
                if masked:
                    p = jnp.where(causal, p, 0.0)
                dp = lax.dot_general(doh, vb, (((1,), (1,)), ((), ())), preferred_element_type=F32)
                ds = p * (dp - dl_ref[0, rows, hh:hh + 1])
                pb, dsb = p.astype(MXU_DTYPE), ds.astype(MXU_DTYPE)
                dv_acc = dv_acc + lax.dot_general(pb, dob, (((0,), (0,)), ((), ())), preferred_element_type=F32)
                dk_acc = dk_acc + lax.dot_general(dsb, qs, (((0,), (0,)), ((), ())), preferred_element_type=F32)
                dqh = jnp.dot(dsb, kb, preferred_element_type=F32) * scale
                dq_ref[rows, :] += jnp.where(mask, dqh, 0.0)
                dc_acc = dc_acc - jnp.sum(ds, axis=0, keepdims=True)
                return dk_acc, dv_acc, dc_acc

            init = (jnp.zeros((t, LANES), F32), jnp.zeros((t, LANES), F32), jnp.zeros((1, t), F32))
            carry = step(j, init, True)
            dk_acc, dv_acc, dc_acc = lax.fori_loop(j + 1, nb, functools.partial(step, masked=False), carry)
            dks.append(dk_acc)
            dvs.append(dv_acc)
            dcs.append(dc_acc)
        dk_ref[...] = jnp.where(first, dks[0], dks[1]).astype(dk_ref.dtype)
        dv_ref[...] = jnp.where(first, dvs[0], dvs[1]).astype(dv_ref.dtype)
        dc_ref[0] = jnp.concatenate(dcs, axis=0)

    return pl.pallas_call(
        body, name="fox_attn_bwd", grid=(np_, nb),
        in_specs=[pl.BlockSpec((s, LANES), lambda p, j: (0, p)),
                  pl.BlockSpec((t, LANES), lambda p, j: (j, np_ + p)),
                  pl.BlockSpec((t, LANES), lambda p, j: (j, 2 * np_ + p)),
                  pl.BlockSpec((s, LANES), lambda p, j: (0, p)),
                  pl.BlockSpec((1, s, 2), lambda p, j: (p, 0, 0)),
                  pl.BlockSpec((1, s, 2), lambda p, j: (p, 0, 0)),
                  pl.BlockSpec((1, s, 2), lambda p, j: (p, 0, 0)),
                  pl.BlockSpec((1, 2, t), lambda p, j: (p, 0, j))],
        out_specs=[pl.BlockSpec((s, LANES), lambda p, j: (0, p)),
                   pl.BlockSpec((t, LANES), lambda p, j: (j, p)),
                   pl.BlockSpec((t, LANES), lambda p, j: (j, p)),
                   pl.BlockSpec((1, 2, t), lambda p, j: (p, 0, j))],
        out_shape=[jax.ShapeDtypeStruct((s, D_MODEL), F32), jax.ShapeDtypeStruct((s, D_MODEL), BF16),
                   jax.ShapeDtypeStruct((s, D_MODEL), BF16), jax.ShapeDtypeStruct((np_, 2, s), F32)],
        compiler_params=_params(("parallel", "arbitrary")))(qkv, qkv, qkv, do, lse, delta, c_col, c_row)


def _head_sums(a, b, name):
    d = a.shape[1]
    sel = (jnp.arange(d)[:, None] // FOX_HEAD_DIM == jnp.arange(LANES)[None, :]).astype(F32)

    def fn(ab, bb, selb):
        prod = ab.astype(F32) * bb.astype(F32)
        return (jnp.dot(prod, selb, precision=lax.Precision.HIGHEST, preferred_element_type=F32),), ()
    return _ew(fn, [a, b], [sel], outs=[(LANES, F32)], name=name)[0]


def _pairs_col(a16):
    s = a16.shape[0]
    return a16.reshape(s, HEAD_PAIRS, 2).transpose(1, 0, 2)


def _fox_fwd(x, gain, w_qkv, w_f, b_f, w_out):
    s = x.shape[0]
    h = _rms_fwd(x, gain, "mix_norm")
    qkv = _mm(h, w_qkv, name="fox_qkv", out_dtype=BF16)
    fl = _mm(h, w_f, name="fox_f", tn=LANES)
    z_row, c_rowf = _fox_gate_fwd(fl[:, :FOX_HEADS].T, b_f.reshape(FOX_HEADS, 1))
    c_row = c_rowf.reshape(HEAD_PAIRS, 2, s)
    c_col = _pairs_col(c_rowf.T)
    o, lse = _attn_fwd(qkv, c_col, c_row)
    x1 = _mm(o, w_out, name="fox_out", add=x)
    return x1, (x, h, qkv, z_row, c_col, c_row, o, lse)


def _fox_bwd(dx1, saved, gain, w_qkv, w_f, w_out):
    x, h, qkv, z_row, c_col, c_row, o, lse = saved
    s = x.shape[0]
    do = _mm(dx1, w_out, tb=True, name="fox_do", out_dtype=BF16)
    dw_out = _mm(o, dx1, ta=True, name="fox_dw_out", out_dtype=BF16)
    delta = _pairs_col(_head_sums(do, o, "fox_delta")[:, :FOX_HEADS])
    dq, dk, dv, dc = _attn_bwd(qkv, do, lse, delta, c_col, c_row)
    dz_row, db = _fox_gate_bwd(dc.reshape(FOX_HEADS, s), z_row)
    dqkv = jnp.concatenate([dq.astype(BF16), dk, dv], axis=1)
    dfl = jnp.pad(dz_row.T, ((0, 0), (0, LANES - FOX_HEADS))).astype(BF16)
    dw_qkv = _mm(h, dqkv, ta=True, name="fox_dw_qkv", out_dtype=BF16)
    dw_f = _mm(h, dfl, ta=True, name="fox_dw_f", out_dtype=BF16, tn=LANES)
    dh = _mm(dqkv, w_qkv, tb=True, name="fox_dh_qkv")
    dh = _mm(dfl, w_f, tb=True, name="fox_dh_f", add=dh)
    dx, dgain = _rms_bwd(x, gain, dh, dx1, "mix_norm_bwd")
    dw_in = jnp.concatenate([dw_qkv, dw_f[:, :FOX_HEADS]], axis=1)
    return dx, dgain, dw_in, db.reshape(FOX_HEADS), dw_out


S5_ROWS = 512
SCAN_CHUNKS = SUBLANES


def _s5_operands(a_re, a_im, log_dt, b_re, b_im, c_re, c_im):
    dt = jnp.exp(log_dt)[:, None]
    mag, ang = jnp.exp(a_re * dt), a_im * dt
    lr, li = mag * jnp.cos(ang), mag * jnp.sin(ang)
    den = a_re * a_re + a_im * a_im
    cr = ((lr - 1.0) * a_re + li * a_im) / den
    ci = (li * a_re - (lr - 1.0) * a_im) / den
    bbr = cr[..., None] * b_re - ci[..., None] * b_im
    bbi = cr[..., None] * b_im + ci[..., None] * b_re
    nb = S5_BLOCKS
    lam = jnp.stack([lr.reshape(nb, 2, S5_HALF), li.reshape(nb, 2, S5_HALF)], axis=2)
    eye4, eye2 = jnp.eye(4, dtype=F32), jnp.eye(2, dtype=F32)
    bb = jnp.stack([bbr, bbi], axis=0).reshape(2, nb, 2, 4, S5_STATE, S5_GROUP)
    bmat = jnp.einsum("rbhgpc,kg,jh->bhjkcrgp", bb, eye4, eye2).reshape(nb, 2, 128, 2 * S5_HALF)
    cc = jnp.stack([c_re, -c_im], axis=0).reshape(2, nb, 2, 4, S5_GROUP, S5_STATE)
    cmat = jnp.einsum("rbhgcp,kg,jh->bhrgpjkc", cc, eye4, eye2).reshape(nb, 2, 2 * S5_HALF, 128)
    return lam, bmat, cmat


def _time_to_scan_order(a):
    s, d = a.shape
    return a.reshape(SCAN_CHUNKS, s // SCAN_CHUNKS, d).transpose(1, 0, 2).reshape(s, d)


def _scan_to_time_order(a):
    s, d = a.shape
    return a.reshape(s // SCAN_CHUNKS, SCAN_CHUNKS, d).transpose(1, 0, 2).reshape(s, d)


def _scan_chunks(xr_ref, xi_ref, lr, li, nst, reverse, after_step=None):
    lanes = lr.shape[1]
    lr8, li8 = jnp.broadcast_to(lr, (SUBLANES, lanes)), jnp.broadcast_to(li, (SUBLANES, lanes))
    zero8 = jnp.zeros((SUBLANES, lanes), F32)

    def rows_of(n):
        s = (nst - 1 - n) if reverse else n
        return s, pl.ds(pl.multiple_of(s * SUBLANES, SUBLANES), SUBLANES)

    def local(n, carry):
        pr, pi = carry
        _, rows = rows_of(n)
        nr = lr8 * pr - li8 * pi + xr_ref[rows, :]
        ni = lr8 * pi + li8 * pr + xi_ref[rows, :]
        xr_ref[rows, :] = nr
        xi_ref[rows, :] = ni
        return nr, ni

    er, ei = lax.fori_loop(0, nst, local, (zero8, zero8))
    pr, pi = lr, li
    for _ in range(int(math.log2(nst))):
        pr, pi = pr * pr - pi * pi, 2.0 * pr * pi
    tr = ti = jnp.zeros((1, lanes), F32)
    ent_r, ent_i = [None] * SCAN_CHUNKS, [None] * SCAN_CHUNKS
    for k in (reversed(range(SCAN_CHUNKS)) if reverse else range(SCAN_CHUNKS)):
        ent_r[k], ent_i[k] = tr, ti
        tr, ti = er[k:k + 1] + (pr * tr - pi * ti), ei[k:k + 1] + (pr * ti + pi * tr)
    in_r, in_i = jnp.concatenate(ent_r, axis=0), jnp.concatenate(ent_i, axis=0)

    def fix(n, carry):
        wr, wi = carry
        s, rows = rows_of(n)
        nr = xr_ref[rows, :] + (wr * in_r - wi * in_i)
        ni = xi_ref[rows, :] + (wr * in_i + wi * in_r)
        xr_ref[rows, :] = nr
        xi_ref[rows, :] = ni
        if after_step is not None:
            after_step(s, nr, ni)
        return wr * lr8 - wi * li8, wr * li8 + wi * lr8

    lax.fori_loop(0, nst, fix, (lr8, li8))
    return in_r, in_i


def _s5_fill_states(u_ref, bm, xr_ref, xi_ref, s):
    rc = min(S5_ROWS, s)

    def fill(r, _):
        rows = pl.ds(pl.multiple_of(r * rc, rc), rc)
        bu = jnp.dot(u_ref[rows, :].astype(MXU_DTYPE), bm, preferred_element_type=F32)
        xr_ref[rows, :] = bu[:, :S5_HALF]
        xi_ref[rows, :] = bu[:, S5_HALF:]
        return 0
    lax.fori_loop(0, s // rc, fill, 0)


def _s5_specs():
    return [pl.BlockSpec((1, 2, 2, S5_HALF), lambda b: (b, 0, 0, 0)),
            pl.BlockSpec((1, 2, 128, 2 * S5_HALF), lambda b: (b, 0, 0, 0)),
            pl.BlockSpec((1, 2, 2 * S5_HALF, 128), lambda b: (b, 0, 0, 0)),
            pl.BlockSpec((1, LANES), lambda b: (0, b))]


def _s5_scan_fwd(u, lam, bmat, cmat, dvec):
    s = u.shape[0]
    nst = s // SCAN_CHUNKS
    rc = min(S5_ROWS, s)

    def body(u_ref, lam_ref, b_ref, c_ref, d_ref, y_ref, xr_ref, xi_ref):
        y_ref[...] = u_ref[...] * d_ref[...]
        for hb in range(2):
            _s5_fill_states(u_ref, b_ref[0, hb], xr_ref, xi_ref, s)
            _scan_chunks(xr_ref, xi_ref, lam_ref[0, hb, 0:1, :], lam_ref[0, hb, 1:2, :], nst, False)
            cm = c_ref[0, hb]

            def emit(r, _, cm=cm):
                rows = pl.ds(pl.multiple_of(r * rc, rc), rc)
                y_ref[rows, :] += (jnp.dot(xr_ref[rows, :].astype(MXU_DTYPE), cm[:S5_HALF], preferred_element_type=F32)
                                   + jnp.dot(xi_ref[rows, :].astype(MXU_DTYPE), cm[S5_HALF:], preferred_element_type=F32))
                return 0
            lax.fori_loop(0, s // rc, emit, 0)

    blk = pl.BlockSpec((s, LANES), lambda b: (0, b))
    return pl.pallas_call(
        body, name="s5_scan_fwd", grid=(S5_BLOCKS,), in_specs=[blk] + _s5_specs(), out_specs=blk,
        out_shape=jax.ShapeDtypeStruct(u.shape, F32),
        scratch_shapes=[pltpu.VMEM((s, S5_HALF), F32)] * 2,
        compiler_params=_params(("parallel",)))(u, lam, bmat, cmat, dvec)


def _s5_scan_bwd(u, dy, lam, bmat, cmat, dvec):
    s = u.shape[0]
    nst = s // SCAN_CHUNKS
    rc = min(S5_ROWS, s)
    nt = (((1,), (1,)), ((), ()))
    tn = (((0,), (0,)), ((), ()))

    def body(u_ref, dy_ref, lam_ref, b_ref, c_ref, d_ref, du_ref, db_ref, dc_ref, dl_ref, dd_ref,
             xr_ref, xi_ref, gr_ref, gi_ref, ar_ref, ai_ref):
        du_ref[...] = dy_ref[...] * d_ref[...]
        dd_ref[...] = jnp.sum(dy_ref[...] * u_ref[...], axis=0, keepdims=True)
        db_ref[...] = jnp.zeros_like(db_ref)
        dc_ref[...] = jnp.zeros_like(dc_ref)
        for hb in range(2):
            bm, cm = b_ref[0, hb], c_ref[0, hb]
            lr, li = lam_ref[0, hb, 0:1, :], lam_ref[0, hb, 1:2, :]
            _s5_fill_states(u_ref, bm, xr_ref, xi_ref, s)
            xin_r, xin_i = _scan_chunks(xr_ref, xi_ref, lr, li, nst, False)

            def fill_g(r, _, cm=cm):
                rows = pl.ds(pl.multiple_of(r * rc, rc), rc)
                g = lax.dot_general(dy_ref[rows, :].astype(MXU_DTYPE), cm, nt, preferred_element_type=F32)
                gr_ref[rows, :] = g[:, :S5_HALF]
                gi_ref[rows, :] = g[:, S5_HALF:]
                return 0
            lax.fori_loop(0, s // rc, fill_g, 0)
            ar_ref[...] = jnp.zeros_like(ar_ref)
            ai_ref[...] = jnp.zeros_like(ai_ref)

            def lam_grad(st, g_r, g_i, xin_r=xin_r, xin_i=xin_i):
                prev = pl.ds(pl.multiple_of(jnp.maximum(st - 1, 0) * SUBLANES, SUBLANES), SUBLANES)
                x_r = jnp.where(st > 0, xr_ref[prev, :], xin_r)
                x_i = jnp.where(st > 0, xi_ref[prev, :], xin_i)
                ar_ref[...] += g_r * x_r + g_i * x_i
                ai_ref[...] += g_i * x_r - g_r * x_i

            _scan_chunks(gr_ref, gi_ref, lr, -li, nst, True, after_step=lam_grad)
            dl_ref[0, hb] = jnp.concatenate([jnp.sum(ar_ref[...], axis=0, keepdims=True),
                                             jnp.sum(ai_ref[...], axis=0, keepdims=True)], axis=0)

            def emit(r, _, bm=bm, hb=hb):
                rows = pl.ds(pl.multiple_of(r * rc, rc), rc)
                g = jnp.concatenate([gr_ref[rows, :], gi_ref[rows, :]], axis=1).astype(MXU_DTYPE)
                x = jnp.concatenate([xr_ref[rows, :], xi_ref[rows, :]], axis=1).astype(MXU_DTYPE)
                du_ref[rows, :] += lax.dot_general(g, bm, nt, preferred_element_type=F32)
                db_ref[0, hb] += lax.dot_general(u_ref[rows, :].astype(MXU_DTYPE), g, tn, preferred_element_type=F32)
                dc_ref[0, hb] += lax.dot_general(dy_ref[rows, :].astype(MXU_DTYPE), x, tn, preferred_element_type=F32)
                return 0
            lax.fori_loop(0, s // rc, emit, 0)

    blk = pl.BlockSpec((s, LANES), lambda b: (0, b))
    mat = pl.BlockSpec((1, 2, 128, 2 * S5_HALF), lambda b: (b, 0, 0, 0))
    return pl.pallas_call(
        body, name="s5_scan_bwd", grid=(S5_BLOCKS,), in_specs=[blk, blk] + _s5_specs(),
        out_specs=[blk, mat, mat, pl.BlockSpec((1, 2, 2, S5_HALF), lambda b: (b, 0, 0, 0)),
                   pl.BlockSpec((1, LANES), lambda b: (0, b))],
        out_shape=[jax.ShapeDtypeStruct(u.shape, F32),
                   jax.ShapeDtypeStruct((S5_BLOCKS, 2, 128, 2 * S5_HALF), F32),
                   jax.ShapeDtypeStruct((S5_BLOCKS, 2, 128, 2 * S5_HALF), F32),
                   jax.ShapeDtypeStruct((S5_BLOCKS, 2, 2, S5_HALF), F32),
                   jax.ShapeDtypeStruct((1, D_MODEL), F32)],
        scratch_shapes=[pltpu.VMEM((s, S5_HALF), F32)] * 4 + [pltpu.VMEM((SUBLANES, S5_HALF), F32)] * 2,
        compiler_params=_params(("parallel",)))(u, dy, lam, bmat, cmat, dvec)


_GELU_C = math.sqrt(2.0 / math.pi)


def _gelu_parts(y):
    inner = _GELU_C * (y + 0.044715 * y * y * y)
    th = jnp.tanh(inner)
    return 0.5 * y * (1.0 + th), th


def _s5_fwd(x, gain, w_in, ssm, dvec, w_glu):
    lam, bmat, cmat = ssm
    h = _rms_fwd(x, gain, "mix_norm")
    u = _mm(h, w_in, name="s5_in")
    y = _scan_to_time_order(_s5_scan_fwd(_time_to_scan_order(u), lam, bmat.astype(MXU_DTYPE), cmat.astype(MXU_DTYPE), dvec))
    g = _ew(lambda yb: ((_gelu_parts(yb)[0],), ()), [y], outs=[(D_MODEL, BF16)], name="s5_gelu")[0]
    vg = _mm(g, w_glu, name="s5_glu", out_dtype=BF16)

    def glu_fn(vb, gb, xb):
        return (xb + vb.astype(F32) * _sigmoid(gb.astype(F32)),), ()
    x1 = _ew(glu_fn, [(vg, D_MODEL, 0), (vg, D_MODEL, 1), x], outs=[(D_MODEL, F32)], name="s5_gate")[0]
    return x1, (x, h, u, y, g, vg)


def _s5_bwd(dx1, saved, gain, w_in, ssm, dvec, w_glu):
    x, h, u, y, g, vg = saved
    lam, bmat, cmat = ssm

    def dglu_fn(db, vb, gb):
        vf, sg = vb.astype(F32), _sigmoid(gb.astype(F32))
        return (jnp.concatenate([db * sg, db * vf * sg * (1.0 - sg)], axis=1),), ()
    dvg = _ew(dglu_fn, [dx1, (vg, D_MODEL, 0), (vg, D_MODEL, 1)], outs=[(2 * D_MODEL, BF16)], name="s5_dgate")[0]
    dw_glu = _mm(g, dvg, ta=True, name="s5_dw_glu", out_dtype=BF16)
    dg = _mm(dvg, w_glu, tb=True, name="s5_dg")

    def dgelu_fn(dgb, yb):
        _, th = _gelu_parts(yb)
        dinner = _GELU_C * (1.0 + 3.0 * 0.044715 * yb * yb)
        return (dgb * (0.5 * (1.0 + th) + 0.5 * yb * (1.0 - th * th) * dinner),), ()
    dy = _ew(dgelu_fn, [dg, y], outs=[(D_MODEL, F32)], name="s5_dgelu")[0]
    du_s, dbm, dct, dlam, ddvec = _s5_scan_bwd(_time_to_scan_order(u), _time_to_scan_order(dy), lam,
                                               bmat.astype(MXU_DTYPE), cmat.astype(MXU_DTYPE), dvec)
    du = _scan_to_time_order(du_s)
    dw_in = _mm(h, du, ta=True, name="s5_dw_in", out_dtype=BF16)
    dh = _mm(du, w_in, tb=True, name="s5_dh")
    dx, dgain = _rms_bwd(x, gain, dh, dx1, "mix_norm_bwd")
    return dx, dgain, dw_in, (dlam, dbm, jnp.swapaxes(dct, 2, 3)), ddvec, dw_glu


POOL_BLOCK = 256
N_POOL_GROUPS = len(POOL_WINDOWS)


def _pool_bands(gi, i, t):
    w = jnp.left_shift(2, gi)
    r = lax.broadcasted_iota(jnp.int32, (t, t), 0)
    c = lax.broadcasted_iota(jnp.int32, (t, t), 1)
    inside = ((c <= r) & (c > r - w)).astype(MXU_DTYPE)
    before = (c > r - w + t).astype(MXU_DTYPE)

    def inv_count(block):
        pos = block * t + lax.broadcasted_iota(jnp.int32, (t, 1), 0)
        return 1.0 / jnp.minimum(pos + 1, w).astype(F32)
    return inside, before, inv_count


def _pool_fwd(x, gain, w_grp, b_grp, scale):
    s = x.shape[0]
    t = min(POOL_BLOCK, s)
    h = _rms_fwd(x, gain, "mix_norm")

    def body(h_ref, hp_ref, w_ref, b_ref, sc_ref, x_ref, x1_ref, diff_ref):
        gi, i = pl.program_id(0), pl.program_id(1)
        inside, before, inv_count = _pool_bands(gi, i, t)
        hc = h_ref[...]
        tot = jnp.dot(inside, hc.astype(MXU_DTYPE), preferred_element_type=F32)
        prev = jnp.dot(before, hp_ref[...].astype(MXU_DTYPE), preferred_element_type=F32)
        tot = tot + jnp.where(i > 0, prev, 0.0)
        diff = (tot * inv_count(i) - hc.astype(F32)).astype(diff_ref.dtype)
        y = (jnp.dot(diff.astype(MXU_DTYPE), w_ref[0], preferred_element_type=F32) + b_ref[...]) * sc_ref[...]
        diff_ref[...] = diff
        x1_ref[...] = x_ref[...] + y

    blk = pl.BlockSpec((t, POOL_WIDTH), lambda gi, i: (i, gi))
    vec = pl.BlockSpec((1, POOL_WIDTH), lambda gi, i: (0, gi))
    x1, diff = pl.pallas_call(
        body, name="pool_fwd", grid=(N_POOL_GROUPS, s // t),
        in_specs=[blk, pl.BlockSpec((t, POOL_WIDTH), lambda gi, i: (jnp.maximum(i - 1, 0), gi)),
                  pl.BlockSpec((1, POOL_WIDTH, POOL_WIDTH), lambda gi, i: (gi, 0, 0)), vec, vec, blk],
        out_specs=[blk, blk],
        out_shape=[jax.ShapeDtypeStruct(x.shape, F32), jax.ShapeDtypeStruct(x.shape, BF16)],
        compiler_params=_params(("parallel", "arbitrary")))(h, h, w_grp, b_grp, scale, x)
    return x1, (x, diff)


def _pool_bwd(dx1, saved, gain, w_grp, b_grp, scale):
    x, diff = saved
    s = x.shape[0]
    t = min(POOL_BLOCK, s)
    nb = s // t

    def body1(dx_ref, diff_ref, w_ref, b_ref, sc_ref, dd_ref, dw_ref, db_ref, dsc_ref):
        i = pl.program_id(1)

        @pl.when(i == 0)
        def _():
            dw_ref[...] = jnp.zeros_like(dw_ref)
            db_ref[...] = jnp.zeros_like(db_ref)
            dsc_ref[...] = jnp.zeros_like(dsc_ref)

        dfb = diff_ref[...].astype(MXU_DTYPE)
        ypre = jnp.dot(dfb, w_ref[0], preferred_element_type=F32) + b_ref[...]
        dxb = dx_ref[...]
        dy = dxb * sc_ref[...]
        dsc_ref[...] += jnp.sum(dxb * ypre, axis=0, keepdims=True)
        db_ref[...] += jnp.sum(dy, axis=0, keepdims=True)
        dyb = dy.astype(MXU_DTYPE)
        dw_ref[0] += lax.dot_general(dfb, dyb, (((0,), (0,)), ((), ())), preferred_element_type=F32)
        dd_ref[...] = lax.dot_general(dyb, w_ref[0], (((1,), (1,)), ((), ())), preferred_element_type=F32)

    blk = pl.BlockSpec((t, POOL_WIDTH), lambda gi, i: (i, gi))
    vec = pl.BlockSpec((1, POOL_WIDTH), lambda gi, i: (0, gi))
    mat = pl.BlockSpec((1, POOL_WIDTH, POOL_WIDTH), lambda gi, i: (gi, 0, 0))
    ddiff, dw, db, dsc = pl.pallas_call(
        body1, name="pool_bwd_map", grid=(N_POOL_GROUPS, nb), in_specs=[blk, blk, mat, vec, vec],
        out_specs=[blk, mat, vec, vec],
        out_shape=[jax.ShapeDtypeStruct(x.shape, F32), jax.ShapeDtypeStruct(w_grp.shape, F32),
                   jax.ShapeDtypeStruct((1, D_MODEL), F32), jax.ShapeDtypeStruct((1, D_MODEL), F32)],
        compiler_params=_params(("parallel", "arbitrary")))(dx1, diff, w_grp, b_grp, scale)

    def body2(dc_ref, dn_ref, dh_ref):
        gi, i = pl.program_id(0), pl.program_id(1)
        inside, before, inv_count = _pool_bands(gi, i, t)
        tn = (((0,), (0,)), ((), ()))
        dc = dc_ref[...]
        tot = lax.dot_general(inside, (dc * inv_count(i)).astype(MXU_DTYPE), tn, preferred_element_type=F32)
        nxt = lax.dot_general(before, (dn_ref[...] * inv_count(i + 1)).astype(MXU_DTYPE), tn, preferred_element_type=F32)
        dh_ref[...] = tot + jnp.where(i < nb - 1, nxt, 0.0) - dc

    dh = pl.pallas_call(
        body2, name="pool_bwd_window", grid=(N_POOL_GROUPS, nb),
        in_specs=[blk, pl.BlockSpec((t, POOL_WIDTH), lambda gi, i: (jnp.minimum(i + 1, nb - 1), gi))],
        out_specs=blk, out_shape=jax.ShapeDtypeStruct(x.shape, F32),
        compiler_params=_params(("parallel", "parallel")))(ddiff, ddiff)
    dx, dgain = _rms_bwd(x, gain, dh, dx1, "mix_norm_bwd")
    return dx, dgain, dw, db, dsc


MESH_ID = pl.DeviceIdType.MESH
ANY = pl.BlockSpec(memory_space=pl.ANY)


def _place():
    x, y, c = lax.axis_index("x"), lax.axis_index("y"), lax.axis_index("c")
    other_chips = [(1 - x, y), (x, 1 - y), (1 - x, 1 - y)]
    return x, y, c, other_chips


def _chip_index(chip):
    return 2 * chip[0] + chip[1]


def _remote(src, dst, send_sems, recv_sems, n, to):
    return pltpu.make_async_remote_copy(src_ref=src, dst_ref=dst, send_sem=send_sems.at[n], recv_sem=recv_sems.at[n],
                                        device_id=to, device_id_type=MESH_ID)


def _gather_weights(wp):
    def body(w_ref, out_ref, send_sems, recv_sems, local_sem):
        x, y, c, chips = _place()
        me, sibling = (x, y, c), (x, y, 1 - c)
        mine = pltpu.make_async_copy(w_ref, out_ref.at[_chip_index((x, y))], local_sem)
        mine.start()
        first = [_remote(w_ref.at[c], out_ref.at[_chip_index((x, y)), c], send_sems, recv_sems, j, (*chip, c))
                 for j, chip in enumerate(chips)]
        for cp in first:
            cp.start()
        landed = [out_ref.at[_chip_index(chip), c] for chip in chips]
        passed = [_remote(landed[j], landed[j], send_sems, recv_sems, 3 + j, sibling) for j in range(3)]
        for j in range(3):
            _remote(landed[j], landed[j], send_sems, recv_sems, j, me).wait_recv()
            passed[j].start()
        for j, chip in enumerate(chips):
            theirs = out_ref.at[_chip_index(chip), 1 - c]
            _remote(theirs, theirs, send_sems, recv_sems, 3 + j, me).wait_recv()
        for cp in first + passed:
            cp.wait_send()
        mine.wait()

    return pl.pallas_call(
        body, name="gather_weights", in_specs=[ANY], out_specs=ANY,
        out_shape=jax.ShapeDtypeStruct((N_CHIPS,) + wp.shape, wp.dtype),
        scratch_shapes=[pltpu.SemaphoreType.DMA((6,)), pltpu.SemaphoreType.DMA((6,)), pltpu.SemaphoreType.DMA],
    )(wp)


def _swap_halves(gp):
    def body(g_ref, out_ref, send_sems, recv_sems):
        x, y, c, _ = _place()
        copies = [_remote(g_ref.at[s, 1 - c], out_ref.at[s], send_sems, recv_sems, s, (x, y, 1 - c))
                  for s in range(N_CHIPS)]
        for cp in copies:
            cp.start()
        for cp in copies:
            cp.wait_recv()
        for cp in copies:
            cp.wait_send()

    return pl.pallas_call(
        body, name="swap_halves", in_specs=[ANY], out_specs=ANY,
        out_shape=jax.ShapeDtypeStruct((N_CHIPS,) + gp.shape[2:], gp.dtype),
        scratch_shapes=[pltpu.SemaphoreType.DMA((N_CHIPS,)), pltpu.SemaphoreType.DMA((N_CHIPS,))],
    )(gp)


def _scatter_partials(p1):
    def body(p_ref, out_ref, send_sems, recv_sems, local_sem):
        x, y, c, chips = _place()
        k = _chip_index((x, y))
        mine = pltpu.make_async_copy(p_ref.at[k], out_ref.at[k], local_sem)
        mine.start()
        sends = [_remote(p_ref.at[_chip_index(chip)], out_ref.at[k], send_sems, recv_sems, j, (*chip, c))
                 for j, chip in enumerate(chips)]
        for cp in sends:
            cp.start()
        for j, chip in enumerate(chips):
            slot = out_ref.at[_chip_index(chip)]
            _remote(slot, slot, send_sems, recv_sems, j, (x, y, c)).wait_recv()
        for cp in sends:
            cp.wait_send()
        mine.wait()

    return pl.pallas_call(
        body, name="scatter_partials", in_specs=[ANY], out_specs=ANY,
        out_shape=jax.ShapeDtypeStruct(p1.shape, p1.dtype),
        scratch_shapes=[pltpu.SemaphoreType.DMA((3,)), pltpu.SemaphoreType.DMA((3,)), pltpu.SemaphoreType.DMA],
    )(p1)


def _share_half(f):
    def body(f_ref, out_ref, send_sems, recv_sems, local_sem):
        x, y, c, _ = _place()
        mine = pltpu.make_async_copy(f_ref, out_ref.at[c], local_sem)
        mine.start()
        send = _remote(f_ref, out_ref.at[c], send_sems, recv_sems, 0, (x, y, 1 - c))
        send.start()
        _remote(out_ref.at[1 - c], out_ref.at[1 - c], send_sems, recv_sems, 0, (x, y, c)).wait_recv()
        send.wait_send()
        mine.wait()

    return pl.pallas_call(
        body, name="share_half", in_specs=[ANY], out_specs=ANY,
        out_shape=jax.ShapeDtypeStruct((2,) + f.shape, f.dtype),
        scratch_shapes=[pltpu.SemaphoreType.DMA((1,)), pltpu.SemaphoreType.DMA((1,)), pltpu.SemaphoreType.DMA],
    )(f)


def _gather_small(v):
    def body(v_ref, out_ref, send_sems, recv_sems, local_sem):
        x, y, c, chips = _place()
        me, sibling = (x, y, c), (x, y, 1 - c)

        def slot(px, py, pc):
            return out_ref.at[4 * px + 2 * py + pc]

        mine = pltpu.make_async_copy(v_ref, slot(*me), local_sem)
        mine.start()
        first = [_remote(v_ref, slot(*me), send_sems, recv_sems, 0, sibling)]
        first += [_remote(v_ref, slot(*me), send_sems, recv_sems, 1 + j, (*chip, c)) for j, chip in enumerate(chips)]
        for cp in first:
            cp.start()
        passed = [_remote(slot(*chip, c), slot(*chip, c), send_sems, recv_sems, 4 + j, sibling)
                  for j, chip in enumerate(chips)]
        for j, chip in enumerate(chips):
            _remote(slot(*chip, c), slot(*chip, c), send_sems, recv_sems, 1 + j, me).wait_recv()
            passed[j].start()
        _remote(slot(*sibling), slot(*sibling), send_sems, recv_sems, 0, me).wait_recv()
        for j, chip in enumerate(chips):
            _remote(slot(*chip, 1 - c), slot(*chip, 1 - c), send_sems, recv_sems, 4 + j, me).wait_recv()
        for cp in first + passed:
            cp.wait_send()
        mine.wait()

    return pl.pallas_call(
        body, name="gather_small", in_specs=[ANY], out_specs=ANY,
        out_shape=jax.ShapeDtypeStruct((N_DEV,) + v.shape, v.dtype),
        scratch_shapes=[pltpu.SemaphoreType.DMA((7,)), pltpu.SemaphoreType.DMA((7,)), pltpu.SemaphoreType.DMA],
    )(v)


PACK_ROWS = 3488
SMALL_ROWS = 256


def _pair_sum(gp, r1, core):
    rows = gp.shape[2]
    tr = PACK_ROWS if rows % PACK_ROWS == 0 else rows

    def body(c_ref, g_ref, r_ref, o_ref):
        o_ref[0] = (g_ref[0, 0].astype(F32) + r_ref[0].astype(F32)).astype(o_ref.dtype)

    return pl.pallas_call(
        body, name="pair_sum",
        grid_spec=pltpu.PrefetchScalarGridSpec(
            num_scalar_prefetch=1, grid=(N_CHIPS, rows // tr),
            in_specs=[pl.BlockSpec((1, 1, tr, LANES), lambda s, i, c_ref: (s, c_ref[0], i, 0)),
                      pl.BlockSpec((1, tr, LANES), lambda s, i, c_ref: (s, i, 0))],
            out_specs=pl.BlockSpec((1, tr, LANES), lambda s, i, c_ref: (s, i, 0))),
        out_shape=jax.ShapeDtypeStruct(r1.shape, BF16),
        compiler_params=_params(("parallel", "parallel")))(core, gp, r1)


def _sum_blocks(a, name):
    n, rows = a.shape[0], a.shape[1]
    tr = PACK_ROWS if rows % PACK_ROWS == 0 else rows

    def body(a_ref, o_ref):
        acc = a_ref[0].astype(F32)
        for s in range(1, n):
            acc = acc + a_ref[s].astype(F32)
        o_ref[...] = acc

    return pl.pallas_call(
        body, name=name, grid=(rows // tr,),
        in_specs=[pl.BlockSpec((n, tr, LANES), lambda i: (0, i, 0))],
        out_specs=pl.BlockSpec((tr, LANES), lambda i: (i, 0)),
        out_shape=jax.ShapeDtypeStruct((rows, LANES), F32),
        compiler_params=_params(("parallel",)))(a)


def _adamw(w, g, m, v, name):
    def fn(wb, gb, mb, vb):
        m2 = ADAM_B1 * mb + (1.0 - ADAM_B1) * gb
        v2 = ADAM_B2 * vb + (1.0 - ADAM_B2) * (gb * gb)
        m_hat = m2 / (1.0 - ADAM_B1 ** ADAM_STEP)
        v_hat = v2 / (1.0 - ADAM_B2 ** ADAM_STEP)
        delta = -ADAM_LR * (m_hat / (jnp.sqrt(v_hat) + ADAM_EPS) + ADAM_WD * wb)
        return (delta, m2, v2), ()
    c = w.shape[1]
    return _ew(fn, [w, g, m, v], outs=[(c, F32)] * 3, name=name)


WEIGHTS = ["mix_norm_g", "ffn_norm_g", "final_norm_g", "fox_w_in", "fox_b_f", "fox_w_out", "s5_w_in", "s5_a_re",
           "s5_a_im", "s5_log_dt", "s5_b_re", "s5_b_im", "s5_c_re", "s5_c_im", "s5_d", "s5_w_glu", "pool_w",
           "pool_b", "pool_scale", "ffn_w_gate_up", "ffn_w_down"]
BIG = {"fox_w_in": 2, "fox_w_out": 1, "s5_w_in": 1, "s5_w_glu": 2, "pool_w": 2, "ffn_w_gate_up": 2, "ffn_w_down": 1}
SLICED = ("pool_b", "pool_scale")
SMALL = [n for n in WEIGHTS if n not in BIG]


def _to_natural(cm, axis):
    moved = jnp.moveaxis(cm, 0, axis)
    shape = moved.shape[:axis] + (moved.shape[axis] * moved.shape[axis + 1],) + moved.shape[axis + 2:]
    return moved.reshape(shape)


def _to_chip_major(nat, axis):
    shape = nat.shape[:axis] + (N_CHIPS, nat.shape[axis] // N_CHIPS) + nat.shape[axis + 1:]
    return jnp.moveaxis(nat.reshape(shape), axis, 0)


def _pack_rows(parts, lead):
    flat = jnp.concatenate([p.reshape(p.shape[:lead] + (-1,)) for p in parts], axis=lead)
    return flat.reshape(flat.shape[:lead] + (flat.shape[lead] // LANES, LANES))


def _unpack_rows(buf, shapes, lead):
    flat = buf.reshape(buf.shape[:lead] + (-1,))
    out, off = [], 0
    for shp in shapes:
        n = int(np.prod(shp))
        out.append(lax.slice_in_dim(flat, off, off + n, axis=lead).reshape(buf.shape[:lead] + tuple(shp)))
        off += n
    return out


def _pack_small(parts):
    flat = jnp.concatenate([p.reshape(-1).astype(F32) for p in parts])
    pad = (-flat.shape[0]) % (SMALL_ROWS * LANES)
    return jnp.pad(flat, (0, pad)).reshape(-1, LANES)


def _unpack_small(buf, shapes):
    flat = buf.reshape(-1)
    out, off = [], 0
    for shp in shapes:
        n = int(np.prod(shp))
        out.append(flat[off:off + n].reshape(shp))
        off += n
    return out


def _half_shape(shape):
    return (int(np.prod(shape)) // 2,)


def _local_step(x, target, w):
    grads = {}
    mixers = ("fox", "s5", "pool")
    saved = []
    ssm, ssm_pull = jax.vjp(_s5_operands, w["s5_a_re"][0], w["s5_a_im"][0], w["s5_log_dt"][0], w["s5_b_re"][0],
                            w["s5_b_im"][0], w["s5_c_re"][0], w["s5_c_im"][0])
    fox_w = []
    for j in range(w["fox_w_in"].shape[0]):
        w_in = w["fox_w_in"][j]
        w_f = jnp.pad(w_in[:, 3 * D_MODEL:], ((0, 0), (0, LANES - FOX_HEADS)))
        fox_w.append((w_in[:, :3 * D_MODEL], w_f, w["fox_w_out"][j]))
    for i in range(DEPTH):
        kind, j = mixers[i % 3], i // 3
        gain = w["mix_norm_g"][i]
        if kind == "fox":
            x1, sv = _fox_fwd(x, gain, fox_w[j][0], fox_w[j][1], w["fox_b_f"][j], fox_w[j][2])
        elif kind == "s5":
            x1, sv = _s5_fwd(x, gain, w["s5_w_in"][j], ssm, w["s5_d"], w["s5_w_glu"][j])
        else:
            x1, sv = _pool_fwd(x, gain, w["pool_w"][j], w["pool_b"], w["pool_scale"])
        x, sf = _ffn_fwd(x1, w["ffn_norm_g"][i], w["ffn_w_gate_up"][i], w["ffn_w_down"][i])
        saved.append((sv, sf))
    loss, dx, grads["final_norm_g"] = _loss_head(x, w["final_norm_g"], target)
    per_layer = {n: [None] * DEPTH for n in ("mix_norm_g", "ffn_norm_g", "ffn_w_gate_up", "ffn_w_down")}
    fox_g = {n: [None] * len(fox_w) for n in ("fox_w_in", "fox_b_f", "fox_w_out")}
    for i in reversed(range(DEPTH)):
        kind, j = mixers[i % 3], i // 3
        sv, sf = saved[i]
        dx, per_layer["ffn_norm_g"][i], per_layer["ffn_w_gate_up"][i], per_layer["ffn_w_down"][i] = _ffn_bwd(
            dx, sf, w["ffn_norm_g"][i], w["ffn_w_gate_up"][i], w["ffn_w_down"][i])
        gain = w["mix_norm_g"][i]
        if kind == "fox":
            dx, per_layer["mix_norm_g"][i], fox_g["fox_w_in"][j], fox_g["fox_b_f"][j], fox_g["fox_w_out"][j] = _fox_bwd(
                dx, sv, gain, fox_w[j][0], fox_w[j][1], fox_w[j][2])
        elif kind == "s5":
            dx, per_layer["mix_norm_g"][i], dw_in, dssm, dd, dw_glu = _s5_bwd(
                dx, sv, gain, w["s5_w_in"][j], ssm, w["s5_d"], w["s5_w_glu"][j])
            grads["s5_w_in"], grads["s5_w_glu"], grads["s5_d"] = dw_in[None], dw_glu[None], dd
            for n, g in zip(("s5_a_re", "s5_a_im", "s5_log_dt", "s5_b_re", "s5_b_im", "s5_c_re", "s5_c_im"), ssm_pull(dssm)):
                grads[n] = g[None]
        else:
            dx, per_layer["mix_norm_g"][i], dw, db, dsc = _pool_bwd(dx, sv, gain, w["pool_w"][j], w["pool_b"], w["pool_scale"])
            grads["pool_w"], grads["pool_b"], grads["pool_scale"] = dw[None].astype(BF16), db, dsc
    for n, parts in {**per_layer, **fox_g}.items():
        grads[n] = jnp.stack(parts)
    return loss, dx, grads


def kernel(x, mix_norm_g, ffn_norm_g, final_norm_g, fox_w_in, fox_b_f, fox_w_out, s5_w_in, s5_a_re, s5_a_im, s5_log_dt, s5_b_re, s5_b_im, s5_c_re, s5_c_im, s5_d, s5_w_glu, pool_w, pool_b, pool_scale, ffn_w_gate_up, ffn_w_down, loss_target, m_mix_norm_g, m_ffn_norm_g, m_final_norm_g, m_fox_w_in, m_fox_b_f, m_fox_w_out, m_s5_w_in, m_s5_a_re, m_s5_a_im, m_s5_log_dt, m_s5_b_re, m_s5_b_im, m_s5_c_re, m_s5_c_im, m_s5_d, m_s5_w_glu, m_pool_w, m_pool_b, m_pool_scale, m_ffn_w_gate_up, m_ffn_w_down, v_mix_norm_g, v_ffn_norm_g, v_final_norm_g, v_fox_w_in, v_fox_b_f, v_fox_w_out, v_s5_w_in, v_s5_a_re, v_s5_a_im, v_s5_log_dt, v_s5_b_re, v_s5_b_im, v_s5_c_re, v_s5_c_im, v_s5_d, v_s5_w_glu, v_pool_w, v_pool_b, v_pool_scale, v_ffn_w_gate_up, v_ffn_w_down):
    given = dict(locals())
    shard = {n: given[n] for n in WEIGHTS}
    chip = 2 * lax.axis_index("x") + lax.axis_index("y")
    core = lax.axis_index("c")

    wp = _pack_rows([shard[n].astype(MXU_DTYPE).reshape((2, -1)) for n in BIG], 1)
    gathered = _gather_weights(wp)
    half_shapes = [_half_shape(shard[n].shape) for n in BIG]
    whole = {}
    for n, halves in zip(BIG, _unpack_rows(gathered, half_shapes, 2)):
        whole[n] = _to_natural(halves.reshape((N_CHIPS,) + shard[n].shape), BIG[n])
    for n in SMALL:
        whole[n] = shard[n]
    sliced_shapes = [shard[n].shape for n in SLICED]
    by_chip = _gather_small(_pack_small([shard[n] for n in SLICED]))[0::2]
    slices = [_unpack_small(by_chip[k], sliced_shapes) for k in range(N_CHIPS)]
    for idx, n in enumerate(SLICED):
        whole[n] = jnp.concatenate([slices[k][idx] for k in range(N_CHIPS)], axis=-1)

    loss_part, dx, grads = _local_step(x[0], loss_target[0], whole)
    loss = lax.psum(loss_part, MESH_AXES)

    gp = _pack_rows([_to_chip_major(grads[n].astype(BF16), BIG[n]).reshape((N_CHIPS, 2, -1)) for n in BIG], 2)
    partial = _pair_sum(gp, _swap_halves(gp), core.reshape(1).astype(jnp.int32))
    summed = _share_half(_sum_blocks(_scatter_partials(partial), "chip_sum"))
    grad = {n: g.reshape(shard[n].shape) for n, g in zip(BIG, _unpack_rows(summed, half_shapes, 1))}

    small_sum = _sum_blocks(_gather_small(_pack_small([grads[n] for n in SMALL])), "small_sum")
    for n, g in zip(SMALL, _unpack_small(small_sum, [whole[n].shape for n in SMALL])):
        grad[n] = g
    for n in SLICED:
        width = shard[n].shape[-1]
        grad[n] = lax.dynamic_slice_in_dim(grad[n], chip * width, width, axis=-1)

    delta, new_m, new_v = {}, {}, {}
    for n in BIG:
        view = (-1, shard[n].shape[-1])
        res = _adamw(shard[n].reshape(view), grad[n].reshape(view), given["m_" + n].reshape(view),
                     given["v_" + n].reshape(view), "adamw_" + n)
        delta[n], new_m[n], new_v[n] = (r.reshape(shard[n].shape) for r in res)
    small_shapes = [shard[n].shape for n in SMALL]
    res = _adamw(_pack_small([shard[n] for n in SMALL]), _pack_small([grad[n] for n in SMALL]),
                 _pack_small([given["m_" + n] for n in SMALL]), _pack_small([given["v_" + n] for n in SMALL]), "adamw_small")
    for out, buf in zip((delta, new_m, new_v), res):
        for n, a in zip(SMALL, _unpack_small(buf, small_shapes)):
            out[n] = a
    return (loss, dx[None], *[grad[n] for n in WEIGHTS], *[delta[n] for n in WEIGHTS],
            *[new_m[n] for n in WEIGHTS], *[new_v[n] for n in WEIGHTS])
```

```python
import functools
import math

import jax
import jax.numpy as jnp
import numpy as np
from jax import lax
from jax.experimental import pallas as pl
from jax.experimental.pallas import tpu as pltpu

F32 = jnp.float32
BF16 = jnp.bfloat16
MXU_DTYPE = jnp.bfloat16

D_MODEL = 1024
DEPTH = 4
EPS = 1e-6
FOX_HEADS = 16
FOX_HEAD_DIM = 64
HEAD_PAIRS = FOX_HEADS // 2
S5_GROUPS = 64
S5_GROUP = 16
S5_STATE = 64
S5_BLOCKS = 8
S5_HALF = 256
POOL_WINDOWS = (2, 4, 8, 16)
POOL_WIDTH = 256
D_FF = 2816
N_CHIPS = 4
N_DEV = 8
LANES = 128
SUBLANES = 8
VMEM_LIMIT = 56 * 1024 * 1024

ADAM_LR = 0.001
ADAM_B1 = 0.9
ADAM_B2 = 0.999
ADAM_EPS = 1e-08
ADAM_WD = 0.01
ADAM_STEP = 10

MESH_AXES = ("x", "y", "c")


def _tile(n, want):
    t = (min(n, want) // LANES) * LANES
    while t >= LANES:
        if n % t == 0:
            return t
        t -= LANES
    return n


def _params(sem=None):
    return pltpu.CompilerParams(dimension_semantics=sem, vmem_limit_bytes=VMEM_LIMIT)


def _mm(a, b, *, name, ta=False, tb=False, out_dtype=F32, add=None, tm=1024, tn=1024, tk=1024):
    m, k = (a.shape[1], a.shape[0]) if ta else a.shape
    n = b.shape[0] if tb else b.shape[1]
    assert (b.shape[1] if tb else b.shape[0]) == k, (a.shape, b.shape, ta, tb)
    tm, tn, tk = _tile(m, tm), _tile(n, tn), _tile(k, tk)
    nk = k // tk
    a_spec = pl.BlockSpec((tk, tm), lambda i, j, kk: (kk, i)) if ta else pl.BlockSpec((tm, tk), lambda i, j, kk: (i, kk))
    b_spec = pl.BlockSpec((tn, tk), lambda i, j, kk: (j, kk)) if tb else pl.BlockSpec((tk, tn), lambda i, j, kk: (kk, j))
    o_spec = pl.BlockSpec((tm, tn), lambda i, j, kk: (i, j))
    dims = (((0 if ta else 1,), (1 if tb else 0,)), ((), ()))
    has_add = add is not None

    def body(*refs):
        if has_add:
            a_ref, b_ref, add_ref, o_ref, acc_ref = refs
        else:
            a_ref, b_ref, o_ref, acc_ref = refs
        kk = pl.program_id(2)

        @pl.when(kk == 0)
        def _():
            acc_ref[...] = jnp.zeros_like(acc_ref)

        acc_ref[...] += lax.dot_general(a_ref[...].astype(MXU_DTYPE), b_ref[...].astype(MXU_DTYPE), dims,
                                        preferred_element_type=F32)

        @pl.when(kk == nk - 1)
        def _():
            r = acc_ref[...]
            if has_add:
                r = r + add_ref[...].astype(F32)
            o_ref[...] = r.astype(out_dtype)

    ins = [a, b] + ([add] if has_add else [])
    specs = [a_spec, b_spec] + ([o_spec] if has_add else [])
    return pl.pallas_call(
        body, name=name, grid=(m // tm, n // tn, nk), in_specs=specs, out_specs=o_spec,
        out_shape=jax.ShapeDtypeStruct((m, n), out_dtype), scratch_shapes=[pltpu.VMEM((tm, tn), F32)],
        compiler_params=_params(("parallel", "parallel", "arbitrary")))(*ins)


def _ew(fn, tens, vecs=(), *, outs=(), sums=(), name, tr=256):
    tens = [t if isinstance(t, tuple) else (t, t.shape[1], 0) for t in tens]
    rows = tens[0][0].shape[0]
    tr = min(tr, rows)
    n_t, n_v, n_o, n_s = len(tens), len(vecs), len(outs), len(sums)

    def body(*refs):
        i = pl.program_id(0)
        t_blocks = [r[...] for r in refs[:n_t]]
        v_blocks = [r[...] for r in refs[n_t:n_t + n_v]]
        o_refs = refs[n_t + n_v:n_t + n_v + n_o]
        s_refs = refs[n_t + n_v + n_o:]
        o_vals, s_vals = fn(*t_blocks, *v_blocks)
        for r, v in zip(o_refs, o_vals):
            r[...] = v.astype(r.dtype)
        if n_s:
            @pl.when(i == 0)
            def _():
                for r in s_refs:
                    r[...] = jnp.zeros_like(r)
            for r, v in zip(s_refs, s_vals):
                r[...] += jnp.sum(v.astype(F32), axis=0, keepdims=True)

    in_specs = [pl.BlockSpec((tr, w), functools.partial(lambda i, cb: (i, cb), cb=cb)) for _, w, cb in tens]
    in_specs += [pl.BlockSpec(v.shape, functools.partial(lambda i, nd: (0,) * nd, nd=v.ndim)) for v in vecs]
    out_specs = [pl.BlockSpec((tr, c), lambda i: (i, 0)) for c, _ in outs]
    out_specs += [pl.BlockSpec((1, c), lambda i: (0, 0)) for c in sums]
    out_shape = [jax.ShapeDtypeStruct((rows, c), dt) for c, dt in outs]
    out_shape += [jax.ShapeDtypeStruct((1, c), F32) for c in sums]
    res = pl.pallas_call(
        body, name=name, grid=(rows // tr,), in_specs=in_specs, out_specs=out_specs, out_shape=out_shape,
        compiler_params=_params(("arbitrary",)))(*[t[0] for t in tens], *vecs)
    return res


def _sigmoid(z):
    return 1.0 / (1.0 + jnp.exp(-z))


def _rms_fwd(x, g, name):
    def fn(xb, gb):
        r = lax.rsqrt(jnp.mean(xb * xb, axis=-1, keepdims=True) + EPS)
        return ((xb * r) * gb,), ()
    return _ew(fn, [x], [g.reshape(1, -1)], outs=[(x.shape[1], BF16)], name=name)[0]


def _rms_bwd(x, g, dh, dres, name):
    def fn(xb, dhb, drb, gb):
        r = lax.rsqrt(jnp.mean(xb * xb, axis=-1, keepdims=True) + EPS)
        xh = xb * r
        dhf = dhb.astype(F32)
        dy = dhf * gb
        dx = r * (dy - xh * jnp.mean(dy * xh, axis=-1, keepdims=True))
        return (drb + dx,), (dhf * xh,)
    dx, dg = _ew(fn, [x, dh, dres], [g.reshape(1, -1)], outs=[(x.shape[1], F32)], sums=[x.shape[1]], name=name)
    return dx, dg[0]


def _ffn_fwd(x1, gain, w_gu, w_down):
    h = _rms_fwd(x1, gain, "ffn_norm")
    gu = _mm(h, w_gu, name="ffn_gate_up", out_dtype=BF16, tn=D_FF // 2)

    def act_fn(gb, ub):
        gf, uf = gb.astype(F32), ub.astype(F32)
        return (gf * _sigmoid(gf) * uf,), ()
    act = _ew(act_fn, [(gu, D_FF, 0), (gu, D_FF, 1)], outs=[(D_FF, BF16)], name="ffn_act")[0]
    x2 = _mm(act, w_down, name="ffn_down", add=x1, tk=D_FF // 2)
    return x2, (x1, h, gu, act)


def _ffn_bwd(dx2, saved, gain, w_gu, w_down):
    x1, h, gu, act = saved
    dact = _mm(dx2, w_down, tb=True, name="ffn_dact", out_dtype=BF16, tn=D_FF // 2)
    dw_down = _mm(act, dx2, ta=True, name="ffn_dw_down", out_dtype=BF16, tm=D_FF // 2)

    def dgu_fn(db, gb, ub):
        df, gf, uf = db.astype(F32), gb.astype(F32), ub.astype(F32)
        sg = _sigmoid(gf)
        silu = gf * sg
        dg = df * uf * (sg * (1.0 + gf * (1.0 - sg)))
        du = df * silu
        return (jnp.concatenate([dg, du], axis=1),), ()
    dgu = _ew(dgu_fn, [dact, (gu, D_FF, 0), (gu, D_FF, 1)], outs=[(2 * D_FF, BF16)], name="ffn_dgu")[0]
    dw_gu = _mm(h, dgu, ta=True, name="ffn_dw_gu", out_dtype=BF16, tn=D_FF // 2)
    dh = _mm(dgu, w_gu, tb=True, name="ffn_dh", tk=512)
    dx1, dgain = _rms_bwd(x1, gain, dh, dx2, "ffn_norm_bwd")
    return dx1, dgain, dw_gu, dw_down


def _loss_head(x, gain, target):
    d = x.shape[1]

    def fn(xb, tb, gb):
        r = lax.rsqrt(jnp.mean(xb * xb, axis=-1, keepdims=True) + EPS)
        xh = xb * r
        y = xh * gb
        err = y - tb
        dyv = err * (1.0 / d)
        dyg = dyv * gb
        dx = r * (dyg - xh * jnp.mean(dyg * xh, axis=-1, keepdims=True))
        return (dx,), (0.5 * err * err * (1.0 / d), dyv * xh)
    dx, lsum, dg = _ew(fn, [x, target], [gain.reshape(1, -1)], outs=[(d, F32)], sums=[d, d], name="loss_head")
    return jnp.sum(lsum), dx, dg[0]


ATT_BLOCK = 256
CUM_BLOCK = 512
NEG_INF = -1e30


def _fox_gate_fwd(fl_row, b_col):
    nh, s = fl_row.shape
    tb = min(CUM_BLOCK, s)

    def body(fl_ref, b_ref, z_ref, c_ref):
        upper = (lax.broadcasted_iota(jnp.int32, (tb, tb), 0) <= lax.broadcasted_iota(jnp.int32, (tb, tb), 1)).astype(F32)
        carry = jnp.zeros((nh, 1), F32)
        for blk in range(s // tb):
            z = fl_ref[:, blk * tb:(blk + 1) * tb] + b_ref[...]
            logf = jnp.minimum(z, 0.0) - jnp.log(1.0 + jnp.exp(-jnp.abs(z)))
            cs = jnp.dot(logf, upper, precision=lax.Precision.HIGHEST, preferred_element_type=F32) + carry
            z_ref[:, blk * tb:(blk + 1) * tb] = z
            c_ref[:, blk * tb:(blk + 1) * tb] = cs
            carry = cs[:, tb - 1:tb]

    return pl.pallas_call(body, name="fox_gate_fwd", out_shape=[jax.ShapeDtypeStruct((nh, s), F32)] * 2,
                          compiler_params=_params())(fl_row, b_col)


def _fox_gate_bwd(dc_row, z_row):
    nh, s = dc_row.shape
    tb = min(CUM_BLOCK, s)

    def body(dc_ref, z_ref, dz_ref, db_ref):
        lower = (lax.broadcasted_iota(jnp.int32, (tb, tb), 0) >= lax.broadcasted_iota(jnp.int32, (tb, tb), 1)).astype(F32)
        carry = jnp.zeros((nh, 1), F32)
        db = jnp.zeros((nh, 1), F32)
        for blk in reversed(range(s // tb)):
            dc = dc_ref[:, blk * tb:(blk + 1) * tb]
            rs = jnp.dot(dc, lower, precision=lax.Precision.HIGHEST, preferred_element_type=F32) + carry
            dz = rs * _sigmoid(-z_ref[:, blk * tb:(blk + 1) * tb])
            dz_ref[:, blk * tb:(blk + 1) * tb] = dz
            db = db + jnp.sum(dz, axis=1, keepdims=True)
            carry = rs[:, 0:1]
        db_ref[...] = db

    return pl.pallas_call(body, name="fox_gate_bwd",
                          out_shape=[jax.ShapeDtypeStruct((nh, s), F32), jax.ShapeDtypeStruct((nh, 1), F32)],
                          compiler_params=_params())(dc_row, z_row)


def _head_masks(rows):
    lane = lax.broadcasted_iota(jnp.int32, (rows, LANES), 1)
    return lane < FOX_HEAD_DIM


def _attn_fwd(qkv, c_row):
    s = qkv.shape[0]
    t = min(ATT_BLOCK, s)
    nq = s // t
    scale = FOX_HEAD_DIM ** -0.5

    def body(q_ref, k_ref, v_ref, cr_ref, o_ref, lse_ref):
        i = pl.program_id(1)
        first = _head_masks(t)
        qs = q_ref[...] * scale
        zero = jnp.zeros_like(qs)
        qh = (jnp.where(first, qs, zero), jnp.where(first, zero, qs))
        causal = lax.broadcasted_iota(jnp.int32, (t, t), 0) >= lax.broadcasted_iota(jnp.int32, (t, t), 1)

        def block(j, carry, masked):
            ms, ls, acc = carry
            start = pl.multiple_of(j * t, t)
            ks, vs = k_ref[pl.ds(start, t), :], v_ref[pl.ds(start, t), :]
            new_m, new_l, alphas, pvs = [], [], [], []
            for hh in range(2):
                sc = lax.dot_general(qh[hh], ks, (((1,), (1,)), ((), ())), preferred_element_type=F32)
                sc = sc - cr_ref[0, hh:hh + 1, pl.ds(start, t)]
                if masked:
                    sc = jnp.where(causal, sc, NEG_INF)
                m_new = jnp.maximum(ms[hh], jnp.max(sc, axis=1, keepdims=True))
                p = jnp.exp(sc - m_new)
                alpha = jnp.exp(ms[hh] - m_new)
                new_m.append(m_new)
                new_l.append(alpha * ls[hh] + jnp.sum(p, axis=1, keepdims=True))
                alphas.append(alpha)
                p_hi = p.astype(MXU_DTYPE)
                p_lo = (p - p_hi.astype(F32)).astype(MXU_DTYPE)
                pvs.append(jnp.dot(p_hi, vs, preferred_element_type=F32) + jnp.dot(p_lo, vs, preferred_element_type=F32))
            acc = jnp.where(first, alphas[0], alphas[1]) * acc + jnp.where(first, pvs[0], pvs[1])
            return tuple(new_m), tuple(new_l), acc

        neg, nil = jnp.full((t, 1), NEG_INF, F32), jnp.zeros((t, 1), F32)
        carry = lax.fori_loop(0, i, functools.partial(block, masked=False), ((neg, neg), (nil, nil), jnp.zeros((t, LANES), F32)))
        ms, ls, acc = block(i, carry, True)
        o_ref[...] = (acc / jnp.where(first, ls[0], ls[1])).astype(o_ref.dtype)
        lse_ref[0] = jnp.concatenate([ms[0] + jnp.log(ls[0]), ms[1] + jnp.log(ls[1])], axis=1)

    np_ = HEAD_PAIRS
    return pl.pallas_call(
        body, name="fox_attn_fwd", grid=(np_, nq),
        in_specs=[pl.BlockSpec((t, LANES), lambda p, i: (i, p)),
                  pl.BlockSpec((s, LANES), lambda p, i: (0, np_ + p)),
                  pl.BlockSpec((s, LANES), lambda p, i: (0, 2 * np_ + p)),
                  pl.BlockSpec((1, 2, s), lambda p, i: (p, 0, 0))],
        out_specs=[pl.BlockSpec((t, LANES), lambda p, i: (i, p)),
                   pl.BlockSpec((1, t, 2), lambda p, i: (p, i, 0))],
        out_shape=[jax.ShapeDtypeStruct((s, D_MODEL), F32), jax.ShapeDtypeStruct((np_, s, 2), F32)],
        compiler_params=_params(("parallel", "arbitrary")))(qkv, qkv, qkv, c_row)


def _attn_bwd(qkv, do, lse, delta, c_row):
    s = qkv.shape[0]
    t = min(ATT_BLOCK, s)
    nb = s // t
    scale = FOX_HEAD_DIM ** -0.5
    np_ = HEAD_PAIRS

    def body(q_ref, k_ref, v_ref, do_ref, lse_ref, dl_ref, cr_ref, dq_ref, dk_ref, dv_ref, dc_ref):
        j = pl.program_id(1)
        first = _head_masks(t)
        causal = lax.broadcasted_iota(jnp.int32, (t, t), 0) >= lax.broadcasted_iota(jnp.int32, (t, t), 1)
        kb = k_ref[...]
        vb = v_ref[...]

        @pl.when(j == 0)
        def _():
            dq_ref[...] = jnp.zeros_like(dq_ref)

        def step(i, carry, masked):
            dk_acc, dv_acc, dc_accs = carry
            rows = pl.ds(pl.multiple_of(i * t, t), t)
            qs = q_ref[rows, :] * scale
            dob = do_ref[rows, :]
            zero = jnp.zeros_like(qs)
            dks, dvs, dqs, dcs = [], [], [], []
            for hh in range(2):
                qh = jnp.where(first, qs, zero) if hh == 0 else jnp.where(first, zero, qs)
                doh = jnp.where(first, dob, zero) if hh == 0 else jnp.where(first, zero, dob)
                sc = lax.dot_general(qh, kb, (((1,), (1,)), ((), ())), preferred_element_type=F32)
                p = jnp.exp(sc - cr_ref[0, hh:hh + 1, :] - lse_ref[0, rows, hh:hh + 1])
                if masked:
                    p = jnp.where(causal, p, 0.0)
                dp = lax.dot_general(doh, vb, (((1,), (1,)), ((), ())), preferred_element_type=F32)
                ds = p * (dp - dl_ref[0, rows, hh:hh + 1])
                pb, dsb = p.astype(MXU_DTYPE), ds.astype(MXU_DTYPE)
                dvs.append(lax.dot_general(pb, dob, (((0,), (0,)), ((), ())), preferred_element_type=F32))
                dks.append(lax.dot_general(dsb, qs, (((0,), (0,)), ((), ())), preferred_element_type=F32))
                dqs.append(jnp.dot(dsb, kb, preferred_element_type=F32))
                dcs.append(dc_accs[hh] - jnp.sum(ds, axis=0, keepdims=True))
            dq_ref[rows, :] += jnp.where(first, dqs[0], dqs[1]) * scale
            return (dk_acc + jnp.where(first, dks[0], dks[1]), dv_acc + jnp.where(first, dvs[0], dvs[1]), tuple(dcs))

        nil = jnp.zeros((1, t), F32)
        carry = step(j, (jnp.zeros((t, LANES), F32), jnp.zeros((t, LANES), F32), (nil, nil)), True)
        dk_acc, dv_acc, dc_accs = lax.fori_loop(j + 1, nb, functools.partial(step, masked=False), carry)
        dk_ref[...] = dk_acc.astype(dk_ref.dtype)
        dv_ref[...] = dv_acc.astype(dv_ref.dtype)
        dc_ref[0] = jnp.concatenate(dc_accs, axis=0)

    return pl.pallas_call(
        body, name="fox_attn_bwd", grid=(np_, nb),
        in_specs=[pl.BlockSpec((s, LANES), lambda p, j: (0, p)),
                  pl.BlockSpec((t, LANES), lambda p, j: (j, np_ + p)),
                  pl.BlockSpec((t, LANES), lambda p, j: (j, 2 * np_ + p)),
                  pl.BlockSpec((s, LANES), lambda p, j: (0, p)),
                  pl.BlockSpec((1, s, 2), lambda p, j: (p, 0, 0)),
                  pl.BlockSpec((1, s, 2), lambda p, j: (p, 0, 0)),
                  pl.BlockSpec((1, 2, t), lambda p, j: (p, 0, j))],
        out_specs=[pl.BlockSpec((s, LANES), lambda p, j: (0, p)),
                   pl.BlockSpec((t, LANES), lambda p, j: (j, p)),
                   pl.BlockSpec((t, LANES), lambda p, j: (j, p)),
                   pl.BlockSpec((1, 2, t), lambda p, j: (p, 0, j))],
        out_shape=[jax.ShapeDtypeStruct((s, D_MODEL), F32), jax.ShapeDtypeStruct((s, D_MODEL), BF16),
                   jax.ShapeDtypeStruct((s, D_MODEL), BF16), jax.ShapeDtypeStruct((np_, 2, s), F32)],
        compiler_params=_params(("parallel", "arbitrary")))(qkv, qkv, qkv, do, lse, delta, c_row)


def _head_sums(a, b, name):
    d = a.shape[1]
    sel = (jnp.arange(d)[:, None] // FOX_HEAD_DIM == jnp.arange(LANES)[None, :]).astype(F32)

    def fn(ab, bb, selb):
        prod = ab.astype(F32) * bb.astype(F32)
        return (jnp.dot(prod, selb, precision=lax.Precision.HIGHEST, preferred_element_type=F32),), ()
    return _ew(fn, [a, b], [sel], outs=[(LANES, F32)], name=name)[0]


def _pairs_col(a16):
    s = a16.shape[0]
    return a16.reshape(s, HEAD_PAIRS, 2).transpose(1, 0, 2)


def _fox_fwd(x, gain, w_qkv, w_f, b_f, w_out):
    s = x.shape[0]
    h = _rms_fwd(x, gain, "mix_norm")
    qkv = _mm(h, w_qkv, name="fox_qkv", out_dtype=BF16)
    fl = _mm(h, w_f, name="fox_f", tn=LANES)
    z_row, c_rowf = _fox_gate_fwd(fl[:, :FOX_HEADS].T, b_f.reshape(FOX_HEADS, 1))
    c_row = c_rowf.reshape(HEAD_PAIRS, 2, s)
    o, lse = _attn_fwd(qkv, c_row)
    x1 = _mm(o, w_out, name="fox_out", add=x)
    return x1, (x, h, qkv, z_row, c_row, o, lse)


def _fox_bwd(dx1, saved, gain, w_qkv, w_f, w_out):
    x, h, qkv, z_row, c_row, o, lse = saved
    s = x.shape[0]
    do = _mm(dx1, w_out, tb=True, name="fox_do", out_dtype=BF16)
    dw_out = _mm(o, dx1, ta=True, name="fox_dw_out", out_dtype=BF16)
    delta = _pairs_col(_head_sums(do, o, "fox_delta")[:, :FOX_HEADS])
    dq, dk, dv, dc = _attn_bwd(qkv, do, lse, delta, c_row)
    dz_row, db = _fox_gate_bwd(dc.reshape(FOX_HEADS, s), z_row)
    dqkv = jnp.concatenate([dq.astype(BF16), dk, dv], axis=1)
    dfl = jnp.pad(dz_row.T, ((0, 0), (0, LANES - FOX_HEADS))).astype(BF16)
    dw_qkv = _mm(h, dqkv, ta=True, name="fox_dw_qkv", out_dtype=BF16)
    dw_f = _mm(h, dfl, ta=True, name="fox_dw_f", out_dtype=BF16, tn=LANES)
    dh = _mm(dqkv, w_qkv, tb=True, name="fox_dh_qkv")
    dh = _mm(dfl, w_f, tb=True, name="fox_dh_f", add=dh)
    dx, dgain = _rms_bwd(x, gain, dh, dx1, "mix_norm_bwd")
    dw_in = jnp.concatenate([dw_qkv, dw_f[:, :FOX_HEADS]], axis=1)
    return dx, dgain, dw_in, db.reshape(FOX_HEADS), dw_out


S5_ROWS = 512
SCAN_CHUNKS = SUBLANES


def _s5_operands(a_re, a_im, log_dt, b_re, b_im, c_re, c_im):
    dt = jnp.exp(log_dt)[:, None]
    mag, ang = jnp.exp(a_re * dt), a_im * dt
    lr, li = mag * jnp.cos(ang), mag * jnp.sin(ang)
    den = a_re * a_re + a_im * a_im
    cr = ((lr - 1.0) * a_re + li * a_im) / den
    ci = (li * a_re - (lr - 1.0) * a_im) / den
    bbr = cr[..., None] * b_re - ci[..., None] * b_im
    bbi = cr[..., None] * b_im + ci[..., None] * b_re
    nb = S5_BLOCKS
    lam = jnp.stack([lr.reshape(nb, 2, S5_HALF), li.reshape(nb, 2, S5_HALF)], axis=2)
    eye4, eye2 = jnp.eye(4, dtype=F32), jnp.eye(2, dtype=F32)
    bb = jnp.stack([bbr, bbi], axis=0).reshape(2, nb, 2, 4, S5_STATE, S5_GROUP)
    bmat = jnp.einsum("rbhgpc,kg,jh->bhjkcrgp", bb, eye4, eye2).reshape(nb, 2, 128, 2 * S5_HALF)
    cc = jnp.stack([c_re, -c_im], axis=0).reshape(2, nb, 2, 4, S5_GROUP, S5_STATE)
    cmat = jnp.einsum("rbhgcp,kg,jh->bhrgpjkc", cc, eye4, eye2).reshape(nb, 2, 2 * S5_HALF, 128)
    return lam, bmat, cmat


def _time_to_scan_order(a):
    s, d = a.shape
    return a.reshape(SCAN_CHUNKS, s // SCAN_CHUNKS, d).transpose(1, 0, 2).reshape(s, d)


def _scan_to_time_order(a):
    s, d = a.shape
    return a.reshape(s // SCAN_CHUNKS, SCAN_CHUNKS, d).transpose(1, 0, 2).reshape(s, d)


def _scan_chunks(xr_ref, xi_ref, lr, li, nst, reverse, after_step=None):
    lanes = lr.shape[1]
    lr8, li8 = jnp.broadcast_to(lr, (SUBLANES, lanes)), jnp.broadcast_to(li, (SUBLANES, lanes))
    zero8 = jnp.zeros((SUBLANES, lanes), F32)

    def rows_of(n):
        s = (nst - 1 - n) if reverse else n
        return s, pl.ds(pl.multiple_of(s * SUBLANES, SUBLANES), SUBLANES)

    def local(n, carry):
        pr, pi = carry
        _, rows = rows_of(n)
        nr = lr8 * pr - li8 * pi + xr_ref[rows, :]
        ni = lr8 * pi + li8 * pr + xi_ref[rows, :]
        xr_ref[rows, :] = nr
        xi_ref[rows, :] = ni
        return nr, ni

    er, ei = lax.fori_loop(0, nst, local, (zero8, zero8))
    pr, pi = lr, li
    for _ in range(int(math.log2(nst))):
        pr, pi = pr * pr - pi * pi, 2.0 * pr * pi
    tr = ti = jnp.zeros((1, lanes), F32)
    ent_r, ent_i = [None] * SCAN_CHUNKS, [None] * SCAN_CHUNKS
    for k in (reversed(range(SCAN_CHUNKS)) if reverse else range(SCAN_CHUNKS)):
        ent_r[k], ent_i[k] = tr, ti
        tr, ti = er[k:k + 1] + (pr * tr - pi * ti), ei[k:k + 1] + (pr * ti + pi * tr)
    in_r, in_i = jnp.concatenate(ent_r, axis=0), jnp.concatenate(ent_i, axis=0)

    def fix(n, carry):
        wr, wi = carry
        s, rows = rows_of(n)
        nr = xr_ref[rows, :] + (wr * in_r - wi * in_i)
        ni = xi_ref[rows, :] + (wr * in_i + wi * in_r)
        xr_ref[rows, :] = nr
        xi_ref[rows, :] = ni
        if after_step is not None:
            after_step(s, nr, ni)
        return wr * lr8 - wi * li8, wr * li8 + wi * lr8

    lax.fori_loop(0, nst, fix, (lr8, li8))
    return in_r, in_i


def _s5_fill_states(u_ref, bm, xr_ref, xi_ref, s):
    rc = min(S5_ROWS, s)

    def fill(r, _):
        rows = pl.ds(pl.multiple_of(r * rc, rc), rc)
        bu = jnp.dot(u_ref[rows, :].astype(MXU_DTYPE), bm, preferred_element_type=F32)
        xr_ref[rows, :] = bu[:, :S5_HALF]
        xi_ref[rows, :] = bu[:, S5_HALF:]
        return 0
    lax.fori_loop(0, s // rc, fill, 0)


def _s5_specs():
    return [pl.BlockSpec((1, 2, 2, S5_HALF), lambda b: (b, 0, 0, 0)),
            pl.BlockSpec((1, 2, 128, 2 * S5_HALF), lambda b: (b, 0, 0, 0)),
            pl.BlockSpec((1, 2, 2 * S5_HALF, 128), lambda b: (b, 0, 0, 0)),
            pl.BlockSpec((1, LANES), lambda b: (0, b))]


def _s5_scan_fwd(u, lam, bmat, cmat, dvec):
    s = u.shape[0]
    nst = s // SCAN_CHUNKS
    rc = min(S5_ROWS, s)

    def body(u_ref, lam_ref, b_ref, c_ref, d_ref, y_ref, xr_ref, xi_ref):
        y_ref[...] = u_ref[...] * d_ref[...]
        for hb in range(2):
            _s5_fill_states(u_ref, b_ref[0, hb], xr_ref, xi_ref, s)
            _scan_chunks(xr_ref, xi_ref, lam_ref[0, hb, 0:1, :], lam_ref[0, hb, 1:2, :], nst, False)
            cm = c_ref[0, hb]

            def emit(r, _, cm=cm):
                rows = pl.ds(pl.multiple_of(r * rc, rc), rc)
                y_ref[rows, :] += (jnp.dot(xr_ref[rows, :].astype(MXU_DTYPE), cm[:S5_HALF], preferred_element_type=F32)
                                   + jnp.dot(xi_ref[rows, :].astype(MXU_DTYPE), cm[S5_HALF:], preferred_element_type=F32))
                return 0
            lax.fori_loop(0, s // rc, emit, 0)

    blk = pl.BlockSpec((s, LANES), lambda b: (0, b))
    return pl.pallas_call(
        body, name="s5_scan_fwd", grid=(S5_BLOCKS,), in_specs=[blk] + _s5_specs(), out_specs=blk,
        out_shape=jax.ShapeDtypeStruct(u.shape, F32),
        scratch_shapes=[pltpu.VMEM((s, S5_HALF), F32)] * 2,
        compiler_params=_params(("parallel",)))(u, lam, bmat, cmat, dvec)


def _s5_scan_bwd(u, dy, lam, bmat, cmat, dvec):
    s = u.shape[0]
    nst = s // SCAN_CHUNKS
    rc = min(S5_ROWS, s)
    nt = (((1,), (1,)), ((), ()))
    tn = (((0,), (0,)), ((), ()))

    def body(u_ref, dy_ref, lam_ref, b_ref, c_ref, d_ref, du_ref, db_ref, dc_ref, dl_ref, dd_ref,
             xr_ref, xi_ref, gr_ref, gi_ref, ar_ref, ai_ref):
        du_ref[...] = dy_ref[...] * d_ref[...]
        dd_ref[...] = jnp.sum(dy_ref[...] * u_ref[...], axis=0, keepdims=True)
        db_ref[...] = jnp.zeros_like(db_ref)
        dc_ref[...] = jnp.zeros_like(dc_ref)
        for hb in range(2):
            bm, cm = b_ref[0, hb], c_ref[0, hb]
            lr, li = lam_ref[0, hb, 0:1, :], lam_ref[0, hb, 1:2, :]
            _s5_fill_states(u_ref, bm, xr_ref, xi_ref, s)
            xin_r, xin_i = _scan_chunks(xr_ref, xi_ref, lr, li, nst, False)

            def fill_g(r, _, cm=cm):
                rows = pl.ds(pl.multiple_of(r * rc, rc), rc)
                g = lax.dot_general(dy_ref[rows, :].astype(MXU_DTYPE), cm, nt, preferred_element_type=F32)
                gr_ref[rows, :] = g[:, :S5_HALF]
                gi_ref[rows, :] = g[:, S5_HALF:]
                return 0
            lax.fori_loop(0, s // rc, fill_g, 0)
            ar_ref[...] = jnp.zeros_like(ar_ref)
            ai_ref[...] = jnp.zeros_like(ai_ref)

            def lam_grad(st, g_r, g_i, xin_r=xin_r, xin_i=xin_i):
                prev = pl.ds(pl.multiple_of(jnp.maximum(st - 1, 0) * SUBLANES, SUBLANES), SUBLANES)
                x_r = jnp.where(st > 0, xr_ref[prev, :], xin_r)
                x_i = jnp.where(st > 0, xi_ref[prev, :], xin_i)
                ar_ref[...] += g_r * x_r + g_i * x_i
                ai_ref[...] += g_i * x_r - g_r * x_i

            _scan_chunks(gr_ref, gi_ref, lr, -li, nst, True, after_step=lam_grad)
            dl_ref[0, hb] = jnp.concatenate([jnp.sum(ar_ref[...], axis=0, keepdims=True),
                                             jnp.sum(ai_ref[...], axis=0, keepdims=True)], axis=0)

            def emit(r, _, bm=bm, hb=hb):
                rows = pl.ds(pl.multiple_of(r * rc, rc), rc)
                g = jnp.concatenate([gr_ref[rows, :], gi_ref[rows, :]], axis=1).astype(MXU_DTYPE)
                x = jnp.concatenate([xr_ref[rows, :], xi_ref[rows, :]], axis=1).astype(MXU_DTYPE)
                du_ref[rows, :] += lax.dot_general(g, bm, nt, preferred_element_type=F32)
                db_ref[0, hb] += lax.dot_general(u_ref[rows, :].astype(MXU_DTYPE), g, tn, preferred_element_type=F32)
                dc_ref[0, hb] += lax.dot_general(dy_ref[rows, :].astype(MXU_DTYPE), x, tn, preferred_element_type=F32)
                return 0
            lax.fori_loop(0, s // rc, emit, 0)

    blk = pl.BlockSpec((s, LANES), lambda b: (0, b))
    mat = pl.BlockSpec((1, 2, 128, 2 * S5_HALF), lambda b: (b, 0, 0, 0))
    return pl.pallas_call(
        body, name="s5_scan_bwd", grid=(S5_BLOCKS,), in_specs=[blk, blk] + _s5_specs(),
        out_specs=[blk, mat, mat, pl.BlockSpec((1, 2, 2, S5_HALF), lambda b: (b, 0, 0, 0)),
                   pl.BlockSpec((1, LANES), lambda b: (0, b))],
        out_shape=[jax.ShapeDtypeStruct(u.shape, F32),
                   jax.ShapeDtypeStruct((S5_BLOCKS, 2, 128, 2 * S5_HALF), F32),
                   jax.ShapeDtypeStruct((S5_BLOCKS, 2, 128, 2 * S5_HALF), F32),
                   jax.ShapeDtypeStruct((S5_BLOCKS, 2, 2, S5_HALF), F32),
                   jax.ShapeDtypeStruct((1, D_MODEL), F32)],
        scratch_shapes=[pltpu.VMEM((s, S5_HALF), F32)] * 4 + [pltpu.VMEM((SUBLANES, S5_HALF), F32)] * 2,
        compiler_params=_params(("parallel",)))(u, dy, lam, bmat, cmat, dvec)


_GELU_C = math.sqrt(2.0 / math.pi)


def _gelu_parts(y):
    inner = _GELU_C * (y + 0.044715 * y * y * y)
    th = jnp.tanh(inner)
    return 0.5 * y * (1.0 + th), th


def _s5_fwd(x, gain, w_in, ssm, dvec, w_glu):
    lam, bmat, cmat = ssm
    h = _rms_fwd(x, gain, "mix_norm")
    u = _mm(h, w_in, name="s5_in")
    y = _scan_to_time_order(_s5_scan_fwd(_time_to_scan_order(u), lam, bmat.astype(MXU_DTYPE), cmat.astype(MXU_DTYPE), dvec))
    g = _ew(lambda yb: ((_gelu_parts(yb)[0],), ()), [y], outs=[(D_MODEL, BF16)], name="s5_gelu")[0]
    vg = _mm(g, w_glu, name="s5_glu", out_dtype=BF16)

    def glu_fn(vb, gb, xb):
        return (xb + vb.astype(F32) * _sigmoid(gb.astype(F32)),), ()
    x1 = _ew(glu_fn, [(vg, D_MODEL, 0), (vg, D_MODEL, 1), x], outs=[(D_MODEL, F32)], name="s5_gate")[0]
    return x1, (x, h, u, y, g, vg)


def _s5_bwd(dx1, saved, gain, w_in, ssm, dvec, w_glu):
    x, h, u, y, g, vg = saved
    lam, bmat, cmat = ssm

    def dglu_fn(db, vb, gb):
        vf, sg = vb.astype(F32), _sigmoid(gb.astype(F32))
        return (jnp.concatenate([db * sg, db * vf * sg * (1.0 - sg)], axis=1),), ()
    dvg = _ew(dglu_fn, [dx1, (vg, D_MODEL, 0), (vg, D_MODEL, 1)], outs=[(2 * D_MODEL, BF16)], name="s5_dgate")[0]
    dw_glu = _mm(g, dvg, ta=True, name="s5_dw_glu", out_dtype=BF16)
    dg = _mm(dvg, w_glu, tb=True, name="s5_dg")

    def dgelu_fn(dgb, yb):
        _, th = _gelu_parts(yb)
        dinner = _GELU_C * (1.0 + 3.0 * 0.044715 * yb * yb)
        return (dgb * (0.5 * (1.0 + th) + 0.5 * yb * (1.0 - th * th) * dinner),), ()
    dy = _ew(dgelu_fn, [dg, y], outs=[(D_MODEL, F32)], name="s5_dgelu")[0]
    du_s, dbm, dct, dlam, ddvec = _s5_scan_bwd(_time_to_scan_order(u), _time_to_scan_order(dy), lam,
                                               bmat.astype(MXU_DTYPE), cmat.astype(MXU_DTYPE), dvec)
    du = _scan_to_time_order(du_s)
    dw_in = _mm(h, du, ta=True, name="s5_dw_in", out_dtype=BF16)
    dh = _mm(du, w_in, tb=True, name="s5_dh")
    dx, dgain = _rms_bwd(x, gain, dh, dx1, "mix_norm_bwd")
    return dx, dgain, dw_in, (dlam, dbm, jnp.swapaxes(dct, 2, 3)), ddvec, dw_glu


POOL_BLOCK = 256
N_POOL_GROUPS = len(POOL_WINDOWS)


def _pool_bands(gi, i, t):
    w = jnp.left_shift(2, gi)
    r = lax.broadcasted_iota(jnp.int32, (t, t), 0)
    c = lax.broadcasted_iota(jnp.int32, (t, t), 1)
    inside = ((c <= r) & (c > r - w)).astype(MXU_DTYPE)
    before = (c > r - w + t).astype(MXU_DTYPE)

    def inv_count(block):
        pos = block * t + lax.broadcasted_iota(jnp.int32, (t, 1), 0)
        return 1.0 / jnp.minimum(pos + 1, w).astype(F32)
    return inside, before, inv_count


def _pool_fwd(x, gain, w_grp, b_grp, scale):
    s = x.shape[0]
    t = min(POOL_BLOCK, s)
    h = _rms_fwd(x, gain, "mix_norm")

    def body(h_ref, hp_ref, w_ref, b_ref, sc_ref, x_ref, x1_ref, diff_ref):
        gi, i = pl.program_id(0), pl.program_id(1)
        inside, before, inv_count = _pool_bands(gi, i, t)
        hc = h_ref[...]
        tot = jnp.dot(inside, hc.astype(MXU_DTYPE), preferred_element_type=F32)
        prev = jnp.dot(before, hp_ref[...].astype(MXU_DTYPE), preferred_element_type=F32)
        tot = tot + jnp.where(i > 0, prev, 0.0)
        diff = (tot * inv_count(i) - hc.astype(F32)).astype(diff_ref.dtype)
        y = (jnp.dot(diff.astype(MXU_DTYPE), w_ref[0], preferred_element_type=F32) + b_ref[...]) * sc_ref[...]
        diff_ref[...] = diff
        x1_ref[...] = x_ref[...] + y

    blk = pl.BlockSpec((t, POOL_WIDTH), lambda gi, i: (i, gi))
    vec = pl.BlockSpec((1, POOL_WIDTH), lambda gi, i: (0, gi))
    x1, diff = pl.pallas_call(
        body, name="pool_fwd", grid=(N_POOL_GROUPS, s // t),
        in_specs=[blk, pl.BlockSpec((t, POOL_WIDTH), lambda gi, i: (jnp.maximum(i - 1, 0), gi)),
                  pl.BlockSpec((1, POOL_WIDTH, POOL_WIDTH), lambda gi, i: (gi, 0, 0)), vec, vec, blk],
        out_specs=[blk, blk],
        out_shape=[jax.ShapeDtypeStruct(x.shape, F32), jax.ShapeDtypeStruct(x.shape, BF16)],
        compiler_params=_params(("parallel", "arbitrary")))(h, h, w_grp, b_grp, scale, x)
    return x1, (x, diff)


def _pool_bwd(dx1, saved, gain, w_grp, b_grp, scale):
    x, diff = saved
    s = x.shape[0]
    t = min(POOL_BLOCK, s)
    nb = s // t

    def body1(dx_ref, diff_ref, w_ref, b_ref, sc_ref, dd_ref, dw_ref, db_ref, dsc_ref):
        i = pl.program_id(1)

        @pl.when(i == 0)
        def _():
            dw_ref[...] = jnp.zeros_like(dw_ref)
            db_ref[...] = jnp.zeros_like(db_ref)
            dsc_ref[...] = jnp.zeros_like(dsc_ref)

        dfb = diff_ref[...].astype(MXU_DTYPE)
        ypre = jnp.dot(dfb, w_ref[0], preferred_element_type=F32) + b_ref[...]
        dxb = dx_ref[...]
        dy = dxb * sc_ref[...]
        dsc_ref[...] += jnp.sum(dxb * ypre, axis=0, keepdims=True)
        db_ref[...] += jnp.sum(dy, axis=0, keepdims=True)
        dyb = dy.astype(MXU_DTYPE)
        dw_ref[0] += lax.dot_general(dfb, dyb, (((0,), (0,)), ((), ())), preferred_element_type=F32)
        dd_ref[...] = lax.dot_general(dyb, w_ref[0], (((1,), (1,)), ((), ())), preferred_element_type=F32)

    blk = pl.BlockSpec((t, POOL_WIDTH), lambda gi, i: (i, gi))
    vec = pl.BlockSpec((1, POOL_WIDTH), lambda gi, i: (0, gi))
    mat = pl.BlockSpec((1, POOL_WIDTH, POOL_WIDTH), lambda gi, i: (gi, 0, 0))
    ddiff, dw, db, dsc = pl.pallas_call(
        body1, name="pool_bwd_map", grid=(N_POOL_GROUPS, nb), in_specs=[blk, blk, mat, vec, vec],
        out_specs=[blk, mat, vec, vec],
        out_shape=[jax.ShapeDtypeStruct(x.shape, F32), jax.ShapeDtypeStruct(w_grp.shape, F32),
                   jax.ShapeDtypeStruct((1, D_MODEL), F32), jax.ShapeDtypeStruct((1, D_MODEL), F32)],
        compiler_params=_params(("parallel", "arbitrary")))(dx1, diff, w_grp, b_grp, scale)

    def body2(dc_ref, dn_ref, dh_ref):
        gi, i = pl.program_id(0), pl.program_id(1)
        inside, before, inv_count = _pool_bands(gi, i, t)
        tn = (((0,), (0,)), ((), ()))
        dc = dc_ref[...]
        tot = lax.dot_general(inside, (dc * inv_count(i)).astype(MXU_DTYPE), tn, preferred_element_type=F32)
        nxt = lax.dot_general(before, (dn_ref[...] * inv_count(i + 1)).astype(MXU_DTYPE), tn, preferred_element_type=F32)
        dh_ref[...] = tot + jnp.where(i < nb - 1, nxt, 0.0) - dc

    dh = pl.pallas_call(
        body2, name="pool_bwd_window", grid=(N_POOL_GROUPS, nb),
        in_specs=[blk, pl.BlockSpec((t, POOL_WIDTH), lambda gi, i: (jnp.minimum(i + 1, nb - 1), gi))],
        out_specs=blk, out_shape=jax.ShapeDtypeStruct(x.shape, F32),
        compiler_params=_params(("parallel", "parallel")))(ddiff, ddiff)
    dx, dgain = _rms_bwd(x, gain, dh, dx1, "mix_norm_bwd")
    return dx, dgain, dw, db, dsc


MESH_ID = pl.DeviceIdType.MESH
ANY = pl.BlockSpec(memory_space=pl.ANY)


def _place():
    x, y, c = lax.axis_index("x"), lax.axis_index("y"), lax.axis_index("c")
    other_chips = [(1 - x, y), (x, 1 - y), (1 - x, 1 - y)]
    return x, y, c, other_chips


def _chip_index(chip):
    return 2 * chip[0] + chip[1]


def _remote(src, dst, send_sems, recv_sems, n, to):
    return pltpu.make_async_remote_copy(src_ref=src, dst_ref=dst, send_sem=send_sems.at[n], recv_sem=recv_sems.at[n],
                                        device_id=to, device_id_type=MESH_ID)


def _gather_weights(ws):
    n = len(ws)

    def body(*refs):
        w_refs, out_refs = refs[:n], refs[n:2 * n]
        send_sems, recv_sems, local_sems = refs[2 * n:]
        x, y, c, chips = _place()
        k = _chip_index((x, y))
        me, sibling = (x, y, c), (x, y, 1 - c)
        mine = [pltpu.make_async_copy(w_refs[t], out_refs[t].at[k], local_sems.at[t]) for t in range(n)]
        first = [_remote(w_refs[t].at[c], out_refs[t].at[k, c], send_sems, recv_sems, 6 * t + j, (*chip, c))
                 for j, chip in enumerate(chips) for t in range(n)]
        for cp in mine + first:
            cp.start()
        passed = []
        for j, chip in enumerate(chips):
            for t in range(n):
                landed = out_refs[t].at[_chip_index(chip), c]
                _remote(landed, landed, send_sems, recv_sems, 6 * t + j, me).wait_recv()
                passed.append(_remote(landed, landed, send_sems, recv_sems, 6 * t + 3 + j, sibling))
                passed[-1].start()
        for j, chip in enumerate(chips):
            for t in range(n):
                theirs = out_refs[t].at[_chip_index(chip), 1 - c]
                _remote(theirs, theirs, send_sems, recv_sems, 6 * t + 3 + j, me).wait_recv()
        for cp in first + passed:
            cp.wait_send()
        for cp in mine:
            cp.wait()

    return pl.pallas_call(
        body, name="gather_weights", in_specs=[ANY] * n, out_specs=[ANY] * n,
        out_shape=[jax.ShapeDtypeStruct((N_CHIPS,) + w.shape, w.dtype) for w in ws],
        scratch_shapes=[pltpu.SemaphoreType.DMA((6 * n,)), pltpu.SemaphoreType.DMA((6 * n,)), pltpu.SemaphoreType.DMA((n,))],
    )(*ws)


def _swap_halves(gs):
    n = len(gs)

    def body(*refs):
        g_refs, out_refs = refs[:n], refs[n:2 * n]
        send_sems, recv_sems = refs[2 * n:]
        x, y, c, _ = _place()
        copies = [_remote(g_refs[t].at[s, 1 - c], out_refs[t].at[s], send_sems, recv_sems, N_CHIPS * t + s, (x, y, 1 - c))
                  for t in range(n) for s in range(N_CHIPS)]
        for cp in copies:
            cp.start()
        for cp in copies:
            cp.wait_recv()
        for cp in copies:
            cp.wait_send()

    return pl.pallas_call(
        body, name="swap_halves", in_specs=[ANY] * n, out_specs=[ANY] * n,
        out_shape=[jax.ShapeDtypeStruct((N_CHIPS,) + g.shape[2:], g.dtype) for g in gs],
        scratch_shapes=[pltpu.SemaphoreType.DMA((N_CHIPS * n,)), pltpu.SemaphoreType.DMA((N_CHIPS * n,))],
    )(*gs)


def _scatter_partials(ps):
    n = len(ps)

    def body(*refs):
        p_refs, out_refs = refs[:n], refs[n:2 * n]
        send_sems, recv_sems, local_sems = refs[2 * n:]
        x, y, c, chips = _place()
        k = _chip_index((x, y))
        mine = [pltpu.make_async_copy(p_refs[t].at[k], out_refs[t].at[k], local_sems.at[t]) for t in range(n)]
        sends = [_remote(p_refs[t].at[_chip_index(chip)], out_refs[t].at[k], send_sems, recv_sems, 3 * t + j, (*chip, c))
                 for j, chip in enumerate(chips) for t in range(n)]
        for cp in mine + sends:
            cp.start()
        for j, chip in enumerate(chips):
            for t in range(n):
                slot = out_refs[t].at[_chip_index(chip)]
                _remote(slot, slot, send_sems, recv_sems, 3 * t + j, (x, y, c)).wait_recv()
        for cp in sends:
            cp.wait_send()
        for cp in mine:
            cp.wait()

    return pl.pallas_call(
        body, name="scatter_partials", in_specs=[ANY] * n, out_specs=[ANY] * n,
        out_shape=[jax.ShapeDtypeStruct(p.shape, p.dtype) for p in ps],
        scratch_shapes=[pltpu.SemaphoreType.DMA((3 * n,)), pltpu.SemaphoreType.DMA((3 * n,)), pltpu.SemaphoreType.DMA((n,))],
    )(*ps)


def _share_half(fs):
    n = len(fs)

    def body(*refs):
        f_refs, out_refs = refs[:n], refs[n:2 * n]
        send_sems, recv_sems, local_sems = refs[2 * n:]
        x, y, c, _ = _place()
        mine = [pltpu.make_async_copy(f_refs[t], out_refs[t].at[c], local_sems.at[t]) for t in range(n)]
        sends = [_remote(f_refs[t], out_refs[t].at[c], send_sems, recv_sems, t, (x, y, 1 - c)) for t in range(n)]
        for cp in mine + sends:
            cp.start()
        for t in range(n):
            theirs = out_refs[t].at[1 - c]
            _remote(theirs, theirs, send_sems, recv_sems, t, (x, y, c)).wait_recv()
        for cp in sends:
            cp.wait_send()
        for cp in mine:
            cp.wait()

    return pl.pallas_call(
        body, name="share_half", in_specs=[ANY] * n, out_specs=[ANY] * n,
        out_shape=[jax.ShapeDtypeStruct((2,) + f.shape, f.dtype) for f in fs],
        scratch_shapes=[pltpu.SemaphoreType.DMA((n,)), pltpu.SemaphoreType.DMA((n,)), pltpu.SemaphoreType.DMA((n,))],
    )(*fs)


def _gather_small(v):
    def body(v_ref, out_ref, send_sems, recv_sems, local_sem):
        x, y, c, chips = _place()
        me, sibling = (x, y, c), (x, y, 1 - c)

        def slot(px, py, pc):
            return out_ref.at[4 * px + 2 * py + pc]

        mine = pltpu.make_async_copy(v_ref, slot(*me), local_sem)
        mine.start()
        first = [_remote(v_ref, slot(*me), send_sems, recv_sems, 0, sibling)]
        first += [_remote(v_ref, slot(*me), send_sems, recv_sems, 1 + j, (*chip, c)) for j, chip in enumerate(chips)]
        for cp in first:
            cp.start()
        passed = [_remote(slot(*chip, c), slot(*chip, c), send_sems, recv_sems, 4 + j, sibling)
                  for j, chip in enumerate(chips)]
        for j, chip in enumerate(chips):
            _remote(slot(*chip, c), slot(*chip, c), send_sems, recv_sems, 1 + j, me).wait_recv()
            passed[j].start()
        _remote(slot(*sibling), slot(*sibling), send_sems, recv_sems, 0, me).wait_recv()
        for j, chip in enumerate(chips):
            _remote(slot(*chip, 1 - c), slot(*chip, 1 - c), send_sems, recv_sems, 4 + j, me).wait_recv()
        for cp in first + passed:
            cp.wait_send()
        mine.wait()

    return pl.pallas_call(
        body, name="gather_small", in_specs=[ANY], out_specs=ANY,
        out_shape=jax.ShapeDtypeStruct((N_DEV,) + v.shape, v.dtype),
        scratch_shapes=[pltpu.SemaphoreType.DMA((7,)), pltpu.SemaphoreType.DMA((7,)), pltpu.SemaphoreType.DMA],
    )(v)


SMALL_ROWS = 256
SUM_ROWS = 256
BF16_ROWS = 16


def _row_tile(rows, want):
    for t in range(min(rows, want) // BF16_ROWS * BF16_ROWS, 0, -BF16_ROWS):
        if rows % t == 0:
            return t
    return rows


def _pair_sum(g, r, core, name):
    rows, cols = g.shape[2:]
    tr = _row_tile(rows, SUM_ROWS)

    def body(c_ref, g_ref, r_ref, o_ref):
        o_ref[0] = (g_ref[0, 0].astype(F32) + r_ref[0].astype(F32)).astype(o_ref.dtype)

    return pl.pallas_call(
        body, name=name,
        grid_spec=pltpu.PrefetchScalarGridSpec(
            num_scalar_prefetch=1, grid=(N_CHIPS, rows // tr),
            in_specs=[pl.BlockSpec((1, 1, tr, cols), lambda s, i, c_ref: (s, c_ref[0], i, 0)),
                      pl.BlockSpec((1, tr, cols), lambda s, i, c_ref: (s, i, 0))],
            out_specs=pl.BlockSpec((1, tr, cols), lambda s, i, c_ref: (s, i, 0))),
        out_shape=jax.ShapeDtypeStruct(r.shape, BF16),
        compiler_params=_params(("parallel", "parallel")))(core, g, r)


def _sum_blocks(a, name):
    n, rows, cols = a.shape
    tr = _row_tile(rows, SUM_ROWS)

    def body(a_ref, o_ref):
        acc = a_ref[0].astype(F32)
        for s in range(1, n):
            acc = acc + a_ref[s].astype(F32)
        o_ref[...] = acc

    return pl.pallas_call(
        body, name=name, grid=(rows // tr,),
        in_specs=[pl.BlockSpec((n, tr, cols), lambda i: (0, i, 0))],
        out_specs=pl.BlockSpec((tr, cols), lambda i: (i, 0)),
        out_shape=jax.ShapeDtypeStruct((rows, cols), F32),
        compiler_params=_params(("parallel",)))(a)


def _adamw(w, g, m, v, name):
    def fn(wb, gb, mb, vb):
        m2 = ADAM_B1 * mb + (1.0 - ADAM_B1) * gb
        v2 = ADAM_B2 * vb + (1.0 - ADAM_B2) * (gb * gb)
        m_hat = m2 / (1.0 - ADAM_B1 ** ADAM_STEP)
        v_hat = v2 / (1.0 - ADAM_B2 ** ADAM_STEP)
        delta = -ADAM_LR * (m_hat / (jnp.sqrt(v_hat) + ADAM_EPS) + ADAM_WD * wb)
        return (delta, m2, v2), ()
    c = w.shape[1]
    return _ew(fn, [w, g, m, v], outs=[(c, F32)] * 3, name=name)


WEIGHTS = ["mix_norm_g", "ffn_norm_g", "final_norm_g", "fox_w_in", "fox_b_f", "fox_w_out", "s5_w_in", "s5_a_re",
           "s5_a_im", "s5_log_dt", "s5_b_re", "s5_b_im", "s5_c_re", "s5_c_im", "s5_d", "s5_w_glu", "pool_w",
           "pool_b", "pool_scale", "ffn_w_gate_up", "ffn_w_down"]
BIG = {"fox_w_in": 2, "fox_w_out": 1, "s5_w_in": 1, "s5_w_glu": 2, "pool_w": 2, "ffn_w_gate_up": 2, "ffn_w_down": 1}
SLICED = ("pool_b", "pool_scale")
SMALL = [n for n in WEIGHTS if n not in BIG]


def _to_natural(cm, axis):
    moved = jnp.moveaxis(cm, 0, axis)
    shape = moved.shape[:axis] + (moved.shape[axis] * moved.shape[axis + 1],) + moved.shape[axis + 2:]
    return moved.reshape(shape)


def _to_chip_major(nat, axis):
    shape = nat.shape[:axis] + (N_CHIPS, nat.shape[axis] // N_CHIPS) + nat.shape[axis + 1:]
    return jnp.moveaxis(nat.reshape(shape), axis, 0)


def _halves_view(shape):
    return (2, int(np.prod(shape[:-1])) // 2, shape[-1])


def _pack_small(parts):
    flat = jnp.concatenate([p.reshape(-1).astype(F32) for p in parts])
    pad = (-flat.shape[0]) % (SMALL_ROWS * LANES)
    return jnp.pad(flat, (0, pad)).reshape(-1, LANES)


def _unpack_small(buf, shapes):
    flat = buf.reshape(-1)
    out, off = [], 0
    for shp in shapes:
        n = int(np.prod(shp))
        out.append(flat[off:off + n].reshape(shp))
        off += n
    return out


def _local_step(x, target, w):
    grads = {}
    mixers = ("fox", "s5", "pool")
    saved = []
    ssm, ssm_pull = jax.vjp(_s5_operands, w["s5_a_re"][0], w["s5_a_im"][0], w["s5_log_dt"][0], w["s5_b_re"][0],
                            w["s5_b_im"][0], w["s5_c_re"][0], w["s5_c_im"][0])
    fox_w = []
    for j in range(w["fox_w_in"].shape[0]):
        w_in = w["fox_w_in"][j]
        w_f = jnp.pad(w_in[:, 3 * D_MODEL:], ((0, 0), (0, LANES - FOX_HEADS)))
        fox_w.append((w_in[:, :3 * D_MODEL], w_f, w["fox_w_out"][j]))
    for i in range(DEPTH):
        kind, j = mixers[i % 3], i // 3
        gain = w["mix_norm_g"][i]
        if kind == "fox":
            x1, sv = _fox_fwd(x, gain, fox_w[j][0], fox_w[j][1], w["fox_b_f"][j], fox_w[j][2])
        elif kind == "s5":
            x1, sv = _s5_fwd(x, gain, w["s5_w_in"][j], ssm, w["s5_d"], w["s5_w_glu"][j])
        else:
            x1, sv = _pool_fwd(x, gain, w["pool_w"][j], w["pool_b"], w["pool_scale"])
        x, sf = _ffn_fwd(x1, w["ffn_norm_g"][i], w["ffn_w_gate_up"][i], w["ffn_w_down"][i])
        saved.append((sv, sf))
    loss, dx, grads["final_norm_g"] = _loss_head(x, w["final_norm_g"], target)
    per_layer = {n: [None] * DEPTH for n in ("mix_norm_g", "ffn_norm_g", "ffn_w_gate_up", "ffn_w_down")}
    fox_g = {n: [None] * len(fox_w) for n in ("fox_w_in", "fox_b_f", "fox_w_out")}
    for i in reversed(range(DEPTH)):
        kind, j = mixers[i % 3], i // 3
        sv, sf = saved[i]
        dx, per_layer["ffn_norm_g"][i], per_layer["ffn_w_gate_up"][i], per_layer["ffn_w_down"][i] = _ffn_bwd(
            dx, sf, w["ffn_norm_g"][i], w["ffn_w_gate_up"][i], w["ffn_w_down"][i])
        gain = w["mix_norm_g"][i]
        if kind == "fox":
            dx, per_layer["mix_norm_g"][i], fox_g["fox_w_in"][j], fox_g["fox_b_f"][j], fox_g["fox_w_out"][j] = _fox_bwd(
                dx, sv, gain, fox_w[j][0], fox_w[j][1], fox_w[j][2])
        elif kind == "s5":
            dx, per_layer["mix_norm_g"][i], dw_in, dssm, dd, dw_glu = _s5_bwd(
                dx, sv, gain, w["s5_w_in"][j], ssm, w["s5_d"], w["s5_w_glu"][j])
            grads["s5_w_in"], grads["s5_w_glu"], grads["s5_d"] = dw_in[None], dw_glu[None], dd
            for n, g in zip(("s5_a_re", "s5_a_im", "s5_log_dt", "s5_b_re", "s5_b_im", "s5_c_re", "s5_c_im"), ssm_pull(dssm)):
                grads[n] = g[None]
        else:
            dx, per_layer["mix_norm_g"][i], dw, db, dsc = _pool_bwd(dx, sv, gain, w["pool_w"][j], w["pool_b"], w["pool_scale"])
            grads["pool_w"], grads["pool_b"], grads["pool_scale"] = dw[None].astype(BF16), db, dsc
    for n, parts in {**per_layer, **fox_g}.items():
        grads[n] = jnp.stack(parts)
    return loss, dx, grads


def kernel(x, mix_norm_g, ffn_norm_g, final_norm_g, fox_w_in, fox_b_f, fox_w_out, s5_w_in, s5_a_re, s5_a_im, s5_log_dt, s5_b_re, s5_b_im, s5_c_re, s5_c_im, s5_d, s5_w_glu, pool_w, pool_b, pool_scale, ffn_w_gate_up, ffn_w_down, loss_target, m_mix_norm_g, m_ffn_norm_g, m_final_norm_g, m_fox_w_in, m_fox_b_f, m_fox_w_out, m_s5_w_in, m_s5_a_re, m_s5_a_im, m_s5_log_dt, m_s5_b_re, m_s5_b_im, m_s5_c_re, m_s5_c_im, m_s5_d, m_s5_w_glu, m_pool_w, m_pool_b, m_pool_scale, m_ffn_w_gate_up, m_ffn_w_down, v_mix_norm_g, v_ffn_norm_g, v_final_norm_g, v_fox_w_in, v_fox_b_f, v_fox_w_out, v_s5_w_in, v_s5_a_re, v_s5_a_im, v_s5_log_dt, v_s5_b_re, v_s5_b_im, v_s5_c_re, v_s5_c_im, v_s5_d, v_s5_w_glu, v_pool_w, v_pool_b, v_pool_scale, v_ffn_w_gate_up, v_ffn_w_down):
    given = dict(locals())
    shard = {n: given[n] for n in WEIGHTS}
    chip = 2 * lax.axis_index("x") + lax.axis_index("y")
    core = lax.axis_index("c")

    views = {n: _halves_view(shard[n].shape) for n in BIG}
    gathered = _gather_weights([shard[n].astype(MXU_DTYPE).reshape(views[n]) for n in BIG])
    whole = {}
    for n, by_chip in zip(BIG, gathered):
        whole[n] = _to_natural(by_chip.reshape((N_CHIPS,) + shard[n].shape), BIG[n])
    for n in SMALL:
        whole[n] = shard[n]
    sliced_shapes = [shard[n].shape for n in SLICED]
    by_chip = _gather_small(_pack_small([shard[n] for n in SLICED]))[0::2]
    slices = [_unpack_small(by_chip[k], sliced_shapes) for k in range(N_CHIPS)]
    for idx, n in enumerate(SLICED):
        whole[n] = jnp.concatenate([slices[k][idx] for k in range(N_CHIPS)], axis=-1)

    loss_part, dx, grads = _local_step(x[0], loss_target[0], whole)
    loss = lax.psum(loss_part, MESH_AXES)

    gs = [_to_chip_major(grads[n].astype(BF16), BIG[n]).reshape((N_CHIPS,) + views[n]) for n in BIG]
    core_id = core.reshape(1).astype(jnp.int32)
    partial = [_pair_sum(g, r, core_id, "pair_sum_" + n) for n, g, r in zip(BIG, gs, _swap_halves(gs))]
    summed = _share_half([_sum_blocks(p, "chip_sum_" + n) for n, p in zip(BIG, _scatter_partials(partial))])
    grad = {n: g.reshape(shard[n].shape) for n, g in zip(BIG, summed)}

    small_sum = _sum_blocks(_gather_small(_pack_small([grads[n] for n in SMALL])), "small_sum")
    for n, g in zip(SMALL, _unpack_small(small_sum, [whole[n].shape for n in SMALL])):
        grad[n] = g
    for n in SLICED:
        width = shard[n].shape[-1]
        grad[n] = lax.dynamic_slice_in_dim(grad[n], chip * width, width, axis=-1)

    delta, new_m, new_v = {}, {}, {}
    for n in BIG:
        view = (-1, shard[n].shape[-1])
        res = _adamw(shard[n].reshape(view), grad[n].reshape(view), given["m_" + n].reshape(view),
                     given["v_" + n].reshape(view), "adamw_" + n)
        delta[n], new_m[n], new_v[n] = (r.reshape(shard[n].shape) for r in res)
    small_shapes = [shard[n].shape for n in SMALL]
    res = _adamw(_pack_small([shard[n] for n in SMALL]), _pack_small([grad[n] for n in SMALL]),
                 _pack_small([given["m_" + n] for n in SMALL]), _pack_small([given["v_" + n] for n in SMALL]), "adamw_small")
    for out, buf in zip((delta, new_m, new_v), res):
        for n, a in zip(SMALL, _unpack_small(buf, small_shapes)):
            out[n] = a
    return (loss, dx[None], *[grad[n] for n in WEIGHTS], *[delta[n] for n in WEIGHTS],
            *[new_m[n] for n in WEIGHTS], *[new_v[n] for n in WEIGHTS])
```

```python
import functools
import math

import jax
import jax.numpy as jnp
import numpy as np
from jax import lax
from jax.experimental import pallas as pl
from jax.experimental.pallas import tpu as pltpu

F32 = jnp.float32
BF16 = jnp.bfloat16
MXU_DTYPE = jnp.bfloat16

D_MODEL = 1024
DEPTH = 4
EPS = 1e-6
FOX_HEADS = 16
FOX_HEAD_DIM = 64
HEAD_PAIRS = FOX_HEADS // 2
S5_GROUPS = 64
S5_GROUP = 16
S5_STATE = 64
S5_BLOCKS = 8
S5_HALF = 256
POOL_WINDOWS = (2, 4, 8, 16)
POOL_WIDTH = 256
D_FF = 2816
N_CHIPS = 4
N_DEV = 8
LANES = 128
SUBLANES = 8
VMEM_LIMIT = 56 * 1024 * 1024

ADAM_LR = 0.001
ADAM_B1 = 0.9
ADAM_B2 = 0.999
ADAM_EPS = 1e-08
ADAM_WD = 0.01
ADAM_STEP = 10

MESH_AXES = ("x", "y", "c")


def _tile(n, want):
    t = (min(n, want) // LANES) * LANES
    while t >= LANES:
        if n % t == 0:
            return t
        t -= LANES
    return n


def _params(sem=None):
    return pltpu.CompilerParams(dimension_semantics=sem, vmem_limit_bytes=VMEM_LIMIT)


def _mm(a, b, *, name, ta=False, tb=False, out_dtype=F32, add=None, tm=1024, tn=1024, tk=1024):
    m, k = (a.shape[1], a.shape[0]) if ta else a.shape
    n = b.shape[0] if tb else b.shape[1]
    assert (b.shape[1] if tb else b.shape[0]) == k, (a.shape, b.shape, ta, tb)
    tm, tn, tk = _tile(m, tm), _tile(n, tn), _tile(k, tk)
    nk = k // tk
    a_spec = pl.BlockSpec((tk, tm), lambda i, j, kk: (kk, i)) if ta else pl.BlockSpec((tm, tk), lambda i, j, kk: (i, kk))
    b_spec = pl.BlockSpec((tn, tk), lambda i, j, kk: (j, kk)) if tb else pl.BlockSpec((tk, tn), lambda i, j, kk: (kk, j))
    o_spec = pl.BlockSpec((tm, tn), lambda i, j, kk: (i, j))
    dims = (((0 if ta else 1,), (1 if tb else 0,)), ((), ()))
    has_add = add is not None

    def body(*refs):
        if has_add:
            a_ref, b_ref, add_ref, o_ref, acc_ref = refs
        else:
            a_ref, b_ref, o_ref, acc_ref = refs
        kk = pl.program_id(2)

        @pl.when(kk == 0)
        def _():
            acc_ref[...] = jnp.zeros_like(acc_ref)

        acc_ref[...] += lax.dot_general(a_ref[...].astype(MXU_DTYPE), b_ref[...].astype(MXU_DTYPE), dims,
                                        preferred_element_type=F32)

        @pl.when(kk == nk - 1)
        def _():
            r = acc_ref[...]
            if has_add:
                r = r + add_ref[...].astype(F32)
            o_ref[...] = r.astype(out_dtype)

    ins = [a, b] + ([add] if has_add else [])
    specs = [a_spec, b_spec] + ([o_spec] if has_add else [])
    return pl.pallas_call(
        body, name=name, grid=(m // tm, n // tn, nk), in_specs=specs, out_specs=o_spec,
        out_shape=jax.ShapeDtypeStruct((m, n), out_dtype), scratch_shapes=[pltpu.VMEM((tm, tn), F32)],
        compiler_params=_params(("parallel", "parallel", "arbitrary")))(*ins)


def _ew(fn, tens, vecs=(), *, outs=(), sums=(), name, tr=256):
    tens = [t if isinstance(t, tuple) else (t, t.shape[1], 0) for t in tens]
    rows = tens[0][0].shape[0]
    tr = min(tr, rows)
    n_t, n_v, n_o, n_s = len(tens), len(vecs), len(outs), len(sums)

    def body(*refs):
        i = pl.program_id(0)
        t_blocks = [r[...] for r in refs[:n_t]]
        v_blocks = [r[...] for r in refs[n_t:n_t + n_v]]
        o_refs = refs[n_t + n_v:n_t + n_v + n_o]
        s_refs = refs[n_t + n_v + n_o:]
        o_vals, s_vals = fn(*t_blocks, *v_blocks)
        for r, v in zip(o_refs, o_vals):
            r[...] = v.astype(r.dtype)
        if n_s:
            @pl.when(i == 0)
            def _():
                for r in s_refs:
                    r[...] = jnp.zeros_like(r)
            for r, v in zip(s_refs, s_vals):
                r[...] += jnp.sum(v.astype(F32), axis=0, keepdims=True)

    in_specs = [pl.BlockSpec((tr, w), functools.partial(lambda i, cb: (i, cb), cb=cb)) for _, w, cb in tens]
    in_specs += [pl.BlockSpec(v.shape, functools.partial(lambda i, nd: (0,) * nd, nd=v.ndim)) for v in vecs]
    out_specs = [pl.BlockSpec((tr, c), lambda i: (i, 0)) for c, _ in outs]
    out_specs += [pl.BlockSpec((1, c), lambda i: (0, 0)) for c in sums]
    out_shape = [jax.ShapeDtypeStruct((rows, c), dt) for c, dt in outs]
    out_shape += [jax.ShapeDtypeStruct((1, c), F32) for c in sums]
    res = pl.pallas_call(
        body, name=name, grid=(rows // tr,), in_specs=in_specs, out_specs=out_specs, out_shape=out_shape,
        compiler_params=_params(("arbitrary",)))(*[t[0] for t in tens], *vecs)
    return res


def _sigmoid(z):
    return 1.0 / (1.0 + jnp.exp(-z))


def _rms_fwd(x, g, name):
    def fn(xb, gb):
        r = lax.rsqrt(jnp.mean(xb * xb, axis=-1, keepdims=True) + EPS)
        return ((xb * r) * gb,), ()
    return _ew(fn, [x], [g.reshape(1, -1)], outs=[(x.shape[1], BF16)], name=name)[0]


def _rms_bwd(x, g, dh, dres, name):
    def fn(xb, dhb, drb, gb):
        r = lax.rsqrt(jnp.mean(xb * xb, axis=-1, keepdims=True) + EPS)
        xh = xb * r
        dhf = dhb.astype(F32)
        dy = dhf * gb
        dx = r * (dy - xh * jnp.mean(dy * xh, axis=-1, keepdims=True))
        return (drb + dx,), (dhf * xh,)
    dx, dg = _ew(fn, [x, dh, dres], [g.reshape(1, -1)], outs=[(x.shape[1], F32)], sums=[x.shape[1]], name=name)
    return dx, dg[0]


def _ffn_fwd(x1, gain, w_gu, w_down):
    h = _rms_fwd(x1, gain, "ffn_norm")
    gu = _mm(h, w_gu, name="ffn_gate_up", out_dtype=BF16, tn=D_FF // 2)

    def act_fn(gb, ub):
        gf, uf = gb.astype(F32), ub.astype(F32)
        return (gf * _sigmoid(gf) * uf,), ()
    act = _ew(act_fn, [(gu, D_FF, 0), (gu, D_FF, 1)], outs=[(D_FF, BF16)], name="ffn_act")[0]
    x2 = _mm(act, w_down, name="ffn_down", add=x1, tk=D_FF // 2)
    return x2, (x1, h, gu, act)


def _ffn_bwd(dx2, saved, gain, w_gu, w_down):
    x1, h, gu, act = saved
    dact = _mm(dx2, w_down, tb=True, name="ffn_dact", out_dtype=BF16, tn=D_FF // 2)
    dw_down = _mm(act, dx2, ta=True, name="ffn_dw_down", out_dtype=BF16, tm=D_FF // 2)

    def dgu_fn(db, gb, ub):
        df, gf, uf = db.astype(F32), gb.astype(F32), ub.astype(F32)
        sg = _sigmoid(gf)
        silu = gf * sg
        dg = df * uf * (sg * (1.0 + gf * (1.0 - sg)))
        du = df * silu
        return (jnp.concatenate([dg, du], axis=1),), ()
    dgu = _ew(dgu_fn, [dact, (gu, D_FF, 0), (gu, D_FF, 1)], outs=[(2 * D_FF, BF16)], name="ffn_dgu")[0]
    dw_gu = _mm(h, dgu, ta=True, name="ffn_dw_gu", out_dtype=BF16, tn=D_FF // 2)
    dh = _mm(dgu, w_gu, tb=True, name="ffn_dh", tk=512)
    dx1, dgain = _rms_bwd(x1, gain, dh, dx2, "ffn_norm_bwd")
    return dx1, dgain, dw_gu, dw_down


def _loss_head(x, gain, target):
    d = x.shape[1]

    def fn(xb, tb, gb):
        r = lax.rsqrt(jnp.mean(xb * xb, axis=-1, keepdims=True) + EPS)
        xh = xb * r
        y = xh * gb
        err = y - tb
        dyv = err * (1.0 / d)
        dyg = dyv * gb
        dx = r * (dyg - xh * jnp.mean(dyg * xh, axis=-1, keepdims=True))
        return (dx,), (0.5 * err * err * (1.0 / d), dyv * xh)
    dx, lsum, dg = _ew(fn, [x, target], [gain.reshape(1, -1)], outs=[(d, F32)], sums=[d, d], name="loss_head")
    return jnp.sum(lsum), dx, dg[0]


ATT_BLOCK = 256
CUM_BLOCK = 512
NEG_INF = -1e30


def _fox_gate_fwd(fl_row, b_col):
    nh, s = fl_row.shape
    tb = min(CUM_BLOCK, s)

    def body(fl_ref, b_ref, z_ref, c_ref):
        upper = (lax.broadcasted_iota(jnp.int32, (tb, tb), 0) <= lax.broadcasted_iota(jnp.int32, (tb, tb), 1)).astype(F32)
        carry = jnp.zeros((nh, 1), F32)
        for blk in range(s // tb):
            z = fl_ref[:, blk * tb:(blk + 1) * tb] + b_ref[...]
            logf = jnp.minimum(z, 0.0) - jnp.log(1.0 + jnp.exp(-jnp.abs(z)))
            cs = jnp.dot(logf, upper, precision=lax.Precision.HIGHEST, preferred_element_type=F32) + carry
            z_ref[:, blk * tb:(blk + 1) * tb] = z
            c_ref[:, blk * tb:(blk + 1) * tb] = cs
            carry = cs[:, tb - 1:tb]

    return pl.pallas_call(body, name="fox_gate_fwd", out_shape=[jax.ShapeDtypeStruct((nh, s), F32)] * 2,
                          compiler_params=_params())(fl_row, b_col)


def _fox_gate_bwd(dc_row, z_row):
    nh, s = dc_row.shape
    tb = min(CUM_BLOCK, s)

    def body(dc_ref, z_ref, dz_ref, db_ref):
        lower = (lax.broadcasted_iota(jnp.int32, (tb, tb), 0) >= lax.broadcasted_iota(jnp.int32, (tb, tb), 1)).astype(F32)
        carry = jnp.zeros((nh, 1), F32)
        db = jnp.zeros((nh, 1), F32)
        for blk in reversed(range(s // tb)):
            dc = dc_ref[:, blk * tb:(blk + 1) * tb]
            rs = jnp.dot(dc, lower, precision=lax.Precision.HIGHEST, preferred_element_type=F32) + carry
            dz = rs * _sigmoid(-z_ref[:, blk * tb:(blk + 1) * tb])
            dz_ref[:, blk * tb:(blk + 1) * tb] = dz
            db = db + jnp.sum(dz, axis=1, keepdims=True)
            carry = rs[:, 0:1]
        db_ref[...] = db

    return pl.pallas_call(body, name="fox_gate_bwd",
                          out_shape=[jax.ShapeDtypeStruct((nh, s), F32), jax.ShapeDtypeStruct((nh, 1), F32)],
                          compiler_params=_params())(dc_row, z_row)


def _head_masks(rows):
    lane = lax.broadcasted_iota(jnp.int32, (rows, LANES), 1)
    return lane < FOX_HEAD_DIM


def _attn_fwd(qkv, c_row):
    s = qkv.shape[0]
    t = min(ATT_BLOCK, s)
    nq = s // t
    scale = FOX_HEAD_DIM ** -0.5

    def body(q_ref, k_ref, v_ref, cr_ref, o_ref, lse_ref):
        i = pl.program_id(1)
        first = _head_masks(t)
        qs = q_ref[...] * scale
        zero = jnp.zeros_like(qs)
        qh = (jnp.where(first, qs, zero), jnp.where(first, zero, qs))
        causal = lax.broadcasted_iota(jnp.int32, (t, t), 0) >= lax.broadcasted_iota(jnp.int32, (t, t), 1)

        def block(j, carry, masked):
            ms, ls, acc = carry
            start = pl.multiple_of(j * t, t)
            ks, vs = k_ref[pl.ds(start, t), :], v_ref[pl.ds(start, t), :]
            new_m, new_l, alphas, pvs = [], [], [], []
            for hh in range(2):
                sc = lax.dot_general(qh[hh], ks, (((1,), (1,)), ((), ())), preferred_element_type=F32)
                sc = sc - cr_ref[0, hh:hh + 1, pl.ds(start, t)]
                if masked:
                    sc = jnp.where(causal, sc, NEG_INF)
                m_new = jnp.maximum(ms[hh], jnp.max(sc, axis=1, keepdims=True))
                p = jnp.exp(sc - m_new)
                alpha = jnp.exp(ms[hh] - m_new)
                new_m.append(m_new)
                new_l.append(alpha * ls[hh] + jnp.sum(p, axis=1, keepdims=True))
                alphas.append(alpha)
                p_hi = p.astype(MXU_DTYPE)
                p_lo = (p - p_hi.astype(F32)).astype(MXU_DTYPE)
                pvs.append(jnp.dot(p_hi, vs, preferred_element_type=F32) + jnp.dot(p_lo, vs, preferred_element_type=F32))
            acc = jnp.where(first, alphas[0], alphas[1]) * acc + jnp.where(first, pvs[0], pvs[1])
            return tuple(new_m), tuple(new_l), acc

        neg, nil = jnp.full((t, 1), NEG_INF, F32), jnp.zeros((t, 1), F32)
        carry = lax.fori_loop(0, i, functools.partial(block, masked=False), ((neg, neg), (nil, nil), jnp.zeros((t, LANES), F32)))
        ms, ls, acc = block(i, carry, True)
        o_ref[...] = (acc / jnp.where(first, ls[0], ls[1])).astype(o_ref.dtype)
        lse_ref[0] = jnp.concatenate([ms[0] + jnp.log(ls[0]), ms[1] + jnp.log(ls[1])], axis=1)

    np_ = HEAD_PAIRS
    return pl.pallas_call(
        body, name="fox_attn_fwd", grid=(np_, nq),
        in_specs=[pl.BlockSpec((t, LANES), lambda p, i: (i, p)),
                  pl.BlockSpec((s, LANES), lambda p, i: (0, np_ + p)),
                  pl.BlockSpec((s, LANES), lambda p, i: (0, 2 * np_ + p)),
                  pl.BlockSpec((1, 2, s), lambda p, i: (p, 0, 0))],
        out_specs=[pl.BlockSpec((t, LANES), lambda p, i: (i, p)),
                   pl.BlockSpec((1, t, 2), lambda p, i: (p, i, 0))],
        out_shape=[jax.ShapeDtypeStruct((s, D_MODEL), F32), jax.ShapeDtypeStruct((np_, s, 2), F32)],
        compiler_params=_params(("parallel", "arbitrary")))(qkv, qkv, qkv, c_row)


def _attn_bwd(qkv, do, lse, delta, c_row):
    s = qkv.shape[0]
    t = min(ATT_BLOCK, s)
    nb = s // t
    scale = FOX_HEAD_DIM ** -0.5
    np_ = HEAD_PAIRS

    def body(q_ref, k_ref, v_ref, do_ref, lse_ref, dl_ref, cr_ref, dq_ref, dk_ref, dv_ref, dc_ref):
        j = pl.program_id(1)
        first = _head_masks(t)
        causal = lax.broadcasted_iota(jnp.int32, (t, t), 0) >= lax.broadcasted_iota(jnp.int32, (t, t), 1)
        kb = k_ref[...]
        vb = v_ref[...]

        @pl.when(j == 0)
        def _():
            dq_ref[...] = jnp.zeros_like(dq_ref)

        def step(i, carry, masked):
            dk_acc, dv_acc, dc_accs = carry
            rows = pl.ds(pl.multiple_of(i * t, t), t)
            qs = q_ref[rows, :] * scale
            dob = do_ref[rows, :]
            zero = jnp.zeros_like(qs)
            dks, dvs, dqs, dcs = [], [], [], []
            for hh in range(2):
                qh = jnp.where(first, qs, zero) if hh == 0 else jnp.where(first, zero, qs)
                doh = jnp.where(first, dob, zero) if hh == 0 else jnp.where(first, zero, dob)
                sc = lax.dot_general(qh, kb, (((1,), (1,)), ((), ())), preferred_element_type=F32)
                p = jnp.exp(sc - cr_ref[0, hh:hh + 1, :] - lse_ref[0, rows, hh:hh + 1])
                if masked:
                    p = jnp.where(causal, p, 0.0)
                dp = lax.dot_general(doh, vb, (((1,), (1,)), ((), ())), preferred_element_type=F32)
                ds = p * (dp - dl_ref[0, rows, hh:hh + 1])
                pb, dsb = p.astype(MXU_DTYPE), ds.astype(MXU_DTYPE)
                dvs.append(lax.dot_general(pb, dob, (((0,), (0,)), ((), ())), preferred_element_type=F32))
                dks.append(lax.dot_general(dsb, qs, (((0,), (0,)), ((), ())), preferred_element_type=F32))
                dqs.append(jnp.dot(dsb, kb, preferred_element_type=F32))
                dcs.append(dc_accs[hh] - jnp.sum(ds, axis=0, keepdims=True))
            dq_ref[rows, :] += jnp.where(first, dqs[0], dqs[1]) * scale
            return (dk_acc + jnp.where(first, dks[0], dks[1]), dv_acc + jnp.where(first, dvs[0], dvs[1]), tuple(dcs))

        nil = jnp.zeros((1, t), F32)
        carry = step(j, (jnp.zeros((t, LANES), F32), jnp.zeros((t, LANES), F32), (nil, nil)), True)
        dk_acc, dv_acc, dc_accs = lax.fori_loop(j + 1, nb, functools.partial(step, masked=False), carry)
        dk_ref[...] = dk_acc.astype(dk_ref.dtype)
        dv_ref[...] = dv_acc.astype(dv_ref.dtype)
        dc_ref[0] = jnp.concatenate(dc_accs, axis=0)

    return pl.pallas_call(
        body, name="fox_attn_bwd", grid=(np_, nb),
        in_specs=[pl.BlockSpec((s, LANES), lambda p, j: (0, p)),
                  pl.BlockSpec((t, LANES), lambda p, j: (j, np_ + p)),
                  pl.BlockSpec((t, LANES), lambda p, j: (j, 2 * np_ + p)),
                  pl.BlockSpec((s, LANES), lambda p, j: (0, p)),
                  pl.BlockSpec((1, s, 2), lambda p, j: (p, 0, 0)),
                  pl.BlockSpec((1, s, 2), lambda p, j: (p, 0, 0)),
                  pl.BlockSpec((1, 2, t), lambda p, j: (p, 0, j))],
        out_specs=[pl.BlockSpec((s, LANES), lambda p, j: (0, p)),
                   pl.BlockSpec((t, LANES), lambda p, j: (j, p)),
                   pl.BlockSpec((t, LANES), lambda p, j: (j, p)),
                   pl.BlockSpec((1, 2, t), lambda p, j: (p, 0, j))],
        out_shape=[jax.ShapeDtypeStruct((s, D_MODEL), F32), jax.ShapeDtypeStruct((s, D_MODEL), BF16),
                   jax.ShapeDtypeStruct((s, D_MODEL), BF16), jax.ShapeDtypeStruct((np_, 2, s), F32)],
        compiler_params=_params(("parallel", "arbitrary")))(qkv, qkv, qkv, do, lse, delta, c_row)


def _lanes2(a):
    return jnp.concatenate([a, a], axis=1)


def _attn_fwd_t(qkv, c_lanes):
    s = qkv.shape[0]
    t = min(ATT_BLOCK, s)
    nq = s // t
    scale = FOX_HEAD_DIM ** -0.5
    np_ = HEAD_PAIRS
    nt = (((1,), (1,)), ((), ()))

    def body(q_ref, k_ref, v_ref, c_ref, o_ref, lse_ref, vt_ref):
        i = pl.program_id(1)

        @pl.when(i == 0)
        def _():
            for r in range(s // t):
                vt_ref[:, r * t:(r + 1) * t] = v_ref[r * t:(r + 1) * t, :].astype(F32).T.astype(vt_ref.dtype)

        first = _head_masks(t)
        upper = lax.broadcasted_iota(jnp.int32, (LANES, t), 0) < FOX_HEAD_DIM
        qs = q_ref[...] * scale
        zero = jnp.zeros_like(qs)
        qh = (jnp.where(first, qs, zero), jnp.where(first, zero, qs))
        causal = lax.broadcasted_iota(jnp.int32, (t, t), 0) <= lax.broadcasted_iota(jnp.int32, (t, t), 1)

        def block(j, carry, masked):
            ms, ls, acc = carry
            start = pl.multiple_of(j * t, t)
            kb, vt = k_ref[pl.ds(start, t), :], vt_ref[:, pl.ds(start, t)]
            new_m, new_l, alphas, pvs = [], [], [], []
            for hh in range(2):
                sc = lax.dot_general(kb, qh[hh], nt, preferred_element_type=F32) - _lanes2(c_ref[0, hh, pl.ds(start, t), :])
                if masked:
                    sc = jnp.where(causal, sc, NEG_INF)
                m_new = jnp.maximum(ms[hh], jnp.max(sc, axis=0, keepdims=True))
                p = jnp.exp(sc - m_new)
                alpha = jnp.exp(ms[hh] - m_new)
                new_m.append(m_new)
                new_l.append(alpha * ls[hh] + jnp.sum(p, axis=0, keepdims=True))
                alphas.append(alpha)
                p_hi = p.astype(MXU_DTYPE)
                p_lo = (p - p_hi.astype(F32)).astype(MXU_DTYPE)
                pvs.append(jnp.dot(vt, p_hi, preferred_element_type=F32) + jnp.dot(vt, p_lo, preferred_element_type=F32))
            acc = jnp.where(upper, alphas[0], alphas[1]) * acc + jnp.where(upper, pvs[0], pvs[1])
            return tuple(new_m), tuple(new_l), acc

        neg, nil = jnp.full((1, t), NEG_INF, F32), jnp.zeros((1, t), F32)
        carry = lax.fori_loop(0, i, functools.partial(block, masked=False), ((neg, neg), (nil, nil), jnp.zeros((LANES, t), F32)))
        ms, ls, acc = block(i, carry, True)
        o_ref[...] = (acc / jnp.where(upper, ls[0], ls[1])).T.astype(o_ref.dtype)
        lse_ref[0] = jnp.concatenate([ms[0] + jnp.log(ls[0]), ms[1] + jnp.log(ls[1])], axis=0)

    return pl.pallas_call(
        body, name="fox_attn_fwd", grid=(np_, nq),
        in_specs=[pl.BlockSpec((t, LANES), lambda p, i: (i, p)),
                  pl.BlockSpec((s, LANES), lambda p, i: (0, np_ + p)),
                  pl.BlockSpec((s, LANES), lambda p, i: (0, 2 * np_ + p)),
                  pl.BlockSpec((1, 2, s, LANES), lambda p, i: (p, 0, 0, 0))],
        out_specs=[pl.BlockSpec((t, LANES), lambda p, i: (i, p)),
                   pl.BlockSpec((1, 2, t), lambda p, i: (p, 0, i))],
        out_shape=[jax.ShapeDtypeStruct((s, D_MODEL), F32), jax.ShapeDtypeStruct((np_, 2, s), F32)],
        scratch_shapes=[pltpu.VMEM((LANES, s), MXU_DTYPE)],
        compiler_params=_params(("parallel", "arbitrary")))(qkv, qkv, qkv, c_lanes)


def _attn_bwd_t(qkv, do, lse, delta, c_lanes):
    s = qkv.shape[0]
    t = min(ATT_BLOCK, s)
    nb = s // t
    scale = FOX_HEAD_DIM ** -0.5
    np_ = HEAD_PAIRS
    nt = (((1,), (1,)), ((), ()))

    def body(q_ref, k_ref, v_ref, do_ref, lse_ref, dl_ref, c_ref, dq_ref, dk_ref, dv_ref, dc_ref, dqt_ref):
        j = pl.program_id(1)
        first = _head_masks(t)
        upper = lax.broadcasted_iota(jnp.int32, (LANES, t), 0) < FOX_HEAD_DIM
        causal = lax.broadcasted_iota(jnp.int32, (t, t), 0) <= lax.broadcasted_iota(jnp.int32, (t, t), 1)
        kb, vb = k_ref[...], v_ref[...]
        kt = kb.astype(F32).T.astype(MXU_DTYPE)
        cb =(_lanes2(c_ref[0, 0]), _lanes2(c_ref[0, 1]))

        @pl.when(j == 0)
        def _():
            dqt_ref[...] = jnp.zeros_like(dqt_ref)

        def step(i, carry, masked):
            dk_acc, dv_acc, dc_accs = carry
            start = pl.multiple_of(i * t, t)
            qs = q_ref[pl.ds(start, t), :] * scale
            dob = do_ref[pl.ds(start, t), :]
            zero = jnp.zeros_like(qs)
            dks, dvs, dqs, dcs = [], [], [], []
            for hh in range(2):
                qh = jnp.where(first, qs, zero) if hh == 0 else jnp.where(first, zero, qs)
                doh = jnp.where(first, dob, zero) if hh == 0 else jnp.where(first, zero, dob)
                sc = lax.dot_general(kb, qh, nt, preferred_element_type=F32)
                p = jnp.exp(sc - cb[hh] - lse_ref[0, hh:hh + 1, pl.ds(start, t)])
                if masked:
                    p = jnp.where(causal, p, 0.0)
                dp = lax.dot_general(vb, doh, nt, preferred_element_type=F32)
                ds = p * (dp - dl_ref[0, hh:hh + 1, pl.ds(start, t)])
                pb, dsb = p.astype(MXU_DTYPE), ds.astype(MXU_DTYPE)
                dvs.append(jnp.dot(pb, dob, preferred_element_type=F32))
                dks.append(jnp.dot(dsb, qs, preferred_element_type=F32))
                dqs.append(jnp.dot(kt, dsb, preferred_element_type=F32))
                dcs.append(dc_accs[hh] - jnp.sum(ds, axis=1, keepdims=True))
            dqt_ref[:, pl.ds(start, t)] += jnp.where(upper, dqs[0], dqs[1]) * scale
            return (dk_acc + jnp.where(first, dks[0], dks[1]), dv_acc + jnp.where(first, dvs[0], dvs[1]), tuple(dcs))

        nil, col = jnp.zeros((t, LANES), F32), jnp.zeros((t, 1), F32)
        carry = step(j, (nil, nil, (col, col)), True)
        dk_acc, dv_acc, dc_accs = lax.fori_loop(j + 1, nb, functools.partial(step, masked=False), carry)
        dk_ref[...] = dk_acc.astype(dk_ref.dtype)
        dv_ref[...] = dv_acc.astype(dv_ref.dtype)
        dc_ref[0, 0] = jnp.broadcast_to(dc_accs[0], (t, LANES))
        dc_ref[0, 1] = jnp.broadcast_to(dc_accs[1], (t, LANES))

        @pl.when(j == nb - 1)
        def _():
            for r in range(nb):
                dq_ref[r * t:(r + 1) * t, :] = dqt_ref[:, r * t:(r + 1) * t].T

    row = pl.BlockSpec((1, 2, s), lambda p, j: (p, 0, 0))
    return pl.pallas_call(
        body, name="fox_attn_bwd", grid=(np_, nb),
        in_specs=[pl.BlockSpec((s, LANES), lambda p, j: (0, p)),
                  pl.BlockSpec((t, LANES), lambda p, j: (j, np_ + p)),
                  pl.BlockSpec((t, LANES), lambda p, j: (j, 2 * np_ + p)),
                  pl.BlockSpec((s, LANES), lambda p, j: (0, p)), row, row,
                  pl.BlockSpec((1, 2, t, LANES), lambda p, j: (p, 0, j, 0))],
        out_specs=[pl.BlockSpec((s, LANES), lambda p, j: (0, p)),
                   pl.BlockSpec((t, LANES), lambda p, j: (j, p)),
                   pl.BlockSpec((t, LANES), lambda p, j: (j, p)),
                   pl.BlockSpec((1, 2, t, LANES), lambda p, j: (p, 0, j, 0))],
        out_shape=[jax.ShapeDtypeStruct((s, D_MODEL), F32), jax.ShapeDtypeStruct((s, D_MODEL), BF16),
                   jax.ShapeDtypeStruct((s, D_MODEL), BF16), jax.ShapeDtypeStruct((np_, 2, s, LANES), F32)],
        scratch_shapes=[pltpu.VMEM((LANES, s), F32)],
        compiler_params=_params(("parallel", "arbitrary")))(qkv, qkv, qkv, do, lse, delta, c_lanes)


def _head_sums(a, b, name):
    d = a.shape[1]
    sel = (jnp.arange(d)[:, None] // FOX_HEAD_DIM == jnp.arange(LANES)[None, :]).astype(F32)

    def fn(ab, bb, selb):
        prod = ab.astype(F32) * bb.astype(F32)
        return (jnp.dot(prod, selb, precision=lax.Precision.HIGHEST, preferred_element_type=F32),), ()
    return _ew(fn, [a, b], [sel], outs=[(LANES, F32)], name=name)[0]


def _pairs_col(a16):
    s = a16.shape[0]
    return a16.reshape(s, HEAD_PAIRS, 2).transpose(1, 0, 2)


def _fox_fwd(x, gain, w_qkv, w_f, b_f, w_out):
    s = x.shape[0]
    h = _rms_fwd(x, gain, "mix_norm")
    qkv = _mm(h, w_qkv, name="fox_qkv", out_dtype=BF16)
    fl = _mm(h, w_f, name="fox_f", tn=LANES)
    z_row, c_rowf = _fox_gate_fwd(fl[:, :FOX_HEADS].T, b_f.reshape(FOX_HEADS, 1))
    c_lanes = jnp.broadcast_to(c_rowf.reshape(HEAD_PAIRS, 2, s, 1), (HEAD_PAIRS, 2, s, LANES))
    o, lse = _attn_fwd_t(qkv, c_lanes)
    x1 = _mm(o, w_out, name="fox_out", add=x)
    return x1, (x, h, qkv, z_row, c_lanes, o, lse)


def _fox_bwd(dx1, saved, gain, w_qkv, w_f, w_out):
    x, h, qkv, z_row, c_lanes, o, lse = saved
    s = x.shape[0]
    do = _mm(dx1, w_out, tb=True, name="fox_do", out_dtype=BF16)
    dw_out = _mm(o, dx1, ta=True, name="fox_dw_out", out_dtype=BF16)
    delta = _head_sums(do, o, "fox_delta")[:, :FOX_HEADS].T.reshape(HEAD_PAIRS, 2, s)
    dq, dk, dv, dc = _attn_bwd_t(qkv, do, lse, delta, c_lanes)
    dz_row, db = _fox_gate_bwd(dc[..., 0].reshape(FOX_HEADS, s), z_row)
    dqkv = jnp.concatenate([dq.astype(BF16), dk, dv], axis=1)
    dfl = jnp.pad(dz_row.T, ((0, 0), (0, LANES - FOX_HEADS))).astype(BF16)
    dw_qkv = _mm(h, dqkv, ta=True, name="fox_dw_qkv", out_dtype=BF16)
    dw_f = _mm(h, dfl, ta=True, name="fox_dw_f", out_dtype=BF16, tn=LANES)
    dh = _mm(dqkv, w_qkv, tb=True, name="fox_dh_qkv")
    dh = _mm(dfl, w_f, tb=True, name="fox_dh_f", add=dh)
    dx, dgain = _rms_bwd(x, gain, dh, dx1, "mix_norm_bwd")
    dw_in = jnp.concatenate([dw_qkv, dw_f[:, :FOX_HEADS]], axis=1)
    return dx, dgain, dw_in, db.reshape(FOX_HEADS), dw_out


S5_ROWS = 512
SCAN_CHUNKS = SUBLANES


def _s5_operands(a_re, a_im, log_dt, b_re, b_im, c_re, c_im):
    dt = jnp.exp(log_dt)[:, None]
    mag, ang = jnp.exp(a_re * dt), a_im * dt
    lr, li = mag * jnp.cos(ang), mag * jnp.sin(ang)
    den = a_re * a_re + a_im * a_im
    cr = ((lr - 1.0) * a_re + li * a_im) / den
    ci = (li * a_re - (lr - 1.0) * a_im) / den
    bbr = cr[..., None] * b_re - ci[..., None] * b_im
    bbi = cr[..., None] * b_im + ci[..., None] * b_re
    nb = S5_BLOCKS
    lam = jnp.stack([lr.reshape(nb, 2, S5_HALF), li.reshape(nb, 2, S5_HALF)], axis=2)
    eye4, eye2 = jnp.eye(4, dtype=F32), jnp.eye(2, dtype=F32)
    bb = jnp.stack([bbr, bbi], axis=0).reshape(2, nb, 2, 4, S5_STATE, S5_GROUP)
    bmat = jnp.einsum("rbhgpc,kg,jh->bhjkcrgp", bb, eye4, eye2).reshape(nb, 2, 128, 2 * S5_HALF)
    cc = jnp.stack([c_re, -c_im], axis=0).reshape(2, nb, 2, 4, S5_GROUP, S5_STATE)
    cmat = jnp.einsum("rbhgcp,kg,jh->bhrgpjkc", cc, eye4, eye2).reshape(nb, 2, 2 * S5_HALF, 128)
    return lam, bmat, cmat


def _time_to_scan_order(a):
    s, d = a.shape
    return a.reshape(SCAN_CHUNKS, s // SCAN_CHUNKS, d).transpose(1, 0, 2).reshape(s, d)


def _scan_to_time_order(a):
    s, d = a.shape
    return a.reshape(s // SCAN_CHUNKS, SCAN_CHUNKS, d).transpose(1, 0, 2).reshape(s, d)


def _scan_chunks(xr_ref, xi_ref, lr, li, nst, reverse, after_step=None):
    lanes = lr.shape[1]
    lr8, li8 = jnp.broadcast_to(lr, (SUBLANES, lanes)), jnp.broadcast_to(li, (SUBLANES, lanes))
    zero8 = jnp.zeros((SUBLANES, lanes), F32)

    def rows_of(n):
        s = (nst - 1 - n) if reverse else n
        return s, pl.ds(pl.multiple_of(s * SUBLANES, SUBLANES), SUBLANES)

    def local(n, carry):
        pr, pi = carry
        _, rows = rows_of(n)
        nr = lr8 * pr - li8 * pi + xr_ref[rows, :]
        ni = lr8 * pi + li8 * pr + xi_ref[rows, :]
        xr_ref[rows, :] = nr
        xi_ref[rows, :] = ni
        return nr, ni

    er, ei = lax.fori_loop(0, nst, local, (zero8, zero8))
    pr, pi = lr, li
    for _ in range(int(math.log2(nst))):
        pr, pi = pr * pr - pi * pi, 2.0 * pr * pi
    tr = ti = jnp.zeros((1, lanes), F32)
    ent_r, ent_i = [None] * SCAN_CHUNKS, [None] * SCAN_CHUNKS
    for k in (reversed(range(SCAN_CHUNKS)) if reverse else range(SCAN_CHUNKS)):
        ent_r[k], ent_i[k] = tr, ti
        tr, ti = er[k:k + 1] + (pr * tr - pi * ti), ei[k:k + 1] + (pr * ti + pi * tr)
    in_r, in_i = jnp.concatenate(ent_r, axis=0), jnp.concatenate(ent_i, axis=0)

    def fix(n, carry):
        wr, wi = carry
        s, rows = rows_of(n)
        nr = xr_ref[rows, :] + (wr * in_r - wi * in_i)
        ni = xi_ref[rows, :] + (wr * in_i + wi * in_r)
        xr_ref[rows, :] = nr
        xi_ref[rows, :] = ni
        if after_step is not None:
            after_step(s, nr, ni)
        return wr * lr8 - wi * li8, wr * li8 + wi * lr8

    lax.fori_loop(0, nst, fix, (lr8, li8))
    return in_r, in_i


def _s5_fill_states(u_ref, bm, xr_ref, xi_ref, s):
    rc = min(S5_ROWS, s)

    def fill(r, _):
        rows = pl.ds(pl.multiple_of(r * rc, rc), rc)
        bu = jnp.dot(u_ref[rows, :].astype(MXU_DTYPE), bm, preferred_element_type=F32)
        xr_ref[rows, :] = bu[:, :S5_HALF]
        xi_ref[rows, :] = bu[:, S5_HALF:]
        return 0
    lax.fori_loop(0, s // rc, fill, 0)


def _s5_specs():
    return [pl.BlockSpec((1, 2, 2, S5_HALF), lambda b: (b, 0, 0, 0)),
            pl.BlockSpec((1, 2, 128, 2 * S5_HALF), lambda b: (b, 0, 0, 0)),
            pl.BlockSpec((1, 2, 2 * S5_HALF, 128), lambda b: (b, 0, 0, 0)),
            pl.BlockSpec((1, LANES), lambda b: (0, b))]


def _s5_scan_fwd(u, lam, bmat, cmat, dvec):
    s = u.shape[0]
    nst = s // SCAN_CHUNKS
    rc = min(S5_ROWS, s)

    def body(u_ref, lam_ref, b_ref, c_ref, d_ref, y_ref, xr_ref, xi_ref):
        y_ref[...] = u_ref[...] * d_ref[...]
        for hb in range(2):
            _s5_fill_states(u_ref, b_ref[0, hb], xr_ref, xi_ref, s)
            _scan_chunks(xr_ref, xi_ref, lam_ref[0, hb, 0:1, :], lam_ref[0, hb, 1:2, :], nst, False)
            cm = c_ref[0, hb]

            def emit(r, _, cm=cm):
                rows = pl.ds(pl.multiple_of(r * rc, rc), rc)
                y_ref[rows, :] += (jnp.dot(xr_ref[rows, :].astype(MXU_DTYPE), cm[:S5_HALF], preferred_element_type=F32)
                                   + jnp.dot(xi_ref[rows, :].astype(MXU_DTYPE), cm[S5_HALF:], preferred_element_type=F32))
                return 0
            lax.fori_loop(0, s // rc, emit, 0)

    blk = pl.BlockSpec((s, LANES), lambda b: (0, b))
    return pl.pallas_call(
        body, name="s5_scan_fwd", grid=(S5_BLOCKS,), in_specs=[blk] + _s5_specs(), out_specs=blk,
        out_shape=jax.ShapeDtypeStruct(u.shape, F32),
        scratch_shapes=[pltpu.VMEM((s, S5_HALF), F32)] * 2,
        compiler_params=_params(("parallel",)))(u, lam, bmat, cmat, dvec)


def _s5_scan_bwd(u, dy, lam, bmat, cmat, dvec):
    s = u.shape[0]
    nst = s // SCAN_CHUNKS
    rc = min(S5_ROWS, s)
    nt = (((1,), (1,)), ((), ()))
    tn = (((0,), (0,)), ((), ()))

    def body(u_ref, dy_ref, lam_ref, b_ref, c_ref, d_ref, du_ref, db_ref, dc_ref, dl_ref, dd_ref,
             xr_ref, xi_ref, gr_ref, gi_ref, ar_ref, ai_ref):
        du_ref[...] = dy_ref[...] * d_ref[...]
        dd_ref[...] = jnp.sum(dy_ref[...] * u_ref[...], axis=0, keepdims=True)
        db_ref[...] = jnp.zeros_like(db_ref)
        dc_ref[...] = jnp.zeros_like(dc_ref)
        for hb in range(2):
            bm, cm = b_ref[0, hb], c_ref[0, hb]
            lr, li = lam_ref[0, hb, 0:1, :], lam_ref[0, hb, 1:2, :]
            _s5_fill_states(u_ref, bm, xr_ref, xi_ref, s)
            xin_r, xin_i = _scan_chunks(xr_ref, xi_ref, lr, li, nst, False)

            def fill_g(r, _, cm=cm):
                rows = pl.ds(pl.multiple_of(r * rc, rc), rc)
                g = lax.dot_general(dy_ref[rows, :].astype(MXU_DTYPE), cm, nt, preferred_element_type=F32)
                gr_ref[rows, :] = g[:, :S5_HALF]
                gi_ref[rows, :] = g[:, S5_HALF:]
                return 0
            lax.fori_loop(0, s // rc, fill_g, 0)
            ar_ref[...] = jnp.zeros_like(ar_ref)
            ai_ref[...] = jnp.zeros_like(ai_ref)

            def lam_grad(st, g_r, g_i, xin_r=xin_r, xin_i=xin_i):
                prev = pl.ds(pl.multiple_of(jnp.maximum(st - 1, 0) * SUBLANES, SUBLANES), SUBLANES)
                x_r = jnp.where(st > 0, xr_ref[prev, :], xin_r)
                x_i = jnp.where(st > 0, xi_ref[prev, :], xin_i)
                ar_ref[...] += g_r * x_r + g_i * x_i
                ai_ref[...] += g_i * x_r - g_r * x_i

            _scan_chunks(gr_ref, gi_ref, lr, -li, nst, True, after_step=lam_grad)
            dl_ref[0, hb] = jnp.concatenate([jnp.sum(ar_ref[...], axis=0, keepdims=True),
                                             jnp.sum(ai_ref[...], axis=0, keepdims=True)], axis=0)

            def emit(r, _, bm=bm, hb=hb):
                rows = pl.ds(pl.multiple_of(r * rc, rc), rc)
                g = jnp.concatenate([gr_ref[rows, :], gi_ref[rows, :]], axis=1).astype(MXU_DTYPE)
                x = jnp.concatenate([xr_ref[rows, :], xi_ref[rows, :]], axis=1).astype(MXU_DTYPE)
                du_ref[rows, :] += lax.dot_general(g, bm, nt, preferred_element_type=F32)
                db_ref[0, hb] += lax.dot_general(u_ref[rows, :].astype(MXU_DTYPE), g, tn, preferred_element_type=F32)
                dc_ref[0, hb] += lax.dot_general(dy_ref[rows, :].astype(MXU_DTYPE), x, tn, preferred_element_type=F32)
                return 0
            lax.fori_loop(0, s // rc, emit, 0)

    blk = pl.BlockSpec((s, LANES), lambda b: (0, b))
    mat = pl.BlockSpec((1, 2, 128, 2 * S5_HALF), lambda b: (b, 0, 0, 0))
    return pl.pallas_call(
        body, name="s5_scan_bwd", grid=(S5_BLOCKS,), in_specs=[blk, blk] + _s5_specs(),
        out_specs=[blk, mat, mat, pl.BlockSpec((1, 2, 2, S5_HALF), lambda b: (b, 0, 0, 0)),
                   pl.BlockSpec((1, LANES), lambda b: (0, b))],
        out_shape=[jax.ShapeDtypeStruct(u.shape, F32),
                   jax.ShapeDtypeStruct((S5_BLOCKS, 2, 128, 2 * S5_HALF), F32),
                   jax.ShapeDtypeStruct((S5_BLOCKS, 2, 128, 2 * S5_HALF), F32),
                   jax.ShapeDtypeStruct((S5_BLOCKS, 2, 2, S5_HALF), F32),
                   jax.ShapeDtypeStruct((1, D_MODEL), F32)],
        scratch_shapes=[pltpu.VMEM((s, S5_HALF), F32)] * 4 + [pltpu.VMEM((SUBLANES, S5_HALF), F32)] * 2,
        compiler_params=_params(("parallel",)))(u, dy, lam, bmat, cmat, dvec)


_GELU_C = math.sqrt(2.0 / math.pi)


def _gelu_parts(y):
    inner = _GELU_C * (y + 0.044715 * y * y * y)
    th = jnp.tanh(inner)
    return 0.5 * y * (1.0 + th), th


def _s5_fwd(x, gain, w_in, ssm, dvec, w_glu):
    lam, bmat, cmat = ssm
    h = _rms_fwd(x, gain, "mix_norm")
    u = _mm(h, w_in, name="s5_in")
    y = _scan_to_time_order(_s5_scan_fwd(_time_to_scan_order(u), lam, bmat.astype(MXU_DTYPE), cmat.astype(MXU_DTYPE), dvec))
    g = _ew(lambda yb: ((_gelu_parts(yb)[0],), ()), [y], outs=[(D_MODEL, BF16)], name="s5_gelu")[0]
    vg = _mm(g, w_glu, name="s5_glu", out_dtype=BF16)

    def glu_fn(vb, gb, xb):
        return (xb + vb.astype(F32) * _sigmoid(gb.astype(F32)),), ()
    x1 = _ew(glu_fn, [(vg, D_MODEL, 0), (vg, D_MODEL, 1), x], outs=[(D_MODEL, F32)], name="s5_gate")[0]
    return x1, (x, h, u, y, g, vg)


def _s5_bwd(dx1, saved, gain, w_in, ssm, dvec, w_glu):
    x, h, u, y, g, vg = saved
    lam, bmat, cmat = ssm

    def dglu_fn(db, vb, gb):
        vf, sg = vb.astype(F32), _sigmoid(gb.astype(F32))
        return (jnp.concatenate([db * sg, db * vf * sg * (1.0 - sg)], axis=1),), ()
    dvg = _ew(dglu_fn, [dx1, (vg, D_MODEL, 0), (vg, D_MODEL, 1)], outs=[(2 * D_MODEL, BF16)], name="s5_dgate")[0]
    dw_glu = _mm(g, dvg, ta=True, name="s5_dw_glu", out_dtype=BF16)
    dg = _mm(dvg, w_glu, tb=True, name="s5_dg")

    def dgelu_fn(dgb, yb):
        _, th = _gelu_parts(yb)
        dinner = _GELU_C * (1.0 + 3.0 * 0.044715 * yb * yb)
        return (dgb * (0.5 * (1.0 + th) + 0.5 * yb * (1.0 - th * th) * dinner),), ()
    dy = _ew(dgelu_fn, [dg, y], outs=[(D_MODEL, F32)], name="s5_dgelu")[0]
    du_s, dbm, dct, dlam, ddvec = _s5_scan_bwd(_time_to_scan_order(u), _time_to_scan_order(dy), lam,
                                               bmat.astype(MXU_DTYPE), cmat.astype(MXU_DTYPE), dvec)
    du = _scan_to_time_order(du_s)
    dw_in = _mm(h, du, ta=True, name="s5_dw_in", out_dtype=BF16)
    dh = _mm(du, w_in, tb=True, name="s5_dh")
    dx, dgain = _rms_bwd(x, gain, dh, dx1, "mix_norm_bwd")
    return dx, dgain, dw_in, (dlam, dbm, jnp.swapaxes(dct, 2, 3)), ddvec, dw_glu


POOL_BLOCK = 256
N_POOL_GROUPS = len(POOL_WINDOWS)


def _pool_bands(gi, i, t):
    w = jnp.left_shift(2, gi)
    r = lax.broadcasted_iota(jnp.int32, (t, t), 0)
    c = lax.broadcasted_iota(jnp.int32, (t, t), 1)
    inside = ((c <= r) & (c > r - w)).astype(MXU_DTYPE)
    before = (c > r - w + t).astype(MXU_DTYPE)

    def inv_count(block):
        pos = block * t + lax.broadcasted_iota(jnp.int32, (t, 1), 0)
        return 1.0 / jnp.minimum(pos + 1, w).astype(F32)
    return inside, before, inv_count


def _pool_fwd(x, gain, w_grp, b_grp, scale):
    s = x.shape[0]
    t = min(POOL_BLOCK, s)
    h = _rms_fwd(x, gain, "mix_norm")

    def body(h_ref, hp_ref, w_ref, b_ref, sc_ref, x_ref, x1_ref, diff_ref):
        gi, i = pl.program_id(0), pl.program_id(1)
        inside, before, inv_count = _pool_bands(gi, i, t)
        hc = h_ref[...]
        tot = jnp.dot(inside, hc.astype(MXU_DTYPE), preferred_element_type=F32)
        prev = jnp.dot(before, hp_ref[...].astype(MXU_DTYPE), preferred_element_type=F32)
        tot = tot + jnp.where(i > 0, prev, 0.0)
        diff = (tot * inv_count(i) - hc.astype(F32)).astype(diff_ref.dtype)
        y = (jnp.dot(diff.astype(MXU_DTYPE), w_ref[0], preferred_element_type=F32) + b_ref[...]) * sc_ref[...]
        diff_ref[...] = diff
        x1_ref[...] = x_ref[...] + y

    blk = pl.BlockSpec((t, POOL_WIDTH), lambda gi, i: (i, gi))
    vec = pl.BlockSpec((1, POOL_WIDTH), lambda gi, i: (0, gi))
    x1, diff = pl.pallas_call(
        body, name="pool_fwd", grid=(N_POOL_GROUPS, s // t),
        in_specs=[blk, pl.BlockSpec((t, POOL_WIDTH), lambda gi, i: (jnp.maximum(i - 1, 0), gi)),
                  pl.BlockSpec((1, POOL_WIDTH, POOL_WIDTH), lambda gi, i: (gi, 0, 0)), vec, vec, blk],
        out_specs=[blk, blk],
        out_shape=[jax.ShapeDtypeStruct(x.shape, F32), jax.ShapeDtypeStruct(x.shape, BF16)],
        compiler_params=_params(("parallel", "arbitrary")))(h, h, w_grp, b_grp, scale, x)
    return x1, (x, diff)


def _pool_bwd(dx1, saved, gain, w_grp, b_grp, scale):
    x, diff = saved
    s = x.shape[0]
    t = min(POOL_BLOCK, s)
    nb = s // t

    def body1(dx_ref, diff_ref, w_ref, b_ref, sc_ref, dd_ref, dw_ref, db_ref, dsc_ref):
        i = pl.program_id(1)

        @pl.when(i == 0)
        def _():
            dw_ref[...] = jnp.zeros_like(dw_ref)
            db_ref[...] = jnp.zeros_like(db_ref)
            dsc_ref[...] = jnp.zeros_like(dsc_ref)

        dfb = diff_ref[...].astype(MXU_DTYPE)
        ypre = jnp.dot(dfb, w_ref[0], preferred_element_type=F32) + b_ref[...]
        dxb = dx_ref[...]
        dy = dxb * sc_ref[...]
        dsc_ref[...] += jnp.sum(dxb * ypre, axis=0, keepdims=True)
        db_ref[...] += jnp.sum(dy, axis=0, keepdims=True)
        dyb = dy.astype(MXU_DTYPE)
        dw_ref[0] += lax.dot_general(dfb, dyb, (((0,), (0,)), ((), ())), preferred_element_type=F32)
        dd_ref[...] = lax.dot_general(dyb, w_ref[0], (((1,), (1,)), ((), ())), preferred_element_type=F32)

    blk = pl.BlockSpec((t, POOL_WIDTH), lambda gi, i: (i, gi))
    vec = pl.BlockSpec((1, POOL_WIDTH), lambda gi, i: (0, gi))
    mat = pl.BlockSpec((1, POOL_WIDTH, POOL_WIDTH), lambda gi, i: (gi, 0, 0))
    ddiff, dw, db, dsc = pl.pallas_call(
        body1, name="pool_bwd_map", grid=(N_POOL_GROUPS, nb), in_specs=[blk, blk, mat, vec, vec],
        out_specs=[blk, mat, vec, vec],
        out_shape=[jax.ShapeDtypeStruct(x.shape, F32), jax.ShapeDtypeStruct(w_grp.shape, F32),
                   jax.ShapeDtypeStruct((1, D_MODEL), F32), jax.ShapeDtypeStruct((1, D_MODEL), F32)],
        compiler_params=_params(("parallel", "arbitrary")))(dx1, diff, w_grp, b_grp, scale)

    def body2(dc_ref, dn_ref, dh_ref):
        gi, i = pl.program_id(0), pl.program_id(1)
        inside, before, inv_count = _pool_bands(gi, i, t)
        tn = (((0,), (0,)), ((), ()))
        dc = dc_ref[...]
        tot = lax.dot_general(inside, (dc * inv_count(i)).astype(MXU_DTYPE), tn, preferred_element_type=F32)
        nxt = lax.dot_general(before, (dn_ref[...] * inv_count(i + 1)).astype(MXU_DTYPE), tn, preferred_element_type=F32)
        dh_ref[...] = tot + jnp.where(i < nb - 1, nxt, 0.0) - dc

    dh = pl.pallas_call(
        body2, name="pool_bwd_window", grid=(N_POOL_GROUPS, nb),
        in_specs=[blk, pl.BlockSpec((t, POOL_WIDTH), lambda gi, i: (jnp.minimum(i + 1, nb - 1), gi))],
        out_specs=blk, out_shape=jax.ShapeDtypeStruct(x.shape, F32),
        compiler_params=_params(("parallel", "parallel")))(ddiff, ddiff)
    dx, dgain = _rms_bwd(x, gain, dh, dx1, "mix_norm_bwd")
    return dx, dgain, dw, db, dsc


MESH_ID = pl.DeviceIdType.MESH
ANY = pl.BlockSpec(memory_space=pl.ANY)


def _place():
    x, y, c = lax.axis_index("x"), lax.axis_index("y"), lax.axis_index("c")
    other_chips = [(1 - x, y), (x, 1 - y), (1 - x, 1 - y)]
    return x, y, c, other_chips


def _chip_index(chip):
    return 2 * chip[0] + chip[1]


def _remote(src, dst, send_sems, recv_sems, n, to):
    return pltpu.make_async_remote_copy(src_ref=src, dst_ref=dst, send_sem=send_sems.at[n], recv_sem=recv_sems.at[n],
                                        device_id=to, device_id_type=MESH_ID)


def _gather_weights(ws):
    n = len(ws)

    def body(*refs):
        w_refs, out_refs = refs[:n], refs[n:2 * n]
        send_sems, recv_sems = refs[2 * n:]
        x, y, c, chips = _place()
        k = _chip_index((x, y))
        me, sibling = (x, y, c), (x, y, 1 - c)
        first = [_remote(w_refs[t].at[c], out_refs[t].at[k, c], send_sems, recv_sems, 6 * t + j, (*chip, c))
                 for j, chip in enumerate(chips) for t in range(n)]
        for cp in first:
            cp.start()
        passed = []
        for j, chip in enumerate(chips):
            for t in range(n):
                landed = out_refs[t].at[_chip_index(chip), c]
                _remote(landed, landed, send_sems, recv_sems, 6 * t + j, me).wait_recv()
                passed.append(_remote(landed, landed, send_sems, recv_sems, 6 * t + 3 + j, sibling))
                passed[-1].start()
        for j, chip in enumerate(chips):
            for t in range(n):
                theirs = out_refs[t].at[_chip_index(chip), 1 - c]
                _remote(theirs, theirs, send_sems, recv_sems, 6 * t + 3 + j, me).wait_recv()
        for cp in first + passed:
            cp.wait_send()

    return pl.pallas_call(
        body, name="gather_weights", in_specs=[ANY] * n, out_specs=[ANY] * n,
        out_shape=[jax.ShapeDtypeStruct((N_CHIPS,) + w.shape, w.dtype) for w in ws],
        scratch_shapes=[pltpu.SemaphoreType.DMA((6 * n,)), pltpu.SemaphoreType.DMA((6 * n,))],
    )(*ws)


def _swap_halves(gs):
    n = len(gs)

    def body(*refs):
        g_refs, out_refs = refs[:n], refs[n:2 * n]
        send_sems, recv_sems = refs[2 * n:]
        x, y, c, _ = _place()
        copies = [_remote(g_refs[t].at[s, 1 - c], out_refs[t].at[s], send_sems, recv_sems, N_CHIPS * t + s, (x, y, 1 - c))
                  for t in range(n) for s in range(N_CHIPS)]
        for cp in copies:
            cp.start()
        for cp in copies:
            cp.wait_recv()
        for cp in copies:
            cp.wait_send()

    return pl.pallas_call(
        body, name="swap_halves", in_specs=[ANY] * n, out_specs=[ANY] * n,
        out_shape=[jax.ShapeDtypeStruct((N_CHIPS,) + g.shape[2:], g.dtype) for g in gs],
        scratch_shapes=[pltpu.SemaphoreType.DMA((N_CHIPS * n,)), pltpu.SemaphoreType.DMA((N_CHIPS * n,))],
    )(*gs)


def _scatter_partials(ps):
    n = len(ps)

    def body(*refs):
        p_refs, out_refs = refs[:n], refs[n:2 * n]
        send_sems, recv_sems = refs[2 * n:]
        x, y, c, chips = _place()
        sends = [_remote(p_refs[t].at[_chip_index(chip)], out_refs[t].at[j], send_sems, recv_sems, 3 * t + j, (*chip, c))
                 for j, chip in enumerate(chips) for t in range(n)]
        for cp in sends:
            cp.start()
        for j in range(3):
            for t in range(n):
                slot = out_refs[t].at[j]
                _remote(slot, slot, send_sems, recv_sems, 3 * t + j, (x, y, c)).wait_recv()
        for cp in sends:
            cp.wait_send()

    return pl.pallas_call(
        body, name="scatter_partials", in_specs=[ANY] * n, out_specs=[ANY] * n,
        out_shape=[jax.ShapeDtypeStruct((3,) + p.shape[1:], p.dtype) for p in ps],
        scratch_shapes=[pltpu.SemaphoreType.DMA((3 * n,)), pltpu.SemaphoreType.DMA((3 * n,))],
    )(*ps)


def _share_half(fs):
    n = len(fs)

    def body(*refs):
        f_refs, out_refs = refs[:n], refs[n:2 * n]
        send_sems, recv_sems = refs[2 * n:]
        x, y, c, _ = _place()
        sends = [_remote(f_refs[t], out_refs[t].at[c], send_sems, recv_sems, t, (x, y, 1 - c)) for t in range(n)]
        for cp in sends:
            cp.start()
        for t in range(n):
            theirs = out_refs[t].at[1 - c]
            _remote(theirs, theirs, send_sems, recv_sems, t, (x, y, c)).wait_recv()
        for cp in sends:
            cp.wait_send()

    return pl.pallas_call(
        body, name="share_half", in_specs=[ANY] * n, out_specs=[ANY] * n,
        out_shape=[jax.ShapeDtypeStruct((2,) + f.shape, f.dtype) for f in fs],
        scratch_shapes=[pltpu.SemaphoreType.DMA((n,)), pltpu.SemaphoreType.DMA((n,))],
    )(*fs)


def _gather_small(v):
    def body(v_ref, out_ref, send_sems, recv_sems):
        x, y, c, chips = _place()
        me, sibling = (x, y, c), (x, y, 1 - c)

        def slot(px, py, pc):
            return out_ref.at[4 * px + 2 * py + pc]

        first = [_remote(v_ref, slot(*me), send_sems, recv_sems, 0, sibling)]
        first += [_remote(v_ref, slot(*me), send_sems, recv_sems, 1 + j, (*chip, c)) for j, chip in enumerate(chips)]
        for cp in first:
            cp.start()
        passed = [_remote(slot(*chip, c), slot(*chip, c), send_sems, recv_sems, 4 + j, sibling)
                  for j, chip in enumerate(chips)]
        for j, chip in enumerate(chips):
            _remote(slot(*chip, c), slot(*chip, c), send_sems, recv_sems, 1 + j, me).wait_recv()
            passed[j].start()
        _remote(slot(*sibling), slot(*sibling), send_sems, recv_sems, 0, me).wait_recv()
        for j, chip in enumerate(chips):
            _remote(slot(*chip, 1 - c), slot(*chip, 1 - c), send_sems, recv_sems, 4 + j, me).wait_recv()
        for cp in first + passed:
            cp.wait_send()

    gathered = pl.pallas_call(
        body, name="gather_small", in_specs=[ANY], out_specs=ANY,
        out_shape=jax.ShapeDtypeStruct((N_DEV,) + v.shape, v.dtype),
        scratch_shapes=[pltpu.SemaphoreType.DMA((7,)), pltpu.SemaphoreType.DMA((7,))],
    )(v)
    device = 4 * lax.axis_index("x") + 2 * lax.axis_index("y") + lax.axis_index("c")
    return lax.dynamic_update_index_in_dim(gathered, v, device, 0)


SMALL_ROWS = 256
SUM_ROWS = 256
BF16_ROWS = 16


def _row_tile(rows, want):
    for t in range(min(rows, want) // BF16_ROWS * BF16_ROWS, 0, -BF16_ROWS):
        if rows % t == 0:
            return t
    return rows


def _pair_sum(g, r, core, name):
    rows, cols = g.shape[2:]
    tr = _row_tile(rows, SUM_ROWS)

    def body(c_ref, g_ref, r_ref, o_ref):
        o_ref[0] = (g_ref[0, 0].astype(F32) + r_ref[0].astype(F32)).astype(o_ref.dtype)

    return pl.pallas_call(
        body, name=name,
        grid_spec=pltpu.PrefetchScalarGridSpec(
            num_scalar_prefetch=1, grid=(N_CHIPS, rows // tr),
            in_specs=[pl.BlockSpec((1, 1, tr, cols), lambda s, i, c_ref: (s, c_ref[0], i, 0)),
                      pl.BlockSpec((1, tr, cols), lambda s, i, c_ref: (s, i, 0))],
            out_specs=pl.BlockSpec((1, tr, cols), lambda s, i, c_ref: (s, i, 0))),
        out_shape=jax.ShapeDtypeStruct(r.shape, BF16),
        compiler_params=_params(("parallel", "parallel")))(core, g, r)


def _chip_sum(p, recv, chip, name):
    rows, cols = p.shape[1:]
    tr = _row_tile(rows, SUM_ROWS)

    def body(k_ref, p_ref, r_ref, o_ref):
        acc = p_ref[0].astype(F32)
        for j in range(3):
            acc = acc + r_ref[j].astype(F32)
        o_ref[...] = acc

    return pl.pallas_call(
        body, name=name,
        grid_spec=pltpu.PrefetchScalarGridSpec(
            num_scalar_prefetch=1, grid=(rows // tr,),
            in_specs=[pl.BlockSpec((1, tr, cols), lambda i, k_ref: (k_ref[0], i, 0)),
                      pl.BlockSpec((3, tr, cols), lambda i, k_ref: (0, i, 0))],
            out_specs=pl.BlockSpec((tr, cols), lambda i, k_ref: (i, 0))),
        out_shape=jax.ShapeDtypeStruct((rows, cols), F32),
        compiler_params=_params(("parallel",)))(chip, p, recv)


def _sum_blocks(a, name):
    n, rows, cols = a.shape
    tr = _row_tile(rows, SUM_ROWS)

    def body(a_ref, o_ref):
        acc = a_ref[0].astype(F32)
        for s in range(1, n):
            acc = acc + a_ref[s].astype(F32)
        o_ref[...] = acc

    return pl.pallas_call(
        body, name=name, grid=(rows // tr,),
        in_specs=[pl.BlockSpec((n, tr, cols), lambda i: (0, i, 0))],
        out_specs=pl.BlockSpec((tr, cols), lambda i: (i, 0)),
        out_shape=jax.ShapeDtypeStruct((rows, cols), F32),
        compiler_params=_params(("parallel",)))(a)


def _adamw(w, g, m, v, name):
    def fn(wb, gb, mb, vb):
        m2 = ADAM_B1 * mb + (1.0 - ADAM_B1) * gb
        v2 = ADAM_B2 * vb + (1.0 - ADAM_B2) * (gb * gb)
        m_hat = m2 / (1.0 - ADAM_B1 ** ADAM_STEP)
        v_hat = v2 / (1.0 - ADAM_B2 ** ADAM_STEP)
        delta = -ADAM_LR * (m_hat / (jnp.sqrt(v_hat) + ADAM_EPS) + ADAM_WD * wb)
        return (delta, m2, v2), ()
    c = w.shape[1]
    return _ew(fn, [w, g, m, v], outs=[(c, F32)] * 3, name=name)


WEIGHTS = ["mix_norm_g", "ffn_norm_g", "final_norm_g", "fox_w_in", "fox_b_f", "fox_w_out", "s5_w_in", "s5_a_re",
           "s5_a_im", "s5_log_dt", "s5_b_re", "s5_b_im", "s5_c_re", "s5_c_im", "s5_d", "s5_w_glu", "pool_w",
           "pool_b", "pool_scale", "ffn_w_gate_up", "ffn_w_down"]
BIG = {"fox_w_in": 2, "fox_w_out": 1, "s5_w_in": 1, "s5_w_glu": 2, "pool_w": 2, "ffn_w_gate_up": 2, "ffn_w_down": 1}
SLICED = ("pool_b", "pool_scale")
SMALL = [n for n in WEIGHTS if n not in BIG]


def _to_natural(cm, axis):
    moved = jnp.moveaxis(cm, 0, axis)
    shape = moved.shape[:axis] + (moved.shape[axis] * moved.shape[axis + 1],) + moved.shape[axis + 2:]
    return moved.reshape(shape)


def _to_chip_major(nat, axis):
    shape = nat.shape[:axis] + (N_CHIPS, nat.shape[axis] // N_CHIPS) + nat.shape[axis + 1:]
    return jnp.moveaxis(nat.reshape(shape), axis, 0)


def _halves_view(shape):
    return (2, int(np.prod(shape[:-1])) // 2, shape[-1])


def _pack_small(parts):
    flat = jnp.concatenate([p.reshape(-1).astype(F32) for p in parts])
    pad = (-flat.shape[0]) % (SMALL_ROWS * LANES)
    return jnp.pad(flat, (0, pad)).reshape(-1, LANES)


def _unpack_small(buf, shapes):
    flat = buf.reshape(-1)
    out, off = [], 0
    for shp in shapes:
        n = int(np.prod(shp))
        out.append(flat[off:off + n].reshape(shp))
        off += n
    return out


def _local_step(x, target, w):
    grads = {}
    mixers = ("fox", "s5", "pool")
    saved = []
    ssm, ssm_pull = jax.vjp(_s5_operands, w["s5_a_re"][0], w["s5_a_im"][0], w["s5_log_dt"][0], w["s5_b_re"][0],
                            w["s5_b_im"][0], w["s5_c_re"][0], w["s5_c_im"][0])
    fox_w = []
    for j in range(w["fox_w_in"].shape[0]):
        w_in = w["fox_w_in"][j]
        w_f = jnp.pad(w_in[:, 3 * D_MODEL:], ((0, 0), (0, LANES - FOX_HEADS)))
        fox_w.append((w_in[:, :3 * D_MODEL], w_f, w["fox_w_out"][j]))
    for i in range(DEPTH):
        kind, j = mixers[i % 3], i // 3
        gain = w["mix_norm_g"][i]
        if kind == "fox":
            x1, sv = _fox_fwd(x, gain, fox_w[j][0], fox_w[j][1], w["fox_b_f"][j], fox_w[j][2])
        elif kind == "s5":
            x1, sv = _s5_fwd(x, gain, w["s5_w_in"][j], ssm, w["s5_d"], w["s5_w_glu"][j])
        else:
            x1, sv = _pool_fwd(x, gain, w["pool_w"][j], w["pool_b"], w["pool_scale"])
        x, sf = _ffn_fwd(x1, w["ffn_norm_g"][i], w["ffn_w_gate_up"][i], w["ffn_w_down"][i])
        saved.append((sv, sf))
    loss, dx, grads["final_norm_g"] = _loss_head(x, w["final_norm_g"], target)
    per_layer = {n: [None] * DEPTH for n in ("mix_norm_g", "ffn_norm_g", "ffn_w_gate_up", "ffn_w_down")}
    fox_g = {n: [None] * len(fox_w) for n in ("fox_w_in", "fox_b_f", "fox_w_out")}
    for i in reversed(range(DEPTH)):
        kind, j = mixers[i % 3], i // 3
        sv, sf = saved[i]
        dx, per_layer["ffn_norm_g"][i], per_layer["ffn_w_gate_up"][i], per_layer["ffn_w_down"][i] = _ffn_bwd(
            dx, sf, w["ffn_norm_g"][i], w["ffn_w_gate_up"][i], w["ffn_w_down"][i])
        gain = w["mix_norm_g"][i]
        if kind == "fox":
            dx, per_layer["mix_norm_g"][i], fox_g["fox_w_in"][j], fox_g["fox_b_f"][j], fox_g["fox_w_out"][j] = _fox_bwd(
                dx, sv, gain, fox_w[j][0], fox_w[j][1], fox_w[j][2])
        elif kind == "s5":
            dx, per_layer["mix_norm_g"][i], dw_in, dssm, dd, dw_glu = _s5_bwd(
                dx, sv, gain, w["s5_w_in"][j], ssm, w["s5_d"], w["s5_w_glu"][j])
            grads["s5_w_in"], grads["s5_w_glu"], grads["s5_d"] = dw_in[None], dw_glu[None], dd
            for n, g in zip(("s5_a_re", "s5_a_im", "s5_log_dt", "s5_b_re", "s5_b_im", "s5_c_re", "s5_c_im"), ssm_pull(dssm)):
                grads[n] = g[None]
        else:
            dx, per_layer["mix_norm_g"][i], dw, db, dsc = _pool_bwd(dx, sv, gain, w["pool_w"][j], w["pool_b"], w["pool_scale"])
            grads["pool_w"], grads["pool_b"], grads["pool_scale"] = dw[None].astype(BF16), db, dsc
    for n, parts in {**per_layer, **fox_g}.items():
        grads[n] = jnp.stack(parts)
    return loss, dx, grads


def kernel(x, mix_norm_g, ffn_norm_g, final_norm_g, fox_w_in, fox_b_f, fox_w_out, s5_w_in, s5_a_re, s5_a_im, s5_log_dt, s5_b_re, s5_b_im, s5_c_re, s5_c_im, s5_d, s5_w_glu, pool_w, pool_b, pool_scale, ffn_w_gate_up, ffn_w_down, loss_target, m_mix_norm_g, m_ffn_norm_g, m_final_norm_g, m_fox_w_in, m_fox_b_f, m_fox_w_out, m_s5_w_in, m_s5_a_re, m_s5_a_im, m_s5_log_dt, m_s5_b_re, m_s5_b_im, m_s5_c_re, m_s5_c_im, m_s5_d, m_s5_w_glu, m_pool_w, m_pool_b, m_pool_scale, m_ffn_w_gate_up, m_ffn_w_down, v_mix_norm_g, v_ffn_norm_g, v_final_norm_g, v_fox_w_in, v_fox_b_f, v_fox_w_out, v_s5_w_in, v_s5_a_re, v_s5_a_im, v_s5_log_dt, v_s5_b_re, v_s5_b_im, v_s5_c_re, v_s5_c_im, v_s5_d, v_s5_w_glu, v_pool_w, v_pool_b, v_pool_scale, v_ffn_w_gate_up, v_ffn_w_down):
    given = dict(locals())
    shard = {n: given[n] for n in WEIGHTS}
    chip = 2 * lax.axis_index("x") + lax.axis_index("y")
    core = lax.axis_index("c")

    views = {n: _halves_view(shard[n].shape) for n in BIG}
    own = [shard[n].astype(MXU_DTYPE).reshape(views[n]) for n in BIG]
    whole = {}
    for n, mine, others in zip(BIG, own, _gather_weights(own)):
        by_chip = lax.dynamic_update_index_in_dim(others, mine, chip, 0)
        whole[n] = _to_natural(by_chip.reshape((N_CHIPS,) + shard[n].shape), BIG[n])
    for n in SMALL:
        whole[n] = shard[n]
    sliced_shapes = [shard[n].shape for n in SLICED]
    by_chip = _gather_small(_pack_small([shard[n] for n in SLICED]))[0::2]
    slices = [_unpack_small(by_chip[k], sliced_shapes) for k in range(N_CHIPS)]
    for idx, n in enumerate(SLICED):
        whole[n] = jnp.concatenate([slices[k][idx] for k in range(N_CHIPS)], axis=-1)

    loss_part, dx, grads = _local_step(x[0], loss_target[0], whole)
    loss = lax.psum(loss_part, MESH_AXES)

    gs = [_to_chip_major(grads[n].astype(BF16), BIG[n]).reshape((N_CHIPS,) + views[n]) for n in BIG]
    core_id, chip_id = core.reshape(1).astype(jnp.int32), chip.reshape(1).astype(jnp.int32)
    partial = [_pair_sum(g, r, core_id, "pair_sum_" + n) for n, g, r in zip(BIG, gs, _swap_halves(gs))]
    half = [_chip_sum(p, r, chip_id, "chip_sum_" + n) for n, p, r in zip(BIG, partial, _scatter_partials(partial))]
    grad = {n: lax.dynamic_update_index_in_dim(both, mine, core, 0).reshape(shard[n].shape)
            for n, mine, both in zip(BIG, half, _share_half(half))}

    small_sum = _sum_blocks(_gather_small(_pack_small([grads[n] for n in SMALL])), "small_sum")
    for n, g in zip(SMALL, _unpack_small(small_sum, [whole[n].shape for n in SMALL])):
        grad[n] = g
    for n in SLICED:
        width = shard[n].shape[-1]
        grad[n] = lax.dynamic_slice_in_dim(grad[n], chip * width, width, axis=-1)

    delta, new_m, new_v = {}, {}, {}
    for n in BIG:
        view = (-1, shard[n].shape[-1])
        res = _adamw(shard[n].reshape(view), grad[n].reshape(view), given["m_" + n].reshape(view),
                     given["v_" + n].reshape(view), "adamw_" + n)
        delta[n], new_m[n], new_v[n] = (r.reshape(shard[n].shape) for r in res)
    small_shapes = [shard[n].shape for n in SMALL]
    res = _adamw(_pack_small([shard[n] for n in SMALL]), _pack_small([grad[n] for n in SMALL]),
                 _pack_small([given["m_" + n] for n in SMALL]), _pack_small([given["v_" + n] for n in SMALL]), "adamw_small")
    for out, buf in zip((delta, new_m, new_v), res):
        for n, a in zip(SMALL, _unpack_small(buf, small_shapes)):
            out[n] = a
    return (loss, dx[None], *[grad[n] for n in WEIGHTS], *[delta[n] for n in WEIGHTS],
            *[new_m[n] for n in WEIGHTS], *[new_v[n] for n in WEIGHTS])
```

```python
import functools
import math

import jax
import jax.numpy as jnp
import numpy as np
from jax import lax
from jax.experimental import pallas as pl
from jax.experimental.pallas import tpu as pltpu

F32 = jnp.float32
BF16 = jnp.bfloat16
MXU_DTYPE = jnp.bfloat16

D_MODEL = 1024
DEPTH = 4
EPS = 1e-6
FOX_HEADS = 16
FOX_HEAD_DIM = 64
HEAD_PAIRS = FOX_HEADS // 2
S5_GROUPS = 64
S5_GROUP = 16
S5_STATE = 64
S5_BLOCKS = 8
S5_HALF = 256
POOL_WINDOWS = (2, 4, 8, 16)
POOL_WIDTH = 256
D_FF = 2816
N_CHIPS = 4
N_DEV = 8
LANES = 128
SUBLANES = 8
VMEM_LIMIT = 56 * 1024 * 1024

ADAM_LR = 0.001
ADAM_B1 = 0.9
ADAM_B2 = 0.999
ADAM_EPS = 1e-08
ADAM_WD = 0.01
ADAM_STEP = 10

MESH_AXES = ("x", "y", "c")


def _tile(n, want):
    t = (min(n, want) // LANES) * LANES
    while t >= LANES:
        if n % t == 0:
            return t
        t -= LANES
    return n


def _params(sem=None):
    return pltpu.CompilerParams(dimension_semantics=sem, vmem_limit_bytes=VMEM_LIMIT)


def _mm(a, b, *, name, ta=False, tb=False, out_dtype=F32, add=None, tm=1024, tn=1024, tk=1024):
    m, k = (a.shape[1], a.shape[0]) if ta else a.shape
    n = b.shape[0] if tb else b.shape[1]
    assert (b.shape[1] if tb else b.shape[0]) == k, (a.shape, b.shape, ta, tb)
    tm, tn, tk = _tile(m, tm), _tile(n, tn), _tile(k, tk)
    nk = k // tk
    a_spec = pl.BlockSpec((tk, tm), lambda i, j, kk: (kk, i)) if ta else pl.BlockSpec((tm, tk), lambda i, j, kk: (i, kk))
    b_spec = pl.BlockSpec((tn, tk), lambda i, j, kk: (j, kk)) if tb else pl.BlockSpec((tk, tn), lambda i, j, kk: (kk, j))
    o_spec = pl.BlockSpec((tm, tn), lambda i, j, kk: (i, j))
    dims = (((0 if ta else 1,), (1 if tb else 0,)), ((), ()))
    has_add = add is not None

    def body(*refs):
        if has_add:
            a_ref, b_ref, add_ref, o_ref, acc_ref = refs
        else:
            a_ref, b_ref, o_ref, acc_ref = refs
        kk = pl.program_id(2)

        @pl.when(kk == 0)
        def _():
            acc_ref[...] = jnp.zeros_like(acc_ref)

        acc_ref[...] += lax.dot_general(a_ref[...].astype(MXU_DTYPE), b_ref[...].astype(MXU_DTYPE), dims,
                                        preferred_element_type=F32)

        @pl.when(kk == nk - 1)
        def _():
            r = acc_ref[...]
            if has_add:
                r = r + add_ref[...].astype(F32)
            o_ref[...] = r.astype(out_dtype)

    ins = [a, b] + ([add] if has_add else [])
    specs = [a_spec, b_spec] + ([o_spec] if has_add else [])
    return pl.pallas_call(
        body, name=name, grid=(m // tm, n // tn, nk), in_specs=specs, out_specs=o_spec,
        out_shape=jax.ShapeDtypeStruct((m, n), out_dtype), scratch_shapes=[pltpu.VMEM((tm, tn), F32)],
        compiler_params=_params(("parallel", "parallel", "arbitrary")))(*ins)


def _ew(fn, tens, vecs=(), *, outs=(), sums=(), name, tr=256):
    tens = [t if isinstance(t, tuple) else (t, t.shape[1], 0) for t in tens]
    rows = tens[0][0].shape[0]
    tr = min(tr, rows)
    n_t, n_v, n_o, n_s = len(tens), len(vecs), len(outs), len(sums)

    def body(*refs):
        i = pl.program_id(0)
        t_blocks = [r[...] for r in refs[:n_t]]
        v_blocks = [r[...] for r in refs[n_t:n_t + n_v]]
        o_refs = refs[n_t + n_v:n_t + n_v + n_o]
        s_refs = refs[n_t + n_v + n_o:]
        o_vals, s_vals = fn(*t_blocks, *v_blocks)
        for r, v in zip(o_refs, o_vals):
            r[...] = v.astype(r.dtype)
        if n_s:
            @pl.when(i == 0)
            def _():
                for r in s_refs:
                    r[...] = jnp.zeros_like(r)
            for r, v in zip(s_refs, s_vals):
                r[...] += jnp.sum(v.astype(F32), axis=0, keepdims=True)

    in_specs = [pl.BlockSpec((tr, w), functools.partial(lambda i, cb: (i, cb), cb=cb)) for _, w, cb in tens]
    in_specs += [pl.BlockSpec(v.shape, functools.partial(lambda i, nd: (0,) * nd, nd=v.ndim)) for v in vecs]
    out_specs = [pl.BlockSpec((tr, c), lambda i: (i, 0)) for c, _ in outs]
    out_specs += [pl.BlockSpec((1, c), lambda i: (0, 0)) for c in sums]
    out_shape = [jax.ShapeDtypeStruct((rows, c), dt) for c, dt in outs]
    out_shape += [jax.ShapeDtypeStruct((1, c), F32) for c in sums]
    res = pl.pallas_call(
        body, name=name, grid=(rows // tr,), in_specs=in_specs, out_specs=out_specs, out_shape=out_shape,
        compiler_params=_params(("arbitrary",)))(*[t[0] for t in tens], *vecs)
    return res


def _sigmoid(z):
    return 1.0 / (1.0 + jnp.exp(-z))


def _rms_fwd(x, g, name):
    def fn(xb, gb):
        r = lax.rsqrt(jnp.mean(xb * xb, axis=-1, keepdims=True) + EPS)
        return ((xb * r) * gb,), ()
    return _ew(fn, [x], [g.reshape(1, -1)], outs=[(x.shape[1], BF16)], name=name)[0]


def _rms_bwd(x, g, dh, dres, name):
    def fn(xb, dhb, drb, gb):
        r = lax.rsqrt(jnp.mean(xb * xb, axis=-1, keepdims=True) + EPS)
        xh = xb * r
        dhf = dhb.astype(F32)
        dy = dhf * gb
        dx = r * (dy - xh * jnp.mean(dy * xh, axis=-1, keepdims=True))
        return (drb + dx,), (dhf * xh,)
    dx, dg = _ew(fn, [x, dh, dres], [g.reshape(1, -1)], outs=[(x.shape[1], F32)], sums=[x.shape[1]], name=name)
    return dx, dg[0]


def _mm_rms_bwd(a, b, x, g, dres, *, name, add=None, tm=512, tk=1024):
    m, k = a.shape
    n = b.shape[0]
    assert b.shape[1] == k and x.shape == (m, n)
    tm, tk = _tile(m, tm), _tile(k, tk)
    nk = k // tk
    has_add = add is not None

    def body(*refs):
        a_ref, b_ref, x_ref, dr_ref, g_ref = refs[:5]
        add_ref = refs[5] if has_add else None
        dx_ref, dg_ref, acc_ref = refs[-3:]
        i, kk = pl.program_id(0), pl.program_id(1)

        @pl.when(kk == 0)
        def _():
            acc_ref[...] = jnp.zeros_like(acc_ref)

        @pl.when((kk == 0) & (i == 0))
        def _():
            dg_ref[...] = jnp.zeros_like(dg_ref)

        acc_ref[...] += lax.dot_general(a_ref[...].astype(MXU_DTYPE), b_ref[...].astype(MXU_DTYPE),
                                        (((1,), (1,)), ((), ())), preferred_element_type=F32)

        @pl.when(kk == nk - 1)
        def _():
            dh = acc_ref[...]
            if has_add:
                dh = dh + add_ref[...]
            xb = x_ref[...]
            r = lax.rsqrt(jnp.mean(xb * xb, axis=-1, keepdims=True) + EPS)
            xh = xb * r
            dy = dh * g_ref[...]
            dx_ref[...] = dr_ref[...] + r * (dy - xh * jnp.mean(dy * xh, axis=-1, keepdims=True))
            dg_ref[...] += jnp.sum(dh * xh, axis=0, keepdims=True)

    row = pl.BlockSpec((tm, n), lambda i, kk: (i, 0))
    vec = pl.BlockSpec((1, n), lambda i, kk: (0, 0))
    ins = [a, b, x, dres, g.reshape(1, n)] + ([add] if has_add else [])
    specs = [pl.BlockSpec((tm, tk), lambda i, kk: (i, kk)), pl.BlockSpec((n, tk), lambda i, kk: (0, kk)), row, row, vec]
    specs += [row] if has_add else []
    dx, dg = pl.pallas_call(
        body, name=name, grid=(m // tm, nk), in_specs=specs, out_specs=[row, vec],
        out_shape=[jax.ShapeDtypeStruct((m, n), F32), jax.ShapeDtypeStruct((1, n), F32)],
        scratch_shapes=[pltpu.VMEM((tm, n), F32)],
        compiler_params=_params(("arbitrary", "arbitrary")))(*ins)
    return dx, dg[0]


FFN_ROWS = 512
FFN_COLS = D_FF // 2


def _ffn_gate_up(h, w_gu):
    s, d = h.shape
    tm = min(FFN_ROWS, s)
    halves = D_FF // FFN_COLS

    def body(h_ref, wg_ref, wu_ref, g_ref, u_ref, a_ref):
        hb = h_ref[...]
        g = jnp.dot(hb, wg_ref[...], preferred_element_type=F32)
        u = jnp.dot(hb, wu_ref[...], preferred_element_type=F32)
        g_ref[...] = g.astype(g_ref.dtype)
        u_ref[...] = u.astype(u_ref.dtype)
        a_ref[...] = (g * _sigmoid(g) * u).astype(a_ref.dtype)

    tile = pl.BlockSpec((tm, FFN_COLS), lambda jj, i: (i, jj))
    return pl.pallas_call(
        body, name="ffn_gate_up", grid=(halves, s // tm),
        in_specs=[pl.BlockSpec((tm, d), lambda jj, i: (i, 0)),
                  pl.BlockSpec((d, FFN_COLS), lambda jj, i: (0, jj)),
                  pl.BlockSpec((d, FFN_COLS), lambda jj, i: (0, halves + jj))],
        out_specs=[tile, tile, tile], out_shape=[jax.ShapeDtypeStruct((s, D_FF), BF16)] * 3,
        compiler_params=_params(("parallel", "parallel")))(h, w_gu, w_gu)


def _ffn_dgate_up(dx2, w_down, g, u):
    s, d = dx2.shape
    tm = min(FFN_ROWS, s)
    halves = D_FF // FFN_COLS

    def body(dx_ref, w_ref, g_ref, u_ref, o_ref):
        jj = pl.program_id(1)
        df = lax.dot_general(dx_ref[...].astype(MXU_DTYPE), w_ref[...], (((1,), (1,)), ((), ())), preferred_element_type=F32)
        gf, uf = g_ref[...].astype(F32), u_ref[...].astype(F32)
        sg = _sigmoid(gf)
        dg = df * uf * (sg * (1.0 + gf * (1.0 - sg)))
        du = df * (gf * sg)
        o_ref[:, pl.ds(pl.multiple_of(jj * FFN_COLS, LANES), FFN_COLS)] = dg.astype(o_ref.dtype)
        o_ref[:, pl.ds(pl.multiple_of(D_FF + jj * FFN_COLS, LANES), FFN_COLS)] = du.astype(o_ref.dtype)

    tile = pl.BlockSpec((tm, FFN_COLS), lambda i, jj: (i, jj))
    return pl.pallas_call(
        body, name="ffn_dgate_up", grid=(s // tm, halves),
        in_specs=[pl.BlockSpec((tm, d), lambda i, jj: (i, 0)),
                  pl.BlockSpec((FFN_COLS, d), lambda i, jj: (jj, 0)), tile, tile],
        out_specs=pl.BlockSpec((tm, 2 * D_FF), lambda i, jj: (i, 0)),
        out_shape=jax.ShapeDtypeStruct((s, 2 * D_FF), BF16),
        compiler_params=_params(("parallel", "arbitrary")))(dx2, w_down, g, u)


def _ffn_fwd(x1, gain, w_gu, w_down):
    h = _rms_fwd(x1, gain, "ffn_norm")
    g, u, act = _ffn_gate_up(h, w_gu)
    x2 = _mm(act, w_down, name="ffn_down", add=x1, tk=D_FF // 2)
    return x2, (x1, h, g, u, act)


def _ffn_bwd(dx2, saved, gain, w_gu, w_down):
    x1, h, g, u, act = saved
    dw_down = _mm(act, dx2, ta=True, name="ffn_dw_down", out_dtype=BF16, tm=D_FF // 2)
    dgu = _ffn_dgate_up(dx2, w_down, g, u)
    dw_gu = _mm(h, dgu, ta=True, name="ffn_dw_gu", out_dtype=BF16, tn=D_FF // 2)
    dx1, dgain = _mm_rms_bwd(dgu, w_gu, x1, gain, dx2, name="ffn_dh", tk=512)
    return dx1, dgain, dw_gu, dw_down


def _loss_head(x, gain, target):
    d = x.shape[1]

    def fn(xb, tb, gb):
        r = lax.rsqrt(jnp.mean(xb * xb, axis=-1, keepdims=True) + EPS)
        xh = xb * r
        y = xh * gb
        err = y - tb
        dyv = err * (1.0 / d)
        dyg = dyv * gb
        dx = r * (dyg - xh * jnp.mean(dyg * xh, axis=-1, keepdims=True))
        return (dx,), (0.5 * err * err * (1.0 / d), dyv * xh)
    dx, lsum, dg = _ew(fn, [x, target], [gain.reshape(1, -1)], outs=[(d, F32)], sums=[d, d], name="loss_head")
    return jnp.sum(lsum), dx, dg[0]


ATT_BLOCK = 256
CUM_BLOCK = 512
NEG_INF = -1e30


def _fox_gate_fwd(fl_row, b_col):
    nh, s = fl_row.shape
    tb = min(CUM_BLOCK, s)

    def body(fl_ref, b_ref, z_ref, c_ref):
        upper = (lax.broadcasted_iota(jnp.int32, (tb, tb), 0) <= lax.broadcasted_iota(jnp.int32, (tb, tb), 1)).astype(F32)
        carry = jnp.zeros((nh, 1), F32)
        for blk in range(s // tb):
            z = fl_ref[:, blk * tb:(blk + 1) * tb] + b_ref[...]
            logf = jnp.minimum(z, 0.0) - jnp.log(1.0 + jnp.exp(-jnp.abs(z)))
            cs = jnp.dot(logf, upper, precision=lax.Precision.HIGHEST, preferred_element_type=F32) + carry
            z_ref[:, blk * tb:(blk + 1) * tb] = z
            c_ref[:, blk * tb:(blk + 1) * tb] = cs
            carry = cs[:, tb - 1:tb]

    return pl.pallas_call(body, name="fox_gate_fwd", out_shape=[jax.ShapeDtypeStruct((nh, s), F32)] * 2,
                          compiler_params=_params())(fl_row, b_col)


def _fox_gate_bwd(dc_row, z_row):
    nh, s = dc_row.shape
    tb = min(CUM_BLOCK, s)

    def body(dc_ref, z_ref, dz_ref, db_ref):
        lower = (lax.broadcasted_iota(jnp.int32, (tb, tb), 0) >= lax.broadcasted_iota(jnp.int32, (tb, tb), 1)).astype(F32)
        carry = jnp.zeros((nh, 1), F32)
        db = jnp.zeros((nh, 1), F32)
        for blk in reversed(range(s // tb)):
            dc = dc_ref[:, blk * tb:(blk + 1) * tb]
            rs = jnp.dot(dc, lower, precision=lax.Precision.HIGHEST, preferred_element_type=F32) + carry
            dz = rs * _sigmoid(-z_ref[:, blk * tb:(blk + 1) * tb])
            dz_ref[:, blk * tb:(blk + 1) * tb] = dz
            db = db + jnp.sum(dz, axis=1, keepdims=True)
            carry = rs[:, 0:1]
        db_ref[...] = db

    return pl.pallas_call(body, name="fox_gate_bwd",
                          out_shape=[jax.ShapeDtypeStruct((nh, s), F32), jax.ShapeDtypeStruct((nh, 1), F32)],
                          compiler_params=_params())(dc_row, z_row)


def _head_masks(rows):
    lane = lax.broadcasted_iota(jnp.int32, (rows, LANES), 1)
    return lane < FOX_HEAD_DIM


def _attn_fwd(qkv, c_row):
    s = qkv.shape[0]
    t = min(ATT_BLOCK, s)
    nq = s // t
    scale = FOX_HEAD_DIM ** -0.5

    def body(q_ref, k_ref, v_ref, cr_ref, o_ref, lse_ref):
        i = pl.program_id(1)
        first = _head_masks(t)
        qs = q_ref[...] * scale
        zero = jnp.zeros_like(qs)
        qh = (jnp.where(first, qs, zero), jnp.where(first, zero, qs))
        causal = lax.broadcasted_iota(jnp.int32, (t, t), 0) >= lax.broadcasted_iota(jnp.int32, (t, t), 1)

        def block(j, carry, masked):
            ms, ls, acc = carry
            start = pl.multiple_of(j * t, t)
            ks, vs = k_ref[pl.ds(start, t), :], v_ref[pl.ds(start, t), :]
            new_m, new_l, alphas, pvs = [], [], [], []
            for hh in range(2):
                sc = lax.dot_general(qh[hh], ks, (((1,), (1,)), ((), ())), preferred_element_type=F32)
                sc = sc - cr_ref[0, hh:hh + 1, pl.ds(start, t)]
                if masked:
                    sc = jnp.where(causal, sc, NEG_INF)
                m_new = jnp.maximum(ms[hh], jnp.max(sc, axis=1, keepdims=True))
                p = jnp.exp(sc - m_new)
                alpha = jnp.exp(ms[hh] - m_new)
                new_m.append(m_new)
                new_l.append(alpha * ls[hh] + jnp.sum(p, axis=1, keepdims=True))
                alphas.append(alpha)
                p_hi = p.astype(MXU_DTYPE)
                p_lo = (p - p_hi.astype(F32)).astype(MXU_DTYPE)
                pvs.append(jnp.dot(p_hi, vs, preferred_element_type=F32) + jnp.dot(p_lo, vs, preferred_element_type=F32))
            acc = jnp.where(first, alphas[0], alphas[1]) * acc + jnp.where(first, pvs[0], pvs[1])
            return tuple(new_m), tuple(new_l), acc

        neg, nil = jnp.full((t, 1), NEG_INF, F32), jnp.zeros((t, 1), F32)
        carry = lax.fori_loop(0, i, functools.partial(block, masked=False), ((neg, neg), (nil, nil), jnp.zeros((t, LANES), F32)))
        ms, ls, acc = block(i, carry, True)
        o_ref[...] = (acc / jnp.where(first, ls[0], ls[1])).astype(o_ref.dtype)
        lse_ref[0] = jnp.concatenate([ms[0] + jnp.log(ls[0]), ms[1] + jnp.log(ls[1])], axis=1)

    np_ = HEAD_PAIRS
    return pl.pallas_call(
        body, name="fox_attn_fwd", grid=(np_, nq),
        in_specs=[pl.BlockSpec((t, LANES), lambda p, i: (i, p)),
                  pl.BlockSpec((s, LANES), lambda p, i: (0, np_ + p)),
                  pl.BlockSpec((s, LANES), lambda p, i: (0, 2 * np_ + p)),
                  pl.BlockSpec((1, 2, s), lambda p, i: (p, 0, 0))],
        out_specs=[pl.BlockSpec((t, LANES), lambda p, i: (i, p)),
                   pl.BlockSpec((1, t, 2), lambda p, i: (p, i, 0))],
        out_shape=[jax.ShapeDtypeStruct((s, D_MODEL), F32), jax.ShapeDtypeStruct((np_, s, 2), F32)],
        compiler_params=_params(("parallel", "arbitrary")))(qkv, qkv, qkv, c_row)


def _attn_bwd(qkv, do, lse, delta, c_row):
    s = qkv.shape[0]
    t = min(ATT_BLOCK, s)
    nb = s // t
    scale = FOX_HEAD_DIM ** -0.5
    np_ = HEAD_PAIRS

    def body(q_ref, k_ref, v_ref, do_ref, lse_ref, dl_ref, cr_ref, dq_ref, dk_ref, dv_ref, dc_ref):
        j = pl.program_id(1)
        first = _head_masks(t)
        causal = lax.broadcasted_iota(jnp.int32, (t, t), 0) >= lax.broadcasted_iota(jnp.int32, (t, t), 1)
        kb = k_ref[...]
        vb = v_ref[...]

        @pl.when(j == 0)
        def _():
            dq_ref[...] = jnp.zeros_like(dq_ref)

        def step(i, carry, masked):
            dk_acc, dv_acc, dc_accs = carry
            rows = pl.ds(pl.multiple_of(i * t, t), t)
            qs = q_ref[rows, :] * scale
            dob = do_ref[rows, :]
            zero = jnp.zeros_like(qs)
            dks, dvs, dqs, dcs = [], [], [], []
            for hh in range(2):
                qh = jnp.where(first, qs, zero) if hh == 0 else jnp.where(first, zero, qs)
                doh = jnp.where(first, dob, zero) if hh == 0 else jnp.where(first, zero, dob)
                sc = lax.dot_general(qh, kb, (((1,), (1,)), ((), ())), preferred_element_type=F32)
                p = jnp.exp(sc - cr_ref[0, hh:hh + 1, :] - lse_ref[0, rows, hh:hh + 1])
                if masked:
                    p = jnp.where(causal, p, 0.0)
                dp = lax.dot_general(doh, vb, (((1,), (1,)), ((), ())), preferred_element_type=F32)
                ds = p * (dp - dl_ref[0, rows, hh:hh + 1])
                pb, dsb = p.astype(MXU_DTYPE), ds.astype(MXU_DTYPE)
                dvs.append(lax.dot_general(pb, dob, (((0,), (0,)), ((), ())), preferred_element_type=F32))
                dks.append(lax.dot_general(dsb, qs, (((0,), (0,)), ((), ())), preferred_element_type=F32))
                dqs.append(jnp.dot(dsb, kb, preferred_element_type=F32))
                dcs.append(dc_accs[hh] - jnp.sum(ds, axis=0, keepdims=True))
            dq_ref[rows, :] += jnp.where(first, dqs[0], dqs[1]) * scale
            return (dk_acc + jnp.where(first, dks[0], dks[1]), dv_acc + jnp.where(first, dvs[0], dvs[1]), tuple(dcs))

        nil = jnp.zeros((1, t), F32)
        carry = step(j, (jnp.zeros((t, LANES), F32), jnp.zeros((t, LANES), F32), (nil, nil)), True)
        dk_acc, dv_acc, dc_accs = lax.fori_loop(j + 1, nb, functools.partial(step, masked=False), carry)
        dk_ref[...] = dk_acc.astype(dk_ref.dtype)
        dv_ref[...] = dv_acc.astype(dv_ref.dtype)
        dc_ref[0] = jnp.concatenate(dc_accs, axis=0)

    return pl.pallas_call(
        body, name="fox_attn_bwd", grid=(np_, nb),
        in_specs=[pl.BlockSpec((s, LANES), lambda p, j: (0, p)),
                  pl.BlockSpec((t, LANES), lambda p, j: (j, np_ + p)),
                  pl.BlockSpec((t, LANES), lambda p, j: (j, 2 * np_ + p)),
                  pl.BlockSpec((s, LANES), lambda p, j: (0, p)),
                  pl.BlockSpec((1, s, 2), lambda p, j: (p, 0, 0)),
                  pl.BlockSpec((1, s, 2), lambda p, j: (p, 0, 0)),
                  pl.BlockSpec((1, 2, t), lambda p, j: (p, 0, j))],
        out_specs=[pl.BlockSpec((s, LANES), lambda p, j: (0, p)),
                   pl.BlockSpec((t, LANES), lambda p, j: (j, p)),
                   pl.BlockSpec((t, LANES), lambda p, j: (j, p)),
                   pl.BlockSpec((1, 2, t), lambda p, j: (p, 0, j))],
        out_shape=[jax.ShapeDtypeStruct((s, D_MODEL), F32), jax.ShapeDtypeStruct((s, D_MODEL), BF16),
                   jax.ShapeDtypeStruct((s, D_MODEL), BF16), jax.ShapeDtypeStruct((np_, 2, s), F32)],
        compiler_params=_params(("parallel", "arbitrary")))(qkv, qkv, qkv, do, lse, delta, c_row)


ATT_QUERIES = 512


def _lanes(a, width):
    return jnp.concatenate([a] * (width // LANES), axis=1)


def _attn_fwd_t(qkv, c_lanes):
    s = qkv.shape[0]
    t, tq = min(ATT_BLOCK, s), min(ATT_QUERIES, s)
    nq, per = s // tq, tq // t
    scale = FOX_HEAD_DIM ** -0.5
    np_ = HEAD_PAIRS
    nt = (((1,), (1,)), ((), ()))

    def body(q_ref, k_ref, v_ref, c_ref, o_ref, lse_ref, vt_ref):
        i = pl.program_id(1)

        @pl.when(i == 0)
        def _():
            for r in range(s // t):
                vt_ref[:, r * t:(r + 1) * t] = v_ref[r * t:(r + 1) * t, :].astype(F32).T.astype(vt_ref.dtype)

        first = _head_masks(tq)
        upper = lax.broadcasted_iota(jnp.int32, (LANES, tq), 0) < FOX_HEAD_DIM
        qs = q_ref[...] * scale
        zero = jnp.zeros_like(qs)
        qh = (jnp.where(first, qs, zero), jnp.where(first, zero, qs))
        key_at = lax.broadcasted_iota(jnp.int32, (t, tq), 0)
        query_at = lax.broadcasted_iota(jnp.int32, (t, tq), 1)

        def block(j, carry, diagonal=None):
            ms, ls, acc = carry
            start = pl.multiple_of(j * t, t)
            kb, vt = k_ref[pl.ds(start, t), :], vt_ref[:, pl.ds(start, t)]
            new_m, new_l, alphas, pvs = [], [], [], []
            for hh in range(2):
                sc = lax.dot_general(kb, qh[hh], nt, preferred_element_type=F32) - _lanes(c_ref[0, hh, pl.ds(start, t), :], tq)
                if diagonal is not None:
                    sc = jnp.where(key_at + diagonal * t <= query_at, sc, NEG_INF)
                m_new = jnp.maximum(ms[hh], jnp.max(sc, axis=0, keepdims=True))
                p = jnp.exp(sc - m_new)
                alpha = jnp.exp(ms[hh] - m_new)
                new_m.append(m_new)
                new_l.append(alpha * ls[hh] + jnp.sum(p, axis=0, keepdims=True))
                alphas.append(alpha)
                p_hi = p.astype(MXU_DTYPE)
                p_lo = (p - p_hi.astype(F32)).astype(MXU_DTYPE)
                pvs.append(jnp.dot(vt, p_hi, preferred_element_type=F32) + jnp.dot(vt, p_lo, preferred_element_type=F32))
            acc = jnp.where(upper, alphas[0], alphas[1]) * acc + jnp.where(upper, pvs[0], pvs[1])
            return tuple(new_m), tuple(new_l), acc

        neg, nil = jnp.full((1, tq), NEG_INF, F32), jnp.zeros((1, tq), F32)
        carry = lax.fori_loop(0, per * i, block, ((neg, neg), (nil, nil), jnp.zeros((LANES, tq), F32)))
        for d in range(per):
            carry = block(per * i + d, carry, diagonal=d)
        ms, ls, acc = carry
        o_ref[...] = (acc / jnp.where(upper, ls[0], ls[1])).T.astype(o_ref.dtype)
        lse_ref[0] = jnp.concatenate([ms[0] + jnp.log(ls[0]), ms[1] + jnp.log(ls[1])], axis=0)

    return pl.pallas_call(
        body, name="fox_attn_fwd", grid=(np_, nq),
        in_specs=[pl.BlockSpec((tq, LANES), lambda p, i: (i, p)),
                  pl.BlockSpec((s, LANES), lambda p, i: (0, np_ + p)),
                  pl.BlockSpec((s, LANES), lambda p, i: (0, 2 * np_ + p)),
                  pl.BlockSpec((1, 2, s, LANES), lambda p, i: (p, 0, 0, 0))],
        out_specs=[pl.BlockSpec((tq, LANES), lambda p, i: (i, p)),
                   pl.BlockSpec((1, 2, tq), lambda p, i: (p, 0, i))],
        out_shape=[jax.ShapeDtypeStruct((s, D_MODEL), F32), jax.ShapeDtypeStruct((np_, 2, s), F32)],
        scratch_shapes=[pltpu.VMEM((LANES, s), MXU_DTYPE)],
        compiler_params=_params(("parallel", "arbitrary")))(qkv, qkv, qkv, c_lanes)


def _attn_bwd_t(qkv, do, lse, delta, c_lanes):
    s = qkv.shape[0]
    t, tq = min(ATT_BLOCK, s), min(ATT_QUERIES, s)
    nb, nq, per = s // t, s // tq, tq // t
    scale = FOX_HEAD_DIM ** -0.5
    np_ = HEAD_PAIRS
    nt = (((1,), (1,)), ((), ()))

    def body(q_ref, k_ref, v_ref, do_ref, lse_ref, dl_ref, c_ref, dq_ref, dk_ref, dv_ref, dc_ref, dqt_ref):
        j = pl.program_id(1)
        first_q, first = _head_masks(tq), _head_masks(t)
        upper = lax.broadcasted_iota(jnp.int32, (LANES, tq), 0) < FOX_HEAD_DIM
        causal = (lax.broadcasted_iota(jnp.int32, (t, tq), 0) + (j % per) * t) <= lax.broadcasted_iota(jnp.int32, (t, tq), 1)
        kb, vb = k_ref[...], v_ref[...]
        kt = kb.astype(F32).T.astype(MXU_DTYPE)
        cb = (_lanes(c_ref[0, 0], tq), _lanes(c_ref[0, 1], tq))

        @pl.when(j == 0)
        def _():
            dqt_ref[...] = jnp.zeros_like(dqt_ref)

        def step(i, carry, masked):
            dk_acc, dv_acc, dc_accs = carry
            start = pl.multiple_of(i * tq, tq)
            qs = q_ref[pl.ds(start, tq), :] * scale
            dob = do_ref[pl.ds(start, tq), :]
            zero = jnp.zeros_like(qs)
            dks, dvs, dqs, dcs = [], [], [], []
            for hh in range(2):
                qh = jnp.where(first_q, qs, zero) if hh == 0 else jnp.where(first_q, zero, qs)
                doh = jnp.where(first_q, dob, zero) if hh == 0 else jnp.where(first_q, zero, dob)
                sc = lax.dot_general(kb, qh, nt, preferred_element_type=F32)
                p = jnp.exp(sc - cb[hh] - lse_ref[0, hh:hh + 1, pl.ds(start, tq)])
                if masked:
                    p = jnp.where(causal, p, 0.0)
                dp = lax.dot_general(vb, doh, nt, preferred_element_type=F32)
                ds = p * (dp - dl_ref[0, hh:hh + 1, pl.ds(start, tq)])
                pb, dsb = p.astype(MXU_DTYPE), ds.astype(MXU_DTYPE)
                dvs.append(jnp.dot(pb, dob, preferred_element_type=F32))
                dks.append(jnp.dot(dsb, qs, preferred_element_type=F32))
                dqs.append(jnp.dot(kt, dsb, preferred_element_type=F32))
                dcs.append(dc_accs[hh] - jnp.sum(ds, axis=1, keepdims=True))
            dqt_ref[:, pl.ds(start, tq)] += jnp.where(upper, dqs[0], dqs[1]) * scale
            return (dk_acc + jnp.where(first, dks[0], dks[1]), dv_acc + jnp.where(first, dvs[0], dvs[1]), tuple(dcs))

        nil, col = jnp.zeros((t, LANES), F32), jnp.zeros((t, 1), F32)
        carry = step(j // per, (nil, nil, (col, col)), True)
        dk_acc, dv_acc, dc_accs = lax.fori_loop(j // per + 1, nq, functools.partial(step, masked=False), carry)
        dk_ref[...] = dk_acc.astype(dk_ref.dtype)
        dv_ref[...] = dv_acc.astype(dv_ref.dtype)
        dc_ref[0, 0] = jnp.broadcast_to(dc_accs[0], (t, LANES))
        dc_ref[0, 1] = jnp.broadcast_to(dc_accs[1], (t, LANES))

        @pl.when(j == nb - 1)
        def _():
            for r in range(nb):
                dq_ref[r * t:(r + 1) * t, :] = dqt_ref[:, r * t:(r + 1) * t].T

    row = pl.BlockSpec((1, 2, s), lambda p, j: (p, 0, 0))
    return pl.pallas_call(
        body, name="fox_attn_bwd", grid=(np_, nb),
        in_specs=[pl.BlockSpec((s, LANES), lambda p, j: (0, p)),
                  pl.BlockSpec((t, LANES), lambda p, j: (j, np_ + p)),
                  pl.BlockSpec((t, LANES), lambda p, j: (j, 2 * np_ + p)),
                  pl.BlockSpec((s, LANES), lambda p, j: (0, p)), row, row,
                  pl.BlockSpec((1, 2, t, LANES), lambda p, j: (p, 0, j, 0))],
        out_specs=[pl.BlockSpec((s, LANES), lambda p, j: (0, p)),
                   pl.BlockSpec((t, LANES), lambda p, j: (j, p)),
                   pl.BlockSpec((t, LANES), lambda p, j: (j, p)),
                   pl.BlockSpec((1, 2, t, LANES), lambda p, j: (p, 0, j, 0))],
        out_shape=[jax.ShapeDtypeStruct((s, D_MODEL), F32), jax.ShapeDtypeStruct((s, D_MODEL), BF16),
                   jax.ShapeDtypeStruct((s, D_MODEL), BF16), jax.ShapeDtypeStruct((np_, 2, s, LANES), F32)],
        scratch_shapes=[pltpu.VMEM((LANES, s), F32)],
        compiler_params=_params(("parallel", "arbitrary")))(qkv, qkv, qkv, do, lse, delta, c_lanes)


def _head_sums(a, b, name):
    d = a.shape[1]
    sel = (jnp.arange(d)[:, None] // FOX_HEAD_DIM == jnp.arange(LANES)[None, :]).astype(F32)

    def fn(ab, bb, selb):
        prod = ab.astype(F32) * bb.astype(F32)
        return (jnp.dot(prod, selb, precision=lax.Precision.HIGHEST, preferred_element_type=F32),), ()
    return _ew(fn, [a, b], [sel], outs=[(LANES, F32)], name=name)[0]


def _pairs_col(a16):
    s = a16.shape[0]
    return a16.reshape(s, HEAD_PAIRS, 2).transpose(1, 0, 2)


def _fox_fwd(x, gain, w_qkv, w_f, b_f, w_out):
    s = x.shape[0]
    h = _rms_fwd(x, gain, "mix_norm")
    qkv = _mm(h, w_qkv, name="fox_qkv", out_dtype=BF16)
    fl = _mm(h, w_f, name="fox_f", tn=LANES)
    z_row, c_rowf = _fox_gate_fwd(fl[:, :FOX_HEADS].T, b_f.reshape(FOX_HEADS, 1))
    c_lanes = jnp.broadcast_to(c_rowf.reshape(HEAD_PAIRS, 2, s, 1), (HEAD_PAIRS, 2, s, LANES))
    o, lse = _attn_fwd_t(qkv, c_lanes)
    x1 = _mm(o, w_out, name="fox_out", add=x)
    return x1, (x, h, qkv, z_row, c_lanes, o, lse)


def _fox_bwd(dx1, saved, gain, w_qkv, w_f, w_out):
    x, h, qkv, z_row, c_lanes, o, lse = saved
    s = x.shape[0]
    do = _mm(dx1, w_out, tb=True, name="fox_do", out_dtype=BF16)
    dw_out = _mm(o, dx1, ta=True, name="fox_dw_out", out_dtype=BF16)
    delta = _head_sums(do, o, "fox_delta")[:, :FOX_HEADS].T.reshape(HEAD_PAIRS, 2, s)
    dq, dk, dv, dc = _attn_bwd_t(qkv, do, lse, delta, c_lanes)
    dz_row, db = _fox_gate_bwd(dc[..., 0].reshape(FOX_HEADS, s), z_row)
    dqkv = jnp.concatenate([dq.astype(BF16), dk, dv], axis=1)
    dfl = jnp.pad(dz_row.T, ((0, 0), (0, LANES - FOX_HEADS))).astype(BF16)
    dw_qkv = _mm(h, dqkv, ta=True, name="fox_dw_qkv", out_dtype=BF16)
    dw_f = _mm(h, dfl, ta=True, name="fox_dw_f", out_dtype=BF16, tn=LANES)
    dh = _mm(dqkv, w_qkv, tb=True, name="fox_dh_qkv")
    dx, dgain = _mm_rms_bwd(dfl, w_f, x, gain, dx1, name="fox_dh_f", add=dh)
    dw_in = jnp.concatenate([dw_qkv, dw_f[:, :FOX_HEADS]], axis=1)
    return dx, dgain, dw_in, db.reshape(FOX_HEADS), dw_out


S5_ROWS = 512
SCAN_CHUNKS = SUBLANES


def _s5_operands(a_re, a_im, log_dt, b_re, b_im, c_re, c_im):
    dt = jnp.exp(log_dt)[:, None]
    mag, ang = jnp.exp(a_re * dt), a_im * dt
    lr, li = mag * jnp.cos(ang), mag * jnp.sin(ang)
    den = a_re * a_re + a_im * a_im
    cr = ((lr - 1.0) * a_re + li * a_im) / den
    ci = (li * a_re - (lr - 1.0) * a_im) / den
    bbr = cr[..., None] * b_re - ci[..., None] * b_im
    bbi = cr[..., None] * b_im + ci[..., None] * b_re
    nb = S5_BLOCKS
    lam = jnp.stack([lr.reshape(nb, 2, S5_HALF), li.reshape(nb, 2, S5_HALF)], axis=2)
    eye4, eye2 = jnp.eye(4, dtype=F32), jnp.eye(2, dtype=F32)
    bb = jnp.stack([bbr, bbi], axis=0).reshape(2, nb, 2, 4, S5_STATE, S5_GROUP)
    bmat = jnp.einsum("rbhgpc,kg,jh->bhjkcrgp", bb, eye4, eye2).reshape(nb, 2, 128, 2 * S5_HALF)
    cc = jnp.stack([c_re, -c_im], axis=0).reshape(2, nb, 2, 4, S5_GROUP, S5_STATE)
    cmat = jnp.einsum("rbhgcp,kg,jh->bhrgpjkc", cc, eye4, eye2).reshape(nb, 2, 2 * S5_HALF, 128)
    return lam, bmat, cmat


def _time_to_scan_order(a):
    s, d = a.shape
    return a.reshape(SCAN_CHUNKS, s // SCAN_CHUNKS, d).transpose(1, 0, 2).reshape(s, d)


def _scan_to_time_order(a):
    s, d = a.shape
    return a.reshape(s // SCAN_CHUNKS, SCAN_CHUNKS, d).transpose(1, 0, 2).reshape(s, d)


def _scan_chunks(xr_ref, xi_ref, lr, li, nst, reverse, after_step=None):
    lanes = lr.shape[1]
    lr8, li8 = jnp.broadcast_to(lr, (SUBLANES, lanes)), jnp.broadcast_to(li, (SUBLANES, lanes))
    zero8 = jnp.zeros((SUBLANES, lanes), F32)

    def rows_of(n):
        s = (nst - 1 - n) if reverse else n
        return s, pl.ds(pl.multiple_of(s * SUBLANES, SUBLANES), SUBLANES)

    def local(n, carry):
        pr, pi = carry
        _, rows = rows_of(n)
        nr = lr8 * pr - li8 * pi + xr_ref[rows, :]
        ni = lr8 * pi + li8 * pr + xi_ref[rows, :]
        xr_ref[rows, :] = nr
        xi_ref[rows, :] = ni
        return nr, ni

    er, ei = lax.fori_loop(0, nst, local, (zero8, zero8))
    pr, pi = lr, li
    for _ in range(int(math.log2(nst))):
        pr, pi = pr * pr - pi * pi, 2.0 * pr * pi
    tr = ti = jnp.zeros((1, lanes), F32)
    ent_r, ent_i = [None] * SCAN_CHUNKS, [None] * SCAN_CHUNKS
    for k in (reversed(range(SCAN_CHUNKS)) if reverse else range(SCAN_CHUNKS)):
        ent_r[k], ent_i[k] = tr, ti
        tr, ti = er[k:k + 1] + (pr * tr - pi * ti), ei[k:k + 1] + (pr * ti + pi * tr)
    in_r, in_i = jnp.concatenate(ent_r, axis=0), jnp.concatenate(ent_i, axis=0)

    def fix(n, carry):
        wr, wi = carry
        s, rows = rows_of(n)
        nr = xr_ref[rows, :] + (wr * in_r - wi * in_i)
        ni = xi_ref[rows, :] + (wr * in_i + wi * in_r)
        xr_ref[rows, :] = nr
        xi_ref[rows, :] = ni
        if after_step is not None:
            after_step(s, nr, ni)
        return wr * lr8 - wi * li8, wr * li8 + wi * lr8

    lax.fori_loop(0, nst, fix, (lr8, li8))
    return in_r, in_i


def _s5_fill_states(u_ref, bm, xr_ref, xi_ref, s):
    rc = min(S5_ROWS, s)

    def fill(r, _):
        rows = pl.ds(pl.multiple_of(r * rc, rc), rc)
        bu = jnp.dot(u_ref[rows, :].astype(MXU_DTYPE), bm, preferred_element_type=F32)
        xr_ref[rows, :] = bu[:, :S5_HALF]
        xi_ref[rows, :] = bu[:, S5_HALF:]
        return 0
    lax.fori_loop(0, s // rc, fill, 0)


def _s5_specs():
    return [pl.BlockSpec((1, 2, 2, S5_HALF), lambda b: (b, 0, 0, 0)),
            pl.BlockSpec((1, 2, 128, 2 * S5_HALF), lambda b: (b, 0, 0, 0)),
            pl.BlockSpec((1, 2, 2 * S5_HALF, 128), lambda b: (b, 0, 0, 0)),
            pl.BlockSpec((1, LANES), lambda b: (0, b))]


def _s5_scan_fwd(u, lam, bmat, cmat, dvec):
    s = u.shape[0]
    nst = s // SCAN_CHUNKS
    rc = min(S5_ROWS, s)

    def body(u_ref, lam_ref, b_ref, c_ref, d_ref, y_ref, xr_ref, xi_ref):
        y_ref[...] = u_ref[...] * d_ref[...]
        for hb in range(2):
            _s5_fill_states(u_ref, b_ref[0, hb], xr_ref, xi_ref, s)
            _scan_chunks(xr_ref, xi_ref, lam_ref[0, hb, 0:1, :], lam_ref[0, hb, 1:2, :], nst, False)
            cm = c_ref[0, hb]

            def emit(r, _, cm=cm):
                rows = pl.ds(pl.multiple_of(r * rc, rc), rc)
                y_ref[rows, :] += (jnp.dot(xr_ref[rows, :].astype(MXU_DTYPE), cm[:S5_HALF], preferred_element_type=F32)
                                   + jnp.dot(xi_ref[rows, :].astype(MXU_DTYPE), cm[S5_HALF:], preferred_element_type=F32))
                return 0
            lax.fori_loop(0, s // rc, emit, 0)

    blk = pl.BlockSpec((s, LANES), lambda b: (0, b))
    return pl.pallas_call(
        body, name="s5_scan_fwd", grid=(S5_BLOCKS,), in_specs=[blk] + _s5_specs(), out_specs=blk,
        out_shape=jax.ShapeDtypeStruct(u.shape, F32),
        scratch_shapes=[pltpu.VMEM((s, S5_HALF), F32)] * 2,
        compiler_params=_params(("parallel",)))(u, lam, bmat, cmat, dvec)


def _s5_scan_bwd(u, dy, lam, bmat, cmat, dvec):
    s = u.shape[0]
    nst = s // SCAN_CHUNKS
    rc = min(S5_ROWS, s)
    nt = (((1,), (1,)), ((), ()))
    tn = (((0,), (0,)), ((), ()))

    def body(u_ref, dy_ref, lam_ref, b_ref, c_ref, d_ref, du_ref, db_ref, dc_ref, dl_ref, dd_ref,
             xr_ref, xi_ref, gr_ref, gi_ref, ar_ref, ai_ref):
        du_ref[...] = dy_ref[...] * d_ref[...]
        dd_ref[...] = jnp.sum(dy_ref[...] * u_ref[...], axis=0, keepdims=True)
        db_ref[...] = jnp.zeros_like(db_ref)
        dc_ref[...] = jnp.zeros_like(dc_ref)
        for hb in range(2):
            bm, cm = b_ref[0, hb], c_ref[0, hb]
            lr, li = lam_ref[0, hb, 0:1, :], lam_ref[0, hb, 1:2, :]
            _s5_fill_states(u_ref, bm, xr_ref, xi_ref, s)
            xin_r, xin_i = _scan_chunks(xr_ref, xi_ref, lr, li, nst, False)

            def fill_g(r, _, cm=cm):
                rows = pl.ds(pl.multiple_of(r * rc, rc), rc)
                g = lax.dot_general(dy_ref[rows, :].astype(MXU_DTYPE), cm, nt, preferred_element_type=F32)
                gr_ref[rows, :] = g[:, :S5_HALF]
                gi_ref[rows, :] = g[:, S5_HALF:]
                return 0
            lax.fori_loop(0, s // rc, fill_g, 0)
            ar_ref[...] = jnp.zeros_like(ar_ref)
            ai_ref[...] = jnp.zeros_like(ai_ref)

            def lam_grad(st, g_r, g_i, xin_r=xin_r, xin_i=xin_i):
                prev = pl.ds(pl.multiple_of(jnp.maximum(st - 1, 0) * SUBLANES, SUBLANES), SUBLANES)
                x_r = jnp.where(st > 0, xr_ref[prev, :], xin_r)
                x_i = jnp.where(st > 0, xi_ref[prev, :], xin_i)
                ar_ref[...] += g_r * x_r + g_i * x_i
                ai_ref[...] += g_i * x_r - g_r * x_i

            _scan_chunks(gr_ref, gi_ref, lr, -li, nst, True, after_step=lam_grad)
            dl_ref[0, hb] = jnp.concatenate([jnp.sum(ar_ref[...], axis=0, keepdims=True),
                                             jnp.sum(ai_ref[...], axis=0, keepdims=True)], axis=0)

            def emit(r, _, bm=bm, hb=hb):
                rows = pl.ds(pl.multiple_of(r * rc, rc), rc)
                g = jnp.concatenate([gr_ref[rows, :], gi_ref[rows, :]], axis=1).astype(MXU_DTYPE)
                x = jnp.concatenate([xr_ref[rows, :], xi_ref[rows, :]], axis=1).astype(MXU_DTYPE)
                du_ref[rows, :] += lax.dot_general(g, bm, nt, preferred_element_type=F32)
                db_ref[0, hb] += lax.dot_general(u_ref[rows, :].astype(MXU_DTYPE), g, tn, preferred_element_type=F32)
                dc_ref[0, hb] += lax.dot_general(dy_ref[rows, :].astype(MXU_DTYPE), x, tn, preferred_element_type=F32)
                return 0
            lax.fori_loop(0, s // rc, emit, 0)

    blk = pl.BlockSpec((s, LANES), lambda b: (0, b))
    mat = pl.BlockSpec((1, 2, 128, 2 * S5_HALF), lambda b: (b, 0, 0, 0))
    return pl.pallas_call(
        body, name="s5_scan_bwd", grid=(S5_BLOCKS,), in_specs=[blk, blk] + _s5_specs(),
        out_specs=[blk, mat, mat, pl.BlockSpec((1, 2, 2, S5_HALF), lambda b: (b, 0, 0, 0)),
                   pl.BlockSpec((1, LANES), lambda b: (0, b))],
        out_shape=[jax.ShapeDtypeStruct(u.shape, F32),
                   jax.ShapeDtypeStruct((S5_BLOCKS, 2, 128, 2 * S5_HALF), F32),
                   jax.ShapeDtypeStruct((S5_BLOCKS, 2, 128, 2 * S5_HALF), F32),
                   jax.ShapeDtypeStruct((S5_BLOCKS, 2, 2, S5_HALF), F32),
                   jax.ShapeDtypeStruct((1, D_MODEL), F32)],
        scratch_shapes=[pltpu.VMEM((s, S5_HALF), F32)] * 4 + [pltpu.VMEM((SUBLANES, S5_HALF), F32)] * 2,
        compiler_params=_params(("parallel",)))(u, dy, lam, bmat, cmat, dvec)


_GELU_C = math.sqrt(2.0 / math.pi)


def _gelu_parts(y):
    inner = _GELU_C * (y + 0.044715 * y * y * y)
    th = jnp.tanh(inner)
    return 0.5 * y * (1.0 + th), th


def _s5_fwd(x, gain, w_in, ssm, dvec, w_glu):
    lam, bmat, cmat = ssm
    h = _rms_fwd(x, gain, "mix_norm")
    u = _mm(h, w_in, name="s5_in")
    y = _scan_to_time_order(_s5_scan_fwd(_time_to_scan_order(u), lam, bmat.astype(MXU_DTYPE), cmat.astype(MXU_DTYPE), dvec))
    g = _ew(lambda yb: ((_gelu_parts(yb)[0],), ()), [y], outs=[(D_MODEL, BF16)], name="s5_gelu")[0]
    vg = _mm(g, w_glu, name="s5_glu", out_dtype=BF16)

    def glu_fn(vb, gb, xb):
        return (xb + vb.astype(F32) * _sigmoid(gb.astype(F32)),), ()
    x1 = _ew(glu_fn, [(vg, D_MODEL, 0), (vg, D_MODEL, 1), x], outs=[(D_MODEL, F32)], name="s5_gate")[0]
    return x1, (x, h, u, y, g, vg)


def _s5_bwd(dx1, saved, gain, w_in, ssm, dvec, w_glu):
    x, h, u, y, g, vg = saved
    lam, bmat, cmat = ssm

    def dglu_fn(db, vb, gb):
        vf, sg = vb.astype(F32), _sigmoid(gb.astype(F32))
        return (jnp.concatenate([db * sg, db * vf * sg * (1.0 - sg)], axis=1),), ()
    dvg = _ew(dglu_fn, [dx1, (vg, D_MODEL, 0), (vg, D_MODEL, 1)], outs=[(2 * D_MODEL, BF16)], name="s5_dgate")[0]
    dw_glu = _mm(g, dvg, ta=True, name="s5_dw_glu", out_dtype=BF16)
    dg = _mm(dvg, w_glu, tb=True, name="s5_dg")

    def dgelu_fn(dgb, yb):
        _, th = _gelu_parts(yb)
        dinner = _GELU_C * (1.0 + 3.0 * 0.044715 * yb * yb)
        return (dgb * (0.5 * (1.0 + th) + 0.5 * yb * (1.0 - th * th) * dinner),), ()
    dy = _ew(dgelu_fn, [dg, y], outs=[(D_MODEL, F32)], name="s5_dgelu")[0]
    du_s, dbm, dct, dlam, ddvec = _s5_scan_bwd(_time_to_scan_order(u), _time_to_scan_order(dy), lam,
                                               bmat.astype(MXU_DTYPE), cmat.astype(MXU_DTYPE), dvec)
    du = _scan_to_time_order(du_s)
    dw_in = _mm(h, du, ta=True, name="s5_dw_in", out_dtype=BF16)
    dx, dgain = _mm_rms_bwd(du, w_in, x, gain, dx1, name="s5_dh")
    return dx, dgain, dw_in, (dlam, dbm, jnp.swapaxes(dct, 2, 3)), ddvec, dw_glu


POOL_BLOCK = 256
N_POOL_GROUPS = len(POOL_WINDOWS)


def _pool_bands(gi, i, t):
    w = jnp.left_shift(2, gi)
    r = lax.broadcasted_iota(jnp.int32, (t, t), 0)
    c = lax.broadcasted_iota(jnp.int32, (t, t), 1)
    inside = ((c <= r) & (c > r - w)).astype(MXU_DTYPE)
    before = (c > r - w + t).astype(MXU_DTYPE)

    def inv_count(block):
        pos = block * t + lax.broadcasted_iota(jnp.int32, (t, 1), 0)
        return 1.0 / jnp.minimum(pos + 1, w).astype(F32)
    return inside, before, inv_count


def _pool_fwd(x, gain, w_grp, b_grp, scale):
    s = x.shape[0]
    t = min(POOL_BLOCK, s)
    h = _rms_fwd(x, gain, "mix_norm")

    def body(h_ref, hp_ref, w_ref, b_ref, sc_ref, x_ref, x1_ref, diff_ref):
        gi, i = pl.program_id(0), pl.program_id(1)
        inside, before, inv_count = _pool_bands(gi, i, t)
        hc = h_ref[...]
        tot = jnp.dot(inside, hc.astype(MXU_DTYPE), preferred_element_type=F32)
        prev = jnp.dot(before, hp_ref[...].astype(MXU_DTYPE), preferred_element_type=F32)
        tot = tot + jnp.where(i > 0, prev, 0.0)
        diff = (tot * inv_count(i) - hc.astype(F32)).astype(diff_ref.dtype)
        y = (jnp.dot(diff.astype(MXU_DTYPE), w_ref[0], preferred_element_type=F32) + b_ref[...]) * sc_ref[...]
        diff_ref[...] = diff
        x1_ref[...] = x_ref[...] + y

    blk = pl.BlockSpec((t, POOL_WIDTH), lambda gi, i: (i, gi))
    vec = pl.BlockSpec((1, POOL_WIDTH), lambda gi, i: (0, gi))
    x1, diff = pl.pallas_call(
        body, name="pool_fwd", grid=(N_POOL_GROUPS, s // t),
        in_specs=[blk, pl.BlockSpec((t, POOL_WIDTH), lambda gi, i: (jnp.maximum(i - 1, 0), gi)),
                  pl.BlockSpec((1, POOL_WIDTH, POOL_WIDTH), lambda gi, i: (gi, 0, 0)), vec, vec, blk],
        out_specs=[blk, blk],
        out_shape=[jax.ShapeDtypeStruct(x.shape, F32), jax.ShapeDtypeStruct(x.shape, BF16)],
        compiler_params=_params(("parallel", "arbitrary")))(h, h, w_grp, b_grp, scale, x)
    return x1, (x, diff)


def _pool_bwd(dx1, saved, gain, w_grp, b_grp, scale):
    x, diff = saved
    s = x.shape[0]
    t = min(POOL_BLOCK, s)
    nb = s // t

    def body1(dx_ref, diff_ref, w_ref, b_ref, sc_ref, dd_ref, dw_ref, db_ref, dsc_ref):
        i = pl.program_id(1)

        @pl.when(i == 0)
        def _():
            dw_ref[...] = jnp.zeros_like(dw_ref)
            db_ref[...] = jnp.zeros_like(db_ref)
            dsc_ref[...] = jnp.zeros_like(dsc_ref)

        dfb = diff_ref[...].astype(MXU_DTYPE)
        ypre = jnp.dot(dfb, w_ref[0], preferred_element_type=F32) + b_ref[...]
        dxb = dx_ref[...]
        dy = dxb * sc_ref[...]
        dsc_ref[...] += jnp.sum(dxb * ypre, axis=0, keepdims=True)
        db_ref[...] += jnp.sum(dy, axis=0, keepdims=True)
        dyb = dy.astype(MXU_DTYPE)
        dw_ref[0] += lax.dot_general(dfb, dyb, (((0,), (0,)), ((), ())), preferred_element_type=F32)
        dd_ref[...] = lax.dot_general(dyb, w_ref[0], (((1,), (1,)), ((), ())), preferred_element_type=F32)

    blk = pl.BlockSpec((t, POOL_WIDTH), lambda gi, i: (i, gi))
    vec = pl.BlockSpec((1, POOL_WIDTH), lambda gi, i: (0, gi))
    mat = pl.BlockSpec((1, POOL_WIDTH, POOL_WIDTH), lambda gi, i: (gi, 0, 0))
    ddiff, dw, db, dsc = pl.pallas_call(
        body1, name="pool_bwd_map", grid=(N_POOL_GROUPS, nb), in_specs=[blk, blk, mat, vec, vec],
        out_specs=[blk, mat, vec, vec],
        out_shape=[jax.ShapeDtypeStruct(x.shape, F32), jax.ShapeDtypeStruct(w_grp.shape, F32),
                   jax.ShapeDtypeStruct((1, D_MODEL), F32), jax.ShapeDtypeStruct((1, D_MODEL), F32)],
        compiler_params=_params(("parallel", "arbitrary")))(dx1, diff, w_grp, b_grp, scale)

    def body2(dc_ref, dn_ref, dh_ref):
        gi, i = pl.program_id(0), pl.program_id(1)
        inside, before, inv_count = _pool_bands(gi, i, t)
        tn = (((0,), (0,)), ((), ()))
        dc = dc_ref[...]
        tot = lax.dot_general(inside, (dc * inv_count(i)).astype(MXU_DTYPE), tn, preferred_element_type=F32)
        nxt = lax.dot_general(before, (dn_ref[...] * inv_count(i + 1)).astype(MXU_DTYPE), tn, preferred_element_type=F32)
        dh_ref[...] = tot + jnp.where(i < nb - 1, nxt, 0.0) - dc

    dh = pl.pallas_call(
        body2, name="pool_bwd_window", grid=(N_POOL_GROUPS, nb),
        in_specs=[blk, pl.BlockSpec((t, POOL_WIDTH), lambda gi, i: (jnp.minimum(i + 1, nb - 1), gi))],
        out_specs=blk, out_shape=jax.ShapeDtypeStruct(x.shape, F32),
        compiler_params=_params(("parallel", "parallel")))(ddiff, ddiff)
    dx, dgain = _rms_bwd(x, gain, dh, dx1, "mix_norm_bwd")
    return dx, dgain, dw, db, dsc


MESH_ID = pl.DeviceIdType.MESH
ANY = pl.BlockSpec(memory_space=pl.ANY)


def _place():
    x, y, c = lax.axis_index("x"), lax.axis_index("y"), lax.axis_index("c")
    other_chips = [(1 - x, y), (x, 1 - y), (1 - x, 1 - y)]
    return x, y, c, other_chips


def _chip_index(chip):
    return 2 * chip[0] + chip[1]


def _remote(src, dst, send_sems, recv_sems, n, to):
    return pltpu.make_async_remote_copy(src_ref=src, dst_ref=dst, send_sem=send_sems.at[n], recv_sem=recv_sems.at[n],
                                        device_id=to, device_id_type=MESH_ID)


def _gather_weights(ws):
    n = len(ws)

    def body(*refs):
        w_refs, out_refs = refs[:n], refs[n:2 * n]
        send_sems, recv_sems = refs[2 * n:]
        x, y, c, chips = _place()
        k = _chip_index((x, y))
        me, sibling = (x, y, c), (x, y, 1 - c)
        first = [_remote(w_refs[t].at[c], out_refs[t].at[k, c], send_sems, recv_sems, 6 * t + j, (*chip, c))
                 for j, chip in enumerate(chips) for t in range(n)]
        for cp in first:
            cp.start()
        passed = []
        for j, chip in enumerate(chips):
            for t in range(n):
                landed = out_refs[t].at[_chip_index(chip), c]
                _remote(landed, landed, send_sems, recv_sems, 6 * t + j, me).wait_recv()
                passed.append(_remote(landed, landed, send_sems, recv_sems, 6 * t + 3 + j, sibling))
                passed[-1].start()
        for j, chip in enumerate(chips):
            for t in range(n):
                theirs = out_refs[t].at[_chip_index(chip), 1 - c]
                _remote(theirs, theirs, send_sems, recv_sems, 6 * t + 3 + j, me).wait_recv()
        for cp in first + passed:
            cp.wait_send()

    return pl.pallas_call(
        body, name="gather_weights", in_specs=[ANY] * n, out_specs=[ANY] * n,
        out_shape=[jax.ShapeDtypeStruct((N_CHIPS,) + w.shape, w.dtype) for w in ws],
        scratch_shapes=[pltpu.SemaphoreType.DMA((6 * n,)), pltpu.SemaphoreType.DMA((6 * n,))],
    )(*ws)


def _swap_halves(gs):
    n = len(gs)

    def body(*refs):
        g_refs, out_refs = refs[:n], refs[n:2 * n]
        send_sems, recv_sems = refs[2 * n:]
        x, y, c, _ = _place()
        copies = [_remote(g_refs[t].at[s, 1 - c], out_refs[t].at[s], send_sems, recv_sems, N_CHIPS * t + s, (x, y, 1 - c))
                  for t in range(n) for s in range(N_CHIPS)]
        for cp in copies:
            cp.start()
        for cp in copies:
            cp.wait_recv()
        for cp in copies:
            cp.wait_send()

    return pl.pallas_call(
        body, name="swap_halves", in_specs=[ANY] * n, out_specs=[ANY] * n,
        out_shape=[jax.ShapeDtypeStruct((N_CHIPS,) + g.shape[2:], g.dtype) for g in gs],
        scratch_shapes=[pltpu.SemaphoreType.DMA((N_CHIPS * n,)), pltpu.SemaphoreType.DMA((N_CHIPS * n,))],
    )(*gs)


def _scatter_partials(ps):
    n = len(ps)

    def body(*refs):
        p_refs, out_refs = refs[:n], refs[n:2 * n]
        send_sems, recv_sems = refs[2 * n:]
        x, y, c, chips = _place()
        sends = [_remote(p_refs[t].at[_chip_index(chip)], out_refs[t].at[j], send_sems, recv_sems, 3 * t + j, (*chip, c))
                 for j, chip in enumerate(chips) for t in range(n)]
        for cp in sends:
            cp.start()
        for j in range(3):
            for t in range(n):
                slot = out_refs[t].at[j]
                _remote(slot, slot, send_sems, recv_sems, 3 * t + j, (x, y, c)).wait_recv()
        for cp in sends:
            cp.wait_send()

    return pl.pallas_call(
        body, name="scatter_partials", in_specs=[ANY] * n, out_specs=[ANY] * n,
        out_shape=[jax.ShapeDtypeStruct((3,) + p.shape[1:], p.dtype) for p in ps],
        scratch_shapes=[pltpu.SemaphoreType.DMA((3 * n,)), pltpu.SemaphoreType.DMA((3 * n,))],
    )(*ps)


def _share_half(fs):
    n = len(fs)

    def body(*refs):
        f_refs, out_refs = refs[:n], refs[n:2 * n]
        send_sems, recv_sems = refs[2 * n:]
        x, y, c, _ = _place()
        sends = [_remote(f_refs[t], out_refs[t].at[c], send_sems, recv_sems, t, (x, y, 1 - c)) for t in range(n)]
        for cp in sends:
            cp.start()
        for t in range(n):
            theirs = out_refs[t].at[1 - c]
            _remote(theirs, theirs, send_sems, recv_sems, t, (x, y, c)).wait_recv()
        for cp in sends:
            cp.wait_send()

    return pl.pallas_call(
        body, name="share_half", in_specs=[ANY] * n, out_specs=[ANY] * n,
        out_shape=[jax.ShapeDtypeStruct((2,) + f.shape, f.dtype) for f in fs],
        scratch_shapes=[pltpu.SemaphoreType.DMA((n,)), pltpu.SemaphoreType.DMA((n,))],
    )(*fs)


def _gather_small(v):
    def body(v_ref, out_ref, send_sems, recv_sems):
        x, y, c, chips = _place()
        me, sibling = (x, y, c), (x, y, 1 - c)

        def slot(px, py, pc):
            return out_ref.at[4 * px + 2 * py + pc]

        first = [_remote(v_ref, slot(*me), send_sems, recv_sems, 0, sibling)]
        first += [_remote(v_ref, slot(*me), send_sems, recv_sems, 1 + j, (*chip, c)) for j, chip in enumerate(chips)]
        for cp in first:
            cp.start()
        passed = [_remote(slot(*chip, c), slot(*chip, c), send_sems, recv_sems, 4 + j, sibling)
                  for j, chip in enumerate(chips)]
        for j, chip in enumerate(chips):
            _remote(slot(*chip, c), slot(*chip, c), send_sems, recv_sems, 1 + j, me).wait_recv()
            passed[j].start()
        _remote(slot(*sibling), slot(*sibling), send_sems, recv_sems, 0, me).wait_recv()
        for j, chip in enumerate(chips):
            _remote(slot(*chip, 1 - c), slot(*chip, 1 - c), send_sems, recv_sems, 4 + j, me).wait_recv()
        for cp in first + passed:
            cp.wait_send()

    gathered = pl.pallas_call(
        body, name="gather_small", in_specs=[ANY], out_specs=ANY,
        out_shape=jax.ShapeDtypeStruct((N_DEV,) + v.shape, v.dtype),
        scratch_shapes=[pltpu.SemaphoreType.DMA((7,)), pltpu.SemaphoreType.DMA((7,))],
    )(v)
    device = 4 * lax.axis_index("x") + 2 * lax.axis_index("y") + lax.axis_index("c")
    return lax.dynamic_update_index_in_dim(gathered, v, device, 0)


SMALL_ROWS = 256
SUM_ROWS = 256
BF16_ROWS = 16


def _row_tile(rows, want):
    for t in range(min(rows, want) // BF16_ROWS * BF16_ROWS, 0, -BF16_ROWS):
        if rows % t == 0:
            return t
    return rows


def _pair_sum(g, r, core, name):
    rows, cols = g.shape[2:]
    tr = _row_tile(rows, SUM_ROWS)

    def body(c_ref, g_ref, r_ref, o_ref):
        o_ref[0] = (g_ref[0, 0].astype(F32) + r_ref[0].astype(F32)).astype(o_ref.dtype)

    return pl.pallas_call(
        body, name=name,
        grid_spec=pltpu.PrefetchScalarGridSpec(
            num_scalar_prefetch=1, grid=(N_CHIPS, rows // tr),
            in_specs=[pl.BlockSpec((1, 1, tr, cols), lambda s, i, c_ref: (s, c_ref[0], i, 0)),
                      pl.BlockSpec((1, tr, cols), lambda s, i, c_ref: (s, i, 0))],
            out_specs=pl.BlockSpec((1, tr, cols), lambda s, i, c_ref: (s, i, 0))),
        out_shape=jax.ShapeDtypeStruct(r.shape, BF16),
        compiler_params=_params(("parallel", "parallel")))(core, g, r)


def _chip_sum(p, recv, chip, name):
    rows, cols = p.shape[1:]
    tr = _row_tile(rows, SUM_ROWS)

    def body(k_ref, p_ref, r_ref, o_ref):
        acc = p_ref[0].astype(F32)
        for j in range(3):
            acc = acc + r_ref[j].astype(F32)
        o_ref[...] = acc

    return pl.pallas_call(
        body, name=name,
        grid_spec=pltpu.PrefetchScalarGridSpec(
            num_scalar_prefetch=1, grid=(rows // tr,),
            in_specs=[pl.BlockSpec((1, tr, cols), lambda i, k_ref: (k_ref[0], i, 0)),
                      pl.BlockSpec((3, tr, cols), lambda i, k_ref: (0, i, 0))],
            out_specs=pl.BlockSpec((tr, cols), lambda i, k_ref: (i, 0))),
        out_shape=jax.ShapeDtypeStruct((rows, cols), F32),
        compiler_params=_params(("parallel",)))(chip, p, recv)


def _sum_blocks(a, name):
    n, rows, cols = a.shape
    tr = _row_tile(rows, SUM_ROWS)

    def body(a_ref, o_ref):
        acc = a_ref[0].astype(F32)
        for s in range(1, n):
            acc = acc + a_ref[s].astype(F32)
        o_ref[...] = acc

    return pl.pallas_call(
        body, name=name, grid=(rows // tr,),
        in_specs=[pl.BlockSpec((n, tr, cols), lambda i: (0, i, 0))],
        out_specs=pl.BlockSpec((tr, cols), lambda i: (i, 0)),
        out_shape=jax.ShapeDtypeStruct((rows, cols), F32),
        compiler_params=_params(("parallel",)))(a)


def _adamw(w, g, m, v, name):
    def fn(wb, gb, mb, vb):
        m2 = ADAM_B1 * mb + (1.0 - ADAM_B1) * gb
        v2 = ADAM_B2 * vb + (1.0 - ADAM_B2) * (gb * gb)
        m_hat = m2 / (1.0 - ADAM_B1 ** ADAM_STEP)
        v_hat = v2 / (1.0 - ADAM_B2 ** ADAM_STEP)
        delta = -ADAM_LR * (m_hat / (jnp.sqrt(v_hat) + ADAM_EPS) + ADAM_WD * wb)
        return (delta, m2, v2), ()
    c = w.shape[1]
    return _ew(fn, [w, g, m, v], outs=[(c, F32)] * 3, name=name)


WEIGHTS = ["mix_norm_g", "ffn_norm_g", "final_norm_g", "fox_w_in", "fox_b_f", "fox_w_out", "s5_w_in", "s5_a_re",
           "s5_a_im", "s5_log_dt", "s5_b_re", "s5_b_im", "s5_c_re", "s5_c_im", "s5_d", "s5_w_glu", "pool_w",
           "pool_b", "pool_scale", "ffn_w_gate_up", "ffn_w_down"]
BIG = {"fox_w_in": 2, "fox_w_out": 1, "s5_w_in": 1, "s5_w_glu": 2, "pool_w": 2, "ffn_w_gate_up": 2, "ffn_w_down": 1}
SLICED = ("pool_b", "pool_scale")
SMALL = [n for n in WEIGHTS if n not in BIG]


def _to_natural(cm, axis):
    moved = jnp.moveaxis(cm, 0, axis)
    shape = moved.shape[:axis] + (moved.shape[axis] * moved.shape[axis + 1],) + moved.shape[axis + 2:]
    return moved.reshape(shape)


def _to_chip_major(nat, axis):
    shape = nat.shape[:axis] + (N_CHIPS, nat.shape[axis] // N_CHIPS) + nat.shape[axis + 1:]
    return jnp.moveaxis(nat.reshape(shape), axis, 0)


def _halves_view(shape):
    return (2, int(np.prod(shape[:-1])) // 2, shape[-1])


def _pack_small(parts):
    flat = jnp.concatenate([p.reshape(-1).astype(F32) for p in parts])
    pad = (-flat.shape[0]) % (SMALL_ROWS * LANES)
    return jnp.pad(flat, (0, pad)).reshape(-1, LANES)


def _unpack_small(buf, shapes):
    flat = buf.reshape(-1)
    out, off = [], 0
    for shp in shapes:
        n = int(np.prod(shp))
        out.append(flat[off:off + n].reshape(shp))
        off += n
    return out


def _local_step(x, target, w):
    grads = {}
    mixers = ("fox", "s5", "pool")
    saved = []
    ssm, ssm_pull = jax.vjp(_s5_operands, w["s5_a_re"][0], w["s5_a_im"][0], w["s5_log_dt"][0], w["s5_b_re"][0],
                            w["s5_b_im"][0], w["s5_c_re"][0], w["s5_c_im"][0])
    fox_w = []
    for j in range(w["fox_w_in"].shape[0]):
        w_in = w["fox_w_in"][j]
        w_f = jnp.pad(w_in[:, 3 * D_MODEL:], ((0, 0), (0, LANES - FOX_HEADS)))
        fox_w.append((w_in[:, :3 * D_MODEL], w_f, w["fox_w_out"][j]))
    for i in range(DEPTH):
        kind, j = mixers[i % 3], i // 3
        gain = w["mix_norm_g"][i]
        if kind == "fox":
            x1, sv = _fox_fwd(x, gain, fox_w[j][0], fox_w[j][1], w["fox_b_f"][j], fox_w[j][2])
        elif kind == "s5":
            x1, sv = _s5_fwd(x, gain, w["s5_w_in"][j], ssm, w["s5_d"], w["s5_w_glu"][j])
        else:
            x1, sv = _pool_fwd(x, gain, w["pool_w"][j], w["pool_b"], w["pool_scale"])
        x, sf = _ffn_fwd(x1, w["ffn_norm_g"][i], w["ffn_w_gate_up"][i], w["ffn_w_down"][i])
        saved.append((sv, sf))
    loss, dx, grads["final_norm_g"] = _loss_head(x, w["final_norm_g"], target)
    per_layer = {n: [None] * DEPTH for n in ("mix_norm_g", "ffn_norm_g", "ffn_w_gate_up", "ffn_w_down")}
    fox_g = {n: [None] * len(fox_w) for n in ("fox_w_in", "fox_b_f", "fox_w_out")}
    for i in reversed(range(DEPTH)):
        kind, j = mixers[i % 3], i // 3
        sv, sf = saved[i]
        dx, per_layer["ffn_norm_g"][i], per_layer["ffn_w_gate_up"][i], per_layer["ffn_w_down"][i] = _ffn_bwd(
            dx, sf, w["ffn_norm_g"][i], w["ffn_w_gate_up"][i], w["ffn_w_down"][i])
        gain = w["mix_norm_g"][i]
        if kind == "fox":
            dx, per_layer["mix_norm_g"][i], fox_g["fox_w_in"][j], fox_g["fox_b_f"][j], fox_g["fox_w_out"][j] = _fox_bwd(
                dx, sv, gain, fox_w[j][0], fox_w[j][1], fox_w[j][2])
        elif kind == "s5":
            dx, per_layer["mix_norm_g"][i], dw_in, dssm, dd, dw_glu = _s5_bwd(
                dx, sv, gain, w["s5_w_in"][j], ssm, w["s5_d"], w["s5_w_glu"][j])
            grads["s5_w_in"], grads["s5_w_glu"], grads["s5_d"] = dw_in[None], dw_glu[None], dd
            for n, g in zip(("s5_a_re", "s5_a_im", "s5_log_dt", "s5_b_re", "s5_b_im", "s5_c_re", "s5_c_im"), ssm_pull(dssm)):
                grads[n] = g[None]
        else:
            dx, per_layer["mix_norm_g"][i], dw, db, dsc = _pool_bwd(dx, sv, gain, w["pool_w"][j], w["pool_b"], w["pool_scale"])
            grads["pool_w"], grads["pool_b"], grads["pool_scale"] = dw[None].astype(BF16), db, dsc
    for n, parts in {**per_layer, **fox_g}.items():
        grads[n] = jnp.stack(parts)
    return loss, dx, grads


def kernel(x, mix_norm_g, ffn_norm_g, final_norm_g, fox_w_in, fox_b_f, fox_w_out, s5_w_in, s5_a_re, s5_a_im, s5_log_dt, s5_b_re, s5_b_im, s5_c_re, s5_c_im, s5_d, s5_w_glu, pool_w, pool_b, pool_scale, ffn_w_gate_up, ffn_w_down, loss_target, m_mix_norm_g, m_ffn_norm_g, m_final_norm_g, m_fox_w_in, m_fox_b_f, m_fox_w_out, m_s5_w_in, m_s5_a_re, m_s5_a_im, m_s5_log_dt, m_s5_b_re, m_s5_b_im, m_s5_c_re, m_s5_c_im, m_s5_d, m_s5_w_glu, m_pool_w, m_pool_b, m_pool_scale, m_ffn_w_gate_up, m_ffn_w_down, v_mix_norm_g, v_ffn_norm_g, v_final_norm_g, v_fox_w_in, v_fox_b_f, v_fox_w_out, v_s5_w_in, v_s5_a_re, v_s5_a_im, v_s5_log_dt, v_s5_b_re, v_s5_b_im, v_s5_c_re, v_s5_c_im, v_s5_d, v_s5_w_glu, v_pool_w, v_pool_b, v_pool_scale, v_ffn_w_gate_up, v_ffn_w_down):
    given = dict(locals())
    shard = {n: given[n] for n in WEIGHTS}
    chip = 2 * lax.axis_index("x") + lax.axis_index("y")
    core = lax.axis_index("c")

    views = {n: _halves_view(shard[n].shape) for n in BIG}
    own = [shard[n].astype(MXU_DTYPE).reshape(views[n]) for n in BIG]
    whole = {}
    for n, mine, others in zip(BIG, own, _gather_weights(own)):
        by_chip = lax.dynamic_update_index_in_dim(others, mine, chip, 0)
        whole[n] = _to_natural(by_chip.reshape((N_CHIPS,) + shard[n].shape), BIG[n])
    for n in SMALL:
        whole[n] = shard[n]
    sliced_shapes = [shard[n].shape for n in SLICED]
    by_chip = _gather_small(_pack_small([shard[n] for n in SLICED]))[0::2]
    slices = [_unpack_small(by_chip[k], sliced_shapes) for k in range(N_CHIPS)]
    for idx, n in enumerate(SLICED):
        whole[n] = jnp.concatenate([slices[k][idx] for k in range(N_CHIPS)], axis=-1)

    loss_part, dx, grads = _local_step(x[0], loss_target[0], whole)
    loss = lax.psum(loss_part, MESH_AXES)

    gs = [_to_chip_major(grads[n].astype(BF16), BIG[n]).reshape((N_CHIPS,) + views[n]) for n in BIG]
    core_id, chip_id = core.reshape(1).astype(jnp.int32), chip.reshape(1).astype(jnp.int32)
    partial = [_pair_sum(g, r, core_id, "pair_sum_" + n) for n, g, r in zip(BIG, gs, _swap_halves(gs))]
    half = [_chip_sum(p, r, chip_id, "chip_sum_" + n) for n, p, r in zip(BIG, partial, _scatter_partials(partial))]
    grad = {n: lax.dynamic_update_index_in_dim(both, mine, core, 0).reshape(shard[n].shape)
            for n, mine, both in zip(BIG, half, _share_half(half))}

    small_sum = _sum_blocks(_gather_small(_pack_small([grads[n] for n in SMALL])), "small_sum")
    for n, g in zip(SMALL, _unpack_small(small_sum, [whole[n].shape for n in SMALL])):
        grad[n] = g
    for n in SLICED:
        width = shard[n].shape[-1]
        grad[n] = lax.dynamic_slice_in_dim(grad[n], chip * width, width, axis=-1)

    delta, new_m, new_v = {}, {}, {}
    for n in BIG:
        view = (-1, shard[n].shape[-1])
        res = _adamw(shard[n].reshape(view), grad[n].reshape(view), given["m_" + n].reshape(view),
                     given["v_" + n].reshape(view), "adamw_" + n)
        delta[n], new_m[n], new_v[n] = (r.reshape(shard[n].shape) for r in res)
    small_shapes = [shard[n].shape for n in SMALL]
    res = _adamw(_pack_small([shard[n] for n in SMALL]), _pack_small([grad[n] for n in SMALL]),
                 _pack_small([given["m_" + n] for n in SMALL]), _pack_small([given["v_" + n] for n in SMALL]), "adamw_small")
    for out, buf in zip((delta, new_m, new_v), res):
        for n, a in zip(SMALL, _unpack_small(buf, small_shapes)):
            out[n] = a
    return (loss, dx[None], *[grad[n] for n in WEIGHTS], *[delta[n] for n in WEIGHTS],
            *[new_m[n] for n in WEIGHTS], *[new_v[n] for n in WEIGHTS])
```

```python
import functools
import math

import jax
import jax.numpy as jnp
import numpy as np
from jax import lax
from jax.experimental import pallas as pl
from jax.experimental.pallas import tpu as pltpu

F32 = jnp.float32
BF16 = jnp.bfloat16
MXU_DTYPE = jnp.bfloat16

D_MODEL = 1024
DEPTH = 4
EPS = 1e-6
FOX_HEADS = 16
FOX_HEAD_DIM = 64
HEAD_PAIRS = FOX_HEADS // 2
S5_GROUPS = 64
S5_GROUP = 16
S5_STATE = 64
S5_BLOCKS = 8
S5_HALF = 256
POOL_WINDOWS = (2, 4, 8, 16)
POOL_WIDTH = 256
D_FF = 2816
N_CHIPS = 4
N_DEV = 8
LANES = 128
SUBLANES = 8
VMEM_LIMIT = 56 * 1024 * 1024

ADAM_LR = 0.001
ADAM_B1 = 0.9
ADAM_B2 = 0.999
ADAM_EPS = 1e-08
ADAM_WD = 0.01
ADAM_STEP = 10

MESH_AXES = ("x", "y", "c")


def _tile(n, want):
    t = (min(n, want) // LANES) * LANES
    while t >= LANES:
        if n % t == 0:
            return t
        t -= LANES
    return n


def _params(sem=None):
    return pltpu.CompilerParams(dimension_semantics=sem, vmem_limit_bytes=VMEM_LIMIT)


def _mm(a, b, *, name, ta=False, tb=False, out_dtype=F32, add=None, tm=1024, tn=1024, tk=1024):
    m, k = (a.shape[1], a.shape[0]) if ta else a.shape
    n = b.shape[0] if tb else b.shape[1]
    assert (b.shape[1] if tb else b.shape[0]) == k, (a.shape, b.shape, ta, tb)
    tm, tn, tk = _tile(m, tm), _tile(n, tn), _tile(k, tk)
    nk = k // tk
    a_spec = pl.BlockSpec((tk, tm), lambda i, j, kk: (kk, i)) if ta else pl.BlockSpec((tm, tk), lambda i, j, kk: (i, kk))
    b_spec = pl.BlockSpec((tn, tk), lambda i, j, kk: (j, kk)) if tb else pl.BlockSpec((tk, tn), lambda i, j, kk: (kk, j))
    o_spec = pl.BlockSpec((tm, tn), lambda i, j, kk: (i, j))
    dims = (((0 if ta else 1,), (1 if tb else 0,)), ((), ()))
    has_add = add is not None

    def body(*refs):
        if has_add:
            a_ref, b_ref, add_ref, o_ref, acc_ref = refs
        else:
            a_ref, b_ref, o_ref, acc_ref = refs
        kk = pl.program_id(2)

        @pl.when(kk == 0)
        def _():
            acc_ref[...] = jnp.zeros_like(acc_ref)

        acc_ref[...] += lax.dot_general(a_ref[...].astype(MXU_DTYPE), b_ref[...].astype(MXU_DTYPE), dims,
                                        preferred_element_type=F32)

        @pl.when(kk == nk - 1)
        def _():
            r = acc_ref[...]
            if has_add:
                r = r + add_ref[...].astype(F32)
            o_ref[...] = r.astype(out_dtype)

    ins = [a, b] + ([add] if has_add else [])
    specs = [a_spec, b_spec] + ([o_spec] if has_add else [])
    return pl.pallas_call(
        body, name=name, grid=(m // tm, n // tn, nk), in_specs=specs, out_specs=o_spec,
        out_shape=jax.ShapeDtypeStruct((m, n), out_dtype), scratch_shapes=[pltpu.VMEM((tm, tn), F32)],
        compiler_params=_params(("parallel", "parallel", "arbitrary")))(*ins)


def _ew(fn, tens, vecs=(), *, outs=(), sums=(), name, tr=256):
    tens = [t if isinstance(t, tuple) else (t, t.shape[1], 0) for t in tens]
    rows = tens[0][0].shape[0]
    tr = min(tr, rows)
    n_t, n_v, n_o, n_s = len(tens), len(vecs), len(outs), len(sums)

    def body(*refs):
        i = pl.program_id(0)
        t_blocks = [r[...] for r in refs[:n_t]]
        v_blocks = [r[...] for r in refs[n_t:n_t + n_v]]
        o_refs = refs[n_t + n_v:n_t + n_v + n_o]
        s_refs = refs[n_t + n_v + n_o:]
        o_vals, s_vals = fn(*t_blocks, *v_blocks)
        for r, v in zip(o_refs, o_vals):
            r[...] = v.astype(r.dtype)
        if n_s:
            @pl.when(i == 0)
            def _():
                for r in s_refs:
                    r[...] = jnp.zeros_like(r)
            for r, v in zip(s_refs, s_vals):
                r[...] += jnp.sum(v.astype(F32), axis=0, keepdims=True)

    in_specs = [pl.BlockSpec((tr, w), functools.partial(lambda i, cb: (i, cb), cb=cb)) for _, w, cb in tens]
    in_specs += [pl.BlockSpec(v.shape, functools.partial(lambda i, nd: (0,) * nd, nd=v.ndim)) for v in vecs]
    out_specs = [pl.BlockSpec((tr, c), lambda i: (i, 0)) for c, _ in outs]
    out_specs += [pl.BlockSpec((1, c), lambda i: (0, 0)) for c in sums]
    out_shape = [jax.ShapeDtypeStruct((rows, c), dt) for c, dt in outs]
    out_shape += [jax.ShapeDtypeStruct((1, c), F32) for c in sums]
    res = pl.pallas_call(
        body, name=name, grid=(rows // tr,), in_specs=in_specs, out_specs=out_specs, out_shape=out_shape,
        compiler_params=_params(("arbitrary",)))(*[t[0] for t in tens], *vecs)
    return res


def _sigmoid(z):
    return 1.0 / (1.0 + jnp.exp(-z))


def _rms_fwd(x, g, name):
    def fn(xb, gb):
        r = lax.rsqrt(jnp.mean(xb * xb, axis=-1, keepdims=True) + EPS)
        return ((xb * r) * gb,), ()
    return _ew(fn, [x], [g.reshape(1, -1)], outs=[(x.shape[1], BF16)], name=name)[0]


def _rms_bwd(x, g, dh, dres, name):
    def fn(xb, dhb, drb, gb):
        r = lax.rsqrt(jnp.mean(xb * xb, axis=-1, keepdims=True) + EPS)
        xh = xb * r
        dhf = dhb.astype(F32)
        dy = dhf * gb
        dx = r * (dy - xh * jnp.mean(dy * xh, axis=-1, keepdims=True))
        return (drb + dx,), (dhf * xh,)
    dx, dg = _ew(fn, [x, dh, dres], [g.reshape(1, -1)], outs=[(x.shape[1], F32)], sums=[x.shape[1]], name=name)
    return dx, dg[0]


def _mm_rms_bwd(a, b, x, g, dres, *, name, add=None, tm=512, tk=1024):
    m, k = a.shape
    n = b.shape[0]
    assert b.shape[1] == k and x.shape == (m, n)
    tm, tk = _tile(m, tm), _tile(k, tk)
    nk = k // tk
    has_add = add is not None

    def body(*refs):
        a_ref, b_ref, x_ref, dr_ref, g_ref = refs[:5]
        add_ref = refs[5] if has_add else None
        dx_ref, dg_ref, acc_ref = refs[-3:]
        i, kk = pl.program_id(0), pl.program_id(1)

        @pl.when(kk == 0)
        def _():
            acc_ref[...] = jnp.zeros_like(acc_ref)

        @pl.when((kk == 0) & (i == 0))
        def _():
            dg_ref[...] = jnp.zeros_like(dg_ref)

        acc_ref[...] += lax.dot_general(a_ref[...].astype(MXU_DTYPE), b_ref[...].astype(MXU_DTYPE),
                                        (((1,), (1,)), ((), ())), preferred_element_type=F32)

        @pl.when(kk == nk - 1)
        def _():
            dh = acc_ref[...]
            if has_add:
                dh = dh + add_ref[...]
            xb = x_ref[...]
            r = lax.rsqrt(jnp.mean(xb * xb, axis=-1, keepdims=True) + EPS)
            xh = xb * r
            dy = dh * g_ref[...]
            dx_ref[...] = dr_ref[...] + r * (dy - xh * jnp.mean(dy * xh, axis=-1, keepdims=True))
            dg_ref[...] += jnp.sum(dh * xh, axis=0, keepdims=True)

    row = pl.BlockSpec((tm, n), lambda i, kk: (i, 0))
    vec = pl.BlockSpec((1, n), lambda i, kk: (0, 0))
    ins = [a, b, x, dres, g.reshape(1, n)] + ([add] if has_add else [])
    specs = [pl.BlockSpec((tm, tk), lambda i, kk: (i, kk)), pl.BlockSpec((n, tk), lambda i, kk: (0, kk)), row, row, vec]
    specs += [row] if has_add else []
    dx, dg = pl.pallas_call(
        body, name=name, grid=(m // tm, nk), in_specs=specs, out_specs=[row, vec],
        out_shape=[jax.ShapeDtypeStruct((m, n), F32), jax.ShapeDtypeStruct((1, n), F32)],
        scratch_shapes=[pltpu.VMEM((tm, n), F32)],
        compiler_params=_params(("arbitrary", "arbitrary")))(*ins)
    return dx, dg[0]


FFN_ROWS = 512
FFN_COLS = D_FF // 2


def _ffn_gate_up(h, w_gu):
    s, d = h.shape
    tm = min(FFN_ROWS, s)
    halves = D_FF // FFN_COLS

    def body(h_ref, wg_ref, wu_ref, g_ref, u_ref, a_ref):
        hb = h_ref[...]
        g = jnp.dot(hb, wg_ref[...], preferred_element_type=F32)
        u = jnp.dot(hb, wu_ref[...], preferred_element_type=F32)
        g_ref[...] = g.astype(g_ref.dtype)
        u_ref[...] = u.astype(u_ref.dtype)
        a_ref[...] = (g * _sigmoid(g) * u).astype(a_ref.dtype)

    tile = pl.BlockSpec((tm, FFN_COLS), lambda jj, i: (i, jj))
    return pl.pallas_call(
        body, name="ffn_gate_up", grid=(halves, s // tm),
        in_specs=[pl.BlockSpec((tm, d), lambda jj, i: (i, 0)),
                  pl.BlockSpec((d, FFN_COLS), lambda jj, i: (0, jj)),
                  pl.BlockSpec((d, FFN_COLS), lambda jj, i: (0, halves + jj))],
        out_specs=[tile, tile, tile], out_shape=[jax.ShapeDtypeStruct((s, D_FF), BF16)] * 3,
        compiler_params=_params(("parallel", "parallel")))(h, w_gu, w_gu)


def _ffn_dgate_up(dx2, w_down, g, u):
    s, d = dx2.shape
    tm = min(FFN_ROWS, s)
    halves = D_FF // FFN_COLS

    def body(dx_ref, w_ref, g_ref, u_ref, o_ref):
        jj = pl.program_id(1)
        df = lax.dot_general(dx_ref[...].astype(MXU_DTYPE), w_ref[...], (((1,), (1,)), ((), ())), preferred_element_type=F32)
        gf, uf = g_ref[...].astype(F32), u_ref[...].astype(F32)
        sg = _sigmoid(gf)
        dg = df * uf * (sg * (1.0 + gf * (1.0 - sg)))
        du = df * (gf * sg)
        o_ref[:, pl.ds(pl.multiple_of(jj * FFN_COLS, LANES), FFN_COLS)] = dg.astype(o_ref.dtype)
        o_ref[:, pl.ds(pl.multiple_of(D_FF + jj * FFN_COLS, LANES), FFN_COLS)] = du.astype(o_ref.dtype)

    tile = pl.BlockSpec((tm, FFN_COLS), lambda i, jj: (i, jj))
    return pl.pallas_call(
        body, name="ffn_dgate_up", grid=(s // tm, halves),
        in_specs=[pl.BlockSpec((tm, d), lambda i, jj: (i, 0)),
                  pl.BlockSpec((FFN_COLS, d), lambda i, jj: (jj, 0)), tile, tile],
        out_specs=pl.BlockSpec((tm, 2 * D_FF), lambda i, jj: (i, 0)),
        out_shape=jax.ShapeDtypeStruct((s, 2 * D_FF), BF16),
        compiler_params=_params(("parallel", "arbitrary")))(dx2, w_down, g, u)


def _ffn_fwd(x1, gain, w_gu, w_down):
    h = _rms_fwd(x1, gain, "ffn_norm")
    g, u, act = _ffn_gate_up(h, w_gu)
    x2 = _mm(act, w_down, name="ffn_down", add=x1, tk=D_FF // 2)
    return x2, (x1, h, g, u, act)


def _ffn_bwd(dx2, saved, gain, w_gu, w_down):
    x1, h, g, u, act = saved
    dw_down = _mm(act, dx2, ta=True, name="ffn_dw_down", out_dtype=BF16, tm=D_FF // 2)
    dgu = _ffn_dgate_up(dx2, w_down, g, u)
    dw_gu = _mm(h, dgu, ta=True, name="ffn_dw_gu", out_dtype=BF16, tn=D_FF // 2)
    dx1, dgain = _mm_rms_bwd(dgu, w_gu, x1, gain, dx2, name="ffn_dh", tm=1024, tk=512)
    return dx1, dgain, dw_gu, dw_down


def _loss_head(x, gain, target):
    d = x.shape[1]

    def fn(xb, tb, gb):
        r = lax.rsqrt(jnp.mean(xb * xb, axis=-1, keepdims=True) + EPS)
        xh = xb * r
        y = xh * gb
        err = y - tb
        dyv = err * (1.0 / d)
        dyg = dyv * gb
        dx = r * (dyg - xh * jnp.mean(dyg * xh, axis=-1, keepdims=True))
        return (dx,), (0.5 * err * err * (1.0 / d), dyv * xh)
    dx, lsum, dg = _ew(fn, [x, target], [gain.reshape(1, -1)], outs=[(d, F32)], sums=[d, d], name="loss_head")
    return jnp.sum(lsum), dx, dg[0]


ATT_BLOCK = 256
CUM_BLOCK = 512
NEG_INF = -1e30


def _fox_gate_fwd(fl_row, b_col):
    nh, s = fl_row.shape
    tb = min(CUM_BLOCK, s)

    def body(fl_ref, b_ref, z_ref, c_ref):
        upper = (lax.broadcasted_iota(jnp.int32, (tb, tb), 0) <= lax.broadcasted_iota(jnp.int32, (tb, tb), 1)).astype(F32)
        carry = jnp.zeros((nh, 1), F32)
        for blk in range(s // tb):
            z = fl_ref[:, blk * tb:(blk + 1) * tb] + b_ref[...]
            logf = jnp.minimum(z, 0.0) - jnp.log(1.0 + jnp.exp(-jnp.abs(z)))
            cs = jnp.dot(logf, upper, precision=lax.Precision.HIGHEST, preferred_element_type=F32) + carry
            z_ref[:, blk * tb:(blk + 1) * tb] = z
            c_ref[:, blk * tb:(blk + 1) * tb] = cs
            carry = cs[:, tb - 1:tb]

    return pl.pallas_call(body, name="fox_gate_fwd", out_shape=[jax.ShapeDtypeStruct((nh, s), F32)] * 2,
                          compiler_params=_params())(fl_row, b_col)


def _fox_gate_bwd(dc_row, z_row):
    nh, s = dc_row.shape
    tb = min(CUM_BLOCK, s)

    def body(dc_ref, z_ref, dz_ref, db_ref):
        lower = (lax.broadcasted_iota(jnp.int32, (tb, tb), 0) >= lax.broadcasted_iota(jnp.int32, (tb, tb), 1)).astype(F32)
        carry = jnp.zeros((nh, 1), F32)
        db = jnp.zeros((nh, 1), F32)
        for blk in reversed(range(s // tb)):
            dc = dc_ref[:, blk * tb:(blk + 1) * tb]
            rs = jnp.dot(dc, lower, precision=lax.Precision.HIGHEST, preferred_element_type=F32) + carry
            dz = rs * _sigmoid(-z_ref[:, blk * tb:(blk + 1) * tb])
            dz_ref[:, blk * tb:(blk + 1) * tb] = dz
            db = db + jnp.sum(dz, axis=1, keepdims=True)
            carry = rs[:, 0:1]
        db_ref[...] = db

    return pl.pallas_call(body, name="fox_gate_bwd",
                          out_shape=[jax.ShapeDtypeStruct((nh, s), F32), jax.ShapeDtypeStruct((nh, 1), F32)],
                          compiler_params=_params())(dc_row, z_row)


def _head_masks(rows):
    lane = lax.broadcasted_iota(jnp.int32, (rows, LANES), 1)
    return lane < FOX_HEAD_DIM


def _attn_fwd(qkv, c_row):
    s = qkv.shape[0]
    t = min(ATT_BLOCK, s)
    nq = s // t
    scale = FOX_HEAD_DIM ** -0.5

    def body(q_ref, k_ref, v_ref, cr_ref, o_ref, lse_ref):
        i = pl.program_id(1)
        first = _head_masks(t)
        qs = q_ref[...] * scale
        zero = jnp.zeros_like(qs)
        qh = (jnp.where(first, qs, zero), jnp.where(first, zero, qs))
        causal = lax.broadcasted_iota(jnp.int32, (t, t), 0) >= lax.broadcasted_iota(jnp.int32, (t, t), 1)

        def block(j, carry, masked):
            ms, ls, acc = carry
            start = pl.multiple_of(j * t, t)
            ks, vs = k_ref[pl.ds(start, t), :], v_ref[pl.ds(start, t), :]
            new_m, new_l, alphas, pvs = [], [], [], []
            for hh in range(2):
                sc = lax.dot_general(qh[hh], ks, (((1,), (1,)), ((), ())), preferred_element_type=F32)
                sc = sc - cr_ref[0, hh:hh + 1, pl.ds(start, t)]
                if masked:
                    sc = jnp.where(causal, sc, NEG_INF)
                m_new = jnp.maximum(ms[hh], jnp.max(sc, axis=1, keepdims=True))
                p = jnp.exp(sc - m_new)
                alpha = jnp.exp(ms[hh] - m_new)
                new_m.append(m_new)
                new_l.append(alpha * ls[hh] + jnp.sum(p, axis=1, keepdims=True))
                alphas.append(alpha)
                p_hi = p.astype(MXU_DTYPE)
                p_lo = (p - p_hi.astype(F32)).astype(MXU_DTYPE)
                pvs.append(jnp.dot(p_hi, vs, preferred_element_type=F32) + jnp.dot(p_lo, vs, preferred_element_type=F32))
            acc = jnp.where(first, alphas[0], alphas[1]) * acc + jnp.where(first, pvs[0], pvs[1])
            return tuple(new_m), tuple(new_l), acc

        neg, nil = jnp.full((t, 1), NEG_INF, F32), jnp.zeros((t, 1), F32)
        carry = lax.fori_loop(0, i, functools.partial(block, masked=False), ((neg, neg), (nil, nil), jnp.zeros((t, LANES), F32)))
        ms, ls, acc = block(i, carry, True)
        o_ref[...] = (acc / jnp.where(first, ls[0], ls[1])).astype(o_ref.dtype)
        lse_ref[0] = jnp.concatenate([ms[0] + jnp.log(ls[0]), ms[1] + jnp.log(ls[1])], axis=1)

    np_ = HEAD_PAIRS
    return pl.pallas_call(
        body, name="fox_attn_fwd", grid=(np_, nq),
        in_specs=[pl.BlockSpec((t, LANES), lambda p, i: (i, p)),
                  pl.BlockSpec((s, LANES), lambda p, i: (0, np_ + p)),
                  pl.BlockSpec((s, LANES), lambda p, i: (0, 2 * np_ + p)),
                  pl.BlockSpec((1, 2, s), lambda p, i: (p, 0, 0))],
        out_specs=[pl.BlockSpec((t, LANES), lambda p, i: (i, p)),
                   pl.BlockSpec((1, t, 2), lambda p, i: (p, i, 0))],
        out_shape=[jax.ShapeDtypeStruct((s, D_MODEL), F32), jax.ShapeDtypeStruct((np_, s, 2), F32)],
        compiler_params=_params(("parallel", "arbitrary")))(qkv, qkv, qkv, c_row)


def _attn_bwd(qkv, do, lse, delta, c_row):
    s = qkv.shape[0]
    t = min(ATT_BLOCK, s)
    nb = s // t
    scale = FOX_HEAD_DIM ** -0.5
    np_ = HEAD_PAIRS

    def body(q_ref, k_ref, v_ref, do_ref, lse_ref, dl_ref, cr_ref, dq_ref, dk_ref, dv_ref, dc_ref):
        j = pl.program_id(1)
        first = _head_masks(t)
        causal = lax.broadcasted_iota(jnp.int32, (t, t), 0) >= lax.broadcasted_iota(jnp.int32, (t, t), 1)
        kb = k_ref[...]
        vb = v_ref[...]

        @pl.when(j == 0)
        def _():
            dq_ref[...] = jnp.zeros_like(dq_ref)

        def step(i, carry, masked):
            dk_acc, dv_acc, dc_accs = carry
            rows = pl.ds(pl.multiple_of(i * t, t), t)
            qs = q_ref[rows, :] * scale
            dob = do_ref[rows, :]
            zero = jnp.zeros_like(qs)
            dks, dvs, dqs, dcs = [], [], [], []
            for hh in range(2):
                qh = jnp.where(first, qs, zero) if hh == 0 else jnp.where(first, zero, qs)
                doh = jnp.where(first, dob, zero) if hh == 0 else jnp.where(first, zero, dob)
                sc = lax.dot_general(qh, kb, (((1,), (1,)), ((), ())), preferred_element_type=F32)
                p = jnp.exp(sc - cr_ref[0, hh:hh + 1, :] - lse_ref[0, rows, hh:hh + 1])
                if masked:
                    p = jnp.where(causal, p, 0.0)
                dp = lax.dot_general(doh, vb, (((1,), (1,)), ((), ())), preferred_element_type=F32)
                ds = p * (dp - dl_ref[0, rows, hh:hh + 1])
                pb, dsb = p.astype(MXU_DTYPE), ds.astype(MXU_DTYPE)
                dvs.append(lax.dot_general(pb, dob, (((0,), (0,)), ((), ())), preferred_element_type=F32))
                dks.append(lax.dot_general(dsb, qs, (((0,), (0,)), ((), ())), preferred_element_type=F32))
                dqs.append(jnp.dot(dsb, kb, preferred_element_type=F32))
                dcs.append(dc_accs[hh] - jnp.sum(ds, axis=0, keepdims=True))
            dq_ref[rows, :] += jnp.where(first, dqs[0], dqs[1]) * scale
            return (dk_acc + jnp.where(first, dks[0], dks[1]), dv_acc + jnp.where(first, dvs[0], dvs[1]), tuple(dcs))

        nil = jnp.zeros((1, t), F32)
        carry = step(j, (jnp.zeros((t, LANES), F32), jnp.zeros((t, LANES), F32), (nil, nil)), True)
        dk_acc, dv_acc, dc_accs = lax.fori_loop(j + 1, nb, functools.partial(step, masked=False), carry)
        dk_ref[...] = dk_acc.astype(dk_ref.dtype)
        dv_ref[...] = dv_acc.astype(dv_ref.dtype)
        dc_ref[0] = jnp.concatenate(dc_accs, axis=0)

    return pl.pallas_call(
        body, name="fox_attn_bwd", grid=(np_, nb),
        in_specs=[pl.BlockSpec((s, LANES), lambda p, j: (0, p)),
                  pl.BlockSpec((t, LANES), lambda p, j: (j, np_ + p)),
                  pl.BlockSpec((t, LANES), lambda p, j: (j, 2 * np_ + p)),
                  pl.BlockSpec((s, LANES), lambda p, j: (0, p)),
                  pl.BlockSpec((1, s, 2), lambda p, j: (p, 0, 0)),
                  pl.BlockSpec((1, s, 2), lambda p, j: (p, 0, 0)),
                  pl.BlockSpec((1, 2, t), lambda p, j: (p, 0, j))],
        out_specs=[pl.BlockSpec((s, LANES), lambda p, j: (0, p)),
                   pl.BlockSpec((t, LANES), lambda p, j: (j, p)),
                   pl.BlockSpec((t, LANES), lambda p, j: (j, p)),
                   pl.BlockSpec((1, 2, t), lambda p, j: (p, 0, j))],
        out_shape=[jax.ShapeDtypeStruct((s, D_MODEL), F32), jax.ShapeDtypeStruct((s, D_MODEL), BF16),
                   jax.ShapeDtypeStruct((s, D_MODEL), BF16), jax.ShapeDtypeStruct((np_, 2, s), F32)],
        compiler_params=_params(("parallel", "arbitrary")))(qkv, qkv, qkv, do, lse, delta, c_row)


ATT_QUERIES = 512


def _lanes(a, width):
    return jnp.concatenate([a] * (width // LANES), axis=1)


def _attn_fwd_t(qkv, c_lanes):
    s = qkv.shape[0]
    t, tq = min(ATT_BLOCK, s), min(ATT_QUERIES, s)
    nq, per = s // tq, tq // t
    scale = FOX_HEAD_DIM ** -0.5
    np_ = HEAD_PAIRS
    nt = (((1,), (1,)), ((), ()))

    def body(q_ref, k_ref, v_ref, c_ref, o_ref, lse_ref, vt_ref):
        i = pl.program_id(1)

        @pl.when(i == 0)
        def _():
            for r in range(s // t):
                vt_ref[:, r * t:(r + 1) * t] = v_ref[r * t:(r + 1) * t, :].astype(F32).T.astype(vt_ref.dtype)

        first = _head_masks(tq)
        upper = lax.broadcasted_iota(jnp.int32, (LANES, tq), 0) < FOX_HEAD_DIM
        qs = q_ref[...] * scale
        zero = jnp.zeros_like(qs)
        qh = (jnp.where(first, qs, zero), jnp.where(first, zero, qs))
        key_at = lax.broadcasted_iota(jnp.int32, (t, tq), 0)
        query_at = lax.broadcasted_iota(jnp.int32, (t, tq), 1)

        def block(j, carry, diagonal=None):
            ms, ls, acc = carry
            start = pl.multiple_of(j * t, t)
            kb, vt = k_ref[pl.ds(start, t), :], vt_ref[:, pl.ds(start, t)]
            new_m, new_l, alphas, pvs = [], [], [], []
            for hh in range(2):
                sc = lax.dot_general(kb, qh[hh], nt, preferred_element_type=F32) - _lanes(c_ref[0, hh, pl.ds(start, t), :], tq)
                if diagonal is not None:
                    sc = jnp.where(key_at + diagonal * t <= query_at, sc, NEG_INF)
                m_new = jnp.maximum(ms[hh], jnp.max(sc, axis=0, keepdims=True))
                p = jnp.exp(sc - m_new)
                alpha = jnp.exp(ms[hh] - m_new)
                new_m.append(m_new)
                new_l.append(alpha * ls[hh] + jnp.sum(p, axis=0, keepdims=True))
                alphas.append(alpha)
                p_hi = p.astype(MXU_DTYPE)
                p_lo = (p - p_hi.astype(F32)).astype(MXU_DTYPE)
                pvs.append(jnp.dot(vt, p_hi, preferred_element_type=F32) + jnp.dot(vt, p_lo, preferred_element_type=F32))
            acc = jnp.where(upper, alphas[0], alphas[1]) * acc + jnp.where(upper, pvs[0], pvs[1])
            return tuple(new_m), tuple(new_l), acc

        neg, nil = jnp.full((1, tq), NEG_INF, F32), jnp.zeros((1, tq), F32)
        carry = lax.fori_loop(0, per * i, block, ((neg, neg), (nil, nil), jnp.zeros((LANES, tq), F32)))
        for d in range(per):
            carry = block(per * i + d, carry, diagonal=d)
        ms, ls, acc = carry
        o_ref[...] = (acc / jnp.where(upper, ls[0], ls[1])).T.astype(o_ref.dtype)
        lse_ref[0] = jnp.concatenate([ms[0] + jnp.log(ls[0]), ms[1] + jnp.log(ls[1])], axis=0)

    return pl.pallas_call(
        body, name="fox_attn_fwd", grid=(np_, nq),
        in_specs=[pl.BlockSpec((tq, LANES), lambda p, i: (i, p)),
                  pl.BlockSpec((s, LANES), lambda p, i: (0, np_ + p)),
                  pl.BlockSpec((s, LANES), lambda p, i: (0, 2 * np_ + p)),
                  pl.BlockSpec((1, 2, s, LANES), lambda p, i: (p, 0, 0, 0))],
        out_specs=[pl.BlockSpec((tq, LANES), lambda p, i: (i, p)),
                   pl.BlockSpec((1, 2, tq), lambda p, i: (p, 0, i))],
        out_shape=[jax.ShapeDtypeStruct((s, D_MODEL), F32), jax.ShapeDtypeStruct((np_, 2, s), F32)],
        scratch_shapes=[pltpu.VMEM((LANES, s), MXU_DTYPE)],
        compiler_params=_params(("parallel", "arbitrary")))(qkv, qkv, qkv, c_lanes)


def _attn_bwd_t(qkv, do, lse, delta, c_lanes):
    s = qkv.shape[0]
    t, tq = min(ATT_BLOCK, s), min(ATT_QUERIES, s)
    nb, nq, per = s // t, s // tq, tq // t
    scale = FOX_HEAD_DIM ** -0.5
    np_ = HEAD_PAIRS
    nt = (((1,), (1,)), ((), ()))

    def body(q_ref, k_ref, v_ref, do_ref, lse_ref, dl_ref, c_ref, dq_ref, dk_ref, dv_ref, dc_ref, dqt_ref):
        j = pl.program_id(1)
        first_q, first = _head_masks(tq), _head_masks(t)
        upper = lax.broadcasted_iota(jnp.int32, (LANES, tq), 0) < FOX_HEAD_DIM
        causal = (lax.broadcasted_iota(jnp.int32, (t, tq), 0) + (j % per) * t) <= lax.broadcasted_iota(jnp.int32, (t, tq), 1)
        kb, vb = k_ref[...], v_ref[...]
        kt = kb.astype(F32).T.astype(MXU_DTYPE)
        cb = (_lanes(c_ref[0, 0], tq), _lanes(c_ref[0, 1], tq))

        @pl.when(j == 0)
        def _():
            dqt_ref[...] = jnp.zeros_like(dqt_ref)

        def step(i, carry, masked):
            dk_acc, dv_acc, dc_accs = carry
            start = pl.multiple_of(i * tq, tq)
            qs = q_ref[pl.ds(start, tq), :] * scale
            dob = do_ref[pl.ds(start, tq), :]
            zero = jnp.zeros_like(qs)
            dks, dvs, dqs, dcs = [], [], [], []
            for hh in range(2):
                qh = jnp.where(first_q, qs, zero) if hh == 0 else jnp.where(first_q, zero, qs)
                doh = jnp.where(first_q, dob, zero) if hh == 0 else jnp.where(first_q, zero, dob)
                sc = lax.dot_general(kb, qh, nt, preferred_element_type=F32)
                p = jnp.exp(sc - cb[hh] - lse_ref[0, hh:hh + 1, pl.ds(start, tq)])
                if masked:
                    p = jnp.where(causal, p, 0.0)
                dp = lax.dot_general(vb, doh, nt, preferred_element_type=F32)
                ds = p * (dp - dl_ref[0, hh:hh + 1, pl.ds(start, tq)])
                pb, dsb = p.astype(MXU_DTYPE), ds.astype(MXU_DTYPE)
                dvs.append(jnp.dot(pb, dob, preferred_element_type=F32))
                dks.append(jnp.dot(dsb, qs, preferred_element_type=F32))
                dqs.append(jnp.dot(kt, dsb, preferred_element_type=F32))
                dcs.append(dc_accs[hh] - jnp.sum(ds, axis=1, keepdims=True))
            dqt_ref[:, pl.ds(start, tq)] += jnp.where(upper, dqs[0], dqs[1]) * scale
            return (dk_acc + jnp.where(first, dks[0], dks[1]), dv_acc + jnp.where(first, dvs[0], dvs[1]), tuple(dcs))

        nil, col = jnp.zeros((t, LANES), F32), jnp.zeros((t, 1), F32)
        carry = step(j // per, (nil, nil, (col, col)), True)
        dk_acc, dv_acc, dc_accs = lax.fori_loop(j // per + 1, nq, functools.partial(step, masked=False), carry)
        dk_ref[...] = dk_acc.astype(dk_ref.dtype)
        dv_ref[...] = dv_acc.astype(dv_ref.dtype)
        dc_ref[0, 0] = jnp.broadcast_to(dc_accs[0], (t, LANES))
        dc_ref[0, 1] = jnp.broadcast_to(dc_accs[1], (t, LANES))

        @pl.when(j == nb - 1)
        def _():
            for r in range(nb):
                dq_ref[r * t:(r + 1) * t, :] = dqt_ref[:, r * t:(r + 1) * t].T

    row = pl.BlockSpec((1, 2, s), lambda p, j: (p, 0, 0))
    return pl.pallas_call(
        body, name="fox_attn_bwd", grid=(np_, nb),
        in_specs=[pl.BlockSpec((s, LANES), lambda p, j: (0, p)),
                  pl.BlockSpec((t, LANES), lambda p, j: (j, np_ + p)),
                  pl.BlockSpec((t, LANES), lambda p, j: (j, 2 * np_ + p)),
                  pl.BlockSpec((s, LANES), lambda p, j: (0, p)), row, row,
                  pl.BlockSpec((1, 2, t, LANES), lambda p, j: (p, 0, j, 0))],
        out_specs=[pl.BlockSpec((s, LANES), lambda p, j: (0, p)),
                   pl.BlockSpec((t, LANES), lambda p, j: (j, p)),
                   pl.BlockSpec((t, LANES), lambda p, j: (j, p)),
                   pl.BlockSpec((1, 2, t, LANES), lambda p, j: (p, 0, j, 0))],
        out_shape=[jax.ShapeDtypeStruct((s, D_MODEL), F32), jax.ShapeDtypeStruct((s, D_MODEL), BF16),
                   jax.ShapeDtypeStruct((s, D_MODEL), BF16), jax.ShapeDtypeStruct((np_, 2, s, LANES), F32)],
        scratch_shapes=[pltpu.VMEM((LANES, s), F32)],
        compiler_params=_params(("parallel", "arbitrary")))(qkv, qkv, qkv, do, lse, delta, c_lanes)


def _head_sums(a, b, name):
    d = a.shape[1]
    sel = (jnp.arange(d)[:, None] // FOX_HEAD_DIM == jnp.arange(LANES)[None, :]).astype(F32)

    def fn(ab, bb, selb):
        prod = ab.astype(F32) * bb.astype(F32)
        return (jnp.dot(prod, selb, precision=lax.Precision.HIGHEST, preferred_element_type=F32),), ()
    return _ew(fn, [a, b], [sel], outs=[(LANES, F32)], name=name)[0]


def _pairs_col(a16):
    s = a16.shape[0]
    return a16.reshape(s, HEAD_PAIRS, 2).transpose(1, 0, 2)


def _fox_fwd(x, gain, w_qkv, w_f, b_f, w_out):
    s = x.shape[0]
    h = _rms_fwd(x, gain, "mix_norm")
    qkv = _mm(h, w_qkv, name="fox_qkv", out_dtype=BF16)
    fl = _mm(h, w_f, name="fox_f", tn=LANES)
    z_row, c_rowf = _fox_gate_fwd(fl[:, :FOX_HEADS].T, b_f.reshape(FOX_HEADS, 1))
    c_lanes = jnp.broadcast_to(c_rowf.reshape(HEAD_PAIRS, 2, s, 1), (HEAD_PAIRS, 2, s, LANES))
    o, lse = _attn_fwd_t(qkv, c_lanes)
    x1 = _mm(o, w_out, name="fox_out", add=x)
    return x1, (x, h, qkv, z_row, c_lanes, o, lse)


def _fox_bwd(dx1, saved, gain, w_qkv, w_f, w_out):
    x, h, qkv, z_row, c_lanes, o, lse = saved
    s = x.shape[0]
    do = _mm(dx1, w_out, tb=True, name="fox_do", out_dtype=BF16)
    dw_out = _mm(o, dx1, ta=True, name="fox_dw_out", out_dtype=BF16)
    delta = _head_sums(do, o, "fox_delta")[:, :FOX_HEADS].T.reshape(HEAD_PAIRS, 2, s)
    dq, dk, dv, dc = _attn_bwd_t(qkv, do, lse, delta, c_lanes)
    dz_row, db = _fox_gate_bwd(dc[..., 0].reshape(FOX_HEADS, s), z_row)
    dqkv = jnp.concatenate([dq.astype(BF16), dk, dv], axis=1)
    dfl = jnp.pad(dz_row.T, ((0, 0), (0, LANES - FOX_HEADS))).astype(BF16)
    dw_qkv = _mm(h, dqkv, ta=True, name="fox_dw_qkv", out_dtype=BF16)
    dw_f = _mm(h, dfl, ta=True, name="fox_dw_f", out_dtype=BF16, tn=LANES)
    dh = _mm(dqkv, w_qkv, tb=True, name="fox_dh_qkv")
    dx, dgain = _mm_rms_bwd(dfl, w_f, x, gain, dx1, name="fox_dh_f", add=dh)
    dw_in = jnp.concatenate([dw_qkv, dw_f[:, :FOX_HEADS]], axis=1)
    return dx, dgain, dw_in, db.reshape(FOX_HEADS), dw_out


S5_ROWS = 512
SCAN_CHUNKS = SUBLANES


def _s5_operands(a_re, a_im, log_dt, b_re, b_im, c_re, c_im):
    dt = jnp.exp(log_dt)[:, None]
    mag, ang = jnp.exp(a_re * dt), a_im * dt
    lr, li = mag * jnp.cos(ang), mag * jnp.sin(ang)
    den = a_re * a_re + a_im * a_im
    cr = ((lr - 1.0) * a_re + li * a_im) / den
    ci = (li * a_re - (lr - 1.0) * a_im) / den
    bbr = cr[..., None] * b_re - ci[..., None] * b_im
    bbi = cr[..., None] * b_im + ci[..., None] * b_re
    nb = S5_BLOCKS
    lam = jnp.stack([lr.reshape(nb, 2, S5_HALF), li.reshape(nb, 2, S5_HALF)], axis=2)
    eye4, eye2 = jnp.eye(4, dtype=F32), jnp.eye(2, dtype=F32)
    bb = jnp.stack([bbr, bbi], axis=0).reshape(2, nb, 2, 4, S5_STATE, S5_GROUP)
    bmat = jnp.einsum("rbhgpc,kg,jh->bhjkcrgp", bb, eye4, eye2).reshape(nb, 2, 128, 2 * S5_HALF)
    cc = jnp.stack([c_re, -c_im], axis=0).reshape(2, nb, 2, 4, S5_GROUP, S5_STATE)
    cmat = jnp.einsum("rbhgcp,kg,jh->bhrgpjkc", cc, eye4, eye2).reshape(nb, 2, 2 * S5_HALF, 128)
    return lam, bmat, cmat


def _time_to_scan_order(a):
    s, d = a.shape
    return a.reshape(SCAN_CHUNKS, s // SCAN_CHUNKS, d).transpose(1, 0, 2).reshape(s, d)


def _scan_to_time_order(a):
    s, d = a.shape
    return a.reshape(s // SCAN_CHUNKS, SCAN_CHUNKS, d).transpose(1, 0, 2).reshape(s, d)


def _scan_chunks(xr_ref, xi_ref, lr, li, nst, reverse, after_step=None, state=None):
    lanes = lr.shape[1]
    lr8, li8 = jnp.broadcast_to(lr, (SUBLANES, lanes)), jnp.broadcast_to(li, (SUBLANES, lanes))
    zero8 = jnp.zeros((SUBLANES, lanes), F32)

    def rows_of(n):
        s = (nst - 1 - n) if reverse else n
        return s, pl.ds(pl.multiple_of(s * SUBLANES, SUBLANES), SUBLANES)

    def local(n, carry):
        pr, pi = carry
        _, rows = rows_of(n)
        nr = lr8 * pr - li8 * pi + xr_ref[rows, :]
        ni = lr8 * pi + li8 * pr + xi_ref[rows, :]
        xr_ref[rows, :] = nr
        xi_ref[rows, :] = ni
        return nr, ni

    er, ei = lax.fori_loop(0, nst, local, (zero8, zero8))
    pr, pi = lr, li
    for _ in range(int(math.log2(nst))):
        pr, pi = pr * pr - pi * pi, 2.0 * pr * pi
    tr = ti = jnp.zeros((1, lanes), F32)
    ent_r, ent_i = [None] * SCAN_CHUNKS, [None] * SCAN_CHUNKS
    for k in (reversed(range(SCAN_CHUNKS)) if reverse else range(SCAN_CHUNKS)):
        ent_r[k], ent_i[k] = tr, ti
        tr, ti = er[k:k + 1] + (pr * tr - pi * ti), ei[k:k + 1] + (pr * ti + pi * tr)
    in_r, in_i = jnp.concatenate(ent_r, axis=0), jnp.concatenate(ent_i, axis=0)

    def fix(n, carry):
        wr, wi, st = carry
        s, rows = rows_of(n)
        nr = xr_ref[rows, :] + (wr * in_r - wi * in_i)
        ni = xi_ref[rows, :] + (wr * in_i + wi * in_r)
        xr_ref[rows, :] = nr
        xi_ref[rows, :] = ni
        if after_step is not None:
            st = after_step(s, nr, ni, st)
        return wr * lr8 - wi * li8, wr * li8 + wi * lr8, st

    _, _, state = lax.fori_loop(0, nst, fix, (lr8, li8, state))
    return in_r, in_i, state


def _s5_fill_states(u_ref, bm, xr_ref, xi_ref, s):
    rc = min(S5_ROWS, s)

    def fill(r, _):
        rows = pl.ds(pl.multiple_of(r * rc, rc), rc)
        bu = jnp.dot(u_ref[rows, :].astype(MXU_DTYPE), bm, preferred_element_type=F32)
        xr_ref[rows, :] = bu[:, :S5_HALF]
        xi_ref[rows, :] = bu[:, S5_HALF:]
        return 0
    lax.fori_loop(0, s // rc, fill, 0)


def _s5_specs():
    return [pl.BlockSpec((1, 2, 2, S5_HALF), lambda b: (b, 0, 0, 0)),
            pl.BlockSpec((1, 2, 128, 2 * S5_HALF), lambda b: (b, 0, 0, 0)),
            pl.BlockSpec((1, 2, 2 * S5_HALF, 128), lambda b: (b, 0, 0, 0)),
            pl.BlockSpec((1, LANES), lambda b: (0, b))]


def _s5_scan_fwd(u, lam, bmat, cmat, dvec):
    s = u.shape[0]
    nst = s // SCAN_CHUNKS
    rc = min(S5_ROWS, s)

    def body(u_ref, lam_ref, b_ref, c_ref, d_ref, y_ref, xr_ref, xi_ref):
        y_ref[...] = u_ref[...] * d_ref[...]
        for hb in range(2):
            _s5_fill_states(u_ref, b_ref[0, hb], xr_ref, xi_ref, s)
            _scan_chunks(xr_ref, xi_ref, lam_ref[0, hb, 0:1, :], lam_ref[0, hb, 1:2, :], nst, False)
            cm = c_ref[0, hb]

            def emit(r, _, cm=cm):
                rows = pl.ds(pl.multiple_of(r * rc, rc), rc)
                y_ref[rows, :] += (jnp.dot(xr_ref[rows, :].astype(MXU_DTYPE), cm[:S5_HALF], preferred_element_type=F32)
                                   + jnp.dot(xi_ref[rows, :].astype(MXU_DTYPE), cm[S5_HALF:], preferred_element_type=F32))
                return 0
            lax.fori_loop(0, s // rc, emit, 0)

    blk = pl.BlockSpec((s, LANES), lambda b: (0, b))
    return pl.pallas_call(
        body, name="s5_scan_fwd", grid=(S5_BLOCKS,), in_specs=[blk] + _s5_specs(), out_specs=blk,
        out_shape=jax.ShapeDtypeStruct(u.shape, F32),
        scratch_shapes=[pltpu.VMEM((s, S5_HALF), F32)] * 2,
        compiler_params=_params(("parallel",)))(u, lam, bmat, cmat, dvec)


def _s5_scan_bwd(u, dy, lam, bmat, cmat, dvec):
    s = u.shape[0]
    nst = s // SCAN_CHUNKS
    rc = min(S5_ROWS, s)
    nt = (((1,), (1,)), ((), ()))
    tn = (((0,), (0,)), ((), ()))

    def body(u_ref, dy_ref, lam_ref, b_ref, c_ref, d_ref, du_ref, db_ref, dc_ref, dl_ref, dd_ref,
             xr_ref, xi_ref, gr_ref, gi_ref):
        du_ref[...] = dy_ref[...] * d_ref[...]
        dd_ref[...] = jnp.sum(dy_ref[...] * u_ref[...], axis=0, keepdims=True)
        db_ref[...] = jnp.zeros_like(db_ref)
        dc_ref[...] = jnp.zeros_like(dc_ref)
        for hb in range(2):
            bm, cm = b_ref[0, hb], c_ref[0, hb]
            lr, li = lam_ref[0, hb, 0:1, :], lam_ref[0, hb, 1:2, :]
            _s5_fill_states(u_ref, bm, xr_ref, xi_ref, s)
            xin_r, xin_i, _ = _scan_chunks(xr_ref, xi_ref, lr, li, nst, False)

            def fill_g(r, _, cm=cm):
                rows = pl.ds(pl.multiple_of(r * rc, rc), rc)
                g = lax.dot_general(dy_ref[rows, :].astype(MXU_DTYPE), cm, nt, preferred_element_type=F32)
                gr_ref[rows, :] = g[:, :S5_HALF]
                gi_ref[rows, :] = g[:, S5_HALF:]
                return 0
            lax.fori_loop(0, s // rc, fill_g, 0)
            def lam_grad(st, g_r, g_i, acc, xin_r=xin_r, xin_i=xin_i):
                prev = pl.ds(pl.multiple_of(jnp.maximum(st - 1, 0) * SUBLANES, SUBLANES), SUBLANES)
                x_r = jnp.where(st > 0, xr_ref[prev, :], xin_r)
                x_i = jnp.where(st > 0, xi_ref[prev, :], xin_i)
                return acc[0] + (g_r * x_r + g_i * x_i), acc[1] + (g_i * x_r - g_r * x_i)

            zero8 = jnp.zeros((SUBLANES, S5_HALF), F32)
            _, _, (a_r, a_i) = _scan_chunks(gr_ref, gi_ref, lr, -li, nst, True, after_step=lam_grad, state=(zero8, zero8))
            dl_ref[0, hb] = jnp.concatenate([jnp.sum(a_r, axis=0, keepdims=True),
                                             jnp.sum(a_i, axis=0, keepdims=True)], axis=0)

            def emit(r, _, bm=bm, hb=hb):
                rows = pl.ds(pl.multiple_of(r * rc, rc), rc)
                g = jnp.concatenate([gr_ref[rows, :], gi_ref[rows, :]], axis=1).astype(MXU_DTYPE)
                x = jnp.concatenate([xr_ref[rows, :], xi_ref[rows, :]], axis=1).astype(MXU_DTYPE)
                du_ref[rows, :] += lax.dot_general(g, bm, nt, preferred_element_type=F32)
                db_ref[0, hb] += lax.dot_general(u_ref[rows, :].astype(MXU_DTYPE), g, tn, preferred_element_type=F32)
                dc_ref[0, hb] += lax.dot_general(dy_ref[rows, :].astype(MXU_DTYPE), x, tn, preferred_element_type=F32)
                return 0
            lax.fori_loop(0, s // rc, emit, 0)

    blk = pl.BlockSpec((s, LANES), lambda b: (0, b))
    mat = pl.BlockSpec((1, 2, 128, 2 * S5_HALF), lambda b: (b, 0, 0, 0))
    return pl.pallas_call(
        body, name="s5_scan_bwd", grid=(S5_BLOCKS,), in_specs=[blk, blk] + _s5_specs(),
        out_specs=[blk, mat, mat, pl.BlockSpec((1, 2, 2, S5_HALF), lambda b: (b, 0, 0, 0)),
                   pl.BlockSpec((1, LANES), lambda b: (0, b))],
        out_shape=[jax.ShapeDtypeStruct(u.shape, F32),
                   jax.ShapeDtypeStruct((S5_BLOCKS, 2, 128, 2 * S5_HALF), F32),
                   jax.ShapeDtypeStruct((S5_BLOCKS, 2, 128, 2 * S5_HALF), F32),
                   jax.ShapeDtypeStruct((S5_BLOCKS, 2, 2, S5_HALF), F32),
                   jax.ShapeDtypeStruct((1, D_MODEL), F32)],
        scratch_shapes=[pltpu.VMEM((s, S5_HALF), F32)] * 4,
        compiler_params=_params(("parallel",)))(u, dy, lam, bmat, cmat, dvec)


_GELU_C = math.sqrt(2.0 / math.pi)


def _gelu_parts(y):
    inner = _GELU_C * (y + 0.044715 * y * y * y)
    th = jnp.tanh(inner)
    return 0.5 * y * (1.0 + th), th


def _s5_fwd(x, gain, w_in, ssm, dvec, w_glu):
    lam, bmat, cmat = ssm
    h = _rms_fwd(x, gain, "mix_norm")
    u = _mm(h, w_in, name="s5_in")
    y = _scan_to_time_order(_s5_scan_fwd(_time_to_scan_order(u), lam, bmat.astype(MXU_DTYPE), cmat.astype(MXU_DTYPE), dvec))
    g = _ew(lambda yb: ((_gelu_parts(yb)[0],), ()), [y], outs=[(D_MODEL, BF16)], name="s5_gelu")[0]
    vg = _mm(g, w_glu, name="s5_glu", out_dtype=BF16)

    def glu_fn(vb, gb, xb):
        return (xb + vb.astype(F32) * _sigmoid(gb.astype(F32)),), ()
    x1 = _ew(glu_fn, [(vg, D_MODEL, 0), (vg, D_MODEL, 1), x], outs=[(D_MODEL, F32)], name="s5_gate")[0]
    return x1, (x, h, u, y, g, vg)


def _s5_bwd(dx1, saved, gain, w_in, ssm, dvec, w_glu):
    x, h, u, y, g, vg = saved
    lam, bmat, cmat = ssm

    def dglu_fn(db, vb, gb):
        vf, sg = vb.astype(F32), _sigmoid(gb.astype(F32))
        return (jnp.concatenate([db * sg, db * vf * sg * (1.0 - sg)], axis=1),), ()
    dvg = _ew(dglu_fn, [dx1, (vg, D_MODEL, 0), (vg, D_MODEL, 1)], outs=[(2 * D_MODEL, BF16)], name="s5_dgate")[0]
    dw_glu = _mm(g, dvg, ta=True, name="s5_dw_glu", out_dtype=BF16)
    dg = _mm(dvg, w_glu, tb=True, name="s5_dg")

    def dgelu_fn(dgb, yb):
        _, th = _gelu_parts(yb)
        dinner = _GELU_C * (1.0 + 3.0 * 0.044715 * yb * yb)
        return (dgb * (0.5 * (1.0 + th) + 0.5 * yb * (1.0 - th * th) * dinner),), ()
    dy = _ew(dgelu_fn, [dg, y], outs=[(D_MODEL, F32)], name="s5_dgelu")[0]
    du_s, dbm, dct, dlam, ddvec = _s5_scan_bwd(_time_to_scan_order(u), _time_to_scan_order(dy), lam,
                                               bmat.astype(MXU_DTYPE), cmat.astype(MXU_DTYPE), dvec)
    du = _scan_to_time_order(du_s)
    dw_in = _mm(h, du, ta=True, name="s5_dw_in", out_dtype=BF16)
    dx, dgain = _mm_rms_bwd(du, w_in, x, gain, dx1, name="s5_dh")
    return dx, dgain, dw_in, (dlam, dbm, jnp.swapaxes(dct, 2, 3)), ddvec, dw_glu


POOL_BLOCK = 256
N_POOL_GROUPS = len(POOL_WINDOWS)


def _pool_bands(gi, i, t):
    w = jnp.left_shift(2, gi)
    r = lax.broadcasted_iota(jnp.int32, (t, t), 0)
    c = lax.broadcasted_iota(jnp.int32, (t, t), 1)
    inside = ((c <= r) & (c > r - w)).astype(MXU_DTYPE)
    before = (c > r - w + t).astype(MXU_DTYPE)

    def inv_count(block):
        pos = block * t + lax.broadcasted_iota(jnp.int32, (t, 1), 0)
        return 1.0 / jnp.minimum(pos + 1, w).astype(F32)
    return inside, before, inv_count


def _pool_fwd(x, gain, w_grp, b_grp, scale):
    s = x.shape[0]
    t = min(POOL_BLOCK, s)
    h = _rms_fwd(x, gain, "mix_norm")

    def body(h_ref, hp_ref, w_ref, b_ref, sc_ref, x_ref, x1_ref, diff_ref):
        gi, i = pl.program_id(0), pl.program_id(1)
        inside, before, inv_count = _pool_bands(gi, i, t)
        hc = h_ref[...]
        tot = jnp.dot(inside, hc.astype(MXU_DTYPE), preferred_element_type=F32)
        prev = jnp.dot(before, hp_ref[...].astype(MXU_DTYPE), preferred_element_type=F32)
        tot = tot + jnp.where(i > 0, prev, 0.0)
        diff = (tot * inv_count(i) - hc.astype(F32)).astype(diff_ref.dtype)
        y = (jnp.dot(diff.astype(MXU_DTYPE), w_ref[0], preferred_element_type=F32) + b_ref[...]) * sc_ref[...]
        diff_ref[...] = diff
        x1_ref[...] = x_ref[...] + y

    blk = pl.BlockSpec((t, POOL_WIDTH), lambda gi, i: (i, gi))
    vec = pl.BlockSpec((1, POOL_WIDTH), lambda gi, i: (0, gi))
    x1, diff = pl.pallas_call(
        body, name="pool_fwd", grid=(N_POOL_GROUPS, s // t),
        in_specs=[blk, pl.BlockSpec((t, POOL_WIDTH), lambda gi, i: (jnp.maximum(i - 1, 0), gi)),
                  pl.BlockSpec((1, POOL_WIDTH, POOL_WIDTH), lambda gi, i: (gi, 0, 0)), vec, vec, blk],
        out_specs=[blk, blk],
        out_shape=[jax.ShapeDtypeStruct(x.shape, F32), jax.ShapeDtypeStruct(x.shape, BF16)],
        compiler_params=_params(("parallel", "arbitrary")))(h, h, w_grp, b_grp, scale, x)
    return x1, (x, diff)


def _pool_bwd(dx1, saved, gain, w_grp, b_grp, scale):
    x, diff = saved
    s = x.shape[0]
    t = min(POOL_BLOCK, s)
    nb = s // t

    def body1(dx_ref, diff_ref, w_ref, b_ref, sc_ref, dd_ref, dw_ref, db_ref, dsc_ref):
        i = pl.program_id(1)

        @pl.when(i == 0)
        def _():
            dw_ref[...] = jnp.zeros_like(dw_ref)
            db_ref[...] = jnp.zeros_like(db_ref)
            dsc_ref[...] = jnp.zeros_like(dsc_ref)

        dfb = diff_ref[...].astype(MXU_DTYPE)
        ypre = jnp.dot(dfb, w_ref[0], preferred_element_type=F32) + b_ref[...]
        dxb = dx_ref[...]
        dy = dxb * sc_ref[...]
        dsc_ref[...] += jnp.sum(dxb * ypre, axis=0, keepdims=True)
        db_ref[...] += jnp.sum(dy, axis=0, keepdims=True)
        dyb = dy.astype(MXU_DTYPE)
        dw_ref[0] += lax.dot_general(dfb, dyb, (((0,), (0,)), ((), ())), preferred_element_type=F32)
        dd_ref[...] = lax.dot_general(dyb, w_ref[0], (((1,), (1,)), ((), ())), preferred_element_type=F32)

    blk = pl.BlockSpec((t, POOL_WIDTH), lambda gi, i: (i, gi))
    vec = pl.BlockSpec((1, POOL_WIDTH), lambda gi, i: (0, gi))
    mat = pl.BlockSpec((1, POOL_WIDTH, POOL_WIDTH), lambda gi, i: (gi, 0, 0))
    ddiff, dw, db, dsc = pl.pallas_call(
        body1, name="pool_bwd_map", grid=(N_POOL_GROUPS, nb), in_specs=[blk, blk, mat, vec, vec],
        out_specs=[blk, mat, vec, vec],
        out_shape=[jax.ShapeDtypeStruct(x.shape, F32), jax.ShapeDtypeStruct(w_grp.shape, F32),
                   jax.ShapeDtypeStruct((1, D_MODEL), F32), jax.ShapeDtypeStruct((1, D_MODEL), F32)],
        compiler_params=_params(("parallel", "arbitrary")))(dx1, diff, w_grp, b_grp, scale)

    def body2(dc_ref, dn_ref, dh_ref):
        gi, i = pl.program_id(0), pl.program_id(1)
        inside, before, inv_count = _pool_bands(gi, i, t)
        tn = (((0,), (0,)), ((), ()))
        dc = dc_ref[...]
        tot = lax.dot_general(inside, (dc * inv_count(i)).astype(MXU_DTYPE), tn, preferred_element_type=F32)
        nxt = lax.dot_general(before, (dn_ref[...] * inv_count(i + 1)).astype(MXU_DTYPE), tn, preferred_element_type=F32)
        dh_ref[...] = tot + jnp.where(i < nb - 1, nxt, 0.0) - dc

    dh = pl.pallas_call(
        body2, name="pool_bwd_window", grid=(N_POOL_GROUPS, nb),
        in_specs=[blk, pl.BlockSpec((t, POOL_WIDTH), lambda gi, i: (jnp.minimum(i + 1, nb - 1), gi))],
        out_specs=blk, out_shape=jax.ShapeDtypeStruct(x.shape, F32),
        compiler_params=_params(("parallel", "parallel")))(ddiff, ddiff)
    dx, dgain = _rms_bwd(x, gain, dh, dx1, "mix_norm_bwd")
    return dx, dgain, dw, db, dsc


MESH_ID = pl.DeviceIdType.MESH
ANY = pl.BlockSpec(memory_space=pl.ANY)


def _place():
    x, y, c = lax.axis_index("x"), lax.axis_index("y"), lax.axis_index("c")
    other_chips = [(1 - x, y), (x, 1 - y), (1 - x, 1 - y)]
    return x, y, c, other_chips


def _chip_index(chip):
    return 2 * chip[0] + chip[1]


def _remote(src, dst, send_sems, recv_sems, n, to):
    return pltpu.make_async_remote_copy(src_ref=src, dst_ref=dst, send_sem=send_sems.at[n], recv_sem=recv_sems.at[n],
                                        device_id=to, device_id_type=MESH_ID)


def _gather_weights(ws):
    n = len(ws)
    from_x, relay_x, from_y, relay_y, sib_x, sib_y, sib_d0, sib_d1 = range(8)

    def body(*refs):
        w_refs, out_refs = refs[:n], refs[n:2 * n]
        send_sems, recv_sems = refs[2 * n:]
        x, y, c, (xn, yn, dn) = _place()
        k = _chip_index((x, y))
        me, sibling = (x, y, c), (x, y, 1 - c)

        def copy(ref, t, slot, to):
            return _remote(ref, ref, send_sems, recv_sems, 8 * t + slot, to)

        def quarter(ref, q):
            rows = ref.shape[0] // 2
            return ref.at[pl.ds(q * rows, rows)]

        started = []

        def start(cp):
            cp.start()
            started.append(cp)

        for t in range(n):
            for slot, chip in ((from_x, xn), (from_y, yn)):
                start(_remote(w_refs[t].at[c], out_refs[t].at[k, c], send_sems, recv_sems, 8 * t + slot, (*chip, c)))
        for t in range(n):
            got = out_refs[t].at[_chip_index(xn), c]
            copy(got, t, from_x, me).wait_recv()
            start(copy(quarter(got, 0), t, relay_y, (*yn, c)))
            start(copy(got, t, sib_x, sibling))
        for t in range(n):
            got = out_refs[t].at[_chip_index(yn), c]
            copy(got, t, from_y, me).wait_recv()
            start(copy(quarter(got, 1), t, relay_x, (*xn, c)))
            start(copy(got, t, sib_y, sibling))
        for t in range(n):
            got = out_refs[t].at[_chip_index(dn), c]
            for q, slot, sib_slot in ((0, relay_y, sib_d0), (1, relay_x, sib_d1)):
                copy(quarter(got, q), t, slot, me).wait_recv()
                start(copy(quarter(got, q), t, sib_slot, sibling))
        for t in range(n):
            for chip, slot in ((xn, sib_x), (yn, sib_y)):
                copy(out_refs[t].at[_chip_index(chip), 1 - c], t, slot, me).wait_recv()
            theirs = out_refs[t].at[_chip_index(dn), 1 - c]
            copy(quarter(theirs, 0), t, sib_d0, me).wait_recv()
            copy(quarter(theirs, 1), t, sib_d1, me).wait_recv()
        for cp in started:
            cp.wait_send()

    return pl.pallas_call(
        body, name="gather_weights", in_specs=[ANY] * n, out_specs=[ANY] * n,
        out_shape=[jax.ShapeDtypeStruct((N_CHIPS,) + w.shape, w.dtype) for w in ws],
        scratch_shapes=[pltpu.SemaphoreType.DMA((8 * n,)), pltpu.SemaphoreType.DMA((8 * n,))],
    )(*ws)


def _swap_halves(gs):
    n = len(gs)

    def body(*refs):
        g_refs, out_refs = refs[:n], refs[n:2 * n]
        send_sems, recv_sems = refs[2 * n:]
        x, y, c, _ = _place()
        copies = [_remote(g_refs[t].at[s, 1 - c], out_refs[t].at[s], send_sems, recv_sems, N_CHIPS * t + s, (x, y, 1 - c))
                  for t in range(n) for s in range(N_CHIPS)]
        for cp in copies:
            cp.start()
        for cp in copies:
            cp.wait_recv()
        for cp in copies:
            cp.wait_send()

    return pl.pallas_call(
        body, name="swap_halves", in_specs=[ANY] * n, out_specs=[ANY] * n,
        out_shape=[jax.ShapeDtypeStruct((N_CHIPS,) + g.shape[2:], g.dtype) for g in gs],
        scratch_shapes=[pltpu.SemaphoreType.DMA((N_CHIPS * n,)), pltpu.SemaphoreType.DMA((N_CHIPS * n,))],
    )(*gs)


def _scatter_partials(ps):
    n = len(ps)

    def body(*refs):
        p_refs, out_refs = refs[:n], refs[n:2 * n]
        send_sems, recv_sems = refs[2 * n:]
        x, y, c, chips = _place()
        sends = [_remote(p_refs[t].at[_chip_index(chip)], out_refs[t].at[j], send_sems, recv_sems, 3 * t + j, (*chip, c))
                 for j, chip in enumerate(chips) for t in range(n)]
        for cp in sends:
            cp.start()
        for j in range(3):
            for t in range(n):
                slot = out_refs[t].at[j]
                _remote(slot, slot, send_sems, recv_sems, 3 * t + j, (x, y, c)).wait_recv()
        for cp in sends:
            cp.wait_send()

    return pl.pallas_call(
        body, name="scatter_partials", in_specs=[ANY] * n, out_specs=[ANY] * n,
        out_shape=[jax.ShapeDtypeStruct((3,) + p.shape[1:], p.dtype) for p in ps],
        scratch_shapes=[pltpu.SemaphoreType.DMA((3 * n,)), pltpu.SemaphoreType.DMA((3 * n,))],
    )(*ps)


def _share_half(fs):
    n = len(fs)

    def body(*refs):
        f_refs, out_refs = refs[:n], refs[n:2 * n]
        send_sems, recv_sems = refs[2 * n:]
        x, y, c, _ = _place()
        sends = [_remote(f_refs[t], out_refs[t].at[c], send_sems, recv_sems, t, (x, y, 1 - c)) for t in range(n)]
        for cp in sends:
            cp.start()
        for t in range(n):
            theirs = out_refs[t].at[1 - c]
            _remote(theirs, theirs, send_sems, recv_sems, t, (x, y, c)).wait_recv()
        for cp in sends:
            cp.wait_send()

    return pl.pallas_call(
        body, name="share_half", in_specs=[ANY] * n, out_specs=[ANY] * n,
        out_shape=[jax.ShapeDtypeStruct((2,) + f.shape, f.dtype) for f in fs],
        scratch_shapes=[pltpu.SemaphoreType.DMA((n,)), pltpu.SemaphoreType.DMA((n,))],
    )(*fs)


def _gather_small(v):
    def body(v_ref, out_ref, send_sems, recv_sems):
        x, y, c, chips = _place()
        me, sibling = (x, y, c), (x, y, 1 - c)

        def slot(px, py, pc):
            return out_ref.at[4 * px + 2 * py + pc]

        first = [_remote(v_ref, slot(*me), send_sems, recv_sems, 0, sibling)]
        first += [_remote(v_ref, slot(*me), send_sems, recv_sems, 1 + j, (*chip, c)) for j, chip in enumerate(chips)]
        for cp in first:
            cp.start()
        passed = [_remote(slot(*chip, c), slot(*chip, c), send_sems, recv_sems, 4 + j, sibling)
                  for j, chip in enumerate(chips)]
        for j, chip in enumerate(chips):
            _remote(slot(*chip, c), slot(*chip, c), send_sems, recv_sems, 1 + j, me).wait_recv()
            passed[j].start()
        _remote(slot(*sibling), slot(*sibling), send_sems, recv_sems, 0, me).wait_recv()
        for j, chip in enumerate(chips):
            _remote(slot(*chip, 1 - c), slot(*chip, 1 - c), send_sems, recv_sems, 4 + j, me).wait_recv()
        for cp in first + passed:
            cp.wait_send()

    gathered = pl.pallas_call(
        body, name="gather_small", in_specs=[ANY], out_specs=ANY,
        out_shape=jax.ShapeDtypeStruct((N_DEV,) + v.shape, v.dtype),
        scratch_shapes=[pltpu.SemaphoreType.DMA((7,)), pltpu.SemaphoreType.DMA((7,))],
    )(v)
    device = 4 * lax.axis_index("x") + 2 * lax.axis_index("y") + lax.axis_index("c")
    return lax.dynamic_update_index_in_dim(gathered, v, device, 0)


SMALL_ROWS = 256
SUM_ROWS = 256
BF16_ROWS = 16


def _row_tile(rows, want):
    for t in range(min(rows, want) // BF16_ROWS * BF16_ROWS, 0, -BF16_ROWS):
        if rows % t == 0:
            return t
    return rows


def _pair_sum(g, r, core, name):
    rows, cols = g.shape[2:]
    tr = _row_tile(rows, SUM_ROWS)

    def body(c_ref, g_ref, r_ref, o_ref):
        o_ref[0] = (g_ref[0, 0].astype(F32) + r_ref[0].astype(F32)).astype(o_ref.dtype)

    return pl.pallas_call(
        body, name=name,
        grid_spec=pltpu.PrefetchScalarGridSpec(
            num_scalar_prefetch=1, grid=(N_CHIPS, rows // tr),
            in_specs=[pl.BlockSpec((1, 1, tr, cols), lambda s, i, c_ref: (s, c_ref[0], i, 0)),
                      pl.BlockSpec((1, tr, cols), lambda s, i, c_ref: (s, i, 0))],
            out_specs=pl.BlockSpec((1, tr, cols), lambda s, i, c_ref: (s, i, 0))),
        out_shape=jax.ShapeDtypeStruct(r.shape, BF16),
        compiler_params=_params(("parallel", "parallel")))(core, g, r)


def _chip_sum(p, recv, chip, name):
    rows, cols = p.shape[1:]
    tr = _row_tile(rows, SUM_ROWS)

    def body(k_ref, p_ref, r_ref, o_ref):
        acc = p_ref[0].astype(F32)
        for j in range(3):
            acc = acc + r_ref[j].astype(F32)
        o_ref[...] = acc

    return pl.pallas_call(
        body, name=name,
        grid_spec=pltpu.PrefetchScalarGridSpec(
            num_scalar_prefetch=1, grid=(rows // tr,),
            in_specs=[pl.BlockSpec((1, tr, cols), lambda i, k_ref: (k_ref[0], i, 0)),
                      pl.BlockSpec((3, tr, cols), lambda i, k_ref: (0, i, 0))],
            out_specs=pl.BlockSpec((tr, cols), lambda i, k_ref: (i, 0))),
        out_shape=jax.ShapeDtypeStruct((rows, cols), F32),
        compiler_params=_params(("parallel",)))(chip, p, recv)


def _sum_blocks(a, name):
    n, rows, cols = a.shape
    tr = _row_tile(rows, SUM_ROWS)

    def body(a_ref, o_ref):
        acc = a_ref[0].astype(F32)
        for s in range(1, n):
            acc = acc + a_ref[s].astype(F32)
        o_ref[...] = acc

    return pl.pallas_call(
        body, name=name, grid=(rows // tr,),
        in_specs=[pl.BlockSpec((n, tr, cols), lambda i: (0, i, 0))],
        out_specs=pl.BlockSpec((tr, cols), lambda i: (i, 0)),
        out_shape=jax.ShapeDtypeStruct((rows, cols), F32),
        compiler_params=_params(("parallel",)))(a)


def _adamw(w, g, m, v, name):
    def fn(wb, gb, mb, vb):
        m2 = ADAM_B1 * mb + (1.0 - ADAM_B1) * gb
        v2 = ADAM_B2 * vb + (1.0 - ADAM_B2) * (gb * gb)
        m_hat = m2 / (1.0 - ADAM_B1 ** ADAM_STEP)
        v_hat = v2 / (1.0 - ADAM_B2 ** ADAM_STEP)
        delta = -ADAM_LR * (m_hat / (jnp.sqrt(v_hat) + ADAM_EPS) + ADAM_WD * wb)
        return (delta, m2, v2), ()
    c = w.shape[1]
    return _ew(fn, [w, g, m, v], outs=[(c, F32)] * 3, name=name)


WEIGHTS = ["mix_norm_g", "ffn_norm_g", "final_norm_g", "fox_w_in", "fox_b_f", "fox_w_out", "s5_w_in", "s5_a_re",
           "s5_a_im", "s5_log_dt", "s5_b_re", "s5_b_im", "s5_c_re", "s5_c_im", "s5_d", "s5_w_glu", "pool_w",
           "pool_b", "pool_scale", "ffn_w_gate_up", "ffn_w_down"]
BIG = {"fox_w_in": 2, "fox_w_out": 1, "s5_w_in": 1, "s5_w_glu": 2, "pool_w": 2, "ffn_w_gate_up": 2, "ffn_w_down": 1}
SLICED = ("pool_b", "pool_scale")
SMALL = [n for n in WEIGHTS if n not in BIG]


def _to_natural(cm, axis):
    moved = jnp.moveaxis(cm, 0, axis)
    shape = moved.shape[:axis] + (moved.shape[axis] * moved.shape[axis + 1],) + moved.shape[axis + 2:]
    return moved.reshape(shape)


def _to_chip_major(nat, axis):
    shape = nat.shape[:axis] + (N_CHIPS, nat.shape[axis] // N_CHIPS) + nat.shape[axis + 1:]
    return jnp.moveaxis(nat.reshape(shape), axis, 0)


def _halves_view(shape):
    return (2, int(np.prod(shape[:-1])) // 2, shape[-1])


def _pack_small(parts):
    flat = jnp.concatenate([p.reshape(-1).astype(F32) for p in parts])
    pad = (-flat.shape[0]) % (SMALL_ROWS * LANES)
    return jnp.pad(flat, (0, pad)).reshape(-1, LANES)


def _unpack_small(buf, shapes):
    flat = buf.reshape(-1)
    out, off = [], 0
    for shp in shapes:
        n = int(np.prod(shp))
        out.append(flat[off:off + n].reshape(shp))
        off += n
    return out


def _local_step(x, target, w):
    grads = {}
    mixers = ("fox", "s5", "pool")
    saved = []
    ssm, ssm_pull = jax.vjp(_s5_operands, w["s5_a_re"][0], w["s5_a_im"][0], w["s5_log_dt"][0], w["s5_b_re"][0],
                            w["s5_b_im"][0], w["s5_c_re"][0], w["s5_c_im"][0])
    fox_w = []
    for j in range(w["fox_w_in"].shape[0]):
        w_in = w["fox_w_in"][j]
        w_f = jnp.pad(w_in[:, 3 * D_MODEL:], ((0, 0), (0, LANES - FOX_HEADS)))
        fox_w.append((w_in[:, :3 * D_MODEL], w_f, w["fox_w_out"][j]))
    for i in range(DEPTH):
        kind, j = mixers[i % 3], i // 3
        gain = w["mix_norm_g"][i]
        if kind == "fox":
            x1, sv = _fox_fwd(x, gain, fox_w[j][0], fox_w[j][1], w["fox_b_f"][j], fox_w[j][2])
        elif kind == "s5":
            x1, sv = _s5_fwd(x, gain, w["s5_w_in"][j], ssm, w["s5_d"], w["s5_w_glu"][j])
        else:
            x1, sv = _pool_fwd(x, gain, w["pool_w"][j], w["pool_b"], w["pool_scale"])
        x, sf = _ffn_fwd(x1, w["ffn_norm_g"][i], w["ffn_w_gate_up"][i], w["ffn_w_down"][i])
        saved.append((sv, sf))
    loss, dx, grads["final_norm_g"] = _loss_head(x, w["final_norm_g"], target)
    per_layer = {n: [None] * DEPTH for n in ("mix_norm_g", "ffn_norm_g", "ffn_w_gate_up", "ffn_w_down")}
    fox_g = {n: [None] * len(fox_w) for n in ("fox_w_in", "fox_b_f", "fox_w_out")}
    for i in reversed(range(DEPTH)):
        kind, j = mixers[i % 3], i // 3
        sv, sf = saved[i]
        dx, per_layer["ffn_norm_g"][i], per_layer["ffn_w_gate_up"][i], per_layer["ffn_w_down"][i] = _ffn_bwd(
            dx, sf, w["ffn_norm_g"][i], w["ffn_w_gate_up"][i], w["ffn_w_down"][i])
        gain = w["mix_norm_g"][i]
        if kind == "fox":
            dx, per_layer["mix_norm_g"][i], fox_g["fox_w_in"][j], fox_g["fox_b_f"][j], fox_g["fox_w_out"][j] = _fox_bwd(
                dx, sv, gain, fox_w[j][0], fox_w[j][1], fox_w[j][2])
        elif kind == "s5":
            dx, per_layer["mix_norm_g"][i], dw_in, dssm, dd, dw_glu = _s5_bwd(
                dx, sv, gain, w["s5_w_in"][j], ssm, w["s5_d"], w["s5_w_glu"][j])
            grads["s5_w_in"], grads["s5_w_glu"], grads["s5_d"] = dw_in[None], dw_glu[None], dd
            for n, g in zip(("s5_a_re", "s5_a_im", "s5_log_dt", "s5_b_re", "s5_b_im", "s5_c_re", "s5_c_im"), ssm_pull(dssm)):
                grads[n] = g[None]
        else:
            dx, per_layer["mix_norm_g"][i], dw, db, dsc = _pool_bwd(dx, sv, gain, w["pool_w"][j], w["pool_b"], w["pool_scale"])
            grads["pool_w"], grads["pool_b"], grads["pool_scale"] = dw[None].astype(BF16), db, dsc
    for n, parts in {**per_layer, **fox_g}.items():
        grads[n] = jnp.stack(parts)
    return loss, dx, grads


def kernel(x, mix_norm_g, ffn_norm_g, final_norm_g, fox_w_in, fox_b_f, fox_w_out, s5_w_in, s5_a_re, s5_a_im, s5_log_dt, s5_b_re, s5_b_im, s5_c_re, s5_c_im, s5_d, s5_w_glu, pool_w, pool_b, pool_scale, ffn_w_gate_up, ffn_w_down, loss_target, m_mix_norm_g, m_ffn_norm_g, m_final_norm_g, m_fox_w_in, m_fox_b_f, m_fox_w_out, m_s5_w_in, m_s5_a_re, m_s5_a_im, m_s5_log_dt, m_s5_b_re, m_s5_b_im, m_s5_c_re, m_s5_c_im, m_s5_d, m_s5_w_glu, m_pool_w, m_pool_b, m_pool_scale, m_ffn_w_gate_up, m_ffn_w_down, v_mix_norm_g, v_ffn_norm_g, v_final_norm_g, v_fox_w_in, v_fox_b_f, v_fox_w_out, v_s5_w_in, v_s5_a_re, v_s5_a_im, v_s5_log_dt, v_s5_b_re, v_s5_b_im, v_s5_c_re, v_s5_c_im, v_s5_d, v_s5_w_glu, v_pool_w, v_pool_b, v_pool_scale, v_ffn_w_gate_up, v_ffn_w_down):
    given = dict(locals())
    shard = {n: given[n] for n in WEIGHTS}
    chip = 2 * lax.axis_index("x") + lax.axis_index("y")
    core = lax.axis_index("c")

    views = {n: _halves_view(shard[n].shape) for n in BIG}
    own = [shard[n].astype(MXU_DTYPE).reshape(views[n]) for n in BIG]
    whole = {}
    for n, mine, others in zip(BIG, own, _gather_weights(own)):
        by_chip = lax.dynamic_update_index_in_dim(others, mine, chip, 0)
        whole[n] = _to_natural(by_chip.reshape((N_CHIPS,) + shard[n].shape), BIG[n])
    for n in SMALL:
        whole[n] = shard[n]
    sliced_shapes = [shard[n].shape for n in SLICED]
    by_chip = _gather_small(_pack_small([shard[n] for n in SLICED]))[0::2]
    slices = [_unpack_small(by_chip[k], sliced_shapes) for k in range(N_CHIPS)]
    for idx, n in enumerate(SLICED):
        whole[n] = jnp.concatenate([slices[k][idx] for k in range(N_CHIPS)], axis=-1)

    loss_part, dx, grads = _local_step(x[0], loss_target[0], whole)
    loss = lax.psum(loss_part, MESH_AXES)

    gs = [_to_chip_major(grads[n].astype(BF16), BIG[n]).reshape((N_CHIPS,) + views[n]) for n in BIG]
    core_id, chip_id = core.reshape(1).astype(jnp.int32), chip.reshape(1).astype(jnp.int32)
    partial = [_pair_sum(g, r, core_id, "pair_sum_" + n) for n, g, r in zip(BIG, gs, _swap_halves(gs))]
    half = [_chip_sum(p, r, chip_id, "chip_sum_" + n) for n, p, r in zip(BIG, partial, _scatter_partials(partial))]
    grad = {n: lax.dynamic_update_index_in_dim(both, mine, core, 0).reshape(shard[n].shape)
            for n, mine, both in zip(BIG, half, _share_half(half))}

    small_sum = _sum_blocks(_gather_small(_pack_small([grads[n] for n in SMALL])), "small_sum")
    for n, g in zip(SMALL, _unpack_small(small_sum, [whole[n].shape for n in SMALL])):
        grad[n] = g
    for n in SLICED:
        width = shard[n].shape[-1]
        grad[n] = lax.dynamic_slice_in_dim(grad[n], chip * width, width, axis=-1)

    delta, new_m, new_v = {}, {}, {}
    for n in BIG:
        view = (-1, shard[n].shape[-1])
        res = _adamw(shard[n].reshape(view), grad[n].reshape(view), given["m_" + n].reshape(view),
                     given["v_" + n].reshape(view), "adamw_" + n)
        delta[n], new_m[n], new_v[n] = (r.reshape(shard[n].shape) for r in res)
    small_shapes = [shard[n].shape for n in SMALL]
    res = _adamw(_pack_small([shard[n] for n in SMALL]), _pack_small([grad[n] for n in SMALL]),
                 _pack_small([given["m_" + n] for n in SMALL]), _pack_small([given["v_" + n] for n in SMALL]), "adamw_small")
    for out, buf in zip((delta, new_m, new_v), res):
        for n, a in zip(SMALL, _unpack_small(buf, small_shapes)):
            out[n] = a
    return (loss, dx[None], *[grad[n] for n in WEIGHTS], *[delta[n] for n in WEIGHTS],
            *[new_m[n] for n in WEIGHTS], *[new_v[n] for n in WEIGHTS])
```

```python
import functools
import math

import jax
import jax.numpy as jnp
import numpy as np
from jax import lax
from jax.experimental import pallas as pl
from jax.experimental.pallas import tpu as pltpu

F32 = jnp.float32
BF16 = jnp.bfloat16
MXU_DTYPE = jnp.bfloat16

D_MODEL = 1024
DEPTH = 4
EPS = 1e-6
FOX_HEADS = 16
FOX_HEAD_DIM = 64
HEAD_PAIRS = FOX_HEADS // 2
S5_GROUPS = 64
S5_GROUP = 16
S5_STATE = 64
S5_BLOCKS = 8
S5_HALF = 256
POOL_WINDOWS = (2, 4, 8, 16)
POOL_WIDTH = 256
D_FF = 2816
N_CHIPS = 4
N_DEV = 8
LANES = 128
SUBLANES = 8
VMEM_LIMIT = 56 * 1024 * 1024

ADAM_LR = 0.001
ADAM_B1 = 0.9
ADAM_B2 = 0.999
ADAM_EPS = 1e-08
ADAM_WD = 0.01
ADAM_STEP = 10

MESH_AXES = ("x", "y", "c")


def _tile(n, want):
    t = (min(n, want) // LANES) * LANES
    while t >= LANES:
        if n % t == 0:
            return t
        t -= LANES
    return n


def _params(sem=None):
    return pltpu.CompilerParams(dimension_semantics=sem, vmem_limit_bytes=VMEM_LIMIT)


def _mm(a, b, *, name, ta=False, tb=False, out_dtype=F32, add=None, tm=1024, tn=1024, tk=1024):
    m, k = (a.shape[1], a.shape[0]) if ta else a.shape
    n = b.shape[0] if tb else b.shape[1]
    assert (b.shape[1] if tb else b.shape[0]) == k, (a.shape, b.shape, ta, tb)
    tm, tn, tk = _tile(m, tm), _tile(n, tn), _tile(k, tk)
    nk = k // tk
    a_spec = pl.BlockSpec((tk, tm), lambda i, j, kk: (kk, i)) if ta else pl.BlockSpec((tm, tk), lambda i, j, kk: (i, kk))
    b_spec = pl.BlockSpec((tn, tk), lambda i, j, kk: (j, kk)) if tb else pl.BlockSpec((tk, tn), lambda i, j, kk: (kk, j))
    o_spec = pl.BlockSpec((tm, tn), lambda i, j, kk: (i, j))
    dims = (((0 if ta else 1,), (1 if tb else 0,)), ((), ()))
    has_add = add is not None

    def body(*refs):
        if has_add:
            a_ref, b_ref, add_ref, o_ref, acc_ref = refs
        else:
            a_ref, b_ref, o_ref, acc_ref = refs
        kk = pl.program_id(2)

        @pl.when(kk == 0)
        def _():
            acc_ref[...] = jnp.zeros_like(acc_ref)

        acc_ref[...] += lax.dot_general(a_ref[...].astype(MXU_DTYPE), b_ref[...].astype(MXU_DTYPE), dims,
                                        preferred_element_type=F32)

        @pl.when(kk == nk - 1)
        def _():
            r = acc_ref[...]
            if has_add:
                r = r + add_ref[...].astype(F32)
            o_ref[...] = r.astype(out_dtype)

    ins = [a, b] + ([add] if has_add else [])
    specs = [a_spec, b_spec] + ([o_spec] if has_add else [])
    return pl.pallas_call(
        body, name=name, grid=(m // tm, n // tn, nk), in_specs=specs, out_specs=o_spec,
        out_shape=jax.ShapeDtypeStruct((m, n), out_dtype), scratch_shapes=[pltpu.VMEM((tm, tn), F32)],
        compiler_params=_params(("parallel", "parallel", "arbitrary")))(*ins)


def _ew(fn, tens, vecs=(), *, outs=(), sums=(), name, tr=256):
    tens = [t if isinstance(t, tuple) else (t, t.shape[1], 0) for t in tens]
    rows = tens[0][0].shape[0]
    tr = min(tr, rows)
    n_t, n_v, n_o, n_s = len(tens), len(vecs), len(outs), len(sums)

    def body(*refs):
        i = pl.program_id(0)
        t_blocks = [r[...] for r in refs[:n_t]]
        v_blocks = [r[...] for r in refs[n_t:n_t + n_v]]
        o_refs = refs[n_t + n_v:n_t + n_v + n_o]
        s_refs = refs[n_t + n_v + n_o:]
        o_vals, s_vals = fn(*t_blocks, *v_blocks)
        for r, v in zip(o_refs, o_vals):
            r[...] = v.astype(r.dtype)
        if n_s:
            @pl.when(i == 0)
            def _():
                for r in s_refs:
                    r[...] = jnp.zeros_like(r)
            for r, v in zip(s_refs, s_vals):
                r[...] += jnp.sum(v.astype(F32), axis=0, keepdims=True)

    in_specs = [pl.BlockSpec((tr, w), functools.partial(lambda i, cb: (i, cb), cb=cb)) for _, w, cb in tens]
    in_specs += [pl.BlockSpec(v.shape, functools.partial(lambda i, nd: (0,) * nd, nd=v.ndim)) for v in vecs]
    out_specs = [pl.BlockSpec((tr, c), lambda i: (i, 0)) for c, _ in outs]
    out_specs += [pl.BlockSpec((1, c), lambda i: (0, 0)) for c in sums]
    out_shape = [jax.ShapeDtypeStruct((rows, c), dt) for c, dt in outs]
    out_shape += [jax.ShapeDtypeStruct((1, c), F32) for c in sums]
    res = pl.pallas_call(
        body, name=name, grid=(rows // tr,), in_specs=in_specs, out_specs=out_specs, out_shape=out_shape,
        compiler_params=_params(("arbitrary",)))(*[t[0] for t in tens], *vecs)
    return res


def _sigmoid(z):
    return 1.0 / (1.0 + jnp.exp(-z))


def _rms_fwd(x, g, name):
    def fn(xb, gb):
        r = lax.rsqrt(jnp.mean(xb * xb, axis=-1, keepdims=True) + EPS)
        return ((xb * r) * gb,), ()
    return _ew(fn, [x], [g.reshape(1, -1)], outs=[(x.shape[1], BF16)], name=name)[0]


def _rms_bwd(x, g, dh, dres, name):
    def fn(xb, dhb, drb, gb):
        r = lax.rsqrt(jnp.mean(xb * xb, axis=-1, keepdims=True) + EPS)
        xh = xb * r
        dhf = dhb.astype(F32)
        dy = dhf * gb
        dx = r * (dy - xh * jnp.mean(dy * xh, axis=-1, keepdims=True))
        return (drb + dx,), (dhf * xh,)
    dx, dg = _ew(fn, [x, dh, dres], [g.reshape(1, -1)], outs=[(x.shape[1], F32)], sums=[x.shape[1]], name=name)
    return dx, dg[0]


def _mm_rms_bwd(a, b, x, g, dres, *, name, add=None, tm=512, tk=1024):
    m, k = a.shape
    n = b.shape[0]
    assert b.shape[1] == k and x.shape == (m, n)
    tm, tk = _tile(m, tm), _tile(k, tk)
    nk = k // tk
    has_add = add is not None

    def body(*refs):
        a_ref, b_ref, x_ref, dr_ref, g_ref = refs[:5]
        add_ref = refs[5] if has_add else None
        dx_ref, dg_ref, acc_ref = refs[-3:]
        i, kk = pl.program_id(0), pl.program_id(1)

        @pl.when(kk == 0)
        def _():
            acc_ref[...] = jnp.zeros_like(acc_ref)

        @pl.when((kk == 0) & (i == 0))
        def _():
            dg_ref[...] = jnp.zeros_like(dg_ref)

        acc_ref[...] += lax.dot_general(a_ref[...].astype(MXU_DTYPE), b_ref[...].astype(MXU_DTYPE),
                                        (((1,), (1,)), ((), ())), preferred_element_type=F32)

        @pl.when(kk == nk - 1)
        def _():
            dh = acc_ref[...]
            if has_add:
                dh = dh + add_ref[...]
            xb = x_ref[...]
            r = lax.rsqrt(jnp.mean(xb * xb, axis=-1, keepdims=True) + EPS)
            xh = xb * r
            dy = dh * g_ref[...]
            dx_ref[...] = dr_ref[...] + r * (dy - xh * jnp.mean(dy * xh, axis=-1, keepdims=True))
            dg_ref[...] += jnp.sum(dh * xh, axis=0, keepdims=True)

    row = pl.BlockSpec((tm, n), lambda i, kk: (i, 0))
    vec = pl.BlockSpec((1, n), lambda i, kk: (0, 0))
    ins = [a, b, x, dres, g.reshape(1, n)] + ([add] if has_add else [])
    specs = [pl.BlockSpec((tm, tk), lambda i, kk: (i, kk)), pl.BlockSpec((n, tk), lambda i, kk: (0, kk)), row, row, vec]
    specs += [row] if has_add else []
    dx, dg = pl.pallas_call(
        body, name=name, grid=(m // tm, nk), in_specs=specs, out_specs=[row, vec],
        out_shape=[jax.ShapeDtypeStruct((m, n), F32), jax.ShapeDtypeStruct((1, n), F32)],
        scratch_shapes=[pltpu.VMEM((tm, n), F32)],
        compiler_params=_params(("arbitrary", "arbitrary")))(*ins)
    return dx, dg[0]


FFN_ROWS = 512
FFN_COLS = D_FF // 2


def _ffn_gate_up(x, gain, w_gu):
    s, d = x.shape
    tm = min(FFN_ROWS, s)
    halves = D_FF // FFN_COLS

    def body(x_ref, gain_ref, wg_ref, wu_ref, h_ref, g_ref, u_ref, a_ref):
        @pl.when(pl.program_id(1) == 0)
        def _():
            xb = x_ref[...]
            h_ref[...] = ((xb * lax.rsqrt(jnp.mean(xb * xb, axis=-1, keepdims=True) + EPS)) * gain_ref[...]).astype(h_ref.dtype)

        hb = h_ref[...].astype(MXU_DTYPE)
        g = jnp.dot(hb, wg_ref[...], preferred_element_type=F32)
        u = jnp.dot(hb, wu_ref[...], preferred_element_type=F32)
        g_ref[...] = g.astype(g_ref.dtype)
        u_ref[...] = u.astype(u_ref.dtype)
        a_ref[...] = (g * _sigmoid(g) * u).astype(a_ref.dtype)

    row = pl.BlockSpec((tm, d), lambda i, jj: (i, 0))
    tile = pl.BlockSpec((tm, FFN_COLS), lambda i, jj: (i, jj))
    return pl.pallas_call(
        body, name="ffn_gate_up", grid=(s // tm, halves),
        in_specs=[row, pl.BlockSpec((1, d), lambda i, jj: (0, 0)),
                  pl.BlockSpec((d, FFN_COLS), lambda i, jj: (0, jj)),
                  pl.BlockSpec((d, FFN_COLS), lambda i, jj: (0, halves + jj))],
        out_specs=[row, tile, tile, tile],
        out_shape=[jax.ShapeDtypeStruct((s, d), BF16)] + [jax.ShapeDtypeStruct((s, D_FF), BF16)] * 3,
        compiler_params=_params(("parallel", "arbitrary")))(x, gain.reshape(1, d), w_gu, w_gu)


def _ffn_dgate_up(dx2, w_down, g, u):
    s, d = dx2.shape
    tm = min(FFN_ROWS, s)
    halves = D_FF // FFN_COLS

    def body(dx_ref, w_ref, g_ref, u_ref, o_ref):
        jj = pl.program_id(1)
        df = lax.dot_general(dx_ref[...].astype(MXU_DTYPE), w_ref[...], (((1,), (1,)), ((), ())), preferred_element_type=F32)
        gf, uf = g_ref[...].astype(F32), u_ref[...].astype(F32)
        sg = _sigmoid(gf)
        dg = df * uf * (sg * (1.0 + gf * (1.0 - sg)))
        du = df * (gf * sg)
        o_ref[:, pl.ds(pl.multiple_of(jj * FFN_COLS, LANES), FFN_COLS)] = dg.astype(o_ref.dtype)
        o_ref[:, pl.ds(pl.multiple_of(D_FF + jj * FFN_COLS, LANES), FFN_COLS)] = du.astype(o_ref.dtype)

    tile = pl.BlockSpec((tm, FFN_COLS), lambda i, jj: (i, jj))
    return pl.pallas_call(
        body, name="ffn_dgate_up", grid=(s // tm, halves),
        in_specs=[pl.BlockSpec((tm, d), lambda i, jj: (i, 0)),
                  pl.BlockSpec((FFN_COLS, d), lambda i, jj: (jj, 0)), tile, tile],
        out_specs=pl.BlockSpec((tm, 2 * D_FF), lambda i, jj: (i, 0)),
        out_shape=jax.ShapeDtypeStruct((s, 2 * D_FF), BF16),
        compiler_params=_params(("parallel", "arbitrary")))(dx2, w_down, g, u)


def _ffn_fwd(x1, gain, w_gu, w_down):
    h, g, u, act = _ffn_gate_up(x1, gain, w_gu)
    x2 = _mm(act, w_down, name="ffn_down", add=x1, tk=D_FF // 2)
    return x2, (x1, h, g, u, act)


def _ffn_bwd(dx2, saved, gain, w_gu, w_down):
    x1, h, g, u, act = saved
    dw_down = _mm(act, dx2, ta=True, name="ffn_dw_down", out_dtype=BF16, tm=D_FF // 2)
    dgu = _ffn_dgate_up(dx2, w_down, g, u)
    dw_gu = _mm(h, dgu, ta=True, name="ffn_dw_gu", out_dtype=BF16, tn=D_FF // 2)
    dx1, dgain = _mm_rms_bwd(dgu, w_gu, x1, gain, dx2, name="ffn_dh", tm=1024, tk=512)
    return dx1, dgain, dw_gu, dw_down


def _loss_head(x, gain, target):
    d = x.shape[1]

    def fn(xb, tb, gb):
        r = lax.rsqrt(jnp.mean(xb * xb, axis=-1, keepdims=True) + EPS)
        xh = xb * r
        y = xh * gb
        err = y - tb
        dyv = err * (1.0 / d)
        dyg = dyv * gb
        dx = r * (dyg - xh * jnp.mean(dyg * xh, axis=-1, keepdims=True))
        return (dx,), (0.5 * err * err * (1.0 / d), dyv * xh)
    dx, lsum, dg = _ew(fn, [x, target], [gain.reshape(1, -1)], outs=[(d, F32)], sums=[d, d], name="loss_head")
    return jnp.sum(lsum), dx, dg[0]


ATT_BLOCK = 256
CUM_BLOCK = 512
NEG_INF = -1e30


def _fox_gate_fwd(fl_row, b_col):
    nh, s = fl_row.shape
    tb = min(CUM_BLOCK, s)

    def body(fl_ref, b_ref, z_ref, c_ref):
        upper = (lax.broadcasted_iota(jnp.int32, (tb, tb), 0) <= lax.broadcasted_iota(jnp.int32, (tb, tb), 1)).astype(F32)
        carry = jnp.zeros((nh, 1), F32)
        for blk in range(s // tb):
            z = fl_ref[:, blk * tb:(blk + 1) * tb] + b_ref[...]
            logf = jnp.minimum(z, 0.0) - jnp.log(1.0 + jnp.exp(-jnp.abs(z)))
            cs = jnp.dot(logf, upper, precision=lax.Precision.HIGHEST, preferred_element_type=F32) + carry
            z_ref[:, blk * tb:(blk + 1) * tb] = z
            c_ref[:, blk * tb:(blk + 1) * tb] = cs
            carry = cs[:, tb - 1:tb]

    return pl.pallas_call(body, name="fox_gate_fwd", out_shape=[jax.ShapeDtypeStruct((nh, s), F32)] * 2,
                          compiler_params=_params())(fl_row, b_col)


def _fox_gate_bwd(dc_row, z_row):
    nh, s = dc_row.shape
    tb = min(CUM_BLOCK, s)

    def body(dc_ref, z_ref, dz_ref, db_ref):
        lower = (lax.broadcasted_iota(jnp.int32, (tb, tb), 0) >= lax.broadcasted_iota(jnp.int32, (tb, tb), 1)).astype(F32)
        carry = jnp.zeros((nh, 1), F32)
        db = jnp.zeros((nh, 1), F32)
        for blk in reversed(range(s // tb)):
            dc = dc_ref[:, blk * tb:(blk + 1) * tb]
            rs = jnp.dot(dc, lower, precision=lax.Precision.HIGHEST, preferred_element_type=F32) + carry
            dz = rs * _sigmoid(-z_ref[:, blk * tb:(blk + 1) * tb])
            dz_ref[:, blk * tb:(blk + 1) * tb] = dz
            db = db + jnp.sum(dz, axis=1, keepdims=True)
            carry = rs[:, 0:1]
        db_ref[...] = db

    return pl.pallas_call(body, name="fox_gate_bwd",
                          out_shape=[jax.ShapeDtypeStruct((nh, s), F32), jax.ShapeDtypeStruct((nh, 1), F32)],
                          compiler_params=_params())(dc_row, z_row)


def _head_masks(rows):
    lane = lax.broadcasted_iota(jnp.int32, (rows, LANES), 1)
    return lane < FOX_HEAD_DIM


def _attn_fwd(qkv, c_row):
    s = qkv.shape[0]
    t = min(ATT_BLOCK, s)
    nq = s // t
    scale = FOX_HEAD_DIM ** -0.5

    def body(q_ref, k_ref, v_ref, cr_ref, o_ref, lse_ref):
        i = pl.program_id(1)
        first = _head_masks(t)
        qs = q_ref[...] * scale
        zero = jnp.zeros_like(qs)
        qh = (jnp.where(first, qs, zero), jnp.where(first, zero, qs))
        causal = lax.broadcasted_iota(jnp.int32, (t, t), 0) >= lax.broadcasted_iota(jnp.int32, (t, t), 1)

        def block(j, carry, masked):
            ms, ls, acc = carry
            start = pl.multiple_of(j * t, t)
            ks, vs = k_ref[pl.ds(start, t), :], v_ref[pl.ds(start, t), :]
            new_m, new_l, alphas, pvs = [], [], [], []
            for hh in range(2):
                sc = lax.dot_general(qh[hh], ks, (((1,), (1,)), ((), ())), preferred_element_type=F32)
                sc = sc - cr_ref[0, hh:hh + 1, pl.ds(start, t)]
                if masked:
                    sc = jnp.where(causal, sc, NEG_INF)
                m_new = jnp.maximum(ms[hh], jnp.max(sc, axis=1, keepdims=True))
                p = jnp.exp(sc - m_new)
                alpha = jnp.exp(ms[hh] - m_new)
                new_m.append(m_new)
                new_l.append(alpha * ls[hh] + jnp.sum(p, axis=1, keepdims=True))
                alphas.append(alpha)
                p_hi = p.astype(MXU_DTYPE)
                p_lo = (p - p_hi.astype(F32)).astype(MXU_DTYPE)
                pvs.append(jnp.dot(p_hi, vs, preferred_element_type=F32) + jnp.dot(p_lo, vs, preferred_element_type=F32))
            acc = jnp.where(first, alphas[0], alphas[1]) * acc + jnp.where(first, pvs[0], pvs[1])
            return tuple(new_m), tuple(new_l), acc

        neg, nil = jnp.full((t, 1), NEG_INF, F32), jnp.zeros((t, 1), F32)
        carry = lax.fori_loop(0, i, functools.partial(block, masked=False), ((neg, neg), (nil, nil), jnp.zeros((t, LANES), F32)))
        ms, ls, acc = block(i, carry, True)
        o_ref[...] = (acc / jnp.where(first, ls[0], ls[1])).astype(o_ref.dtype)
        lse_ref[0] = jnp.concatenate([ms[0] + jnp.log(ls[0]), ms[1] + jnp.log(ls[1])], axis=1)

    np_ = HEAD_PAIRS
    return pl.pallas_call(
        body, name="fox_attn_fwd", grid=(np_, nq),
        in_specs=[pl.BlockSpec((t, LANES), lambda p, i: (i, p)),
                  pl.BlockSpec((s, LANES), lambda p, i: (0, np_ + p)),
                  pl.BlockSpec((s, LANES), lambda p, i: (0, 2 * np_ + p)),
                  pl.BlockSpec((1, 2, s), lambda p, i: (p, 0, 0))],
        out_specs=[pl.BlockSpec((t, LANES), lambda p, i: (i, p)),
                   pl.BlockSpec((1, t, 2), lambda p, i: (p, i, 0))],
        out_shape=[jax.ShapeDtypeStruct((s, D_MODEL), F32), jax.ShapeDtypeStruct((np_, s, 2), F32)],
        compiler_params=_params(("parallel", "arbitrary")))(qkv, qkv, qkv, c_row)


def _attn_bwd(qkv, do, lse, delta, c_row):
    s = qkv.shape[0]
    t = min(ATT_BLOCK, s)
    nb = s // t
    scale = FOX_HEAD_DIM ** -0.5
    np_ = HEAD_PAIRS

    def body(q_ref, k_ref, v_ref, do_ref, lse_ref, dl_ref, cr_ref, dq_ref, dk_ref, dv_ref, dc_ref):
        j = pl.program_id(1)
        first = _head_masks(t)
        causal = lax.broadcasted_iota(jnp.int32, (t, t), 0) >= lax.broadcasted_iota(jnp.int32, (t, t), 1)
        kb = k_ref[...]
        vb = v_ref[...]

        @pl.when(j == 0)
        def _():
            dq_ref[...] = jnp.zeros_like(dq_ref)

        def step(i, carry, masked):
            dk_acc, dv_acc, dc_accs = carry
            rows = pl.ds(pl.multiple_of(i * t, t), t)
            qs = q_ref[rows, :] * scale
            dob = do_ref[rows, :]
            zero = jnp.zeros_like(qs)
            dks, dvs, dqs, dcs = [], [], [], []
            for hh in range(2):
                qh = jnp.where(first, qs, zero) if hh == 0 else jnp.where(first, zero, qs)
                doh = jnp.where(first, dob, zero) if hh == 0 else jnp.where(first, zero, dob)
                sc = lax.dot_general(qh, kb, (((1,), (1,)), ((), ())), preferred_element_type=F32)
                p = jnp.exp(sc - cr_ref[0, hh:hh + 1, :] - lse_ref[0, rows, hh:hh + 1])
                if masked:
                    p = jnp.where(causal, p, 0.0)
                dp = lax.dot_general(doh, vb, (((1,), (1,)), ((), ())), preferred_element_type=F32)
                ds = p * (dp - dl_ref[0, rows, hh:hh + 1])
                pb, dsb = p.astype(MXU_DTYPE), ds.astype(MXU_DTYPE)
                dvs.append(lax.dot_general(pb, dob, (((0,), (0,)), ((), ())), preferred_element_type=F32))
                dks.append(lax.dot_general(dsb, qs, (((0,), (0,)), ((), ())), preferred_element_type=F32))
                dqs.append(jnp.dot(dsb, kb, preferred_element_type=F32))
                dcs.append(dc_accs[hh] - jnp.sum(ds, axis=0, keepdims=True))
            dq_ref[rows, :] += jnp.where(first, dqs[0], dqs[1]) * scale
            return (dk_acc + jnp.where(first, dks[0], dks[1]), dv_acc + jnp.where(first, dvs[0], dvs[1]), tuple(dcs))

        nil = jnp.zeros((1, t), F32)
        carry = step(j, (jnp.zeros((t, LANES), F32), jnp.zeros((t, LANES), F32), (nil, nil)), True)
        dk_acc, dv_acc, dc_accs = lax.fori_loop(j + 1, nb, functools.partial(step, masked=False), carry)
        dk_ref[...] = dk_acc.astype(dk_ref.dtype)
        dv_ref[...] = dv_acc.astype(dv_ref.dtype)
        dc_ref[0] = jnp.concatenate(dc_accs, axis=0)

    return pl.pallas_call(
        body, name="fox_attn_bwd", grid=(np_, nb),
        in_specs=[pl.BlockSpec((s, LANES), lambda p, j: (0, p)),
                  pl.BlockSpec((t, LANES), lambda p, j: (j, np_ + p)),
                  pl.BlockSpec((t, LANES), lambda p, j: (j, 2 * np_ + p)),
                  pl.BlockSpec((s, LANES), lambda p, j: (0, p)),
                  pl.BlockSpec((1, s, 2), lambda p, j: (p, 0, 0)),
                  pl.BlockSpec((1, s, 2), lambda p, j: (p, 0, 0)),
                  pl.BlockSpec((1, 2, t), lambda p, j: (p, 0, j))],
        out_specs=[pl.BlockSpec((s, LANES), lambda p, j: (0, p)),
                   pl.BlockSpec((t, LANES), lambda p, j: (j, p)),
                   pl.BlockSpec((t, LANES), lambda p, j: (j, p)),
                   pl.BlockSpec((1, 2, t), lambda p, j: (p, 0, j))],
        out_shape=[jax.ShapeDtypeStruct((s, D_MODEL), F32), jax.ShapeDtypeStruct((s, D_MODEL), BF16),
                   jax.ShapeDtypeStruct((s, D_MODEL), BF16), jax.ShapeDtypeStruct((np_, 2, s), F32)],
        compiler_params=_params(("parallel", "arbitrary")))(qkv, qkv, qkv, do, lse, delta, c_row)


ATT_QUERIES = 512


def _lanes(a, width):
    return jnp.concatenate([a] * (width // LANES), axis=1)


def _attn_fwd_t(qkv, c_lanes):
    s = qkv.shape[0]
    t, tq = min(ATT_BLOCK, s), min(2 * ATT_QUERIES, s)
    nq, per = s // tq, tq // t
    scale = FOX_HEAD_DIM ** -0.5
    np_ = HEAD_PAIRS
    nt = (((1,), (1,)), ((), ()))

    def body(q_ref, k_ref, v_ref, c_ref, o_ref, lse_ref, vt_ref):
        i = pl.program_id(1)

        @pl.when(i == 0)
        def _():
            for r in range(s // t):
                vt_ref[:, r * t:(r + 1) * t] = v_ref[r * t:(r + 1) * t, :].astype(F32).T.astype(vt_ref.dtype)

        first = _head_masks(tq)
        upper = lax.broadcasted_iota(jnp.int32, (LANES, tq), 0) < FOX_HEAD_DIM
        qs = q_ref[...] * scale
        zero = jnp.zeros_like(qs)
        qh = (jnp.where(first, qs, zero), jnp.where(first, zero, qs))
        key_at = lax.broadcasted_iota(jnp.int32, (t, tq), 0)
        query_at = lax.broadcasted_iota(jnp.int32, (t, tq), 1)

        def block(j, carry, diagonal=None):
            ms, ls, acc = carry
            start = pl.multiple_of(j * t, t)
            kb, vt = k_ref[pl.ds(start, t), :], vt_ref[:, pl.ds(start, t)]
            new_m, new_l, alphas, pvs = [], [], [], []
            for hh in range(2):
                sc = lax.dot_general(kb, qh[hh], nt, preferred_element_type=F32) - _lanes(c_ref[0, hh, pl.ds(start, t), :], tq)
                if diagonal is not None:
                    sc = jnp.where(key_at + diagonal * t <= query_at, sc, NEG_INF)
                m_new = jnp.maximum(ms[hh], jnp.max(sc, axis=0, keepdims=True))
                p = jnp.exp(sc - m_new)
                alpha = jnp.exp(ms[hh] - m_new)
                new_m.append(m_new)
                new_l.append(alpha * ls[hh] + jnp.sum(p, axis=0, keepdims=True))
                alphas.append(alpha)
                p_hi = p.astype(MXU_DTYPE)
                p_lo = (p - p_hi.astype(F32)).astype(MXU_DTYPE)
                pvs.append(jnp.dot(vt, p_hi, preferred_element_type=F32) + jnp.dot(vt, p_lo, preferred_element_type=F32))
            acc = jnp.where(upper, alphas[0], alphas[1]) * acc + jnp.where(upper, pvs[0], pvs[1])
            return tuple(new_m), tuple(new_l), acc

        neg, nil = jnp.full((1, tq), NEG_INF, F32), jnp.zeros((1, tq), F32)
        carry = lax.fori_loop(0, per * i, block, ((neg, neg), (nil, nil), jnp.zeros((LANES, tq), F32)))
        for d in range(per):
            carry = block(per * i + d, carry, diagonal=d)
        ms, ls, acc = carry
        o_ref[...] = (acc / jnp.where(upper, ls[0], ls[1])).T.astype(o_ref.dtype)
        lse_ref[0] = jnp.concatenate([ms[0] + jnp.log(ls[0]), ms[1] + jnp.log(ls[1])], axis=0)

    return pl.pallas_call(
        body, name="fox_attn_fwd", grid=(np_, nq),
        in_specs=[pl.BlockSpec((tq, LANES), lambda p, i: (i, p)),
                  pl.BlockSpec((s, LANES), lambda p, i: (0, np_ + p)),
                  pl.BlockSpec((s, LANES), lambda p, i: (0, 2 * np_ + p)),
                  pl.BlockSpec((1, 2, s, LANES), lambda p, i: (p, 0, 0, 0))],
        out_specs=[pl.BlockSpec((tq, LANES), lambda p, i: (i, p)),
                   pl.BlockSpec((1, 2, tq), lambda p, i: (p, 0, i))],
        out_shape=[jax.ShapeDtypeStruct((s, D_MODEL), F32), jax.ShapeDtypeStruct((np_, 2, s), F32)],
        scratch_shapes=[pltpu.VMEM((LANES, s), MXU_DTYPE)],
        compiler_params=_params(("parallel", "arbitrary")))(qkv, qkv, qkv, c_lanes)


def _attn_bwd_t(qkv, do, lse, delta, c_lanes):
    s = qkv.shape[0]
    t, tq = min(ATT_BLOCK, s), min(ATT_QUERIES, s)
    nb, nq, per = s // t, s // tq, tq // t
    scale = FOX_HEAD_DIM ** -0.5
    np_ = HEAD_PAIRS
    nt = (((1,), (1,)), ((), ()))

    def body(q_ref, k_ref, v_ref, do_ref, lse_ref, dl_ref, c_ref, dq_ref, dk_ref, dv_ref, dc_ref, dqt_ref):
        j = pl.program_id(1)
        first_q, first = _head_masks(tq), _head_masks(t)
        upper = lax.broadcasted_iota(jnp.int32, (LANES, tq), 0) < FOX_HEAD_DIM
        causal = (lax.broadcasted_iota(jnp.int32, (t, tq), 0) + (j % per) * t) <= lax.broadcasted_iota(jnp.int32, (t, tq), 1)
        kb, vb = k_ref[...], v_ref[...]
        kt = kb.astype(F32).T.astype(MXU_DTYPE)
        cb = (_lanes(c_ref[0, 0], tq), _lanes(c_ref[0, 1], tq))

        @pl.when(j == 0)
        def _():
            dqt_ref[...] = jnp.zeros_like(dqt_ref)

        def step(i, carry, masked):
            dk_acc, dv_acc, dc_accs = carry
            start = pl.multiple_of(i * tq, tq)
            qs = q_ref[pl.ds(start, tq), :] * scale
            dob = do_ref[pl.ds(start, tq), :]
            zero = jnp.zeros_like(qs)
            dks, dvs, dqs, dcs = [], [], [], []
            for hh in range(2):
                qh = jnp.where(first_q, qs, zero) if hh == 0 else jnp.where(first_q, zero, qs)
                doh = jnp.where(first_q, dob, zero) if hh == 0 else jnp.where(first_q, zero, dob)
                sc = lax.dot_general(kb, qh, nt, preferred_element_type=F32)
                p = jnp.exp(sc - cb[hh] - lse_ref[0, hh:hh + 1, pl.ds(start, tq)])
                if masked:
                    p = jnp.where(causal, p, 0.0)
                dp = lax.dot_general(vb, doh, nt, preferred_element_type=F32)
                ds = p * (dp - dl_ref[0, hh:hh + 1, pl.ds(start, tq)])
                pb, dsb = p.astype(MXU_DTYPE), ds.astype(MXU_DTYPE)
                dvs.append(jnp.dot(pb, dob, preferred_element_type=F32))
                dks.append(jnp.dot(dsb, qs, preferred_element_type=F32))
                dqs.append(jnp.dot(kt, dsb, preferred_element_type=F32))
                dcs.append(dc_accs[hh] - jnp.sum(ds, axis=1, keepdims=True))
            dqt_ref[:, pl.ds(start, tq)] += jnp.where(upper, dqs[0], dqs[1]) * scale
            return (dk_acc + jnp.where(first, dks[0], dks[1]), dv_acc + jnp.where(first, dvs[0], dvs[1]), tuple(dcs))

        nil, col = jnp.zeros((t, LANES), F32), jnp.zeros((t, 1), F32)
        carry = step(j // per, (nil, nil, (col, col)), True)
        dk_acc, dv_acc, dc_accs = lax.fori_loop(j // per + 1, nq, functools.partial(step, masked=False), carry)
        dk_ref[...] = dk_acc.astype(dk_ref.dtype)
        dv_ref[...] = dv_acc.astype(dv_ref.dtype)
        dc_ref[0, 0] = jnp.broadcast_to(dc_accs[0], (t, LANES))
        dc_ref[0, 1] = jnp.broadcast_to(dc_accs[1], (t, LANES))

        @pl.when(j == nb - 1)
        def _():
            for r in range(nb):
                dq_ref[r * t:(r + 1) * t, :] = dqt_ref[:, r * t:(r + 1) * t].T

    row = pl.BlockSpec((1, 2, s), lambda p, j: (p, 0, 0))
    return pl.pallas_call(
        body, name="fox_attn_bwd", grid=(np_, nb),
        in_specs=[pl.BlockSpec((s, LANES), lambda p, j: (0, p)),
                  pl.BlockSpec((t, LANES), lambda p, j: (j, np_ + p)),
                  pl.BlockSpec((t, LANES), lambda p, j: (j, 2 * np_ + p)),
                  pl.BlockSpec((s, LANES), lambda p, j: (0, p)), row, row,
                  pl.BlockSpec((1, 2, t, LANES), lambda p, j: (p, 0, j, 0))],
        out_specs=[pl.BlockSpec((s, LANES), lambda p, j: (0, p)),
                   pl.BlockSpec((t, LANES), lambda p, j: (j, p)),
                   pl.BlockSpec((t, LANES), lambda p, j: (j, p)),
                   pl.BlockSpec((1, 2, t, LANES), lambda p, j: (p, 0, j, 0))],
        out_shape=[jax.ShapeDtypeStruct((s, D_MODEL), F32), jax.ShapeDtypeStruct((s, D_MODEL), BF16),
                   jax.ShapeDtypeStruct((s, D_MODEL), BF16), jax.ShapeDtypeStruct((np_, 2, s, LANES), F32)],
        scratch_shapes=[pltpu.VMEM((LANES, s), F32)],
        compiler_params=_params(("parallel", "arbitrary")))(qkv, qkv, qkv, do, lse, delta, c_lanes)


def _head_sums(a, b, name):
    d = a.shape[1]
    sel = (jnp.arange(d)[:, None] // FOX_HEAD_DIM == jnp.arange(LANES)[None, :]).astype(F32)

    def fn(ab, bb, selb):
        prod = ab.astype(F32) * bb.astype(F32)
        return (jnp.dot(prod, selb, precision=lax.Precision.HIGHEST, preferred_element_type=F32),), ()
    return _ew(fn, [a, b], [sel], outs=[(LANES, F32)], name=name)[0]


def _pairs_col(a16):
    s = a16.shape[0]
    return a16.reshape(s, HEAD_PAIRS, 2).transpose(1, 0, 2)


def _fox_fwd(x, gain, w_qkv, w_f, b_f, w_out):
    s = x.shape[0]
    h = _rms_fwd(x, gain, "mix_norm")
    qkv = _mm(h, w_qkv, name="fox_qkv", out_dtype=BF16)
    fl = _mm(h, w_f, name="fox_f", tn=LANES)
    z_row, c_rowf = _fox_gate_fwd(fl[:, :FOX_HEADS].T, b_f.reshape(FOX_HEADS, 1))
    c_lanes = jnp.broadcast_to(c_rowf.reshape(HEAD_PAIRS, 2, s, 1), (HEAD_PAIRS, 2, s, LANES))
    o, lse = _attn_fwd_t(qkv, c_lanes)
    x1 = _mm(o, w_out, name="fox_out", add=x)
    return x1, (x, h, qkv, z_row, c_lanes, o, lse)


def _fox_bwd(dx1, saved, gain, w_qkv, w_f, w_out):
    x, h, qkv, z_row, c_lanes, o, lse = saved
    s = x.shape[0]
    do = _mm(dx1, w_out, tb=True, name="fox_do", out_dtype=BF16)
    dw_out = _mm(o, dx1, ta=True, name="fox_dw_out", out_dtype=BF16)
    delta = _head_sums(do, o, "fox_delta")[:, :FOX_HEADS].T.reshape(HEAD_PAIRS, 2, s)
    dq, dk, dv, dc = _attn_bwd_t(qkv, do, lse, delta, c_lanes)
    dz_row, db = _fox_gate_bwd(dc[..., 0].reshape(FOX_HEADS, s), z_row)
    dqkv = jnp.concatenate([dq.astype(BF16), dk, dv], axis=1)
    dfl = jnp.pad(dz_row.T, ((0, 0), (0, LANES - FOX_HEADS))).astype(BF16)
    dw_qkv = _mm(h, dqkv, ta=True, name="fox_dw_qkv", out_dtype=BF16)
    dw_f = _mm(h, dfl, ta=True, name="fox_dw_f", out_dtype=BF16, tn=LANES)
    dh = _mm(dqkv, w_qkv, tb=True, name="fox_dh_qkv")
    dx, dgain = _mm_rms_bwd(dfl, w_f, x, gain, dx1, name="fox_dh_f", add=dh)
    dw_in = jnp.concatenate([dw_qkv, dw_f[:, :FOX_HEADS]], axis=1)
    return dx, dgain, dw_in, db.reshape(FOX_HEADS), dw_out


S5_ROWS = 512
SCAN_CHUNKS = SUBLANES


def _s5_operands(a_re, a_im, log_dt, b_re, b_im, c_re, c_im):
    dt = jnp.exp(log_dt)[:, None]
    mag, ang = jnp.exp(a_re * dt), a_im * dt
    lr, li = mag * jnp.cos(ang), mag * jnp.sin(ang)
    den = a_re * a_re + a_im * a_im
    cr = ((lr - 1.0) * a_re + li * a_im) / den
    ci = (li * a_re - (lr - 1.0) * a_im) / den
    bbr = cr[..., None] * b_re - ci[..., None] * b_im
    bbi = cr[..., None] * b_im + ci[..., None] * b_re
    nb = S5_BLOCKS
    lam = jnp.stack([lr.reshape(nb, 2, S5_HALF), li.reshape(nb, 2, S5_HALF)], axis=2)
    eye4, eye2 = jnp.eye(4, dtype=F32), jnp.eye(2, dtype=F32)
    bb = jnp.stack([bbr, bbi], axis=0).reshape(2, nb, 2, 4, S5_STATE, S5_GROUP)
    bmat = jnp.einsum("rbhgpc,kg,jh->bhjkcrgp", bb, eye4, eye2).reshape(nb, 2, 128, 2 * S5_HALF)
    cc = jnp.stack([c_re, -c_im], axis=0).reshape(2, nb, 2, 4, S5_GROUP, S5_STATE)
    cmat = jnp.einsum("rbhgcp,kg,jh->bhrgpjkc", cc, eye4, eye2).reshape(nb, 2, 2 * S5_HALF, 128)
    return lam, bmat, cmat


def _time_to_scan_order(a):
    s, d = a.shape
    return a.reshape(SCAN_CHUNKS, s // SCAN_CHUNKS, d).transpose(1, 0, 2).reshape(s, d)


def _scan_to_time_order(a):
    s, d = a.shape
    return a.reshape(s // SCAN_CHUNKS, SCAN_CHUNKS, d).transpose(1, 0, 2).reshape(s, d)


def _scan_chunks(xr_ref, xi_ref, lr, li, nst, reverse, after_step=None, state=None):
    lanes = lr.shape[1]
    lr8, li8 = jnp.broadcast_to(lr, (SUBLANES, lanes)), jnp.broadcast_to(li, (SUBLANES, lanes))
    zero8 = jnp.zeros((SUBLANES, lanes), F32)

    def rows_of(n):
        s = (nst - 1 - n) if reverse else n
        return s, pl.ds(pl.multiple_of(s * SUBLANES, SUBLANES), SUBLANES)

    def local(n, carry):
        pr, pi = carry
        _, rows = rows_of(n)
        nr = lr8 * pr - li8 * pi + xr_ref[rows, :]
        ni = lr8 * pi + li8 * pr + xi_ref[rows, :]
        xr_ref[rows, :] = nr
        xi_ref[rows, :] = ni
        return nr, ni

    er, ei = lax.fori_loop(0, nst, local, (zero8, zero8))
    pr, pi = lr, li
    for _ in range(int(math.log2(nst))):
        pr, pi = pr * pr - pi * pi, 2.0 * pr * pi
    tr = ti = jnp.zeros((1, lanes), F32)
    ent_r, ent_i = [None] * SCAN_CHUNKS, [None] * SCAN_CHUNKS
    for k in (reversed(range(SCAN_CHUNKS)) if reverse else range(SCAN_CHUNKS)):
        ent_r[k], ent_i[k] = tr, ti
        tr, ti = er[k:k + 1] + (pr * tr - pi * ti), ei[k:k + 1] + (pr * ti + pi * tr)
    in_r, in_i = jnp.concatenate(ent_r, axis=0), jnp.concatenate(ent_i, axis=0)

    def fix(n, carry):
        wr, wi, st = carry
        s, rows = rows_of(n)
        nr = xr_ref[rows, :] + (wr * in_r - wi * in_i)
        ni = xi_ref[rows, :] + (wr * in_i + wi * in_r)
        xr_ref[rows, :] = nr
        xi_ref[rows, :] = ni
        if after_step is not None:
            st = after_step(s, nr, ni, st)
        return wr * lr8 - wi * li8, wr * li8 + wi * lr8, st

    _, _, state = lax.fori_loop(0, nst, fix, (lr8, li8, state))
    return in_r, in_i, state


def _s5_fill_states(u_ref, bm, xr_ref, xi_ref, s):
    rc = min(S5_ROWS, s)

    def fill(r, _):
        rows = pl.ds(pl.multiple_of(r * rc, rc), rc)
        bu = jnp.dot(u_ref[rows, :].astype(MXU_DTYPE), bm, preferred_element_type=F32)
        xr_ref[rows, :] = bu[:, :S5_HALF]
        xi_ref[rows, :] = bu[:, S5_HALF:]
        return 0
    lax.fori_loop(0, s // rc, fill, 0)


def _s5_specs():
    return [pl.BlockSpec((1, 2, 2, S5_HALF), lambda b: (b, 0, 0, 0)),
            pl.BlockSpec((1, 2, 128, 2 * S5_HALF), lambda b: (b, 0, 0, 0)),
            pl.BlockSpec((1, 2, 2 * S5_HALF, 128), lambda b: (b, 0, 0, 0)),
            pl.BlockSpec((1, LANES), lambda b: (0, b))]


def _s5_scan_fwd(u, lam, bmat, cmat, dvec):
    s = u.shape[0]
    nst = s // SCAN_CHUNKS
    rc = min(S5_ROWS, s)

    def body(u_ref, lam_ref, b_ref, c_ref, d_ref, y_ref, xr_ref, xi_ref):
        y_ref[...] = u_ref[...] * d_ref[...]
        for hb in range(2):
            _s5_fill_states(u_ref, b_ref[0, hb], xr_ref, xi_ref, s)
            _scan_chunks(xr_ref, xi_ref, lam_ref[0, hb, 0:1, :], lam_ref[0, hb, 1:2, :], nst, False)
            cm = c_ref[0, hb]

            def emit(r, _, cm=cm):
                rows = pl.ds(pl.multiple_of(r * rc, rc), rc)
                y_ref[rows, :] += (jnp.dot(xr_ref[rows, :].astype(MXU_DTYPE), cm[:S5_HALF], preferred_element_type=F32)
                                   + jnp.dot(xi_ref[rows, :].astype(MXU_DTYPE), cm[S5_HALF:], preferred_element_type=F32))
                return 0
            lax.fori_loop(0, s // rc, emit, 0)

    blk = pl.BlockSpec((s, LANES), lambda b: (0, b))
    return pl.pallas_call(
        body, name="s5_scan_fwd", grid=(S5_BLOCKS,), in_specs=[blk] + _s5_specs(), out_specs=blk,
        out_shape=jax.ShapeDtypeStruct(u.shape, F32),
        scratch_shapes=[pltpu.VMEM((s, S5_HALF), F32)] * 2,
        compiler_params=_params(("parallel",)))(u, lam, bmat, cmat, dvec)


def _s5_scan_bwd(u, dy, lam, bmat, cmat, dvec):
    s = u.shape[0]
    nst = s // SCAN_CHUNKS
    rc = min(S5_ROWS, s)
    nt = (((1,), (1,)), ((), ()))
    tn = (((0,), (0,)), ((), ()))

    def body(u_ref, dy_ref, lam_ref, b_ref, c_ref, d_ref, du_ref, db_ref, dc_ref, dl_ref, dd_ref,
             xr_ref, xi_ref, gr_ref, gi_ref):
        du_ref[...] = dy_ref[...] * d_ref[...]
        dd_ref[...] = jnp.sum(dy_ref[...] * u_ref[...], axis=0, keepdims=True)
        db_ref[...] = jnp.zeros_like(db_ref)
        dc_ref[...] = jnp.zeros_like(dc_ref)
        for hb in range(2):
            bm, cm = b_ref[0, hb], c_ref[0, hb]
            lr, li = lam_ref[0, hb, 0:1, :], lam_ref[0, hb, 1:2, :]
            _s5_fill_states(u_ref, bm, xr_ref, xi_ref, s)
            xin_r, xin_i, _ = _scan_chunks(xr_ref, xi_ref, lr, li, nst, False)

            def fill_g(r, _, cm=cm):
                rows = pl.ds(pl.multiple_of(r * rc, rc), rc)
                g = lax.dot_general(dy_ref[rows, :].astype(MXU_DTYPE), cm, nt, preferred_element_type=F32)
                gr_ref[rows, :] = g[:, :S5_HALF]
                gi_ref[rows, :] = g[:, S5_HALF:]
                return 0
            lax.fori_loop(0, s // rc, fill_g, 0)
            def lam_grad(st, g_r, g_i, acc, xin_r=xin_r, xin_i=xin_i):
                prev = pl.ds(pl.multiple_of(jnp.maximum(st - 1, 0) * SUBLANES, SUBLANES), SUBLANES)
                x_r = jnp.where(st > 0, xr_ref[prev, :], xin_r)
                x_i = jnp.where(st > 0, xi_ref[prev, :], xin_i)
                return acc[0] + (g_r * x_r + g_i * x_i), acc[1] + (g_i * x_r - g_r * x_i)

            zero8 = jnp.zeros((SUBLANES, S5_HALF), F32)
            _, _, (a_r, a_i) = _scan_chunks(gr_ref, gi_ref, lr, -li, nst, True, after_step=lam_grad, state=(zero8, zero8))
            dl_ref[0, hb] = jnp.concatenate([jnp.sum(a_r, axis=0, keepdims=True),
                                             jnp.sum(a_i, axis=0, keepdims=True)], axis=0)

            def emit(r, _, bm=bm, hb=hb):
                rows = pl.ds(pl.multiple_of(r * rc, rc), rc)
                g = jnp.concatenate([gr_ref[rows, :], gi_ref[rows, :]], axis=1).astype(MXU_DTYPE)
                x = jnp.concatenate([xr_ref[rows, :], xi_ref[rows, :]], axis=1).astype(MXU_DTYPE)
                du_ref[rows, :] += lax.dot_general(g, bm, nt, preferred_element_type=F32)
                db_ref[0, hb] += lax.dot_general(u_ref[rows, :].astype(MXU_DTYPE), g, tn, preferred_element_type=F32)
                dc_ref[0, hb] += lax.dot_general(dy_ref[rows, :].astype(MXU_DTYPE), x, tn, preferred_element_type=F32)
                return 0
            lax.fori_loop(0, s // rc, emit, 0)

    blk = pl.BlockSpec((s, LANES), lambda b: (0, b))
    mat = pl.BlockSpec((1, 2, 128, 2 * S5_HALF), lambda b: (b, 0, 0, 0))
    return pl.pallas_call(
        body, name="s5_scan_bwd", grid=(S5_BLOCKS,), in_specs=[blk, blk] + _s5_specs(),
        out_specs=[blk, mat, mat, pl.BlockSpec((1, 2, 2, S5_HALF), lambda b: (b, 0, 0, 0)),
                   pl.BlockSpec((1, LANES), lambda b: (0, b))],
        out_shape=[jax.ShapeDtypeStruct(u.shape, F32),
                   jax.ShapeDtypeStruct((S5_BLOCKS, 2, 128, 2 * S5_HALF), F32),
                   jax.ShapeDtypeStruct((S5_BLOCKS, 2, 128, 2 * S5_HALF), F32),
                   jax.ShapeDtypeStruct((S5_BLOCKS, 2, 2, S5_HALF), F32),
                   jax.ShapeDtypeStruct((1, D_MODEL), F32)],
        scratch_shapes=[pltpu.VMEM((s, S5_HALF), F32)] * 4,
        compiler_params=_params(("parallel",)))(u, dy, lam, bmat, cmat, dvec)


_GELU_C = math.sqrt(2.0 / math.pi)


def _gelu_parts(y):
    inner = _GELU_C * (y + 0.044715 * y * y * y)
    th = jnp.tanh(inner)
    return 0.5 * y * (1.0 + th), th


def _s5_fwd(x, gain, w_in, ssm, dvec, w_glu):
    lam, bmat, cmat = ssm
    h = _rms_fwd(x, gain, "mix_norm")
    u = _mm(h, w_in, name="s5_in")
    y = _scan_to_time_order(_s5_scan_fwd(_time_to_scan_order(u), lam, bmat.astype(MXU_DTYPE), cmat.astype(MXU_DTYPE), dvec))
    g = _ew(lambda yb: ((_gelu_parts(yb)[0],), ()), [y], outs=[(D_MODEL, BF16)], name="s5_gelu")[0]
    vg = _mm(g, w_glu, name="s5_glu", out_dtype=BF16)

    def glu_fn(vb, gb, xb):
        return (xb + vb.astype(F32) * _sigmoid(gb.astype(F32)),), ()
    x1 = _ew(glu_fn, [(vg, D_MODEL, 0), (vg, D_MODEL, 1), x], outs=[(D_MODEL, F32)], name="s5_gate")[0]
    return x1, (x, h, u, y, g, vg)


def _s5_bwd(dx1, saved, gain, w_in, ssm, dvec, w_glu):
    x, h, u, y, g, vg = saved
    lam, bmat, cmat = ssm

    def dglu_fn(db, vb, gb):
        vf, sg = vb.astype(F32), _sigmoid(gb.astype(F32))
        return (jnp.concatenate([db * sg, db * vf * sg * (1.0 - sg)], axis=1),), ()
    dvg = _ew(dglu_fn, [dx1, (vg, D_MODEL, 0), (vg, D_MODEL, 1)], outs=[(2 * D_MODEL, BF16)], name="s5_dgate")[0]
    dw_glu = _mm(g, dvg, ta=True, name="s5_dw_glu", out_dtype=BF16)
    dg = _mm(dvg, w_glu, tb=True, name="s5_dg")

    def dgelu_fn(dgb, yb):
        _, th = _gelu_parts(yb)
        dinner = _GELU_C * (1.0 + 3.0 * 0.044715 * yb * yb)
        return (dgb * (0.5 * (1.0 + th) + 0.5 * yb * (1.0 - th * th) * dinner),), ()
    dy = _ew(dgelu_fn, [dg, y], outs=[(D_MODEL, F32)], name="s5_dgelu")[0]
    du_s, dbm, dct, dlam, ddvec = _s5_scan_bwd(_time_to_scan_order(u), _time_to_scan_order(dy), lam,
                                               bmat.astype(MXU_DTYPE), cmat.astype(MXU_DTYPE), dvec)
    du = _scan_to_time_order(du_s)
    dw_in = _mm(h, du, ta=True, name="s5_dw_in", out_dtype=BF16)
    dx, dgain = _mm_rms_bwd(du, w_in, x, gain, dx1, name="s5_dh")
    return dx, dgain, dw_in, (dlam, dbm, jnp.swapaxes(dct, 2, 3)), ddvec, dw_glu


POOL_BLOCK = 256
N_POOL_GROUPS = len(POOL_WINDOWS)


def _pool_bands(gi, i, t):
    w = jnp.left_shift(2, gi)
    r = lax.broadcasted_iota(jnp.int32, (t, t), 0)
    c = lax.broadcasted_iota(jnp.int32, (t, t), 1)
    inside = ((c <= r) & (c > r - w)).astype(MXU_DTYPE)
    before = (c > r - w + t).astype(MXU_DTYPE)

    def inv_count(block):
        pos = block * t + lax.broadcasted_iota(jnp.int32, (t, 1), 0)
        return 1.0 / jnp.minimum(pos + 1, w).astype(F32)
    return inside, before, inv_count


def _pool_fwd(x, gain, w_grp, b_grp, scale):
    s = x.shape[0]
    t = min(POOL_BLOCK, s)
    h = _rms_fwd(x, gain, "mix_norm")

    def body(h_ref, hp_ref, w_ref, b_ref, sc_ref, x_ref, x1_ref, diff_ref):
        gi, i = pl.program_id(0), pl.program_id(1)
        inside, before, inv_count = _pool_bands(gi, i, t)
        hc = h_ref[...]
        tot = jnp.dot(inside, hc.astype(MXU_DTYPE), preferred_element_type=F32)
        prev = jnp.dot(before, hp_ref[...].astype(MXU_DTYPE), preferred_element_type=F32)
        tot = tot + jnp.where(i > 0, prev, 0.0)
        diff = (tot * inv_count(i) - hc.astype(F32)).astype(diff_ref.dtype)
        y = (jnp.dot(diff.astype(MXU_DTYPE), w_ref[0], preferred_element_type=F32) + b_ref[...]) * sc_ref[...]
        diff_ref[...] = diff
        x1_ref[...] = x_ref[...] + y

    blk = pl.BlockSpec((t, POOL_WIDTH), lambda gi, i: (i, gi))
    vec = pl.BlockSpec((1, POOL_WIDTH), lambda gi, i: (0, gi))
    x1, diff = pl.pallas_call(
        body, name="pool_fwd", grid=(N_POOL_GROUPS, s // t),
        in_specs=[blk, pl.BlockSpec((t, POOL_WIDTH), lambda gi, i: (jnp.maximum(i - 1, 0), gi)),
                  pl.BlockSpec((1, POOL_WIDTH, POOL_WIDTH), lambda gi, i: (gi, 0, 0)), vec, vec, blk],
        out_specs=[blk, blk],
        out_shape=[jax.ShapeDtypeStruct(x.shape, F32), jax.ShapeDtypeStruct(x.shape, BF16)],
        compiler_params=_params(("parallel", "arbitrary")))(h, h, w_grp, b_grp, scale, x)
    return x1, (x, diff)


def _pool_bwd(dx1, saved, gain, w_grp, b_grp, scale):
    x, diff = saved
    s = x.shape[0]
    t = min(POOL_BLOCK, s)
    nb = s // t

    def body1(dx_ref, diff_ref, w_ref, b_ref, sc_ref, dd_ref, dw_ref, db_ref, dsc_ref):
        i = pl.program_id(1)

        @pl.when(i == 0)
        def _():
            dw_ref[...] = jnp.zeros_like(dw_ref)
            db_ref[...] = jnp.zeros_like(db_ref)
            dsc_ref[...] = jnp.zeros_like(dsc_ref)

        dfb = diff_ref[...].astype(MXU_DTYPE)
        ypre = jnp.dot(dfb, w_ref[0], preferred_element_type=F32) + b_ref[...]
        dxb = dx_ref[...]
        dy = dxb * sc_ref[...]
        dsc_ref[...] += jnp.sum(dxb * ypre, axis=0, keepdims=True)
        db_ref[...] += jnp.sum(dy, axis=0, keepdims=True)
        dyb = dy.astype(MXU_DTYPE)
        dw_ref[0] += lax.dot_general(dfb, dyb, (((0,), (0,)), ((), ())), preferred_element_type=F32)
        dd_ref[...] = lax.dot_general(dyb, w_ref[0], (((1,), (1,)), ((), ())), preferred_element_type=F32)

    blk = pl.BlockSpec((t, POOL_WIDTH), lambda gi, i: (i, gi))
    vec = pl.BlockSpec((1, POOL_WIDTH), lambda gi, i: (0, gi))
    mat = pl.BlockSpec((1, POOL_WIDTH, POOL_WIDTH), lambda gi, i: (gi, 0, 0))
    ddiff, dw, db, dsc = pl.pallas_call(
        body1, name="pool_bwd_map", grid=(N_POOL_GROUPS, nb), in_specs=[blk, blk, mat, vec, vec],
        out_specs=[blk, mat, vec, vec],
        out_shape=[jax.ShapeDtypeStruct(x.shape, F32), jax.ShapeDtypeStruct(w_grp.shape, F32),
                   jax.ShapeDtypeStruct((1, D_MODEL), F32), jax.ShapeDtypeStruct((1, D_MODEL), F32)],
        compiler_params=_params(("parallel", "arbitrary")))(dx1, diff, w_grp, b_grp, scale)

    def body2(dc_ref, dn_ref, dh_ref):
        gi, i = pl.program_id(0), pl.program_id(1)
        inside, before, inv_count = _pool_bands(gi, i, t)
        tn = (((0,), (0,)), ((), ()))
        dc = dc_ref[...]
        tot = lax.dot_general(inside, (dc * inv_count(i)).astype(MXU_DTYPE), tn, preferred_element_type=F32)
        nxt = lax.dot_general(before, (dn_ref[...] * inv_count(i + 1)).astype(MXU_DTYPE), tn, preferred_element_type=F32)
        dh_ref[...] = tot + jnp.where(i < nb - 1, nxt, 0.0) - dc

    dh = pl.pallas_call(
        body2, name="pool_bwd_window", grid=(N_POOL_GROUPS, nb),
        in_specs=[blk, pl.BlockSpec((t, POOL_WIDTH), lambda gi, i: (jnp.minimum(i + 1, nb - 1), gi))],
        out_specs=blk, out_shape=jax.ShapeDtypeStruct(x.shape, F32),
        compiler_params=_params(("parallel", "parallel")))(ddiff, ddiff)
    dx, dgain = _rms_bwd(x, gain, dh, dx1, "mix_norm_bwd")
    return dx, dgain, dw, db, dsc


MESH_ID = pl.DeviceIdType.MESH
ANY = pl.BlockSpec(memory_space=pl.ANY)


def _place():
    x, y, c = lax.axis_index("x"), lax.axis_index("y"), lax.axis_index("c")
    other_chips = [(1 - x, y), (x, 1 - y), (1 - x, 1 - y)]
    return x, y, c, other_chips


def _chip_index(chip):
    return 2 * chip[0] + chip[1]


def _remote(src, dst, send_sems, recv_sems, n, to):
    return pltpu.make_async_remote_copy(src_ref=src, dst_ref=dst, send_sem=send_sems.at[n], recv_sem=recv_sems.at[n],
                                        device_id=to, device_id_type=MESH_ID)


def _gather_weights(ws):
    n = len(ws)
    from_x, relay_x, from_y, relay_y, sib_x, sib_y, sib_d0, sib_d1, sib_own = range(9)
    slots = 9

    def body(*refs):
        w_refs, out_refs = refs[:n], refs[n:2 * n]
        send_sems, recv_sems = refs[2 * n:]
        x, y, c, (xn, yn, dn) = _place()
        k = _chip_index((x, y))
        me, sibling = (x, y, c), (x, y, 1 - c)

        def copy(ref, t, slot, to):
            return _remote(ref, ref, send_sems, recv_sems, slots * t + slot, to)

        def quarter(ref, q):
            rows = ref.shape[0] // 2
            return ref.at[pl.ds(q * rows, rows)]

        started = []

        def start(cp):
            cp.start()
            started.append(cp)

        for t in range(n):
            for slot, chip in ((from_x, xn), (from_y, yn)):
                start(_remote(w_refs[t].at[c], out_refs[t].at[k, c], send_sems, recv_sems, slots * t + slot, (*chip, c)))
            start(_remote(w_refs[t], out_refs[t].at[k], send_sems, recv_sems, slots * t + sib_own, sibling))
        for t in range(n):
            got = out_refs[t].at[_chip_index(xn), c]
            copy(got, t, from_x, me).wait_recv()
            start(copy(quarter(got, 0), t, relay_y, (*yn, c)))
            start(copy(got, t, sib_x, sibling))
        for t in range(n):
            got = out_refs[t].at[_chip_index(yn), c]
            copy(got, t, from_y, me).wait_recv()
            start(copy(quarter(got, 1), t, relay_x, (*xn, c)))
            start(copy(got, t, sib_y, sibling))
        for t in range(n):
            got = out_refs[t].at[_chip_index(dn), c]
            for q, slot, sib_slot in ((0, relay_y, sib_d0), (1, relay_x, sib_d1)):
                copy(quarter(got, q), t, slot, me).wait_recv()
                start(copy(quarter(got, q), t, sib_slot, sibling))
        for t in range(n):
            for chip, slot in ((xn, sib_x), (yn, sib_y)):
                copy(out_refs[t].at[_chip_index(chip), 1 - c], t, slot, me).wait_recv()
            theirs = out_refs[t].at[_chip_index(dn), 1 - c]
            copy(quarter(theirs, 0), t, sib_d0, me).wait_recv()
            copy(quarter(theirs, 1), t, sib_d1, me).wait_recv()
            copy(out_refs[t].at[k], t, sib_own, me).wait_recv()
        for cp in started:
            cp.wait_send()

    return pl.pallas_call(
        body, name="gather_weights", in_specs=[ANY] * n, out_specs=[ANY] * n,
        out_shape=[jax.ShapeDtypeStruct((N_CHIPS,) + w.shape, w.dtype) for w in ws],
        scratch_shapes=[pltpu.SemaphoreType.DMA((slots * n,)), pltpu.SemaphoreType.DMA((slots * n,))],
    )(*ws)


def _swap_halves(gs):
    n = len(gs)

    def body(*refs):
        g_refs, out_refs = refs[:n], refs[n:2 * n]
        send_sems, recv_sems = refs[2 * n:]
        x, y, c, _ = _place()
        copies = [_remote(g_refs[t].at[s, 1 - c], out_refs[t].at[s], send_sems, recv_sems, N_CHIPS * t + s, (x, y, 1 - c))
                  for t in range(n) for s in range(N_CHIPS)]
        for cp in copies:
            cp.start()
        for cp in copies:
            cp.wait_recv()
        for cp in copies:
            cp.wait_send()

    return pl.pallas_call(
        body, name="swap_halves", in_specs=[ANY] * n, out_specs=[ANY] * n,
        out_shape=[jax.ShapeDtypeStruct((N_CHIPS,) + g.shape[2:], g.dtype) for g in gs],
        scratch_shapes=[pltpu.SemaphoreType.DMA((N_CHIPS * n,)), pltpu.SemaphoreType.DMA((N_CHIPS * n,))],
    )(*gs)


def _scatter_partials(ps):
    n = len(ps)

    def body(*refs):
        p_refs, out_refs = refs[:n], refs[n:2 * n]
        send_sems, recv_sems = refs[2 * n:]
        x, y, c, chips = _place()
        sends = [_remote(p_refs[t].at[_chip_index(chip)], out_refs[t].at[j], send_sems, recv_sems, 3 * t + j, (*chip, c))
                 for j, chip in enumerate(chips) for t in range(n)]
        for cp in sends:
            cp.start()
        for j in range(3):
            for t in range(n):
                slot = out_refs[t].at[j]
                _remote(slot, slot, send_sems, recv_sems, 3 * t + j, (x, y, c)).wait_recv()
        for cp in sends:
            cp.wait_send()

    return pl.pallas_call(
        body, name="scatter_partials", in_specs=[ANY] * n, out_specs=[ANY] * n,
        out_shape=[jax.ShapeDtypeStruct((3,) + p.shape[1:], p.dtype) for p in ps],
        scratch_shapes=[pltpu.SemaphoreType.DMA((3 * n,)), pltpu.SemaphoreType.DMA((3 * n,))],
    )(*ps)


def _share_half(fs):
    n = len(fs)

    def body(*refs):
        f_refs, out_refs = refs[:n], refs[n:2 * n]
        send_sems, recv_sems = refs[2 * n:]
        x, y, c, _ = _place()
        sends = [_remote(f_refs[t], out_refs[t].at[c], send_sems, recv_sems, t, (x, y, 1 - c)) for t in range(n)]
        for cp in sends:
            cp.start()
        for t in range(n):
            theirs = out_refs[t].at[1 - c]
            _remote(theirs, theirs, send_sems, recv_sems, t, (x, y, c)).wait_recv()
        for cp in sends:
            cp.wait_send()

    return pl.pallas_call(
        body, name="share_half", in_specs=[ANY] * n, out_specs=[ANY] * n,
        out_shape=[jax.ShapeDtypeStruct((2,) + f.shape, f.dtype) for f in fs],
        scratch_shapes=[pltpu.SemaphoreType.DMA((n,)), pltpu.SemaphoreType.DMA((n,))],
    )(*fs)


def _gather_small(v):
    def body(v_ref, out_ref, send_sems, recv_sems):
        x, y, c, chips = _place()
        me, sibling = (x, y, c), (x, y, 1 - c)

        def slot(px, py, pc):
            return out_ref.at[4 * px + 2 * py + pc]

        first = [_remote(v_ref, slot(*me), send_sems, recv_sems, 0, sibling)]
        first += [_remote(v_ref, slot(*me), send_sems, recv_sems, 1 + j, (*chip, c)) for j, chip in enumerate(chips)]
        for cp in first:
            cp.start()
        passed = [_remote(slot(*chip, c), slot(*chip, c), send_sems, recv_sems, 4 + j, sibling)
                  for j, chip in enumerate(chips)]
        for j, chip in enumerate(chips):
            _remote(slot(*chip, c), slot(*chip, c), send_sems, recv_sems, 1 + j, me).wait_recv()
            passed[j].start()
        _remote(slot(*sibling), slot(*sibling), send_sems, recv_sems, 0, me).wait_recv()
        for j, chip in enumerate(chips):
            _remote(slot(*chip, 1 - c), slot(*chip, 1 - c), send_sems, recv_sems, 4 + j, me).wait_recv()
        for cp in first + passed:
            cp.wait_send()

    gathered = pl.pallas_call(
        body, name="gather_small", in_specs=[ANY], out_specs=ANY,
        out_shape=jax.ShapeDtypeStruct((N_DEV,) + v.shape, v.dtype),
        scratch_shapes=[pltpu.SemaphoreType.DMA((7,)), pltpu.SemaphoreType.DMA((7,))],
    )(v)
    device = 4 * lax.axis_index("x") + 2 * lax.axis_index("y") + lax.axis_index("c")
    return lax.dynamic_update_index_in_dim(gathered, v, device, 0)


SMALL_ROWS = 256
SUM_ROWS = 256
BF16_ROWS = 16


def _row_tile(rows, want):
    for t in range(min(rows, want) // BF16_ROWS * BF16_ROWS, 0, -BF16_ROWS):
        if rows % t == 0:
            return t
    return rows


def _pair_sum(g, r, core, name):
    rows, cols = g.shape[2:]
    tr = _row_tile(rows, SUM_ROWS)

    def body(c_ref, g_ref, r_ref, o_ref):
        o_ref[0] = (g_ref[0, 0].astype(F32) + r_ref[0].astype(F32)).astype(o_ref.dtype)

    return pl.pallas_call(
        body, name=name,
        grid_spec=pltpu.PrefetchScalarGridSpec(
            num_scalar_prefetch=1, grid=(N_CHIPS, rows // tr),
            in_specs=[pl.BlockSpec((1, 1, tr, cols), lambda s, i, c_ref: (s, c_ref[0], i, 0)),
                      pl.BlockSpec((1, tr, cols), lambda s, i, c_ref: (s, i, 0))],
            out_specs=pl.BlockSpec((1, tr, cols), lambda s, i, c_ref: (s, i, 0))),
        out_shape=jax.ShapeDtypeStruct(r.shape, BF16),
        compiler_params=_params(("parallel", "parallel")))(core, g, r)


def _chip_sum(p, recv, chip, name):
    rows, cols = p.shape[1:]
    tr = _row_tile(rows, SUM_ROWS)

    def body(k_ref, p_ref, r_ref, o_ref):
        acc = p_ref[0].astype(F32)
        for j in range(3):
            acc = acc + r_ref[j].astype(F32)
        o_ref[...] = acc

    return pl.pallas_call(
        body, name=name,
        grid_spec=pltpu.PrefetchScalarGridSpec(
            num_scalar_prefetch=1, grid=(rows // tr,),
            in_specs=[pl.BlockSpec((1, tr, cols), lambda i, k_ref: (k_ref[0], i, 0)),
                      pl.BlockSpec((3, tr, cols), lambda i, k_ref: (0, i, 0))],
            out_specs=pl.BlockSpec((tr, cols), lambda i, k_ref: (i, 0))),
        out_shape=jax.ShapeDtypeStruct((rows, cols), F32),
        compiler_params=_params(("parallel",)))(chip, p, recv)


def _sum_blocks(a, name):
    n, rows, cols = a.shape
    tr = _row_tile(rows, SUM_ROWS)

    def body(a_ref, o_ref):
        acc = a_ref[0].astype(F32)
        for s in range(1, n):
            acc = acc + a_ref[s].astype(F32)
        o_ref[...] = acc

    return pl.pallas_call(
        body, name=name, grid=(rows // tr,),
        in_specs=[pl.BlockSpec((n, tr, cols), lambda i: (0, i, 0))],
        out_specs=pl.BlockSpec((tr, cols), lambda i: (i, 0)),
        out_shape=jax.ShapeDtypeStruct((rows, cols), F32),
        compiler_params=_params(("parallel",)))(a)


def _adamw(w, g, m, v, name):
    def fn(wb, gb, mb, vb):
        m2 = ADAM_B1 * mb + (1.0 - ADAM_B1) * gb
        v2 = ADAM_B2 * vb + (1.0 - ADAM_B2) * (gb * gb)
        m_hat = m2 / (1.0 - ADAM_B1 ** ADAM_STEP)
        v_hat = v2 / (1.0 - ADAM_B2 ** ADAM_STEP)
        delta = -ADAM_LR * (m_hat / (jnp.sqrt(v_hat) + ADAM_EPS) + ADAM_WD * wb)
        return (delta, m2, v2), ()
    c = w.shape[1]
    return _ew(fn, [w, g, m, v], outs=[(c, F32)] * 3, name=name)


WEIGHTS = ["mix_norm_g", "ffn_norm_g", "final_norm_g", "fox_w_in", "fox_b_f", "fox_w_out", "s5_w_in", "s5_a_re",
           "s5_a_im", "s5_log_dt", "s5_b_re", "s5_b_im", "s5_c_re", "s5_c_im", "s5_d", "s5_w_glu", "pool_w",
           "pool_b", "pool_scale", "ffn_w_gate_up", "ffn_w_down"]
BIG = {"fox_w_in": 2, "fox_w_out": 1, "s5_w_in": 1, "s5_w_glu": 2, "pool_w": 2, "ffn_w_gate_up": 2, "ffn_w_down": 1}
SLICED = ("pool_b", "pool_scale")
SMALL = [n for n in WEIGHTS if n not in BIG]


def _to_natural(cm, axis):
    moved = jnp.moveaxis(cm, 0, axis)
    shape = moved.shape[:axis] + (moved.shape[axis] * moved.shape[axis + 1],) + moved.shape[axis + 2:]
    return moved.reshape(shape)


def _to_chip_major(nat, axis):
    shape = nat.shape[:axis] + (N_CHIPS, nat.shape[axis] // N_CHIPS) + nat.shape[axis + 1:]
    return jnp.moveaxis(nat.reshape(shape), axis, 0)


def _halves_view(shape):
    return (2, int(np.prod(shape[:-1])) // 2, shape[-1])


def _pack_small(parts):
    flat = jnp.concatenate([p.reshape(-1).astype(F32) for p in parts])
    pad = (-flat.shape[0]) % (SMALL_ROWS * LANES)
    return jnp.pad(flat, (0, pad)).reshape(-1, LANES)


def _unpack_small(buf, shapes):
    flat = buf.reshape(-1)
    out, off = [], 0
    for shp in shapes:
        n = int(np.prod(shp))
        out.append(flat[off:off + n].reshape(shp))
        off += n
    return out


def _local_step(x, target, w):
    grads = {}
    mixers = ("fox", "s5", "pool")
    saved = []
    ssm, ssm_pull = jax.vjp(_s5_operands, w["s5_a_re"][0], w["s5_a_im"][0], w["s5_log_dt"][0], w["s5_b_re"][0],
                            w["s5_b_im"][0], w["s5_c_re"][0], w["s5_c_im"][0])
    fox_w = []
    for j in range(w["fox_w_in"].shape[0]):
        w_in = w["fox_w_in"][j]
        w_f = jnp.pad(w_in[:, 3 * D_MODEL:], ((0, 0), (0, LANES - FOX_HEADS)))
        fox_w.append((w_in[:, :3 * D_MODEL], w_f, w["fox_w_out"][j]))
    for i in range(DEPTH):
        kind, j = mixers[i % 3], i // 3
        gain = w["mix_norm_g"][i]
        if kind == "fox":
            x1, sv = _fox_fwd(x, gain, fox_w[j][0], fox_w[j][1], w["fox_b_f"][j], fox_w[j][2])
        elif kind == "s5":
            x1, sv = _s5_fwd(x, gain, w["s5_w_in"][j], ssm, w["s5_d"], w["s5_w_glu"][j])
        else:
            x1, sv = _pool_fwd(x, gain, w["pool_w"][j], w["pool_b"], w["pool_scale"])
        x, sf = _ffn_fwd(x1, w["ffn_norm_g"][i], w["ffn_w_gate_up"][i], w["ffn_w_down"][i])
        saved.append((sv, sf))
    loss, dx, grads["final_norm_g"] = _loss_head(x, w["final_norm_g"], target)
    per_layer = {n: [None] * DEPTH for n in ("mix_norm_g", "ffn_norm_g", "ffn_w_gate_up", "ffn_w_down")}
    fox_g = {n: [None] * len(fox_w) for n in ("fox_w_in", "fox_b_f", "fox_w_out")}
    for i in reversed(range(DEPTH)):
        kind, j = mixers[i % 3], i // 3
        sv, sf = saved[i]
        dx, per_layer["ffn_norm_g"][i], per_layer["ffn_w_gate_up"][i], per_layer["ffn_w_down"][i] = _ffn_bwd(
            dx, sf, w["ffn_norm_g"][i], w["ffn_w_gate_up"][i], w["ffn_w_down"][i])
        gain = w["mix_norm_g"][i]
        if kind == "fox":
            dx, per_layer["mix_norm_g"][i], fox_g["fox_w_in"][j], fox_g["fox_b_f"][j], fox_g["fox_w_out"][j] = _fox_bwd(
                dx, sv, gain, fox_w[j][0], fox_w[j][1], fox_w[j][2])
        elif kind == "s5":
            dx, per_layer["mix_norm_g"][i], dw_in, dssm, dd, dw_glu = _s5_bwd(
                dx, sv, gain, w["s5_w_in"][j], ssm, w["s5_d"], w["s5_w_glu"][j])
            grads["s5_w_in"], grads["s5_w_glu"], grads["s5_d"] = dw_in[None], dw_glu[None], dd
            for n, g in zip(("s5_a_re", "s5_a_im", "s5_log_dt", "s5_b_re", "s5_b_im", "s5_c_re", "s5_c_im"), ssm_pull(dssm)):
                grads[n] = g[None]
        else:
            dx, per_layer["mix_norm_g"][i], dw, db, dsc = _pool_bwd(dx, sv, gain, w["pool_w"][j], w["pool_b"], w["pool_scale"])
            grads["pool_w"], grads["pool_b"], grads["pool_scale"] = dw[None].astype(BF16), db, dsc
    for n, parts in {**per_layer, **fox_g}.items():
        grads[n] = jnp.stack(parts)
    return loss, dx, grads


def kernel(x, mix_norm_g, ffn_norm_g, final_norm_g, fox_w_in, fox_b_f, fox_w_out, s5_w_in, s5_a_re, s5_a_im, s5_log_dt, s5_b_re, s5_b_im, s5_c_re, s5_c_im, s5_d, s5_w_glu, pool_w, pool_b, pool_scale, ffn_w_gate_up, ffn_w_down, loss_target, m_mix_norm_g, m_ffn_norm_g, m_final_norm_g, m_fox_w_in, m_fox_b_f, m_fox_w_out, m_s5_w_in, m_s5_a_re, m_s5_a_im, m_s5_log_dt, m_s5_b_re, m_s5_b_im, m_s5_c_re, m_s5_c_im, m_s5_d, m_s5_w_glu, m_pool_w, m_pool_b, m_pool_scale, m_ffn_w_gate_up, m_ffn_w_down, v_mix_norm_g, v_ffn_norm_g, v_final_norm_g, v_fox_w_in, v_fox_b_f, v_fox_w_out, v_s5_w_in, v_s5_a_re, v_s5_a_im, v_s5_log_dt, v_s5_b_re, v_s5_b_im, v_s5_c_re, v_s5_c_im, v_s5_d, v_s5_w_glu, v_pool_w, v_pool_b, v_pool_scale, v_ffn_w_gate_up, v_ffn_w_down):
    given = dict(locals())
    shard = {n: given[n] for n in WEIGHTS}
    chip = 2 * lax.axis_index("x") + lax.axis_index("y")
    core = lax.axis_index("c")

    views = {n: _halves_view(shard[n].shape) for n in BIG}
    own = [shard[n].astype(MXU_DTYPE).reshape(views[n]) for n in BIG]
    whole = {}
    for n, by_chip in zip(BIG, _gather_weights(own)):
        whole[n] = _to_natural(by_chip.reshape((N_CHIPS,) + shard[n].shape), BIG[n])
    for n in SMALL:
        whole[n] = shard[n]
    sliced_shapes = [shard[n].shape for n in SLICED]
    by_chip = _gather_small(_pack_small([shard[n] for n in SLICED]))[0::2]
    slices = [_unpack_small(by_chip[k], sliced_shapes) for k in range(N_CHIPS)]
    for idx, n in enumerate(SLICED):
        whole[n] = jnp.concatenate([slices[k][idx] for k in range(N_CHIPS)], axis=-1)

    loss_part, dx, grads = _local_step(x[0], loss_target[0], whole)
    loss = lax.psum(loss_part, MESH_AXES)

    gs = [_to_chip_major(grads[n].astype(BF16), BIG[n]).reshape((N_CHIPS,) + views[n]) for n in BIG]
    core_id, chip_id = core.reshape(1).astype(jnp.int32), chip.reshape(1).astype(jnp.int32)
    partial = [_pair_sum(g, r, core_id, "pair_sum_" + n) for n, g, r in zip(BIG, gs, _swap_halves(gs))]
    half = [_chip_sum(p, r, chip_id, "chip_sum_" + n) for n, p, r in zip(BIG, partial, _scatter_partials(partial))]
    grad = {n: lax.dynamic_update_index_in_dim(both, mine, core, 0).reshape(shard[n].shape)
            for n, mine, both in zip(BIG, half, _share_half(half))}

    small_sum = _sum_blocks(_gather_small(_pack_small([grads[n] for n in SMALL])), "small_sum")
    for n, g in zip(SMALL, _unpack_small(small_sum, [whole[n].shape for n in SMALL])):
        grad[n] = g
    for n in SLICED:
        width = shard[n].shape[-1]
        grad[n] = lax.dynamic_slice_in_dim(grad[n], chip * width, width, axis=-1)

    delta, new_m, new_v = {}, {}, {}
    for n in BIG:
        view = (-1, shard[n].shape[-1])
        res = _adamw(shard[n].reshape(view), grad[n].reshape(view), given["m_" + n].reshape(view),
                     given["v_" + n].reshape(view), "adamw_" + n)
        delta[n], new_m[n], new_v[n] = (r.reshape(shard[n].shape) for r in res)
    small_shapes = [shard[n].shape for n in SMALL]
    res = _adamw(_pack_small([shard[n] for n in SMALL]), _pack_small([grad[n] for n in SMALL]),
                 _pack_small([given["m_" + n] for n in SMALL]), _pack_small([given["v_" + n] for n in SMALL]), "adamw_small")
    for out, buf in zip((delta, new_m, new_v), res):
        for n, a in zip(SMALL, _unpack_small(buf, small_shapes)):
            out[n] = a
    return (loss, dx[None], *[grad[n] for n in WEIGHTS], *[delta[n] for n in WEIGHTS],
            *[new_m[n] for n in WEIGHTS], *[new_v[n] for n in WEIGHTS])
```

```python
import functools
import math

import jax
import jax.numpy as jnp
import numpy as np
from jax import lax
from jax.experimental import pallas as pl
from jax.experimental.pallas import tpu as pltpu

F32 = jnp.float32
BF16 = jnp.bfloat16
MXU_DTYPE = jnp.bfloat16

D_MODEL = 1024
DEPTH = 4
EPS = 1e-6
FOX_HEADS = 16
FOX_HEAD_DIM = 64
HEAD_PAIRS = FOX_HEADS // 2
S5_GROUPS = 64
S5_GROUP = 16
S5_STATE = 64
S5_BLOCKS = 8
S5_HALF = 256
POOL_WINDOWS = (2, 4, 8, 16)
POOL_WIDTH = 256
D_FF = 2816
N_CHIPS = 4
N_DEV = 8
LANES = 128
SUBLANES = 8
VMEM_LIMIT = 56 * 1024 * 1024

ADAM_LR = 0.001
ADAM_B1 = 0.9
ADAM_B2 = 0.999
ADAM_EPS = 1e-08
ADAM_WD = 0.01
ADAM_STEP = 10

MESH_AXES = ("x", "y", "c")


def _tile(n, want):
    t = (min(n, want) // LANES) * LANES
    while t >= LANES:
        if n % t == 0:
            return t
        t -= LANES
    return n


def _params(sem=None):
    return pltpu.CompilerParams(dimension_semantics=sem, vmem_limit_bytes=VMEM_LIMIT)


def _mm(a, b, *, name, ta=False, tb=False, out_dtype=F32, add=None, tm=1024, tn=1024, tk=1024,
        b_tiles=None, out_tiles=None, into=None):
    m, k = (a.shape[1], a.shape[0]) if ta else a.shape
    if b_tiles is None:
        n = b.shape[0] if tb else b.shape[1]
        assert (b.shape[1] if tb else b.shape[0]) == k, (a.shape, b.shape, ta, tb)
    else:
        assert b_tiles[0] == k, (a.shape, b_tiles[0])
        n = b_tiles[1]
    tm, tn, tk = _tile(m, tm), _tile(n, tn), _tile(k, tk)
    nk = k // tk
    a_spec = pl.BlockSpec((tk, tm), lambda i, j, kk: (kk, i)) if ta else pl.BlockSpec((tm, tk), lambda i, j, kk: (i, kk))
    b_shape = (tn, tk) if tb else (tk, tn)
    if b_tiles is None:
        b_spec = pl.BlockSpec(b_shape, (lambda i, j, kk: (j, kk)) if tb else (lambda i, j, kk: (kk, j)))
    else:
        b_spec = pl.BlockSpec(b_tiles[2], b_tiles[3])
    add_spec = pl.BlockSpec((tm, tn), lambda i, j, kk: (i, j))
    if out_tiles is None:
        o_spec, o_struct = add_spec, jax.ShapeDtypeStruct((m, n), out_dtype)
    else:
        o_spec, o_struct = pl.BlockSpec(out_tiles[1], out_tiles[2]), jax.ShapeDtypeStruct(out_tiles[0], out_dtype)
    dims = (((0 if ta else 1,), (1 if tb else 0,)), ((), ()))
    has_add, has_into = add is not None, into is not None

    def body(*refs):
        a_ref, b_ref = refs[:2]
        add_ref = refs[2] if has_add else None
        o_ref, acc_ref = refs[-2:]
        kk = pl.program_id(2)

        @pl.when(kk == 0)
        def _():
            acc_ref[...] = jnp.zeros_like(acc_ref)

        acc_ref[...] += lax.dot_general(a_ref[...].astype(MXU_DTYPE), b_ref[...].reshape(b_shape).astype(MXU_DTYPE), dims,
                                        preferred_element_type=F32)

        @pl.when(kk == nk - 1)
        def _():
            r = acc_ref[...]
            if has_add:
                r = r + add_ref[...].astype(F32)
            o_ref[...] = r.astype(out_dtype).reshape(o_ref.shape)

    ins = [a, b] + ([add] if has_add else []) + ([into] if has_into else [])
    specs = [a_spec, b_spec] + ([add_spec] if has_add else []) + ([pl.BlockSpec(memory_space=pl.ANY)] if has_into else [])
    return pl.pallas_call(
        body, name=name, grid=(m // tm, n // tn, nk), in_specs=specs, out_specs=o_spec,
        out_shape=o_struct, scratch_shapes=[pltpu.VMEM((tm, tn), F32)],
        input_output_aliases={len(ins) - 1: 0} if has_into else {},
        compiler_params=_params(("parallel", "parallel", "arbitrary")))(*ins)


def _ew(fn, tens, vecs=(), *, outs=(), sums=(), name, tr=256):
    tens = [t if isinstance(t, tuple) else (t, t.shape[1], 0) for t in tens]
    rows = tens[0][0].shape[0]
    tr = min(tr, rows)
    n_t, n_v, n_o, n_s = len(tens), len(vecs), len(outs), len(sums)

    def body(*refs):
        i = pl.program_id(0)
        t_blocks = [r[...] for r in refs[:n_t]]
        v_blocks = [r[...] for r in refs[n_t:n_t + n_v]]
        o_refs = refs[n_t + n_v:n_t + n_v + n_o]
        s_refs = refs[n_t + n_v + n_o:]
        o_vals, s_vals = fn(*t_blocks, *v_blocks)
        for r, v in zip(o_refs, o_vals):
            r[...] = v.astype(r.dtype)
        if n_s:
            @pl.when(i == 0)
            def _():
                for r in s_refs:
                    r[...] = jnp.zeros_like(r)
            for r, v in zip(s_refs, s_vals):
                r[...] += jnp.sum(v.astype(F32), axis=0, keepdims=True)

    in_specs = [pl.BlockSpec((tr, w), functools.partial(lambda i, cb: (i, cb), cb=cb)) for _, w, cb in tens]
    in_specs += [pl.BlockSpec(v.shape, functools.partial(lambda i, nd: (0,) * nd, nd=v.ndim)) for v in vecs]
    out_specs = [pl.BlockSpec((tr, c), lambda i: (i, 0)) for c, _ in outs]
    out_specs += [pl.BlockSpec((1, c), lambda i: (0, 0)) for c in sums]
    out_shape = [jax.ShapeDtypeStruct((rows, c), dt) for c, dt in outs]
    out_shape += [jax.ShapeDtypeStruct((1, c), F32) for c in sums]
    res = pl.pallas_call(
        body, name=name, grid=(rows // tr,), in_specs=in_specs, out_specs=out_specs, out_shape=out_shape,
        compiler_params=_params(("arbitrary",)))(*[t[0] for t in tens], *vecs)
    return res


def _sigmoid(z):
    return 1.0 / (1.0 + jnp.exp(-z))


def _rms_fwd(x, g, name):
    def fn(xb, gb):
        r = lax.rsqrt(jnp.mean(xb * xb, axis=-1, keepdims=True) + EPS)
        return ((xb * r) * gb,), ()
    return _ew(fn, [x], [g.reshape(1, -1)], outs=[(x.shape[1], BF16)], name=name)[0]


def _rms_bwd(x, g, dh, dres, name):
    def fn(xb, dhb, drb, gb):
        r = lax.rsqrt(jnp.mean(xb * xb, axis=-1, keepdims=True) + EPS)
        xh = xb * r
        dhf = dhb.astype(F32)
        dy = dhf * gb
        dx = r * (dy - xh * jnp.mean(dy * xh, axis=-1, keepdims=True))
        return (drb + dx,), (dhf * xh,)
    dx, dg = _ew(fn, [x, dh, dres], [g.reshape(1, -1)], outs=[(x.shape[1], F32)], sums=[x.shape[1]], name=name)
    return dx, dg[0]


def _mm_rms_bwd(a, b, x, g, dres, *, name, add=None, tm=512, tk=1024, b_tiles=None):
    m, k = a.shape
    n = x.shape[1]
    assert x.shape == (m, n) and (b_tiles is not None or b.shape == (n, k))
    tm, tk = _tile(m, tm), _tile(k, tk)
    nk = k // tk
    has_add = add is not None

    def body(*refs):
        a_ref, b_ref, x_ref, dr_ref, g_ref = refs[:5]
        add_ref = refs[5] if has_add else None
        dx_ref, dg_ref, acc_ref = refs[-3:]
        i, kk = pl.program_id(0), pl.program_id(1)

        @pl.when(kk == 0)
        def _():
            acc_ref[...] = jnp.zeros_like(acc_ref)

        @pl.when((kk == 0) & (i == 0))
        def _():
            dg_ref[...] = jnp.zeros_like(dg_ref)

        acc_ref[...] += lax.dot_general(a_ref[...].astype(MXU_DTYPE), b_ref[...].reshape((n, tk)).astype(MXU_DTYPE),
                                        (((1,), (1,)), ((), ())), preferred_element_type=F32)

        @pl.when(kk == nk - 1)
        def _():
            dh = acc_ref[...]
            if has_add:
                dh = dh + add_ref[...]
            xb = x_ref[...]
            r = lax.rsqrt(jnp.mean(xb * xb, axis=-1, keepdims=True) + EPS)
            xh = xb * r
            dy = dh * g_ref[...]
            dx_ref[...] = dr_ref[...] + r * (dy - xh * jnp.mean(dy * xh, axis=-1, keepdims=True))
            dg_ref[...] += jnp.sum(dh * xh, axis=0, keepdims=True)

    row = pl.BlockSpec((tm, n), lambda i, kk: (i, 0))
    vec = pl.BlockSpec((1, n), lambda i, kk: (0, 0))
    ins = [a, b, x, dres, g.reshape(1, n)] + ([add] if has_add else [])
    b_spec = pl.BlockSpec((n, tk), lambda i, kk: (0, kk)) if b_tiles is None else pl.BlockSpec(*b_tiles)
    specs = [pl.BlockSpec((tm, tk), lambda i, kk: (i, kk)), b_spec, row, row, vec]
    specs += [row] if has_add else []
    dx, dg = pl.pallas_call(
        body, name=name, grid=(m // tm, nk), in_specs=specs, out_specs=[row, vec],
        out_shape=[jax.ShapeDtypeStruct((m, n), F32), jax.ShapeDtypeStruct((1, n), F32)],
        scratch_shapes=[pltpu.VMEM((tm, n), F32)],
        compiler_params=_params(("arbitrary", "arbitrary")))(*ins)
    return dx, dg[0]


FFN_ROWS = 512
FFN_COLS = D_FF // 2


def _ffn_gate_up(x, gain, w_gu, layer):
    s, d = x.shape
    tm = min(FFN_ROWS, s)
    halves = D_FF // FFN_COLS

    def body(x_ref, gain_ref, wg_ref, wu_ref, h_ref, g_ref, u_ref, a_ref):
        @pl.when(pl.program_id(1) == 0)
        def _():
            xb = x_ref[...]
            h_ref[...] = ((xb * lax.rsqrt(jnp.mean(xb * xb, axis=-1, keepdims=True) + EPS)) * gain_ref[...]).astype(h_ref.dtype)

        hb = h_ref[...].astype(MXU_DTYPE)
        g = jnp.dot(hb, wg_ref[0, 0], preferred_element_type=F32)
        u = jnp.dot(hb, wu_ref[0, 0], preferred_element_type=F32)
        g_ref[...] = g.astype(g_ref.dtype)
        u_ref[...] = u.astype(u_ref.dtype)
        a_ref[...] = (g * _sigmoid(g) * u).astype(a_ref.dtype)

    row = pl.BlockSpec((tm, d), lambda i, jj: (i, 0))
    tile = pl.BlockSpec((tm, FFN_COLS), lambda i, jj: (i, jj))
    return pl.pallas_call(
        body, name="ffn_gate_up", grid=(s // tm, halves),
        in_specs=[row, pl.BlockSpec((1, d), lambda i, jj: (0, 0)),
                  pl.BlockSpec((1, 1, d, FFN_COLS), lambda i, jj: (jj, layer, 0, 0)),
                  pl.BlockSpec((1, 1, d, FFN_COLS), lambda i, jj: (halves + jj, layer, 0, 0))],
        out_specs=[row, tile, tile, tile],
        out_shape=[jax.ShapeDtypeStruct((s, d), BF16)] + [jax.ShapeDtypeStruct((s, D_FF), BF16)] * 3,
        compiler_params=_params(("parallel", "arbitrary")))(x, gain.reshape(1, d), w_gu, w_gu)


def _ffn_dgate_up(dx2, w_down, layer, g, u):
    s, d = dx2.shape
    tm = min(FFN_ROWS, s)
    halves = D_FF // FFN_COLS

    def body(dx_ref, w_ref, g_ref, u_ref, o_ref):
        jj = pl.program_id(1)
        df = lax.dot_general(dx_ref[...].astype(MXU_DTYPE), w_ref[...].reshape((FFN_COLS, d)), (((1,), (1,)), ((), ())),
                             preferred_element_type=F32)
        gf, uf = g_ref[...].astype(F32), u_ref[...].astype(F32)
        sg = _sigmoid(gf)
        dg = df * uf * (sg * (1.0 + gf * (1.0 - sg)))
        du = df * (gf * sg)
        o_ref[:, pl.ds(pl.multiple_of(jj * FFN_COLS, LANES), FFN_COLS)] = dg.astype(o_ref.dtype)
        o_ref[:, pl.ds(pl.multiple_of(D_FF + jj * FFN_COLS, LANES), FFN_COLS)] = du.astype(o_ref.dtype)

    tile = pl.BlockSpec((tm, FFN_COLS), lambda i, jj: (i, jj))
    return pl.pallas_call(
        body, name="ffn_dgate_up", grid=(s // tm, halves),
        in_specs=[pl.BlockSpec((tm, d), lambda i, jj: (i, 0)),
                  pl.BlockSpec((2, 1, FFN_COLS // 2, d), lambda i, jj: (jj, layer, 0, 0)), tile, tile],
        out_specs=pl.BlockSpec((tm, 2 * D_FF), lambda i, jj: (i, 0)),
        out_shape=jax.ShapeDtypeStruct((s, 2 * D_FF), BF16),
        compiler_params=_params(("parallel", "arbitrary")))(dx2, w_down, g, u)


def _ffn_fwd(x1, gain, w_gu, w_down, layer):
    d = x1.shape[1]
    h, g, u, act = _ffn_gate_up(x1, gain, w_gu, layer)
    x2 = _mm(act, w_down, name="ffn_down", add=x1, tk=FFN_COLS,
             b_tiles=(D_FF, d, (2, 1, FFN_COLS // 2, d), lambda i, j, kk: (kk, layer, 0, 0)))
    return x2, (x1, h, g, u, act)


def _ffn_bwd(dx2, saved, gain, w_gu, w_down, layer, dw_gu, dw_down):
    x1, h, g, u, act = saved
    d = x1.shape[1]
    dw_down = _mm(act, dx2, ta=True, name="ffn_dw_down", out_dtype=BF16, tm=FFN_COLS, into=dw_down,
                  out_tiles=(dw_down.shape, (2, 1, FFN_COLS // 2, d), lambda i, j, kk: (i, layer, 0, 0)))
    dgu = _ffn_dgate_up(dx2, w_down, layer, g, u)
    dw_gu = _mm(h, dgu, ta=True, name="ffn_dw_gu", out_dtype=BF16, tn=FFN_COLS, into=dw_gu,
                out_tiles=(dw_gu.shape, (1, 1, d, FFN_COLS), lambda i, j, kk: (j, layer, i, 0)))
    dx1, dgain = _mm_rms_bwd(dgu, w_gu, x1, gain, dx2, name="ffn_dh", tm=1024, tk=FFN_COLS,
                             b_tiles=((1, 1, d, FFN_COLS), lambda i, kk: (kk, layer, 0, 0)))
    return dx1, dgain, dw_gu, dw_down


def _loss_head(x, gain, target):
    d = x.shape[1]

    def fn(xb, tb, gb):
        r = lax.rsqrt(jnp.mean(xb * xb, axis=-1, keepdims=True) + EPS)
        xh = xb * r
        y = xh * gb
        err = y - tb
        dyv = err * (1.0 / d)
        dyg = dyv * gb
        dx = r * (dyg - xh * jnp.mean(dyg * xh, axis=-1, keepdims=True))
        return (dx,), (0.5 * err * err * (1.0 / d), dyv * xh)
    dx, lsum, dg = _ew(fn, [x, target], [gain.reshape(1, -1)], outs=[(d, F32)], sums=[d, d], name="loss_head")
    return jnp.sum(lsum), dx, dg[0]


ATT_BLOCK = 256
CUM_BLOCK = 512
NEG_INF = -1e30


def _fox_gate_fwd(fl_row, b_col):
    nh, s = fl_row.shape
    tb = min(CUM_BLOCK, s)

    def body(fl_ref, b_ref, z_ref, c_ref):
        upper = (lax.broadcasted_iota(jnp.int32, (tb, tb), 0) <= lax.broadcasted_iota(jnp.int32, (tb, tb), 1)).astype(F32)
        carry = jnp.zeros((nh, 1), F32)
        for blk in range(s // tb):
            z = fl_ref[:, blk * tb:(blk + 1) * tb] + b_ref[...]
            logf = jnp.minimum(z, 0.0) - jnp.log(1.0 + jnp.exp(-jnp.abs(z)))
            cs = jnp.dot(logf, upper, precision=lax.Precision.HIGHEST, preferred_element_type=F32) + carry
            z_ref[:, blk * tb:(blk + 1) * tb] = z
            c_ref[:, blk * tb:(blk + 1) * tb] = cs
            carry = cs[:, tb - 1:tb]

    return pl.pallas_call(body, name="fox_gate_fwd", out_shape=[jax.ShapeDtypeStruct((nh, s), F32)] * 2,
                          compiler_params=_params())(fl_row, b_col)


def _fox_gate_bwd(dc_row, z_row):
    nh, s = dc_row.shape
    tb = min(CUM_BLOCK, s)

    def body(dc_ref, z_ref, dz_ref, db_ref):
        lower = (lax.broadcasted_iota(jnp.int32, (tb, tb), 0) >= lax.broadcasted_iota(jnp.int32, (tb, tb), 1)).astype(F32)
        carry = jnp.zeros((nh, 1), F32)
        db = jnp.zeros((nh, 1), F32)
        for blk in reversed(range(s // tb)):
            dc = dc_ref[:, blk * tb:(blk + 1) * tb]
            rs = jnp.dot(dc, lower, precision=lax.Precision.HIGHEST, preferred_element_type=F32) + carry
            dz = rs * _sigmoid(-z_ref[:, blk * tb:(blk + 1) * tb])
            dz_ref[:, blk * tb:(blk + 1) * tb] = dz
            db = db + jnp.sum(dz, axis=1, keepdims=True)
            carry = rs[:, 0:1]
        db_ref[...] = db

    return pl.pallas_call(body, name="fox_gate_bwd",
                          out_shape=[jax.ShapeDtypeStruct((nh, s), F32), jax.ShapeDtypeStruct((nh, 1), F32)],
                          compiler_params=_params())(dc_row, z_row)


def _head_masks(rows):
    lane = lax.broadcasted_iota(jnp.int32, (rows, LANES), 1)
    return lane < FOX_HEAD_DIM


def _attn_fwd(qkv, c_row):
    s = qkv.shape[0]
    t = min(ATT_BLOCK, s)
    nq = s // t
    scale = FOX_HEAD_DIM ** -0.5

    def body(q_ref, k_ref, v_ref, cr_ref, o_ref, lse_ref):
        i = pl.program_id(1)
        first = _head_masks(t)
        qs = q_ref[...] * scale
        zero = jnp.zeros_like(qs)
        qh = (jnp.where(first, qs, zero), jnp.where(first, zero, qs))
        causal = lax.broadcasted_iota(jnp.int32, (t, t), 0) >= lax.broadcasted_iota(jnp.int32, (t, t), 1)

        def block(j, carry, masked):
            ms, ls, acc = carry
            start = pl.multiple_of(j * t, t)
            ks, vs = k_ref[pl.ds(start, t), :], v_ref[pl.ds(start, t), :]
            new_m, new_l, alphas, pvs = [], [], [], []
            for hh in range(2):
                sc = lax.dot_general(qh[hh], ks, (((1,), (1,)), ((), ())), preferred_element_type=F32)
                sc = sc - cr_ref[0, hh:hh + 1, pl.ds(start, t)]
                if masked:
                    sc = jnp.where(causal, sc, NEG_INF)
                m_new = jnp.maximum(ms[hh], jnp.max(sc, axis=1, keepdims=True))
                p = jnp.exp(sc - m_new)
                alpha = jnp.exp(ms[hh] - m_new)
                new_m.append(m_new)
                new_l.append(alpha * ls[hh] + jnp.sum(p, axis=1, keepdims=True))
                alphas.append(alpha)
                p_hi = p.astype(MXU_DTYPE)
                p_lo = (p - p_hi.astype(F32)).astype(MXU_DTYPE)
                pvs.append(jnp.dot(p_hi, vs, preferred_element_type=F32) + jnp.dot(p_lo, vs, preferred_element_type=F32))
            acc = jnp.where(first, alphas[0], alphas[1]) * acc + jnp.where(first, pvs[0], pvs[1])
            return tuple(new_m), tuple(new_l), acc

        neg, nil = jnp.full((t, 1), NEG_INF, F32), jnp.zeros((t, 1), F32)
        carry = lax.fori_loop(0, i, functools.partial(block, masked=False), ((neg, neg), (nil, nil), jnp.zeros((t, LANES), F32)))
        ms, ls, acc = block(i, carry, True)
        o_ref[...] = (acc / jnp.where(first, ls[0], ls[1])).astype(o_ref.dtype)
        lse_ref[0] = jnp.concatenate([ms[0] + jnp.log(ls[0]), ms[1] + jnp.log(ls[1])], axis=1)

    np_ = HEAD_PAIRS
    return pl.pallas_call(
        body, name="fox_attn_fwd", grid=(np_, nq),
        in_specs=[pl.BlockSpec((t, LANES), lambda p, i: (i, p)),
                  pl.BlockSpec((s, LANES), lambda p, i: (0, np_ + p)),
                  pl.BlockSpec((s, LANES), lambda p, i: (0, 2 * np_ + p)),
                  pl.BlockSpec((1, 2, s), lambda p, i: (p, 0, 0))],
        out_specs=[pl.BlockSpec((t, LANES), lambda p, i: (i, p)),
                   pl.BlockSpec((1, t, 2), lambda p, i: (p, i, 0))],
        out_shape=[jax.ShapeDtypeStruct((s, D_MODEL), F32), jax.ShapeDtypeStruct((np_, s, 2), F32)],
        compiler_params=_params(("parallel", "arbitrary")))(qkv, qkv, qkv, c_row)


def _attn_bwd(qkv, do, lse, delta, c_row):
    s = qkv.shape[0]
    t = min(ATT_BLOCK, s)
    nb = s // t
    scale = FOX_HEAD_DIM ** -0.5
    np_ = HEAD_PAIRS

    def body(q_ref, k_ref, v_ref, do_ref, lse_ref, dl_ref, cr_ref, dq_ref, dk_ref, dv_ref, dc_ref):
        j = pl.program_id(1)
        first = _head_masks(t)
        causal = lax.broadcasted_iota(jnp.int32, (t, t), 0) >= lax.broadcasted_iota(jnp.int32, (t, t), 1)
        kb = k_ref[...]
        vb = v_ref[...]

        @pl.when(j == 0)
        def _():
            dq_ref[...] = jnp.zeros_like(dq_ref)

        def step(i, carry, masked):
            dk_acc, dv_acc, dc_accs = carry
            rows = pl.ds(pl.multiple_of(i * t, t), t)
            qs = q_ref[rows, :] * scale
            dob = do_ref[rows, :]
            zero = jnp.zeros_like(qs)
            dks, dvs, dqs, dcs = [], [], [], []
            for hh in range(2):
                qh = jnp.where(first, qs, zero) if hh == 0 else jnp.where(first, zero, qs)
                doh = jnp.where(first, dob, zero) if hh == 0 else jnp.where(first, zero, dob)
                sc = lax.dot_general(qh, kb, (((1,), (1,)), ((), ())), preferred_element_type=F32)
                p = jnp.exp(sc - cr_ref[0, hh:hh + 1, :] - lse_ref[0, rows, hh:hh + 1])
                if masked:
                    p = jnp.where(causal, p, 0.0)
                dp = lax.dot_general(doh, vb, (((1,), (1,)), ((), ())), preferred_element_type=F32)
                ds = p * (dp - dl_ref[0, rows, hh:hh + 1])
                pb, dsb = p.astype(MXU_DTYPE), ds.astype(MXU_DTYPE)
                dvs.append(lax.dot_general(pb, dob, (((0,), (0,)), ((), ())), preferred_element_type=F32))
                dks.append(lax.dot_general(dsb, qs, (((0,), (0,)), ((), ())), preferred_element_type=F32))
                dqs.append(jnp.dot(dsb, kb, preferred_element_type=F32))
                dcs.append(dc_accs[hh] - jnp.sum(ds, axis=0, keepdims=True))
            dq_ref[rows, :] += jnp.where(first, dqs[0], dqs[1]) * scale
            return (dk_acc + jnp.where(first, dks[0], dks[1]), dv_acc + jnp.where(first, dvs[0], dvs[1]), tuple(dcs))

        nil = jnp.zeros((1, t), F32)
        carry = step(j, (jnp.zeros((t, LANES), F32), jnp.zeros((t, LANES), F32), (nil, nil)), True)
        dk_acc, dv_acc, dc_accs = lax.fori_loop(j + 1, nb, functools.partial(step, masked=False), carry)
        dk_ref[...] = dk_acc.astype(dk_ref.dtype)
        dv_ref[...] = dv_acc.astype(dv_ref.dtype)
        dc_ref[0] = jnp.concatenate(dc_accs, axis=0)

    return pl.pallas_call(
        body, name="fox_attn_bwd", grid=(np_, nb),
        in_specs=[pl.BlockSpec((s, LANES), lambda p, j: (0, p)),
                  pl.BlockSpec((t, LANES), lambda p, j: (j, np_ + p)),
                  pl.BlockSpec((t, LANES), lambda p, j: (j, 2 * np_ + p)),
                  pl.BlockSpec((s, LANES), lambda p, j: (0, p)),
                  pl.BlockSpec((1, s, 2), lambda p, j: (p, 0, 0)),
                  pl.BlockSpec((1, s, 2), lambda p, j: (p, 0, 0)),
                  pl.BlockSpec((1, 2, t), lambda p, j: (p, 0, j))],
        out_specs=[pl.BlockSpec((s, LANES), lambda p, j: (0, p)),
                   pl.BlockSpec((t, LANES), lambda p, j: (j, p)),
                   pl.BlockSpec((t, LANES), lambda p, j: (j, p)),
                   pl.BlockSpec((1, 2, t), lambda p, j: (p, 0, j))],
        out_shape=[jax.ShapeDtypeStruct((s, D_MODEL), F32), jax.ShapeDtypeStruct((s, D_MODEL), BF16),
                   jax.ShapeDtypeStruct((s, D_MODEL), BF16), jax.ShapeDtypeStruct((np_, 2, s), F32)],
        compiler_params=_params(("parallel", "arbitrary")))(qkv, qkv, qkv, do, lse, delta, c_row)


ATT_QUERIES = 512


def _lanes(a, width):
    return jnp.concatenate([a] * (width // LANES), axis=1)


def _attn_fwd_t(qkv, c_lanes):
    s = qkv.shape[0]
    t, tq = min(ATT_BLOCK, s), min(2 * ATT_QUERIES, s)
    nq, per = s // tq, tq // t
    scale = FOX_HEAD_DIM ** -0.5
    np_ = HEAD_PAIRS
    nt = (((1,), (1,)), ((), ()))

    def body(q_ref, k_ref, v_ref, c_ref, o_ref, lse_ref, vt_ref):
        i = pl.program_id(1)

        @pl.when(i == 0)
        def _():
            for r in range(s // t):
                vt_ref[:, r * t:(r + 1) * t] = v_ref[r * t:(r + 1) * t, :].astype(F32).T.astype(vt_ref.dtype)

        first = _head_masks(tq)
        upper = lax.broadcasted_iota(jnp.int32, (LANES, tq), 0) < FOX_HEAD_DIM
        qs = q_ref[...] * scale
        zero = jnp.zeros_like(qs)
        qh = (jnp.where(first, qs, zero), jnp.where(first, zero, qs))
        key_at = lax.broadcasted_iota(jnp.int32, (t, tq), 0)
        query_at = lax.broadcasted_iota(jnp.int32, (t, tq), 1)

        def block(j, carry, diagonal=None):
            ms, ls, acc = carry
            start = pl.multiple_of(j * t, t)
            kb, vt = k_ref[pl.ds(start, t), :], vt_ref[:, pl.ds(start, t)]
            new_m, new_l, alphas, pvs = [], [], [], []
            for hh in range(2):
                sc = lax.dot_general(kb, qh[hh], nt, preferred_element_type=F32) - _lanes(c_ref[0, hh, pl.ds(start, t), :], tq)
                if diagonal is not None:
                    sc = jnp.where(key_at + diagonal * t <= query_at, sc, NEG_INF)
                m_new = jnp.maximum(ms[hh], jnp.max(sc, axis=0, keepdims=True))
                p = jnp.exp(sc - m_new)
                alpha = jnp.exp(ms[hh] - m_new)
                new_m.append(m_new)
                new_l.append(alpha * ls[hh] + jnp.sum(p, axis=0, keepdims=True))
                alphas.append(alpha)
                p_hi = p.astype(MXU_DTYPE)
                p_lo = (p - p_hi.astype(F32)).astype(MXU_DTYPE)
                pvs.append(jnp.dot(vt, p_hi, preferred_element_type=F32) + jnp.dot(vt, p_lo, preferred_element_type=F32))
            acc = jnp.where(upper, alphas[0], alphas[1]) * acc + jnp.where(upper, pvs[0], pvs[1])
            return tuple(new_m), tuple(new_l), acc

        neg, nil = jnp.full((1, tq), NEG_INF, F32), jnp.zeros((1, tq), F32)
        carry = lax.fori_loop(0, per * i, block, ((neg, neg), (nil, nil), jnp.zeros((LANES, tq), F32)))
        for d in range(per):
            carry = block(per * i + d, carry, diagonal=d)
        ms, ls, acc = carry
        o_ref[...] = (acc / jnp.where(upper, ls[0], ls[1])).T.astype(o_ref.dtype)
        lse_ref[0] = jnp.concatenate([ms[0] + jnp.log(ls[0]), ms[1] + jnp.log(ls[1])], axis=0)

    return pl.pallas_call(
        body, name="fox_attn_fwd", grid=(np_, nq),
        in_specs=[pl.BlockSpec((tq, LANES), lambda p, i: (i, p)),
                  pl.BlockSpec((s, LANES), lambda p, i: (0, np_ + p)),
                  pl.BlockSpec((s, LANES), lambda p, i: (0, 2 * np_ + p)),
                  pl.BlockSpec((1, 2, s, LANES), lambda p, i: (p, 0, 0, 0))],
        out_specs=[pl.BlockSpec((tq, LANES), lambda p, i: (i, p)),
                   pl.BlockSpec((1, 2, tq), lambda p, i: (p, 0, i))],
        out_shape=[jax.ShapeDtypeStruct((s, D_MODEL), F32), jax.ShapeDtypeStruct((np_, 2, s), F32)],
        scratch_shapes=[pltpu.VMEM((LANES, s), MXU_DTYPE)],
        compiler_params=_params(("parallel", "arbitrary")))(qkv, qkv, qkv, c_lanes)


def _attn_bwd_t(qkv, do, lse, delta, c_lanes):
    s = qkv.shape[0]
    t, tq = min(ATT_BLOCK, s), min(ATT_QUERIES, s)
    nb, nq, per = s // t, s // tq, tq // t
    scale = FOX_HEAD_DIM ** -0.5
    np_ = HEAD_PAIRS
    nt = (((1,), (1,)), ((), ()))

    def body(q_ref, k_ref, v_ref, do_ref, lse_ref, dl_ref, c_ref, dq_ref, dk_ref, dv_ref, dc_ref, dqt_ref):
        j = pl.program_id(1)
        first_q, first = _head_masks(tq), _head_masks(t)
        upper = lax.broadcasted_iota(jnp.int32, (LANES, tq), 0) < FOX_HEAD_DIM
        causal = (lax.broadcasted_iota(jnp.int32, (t, tq), 0) + (j % per) * t) <= lax.broadcasted_iota(jnp.int32, (t, tq), 1)
        kb, vb = k_ref[...], v_ref[...]
        kt = kb.astype(F32).T.astype(MXU_DTYPE)
        cb = (_lanes(c_ref[0, 0], tq), _lanes(c_ref[0, 1], tq))

        @pl.when(j == 0)
        def _():
            dqt_ref[...] = jnp.zeros_like(dqt_ref)

        def step(i, carry, masked):
            dk_acc, dv_acc, dc_accs = carry
            start = pl.multiple_of(i * tq, tq)
            qs = q_ref[pl.ds(start, tq), :] * scale
            dob = do_ref[pl.ds(start, tq), :]
            zero = jnp.zeros_like(qs)
            dks, dvs, dqs, dcs = [], [], [], []
            for hh in range(2):
                qh = jnp.where(first_q, qs, zero) if hh == 0 else jnp.where(first_q, zero, qs)
                doh = jnp.where(first_q, dob, zero) if hh == 0 else jnp.where(first_q, zero, dob)
                sc = lax.dot_general(kb, qh, nt, preferred_element_type=F32)
                p = jnp.exp(sc - cb[hh] - lse_ref[0, hh:hh + 1, pl.ds(start, tq)])
                if masked:
                    p = jnp.where(causal, p, 0.0)
                dp = lax.dot_general(vb, doh, nt, preferred_element_type=F32)
                ds = p * (dp - dl_ref[0, hh:hh + 1, pl.ds(start, tq)])
                pb, dsb = p.astype(MXU_DTYPE), ds.astype(MXU_DTYPE)
                dvs.append(jnp.dot(pb, dob, preferred_element_type=F32))
                dks.append(jnp.dot(dsb, qs, preferred_element_type=F32))
                dqs.append(jnp.dot(kt, dsb, preferred_element_type=F32))
                dcs.append(dc_accs[hh] - jnp.sum(ds, axis=1, keepdims=True))
            dqt_ref[:, pl.ds(start, tq)] += jnp.where(upper, dqs[0], dqs[1]) * scale
            return (dk_acc + jnp.where(first, dks[0], dks[1]), dv_acc + jnp.where(first, dvs[0], dvs[1]), tuple(dcs))

        nil, col = jnp.zeros((t, LANES), F32), jnp.zeros((t, 1), F32)
        carry = step(j // per, (nil, nil, (col, col)), True)
        dk_acc, dv_acc, dc_accs = lax.fori_loop(j // per + 1, nq, functools.partial(step, masked=False), carry)
        dk_ref[...] = dk_acc.astype(dk_ref.dtype)
        dv_ref[...] = dv_acc.astype(dv_ref.dtype)
        dc_ref[0, 0] = jnp.broadcast_to(dc_accs[0], (t, LANES))
        dc_ref[0, 1] = jnp.broadcast_to(dc_accs[1], (t, LANES))

        @pl.when(j == nb - 1)
        def _():
            for r in range(nb):
                dq_ref[r * t:(r + 1) * t, :] = dqt_ref[:, r * t:(r + 1) * t].T

    row = pl.BlockSpec((1, 2, s), lambda p, j: (p, 0, 0))
    return pl.pallas_call(
        body, name="fox_attn_bwd", grid=(np_, nb),
        in_specs=[pl.BlockSpec((s, LANES), lambda p, j: (0, p)),
                  pl.BlockSpec((t, LANES), lambda p, j: (j, np_ + p)),
                  pl.BlockSpec((t, LANES), lambda p, j: (j, 2 * np_ + p)),
                  pl.BlockSpec((s, LANES), lambda p, j: (0, p)), row, row,
                  pl.BlockSpec((1, 2, t, LANES), lambda p, j: (p, 0, j, 0))],
        out_specs=[pl.BlockSpec((s, LANES), lambda p, j: (0, p)),
                   pl.BlockSpec((t, LANES), lambda p, j: (j, p)),
                   pl.BlockSpec((t, LANES), lambda p, j: (j, p)),
                   pl.BlockSpec((1, 2, t, LANES), lambda p, j: (p, 0, j, 0))],
        out_shape=[jax.ShapeDtypeStruct((s, D_MODEL), F32), jax.ShapeDtypeStruct((s, D_MODEL), BF16),
                   jax.ShapeDtypeStruct((s, D_MODEL), BF16), jax.ShapeDtypeStruct((np_, 2, s, LANES), F32)],
        scratch_shapes=[pltpu.VMEM((LANES, s), F32)],
        compiler_params=_params(("parallel", "arbitrary")))(qkv, qkv, qkv, do, lse, delta, c_lanes)


def _head_sums(a, b, name):
    d = a.shape[1]
    sel = (jnp.arange(d)[:, None] // FOX_HEAD_DIM == jnp.arange(LANES)[None, :]).astype(F32)

    def fn(ab, bb, selb):
        prod = ab.astype(F32) * bb.astype(F32)
        return (jnp.dot(prod, selb, precision=lax.Precision.HIGHEST, preferred_element_type=F32),), ()
    return _ew(fn, [a, b], [sel], outs=[(LANES, F32)], name=name)[0]


def _pairs_col(a16):
    s = a16.shape[0]
    return a16.reshape(s, HEAD_PAIRS, 2).transpose(1, 0, 2)


def _fox_fwd(x, gain, w_qkv, w_f, b_f, w_out):
    s = x.shape[0]
    h = _rms_fwd(x, gain, "mix_norm")
    qkv = _mm(h, w_qkv, name="fox_qkv", out_dtype=BF16)
    fl = _mm(h, w_f, name="fox_f", tn=LANES)
    z_row, c_rowf = _fox_gate_fwd(fl[:, :FOX_HEADS].T, b_f.reshape(FOX_HEADS, 1))
    c_lanes = jnp.broadcast_to(c_rowf.reshape(HEAD_PAIRS, 2, s, 1), (HEAD_PAIRS, 2, s, LANES))
    o, lse = _attn_fwd_t(qkv, c_lanes)
    x1 = _mm(o, w_out, name="fox_out", add=x)
    return x1, (x, h, qkv, z_row, c_lanes, o, lse)


def _fox_bwd(dx1, saved, gain, w_qkv, w_f, w_out):
    x, h, qkv, z_row, c_lanes, o, lse = saved
    s = x.shape[0]
    do = _mm(dx1, w_out, tb=True, name="fox_do", out_dtype=BF16)
    dw_out = _mm(o, dx1, ta=True, name="fox_dw_out", out_dtype=BF16)
    delta = _head_sums(do, o, "fox_delta")[:, :FOX_HEADS].T.reshape(HEAD_PAIRS, 2, s)
    dq, dk, dv, dc = _attn_bwd_t(qkv, do, lse, delta, c_lanes)
    dz_row, db = _fox_gate_bwd(dc[..., 0].reshape(FOX_HEADS, s), z_row)
    dqkv = jnp.concatenate([dq.astype(BF16), dk, dv], axis=1)
    dfl = jnp.pad(dz_row.T, ((0, 0), (0, LANES - FOX_HEADS))).astype(BF16)
    dw_qkv = _mm(h, dqkv, ta=True, name="fox_dw_qkv", out_dtype=BF16)
    dw_f = _mm(h, dfl, ta=True, name="fox_dw_f", out_dtype=BF16, tn=LANES)
    dh = _mm(dqkv, w_qkv, tb=True, name="fox_dh_qkv")
    dx, dgain = _mm_rms_bwd(dfl, w_f, x, gain, dx1, name="fox_dh_f", add=dh)
    dw_in = jnp.concatenate([dw_qkv, dw_f[:, :FOX_HEADS]], axis=1)
    return dx, dgain, dw_in, db.reshape(FOX_HEADS), dw_out


S5_ROWS = 512
SCAN_CHUNKS = SUBLANES


def _s5_operands(a_re, a_im, log_dt, b_re, b_im, c_re, c_im):
    dt = jnp.exp(log_dt)[:, None]
    mag, ang = jnp.exp(a_re * dt), a_im * dt
    lr, li = mag * jnp.cos(ang), mag * jnp.sin(ang)
    den = a_re * a_re + a_im * a_im
    cr = ((lr - 1.0) * a_re + li * a_im) / den
    ci = (li * a_re - (lr - 1.0) * a_im) / den
    bbr = cr[..., None] * b_re - ci[..., None] * b_im
    bbi = cr[..., None] * b_im + ci[..., None] * b_re
    nb = S5_BLOCKS
    lam = jnp.stack([lr.reshape(nb, 2, S5_HALF), li.reshape(nb, 2, S5_HALF)], axis=2)
    eye4, eye2 = jnp.eye(4, dtype=F32), jnp.eye(2, dtype=F32)
    bb = jnp.stack([bbr, bbi], axis=0).reshape(2, nb, 2, 4, S5_STATE, S5_GROUP)
    bmat = jnp.einsum("rbhgpc,kg,jh->bhjkcrgp", bb, eye4, eye2).reshape(nb, 2, 128, 2 * S5_HALF)
    cc = jnp.stack([c_re, -c_im], axis=0).reshape(2, nb, 2, 4, S5_GROUP, S5_STATE)
    cmat = jnp.einsum("rbhgcp,kg,jh->bhrgpjkc", cc, eye4, eye2).reshape(nb, 2, 2 * S5_HALF, 128)
    return lam, bmat, cmat


def _time_to_scan_order(a):
    s, d = a.shape
    return a.reshape(SCAN_CHUNKS, s // SCAN_CHUNKS, d).transpose(1, 0, 2).reshape(s, d)


def _scan_to_time_order(a):
    s, d = a.shape
    return a.reshape(s // SCAN_CHUNKS, SCAN_CHUNKS, d).transpose(1, 0, 2).reshape(s, d)


def _scan_chunks(xr_ref, xi_ref, lr, li, nst, reverse, after_step=None, state=None):
    lanes = lr.shape[1]
    lr8, li8 = jnp.broadcast_to(lr, (SUBLANES, lanes)), jnp.broadcast_to(li, (SUBLANES, lanes))
    zero8 = jnp.zeros((SUBLANES, lanes), F32)

    def rows_of(n):
        s = (nst - 1 - n) if reverse else n
        return s, pl.ds(pl.multiple_of(s * SUBLANES, SUBLANES), SUBLANES)

    def local(n, carry):
        pr, pi = carry
        _, rows = rows_of(n)
        nr = lr8 * pr - li8 * pi + xr_ref[rows, :]
        ni = lr8 * pi + li8 * pr + xi_ref[rows, :]
        xr_ref[rows, :] = nr
        xi_ref[rows, :] = ni
        return nr, ni

    er, ei = lax.fori_loop(0, nst, local, (zero8, zero8))
    pr, pi = lr, li
    for _ in range(int(math.log2(nst))):
        pr, pi = pr * pr - pi * pi, 2.0 * pr * pi
    tr = ti = jnp.zeros((1, lanes), F32)
    ent_r, ent_i = [None] * SCAN_CHUNKS, [None] * SCAN_CHUNKS
    for k in (reversed(range(SCAN_CHUNKS)) if reverse else range(SCAN_CHUNKS)):
        ent_r[k], ent_i[k] = tr, ti
        tr, ti = er[k:k + 1] + (pr * tr - pi * ti), ei[k:k + 1] + (pr * ti + pi * tr)
    in_r, in_i = jnp.concatenate(ent_r, axis=0), jnp.concatenate(ent_i, axis=0)

    def fix(n, carry):
        wr, wi, st = carry
        s, rows = rows_of(n)
        nr = xr_ref[rows, :] + (wr * in_r - wi * in_i)
        ni = xi_ref[rows, :] + (wr * in_i + wi * in_r)
        xr_ref[rows, :] = nr
        xi_ref[rows, :] = ni
        if after_step is not None:
            st = after_step(s, nr, ni, st)
        return wr * lr8 - wi * li8, wr * li8 + wi * lr8, st

    _, _, state = lax.fori_loop(0, nst, fix, (lr8, li8, state))
    return in_r, in_i, state


def _s5_fill_states(u_ref, bm, xr_ref, xi_ref, s):
    rc = min(S5_ROWS, s)

    def fill(r, _):
        rows = pl.ds(pl.multiple_of(r * rc, rc), rc)
        bu = jnp.dot(u_ref[rows, :].astype(MXU_DTYPE), bm, preferred_element_type=F32)
        xr_ref[rows, :] = bu[:, :S5_HALF]
        xi_ref[rows, :] = bu[:, S5_HALF:]
        return 0
    lax.fori_loop(0, s // rc, fill, 0)


def _s5_specs():
    return [pl.BlockSpec((1, 2, 2, S5_HALF), lambda b: (b, 0, 0, 0)),
            pl.BlockSpec((1, 2, 128, 2 * S5_HALF), lambda b: (b, 0, 0, 0)),
            pl.BlockSpec((1, 2, 2 * S5_HALF, 128), lambda b: (b, 0, 0, 0)),
            pl.BlockSpec((1, LANES), lambda b: (0, b))]


def _s5_scan_fwd(u, lam, bmat, cmat, dvec):
    s = u.shape[0]
    nst = s // SCAN_CHUNKS
    rc = min(S5_ROWS, s)

    def body(u_ref, lam_ref, b_ref, c_ref, d_ref, y_ref, xr_ref, xi_ref):
        y_ref[...] = u_ref[...] * d_ref[...]
        for hb in range(2):
            _s5_fill_states(u_ref, b_ref[0, hb], xr_ref, xi_ref, s)
            _scan_chunks(xr_ref, xi_ref, lam_ref[0, hb, 0:1, :], lam_ref[0, hb, 1:2, :], nst, False)
            cm = c_ref[0, hb]

            def emit(r, _, cm=cm):
                rows = pl.ds(pl.multiple_of(r * rc, rc), rc)
                y_ref[rows, :] += (jnp.dot(xr_ref[rows, :].astype(MXU_DTYPE), cm[:S5_HALF], preferred_element_type=F32)
                                   + jnp.dot(xi_ref[rows, :].astype(MXU_DTYPE), cm[S5_HALF:], preferred_element_type=F32))
                return 0
            lax.fori_loop(0, s // rc, emit, 0)

    blk = pl.BlockSpec((s, LANES), lambda b: (0, b))
    return pl.pallas_call(
        body, name="s5_scan_fwd", grid=(S5_BLOCKS,), in_specs=[blk] + _s5_specs(), out_specs=blk,
        out_shape=jax.ShapeDtypeStruct(u.shape, F32),
        scratch_shapes=[pltpu.VMEM((s, S5_HALF), F32)] * 2,
        compiler_params=_params(("parallel",)))(u, lam, bmat, cmat, dvec)


def _s5_scan_bwd(u, dy, lam, bmat, cmat, dvec):
    s = u.shape[0]
    nst = s // SCAN_CHUNKS
    rc = min(S5_ROWS, s)
    nt = (((1,), (1,)), ((), ()))
    tn = (((0,), (0,)), ((), ()))

    def body(u_ref, dy_ref, lam_ref, b_ref, c_ref, d_ref, du_ref, db_ref, dc_ref, dl_ref, dd_ref,
             xr_ref, xi_ref, gr_ref, gi_ref):
        du_ref[...] = dy_ref[...] * d_ref[...]
        dd_ref[...] = jnp.sum(dy_ref[...] * u_ref[...], axis=0, keepdims=True)
        db_ref[...] = jnp.zeros_like(db_ref)
        dc_ref[...] = jnp.zeros_like(dc_ref)
        for hb in range(2):
            bm, cm = b_ref[0, hb], c_ref[0, hb]
            lr, li = lam_ref[0, hb, 0:1, :], lam_ref[0, hb, 1:2, :]
            _s5_fill_states(u_ref, bm, xr_ref, xi_ref, s)
            xin_r, xin_i, _ = _scan_chunks(xr_ref, xi_ref, lr, li, nst, False)

            def fill_g(r, _, cm=cm):
                rows = pl.ds(pl.multiple_of(r * rc, rc), rc)
                g = lax.dot_general(dy_ref[rows, :].astype(MXU_DTYPE), cm, nt, preferred_element_type=F32)
                gr_ref[rows, :] = g[:, :S5_HALF]
                gi_ref[rows, :] = g[:, S5_HALF:]
                return 0
            lax.fori_loop(0, s // rc, fill_g, 0)
            def lam_grad(st, g_r, g_i, acc, xin_r=xin_r, xin_i=xin_i):
                prev = pl.ds(pl.multiple_of(jnp.maximum(st - 1, 0) * SUBLANES, SUBLANES), SUBLANES)
                x_r = jnp.where(st > 0, xr_ref[prev, :], xin_r)
                x_i = jnp.where(st > 0, xi_ref[prev, :], xin_i)
                return acc[0] + (g_r * x_r + g_i * x_i), acc[1] + (g_i * x_r - g_r * x_i)

            zero8 = jnp.zeros((SUBLANES, S5_HALF), F32)
            _, _, (a_r, a_i) = _scan_chunks(gr_ref, gi_ref, lr, -li, nst, True, after_step=lam_grad, state=(zero8, zero8))
            dl_ref[0, hb] = jnp.concatenate([jnp.sum(a_r, axis=0, keepdims=True),
                                             jnp.sum(a_i, axis=0, keepdims=True)], axis=0)

            def emit(r, _, bm=bm, hb=hb):
                rows = pl.ds(pl.multiple_of(r * rc, rc), rc)
                g = jnp.concatenate([gr_ref[rows, :], gi_ref[rows, :]], axis=1).astype(MXU_DTYPE)
                x = jnp.concatenate([xr_ref[rows, :], xi_ref[rows, :]], axis=1).astype(MXU_DTYPE)
                du_ref[rows, :] += lax.dot_general(g, bm, nt, preferred_element_type=F32)
                db_ref[0, hb] += lax.dot_general(u_ref[rows, :].astype(MXU_DTYPE), g, tn, preferred_element_type=F32)
                dc_ref[0, hb] += lax.dot_general(dy_ref[rows, :].astype(MXU_DTYPE), x, tn, preferred_element_type=F32)
                return 0
            lax.fori_loop(0, s // rc, emit, 0)

    blk = pl.BlockSpec((s, LANES), lambda b: (0, b))
    mat = pl.BlockSpec((1, 2, 128, 2 * S5_HALF), lambda b: (b, 0, 0, 0))
    return pl.pallas_call(
        body, name="s5_scan_bwd", grid=(S5_BLOCKS,), in_specs=[blk, blk] + _s5_specs(),
        out_specs=[blk, mat, mat, pl.BlockSpec((1, 2, 2, S5_HALF), lambda b: (b, 0, 0, 0)),
                   pl.BlockSpec((1, LANES), lambda b: (0, b))],
        out_shape=[jax.ShapeDtypeStruct(u.shape, F32),
                   jax.ShapeDtypeStruct((S5_BLOCKS, 2, 128, 2 * S5_HALF), F32),
                   jax.ShapeDtypeStruct((S5_BLOCKS, 2, 128, 2 * S5_HALF), F32),
                   jax.ShapeDtypeStruct((S5_BLOCKS, 2, 2, S5_HALF), F32),
                   jax.ShapeDtypeStruct((1, D_MODEL), F32)],
        scratch_shapes=[pltpu.VMEM((s, S5_HALF), F32)] * 4,
        compiler_params=_params(("parallel",)))(u, dy, lam, bmat, cmat, dvec)


_GELU_C = math.sqrt(2.0 / math.pi)


def _gelu_parts(y):
    inner = _GELU_C * (y + 0.044715 * y * y * y)
    th = jnp.tanh(inner)
    return 0.5 * y * (1.0 + th), th


def _s5_fwd(x, gain, w_in, ssm, dvec, w_glu):
    lam, bmat, cmat = ssm
    h = _rms_fwd(x, gain, "mix_norm")
    u = _mm(h, w_in, name="s5_in")
    y = _scan_to_time_order(_s5_scan_fwd(_time_to_scan_order(u), lam, bmat.astype(MXU_DTYPE), cmat.astype(MXU_DTYPE), dvec))
    g = _ew(lambda yb: ((_gelu_parts(yb)[0],), ()), [y], outs=[(D_MODEL, BF16)], name="s5_gelu")[0]
    vg = _mm(g, w_glu, name="s5_glu", out_dtype=BF16)

    def glu_fn(vb, gb, xb):
        return (xb + vb.astype(F32) * _sigmoid(gb.astype(F32)),), ()
    x1 = _ew(glu_fn, [(vg, D_MODEL, 0), (vg, D_MODEL, 1), x], outs=[(D_MODEL, F32)], name="s5_gate")[0]
    return x1, (x, h, u, y, g, vg)


def _s5_bwd(dx1, saved, gain, w_in, ssm, dvec, w_glu):
    x, h, u, y, g, vg = saved
    lam, bmat, cmat = ssm

    def dglu_fn(db, vb, gb):
        vf, sg = vb.astype(F32), _sigmoid(gb.astype(F32))
        return (jnp.concatenate([db * sg, db * vf * sg * (1.0 - sg)], axis=1),), ()
    dvg = _ew(dglu_fn, [dx1, (vg, D_MODEL, 0), (vg, D_MODEL, 1)], outs=[(2 * D_MODEL, BF16)], name="s5_dgate")[0]
    dw_glu = _mm(g, dvg, ta=True, name="s5_dw_glu", out_dtype=BF16)
    dg = _mm(dvg, w_glu, tb=True, name="s5_dg")

    def dgelu_fn(dgb, yb):
        _, th = _gelu_parts(yb)
        dinner = _GELU_C * (1.0 + 3.0 * 0.044715 * yb * yb)
        return (dgb * (0.5 * (1.0 + th) + 0.5 * yb * (1.0 - th * th) * dinner),), ()
    dy = _ew(dgelu_fn, [dg, y], outs=[(D_MODEL, F32)], name="s5_dgelu")[0]
    du_s, dbm, dct, dlam, ddvec = _s5_scan_bwd(_time_to_scan_order(u), _time_to_scan_order(dy), lam,
                                               bmat.astype(MXU_DTYPE), cmat.astype(MXU_DTYPE), dvec)
    du = _scan_to_time_order(du_s)
    dw_in = _mm(h, du, ta=True, name="s5_dw_in", out_dtype=BF16)
    dx, dgain = _mm_rms_bwd(du, w_in, x, gain, dx1, name="s5_dh")
    return dx, dgain, dw_in, (dlam, dbm, jnp.swapaxes(dct, 2, 3)), ddvec, dw_glu


POOL_BLOCK = 256
N_POOL_GROUPS = len(POOL_WINDOWS)


def _pool_bands(gi, i, t):
    w = jnp.left_shift(2, gi)
    r = lax.broadcasted_iota(jnp.int32, (t, t), 0)
    c = lax.broadcasted_iota(jnp.int32, (t, t), 1)
    inside = ((c <= r) & (c > r - w)).astype(MXU_DTYPE)
    before = (c > r - w + t).astype(MXU_DTYPE)

    def inv_count(block):
        pos = block * t + lax.broadcasted_iota(jnp.int32, (t, 1), 0)
        return 1.0 / jnp.minimum(pos + 1, w).astype(F32)
    return inside, before, inv_count


def _pool_fwd(x, gain, w_grp, b_grp, scale):
    s = x.shape[0]
    t = min(POOL_BLOCK, s)
    h = _rms_fwd(x, gain, "mix_norm")

    def body(h_ref, hp_ref, w_ref, b_ref, sc_ref, x_ref, x1_ref, diff_ref):
        gi, i = pl.program_id(0), pl.program_id(1)
        inside, before, inv_count = _pool_bands(gi, i, t)
        hc = h_ref[...]
        tot = jnp.dot(inside, hc.astype(MXU_DTYPE), preferred_element_type=F32)
        prev = jnp.dot(before, hp_ref[...].astype(MXU_DTYPE), preferred_element_type=F32)
        tot = tot + jnp.where(i > 0, prev, 0.0)
        diff = (tot * inv_count(i) - hc.astype(F32)).astype(diff_ref.dtype)
        y = (jnp.dot(diff.astype(MXU_DTYPE), w_ref[0], preferred_element_type=F32) + b_ref[...]) * sc_ref[...]
        diff_ref[...] = diff
        x1_ref[...] = x_ref[...] + y

    blk = pl.BlockSpec((t, POOL_WIDTH), lambda gi, i: (i, gi))
    vec = pl.BlockSpec((1, POOL_WIDTH), lambda gi, i: (0, gi))
    x1, diff = pl.pallas_call(
        body, name="pool_fwd", grid=(N_POOL_GROUPS, s // t),
        in_specs=[blk, pl.BlockSpec((t, POOL_WIDTH), lambda gi, i: (jnp.maximum(i - 1, 0), gi)),
                  pl.BlockSpec((1, POOL_WIDTH, POOL_WIDTH), lambda gi, i: (gi, 0, 0)), vec, vec, blk],
        out_specs=[blk, blk],
        out_shape=[jax.ShapeDtypeStruct(x.shape, F32), jax.ShapeDtypeStruct(x.shape, BF16)],
        compiler_params=_params(("parallel", "arbitrary")))(h, h, w_grp, b_grp, scale, x)
    return x1, (x, diff)


def _pool_bwd(dx1, saved, gain, w_grp, b_grp, scale):
    x, diff = saved
    s = x.shape[0]
    t = min(POOL_BLOCK, s)
    nb = s // t

    def body1(dx_ref, diff_ref, w_ref, b_ref, sc_ref, dd_ref, dw_ref, db_ref, dsc_ref):
        i = pl.program_id(1)

        @pl.when(i == 0)
        def _():
            dw_ref[...] = jnp.zeros_like(dw_ref)
            db_ref[...] = jnp.zeros_like(db_ref)
            dsc_ref[...] = jnp.zeros_like(dsc_ref)

        dfb = diff_ref[...].astype(MXU_DTYPE)
        ypre = jnp.dot(dfb, w_ref[0], preferred_element_type=F32) + b_ref[...]
        dxb = dx_ref[...]
        dy = dxb * sc_ref[...]
        dsc_ref[...] += jnp.sum(dxb * ypre, axis=0, keepdims=True)
        db_ref[...] += jnp.sum(dy, axis=0, keepdims=True)
        dyb = dy.astype(MXU_DTYPE)
        dw_ref[0] += lax.dot_general(dfb, dyb, (((0,), (0,)), ((), ())), preferred_element_type=F32)
        dd_ref[...] = lax.dot_general(dyb, w_ref[0], (((1,), (1,)), ((), ())), preferred_element_type=F32)

    blk = pl.BlockSpec((t, POOL_WIDTH), lambda gi, i: (i, gi))
    vec = pl.BlockSpec((1, POOL_WIDTH), lambda gi, i: (0, gi))
    mat = pl.BlockSpec((1, POOL_WIDTH, POOL_WIDTH), lambda gi, i: (gi, 0, 0))
    ddiff, dw, db, dsc = pl.pallas_call(
        body1, name="pool_bwd_map", grid=(N_POOL_GROUPS, nb), in_specs=[blk, blk, mat, vec, vec],
        out_specs=[blk, mat, vec, vec],
        out_shape=[jax.ShapeDtypeStruct(x.shape, F32), jax.ShapeDtypeStruct(w_grp.shape, F32),
                   jax.ShapeDtypeStruct((1, D_MODEL), F32), jax.ShapeDtypeStruct((1, D_MODEL), F32)],
        compiler_params=_params(("parallel", "arbitrary")))(dx1, diff, w_grp, b_grp, scale)

    def body2(dc_ref, dn_ref, dh_ref):
        gi, i = pl.program_id(0), pl.program_id(1)
        inside, before, inv_count = _pool_bands(gi, i, t)
        tn = (((0,), (0,)), ((), ()))
        dc = dc_ref[...]
        tot = lax.dot_general(inside, (dc * inv_count(i)).astype(MXU_DTYPE), tn, preferred_element_type=F32)
        nxt = lax.dot_general(before, (dn_ref[...] * inv_count(i + 1)).astype(MXU_DTYPE), tn, preferred_element_type=F32)
        dh_ref[...] = tot + jnp.where(i < nb - 1, nxt, 0.0) - dc

    dh = pl.pallas_call(
        body2, name="pool_bwd_window", grid=(N_POOL_GROUPS, nb),
        in_specs=[blk, pl.BlockSpec((t, POOL_WIDTH), lambda gi, i: (jnp.minimum(i + 1, nb - 1), gi))],
        out_specs=blk, out_shape=jax.ShapeDtypeStruct(x.shape, F32),
        compiler_params=_params(("parallel", "parallel")))(ddiff, ddiff)
    dx, dgain = _rms_bwd(x, gain, dh, dx1, "mix_norm_bwd")
    return dx, dgain, dw, db, dsc


MESH_ID = pl.DeviceIdType.MESH
ANY = pl.BlockSpec(memory_space=pl.ANY)


def _place():
    x, y, c = lax.axis_index("x"), lax.axis_index("y"), lax.axis_index("c")
    other_chips = [(1 - x, y), (x, 1 - y), (1 - x, 1 - y)]
    return x, y, c, other_chips


def _chip_index(chip):
    return 2 * chip[0] + chip[1]


def _remote(src, dst, send_sems, recv_sems, n, to):
    return pltpu.make_async_remote_copy(src_ref=src, dst_ref=dst, send_sem=send_sems.at[n], recv_sem=recv_sems.at[n],
                                        device_id=to, device_id_type=MESH_ID)


def _gather_weights(ws):
    n = len(ws)
    from_x, relay_x, from_y, relay_y, sib_x, sib_y, sib_d0, sib_d1, sib_own = range(9)
    slots = 9

    def body(*refs):
        w_refs, out_refs = refs[:n], refs[n:2 * n]
        send_sems, recv_sems = refs[2 * n:]
        x, y, c, (xn, yn, dn) = _place()
        k = _chip_index((x, y))
        me, sibling = (x, y, c), (x, y, 1 - c)

        def copy(ref, t, slot, to):
            return _remote(ref, ref, send_sems, recv_sems, slots * t + slot, to)

        def quarter(ref, q):
            rows = ref.shape[0] // 2
            return ref.at[pl.ds(q * rows, rows)]

        started = []

        def start(cp):
            cp.start()
            started.append(cp)

        for t in range(n):
            for slot, chip in ((from_x, xn), (from_y, yn)):
                start(_remote(w_refs[t].at[c], out_refs[t].at[k, c], send_sems, recv_sems, slots * t + slot, (*chip, c)))
            start(_remote(w_refs[t], out_refs[t].at[k], send_sems, recv_sems, slots * t + sib_own, sibling))
        for t in range(n):
            got = out_refs[t].at[_chip_index(xn), c]
            copy(got, t, from_x, me).wait_recv()
            start(copy(quarter(got, 0), t, relay_y, (*yn, c)))
            start(copy(got, t, sib_x, sibling))
        for t in range(n):
            got = out_refs[t].at[_chip_index(yn), c]
            copy(got, t, from_y, me).wait_recv()
            start(copy(quarter(got, 1), t, relay_x, (*xn, c)))
            start(copy(got, t, sib_y, sibling))
        for t in range(n):
            got = out_refs[t].at[_chip_index(dn), c]
            for q, slot, sib_slot in ((0, relay_y, sib_d0), (1, relay_x, sib_d1)):
                copy(quarter(got, q), t, slot, me).wait_recv()
                start(copy(quarter(got, q), t, sib_slot, sibling))
        for t in range(n):
            for chip, slot in ((xn, sib_x), (yn, sib_y)):
                copy(out_refs[t].at[_chip_index(chip), 1 - c], t, slot, me).wait_recv()
            theirs = out_refs[t].at[_chip_index(dn), 1 - c]
            copy(quarter(theirs, 0), t, sib_d0, me).wait_recv()
            copy(quarter(theirs, 1), t, sib_d1, me).wait_recv()
            copy(out_refs[t].at[k], t, sib_own, me).wait_recv()
        for cp in started:
            cp.wait_send()

    return pl.pallas_call(
        body, name="gather_weights", in_specs=[ANY] * n, out_specs=[ANY] * n,
        out_shape=[jax.ShapeDtypeStruct((N_CHIPS,) + w.shape, w.dtype) for w in ws],
        scratch_shapes=[pltpu.SemaphoreType.DMA((slots * n,)), pltpu.SemaphoreType.DMA((slots * n,))],
    )(*ws)


def _swap_halves(gs):
    n = len(gs)

    def body(*refs):
        g_refs, out_refs = refs[:n], refs[n:2 * n]
        send_sems, recv_sems = refs[2 * n:]
        x, y, c, _ = _place()
        copies = [_remote(g_refs[t].at[s, 1 - c], out_refs[t].at[s], send_sems, recv_sems, N_CHIPS * t + s, (x, y, 1 - c))
                  for t in range(n) for s in range(N_CHIPS)]
        for cp in copies:
            cp.start()
        for cp in copies:
            cp.wait_recv()
        for cp in copies:
            cp.wait_send()

    return pl.pallas_call(
        body, name="swap_halves", in_specs=[ANY] * n, out_specs=[ANY] * n,
        out_shape=[jax.ShapeDtypeStruct((N_CHIPS,) + g.shape[2:], g.dtype) for g in gs],
        scratch_shapes=[pltpu.SemaphoreType.DMA((N_CHIPS * n,)), pltpu.SemaphoreType.DMA((N_CHIPS * n,))],
    )(*gs)


def _scatter_partials(ps):
    n = len(ps)

    def body(*refs):
        p_refs, out_refs = refs[:n], refs[n:2 * n]
        send_sems, recv_sems = refs[2 * n:]
        x, y, c, chips = _place()
        sends = [_remote(p_refs[t].at[_chip_index(chip)], out_refs[t].at[j], send_sems, recv_sems, 3 * t + j, (*chip, c))
                 for j, chip in enumerate(chips) for t in range(n)]
        for cp in sends:
            cp.start()
        for j in range(3):
            for t in range(n):
                slot = out_refs[t].at[j]
                _remote(slot, slot, send_sems, recv_sems, 3 * t + j, (x, y, c)).wait_recv()
        for cp in sends:
            cp.wait_send()

    return pl.pallas_call(
        body, name="scatter_partials", in_specs=[ANY] * n, out_specs=[ANY] * n,
        out_shape=[jax.ShapeDtypeStruct((3,) + p.shape[1:], p.dtype) for p in ps],
        scratch_shapes=[pltpu.SemaphoreType.DMA((3 * n,)), pltpu.SemaphoreType.DMA((3 * n,))],
    )(*ps)


def _share_half(fs):
    n = len(fs)

    def body(*refs):
        f_refs, out_refs = refs[:n], refs[n:2 * n]
        send_sems, recv_sems = refs[2 * n:]
        x, y, c, _ = _place()
        sends = [_remote(f_refs[t], out_refs[t].at[c], send_sems, recv_sems, t, (x, y, 1 - c)) for t in range(n)]
        for cp in sends:
            cp.start()
        for t in range(n):
            theirs = out_refs[t].at[1 - c]
            _remote(theirs, theirs, send_sems, recv_sems, t, (x, y, c)).wait_recv()
        for cp in sends:
            cp.wait_send()

    return pl.pallas_call(
        body, name="share_half", in_specs=[ANY] * n, out_specs=[ANY] * n,
        out_shape=[jax.ShapeDtypeStruct((2,) + f.shape, f.dtype) for f in fs],
        scratch_shapes=[pltpu.SemaphoreType.DMA((n,)), pltpu.SemaphoreType.DMA((n,))],
    )(*fs)


def _gather_small(v):
    def body(v_ref, out_ref, send_sems, recv_sems):
        x, y, c, chips = _place()
        me, sibling = (x, y, c), (x, y, 1 - c)

        def slot(px, py, pc):
            return out_ref.at[4 * px + 2 * py + pc]

        first = [_remote(v_ref, slot(*me), send_sems, recv_sems, 0, sibling)]
        first += [_remote(v_ref, slot(*me), send_sems, recv_sems, 1 + j, (*chip, c)) for j, chip in enumerate(chips)]
        for cp in first:
            cp.start()
        passed = [_remote(slot(*chip, c), slot(*chip, c), send_sems, recv_sems, 4 + j, sibling)
                  for j, chip in enumerate(chips)]
        for j, chip in enumerate(chips):
            _remote(slot(*chip, c), slot(*chip, c), send_sems, recv_sems, 1 + j, me).wait_recv()
            passed[j].start()
        _remote(slot(*sibling), slot(*sibling), send_sems, recv_sems, 0, me).wait_recv()
        for j, chip in enumerate(chips):
            _remote(slot(*chip, 1 - c), slot(*chip, 1 - c), send_sems, recv_sems, 4 + j, me).wait_recv()
        for cp in first + passed:
            cp.wait_send()

    gathered = pl.pallas_call(
        body, name="gather_small", in_specs=[ANY], out_specs=ANY,
        out_shape=jax.ShapeDtypeStruct((N_DEV,) + v.shape, v.dtype),
        scratch_shapes=[pltpu.SemaphoreType.DMA((7,)), pltpu.SemaphoreType.DMA((7,))],
    )(v)
    device = 4 * lax.axis_index("x") + 2 * lax.axis_index("y") + lax.axis_index("c")
    return lax.dynamic_update_index_in_dim(gathered, v, device, 0)


SMALL_ROWS = 256
SUM_ROWS = 256
BF16_ROWS = 16


def _row_tile(rows, want):
    for t in range(min(rows, want) // BF16_ROWS * BF16_ROWS, 0, -BF16_ROWS):
        if rows % t == 0:
            return t
    return rows


def _pair_sum(g, r, core, name):
    rows, cols = g.shape[2:]
    tr = _row_tile(rows, SUM_ROWS)

    def body(c_ref, g_ref, r_ref, o_ref):
        o_ref[0] = (g_ref[0, 0].astype(F32) + r_ref[0].astype(F32)).astype(o_ref.dtype)

    return pl.pallas_call(
        body, name=name,
        grid_spec=pltpu.PrefetchScalarGridSpec(
            num_scalar_prefetch=1, grid=(N_CHIPS, rows // tr),
            in_specs=[pl.BlockSpec((1, 1, tr, cols), lambda s, i, c_ref: (s, c_ref[0], i, 0)),
                      pl.BlockSpec((1, tr, cols), lambda s, i, c_ref: (s, i, 0))],
            out_specs=pl.BlockSpec((1, tr, cols), lambda s, i, c_ref: (s, i, 0))),
        out_shape=jax.ShapeDtypeStruct(r.shape, BF16),
        compiler_params=_params(("parallel", "parallel")))(core, g, r)


def _chip_sum(p, recv, chip, name):
    rows, cols = p.shape[1:]
    tr = _row_tile(rows, SUM_ROWS)

    def body(k_ref, p_ref, r_ref, o_ref):
        acc = p_ref[0].astype(F32)
        for j in range(3):
            acc = acc + r_ref[j].astype(F32)
        o_ref[...] = acc

    return pl.pallas_call(
        body, name=name,
        grid_spec=pltpu.PrefetchScalarGridSpec(
            num_scalar_prefetch=1, grid=(rows // tr,),
            in_specs=[pl.BlockSpec((1, tr, cols), lambda i, k_ref: (k_ref[0], i, 0)),
                      pl.BlockSpec((3, tr, cols), lambda i, k_ref: (0, i, 0))],
            out_specs=pl.BlockSpec((tr, cols), lambda i, k_ref: (i, 0))),
        out_shape=jax.ShapeDtypeStruct((rows, cols), F32),
        compiler_params=_params(("parallel",)))(chip, p, recv)


def _sum_blocks(a, name):
    n, rows, cols = a.shape
    tr = _row_tile(rows, SUM_ROWS)

    def body(a_ref, o_ref):
        acc = a_ref[0].astype(F32)
        for s in range(1, n):
            acc = acc + a_ref[s].astype(F32)
        o_ref[...] = acc

    return pl.pallas_call(
        body, name=name, grid=(rows // tr,),
        in_specs=[pl.BlockSpec((n, tr, cols), lambda i: (0, i, 0))],
        out_specs=pl.BlockSpec((tr, cols), lambda i: (i, 0)),
        out_shape=jax.ShapeDtypeStruct((rows, cols), F32),
        compiler_params=_params(("parallel",)))(a)


def _adamw(w, g, m, v, name):
    def fn(wb, gb, mb, vb):
        m2 = ADAM_B1 * mb + (1.0 - ADAM_B1) * gb
        v2 = ADAM_B2 * vb + (1.0 - ADAM_B2) * (gb * gb)
        m_hat = m2 / (1.0 - ADAM_B1 ** ADAM_STEP)
        v_hat = v2 / (1.0 - ADAM_B2 ** ADAM_STEP)
        delta = -ADAM_LR * (m_hat / (jnp.sqrt(v_hat) + ADAM_EPS) + ADAM_WD * wb)
        return (delta, m2, v2), ()
    c = w.shape[1]
    return _ew(fn, [w, g, m, v], outs=[(c, F32)] * 3, name=name)


WEIGHTS = ["mix_norm_g", "ffn_norm_g", "final_norm_g", "fox_w_in", "fox_b_f", "fox_w_out", "s5_w_in", "s5_a_re",
           "s5_a_im", "s5_log_dt", "s5_b_re", "s5_b_im", "s5_c_re", "s5_c_im", "s5_d", "s5_w_glu", "pool_w",
           "pool_b", "pool_scale", "ffn_w_gate_up", "ffn_w_down"]
BIG = {"fox_w_in": 2, "fox_w_out": 1, "s5_w_in": 1, "s5_w_glu": 2, "pool_w": 2, "ffn_w_gate_up": 2, "ffn_w_down": 1}
BY_CHIP = ("ffn_w_gate_up", "ffn_w_down")
SLICED = ("pool_b", "pool_scale")
SMALL = [n for n in WEIGHTS if n not in BIG]


def _to_natural(cm, axis):
    moved = jnp.moveaxis(cm, 0, axis)
    shape = moved.shape[:axis] + (moved.shape[axis] * moved.shape[axis + 1],) + moved.shape[axis + 2:]
    return moved.reshape(shape)


def _to_chip_major(nat, axis):
    shape = nat.shape[:axis] + (N_CHIPS, nat.shape[axis] // N_CHIPS) + nat.shape[axis + 1:]
    return jnp.moveaxis(nat.reshape(shape), axis, 0)


def _halves_view(shape):
    return (2, int(np.prod(shape[:-1])) // 2, shape[-1])


def _pack_small(parts):
    flat = jnp.concatenate([p.reshape(-1).astype(F32) for p in parts])
    pad = (-flat.shape[0]) % (SMALL_ROWS * LANES)
    return jnp.pad(flat, (0, pad)).reshape(-1, LANES)


def _unpack_small(buf, shapes):
    flat = buf.reshape(-1)
    out, off = [], 0
    for shp in shapes:
        n = int(np.prod(shp))
        out.append(flat[off:off + n].reshape(shp))
        off += n
    return out


def _local_step(x, target, w):
    grads = {}
    mixers = ("fox", "s5", "pool")
    saved = []
    ssm, ssm_pull = jax.vjp(_s5_operands, w["s5_a_re"][0], w["s5_a_im"][0], w["s5_log_dt"][0], w["s5_b_re"][0],
                            w["s5_b_im"][0], w["s5_c_re"][0], w["s5_c_im"][0])
    fox_w = []
    for j in range(w["fox_w_in"].shape[0]):
        w_in = w["fox_w_in"][j]
        w_f = jnp.pad(w_in[:, 3 * D_MODEL:], ((0, 0), (0, LANES - FOX_HEADS)))
        fox_w.append((w_in[:, :3 * D_MODEL], w_f, w["fox_w_out"][j]))
    for i in range(DEPTH):
        kind, j = mixers[i % 3], i // 3
        gain = w["mix_norm_g"][i]
        if kind == "fox":
            x1, sv = _fox_fwd(x, gain, fox_w[j][0], fox_w[j][1], w["fox_b_f"][j], fox_w[j][2])
        elif kind == "s5":
            x1, sv = _s5_fwd(x, gain, w["s5_w_in"][j], ssm, w["s5_d"], w["s5_w_glu"][j])
        else:
            x1, sv = _pool_fwd(x, gain, w["pool_w"][j], w["pool_b"], w["pool_scale"])
        x, sf = _ffn_fwd(x1, w["ffn_norm_g"][i], w["ffn_w_gate_up"], w["ffn_w_down"], i)
        saved.append((sv, sf))
    loss, dx, grads["final_norm_g"] = _loss_head(x, w["final_norm_g"], target)
    per_layer = {n: [None] * DEPTH for n in ("mix_norm_g", "ffn_norm_g")}
    fox_g = {n: [None] * len(fox_w) for n in ("fox_w_in", "fox_b_f", "fox_w_out")}
    for n in BY_CHIP:
        grads[n] = lax.empty(w[n].shape, BF16)
    for i in reversed(range(DEPTH)):
        kind, j = mixers[i % 3], i // 3
        sv, sf = saved[i]
        dx, per_layer["ffn_norm_g"][i], grads["ffn_w_gate_up"], grads["ffn_w_down"] = _ffn_bwd(
            dx, sf, w["ffn_norm_g"][i], w["ffn_w_gate_up"], w["ffn_w_down"], i, grads["ffn_w_gate_up"], grads["ffn_w_down"])
        gain = w["mix_norm_g"][i]
        if kind == "fox":
            dx, per_layer["mix_norm_g"][i], fox_g["fox_w_in"][j], fox_g["fox_b_f"][j], fox_g["fox_w_out"][j] = _fox_bwd(
                dx, sv, gain, fox_w[j][0], fox_w[j][1], fox_w[j][2])
        elif kind == "s5":
            dx, per_layer["mix_norm_g"][i], dw_in, dssm, dd, dw_glu = _s5_bwd(
                dx, sv, gain, w["s5_w_in"][j], ssm, w["s5_d"], w["s5_w_glu"][j])
            grads["s5_w_in"], grads["s5_w_glu"], grads["s5_d"] = dw_in[None], dw_glu[None], dd
            for n, g in zip(("s5_a_re", "s5_a_im", "s5_log_dt", "s5_b_re", "s5_b_im", "s5_c_re", "s5_c_im"), ssm_pull(dssm)):
                grads[n] = g[None]
        else:
            dx, per_layer["mix_norm_g"][i], dw, db, dsc = _pool_bwd(dx, sv, gain, w["pool_w"][j], w["pool_b"], w["pool_scale"])
            grads["pool_w"], grads["pool_b"], grads["pool_scale"] = dw[None].astype(BF16), db, dsc
    for n, parts in {**per_layer, **fox_g}.items():
        grads[n] = jnp.stack(parts)
    return loss, dx, grads


def kernel(x, mix_norm_g, ffn_norm_g, final_norm_g, fox_w_in, fox_b_f, fox_w_out, s5_w_in, s5_a_re, s5_a_im, s5_log_dt, s5_b_re, s5_b_im, s5_c_re, s5_c_im, s5_d, s5_w_glu, pool_w, pool_b, pool_scale, ffn_w_gate_up, ffn_w_down, loss_target, m_mix_norm_g, m_ffn_norm_g, m_final_norm_g, m_fox_w_in, m_fox_b_f, m_fox_w_out, m_s5_w_in, m_s5_a_re, m_s5_a_im, m_s5_log_dt, m_s5_b_re, m_s5_b_im, m_s5_c_re, m_s5_c_im, m_s5_d, m_s5_w_glu, m_pool_w, m_pool_b, m_pool_scale, m_ffn_w_gate_up, m_ffn_w_down, v_mix_norm_g, v_ffn_norm_g, v_final_norm_g, v_fox_w_in, v_fox_b_f, v_fox_w_out, v_s5_w_in, v_s5_a_re, v_s5_a_im, v_s5_log_dt, v_s5_b_re, v_s5_b_im, v_s5_c_re, v_s5_c_im, v_s5_d, v_s5_w_glu, v_pool_w, v_pool_b, v_pool_scale, v_ffn_w_gate_up, v_ffn_w_down):
    given = dict(locals())
    shard = {n: given[n] for n in WEIGHTS}
    chip = 2 * lax.axis_index("x") + lax.axis_index("y")
    core = lax.axis_index("c")

    views = {n: _halves_view(shard[n].shape) for n in BIG}
    own = [shard[n].astype(MXU_DTYPE).reshape(views[n]) for n in BIG]
    whole = {}
    for n, by_chip in zip(BIG, _gather_weights(own)):
        by_chip = by_chip.reshape((N_CHIPS,) + shard[n].shape)
        whole[n] = by_chip if n in BY_CHIP else _to_natural(by_chip, BIG[n])
    for n in SMALL:
        whole[n] = shard[n]
    sliced_shapes = [shard[n].shape for n in SLICED]
    by_chip = _gather_small(_pack_small([shard[n] for n in SLICED]))[0::2]
    slices = [_unpack_small(by_chip[k], sliced_shapes) for k in range(N_CHIPS)]
    for idx, n in enumerate(SLICED):
        whole[n] = jnp.concatenate([slices[k][idx] for k in range(N_CHIPS)], axis=-1)

    loss_part, dx, grads = _local_step(x[0], loss_target[0], whole)
    loss = lax.psum(loss_part, MESH_AXES)

    gs = [(grads[n] if n in BY_CHIP else _to_chip_major(grads[n].astype(BF16), BIG[n])).reshape((N_CHIPS,) + views[n])
          for n in BIG]
    core_id, chip_id = core.reshape(1).astype(jnp.int32), chip.reshape(1).astype(jnp.int32)
    partial = [_pair_sum(g, r, core_id, "pair_sum_" + n) for n, g, r in zip(BIG, gs, _swap_halves(gs))]
    half = [_chip_sum(p, r, chip_id, "chip_sum_" + n) for n, p, r in zip(BIG, partial, _scatter_partials(partial))]
    grad = {n: lax.dynamic_update_index_in_dim(both, mine, core, 0).reshape(shard[n].shape)
            for n, mine, both in zip(BIG, half, _share_half(half))}

    small_sum = _sum_blocks(_gather_small(_pack_small([grads[n] for n in SMALL])), "small_sum")
    for n, g in zip(SMALL, _unpack_small(small_sum, [whole[n].shape for n in SMALL])):
        grad[n] = g
    for n in SLICED:
        width = shard[n].shape[-1]
        grad[n] = lax.dynamic_slice_in_dim(grad[n], chip * width, width, axis=-1)

    delta, new_m, new_v = {}, {}, {}
    for n in BIG:
        view = (-1, shard[n].shape[-1])
        res = _adamw(shard[n].reshape(view), grad[n].reshape(view), given["m_" + n].reshape(view),
                     given["v_" + n].reshape(view), "adamw_" + n)
        delta[n], new_m[n], new_v[n] = (r.reshape(shard[n].shape) for r in res)
    small_shapes = [shard[n].shape for n in SMALL]
    res = _adamw(_pack_small([shard[n] for n in SMALL]), _pack_small([grad[n] for n in SMALL]),
                 _pack_small([given["m_" + n] for n in SMALL]), _pack_small([given["v_" + n] for n in SMALL]), "adamw_small")
    for out, buf in zip((delta, new_m, new_v), res):
        for n, a in zip(SMALL, _unpack_small(buf, small_shapes)):
            out[n] = a
    return (loss, dx[None], *[grad[n] for n in WEIGHTS], *[delta[n] for n in WEIGHTS],
            *[new_m[n] for n in WEIGHTS], *[new_v[n] for n in WEIGHTS])
```

```python
import functools
import math

import jax
import jax.numpy as jnp
import numpy as np
from jax import lax
from jax.experimental import pallas as pl
from jax.experimental.pallas import tpu as pltpu

F32 = jnp.float32
BF16 = jnp.bfloat16
MXU_DTYPE = jnp.bfloat16

D_MODEL = 1024
DEPTH = 4
EPS = 1e-6
FOX_HEADS = 16
FOX_HEAD_DIM = 64
HEAD_PAIRS = FOX_HEADS // 2
S5_GROUPS = 64
S5_GROUP = 16
S5_STATE = 64
S5_BLOCKS = 8
S5_HALF = 256
POOL_WINDOWS = (2, 4, 8, 16)
POOL_WIDTH = 256
D_FF = 2816
N_CHIPS = 4
N_DEV = 8
LANES = 128
SUBLANES = 8
VMEM_LIMIT = 56 * 1024 * 1024

ADAM_LR = 0.001
ADAM_B1 = 0.9
ADAM_B2 = 0.999
ADAM_EPS = 1e-08
ADAM_WD = 0.01
ADAM_STEP = 10

MESH_AXES = ("x", "y", "c")


def _tile(n, want):
    t = (min(n, want) // LANES) * LANES
    while t >= LANES:
        if n % t == 0:
            return t
        t -= LANES
    return n


def _params(sem=None):
    return pltpu.CompilerParams(dimension_semantics=sem, vmem_limit_bytes=VMEM_LIMIT)


def _mm(a, b, *, name, ta=False, tb=False, out_dtype=F32, add=None, tm=1024, tn=1024, tk=1024,
        b_tiles=None, out_tiles=None, into=None):
    m, k = (a.shape[1], a.shape[0]) if ta else a.shape
    if b_tiles is None:
        n = b.shape[0] if tb else b.shape[1]
        assert (b.shape[1] if tb else b.shape[0]) == k, (a.shape, b.shape, ta, tb)
    else:
        assert b_tiles[0] == k, (a.shape, b_tiles[0])
        n = b_tiles[1]
    tm, tn, tk = _tile(m, tm), _tile(n, tn), _tile(k, tk)
    nk = k // tk
    a_spec = pl.BlockSpec((tk, tm), lambda i, j, kk: (kk, i)) if ta else pl.BlockSpec((tm, tk), lambda i, j, kk: (i, kk))
    b_shape = (tn, tk) if tb else (tk, tn)
    if b_tiles is None:
        b_spec = pl.BlockSpec(b_shape, (lambda i, j, kk: (j, kk)) if tb else (lambda i, j, kk: (kk, j)))
    else:
        b_spec = pl.BlockSpec(b_tiles[2], b_tiles[3])
    add_spec = pl.BlockSpec((tm, tn), lambda i, j, kk: (i, j))
    if out_tiles is None:
        o_spec, o_struct = add_spec, jax.ShapeDtypeStruct((m, n), out_dtype)
    else:
        o_spec, o_struct = pl.BlockSpec(out_tiles[1], out_tiles[2]), jax.ShapeDtypeStruct(out_tiles[0], out_dtype)
    dims = (((0 if ta else 1,), (1 if tb else 0,)), ((), ()))
    has_add, has_into = add is not None, into is not None

    def body(*refs):
        a_ref, b_ref = refs[:2]
        add_ref = refs[2] if has_add else None
        o_ref, acc_ref = refs[-2:]
        kk = pl.program_id(2)

        @pl.when(kk == 0)
        def _():
            acc_ref[...] = jnp.zeros_like(acc_ref)

        acc_ref[...] += lax.dot_general(a_ref[...].astype(MXU_DTYPE), b_ref[...].reshape(b_shape).astype(MXU_DTYPE), dims,
                                        preferred_element_type=F32)

        @pl.when(kk == nk - 1)
        def _():
            r = acc_ref[...]
            if has_add:
                r = r + add_ref[...].astype(F32)
            o_ref[...] = r.astype(out_dtype).reshape(o_ref.shape)

    ins = [a, b] + ([add] if has_add else []) + ([into] if has_into else [])
    specs = [a_spec, b_spec] + ([add_spec] if has_add else []) + ([pl.BlockSpec(memory_space=pl.ANY)] if has_into else [])
    return pl.pallas_call(
        body, name=name, grid=(m // tm, n // tn, nk), in_specs=specs, out_specs=o_spec,
        out_shape=o_struct, scratch_shapes=[pltpu.VMEM((tm, tn), F32)],
        input_output_aliases={len(ins) - 1: 0} if has_into else {},
        compiler_params=_params(("parallel", "parallel", "arbitrary")))(*ins)


def _ew(fn, tens, vecs=(), *, outs=(), sums=(), name, tr=256):
    tens = [t if isinstance(t, tuple) else (t, t.shape[1], 0) for t in tens]
    rows = tens[0][0].shape[0]
    tr = min(tr, rows)
    n_t, n_v, n_o, n_s = len(tens), len(vecs), len(outs), len(sums)

    def body(*refs):
        i = pl.program_id(0)
        t_blocks = [r[...] for r in refs[:n_t]]
        v_blocks = [r[...] for r in refs[n_t:n_t + n_v]]
        o_refs = refs[n_t + n_v:n_t + n_v + n_o]
        s_refs = refs[n_t + n_v + n_o:]
        o_vals, s_vals = fn(*t_blocks, *v_blocks)
        for r, v in zip(o_refs, o_vals):
            r[...] = v.astype(r.dtype)
        if n_s:
            @pl.when(i == 0)
            def _():
                for r in s_refs:
                    r[...] = jnp.zeros_like(r)
            for r, v in zip(s_refs, s_vals):
                r[...] += jnp.sum(v.astype(F32), axis=0, keepdims=True)

    in_specs = [pl.BlockSpec((tr, w), functools.partial(lambda i, cb: (i, cb), cb=cb)) for _, w, cb in tens]
    in_specs += [pl.BlockSpec(v.shape, functools.partial(lambda i, nd: (0,) * nd, nd=v.ndim)) for v in vecs]
    out_specs = [pl.BlockSpec((tr, c), lambda i: (i, 0)) for c, _ in outs]
    out_specs += [pl.BlockSpec((1, c), lambda i: (0, 0)) for c in sums]
    out_shape = [jax.ShapeDtypeStruct((rows, c), dt) for c, dt in outs]
    out_shape += [jax.ShapeDtypeStruct((1, c), F32) for c in sums]
    res = pl.pallas_call(
        body, name=name, grid=(rows // tr,), in_specs=in_specs, out_specs=out_specs, out_shape=out_shape,
        compiler_params=_params(("arbitrary",)))(*[t[0] for t in tens], *vecs)
    return res


def _sigmoid(z):
    return 1.0 / (1.0 + jnp.exp(-z))


def _rms_fwd(x, g, name):
    def fn(xb, gb):
        r = lax.rsqrt(jnp.mean(xb * xb, axis=-1, keepdims=True) + EPS)
        return ((xb * r) * gb,), ()
    return _ew(fn, [x], [g.reshape(1, -1)], outs=[(x.shape[1], BF16)], name=name)[0]


def _rms_bwd(x, g, dh, dres, name):
    def fn(xb, dhb, drb, gb):
        r = lax.rsqrt(jnp.mean(xb * xb, axis=-1, keepdims=True) + EPS)
        xh = xb * r
        dhf = dhb.astype(F32)
        dy = dhf * gb
        dx = r * (dy - xh * jnp.mean(dy * xh, axis=-1, keepdims=True))
        return (drb + dx,), (dhf * xh,)
    dx, dg = _ew(fn, [x, dh, dres], [g.reshape(1, -1)], outs=[(x.shape[1], F32)], sums=[x.shape[1]], name=name)
    return dx, dg[0]


def _mm_rms_bwd(a, b, x, g, dres, *, name, add=None, tm=512, tk=1024, b_tiles=None):
    m, k = a.shape
    n = x.shape[1]
    assert x.shape == (m, n) and (b_tiles is not None or b.shape == (n, k))
    tm, tk = _tile(m, tm), _tile(k, tk)
    nk = k // tk
    has_add = add is not None

    def body(*refs):
        a_ref, b_ref, x_ref, dr_ref, g_ref = refs[:5]
        add_ref = refs[5] if has_add else None
        dx_ref, dg_ref, acc_ref = refs[-3:]
        i, kk = pl.program_id(0), pl.program_id(1)

        @pl.when(kk == 0)
        def _():
            acc_ref[...] = jnp.zeros_like(acc_ref)

        @pl.when((kk == 0) & (i == 0))
        def _():
            dg_ref[...] = jnp.zeros_like(dg_ref)

        acc_ref[...] += lax.dot_general(a_ref[...].astype(MXU_DTYPE), b_ref[...].reshape((n, tk)).astype(MXU_DTYPE),
                                        (((1,), (1,)), ((), ())), preferred_element_type=F32)

        @pl.when(kk == nk - 1)
        def _():
            dh = acc_ref[...]
            if has_add:
                dh = dh + add_ref[...]
            xb = x_ref[...]
            r = lax.rsqrt(jnp.mean(xb * xb, axis=-1, keepdims=True) + EPS)
            xh = xb * r
            dy = dh * g_ref[...]
            dx_ref[...] = dr_ref[...] + r * (dy - xh * jnp.mean(dy * xh, axis=-1, keepdims=True))
            dg_ref[...] += jnp.sum(dh * xh, axis=0, keepdims=True)

    row = pl.BlockSpec((tm, n), lambda i, kk: (i, 0))
    vec = pl.BlockSpec((1, n), lambda i, kk: (0, 0))
    ins = [a, b, x, dres, g.reshape(1, n)] + ([add] if has_add else [])
    b_spec = pl.BlockSpec((n, tk), lambda i, kk: (0, kk)) if b_tiles is None else pl.BlockSpec(*b_tiles)
    specs = [pl.BlockSpec((tm, tk), lambda i, kk: (i, kk)), b_spec, row, row, vec]
    specs += [row] if has_add else []
    dx, dg = pl.pallas_call(
        body, name=name, grid=(m // tm, nk), in_specs=specs, out_specs=[row, vec],
        out_shape=[jax.ShapeDtypeStruct((m, n), F32), jax.ShapeDtypeStruct((1, n), F32)],
        scratch_shapes=[pltpu.VMEM((tm, n), F32)],
        compiler_params=_params(("arbitrary", "arbitrary")))(*ins)
    return dx, dg[0]


FFN_ROWS = 512
FFN_COLS = D_FF // 2


def _ffn_gate_up(x, gain, w_gu, layer):
    s, d = x.shape
    tm = min(FFN_ROWS, s)
    halves = D_FF // FFN_COLS

    def body(x_ref, gain_ref, wg_ref, wu_ref, h_ref, g_ref, u_ref, a_ref):
        @pl.when(pl.program_id(1) == 0)
        def _():
            xb = x_ref[...]
            h_ref[...] = ((xb * lax.rsqrt(jnp.mean(xb * xb, axis=-1, keepdims=True) + EPS)) * gain_ref[...]).astype(h_ref.dtype)

        hb = h_ref[...].astype(MXU_DTYPE)
        g = jnp.dot(hb, wg_ref[0, 0], preferred_element_type=F32)
        u = jnp.dot(hb, wu_ref[0, 0], preferred_element_type=F32)
        g_ref[...] = g.astype(g_ref.dtype)
        u_ref[...] = u.astype(u_ref.dtype)
        a_ref[...] = (g * _sigmoid(g) * u).astype(a_ref.dtype)

    row = pl.BlockSpec((tm, d), lambda i, jj: (i, 0))
    tile = pl.BlockSpec((tm, FFN_COLS), lambda i, jj: (i, jj))
    return pl.pallas_call(
        body, name="ffn_gate_up", grid=(s // tm, halves),
        in_specs=[row, pl.BlockSpec((1, d), lambda i, jj: (0, 0)),
                  pl.BlockSpec((1, 1, d, FFN_COLS), lambda i, jj: (jj, layer, 0, 0)),
                  pl.BlockSpec((1, 1, d, FFN_COLS), lambda i, jj: (halves + jj, layer, 0, 0))],
        out_specs=[row, tile, tile, tile],
        out_shape=[jax.ShapeDtypeStruct((s, d), BF16)] + [jax.ShapeDtypeStruct((s, D_FF), BF16)] * 3,
        compiler_params=_params(("parallel", "arbitrary")))(x, gain.reshape(1, d), w_gu, w_gu)


def _ffn_dgate_up(dx2, w_down, layer, g, u):
    s, d = dx2.shape
    tm = min(FFN_ROWS, s)
    halves = D_FF // FFN_COLS

    def body(dx_ref, w_ref, g_ref, u_ref, o_ref):
        jj = pl.program_id(1)
        df = lax.dot_general(dx_ref[...].astype(MXU_DTYPE), w_ref[...].reshape((FFN_COLS, d)), (((1,), (1,)), ((), ())),
                             preferred_element_type=F32)
        gf, uf = g_ref[...].astype(F32), u_ref[...].astype(F32)
        sg = _sigmoid(gf)
        dg = df * uf * (sg * (1.0 + gf * (1.0 - sg)))
        du = df * (gf * sg)
        o_ref[:, pl.ds(pl.multiple_of(jj * FFN_COLS, LANES), FFN_COLS)] = dg.astype(o_ref.dtype)
        o_ref[:, pl.ds(pl.multiple_of(D_FF + jj * FFN_COLS, LANES), FFN_COLS)] = du.astype(o_ref.dtype)

    tile = pl.BlockSpec((tm, FFN_COLS), lambda i, jj: (i, jj))
    return pl.pallas_call(
        body, name="ffn_dgate_up", grid=(s // tm, halves),
        in_specs=[pl.BlockSpec((tm, d), lambda i, jj: (i, 0)),
                  pl.BlockSpec((2, 1, FFN_COLS // 2, d), lambda i, jj: (jj, layer, 0, 0)), tile, tile],
        out_specs=pl.BlockSpec((tm, 2 * D_FF), lambda i, jj: (i, 0)),
        out_shape=jax.ShapeDtypeStruct((s, 2 * D_FF), BF16),
        compiler_params=_params(("parallel", "arbitrary")))(dx2, w_down, g, u)


def _ffn_fwd(x1, gain, w_gu, w_down, layer):
    d = x1.shape[1]
    h, g, u, act = _ffn_gate_up(x1, gain, w_gu, layer)
    x2 = _mm(act, w_down, name="ffn_down", add=x1, tk=FFN_COLS,
             b_tiles=(D_FF, d, (2, 1, FFN_COLS // 2, d), lambda i, j, kk: (kk, layer, 0, 0)))
    return x2, (x1, h, g, u, act)


def _ffn_bwd(dx2, saved, gain, w_gu, w_down, layer, dw_gu, dw_down):
    x1, h, g, u, act = saved
    d = x1.shape[1]
    dw_down = _mm(act, dx2, ta=True, name="ffn_dw_down", out_dtype=BF16, tm=FFN_COLS, into=dw_down,
                  out_tiles=(dw_down.shape, (2, 1, FFN_COLS // 2, d), lambda i, j, kk: (i, layer, 0, 0)))
    dgu = _ffn_dgate_up(dx2, w_down, layer, g, u)
    dw_gu = _mm(h, dgu, ta=True, name="ffn_dw_gu", out_dtype=BF16, tn=FFN_COLS, into=dw_gu,
                out_tiles=(dw_gu.shape, (1, 1, d, FFN_COLS), lambda i, j, kk: (j, layer, i, 0)))
    dx1, dgain = _mm_rms_bwd(dgu, w_gu, x1, gain, dx2, name="ffn_dh", tm=1024, tk=FFN_COLS,
                             b_tiles=((1, 1, d, FFN_COLS), lambda i, kk: (kk, layer, 0, 0)))
    return dx1, dgain, dw_gu, dw_down


def _loss_head(x, gain, target):
    d = x.shape[1]

    def fn(xb, tb, gb):
        r = lax.rsqrt(jnp.mean(xb * xb, axis=-1, keepdims=True) + EPS)
        xh = xb * r
        y = xh * gb
        err = y - tb
        dyv = err * (1.0 / d)
        dyg = dyv * gb
        dx = r * (dyg - xh * jnp.mean(dyg * xh, axis=-1, keepdims=True))
        return (dx,), (0.5 * err * err * (1.0 / d), dyv * xh)
    dx, lsum, dg = _ew(fn, [x, target], [gain.reshape(1, -1)], outs=[(d, F32)], sums=[d, d], name="loss_head")
    return jnp.sum(lsum), dx, dg[0]


ATT_BLOCK = 256
CUM_BLOCK = 512
NEG_INF = -1e30


def _fox_gate_fwd(fl_row, b_col):
    nh, s = fl_row.shape
    tb = min(CUM_BLOCK, s)

    def body(fl_ref, b_ref, z_ref, c_ref):
        upper = (lax.broadcasted_iota(jnp.int32, (tb, tb), 0) <= lax.broadcasted_iota(jnp.int32, (tb, tb), 1)).astype(F32)
        carry = jnp.zeros((nh, 1), F32)
        for blk in range(s // tb):
            z = fl_ref[:, blk * tb:(blk + 1) * tb] + b_ref[...]
            logf = jnp.minimum(z, 0.0) - jnp.log(1.0 + jnp.exp(-jnp.abs(z)))
            cs = jnp.dot(logf, upper, precision=lax.Precision.HIGHEST, preferred_element_type=F32) + carry
            z_ref[:, blk * tb:(blk + 1) * tb] = z
            c_ref[:, blk * tb:(blk + 1) * tb] = cs
            carry = cs[:, tb - 1:tb]

    return pl.pallas_call(body, name="fox_gate_fwd", out_shape=[jax.ShapeDtypeStruct((nh, s), F32)] * 2,
                          compiler_params=_params())(fl_row, b_col)


def _fox_gate_bwd(dc_row, z_row):
    nh, s = dc_row.shape
    tb = min(CUM_BLOCK, s)

    def body(dc_ref, z_ref, dz_ref, db_ref):
        lower = (lax.broadcasted_iota(jnp.int32, (tb, tb), 0) >= lax.broadcasted_iota(jnp.int32, (tb, tb), 1)).astype(F32)
        carry = jnp.zeros((nh, 1), F32)
        db = jnp.zeros((nh, 1), F32)
        for blk in reversed(range(s // tb)):
            dc = dc_ref[:, blk * tb:(blk + 1) * tb]
            rs = jnp.dot(dc, lower, precision=lax.Precision.HIGHEST, preferred_element_type=F32) + carry
            dz = rs * _sigmoid(-z_ref[:, blk * tb:(blk + 1) * tb])
            dz_ref[:, blk * tb:(blk + 1) * tb] = dz
            db = db + jnp.sum(dz, axis=1, keepdims=True)
            carry = rs[:, 0:1]
        db_ref[...] = db

    return pl.pallas_call(body, name="fox_gate_bwd",
                          out_shape=[jax.ShapeDtypeStruct((nh, s), F32), jax.ShapeDtypeStruct((nh, 1), F32)],
                          compiler_params=_params())(dc_row, z_row)


def _head_masks(rows):
    lane = lax.broadcasted_iota(jnp.int32, (rows, LANES), 1)
    return lane < FOX_HEAD_DIM


ATT_QUERIES = 512


def _lanes(a, width):
    return jnp.concatenate([a] * (width // LANES), axis=1)


def _attn_fwd_t(qkv, c_lanes):
    s = qkv.shape[0]
    t, tq = min(ATT_BLOCK, s), min(2 * ATT_QUERIES, s)
    nq, per = s // tq, tq // t
    scale = FOX_HEAD_DIM ** -0.5
    np_ = HEAD_PAIRS
    nt = (((1,), (1,)), ((), ()))

    def body(q_ref, k_ref, v_ref, c_ref, o_ref, lse_ref, vt_ref):
        i = pl.program_id(1)

        @pl.when(i == 0)
        def _():
            for r in range(s // t):
                vt_ref[:, r * t:(r + 1) * t] = v_ref[r * t:(r + 1) * t, :].astype(F32).T.astype(vt_ref.dtype)

        first = _head_masks(tq)
        upper = lax.broadcasted_iota(jnp.int32, (LANES, tq), 0) < FOX_HEAD_DIM
        qs = q_ref[...] * scale
        zero = jnp.zeros_like(qs)
        qh = (jnp.where(first, qs, zero), jnp.where(first, zero, qs))
        key_at = lax.broadcasted_iota(jnp.int32, (t, tq), 0)
        query_at = lax.broadcasted_iota(jnp.int32, (t, tq), 1)

        def block(j, carry, diagonal=None):
            ms, ls, acc = carry
            start = pl.multiple_of(j * t, t)
            kb, vt = k_ref[pl.ds(start, t), :], vt_ref[:, pl.ds(start, t)]
            new_m, new_l, alphas, pvs = [], [], [], []
            for hh in range(2):
                sc = lax.dot_general(kb, qh[hh], nt, preferred_element_type=F32) - _lanes(c_ref[0, hh, pl.ds(start, t), :], tq)
                if diagonal is not None:
                    sc = jnp.where(key_at + diagonal * t <= query_at, sc, NEG_INF)
                m_new = jnp.maximum(ms[hh], jnp.max(sc, axis=0, keepdims=True))
                p = jnp.exp(sc - m_new)
                alpha = jnp.exp(ms[hh] - m_new)
                new_m.append(m_new)
                new_l.append(alpha * ls[hh] + jnp.sum(p, axis=0, keepdims=True))
                alphas.append(alpha)
                p_hi = p.astype(MXU_DTYPE)
                p_lo = (p - p_hi.astype(F32)).astype(MXU_DTYPE)
                pvs.append(jnp.dot(vt, p_hi, preferred_element_type=F32) + jnp.dot(vt, p_lo, preferred_element_type=F32))
            acc = jnp.where(upper, alphas[0], alphas[1]) * acc + jnp.where(upper, pvs[0], pvs[1])
            return tuple(new_m), tuple(new_l), acc

        neg, nil = jnp.full((1, tq), NEG_INF, F32), jnp.zeros((1, tq), F32)
        carry = lax.fori_loop(0, per * i, block, ((neg, neg), (nil, nil), jnp.zeros((LANES, tq), F32)))
        for d in range(per):
            carry = block(per * i + d, carry, diagonal=d)
        ms, ls, acc = carry
        o_ref[...] = (acc / jnp.where(upper, ls[0], ls[1])).T.astype(o_ref.dtype)
        lse_ref[0] = jnp.concatenate([ms[0] + jnp.log(ls[0]), ms[1] + jnp.log(ls[1])], axis=0)

    return pl.pallas_call(
        body, name="fox_attn_fwd", grid=(np_, nq),
        in_specs=[pl.BlockSpec((tq, LANES), lambda p, i: (i, p)),
                  pl.BlockSpec((s, LANES), lambda p, i: (0, np_ + p)),
                  pl.BlockSpec((s, LANES), lambda p, i: (0, 2 * np_ + p)),
                  pl.BlockSpec((1, 2, s, LANES), lambda p, i: (p, 0, 0, 0))],
        out_specs=[pl.BlockSpec((tq, LANES), lambda p, i: (i, p)),
                   pl.BlockSpec((1, 2, tq), lambda p, i: (p, 0, i))],
        out_shape=[jax.ShapeDtypeStruct((s, D_MODEL), F32), jax.ShapeDtypeStruct((np_, 2, s), F32)],
        scratch_shapes=[pltpu.VMEM((LANES, s), MXU_DTYPE)],
        compiler_params=_params(("parallel", "arbitrary")))(qkv, qkv, qkv, c_lanes)


def _attn_bwd_t(qkv, do, lse, delta, c_lanes):
    s = qkv.shape[0]
    t, tq = min(ATT_BLOCK, s), min(ATT_QUERIES, s)
    nb, nq, per = s // t, s // tq, tq // t
    scale = FOX_HEAD_DIM ** -0.5
    np_ = HEAD_PAIRS
    nt = (((1,), (1,)), ((), ()))

    def body(q_ref, k_ref, v_ref, do_ref, lse_ref, dl_ref, c_ref, dq_ref, dk_ref, dv_ref, dc_ref, dqt_ref):
        j = pl.program_id(1)
        first_q, first = _head_masks(tq), _head_masks(t)
        upper = lax.broadcasted_iota(jnp.int32, (LANES, tq), 0) < FOX_HEAD_DIM
        causal = (lax.broadcasted_iota(jnp.int32, (t, tq), 0) + (j % per) * t) <= lax.broadcasted_iota(jnp.int32, (t, tq), 1)
        kb, vb = k_ref[...], v_ref[...]
        kt = kb.astype(F32).T.astype(MXU_DTYPE)
        cb = (_lanes(c_ref[0, 0], tq), _lanes(c_ref[0, 1], tq))

        @pl.when(j == 0)
        def _():
            dqt_ref[...] = jnp.zeros_like(dqt_ref)

        def step(i, carry, masked):
            dk_acc, dv_acc, dc_accs = carry
            start = pl.multiple_of(i * tq, tq)
            qs = q_ref[pl.ds(start, tq), :] * scale
            dob = do_ref[pl.ds(start, tq), :]
            zero = jnp.zeros_like(qs)
            dks, dvs, dqs, dcs = [], [], [], []
            for hh in range(2):
                qh = jnp.where(first_q, qs, zero) if hh == 0 else jnp.where(first_q, zero, qs)
                doh = jnp.where(first_q, dob, zero) if hh == 0 else jnp.where(first_q, zero, dob)
                sc = lax.dot_general(kb, qh, nt, preferred_element_type=F32)
                p = jnp.exp(sc - cb[hh] - lse_ref[0, hh:hh + 1, pl.ds(start, tq)])
                if masked:
                    p = jnp.where(causal, p, 0.0)
                dp = lax.dot_general(vb, doh, nt, preferred_element_type=F32)
                ds = p * (dp - dl_ref[0, hh:hh + 1, pl.ds(start, tq)])
                pb, dsb = p.astype(MXU_DTYPE), ds.astype(MXU_DTYPE)
                dvs.append(jnp.dot(pb, dob, preferred_element_type=F32))
                dks.append(jnp.dot(dsb, qs, preferred_element_type=F32))
                dqs.append(jnp.dot(kt, dsb, preferred_element_type=F32))
                dcs.append(dc_accs[hh] - jnp.sum(ds, axis=1, keepdims=True))
            dqt_ref[:, pl.ds(start, tq)] += jnp.where(upper, dqs[0], dqs[1]) * scale
            return (dk_acc + jnp.where(first, dks[0], dks[1]), dv_acc + jnp.where(first, dvs[0], dvs[1]), tuple(dcs))

        nil, col = jnp.zeros((t, LANES), F32), jnp.zeros((t, 1), F32)
        carry = step(j // per, (nil, nil, (col, col)), True)
        dk_acc, dv_acc, dc_accs = lax.fori_loop(j // per + 1, nq, functools.partial(step, masked=False), carry)
        dk_ref[...] = dk_acc.astype(dk_ref.dtype)
        dv_ref[...] = dv_acc.astype(dv_ref.dtype)
        dc_ref[0, 0] = jnp.broadcast_to(dc_accs[0], (t, LANES))
        dc_ref[0, 1] = jnp.broadcast_to(dc_accs[1], (t, LANES))

        @pl.when(j == nb - 1)
        def _():
            for r in range(nb):
                dq_ref[r * t:(r + 1) * t, :] = dqt_ref[:, r * t:(r + 1) * t].T

    row = pl.BlockSpec((1, 2, s), lambda p, j: (p, 0, 0))
    return pl.pallas_call(
        body, name="fox_attn_bwd", grid=(np_, nb),
        in_specs=[pl.BlockSpec((s, LANES), lambda p, j: (0, p)),
                  pl.BlockSpec((t, LANES), lambda p, j: (j, np_ + p)),
                  pl.BlockSpec((t, LANES), lambda p, j: (j, 2 * np_ + p)),
                  pl.BlockSpec((s, LANES), lambda p, j: (0, p)), row, row,
                  pl.BlockSpec((1, 2, t, LANES), lambda p, j: (p, 0, j, 0))],
        out_specs=[pl.BlockSpec((s, LANES), lambda p, j: (0, p)),
                   pl.BlockSpec((t, LANES), lambda p, j: (j, p)),
                   pl.BlockSpec((t, LANES), lambda p, j: (j, p)),
                   pl.BlockSpec((1, 2, t, LANES), lambda p, j: (p, 0, j, 0))],
        out_shape=[jax.ShapeDtypeStruct((s, D_MODEL), F32), jax.ShapeDtypeStruct((s, D_MODEL), BF16),
                   jax.ShapeDtypeStruct((s, D_MODEL), BF16), jax.ShapeDtypeStruct((np_, 2, s, LANES), F32)],
        scratch_shapes=[pltpu.VMEM((LANES, s), F32)],
        compiler_params=_params(("parallel", "arbitrary")))(qkv, qkv, qkv, do, lse, delta, c_lanes)


def _head_sums(a, b, name):
    d = a.shape[1]
    sel = (jnp.arange(d)[:, None] // FOX_HEAD_DIM == jnp.arange(LANES)[None, :]).astype(F32)

    def fn(ab, bb, selb):
        prod = ab.astype(F32) * bb.astype(F32)
        return (jnp.dot(prod, selb, precision=lax.Precision.HIGHEST, preferred_element_type=F32),), ()
    return _ew(fn, [a, b], [sel], outs=[(LANES, F32)], name=name)[0]


def _fox_fwd(x, gain, w_qkv, w_f, b_f, w_out):
    s = x.shape[0]
    h = _rms_fwd(x, gain, "mix_norm")
    qkv = _mm(h, w_qkv, name="fox_qkv", out_dtype=BF16)
    fl = _mm(h, w_f, name="fox_f", tn=LANES)
    z_row, c_rowf = _fox_gate_fwd(fl[:, :FOX_HEADS].T, b_f.reshape(FOX_HEADS, 1))
    c_lanes = jnp.broadcast_to(c_rowf.reshape(HEAD_PAIRS, 2, s, 1), (HEAD_PAIRS, 2, s, LANES))
    o, lse = _attn_fwd_t(qkv, c_lanes)
    x1 = _mm(o, w_out, name="fox_out", add=x)
    return x1, (x, h, qkv, z_row, c_lanes, o, lse)


def _fox_bwd(dx1, saved, gain, w_qkv, w_f, w_out):
    x, h, qkv, z_row, c_lanes, o, lse = saved
    s = x.shape[0]
    do = _mm(dx1, w_out, tb=True, name="fox_do", out_dtype=BF16)
    dw_out = _mm(o, dx1, ta=True, name="fox_dw_out", out_dtype=BF16)
    delta = _head_sums(do, o, "fox_delta")[:, :FOX_HEADS].T.reshape(HEAD_PAIRS, 2, s)
    dq, dk, dv, dc = _attn_bwd_t(qkv, do, lse, delta, c_lanes)
    dz_row, db = _fox_gate_bwd(dc[..., 0].reshape(FOX_HEADS, s), z_row)
    dqkv = jnp.concatenate([dq.astype(BF16), dk, dv], axis=1)
    dfl = jnp.pad(dz_row.T, ((0, 0), (0, LANES - FOX_HEADS))).astype(BF16)
    dw_qkv = _mm(h, dqkv, ta=True, name="fox_dw_qkv", out_dtype=BF16)
    dw_f = _mm(h, dfl, ta=True, name="fox_dw_f", out_dtype=BF16, tn=LANES)
    dh = _mm(dqkv, w_qkv, tb=True, name="fox_dh_qkv")
    dx, dgain = _mm_rms_bwd(dfl, w_f, x, gain, dx1, name="fox_dh_f", add=dh)
    dw_in = jnp.concatenate([dw_qkv, dw_f[:, :FOX_HEADS]], axis=1)
    return dx, dgain, dw_in, db.reshape(FOX_HEADS), dw_out


S5_ROWS = 512
SCAN_CHUNKS = SUBLANES


def _s5_operands(a_re, a_im, log_dt, b_re, b_im, c_re, c_im):
    dt = jnp.exp(log_dt)[:, None]
    mag, ang = jnp.exp(a_re * dt), a_im * dt
    lr, li = mag * jnp.cos(ang), mag * jnp.sin(ang)
    den = a_re * a_re + a_im * a_im
    cr = ((lr - 1.0) * a_re + li * a_im) / den
    ci = (li * a_re - (lr - 1.0) * a_im) / den
    bbr = cr[..., None] * b_re - ci[..., None] * b_im
    bbi = cr[..., None] * b_im + ci[..., None] * b_re
    nb = S5_BLOCKS
    lam = jnp.stack([lr.reshape(nb, 2, S5_HALF), li.reshape(nb, 2, S5_HALF)], axis=2)
    eye4, eye2 = jnp.eye(4, dtype=F32), jnp.eye(2, dtype=F32)
    bb = jnp.stack([bbr, bbi], axis=0).reshape(2, nb, 2, 4, S5_STATE, S5_GROUP)
    bmat = jnp.einsum("rbhgpc,kg,jh->bhjkcrgp", bb, eye4, eye2).reshape(nb, 2, 128, 2 * S5_HALF)
    cc = jnp.stack([c_re, -c_im], axis=0).reshape(2, nb, 2, 4, S5_GROUP, S5_STATE)
    cmat = jnp.einsum("rbhgcp,kg,jh->bhrgpjkc", cc, eye4, eye2).reshape(nb, 2, 2 * S5_HALF, 128)
    return lam, bmat, cmat


def _time_to_scan_order(a):
    s, d = a.shape
    return a.reshape(SCAN_CHUNKS, s // SCAN_CHUNKS, d).transpose(1, 0, 2).reshape(s, d)


def _scan_to_time_order(a):
    s, d = a.shape
    return a.reshape(s // SCAN_CHUNKS, SCAN_CHUNKS, d).transpose(1, 0, 2).reshape(s, d)


def _scan_chunks(xr_ref, xi_ref, lr, li, nst, reverse, after_step=None, state=None):
    lanes = lr.shape[1]
    lr8, li8 = jnp.broadcast_to(lr, (SUBLANES, lanes)), jnp.broadcast_to(li, (SUBLANES, lanes))
    zero8 = jnp.zeros((SUBLANES, lanes), F32)

    def rows_of(n):
        s = (nst - 1 - n) if reverse else n
        return s, pl.ds(pl.multiple_of(s * SUBLANES, SUBLANES), SUBLANES)

    def local(n, carry):
        pr, pi = carry
        _, rows = rows_of(n)
        nr = lr8 * pr - li8 * pi + xr_ref[rows, :]
        ni = lr8 * pi + li8 * pr + xi_ref[rows, :]
        xr_ref[rows, :] = nr
        xi_ref[rows, :] = ni
        return nr, ni

    er, ei = lax.fori_loop(0, nst, local, (zero8, zero8))
    pr, pi = lr, li
    for _ in range(int(math.log2(nst))):
        pr, pi = pr * pr - pi * pi, 2.0 * pr * pi
    tr = ti = jnp.zeros((1, lanes), F32)
    ent_r, ent_i = [None] * SCAN_CHUNKS, [None] * SCAN_CHUNKS
    for k in (reversed(range(SCAN_CHUNKS)) if reverse else range(SCAN_CHUNKS)):
        ent_r[k], ent_i[k] = tr, ti
        tr, ti = er[k:k + 1] + (pr * tr - pi * ti), ei[k:k + 1] + (pr * ti + pi * tr)
    in_r, in_i = jnp.concatenate(ent_r, axis=0), jnp.concatenate(ent_i, axis=0)

    def fix(n, carry):
        wr, wi, st = carry
        s, rows = rows_of(n)
        nr = xr_ref[rows, :] + (wr * in_r - wi * in_i)
        ni = xi_ref[rows, :] + (wr * in_i + wi * in_r)
        xr_ref[rows, :] = nr
        xi_ref[rows, :] = ni
        if after_step is not None:
            st = after_step(s, nr, ni, st)
        return wr * lr8 - wi * li8, wr * li8 + wi * lr8, st

    _, _, state = lax.fori_loop(0, nst, fix, (lr8, li8, state))
    return in_r, in_i, state


def _s5_fill_states(u_ref, bm, xr_ref, xi_ref, s):
    rc = min(S5_ROWS, s)

    def fill(r, _):
        rows = pl.ds(pl.multiple_of(r * rc, rc), rc)
        bu = jnp.dot(u_ref[rows, :].astype(MXU_DTYPE), bm, preferred_element_type=F32)
        xr_ref[rows, :] = bu[:, :S5_HALF]
        xi_ref[rows, :] = bu[:, S5_HALF:]
        return 0
    lax.fori_loop(0, s // rc, fill, 0)


def _s5_specs():
    return [pl.BlockSpec((1, 2, 2, S5_HALF), lambda b: (b, 0, 0, 0)),
            pl.BlockSpec((1, 2, 128, 2 * S5_HALF), lambda b: (b, 0, 0, 0)),
            pl.BlockSpec((1, 2, 2 * S5_HALF, 128), lambda b: (b, 0, 0, 0)),
            pl.BlockSpec((1, LANES), lambda b: (0, b))]


def _s5_scan_fwd(u, lam, bmat, cmat, dvec):
    s = u.shape[0]
    nst = s // SCAN_CHUNKS
    rc = min(S5_ROWS, s)

    def body(u_ref, lam_ref, b_ref, c_ref, d_ref, y_ref, xr_ref, xi_ref):
        y_ref[...] = u_ref[...] * d_ref[...]
        for hb in range(2):
            _s5_fill_states(u_ref, b_ref[0, hb], xr_ref, xi_ref, s)
            _scan_chunks(xr_ref, xi_ref, lam_ref[0, hb, 0:1, :], lam_ref[0, hb, 1:2, :], nst, False)
            cm = c_ref[0, hb]

            def emit(r, _, cm=cm):
                rows = pl.ds(pl.multiple_of(r * rc, rc), rc)
                y_ref[rows, :] += (jnp.dot(xr_ref[rows, :].astype(MXU_DTYPE), cm[:S5_HALF], preferred_element_type=F32)
                                   + jnp.dot(xi_ref[rows, :].astype(MXU_DTYPE), cm[S5_HALF:], preferred_element_type=F32))
                return 0
            lax.fori_loop(0, s // rc, emit, 0)

    blk = pl.BlockSpec((s, LANES), lambda b: (0, b))
    return pl.pallas_call(
        body, name="s5_scan_fwd", grid=(S5_BLOCKS,), in_specs=[blk] + _s5_specs(), out_specs=blk,
        out_shape=jax.ShapeDtypeStruct(u.shape, F32),
        scratch_shapes=[pltpu.VMEM((s, S5_HALF), F32)] * 2,
        compiler_params=_params(("parallel",)))(u, lam, bmat, cmat, dvec)


def _s5_scan_bwd(u, dy, lam, bmat, cmat, dvec):
    s = u.shape[0]
    nst = s // SCAN_CHUNKS
    rc = min(S5_ROWS, s)
    nt = (((1,), (1,)), ((), ()))
    tn = (((0,), (0,)), ((), ()))

    def body(u_ref, dy_ref, lam_ref, b_ref, c_ref, d_ref, du_ref, db_ref, dc_ref, dl_ref, dd_ref,
             xr_ref, xi_ref, gr_ref, gi_ref):
        du_ref[...] = dy_ref[...] * d_ref[...]
        dd_ref[...] = jnp.sum(dy_ref[...] * u_ref[...], axis=0, keepdims=True)
        db_ref[...] = jnp.zeros_like(db_ref)
        dc_ref[...] = jnp.zeros_like(dc_ref)
        for hb in range(2):
            bm, cm = b_ref[0, hb], c_ref[0, hb]
            lr, li = lam_ref[0, hb, 0:1, :], lam_ref[0, hb, 1:2, :]
            _s5_fill_states(u_ref, bm, xr_ref, xi_ref, s)
            xin_r, xin_i, _ = _scan_chunks(xr_ref, xi_ref, lr, li, nst, False)

            def fill_g(r, _, cm=cm):
                rows = pl.ds(pl.multiple_of(r * rc, rc), rc)
                g = lax.dot_general(dy_ref[rows, :].astype(MXU_DTYPE), cm, nt, preferred_element_type=F32)
                gr_ref[rows, :] = g[:, :S5_HALF]
                gi_ref[rows, :] = g[:, S5_HALF:]
                return 0
            lax.fori_loop(0, s // rc, fill_g, 0)
            def lam_grad(st, g_r, g_i, acc, xin_r=xin_r, xin_i=xin_i):
                prev = pl.ds(pl.multiple_of(jnp.maximum(st - 1, 0) * SUBLANES, SUBLANES), SUBLANES)
                x_r = jnp.where(st > 0, xr_ref[prev, :], xin_r)
                x_i = jnp.where(st > 0, xi_ref[prev, :], xin_i)
                return acc[0] + (g_r * x_r + g_i * x_i), acc[1] + (g_i * x_r - g_r * x_i)

            zero8 = jnp.zeros((SUBLANES, S5_HALF), F32)
            _, _, (a_r, a_i) = _scan_chunks(gr_ref, gi_ref, lr, -li, nst, True, after_step=lam_grad, state=(zero8, zero8))
            dl_ref[0, hb] = jnp.concatenate([jnp.sum(a_r, axis=0, keepdims=True),
                                             jnp.sum(a_i, axis=0, keepdims=True)], axis=0)

            def emit(r, _, bm=bm, hb=hb):
                rows = pl.ds(pl.multiple_of(r * rc, rc), rc)
                g = jnp.concatenate([gr_ref[rows, :], gi_ref[rows, :]], axis=1).astype(MXU_DTYPE)
                x = jnp.concatenate([xr_ref[rows, :], xi_ref[rows, :]], axis=1).astype(MXU_DTYPE)
                du_ref[rows, :] += lax.dot_general(g, bm, nt, preferred_element_type=F32)
                db_ref[0, hb] += lax.dot_general(u_ref[rows, :].astype(MXU_DTYPE), g, tn, preferred_element_type=F32)
                dc_ref[0, hb] += lax.dot_general(dy_ref[rows, :].astype(MXU_DTYPE), x, tn, preferred_element_type=F32)
                return 0
            lax.fori_loop(0, s // rc, emit, 0)

    blk = pl.BlockSpec((s, LANES), lambda b: (0, b))
    mat = pl.BlockSpec((1, 2, 128, 2 * S5_HALF), lambda b: (b, 0, 0, 0))
    return pl.pallas_call(
        body, name="s5_scan_bwd", grid=(S5_BLOCKS,), in_specs=[blk, blk] + _s5_specs(),
        out_specs=[blk, mat, mat, pl.BlockSpec((1, 2, 2, S5_HALF), lambda b: (b, 0, 0, 0)),
                   pl.BlockSpec((1, LANES), lambda b: (0, b))],
        out_shape=[jax.ShapeDtypeStruct(u.shape, F32),
                   jax.ShapeDtypeStruct((S5_BLOCKS, 2, 128, 2 * S5_HALF), F32),
                   jax.ShapeDtypeStruct((S5_BLOCKS, 2, 128, 2 * S5_HALF), F32),
                   jax.ShapeDtypeStruct((S5_BLOCKS, 2, 2, S5_HALF), F32),
                   jax.ShapeDtypeStruct((1, D_MODEL), F32)],
        scratch_shapes=[pltpu.VMEM((s, S5_HALF), F32)] * 4,
        compiler_params=_params(("parallel",)))(u, dy, lam, bmat, cmat, dvec)


_GELU_C = math.sqrt(2.0 / math.pi)


def _gelu_parts(y):
    inner = _GELU_C * (y + 0.044715 * y * y * y)
    th = jnp.tanh(inner)
    return 0.5 * y * (1.0 + th), th


def _s5_fwd(x, gain, w_in, ssm, dvec, w_glu):
    lam, bmat, cmat = ssm
    h = _rms_fwd(x, gain, "mix_norm")
    u = _mm(h, w_in, name="s5_in")
    y = _scan_to_time_order(_s5_scan_fwd(_time_to_scan_order(u), lam, bmat.astype(MXU_DTYPE), cmat.astype(MXU_DTYPE), dvec))
    g = _ew(lambda yb: ((_gelu_parts(yb)[0],), ()), [y], outs=[(D_MODEL, BF16)], name="s5_gelu")[0]
    vg = _mm(g, w_glu, name="s5_glu", out_dtype=BF16)

    def glu_fn(vb, gb, xb):
        return (xb + vb.astype(F32) * _sigmoid(gb.astype(F32)),), ()
    x1 = _ew(glu_fn, [(vg, D_MODEL, 0), (vg, D_MODEL, 1), x], outs=[(D_MODEL, F32)], name="s5_gate")[0]
    return x1, (x, h, u, y, g, vg)


def _s5_bwd(dx1, saved, gain, w_in, ssm, dvec, w_glu):
    x, h, u, y, g, vg = saved
    lam, bmat, cmat = ssm

    def dglu_fn(db, vb, gb):
        vf, sg = vb.astype(F32), _sigmoid(gb.astype(F32))
        return (jnp.concatenate([db * sg, db * vf * sg * (1.0 - sg)], axis=1),), ()
    dvg = _ew(dglu_fn, [dx1, (vg, D_MODEL, 0), (vg, D_MODEL, 1)], outs=[(2 * D_MODEL, BF16)], name="s5_dgate")[0]
    dw_glu = _mm(g, dvg, ta=True, name="s5_dw_glu", out_dtype=BF16)
    dg = _mm(dvg, w_glu, tb=True, name="s5_dg")

    def dgelu_fn(dgb, yb):
        _, th = _gelu_parts(yb)
        dinner = _GELU_C * (1.0 + 3.0 * 0.044715 * yb * yb)
        return (dgb * (0.5 * (1.0 + th) + 0.5 * yb * (1.0 - th * th) * dinner),), ()
    dy = _ew(dgelu_fn, [dg, y], outs=[(D_MODEL, F32)], name="s5_dgelu")[0]
    du_s, dbm, dct, dlam, ddvec = _s5_scan_bwd(_time_to_scan_order(u), _time_to_scan_order(dy), lam,
                                               bmat.astype(MXU_DTYPE), cmat.astype(MXU_DTYPE), dvec)
    du = _scan_to_time_order(du_s)
    dw_in = _mm(h, du, ta=True, name="s5_dw_in", out_dtype=BF16)
    dx, dgain = _mm_rms_bwd(du, w_in, x, gain, dx1, name="s5_dh")
    return dx, dgain, dw_in, (dlam, dbm, jnp.swapaxes(dct, 2, 3)), ddvec, dw_glu


POOL_BLOCK = 256
N_POOL_GROUPS = len(POOL_WINDOWS)


def _pool_bands(gi, i, t):
    w = jnp.left_shift(2, gi)
    r = lax.broadcasted_iota(jnp.int32, (t, t), 0)
    c = lax.broadcasted_iota(jnp.int32, (t, t), 1)
    inside = ((c <= r) & (c > r - w)).astype(MXU_DTYPE)
    before = (c > r - w + t).astype(MXU_DTYPE)

    def inv_count(block):
        pos = block * t + lax.broadcasted_iota(jnp.int32, (t, 1), 0)
        return 1.0 / jnp.minimum(pos + 1, w).astype(F32)
    return inside, before, inv_count


def _pool_fwd(x, gain, w_grp, b_grp, scale):
    s = x.shape[0]
    t = min(POOL_BLOCK, s)
    h = _rms_fwd(x, gain, "mix_norm")

    def body(h_ref, hp_ref, w_ref, b_ref, sc_ref, x_ref, x1_ref, diff_ref):
        gi, i = pl.program_id(0), pl.program_id(1)
        inside, before, inv_count = _pool_bands(gi, i, t)
        hc = h_ref[...]
        tot = jnp.dot(inside, hc.astype(MXU_DTYPE), preferred_element_type=F32)
        prev = jnp.dot(before, hp_ref[...].astype(MXU_DTYPE), preferred_element_type=F32)
        tot = tot + jnp.where(i > 0, prev, 0.0)
        diff = (tot * inv_count(i) - hc.astype(F32)).astype(diff_ref.dtype)
        y = (jnp.dot(diff.astype(MXU_DTYPE), w_ref[0], preferred_element_type=F32) + b_ref[...]) * sc_ref[...]
        diff_ref[...] = diff
        x1_ref[...] = x_ref[...] + y

    blk = pl.BlockSpec((t, POOL_WIDTH), lambda gi, i: (i, gi))
    vec = pl.BlockSpec((1, POOL_WIDTH), lambda gi, i: (0, gi))
    x1, diff = pl.pallas_call(
        body, name="pool_fwd", grid=(N_POOL_GROUPS, s // t),
        in_specs=[blk, pl.BlockSpec((t, POOL_WIDTH), lambda gi, i: (jnp.maximum(i - 1, 0), gi)),
                  pl.BlockSpec((1, POOL_WIDTH, POOL_WIDTH), lambda gi, i: (gi, 0, 0)), vec, vec, blk],
        out_specs=[blk, blk],
        out_shape=[jax.ShapeDtypeStruct(x.shape, F32), jax.ShapeDtypeStruct(x.shape, BF16)],
        compiler_params=_params(("parallel", "arbitrary")))(h, h, w_grp, b_grp, scale, x)
    return x1, (x, diff)


def _pool_bwd(dx1, saved, gain, w_grp, b_grp, scale):
    x, diff = saved
    s = x.shape[0]
    t = min(POOL_BLOCK, s)
    nb = s // t

    def body1(dx_ref, diff_ref, w_ref, b_ref, sc_ref, dd_ref, dw_ref, db_ref, dsc_ref):
        i = pl.program_id(1)

        @pl.when(i == 0)
        def _():
            dw_ref[...] = jnp.zeros_like(dw_ref)
            db_ref[...] = jnp.zeros_like(db_ref)
            dsc_ref[...] = jnp.zeros_like(dsc_ref)

        dfb = diff_ref[...].astype(MXU_DTYPE)
        ypre = jnp.dot(dfb, w_ref[0], preferred_element_type=F32) + b_ref[...]
        dxb = dx_ref[...]
        dy = dxb * sc_ref[...]
        dsc_ref[...] += jnp.sum(dxb * ypre, axis=0, keepdims=True)
        db_ref[...] += jnp.sum(dy, axis=0, keepdims=True)
        dyb = dy.astype(MXU_DTYPE)
        dw_ref[0] += lax.dot_general(dfb, dyb, (((0,), (0,)), ((), ())), preferred_element_type=F32)
        dd_ref[...] = lax.dot_general(dyb, w_ref[0], (((1,), (1,)), ((), ())), preferred_element_type=F32)

    blk = pl.BlockSpec((t, POOL_WIDTH), lambda gi, i: (i, gi))
    vec = pl.BlockSpec((1, POOL_WIDTH), lambda gi, i: (0, gi))
    mat = pl.BlockSpec((1, POOL_WIDTH, POOL_WIDTH), lambda gi, i: (gi, 0, 0))
    ddiff, dw, db, dsc = pl.pallas_call(
        body1, name="pool_bwd_map", grid=(N_POOL_GROUPS, nb), in_specs=[blk, blk, mat, vec, vec],
        out_specs=[blk, mat, vec, vec],
        out_shape=[jax.ShapeDtypeStruct(x.shape, F32), jax.ShapeDtypeStruct(w_grp.shape, F32),
                   jax.ShapeDtypeStruct((1, D_MODEL), F32), jax.ShapeDtypeStruct((1, D_MODEL), F32)],
        compiler_params=_params(("parallel", "arbitrary")))(dx1, diff, w_grp, b_grp, scale)

    def body2(dc_ref, dn_ref, dh_ref):
        gi, i = pl.program_id(0), pl.program_id(1)
        inside, before, inv_count = _pool_bands(gi, i, t)
        tn = (((0,), (0,)), ((), ()))
        dc = dc_ref[...]
        tot = lax.dot_general(inside, (dc * inv_count(i)).astype(MXU_DTYPE), tn, preferred_element_type=F32)
        nxt = lax.dot_general(before, (dn_ref[...] * inv_count(i + 1)).astype(MXU_DTYPE), tn, preferred_element_type=F32)
        dh_ref[...] = tot + jnp.where(i < nb - 1, nxt, 0.0) - dc

    dh = pl.pallas_call(
        body2, name="pool_bwd_window", grid=(N_POOL_GROUPS, nb),
        in_specs=[blk, pl.BlockSpec((t, POOL_WIDTH), lambda gi, i: (jnp.minimum(i + 1, nb - 1), gi))],
        out_specs=blk, out_shape=jax.ShapeDtypeStruct(x.shape, F32),
        compiler_params=_params(("parallel", "parallel")))(ddiff, ddiff)
    dx, dgain = _rms_bwd(x, gain, dh, dx1, "mix_norm_bwd")
    return dx, dgain, dw, db, dsc


MESH_ID = pl.DeviceIdType.MESH
ANY = pl.BlockSpec(memory_space=pl.ANY)


def _place():
    x, y, c = lax.axis_index("x"), lax.axis_index("y"), lax.axis_index("c")
    other_chips = [(1 - x, y), (x, 1 - y), (1 - x, 1 - y)]
    return x, y, c, other_chips


def _chip_index(chip):
    return 2 * chip[0] + chip[1]


def _remote(src, dst, send_sems, recv_sems, n, to):
    return pltpu.make_async_remote_copy(src_ref=src, dst_ref=dst, send_sem=send_sems.at[n], recv_sem=recv_sems.at[n],
                                        device_id=to, device_id_type=MESH_ID)


def _gather_weights(ws):
    n = len(ws)
    from_x, relay_x, from_y, relay_y, sib_x, sib_y, sib_d0, sib_d1, sib_own = range(9)
    slots = 9

    def body(*refs):
        w_refs, out_refs = refs[:n], refs[n:2 * n]
        send_sems, recv_sems = refs[2 * n:]
        x, y, c, (xn, yn, dn) = _place()
        k = _chip_index((x, y))
        me, sibling = (x, y, c), (x, y, 1 - c)

        def copy(ref, t, slot, to):
            return _remote(ref, ref, send_sems, recv_sems, slots * t + slot, to)

        def quarter(ref, q):
            rows = ref.shape[0] // 2
            return ref.at[pl.ds(q * rows, rows)]

        started = []

        def start(cp):
            cp.start()
            started.append(cp)

        for t in range(n):
            for slot, chip in ((from_x, xn), (from_y, yn)):
                start(_remote(w_refs[t].at[c], out_refs[t].at[k, c], send_sems, recv_sems, slots * t + slot, (*chip, c)))
            start(_remote(w_refs[t], out_refs[t].at[k], send_sems, recv_sems, slots * t + sib_own, sibling))
        for t in range(n):
            got = out_refs[t].at[_chip_index(xn), c]
            copy(got, t, from_x, me).wait_recv()
            start(copy(quarter(got, 0), t, relay_y, (*yn, c)))
            start(copy(got, t, sib_x, sibling))
        for t in range(n):
            got = out_refs[t].at[_chip_index(yn), c]
            copy(got, t, from_y, me).wait_recv()
            start(copy(quarter(got, 1), t, relay_x, (*xn, c)))
            start(copy(got, t, sib_y, sibling))
        for t in range(n):
            got = out_refs[t].at[_chip_index(dn), c]
            for q, slot, sib_slot in ((0, relay_y, sib_d0), (1, relay_x, sib_d1)):
                copy(quarter(got, q), t, slot, me).wait_recv()
                start(copy(quarter(got, q), t, sib_slot, sibling))
        for t in range(n):
            for chip, slot in ((xn, sib_x), (yn, sib_y)):
                copy(out_refs[t].at[_chip_index(chip), 1 - c], t, slot, me).wait_recv()
            theirs = out_refs[t].at[_chip_index(dn), 1 - c]
            copy(quarter(theirs, 0), t, sib_d0, me).wait_recv()
            copy(quarter(theirs, 1), t, sib_d1, me).wait_recv()
            copy(out_refs[t].at[k], t, sib_own, me).wait_recv()
        for cp in started:
            cp.wait_send()

    return pl.pallas_call(
        body, name="gather_weights", in_specs=[ANY] * n, out_specs=[ANY] * n,
        out_shape=[jax.ShapeDtypeStruct((N_CHIPS,) + w.shape, w.dtype) for w in ws],
        scratch_shapes=[pltpu.SemaphoreType.DMA((slots * n,)), pltpu.SemaphoreType.DMA((slots * n,))],
    )(*ws)


def _swap_halves(gs):
    n = len(gs)

    def body(*refs):
        g_refs, out_refs = refs[:n], refs[n:2 * n]
        send_sems, recv_sems = refs[2 * n:]
        x, y, c, _ = _place()
        copies = [_remote(g_refs[t].at[s, 1 - c], out_refs[t].at[s], send_sems, recv_sems, N_CHIPS * t + s, (x, y, 1 - c))
                  for t in range(n) for s in range(N_CHIPS)]
        for cp in copies:
            cp.start()
        for cp in copies:
            cp.wait_recv()
        for cp in copies:
            cp.wait_send()

    return pl.pallas_call(
        body, name="swap_halves", in_specs=[ANY] * n, out_specs=[ANY] * n,
        out_shape=[jax.ShapeDtypeStruct((N_CHIPS,) + g.shape[2:], g.dtype) for g in gs],
        scratch_shapes=[pltpu.SemaphoreType.DMA((N_CHIPS * n,)), pltpu.SemaphoreType.DMA((N_CHIPS * n,))],
    )(*gs)


def _neighbour_exchange(srcs, out_shapes, name):
    n = len(out_shapes)

    def body(*refs):
        in_refs, out_refs = refs[:n], refs[n:2 * n]
        send_sems, recv_sems = refs[2 * n:]
        x, y, c, chips = _place()
        sends = []
        for t in range(n):
            for slot, (src, chip) in enumerate(zip(srcs(in_refs[t], chips), chips[:2])):
                sends.append(_remote(src, out_refs[t].at[slot], send_sems, recv_sems, 2 * t + slot, (*chip, c)))
        for cp in sends:
            cp.start()
        for t in range(n):
            for slot in range(2):
                landed = out_refs[t].at[slot]
                _remote(landed, landed, send_sems, recv_sems, 2 * t + slot, (x, y, c)).wait_recv()
        for cp in sends:
            cp.wait_send()

    return pl.pallas_call(
        body, name=name, in_specs=[ANY] * n, out_specs=[ANY] * n, out_shape=out_shapes,
        scratch_shapes=[pltpu.SemaphoreType.DMA((2 * n,)), pltpu.SemaphoreType.DMA((2 * n,))])


def _relay_partials(ps):
    def srcs(p_ref, chips):
        half = p_ref.shape[1] // 2
        diagonal = p_ref.at[_chip_index(chips[2])]
        return diagonal.at[pl.ds(0, half)], diagonal.at[pl.ds(half, half)]

    shapes = [jax.ShapeDtypeStruct((2, p.shape[1] // 2, p.shape[2]), p.dtype) for p in ps]
    return _neighbour_exchange(srcs, shapes, "relay_partials")(*ps)


def _merge_relayed(p, relayed, targets, name):
    rows, cols = p.shape[1:]
    tr = _row_tile(rows // 2, SUM_ROWS)
    per = rows // 2 // tr

    def body(x_ref, y_ref, p_ref, r_ref, o_ref):
        to, q = pl.program_id(0), pl.program_id(1)
        extra = jnp.where(q == 1 - to, r_ref[0].astype(F32), 0.0)
        o_ref[0] = (p_ref[0].astype(F32) + extra).astype(o_ref.dtype)

    return pl.pallas_call(
        body, name=name,
        grid_spec=pltpu.PrefetchScalarGridSpec(
            num_scalar_prefetch=2, grid=(2, 2, per),
            in_specs=[pl.BlockSpec((1, tr, cols), lambda to, q, i, xs, ys: (jnp.where(to == 0, xs[0], ys[0]), q * per + i, 0)),
                      pl.BlockSpec((1, tr, cols), lambda to, q, i, xs, ys: (1 - to, i, 0))],
            out_specs=pl.BlockSpec((1, tr, cols), lambda to, q, i, xs, ys: (to, q * per + i, 0))),
        out_shape=jax.ShapeDtypeStruct((2, rows, cols), p.dtype),
        compiler_params=_params(("parallel", "parallel", "parallel")))(targets[0], targets[1], p, relayed)


def _scatter_partials(ms):
    shapes = [jax.ShapeDtypeStruct(m.shape, m.dtype) for m in ms]
    return _neighbour_exchange(lambda m_ref, chips: (m_ref.at[0], m_ref.at[1]), shapes, "scatter_partials")(*ms)


def _share_half(fs):
    n = len(fs)

    def body(*refs):
        f_refs, out_refs = refs[:n], refs[n:2 * n]
        send_sems, recv_sems = refs[2 * n:]
        x, y, c, _ = _place()
        sends = [_remote(f_refs[t], out_refs[t].at[c], send_sems, recv_sems, t, (x, y, 1 - c)) for t in range(n)]
        for cp in sends:
            cp.start()
        for t in range(n):
            theirs = out_refs[t].at[1 - c]
            _remote(theirs, theirs, send_sems, recv_sems, t, (x, y, c)).wait_recv()
        for cp in sends:
            cp.wait_send()

    return pl.pallas_call(
        body, name="share_half", in_specs=[ANY] * n, out_specs=[ANY] * n,
        out_shape=[jax.ShapeDtypeStruct((2,) + f.shape, f.dtype) for f in fs],
        scratch_shapes=[pltpu.SemaphoreType.DMA((n,)), pltpu.SemaphoreType.DMA((n,))],
    )(*fs)


def _gather_small(v):
    def body(v_ref, out_ref, send_sems, recv_sems):
        x, y, c, chips = _place()
        me, sibling = (x, y, c), (x, y, 1 - c)

        def slot(px, py, pc):
            return out_ref.at[4 * px + 2 * py + pc]

        first = [_remote(v_ref, slot(*me), send_sems, recv_sems, 0, sibling)]
        first += [_remote(v_ref, slot(*me), send_sems, recv_sems, 1 + j, (*chip, c)) for j, chip in enumerate(chips)]
        for cp in first:
            cp.start()
        passed = [_remote(slot(*chip, c), slot(*chip, c), send_sems, recv_sems, 4 + j, sibling)
                  for j, chip in enumerate(chips)]
        for j, chip in enumerate(chips):
            _remote(slot(*chip, c), slot(*chip, c), send_sems, recv_sems, 1 + j, me).wait_recv()
            passed[j].start()
        _remote(slot(*sibling), slot(*sibling), send_sems, recv_sems, 0, me).wait_recv()
        for j, chip in enumerate(chips):
            _remote(slot(*chip, 1 - c), slot(*chip, 1 - c), send_sems, recv_sems, 4 + j, me).wait_recv()
        for cp in first + passed:
            cp.wait_send()

    gathered = pl.pallas_call(
        body, name="gather_small", in_specs=[ANY], out_specs=ANY,
        out_shape=jax.ShapeDtypeStruct((N_DEV,) + v.shape, v.dtype),
        scratch_shapes=[pltpu.SemaphoreType.DMA((7,)), pltpu.SemaphoreType.DMA((7,))],
    )(v)
    device = 4 * lax.axis_index("x") + 2 * lax.axis_index("y") + lax.axis_index("c")
    return lax.dynamic_update_index_in_dim(gathered, v, device, 0)


SMALL_ROWS = 256
SUM_ROWS = 256
BF16_ROWS = 16


def _row_tile(rows, want):
    for t in range(min(rows, want) // BF16_ROWS * BF16_ROWS, 0, -BF16_ROWS):
        if rows % t == 0:
            return t
    return rows


def _pair_sum(g, r, core, name):
    rows, cols = g.shape[2:]
    tr = _row_tile(rows, SUM_ROWS)

    def body(c_ref, g_ref, r_ref, o_ref):
        o_ref[0] = (g_ref[0, 0].astype(F32) + r_ref[0].astype(F32)).astype(o_ref.dtype)

    return pl.pallas_call(
        body, name=name,
        grid_spec=pltpu.PrefetchScalarGridSpec(
            num_scalar_prefetch=1, grid=(N_CHIPS, rows // tr),
            in_specs=[pl.BlockSpec((1, 1, tr, cols), lambda s, i, c_ref: (s, c_ref[0], i, 0)),
                      pl.BlockSpec((1, tr, cols), lambda s, i, c_ref: (s, i, 0))],
            out_specs=pl.BlockSpec((1, tr, cols), lambda s, i, c_ref: (s, i, 0))),
        out_shape=jax.ShapeDtypeStruct(r.shape, BF16),
        compiler_params=_params(("parallel", "parallel")))(core, g, r)


def _chip_sum(p, recv, chip, name):
    rows, cols = p.shape[1:]
    tr = _row_tile(rows, SUM_ROWS)
    n_recv = recv.shape[0]

    def body(k_ref, p_ref, r_ref, o_ref):
        acc = p_ref[0].astype(F32)
        for j in range(n_recv):
            acc = acc + r_ref[j].astype(F32)
        o_ref[...] = acc

    return pl.pallas_call(
        body, name=name,
        grid_spec=pltpu.PrefetchScalarGridSpec(
            num_scalar_prefetch=1, grid=(rows // tr,),
            in_specs=[pl.BlockSpec((1, tr, cols), lambda i, k_ref: (k_ref[0], i, 0)),
                      pl.BlockSpec((n_recv, tr, cols), lambda i, k_ref: (0, i, 0))],
            out_specs=pl.BlockSpec((tr, cols), lambda i, k_ref: (i, 0))),
        out_shape=jax.ShapeDtypeStruct((rows, cols), F32),
        compiler_params=_params(("parallel",)))(chip, p, recv)


def _sum_blocks(a, name):
    n, rows, cols = a.shape
    tr = _row_tile(rows, SUM_ROWS)

    def body(a_ref, o_ref):
        acc = a_ref[0].astype(F32)
        for s in range(1, n):
            acc = acc + a_ref[s].astype(F32)
        o_ref[...] = acc

    return pl.pallas_call(
        body, name=name, grid=(rows // tr,),
        in_specs=[pl.BlockSpec((n, tr, cols), lambda i: (0, i, 0))],
        out_specs=pl.BlockSpec((tr, cols), lambda i: (i, 0)),
        out_shape=jax.ShapeDtypeStruct((rows, cols), F32),
        compiler_params=_params(("parallel",)))(a)


def _adamw(w, g, m, v, name):
    def fn(wb, gb, mb, vb):
        m2 = ADAM_B1 * mb + (1.0 - ADAM_B1) * gb
        v2 = ADAM_B2 * vb + (1.0 - ADAM_B2) * (gb * gb)
        m_hat = m2 / (1.0 - ADAM_B1 ** ADAM_STEP)
        v_hat = v2 / (1.0 - ADAM_B2 ** ADAM_STEP)
        delta = -ADAM_LR * (m_hat / (jnp.sqrt(v_hat) + ADAM_EPS) + ADAM_WD * wb)
        return (delta, m2, v2), ()
    c = w.shape[1]
    return _ew(fn, [w, g, m, v], outs=[(c, F32)] * 3, name=name)


WEIGHTS = ["mix_norm_g", "ffn_norm_g", "final_norm_g", "fox_w_in", "fox_b_f", "fox_w_out", "s5_w_in", "s5_a_re",
           "s5_a_im", "s5_log_dt", "s5_b_re", "s5_b_im", "s5_c_re", "s5_c_im", "s5_d", "s5_w_glu", "pool_w",
           "pool_b", "pool_scale", "ffn_w_gate_up", "ffn_w_down"]
BIG = {"fox_w_in": 2, "fox_w_out": 1, "s5_w_in": 1, "s5_w_glu": 2, "pool_w": 2, "ffn_w_gate_up": 2, "ffn_w_down": 1}
BY_CHIP = ("ffn_w_gate_up", "ffn_w_down")
SLICED = ("pool_b", "pool_scale")
SMALL = [n for n in WEIGHTS if n not in BIG]


def _to_natural(cm, axis):
    moved = jnp.moveaxis(cm, 0, axis)
    shape = moved.shape[:axis] + (moved.shape[axis] * moved.shape[axis + 1],) + moved.shape[axis + 2:]
    return moved.reshape(shape)


def _to_chip_major(nat, axis):
    shape = nat.shape[:axis] + (N_CHIPS, nat.shape[axis] // N_CHIPS) + nat.shape[axis + 1:]
    return jnp.moveaxis(nat.reshape(shape), axis, 0)


def _halves_view(shape):
    return (2, int(np.prod(shape[:-1])) // 2, shape[-1])


def _pack_small(parts):
    flat = jnp.concatenate([p.reshape(-1).astype(F32) for p in parts])
    pad = (-flat.shape[0]) % (SMALL_ROWS * LANES)
    return jnp.pad(flat, (0, pad)).reshape(-1, LANES)


def _unpack_small(buf, shapes):
    flat = buf.reshape(-1)
    out, off = [], 0
    for shp in shapes:
        n = int(np.prod(shp))
        out.append(flat[off:off + n].reshape(shp))
        off += n
    return out


def _local_step(x, target, w):
    grads = {}
    mixers = ("fox", "s5", "pool")
    saved = []
    ssm, ssm_pull = jax.vjp(_s5_operands, w["s5_a_re"][0], w["s5_a_im"][0], w["s5_log_dt"][0], w["s5_b_re"][0],
                            w["s5_b_im"][0], w["s5_c_re"][0], w["s5_c_im"][0])
    fox_w = []
    for j in range(w["fox_w_in"].shape[0]):
        w_in = w["fox_w_in"][j]
        w_f = jnp.pad(w_in[:, 3 * D_MODEL:], ((0, 0), (0, LANES - FOX_HEADS)))
        fox_w.append((w_in[:, :3 * D_MODEL], w_f, w["fox_w_out"][j]))
    for i in range(DEPTH):
        kind, j = mixers[i % 3], i // 3
        gain = w["mix_norm_g"][i]
        if kind == "fox":
            x1, sv = _fox_fwd(x, gain, fox_w[j][0], fox_w[j][1], w["fox_b_f"][j], fox_w[j][2])
        elif kind == "s5":
            x1, sv = _s5_fwd(x, gain, w["s5_w_in"][j], ssm, w["s5_d"], w["s5_w_glu"][j])
        else:
            x1, sv = _pool_fwd(x, gain, w["pool_w"][j], w["pool_b"], w["pool_scale"])
        x, sf = _ffn_fwd(x1, w["ffn_norm_g"][i], w["ffn_w_gate_up"], w["ffn_w_down"], i)
        saved.append((sv, sf))
    loss, dx, grads["final_norm_g"] = _loss_head(x, w["final_norm_g"], target)
    per_layer = {n: [None] * DEPTH for n in ("mix_norm_g", "ffn_norm_g")}
    fox_g = {n: [None] * len(fox_w) for n in ("fox_w_in", "fox_b_f", "fox_w_out")}
    for n in BY_CHIP:
        grads[n] = lax.empty(w[n].shape, BF16)
    for i in reversed(range(DEPTH)):
        kind, j = mixers[i % 3], i // 3
        sv, sf = saved[i]
        dx, per_layer["ffn_norm_g"][i], grads["ffn_w_gate_up"], grads["ffn_w_down"] = _ffn_bwd(
            dx, sf, w["ffn_norm_g"][i], w["ffn_w_gate_up"], w["ffn_w_down"], i, grads["ffn_w_gate_up"], grads["ffn_w_down"])
        gain = w["mix_norm_g"][i]
        if kind == "fox":
            dx, per_layer["mix_norm_g"][i], fox_g["fox_w_in"][j], fox_g["fox_b_f"][j], fox_g["fox_w_out"][j] = _fox_bwd(
                dx, sv, gain, fox_w[j][0], fox_w[j][1], fox_w[j][2])
        elif kind == "s5":
            dx, per_layer["mix_norm_g"][i], dw_in, dssm, dd, dw_glu = _s5_bwd(
                dx, sv, gain, w["s5_w_in"][j], ssm, w["s5_d"], w["s5_w_glu"][j])
            grads["s5_w_in"], grads["s5_w_glu"], grads["s5_d"] = dw_in[None], dw_glu[None], dd
            for n, g in zip(("s5_a_re", "s5_a_im", "s5_log_dt", "s5_b_re", "s5_b_im", "s5_c_re", "s5_c_im"), ssm_pull(dssm)):
                grads[n] = g[None]
        else:
            dx, per_layer["mix_norm_g"][i], dw, db, dsc = _pool_bwd(dx, sv, gain, w["pool_w"][j], w["pool_b"], w["pool_scale"])
            grads["pool_w"], grads["pool_b"], grads["pool_scale"] = dw[None].astype(BF16), db, dsc
    for n, parts in {**per_layer, **fox_g}.items():
        grads[n] = jnp.stack(parts)
    return loss, dx, grads


def kernel(x, mix_norm_g, ffn_norm_g, final_norm_g, fox_w_in, fox_b_f, fox_w_out, s5_w_in, s5_a_re, s5_a_im, s5_log_dt, s5_b_re, s5_b_im, s5_c_re, s5_c_im, s5_d, s5_w_glu, pool_w, pool_b, pool_scale, ffn_w_gate_up, ffn_w_down, loss_target, m_mix_norm_g, m_ffn_norm_g, m_final_norm_g, m_fox_w_in, m_fox_b_f, m_fox_w_out, m_s5_w_in, m_s5_a_re, m_s5_a_im, m_s5_log_dt, m_s5_b_re, m_s5_b_im, m_s5_c_re, m_s5_c_im, m_s5_d, m_s5_w_glu, m_pool_w, m_pool_b, m_pool_scale, m_ffn_w_gate_up, m_ffn_w_down, v_mix_norm_g, v_ffn_norm_g, v_final_norm_g, v_fox_w_in, v_fox_b_f, v_fox_w_out, v_s5_w_in, v_s5_a_re, v_s5_a_im, v_s5_log_dt, v_s5_b_re, v_s5_b_im, v_s5_c_re, v_s5_c_im, v_s5_d, v_s5_w_glu, v_pool_w, v_pool_b, v_pool_scale, v_ffn_w_gate_up, v_ffn_w_down):
    given = dict(locals())
    shard = {n: given[n] for n in WEIGHTS}
    chip = 2 * lax.axis_index("x") + lax.axis_index("y")
    core = lax.axis_index("c")

    views = {n: _halves_view(shard[n].shape) for n in BIG}
    own = [shard[n].astype(MXU_DTYPE).reshape(views[n]) for n in BIG]
    whole = {}
    for n, by_chip in zip(BIG, _gather_weights(own)):
        by_chip = by_chip.reshape((N_CHIPS,) + shard[n].shape)
        whole[n] = by_chip if n in BY_CHIP else _to_natural(by_chip, BIG[n])
    for n in SMALL:
        whole[n] = shard[n]
    sliced_shapes = [shard[n].shape for n in SLICED]
    by_chip = _gather_small(_pack_small([shard[n] for n in SLICED]))[0::2]
    slices = [_unpack_small(by_chip[k], sliced_shapes) for k in range(N_CHIPS)]
    for idx, n in enumerate(SLICED):
        whole[n] = jnp.concatenate([slices[k][idx] for k in range(N_CHIPS)], axis=-1)

    loss_part, dx, grads = _local_step(x[0], loss_target[0], whole)
    loss = lax.psum(loss_part, MESH_AXES)

    gs = [(grads[n] if n in BY_CHIP else _to_chip_major(grads[n].astype(BF16), BIG[n])).reshape((N_CHIPS,) + views[n])
          for n in BIG]
    core_id, chip_id = core.reshape(1).astype(jnp.int32), chip.reshape(1).astype(jnp.int32)
    partial = [_pair_sum(g, r, core_id, "pair_sum_" + n) for n, g, r in zip(BIG, gs, _swap_halves(gs))]
    x_chip, y_chip = 2 * (1 - lax.axis_index("x")) + lax.axis_index("y"), 2 * lax.axis_index("x") + 1 - lax.axis_index("y")
    neighbours = (x_chip.reshape(1).astype(jnp.int32), y_chip.reshape(1).astype(jnp.int32))
    merged = [_merge_relayed(p, r, neighbours, "merge_relayed_" + n) for n, p, r in zip(BIG, partial, _relay_partials(partial))]
    half = [_chip_sum(p, r, chip_id, "chip_sum_" + n) for n, p, r in zip(BIG, partial, _scatter_partials(merged))]
    grad = {n: lax.dynamic_update_index_in_dim(both, mine, core, 0).reshape(shard[n].shape)
            for n, mine, both in zip(BIG, half, _share_half(half))}

    small_sum = _sum_blocks(_gather_small(_pack_small([grads[n] for n in SMALL])), "small_sum")
    for n, g in zip(SMALL, _unpack_small(small_sum, [whole[n].shape for n in SMALL])):
        grad[n] = g
    for n in SLICED:
        width = shard[n].shape[-1]
        grad[n] = lax.dynamic_slice_in_dim(grad[n], chip * width, width, axis=-1)

    delta, new_m, new_v = {}, {}, {}
    for n in BIG:
        view = (-1, shard[n].shape[-1])
        res = _adamw(shard[n].reshape(view), grad[n].reshape(view), given["m_" + n].reshape(view),
                     given["v_" + n].reshape(view), "adamw_" + n)
        delta[n], new_m[n], new_v[n] = (r.reshape(shard[n].shape) for r in res)
    small_shapes = [shard[n].shape for n in SMALL]
    res = _adamw(_pack_small([shard[n] for n in SMALL]), _pack_small([grad[n] for n in SMALL]),
                 _pack_small([given["m_" + n] for n in SMALL]), _pack_small([given["v_" + n] for n in SMALL]), "adamw_small")
    for out, buf in zip((delta, new_m, new_v), res):
        for n, a in zip(SMALL, _unpack_small(buf, small_shapes)):
            out[n] = a
    return (loss, dx[None], *[grad[n] for n in WEIGHTS], *[delta[n] for n in WEIGHTS],
            *[new_m[n] for n in WEIGHTS], *[new_v[n] for n in WEIGHTS])
```

```python
import functools
import math

import jax
import jax.numpy as jnp
import numpy as np
from jax import lax
from jax.experimental import pallas as pl
from jax.experimental.pallas import tpu as pltpu

F32 = jnp.float32
BF16 = jnp.bfloat16
MXU_DTYPE = jnp.bfloat16

D_MODEL = 1024
DEPTH = 4
EPS = 1e-6
FOX_HEADS = 16
FOX_HEAD_DIM = 64
HEAD_PAIRS = FOX_HEADS // 2
S5_GROUPS = 64
S5_GROUP = 16
S5_STATE = 64
S5_BLOCKS = 8
S5_HALF = 256
POOL_WINDOWS = (2, 4, 8, 16)
POOL_WIDTH = 256
D_FF = 2816
N_CHIPS = 4
N_DEV = 8
LANES = 128
SUBLANES = 8
VMEM_LIMIT = 56 * 1024 * 1024

ADAM_LR = 0.001
ADAM_B1 = 0.9
ADAM_B2 = 0.999
ADAM_EPS = 1e-08
ADAM_WD = 0.01
ADAM_STEP = 10

MESH_AXES = ("x", "y", "c")


def _tile(n, want):
    t = (min(n, want) // LANES) * LANES
    while t >= LANES:
        if n % t == 0:
            return t
        t -= LANES
    return n


def _params(sem=None):
    return pltpu.CompilerParams(dimension_semantics=sem, vmem_limit_bytes=VMEM_LIMIT)


def _mm(a, b, *, name, ta=False, tb=False, out_dtype=F32, add=None, tm=1024, tn=1024, tk=1024,
        b_tiles=None, out_tiles=None, into=None):
    m, k = (a.shape[1], a.shape[0]) if ta else a.shape
    if b_tiles is None:
        n = b.shape[0] if tb else b.shape[1]
        assert (b.shape[1] if tb else b.shape[0]) == k, (a.shape, b.shape, ta, tb)
    else:
        assert b_tiles[0] == k, (a.shape, b_tiles[0])
        n = b_tiles[1]
    tm, tn, tk = _tile(m, tm), _tile(n, tn), _tile(k, tk)
    nk = k // tk
    a_spec = pl.BlockSpec((tk, tm), lambda i, j, kk: (kk, i)) if ta else pl.BlockSpec((tm, tk), lambda i, j, kk: (i, kk))
    b_shape = (tn, tk) if tb else (tk, tn)
    if b_tiles is None:
        b_spec = pl.BlockSpec(b_shape, (lambda i, j, kk: (j, kk)) if tb else (lambda i, j, kk: (kk, j)))
    else:
        b_spec = pl.BlockSpec(b_tiles[2], b_tiles[3])
    add_spec = pl.BlockSpec((tm, tn), lambda i, j, kk: (i, j))
    if out_tiles is None:
        o_spec, o_struct = add_spec, jax.ShapeDtypeStruct((m, n), out_dtype)
    else:
        o_spec, o_struct = pl.BlockSpec(out_tiles[1], out_tiles[2]), jax.ShapeDtypeStruct(out_tiles[0], out_dtype)
    dims = (((0 if ta else 1,), (1 if tb else 0,)), ((), ()))
    has_add, has_into = add is not None, into is not None

    def body(*refs):
        a_ref, b_ref = refs[:2]
        add_ref = refs[2] if has_add else None
        o_ref, acc_ref = refs[-2:]
        kk = pl.program_id(2)

        @pl.when(kk == 0)
        def _():
            acc_ref[...] = jnp.zeros_like(acc_ref)

        acc_ref[...] += lax.dot_general(a_ref[...].astype(MXU_DTYPE), b_ref[...].reshape(b_shape).astype(MXU_DTYPE), dims,
                                        preferred_element_type=F32)

        @pl.when(kk == nk - 1)
        def _():
            r = acc_ref[...]
            if has_add:
                r = r + add_ref[...].astype(F32)
            o_ref[...] = r.astype(out_dtype).reshape(o_ref.shape)

    ins = [a, b] + ([add] if has_add else []) + ([into] if has_into else [])
    specs = [a_spec, b_spec] + ([add_spec] if has_add else []) + ([pl.BlockSpec(memory_space=pl.ANY)] if has_into else [])
    return pl.pallas_call(
        body, name=name, grid=(m // tm, n // tn, nk), in_specs=specs, out_specs=o_spec,
        out_shape=o_struct, scratch_shapes=[pltpu.VMEM((tm, tn), F32)],
        input_output_aliases={len(ins) - 1: 0} if has_into else {},
        compiler_params=_params(("parallel", "parallel", "arbitrary")))(*ins)


def _ew(fn, tens, vecs=(), *, outs=(), sums=(), name, tr=256):
    tens = [t if isinstance(t, tuple) else (t, t.shape[1], 0) for t in tens]
    rows = tens[0][0].shape[0]
    tr = min(tr, rows)
    n_t, n_v, n_o, n_s = len(tens), len(vecs), len(outs), len(sums)

    def body(*refs):
        i = pl.program_id(0)
        t_blocks = [r[...] for r in refs[:n_t]]
        v_blocks = [r[...] for r in refs[n_t:n_t + n_v]]
        o_refs = refs[n_t + n_v:n_t + n_v + n_o]
        s_refs = refs[n_t + n_v + n_o:]
        o_vals, s_vals = fn(*t_blocks, *v_blocks)
        for r, v in zip(o_refs, o_vals):
            r[...] = v.astype(r.dtype)
        if n_s:
            @pl.when(i == 0)
            def _():
                for r in s_refs:
                    r[...] = jnp.zeros_like(r)
            for r, v in zip(s_refs, s_vals):
                r[...] += jnp.sum(v.astype(F32), axis=0, keepdims=True)

    in_specs = [pl.BlockSpec((tr, w), functools.partial(lambda i, cb: (i, cb), cb=cb)) for _, w, cb in tens]
    in_specs += [pl.BlockSpec(v.shape, functools.partial(lambda i, nd: (0,) * nd, nd=v.ndim)) for v in vecs]
    out_specs = [pl.BlockSpec((tr, c), lambda i: (i, 0)) for c, _ in outs]
    out_specs += [pl.BlockSpec((1, c), lambda i: (0, 0)) for c in sums]
    out_shape = [jax.ShapeDtypeStruct((rows, c), dt) for c, dt in outs]
    out_shape += [jax.ShapeDtypeStruct((1, c), F32) for c in sums]
    res = pl.pallas_call(
        body, name=name, grid=(rows // tr,), in_specs=in_specs, out_specs=out_specs, out_shape=out_shape,
        compiler_params=_params(("arbitrary",)))(*[t[0] for t in tens], *vecs)
    return res


def _sigmoid(z):
    return 1.0 / (1.0 + jnp.exp(-z))


def _rms_fwd(x, g, name):
    def fn(xb, gb):
        r = lax.rsqrt(jnp.mean(xb * xb, axis=-1, keepdims=True) + EPS)
        return ((xb * r) * gb,), ()
    return _ew(fn, [x], [g.reshape(1, -1)], outs=[(x.shape[1], BF16)], name=name)[0]


def _rms_bwd(x, g, dh, dres, name):
    def fn(xb, dhb, drb, gb):
        r = lax.rsqrt(jnp.mean(xb * xb, axis=-1, keepdims=True) + EPS)
        xh = xb * r
        dhf = dhb.astype(F32)
        dy = dhf * gb
        dx = r * (dy - xh * jnp.mean(dy * xh, axis=-1, keepdims=True))
        return (drb + dx,), (dhf * xh,)
    dx, dg = _ew(fn, [x, dh, dres], [g.reshape(1, -1)], outs=[(x.shape[1], F32)], sums=[x.shape[1]], name=name)
    return dx, dg[0]


def _mm_rms_bwd(a, b, x, g, dres, *, name, add=None, tm=512, tk=1024, b_tiles=None):
    m, k = a.shape
    n = x.shape[1]
    assert x.shape == (m, n) and (b_tiles is not None or b.shape == (n, k))
    tm, tk = _tile(m, tm), _tile(k, tk)
    nk = k // tk
    has_add = add is not None

    def body(*refs):
        a_ref, b_ref, x_ref, dr_ref, g_ref = refs[:5]
        add_ref = refs[5] if has_add else None
        dx_ref, dg_ref, acc_ref = refs[-3:]
        i, kk = pl.program_id(0), pl.program_id(1)

        @pl.when(kk == 0)
        def _():
            acc_ref[...] = jnp.zeros_like(acc_ref)

        @pl.when((kk == 0) & (i == 0))
        def _():
            dg_ref[...] = jnp.zeros_like(dg_ref)

        acc_ref[...] += lax.dot_general(a_ref[...].astype(MXU_DTYPE), b_ref[...].reshape((n, tk)).astype(MXU_DTYPE),
                                        (((1,), (1,)), ((), ())), preferred_element_type=F32)

        @pl.when(kk == nk - 1)
        def _():
            dh = acc_ref[...]
            if has_add:
                dh = dh + add_ref[...]
            xb = x_ref[...]
            r = lax.rsqrt(jnp.mean(xb * xb, axis=-1, keepdims=True) + EPS)
            xh = xb * r
            dy = dh * g_ref[...]
            dx_ref[...] = dr_ref[...] + r * (dy - xh * jnp.mean(dy * xh, axis=-1, keepdims=True))
            dg_ref[...] += jnp.sum(dh * xh, axis=0, keepdims=True)

    row = pl.BlockSpec((tm, n), lambda i, kk: (i, 0))
    vec = pl.BlockSpec((1, n), lambda i, kk: (0, 0))
    ins = [a, b, x, dres, g.reshape(1, n)] + ([add] if has_add else [])
    b_spec = pl.BlockSpec((n, tk), lambda i, kk: (0, kk)) if b_tiles is None else pl.BlockSpec(*b_tiles)
    specs = [pl.BlockSpec((tm, tk), lambda i, kk: (i, kk)), b_spec, row, row, vec]
    specs += [row] if has_add else []
    dx, dg = pl.pallas_call(
        body, name=name, grid=(m // tm, nk), in_specs=specs, out_specs=[row, vec],
        out_shape=[jax.ShapeDtypeStruct((m, n), F32), jax.ShapeDtypeStruct((1, n), F32)],
        scratch_shapes=[pltpu.VMEM((tm, n), F32)],
        compiler_params=_params(("arbitrary", "arbitrary")))(*ins)
    return dx, dg[0]


FFN_ROWS = 512
FFN_COLS = D_FF // 2


def _ffn_gate_up(x, gain, w_gu, layer):
    s, d = x.shape
    tm = min(FFN_ROWS, s)
    halves = D_FF // FFN_COLS

    def body(x_ref, gain_ref, wg_ref, wu_ref, h_ref, g_ref, u_ref, a_ref):
        @pl.when(pl.program_id(1) == 0)
        def _():
            xb = x_ref[...]
            h_ref[...] = ((xb * lax.rsqrt(jnp.mean(xb * xb, axis=-1, keepdims=True) + EPS)) * gain_ref[...]).astype(h_ref.dtype)

        hb = h_ref[...].astype(MXU_DTYPE)
        g = jnp.dot(hb, wg_ref[0, 0], preferred_element_type=F32)
        u = jnp.dot(hb, wu_ref[0, 0], preferred_element_type=F32)
        g_ref[...] = g.astype(g_ref.dtype)
        u_ref[...] = u.astype(u_ref.dtype)
        a_ref[...] = (g * _sigmoid(g) * u).astype(a_ref.dtype)

    row = pl.BlockSpec((tm, d), lambda i, jj: (i, 0))
    tile = pl.BlockSpec((tm, FFN_COLS), lambda i, jj: (i, jj))
    return pl.pallas_call(
        body, name="ffn_gate_up", grid=(s // tm, halves),
        in_specs=[row, pl.BlockSpec((1, d), lambda i, jj: (0, 0)),
                  pl.BlockSpec((1, 1, d, FFN_COLS), lambda i, jj: (jj, layer, 0, 0)),
                  pl.BlockSpec((1, 1, d, FFN_COLS), lambda i, jj: (halves + jj, layer, 0, 0))],
        out_specs=[row, tile, tile, tile],
        out_shape=[jax.ShapeDtypeStruct((s, d), BF16)] + [jax.ShapeDtypeStruct((s, D_FF), BF16)] * 3,
        compiler_params=_params(("parallel", "arbitrary")))(x, gain.reshape(1, d), w_gu, w_gu)


def _ffn_dgate_up(dx2, w_down, layer, g, u):
    s, d = dx2.shape
    tm = min(FFN_ROWS, s)
    halves = D_FF // FFN_COLS

    def body(dx_ref, w_ref, g_ref, u_ref, o_ref):
        jj = pl.program_id(1)
        df = lax.dot_general(dx_ref[...].astype(MXU_DTYPE), w_ref[...].reshape((FFN_COLS, d)), (((1,), (1,)), ((), ())),
                             preferred_element_type=F32)
        gf, uf = g_ref[...].astype(F32), u_ref[...].astype(F32)
        sg = _sigmoid(gf)
        dg = df * uf * (sg * (1.0 + gf * (1.0 - sg)))
        du = df * (gf * sg)
        o_ref[:, pl.ds(pl.multiple_of(jj * FFN_COLS, LANES), FFN_COLS)] = dg.astype(o_ref.dtype)
        o_ref[:, pl.ds(pl.multiple_of(D_FF + jj * FFN_COLS, LANES), FFN_COLS)] = du.astype(o_ref.dtype)

    tile = pl.BlockSpec((tm, FFN_COLS), lambda i, jj: (i, jj))
    return pl.pallas_call(
        body, name="ffn_dgate_up", grid=(s // tm, halves),
        in_specs=[pl.BlockSpec((tm, d), lambda i, jj: (i, 0)),
                  pl.BlockSpec((2, 1, FFN_COLS // 2, d), lambda i, jj: (jj, layer, 0, 0)), tile, tile],
        out_specs=pl.BlockSpec((tm, 2 * D_FF), lambda i, jj: (i, 0)),
        out_shape=jax.ShapeDtypeStruct((s, 2 * D_FF), BF16),
        compiler_params=_params(("parallel", "arbitrary")))(dx2, w_down, g, u)


def _ffn_fwd(x1, gain, w_gu, w_down, layer):
    d = x1.shape[1]
    h, g, u, act = _ffn_gate_up(x1, gain, w_gu, layer)
    x2 = _mm(act, w_down, name="ffn_down", add=x1, tk=FFN_COLS,
             b_tiles=(D_FF, d, (2, 1, FFN_COLS // 2, d), lambda i, j, kk: (kk, layer, 0, 0)))
    return x2, (x1, h, g, u, act)


def _ffn_bwd(dx2, saved, gain, w_gu, w_down, layer, dw_gu, dw_down):
    x1, h, g, u, act = saved
    d = x1.shape[1]
    dw_down = _mm(act, dx2, ta=True, name="ffn_dw_down", out_dtype=BF16, tm=FFN_COLS, into=dw_down,
                  out_tiles=(dw_down.shape, (2, 1, FFN_COLS // 2, d), lambda i, j, kk: (i, layer, 0, 0)))
    dgu = _ffn_dgate_up(dx2, w_down, layer, g, u)
    dw_gu = _mm(h, dgu, ta=True, name="ffn_dw_gu", out_dtype=BF16, tn=FFN_COLS, into=dw_gu,
                out_tiles=(dw_gu.shape, (1, 1, d, FFN_COLS), lambda i, j, kk: (j, layer, i, 0)))
    dx1, dgain = _mm_rms_bwd(dgu, w_gu, x1, gain, dx2, name="ffn_dh", tm=1024, tk=FFN_COLS,
                             b_tiles=((1, 1, d, FFN_COLS), lambda i, kk: (kk, layer, 0, 0)))
    return dx1, dgain, dw_gu, dw_down


def _loss_head(x, gain, target):
    d = x.shape[1]

    def fn(xb, tb, gb):
        r = lax.rsqrt(jnp.mean(xb * xb, axis=-1, keepdims=True) + EPS)
        xh = xb * r
        y = xh * gb
        err = y - tb
        dyv = err * (1.0 / d)
        dyg = dyv * gb
        dx = r * (dyg - xh * jnp.mean(dyg * xh, axis=-1, keepdims=True))
        return (dx,), (0.5 * err * err * (1.0 / d), dyv * xh)
    dx, lsum, dg = _ew(fn, [x, target], [gain.reshape(1, -1)], outs=[(d, F32)], sums=[d, d], name="loss_head")
    return jnp.sum(lsum), dx, dg[0]


ATT_BLOCK = 256
CUM_BLOCK = 512
NEG_INF = -1e30


def _fox_gate_fwd(fl_row, b_col):
    nh, s = fl_row.shape
    tb = min(CUM_BLOCK, s)

    def body(fl_ref, b_ref, z_ref, c_ref):
        upper = (lax.broadcasted_iota(jnp.int32, (tb, tb), 0) <= lax.broadcasted_iota(jnp.int32, (tb, tb), 1)).astype(F32)
        carry = jnp.zeros((nh, 1), F32)
        for blk in range(s // tb):
            z = fl_ref[:, blk * tb:(blk + 1) * tb] + b_ref[...]
            logf = jnp.minimum(z, 0.0) - jnp.log(1.0 + jnp.exp(-jnp.abs(z)))
            cs = jnp.dot(logf, upper, precision=lax.Precision.HIGHEST, preferred_element_type=F32) + carry
            z_ref[:, blk * tb:(blk + 1) * tb] = z
            c_ref[:, blk * tb:(blk + 1) * tb] = cs
            carry = cs[:, tb - 1:tb]

    return pl.pallas_call(body, name="fox_gate_fwd", out_shape=[jax.ShapeDtypeStruct((nh, s), F32)] * 2,
                          compiler_params=_params())(fl_row, b_col)


def _fox_gate_bwd(dc_row, z_row):
    nh, s = dc_row.shape
    tb = min(CUM_BLOCK, s)

    def body(dc_ref, z_ref, dz_ref, db_ref):
        lower = (lax.broadcasted_iota(jnp.int32, (tb, tb), 0) >= lax.broadcasted_iota(jnp.int32, (tb, tb), 1)).astype(F32)
        carry = jnp.zeros((nh, 1), F32)
        db = jnp.zeros((nh, 1), F32)
        for blk in reversed(range(s // tb)):
            dc = dc_ref[:, blk * tb:(blk + 1) * tb]
            rs = jnp.dot(dc, lower, precision=lax.Precision.HIGHEST, preferred_element_type=F32) + carry
            dz = rs * _sigmoid(-z_ref[:, blk * tb:(blk + 1) * tb])
            dz_ref[:, blk * tb:(blk + 1) * tb] = dz
            db = db + jnp.sum(dz, axis=1, keepdims=True)
            carry = rs[:, 0:1]
        db_ref[...] = db

    return pl.pallas_call(body, name="fox_gate_bwd",
                          out_shape=[jax.ShapeDtypeStruct((nh, s), F32), jax.ShapeDtypeStruct((nh, 1), F32)],
                          compiler_params=_params())(dc_row, z_row)


def _head_masks(rows):
    lane = lax.broadcasted_iota(jnp.int32, (rows, LANES), 1)
    return lane < FOX_HEAD_DIM


ATT_QUERIES = 512


def _lanes(a, width):
    return jnp.concatenate([a] * (width // LANES), axis=1)


def _attn_fwd_t(qkv, c_lanes):
    s = qkv.shape[0]
    t, tq = min(ATT_BLOCK, s), min(2 * ATT_QUERIES, s)
    nq, per = s // tq, tq // t
    scale = FOX_HEAD_DIM ** -0.5
    np_ = HEAD_PAIRS
    nt = (((1,), (1,)), ((), ()))

    def body(q_ref, k_ref, v_ref, c_ref, o_ref, lse_ref, vt_ref):
        i = pl.program_id(1)

        @pl.when(i == 0)
        def _():
            for r in range(s // t):
                vt_ref[:, r * t:(r + 1) * t] = v_ref[r * t:(r + 1) * t, :].astype(F32).T.astype(vt_ref.dtype)

        first = _head_masks(tq)
        upper = lax.broadcasted_iota(jnp.int32, (LANES, tq), 0) < FOX_HEAD_DIM
        qs = q_ref[...] * scale
        zero = jnp.zeros_like(qs)
        qh = (jnp.where(first, qs, zero), jnp.where(first, zero, qs))
        key_at = lax.broadcasted_iota(jnp.int32, (t, tq), 0)
        query_at = lax.broadcasted_iota(jnp.int32, (t, tq), 1)

        def block(j, carry, diagonal=None):
            ms, ls, acc = carry
            start = pl.multiple_of(j * t, t)
            kb, vt = k_ref[pl.ds(start, t), :], vt_ref[:, pl.ds(start, t)]
            new_m, new_l, alphas, pvs = [], [], [], []
            for hh in range(2):
                sc = lax.dot_general(kb, qh[hh], nt, preferred_element_type=F32) - _lanes(c_ref[0, hh, pl.ds(start, t), :], tq)
                if diagonal is not None:
                    sc = jnp.where(key_at + diagonal * t <= query_at, sc, NEG_INF)
                m_new = jnp.maximum(ms[hh], jnp.max(sc, axis=0, keepdims=True))
                p = jnp.exp(sc - m_new)
                alpha = jnp.exp(ms[hh] - m_new)
                new_m.append(m_new)
                new_l.append(alpha * ls[hh] + jnp.sum(p, axis=0, keepdims=True))
                alphas.append(alpha)
                p_hi = p.astype(MXU_DTYPE)
                p_lo = (p - p_hi.astype(F32)).astype(MXU_DTYPE)
                pvs.append(jnp.dot(vt, p_hi, preferred_element_type=F32) + jnp.dot(vt, p_lo, preferred_element_type=F32))
            acc = jnp.where(upper, alphas[0], alphas[1]) * acc + jnp.where(upper, pvs[0], pvs[1])
            return tuple(new_m), tuple(new_l), acc

        neg, nil = jnp.full((1, tq), NEG_INF, F32), jnp.zeros((1, tq), F32)
        carry = lax.fori_loop(0, per * i, block, ((neg, neg), (nil, nil), jnp.zeros((LANES, tq), F32)))
        for d in range(per):
            carry = block(per * i + d, carry, diagonal=d)
        ms, ls, acc = carry
        o_ref[...] = (acc / jnp.where(upper, ls[0], ls[1])).T.astype(o_ref.dtype)
        lse_ref[0] = jnp.concatenate([ms[0] + jnp.log(ls[0]), ms[1] + jnp.log(ls[1])], axis=0)

    return pl.pallas_call(
        body, name="fox_attn_fwd", grid=(np_, nq),
        in_specs=[pl.BlockSpec((tq, LANES), lambda p, i: (i, p)),
                  pl.BlockSpec((s, LANES), lambda p, i: (0, np_ + p)),
                  pl.BlockSpec((s, LANES), lambda p, i: (0, 2 * np_ + p)),
                  pl.BlockSpec((1, 2, s, LANES), lambda p, i: (p, 0, 0, 0))],
        out_specs=[pl.BlockSpec((tq, LANES), lambda p, i: (i, p)),
                   pl.BlockSpec((1, 2, tq), lambda p, i: (p, 0, i))],
        out_shape=[jax.ShapeDtypeStruct((s, D_MODEL), F32), jax.ShapeDtypeStruct((np_, 2, s), F32)],
        scratch_shapes=[pltpu.VMEM((LANES, s), MXU_DTYPE)],
        compiler_params=_params(("parallel", "arbitrary")))(qkv, qkv, qkv, c_lanes)


def _attn_bwd_t(qkv, do, lse, delta, c_lanes):
    s = qkv.shape[0]
    t, tq = min(ATT_BLOCK, s), min(ATT_QUERIES, s)
    nb, nq, per = s // t, s // tq, tq // t
    scale = FOX_HEAD_DIM ** -0.5
    np_ = HEAD_PAIRS
    nt = (((1,), (1,)), ((), ()))

    def body(q_ref, k_ref, v_ref, do_ref, lse_ref, dl_ref, c_ref, dq_ref, dk_ref, dv_ref, dc_ref, dqt_ref):
        j = pl.program_id(1)
        first_q, first = _head_masks(tq), _head_masks(t)
        upper = lax.broadcasted_iota(jnp.int32, (LANES, tq), 0) < FOX_HEAD_DIM
        causal = (lax.broadcasted_iota(jnp.int32, (t, tq), 0) + (j % per) * t) <= lax.broadcasted_iota(jnp.int32, (t, tq), 1)
        kb, vb = k_ref[...], v_ref[...]
        kt = kb.astype(F32).T.astype(MXU_DTYPE)
        cb = (_lanes(c_ref[0, 0], tq), _lanes(c_ref[0, 1], tq))

        @pl.when(j == 0)
        def _():
            dqt_ref[...] = jnp.zeros_like(dqt_ref)

        def step(i, carry, masked):
            dk_acc, dv_acc, dc_accs = carry
            start = pl.multiple_of(i * tq, tq)
            qs = q_ref[pl.ds(start, tq), :] * scale
            dob = do_ref[pl.ds(start, tq), :]
            zero = jnp.zeros_like(qs)
            dks, dvs, dqs, dcs = [], [], [], []
            for hh in range(2):
                qh = jnp.where(first_q, qs, zero) if hh == 0 else jnp.where(first_q, zero, qs)
                doh = jnp.where(first_q, dob, zero) if hh == 0 else jnp.where(first_q, zero, dob)
                sc = lax.dot_general(kb, qh, nt, preferred_element_type=F32)
                p = jnp.exp(sc - cb[hh] - lse_ref[0, hh:hh + 1, pl.ds(start, tq)])
                if masked:
                    p = jnp.where(causal, p, 0.0)
                dp = lax.dot_general(vb, doh, nt, preferred_element_type=F32)
                ds = p * (dp - dl_ref[0, hh:hh + 1, pl.ds(start, tq)])
                pb, dsb = p.astype(MXU_DTYPE), ds.astype(MXU_DTYPE)
                dvs.append(jnp.dot(pb, dob, preferred_element_type=F32))
                dks.append(jnp.dot(dsb, qs, preferred_element_type=F32))
                dqs.append(jnp.dot(kt, dsb, preferred_element_type=F32))
                dcs.append(dc_accs[hh] - jnp.sum(ds, axis=1, keepdims=True))
            dqt_ref[:, pl.ds(start, tq)] += jnp.where(upper, dqs[0], dqs[1]) * scale
            return (dk_acc + jnp.where(first, dks[0], dks[1]), dv_acc + jnp.where(first, dvs[0], dvs[1]), tuple(dcs))

        nil, col = jnp.zeros((t, LANES), F32), jnp.zeros((t, 1), F32)
        carry = step(j // per, (nil, nil, (col, col)), True)
        dk_acc, dv_acc, dc_accs = lax.fori_loop(j // per + 1, nq, functools.partial(step, masked=False), carry)
        dk_ref[...] = dk_acc.astype(dk_ref.dtype)
        dv_ref[...] = dv_acc.astype(dv_ref.dtype)
        dc_ref[0, 0] = jnp.broadcast_to(dc_accs[0], (t, LANES))
        dc_ref[0, 1] = jnp.broadcast_to(dc_accs[1], (t, LANES))

        @pl.when(j == nb - 1)
        def _():
            for r in range(nb):
                dq_ref[r * t:(r + 1) * t, :] = dqt_ref[:, r * t:(r + 1) * t].T

    row = pl.BlockSpec((1, 2, s), lambda p, j: (p, 0, 0))
    return pl.pallas_call(
        body, name="fox_attn_bwd", grid=(np_, nb),
        in_specs=[pl.BlockSpec((s, LANES), lambda p, j: (0, p)),
                  pl.BlockSpec((t, LANES), lambda p, j: (j, np_ + p)),
                  pl.BlockSpec((t, LANES), lambda p, j: (j, 2 * np_ + p)),
                  pl.BlockSpec((s, LANES), lambda p, j: (0, p)), row, row,
                  pl.BlockSpec((1, 2, t, LANES), lambda p, j: (p, 0, j, 0))],
        out_specs=[pl.BlockSpec((s, LANES), lambda p, j: (0, p)),
                   pl.BlockSpec((t, LANES), lambda p, j: (j, p)),
                   pl.BlockSpec((t, LANES), lambda p, j: (j, p)),
                   pl.BlockSpec((1, 2, t, LANES), lambda p, j: (p, 0, j, 0))],
        out_shape=[jax.ShapeDtypeStruct((s, D_MODEL), F32), jax.ShapeDtypeStruct((s, D_MODEL), BF16),
                   jax.ShapeDtypeStruct((s, D_MODEL), BF16), jax.ShapeDtypeStruct((np_, 2, s, LANES), F32)],
        scratch_shapes=[pltpu.VMEM((LANES, s), F32)],
        compiler_params=_params(("parallel", "arbitrary")))(qkv, qkv, qkv, do, lse, delta, c_lanes)


def _head_sums(a, b, name):
    d = a.shape[1]
    sel = (jnp.arange(d)[:, None] // FOX_HEAD_DIM == jnp.arange(LANES)[None, :]).astype(F32)

    def fn(ab, bb, selb):
        prod = ab.astype(F32) * bb.astype(F32)
        return (jnp.dot(prod, selb, precision=lax.Precision.HIGHEST, preferred_element_type=F32),), ()
    return _ew(fn, [a, b], [sel], outs=[(LANES, F32)], name=name)[0]


def _fox_fwd(x, gain, w_qkv, w_f, b_f, w_out):
    s = x.shape[0]
    h = _rms_fwd(x, gain, "mix_norm")
    qkv = _mm(h, w_qkv, name="fox_qkv", out_dtype=BF16)
    fl = _mm(h, w_f, name="fox_f", tn=LANES)
    z_row, c_rowf = _fox_gate_fwd(fl[:, :FOX_HEADS].T, b_f.reshape(FOX_HEADS, 1))
    c_lanes = jnp.broadcast_to(c_rowf.reshape(HEAD_PAIRS, 2, s, 1), (HEAD_PAIRS, 2, s, LANES))
    o, lse = _attn_fwd_t(qkv, c_lanes)
    x1 = _mm(o, w_out, name="fox_out", add=x)
    return x1, (x, h, qkv, z_row, c_lanes, o, lse)


def _fox_bwd(dx1, saved, gain, w_qkv, w_f, w_out):
    x, h, qkv, z_row, c_lanes, o, lse = saved
    s = x.shape[0]
    do = _mm(dx1, w_out, tb=True, name="fox_do", out_dtype=BF16)
    dw_out = _mm(o, dx1, ta=True, name="fox_dw_out", out_dtype=BF16)
    delta = _head_sums(do, o, "fox_delta")[:, :FOX_HEADS].T.reshape(HEAD_PAIRS, 2, s)
    dq, dk, dv, dc = _attn_bwd_t(qkv, do, lse, delta, c_lanes)
    dz_row, db = _fox_gate_bwd(dc[..., 0].reshape(FOX_HEADS, s), z_row)
    dqkv = jnp.concatenate([dq.astype(BF16), dk, dv], axis=1)
    dfl = jnp.pad(dz_row.T, ((0, 0), (0, LANES - FOX_HEADS))).astype(BF16)
    dw_qkv = _mm(h, dqkv, ta=True, name="fox_dw_qkv", out_dtype=BF16)
    dw_f = _mm(h, dfl, ta=True, name="fox_dw_f", out_dtype=BF16, tn=LANES)
    dh = _mm(dqkv, w_qkv, tb=True, name="fox_dh_qkv")
    dx, dgain = _mm_rms_bwd(dfl, w_f, x, gain, dx1, name="fox_dh_f", add=dh)
    dw_in = jnp.concatenate([dw_qkv, dw_f[:, :FOX_HEADS]], axis=1)
    return dx, dgain, dw_in, db.reshape(FOX_HEADS), dw_out


S5_ROWS = 2048
SCAN_CHUNKS = SUBLANES


def _s5_operands(a_re, a_im, log_dt, b_re, b_im, c_re, c_im):
    dt = jnp.exp(log_dt)[:, None]
    mag, ang = jnp.exp(a_re * dt), a_im * dt
    lr, li = mag * jnp.cos(ang), mag * jnp.sin(ang)
    den = a_re * a_re + a_im * a_im
    cr = ((lr - 1.0) * a_re + li * a_im) / den
    ci = (li * a_re - (lr - 1.0) * a_im) / den
    bbr = cr[..., None] * b_re - ci[..., None] * b_im
    bbi = cr[..., None] * b_im + ci[..., None] * b_re
    nb = S5_BLOCKS
    lam = jnp.stack([lr.reshape(nb, 2, S5_HALF), li.reshape(nb, 2, S5_HALF)], axis=2)
    eye4, eye2 = jnp.eye(4, dtype=F32), jnp.eye(2, dtype=F32)
    bb = jnp.stack([bbr, bbi], axis=0).reshape(2, nb, 2, 4, S5_STATE, S5_GROUP)
    bmat = jnp.einsum("rbhgpc,kg,jh->bhjkcrgp", bb, eye4, eye2).reshape(nb, 2, 128, 2 * S5_HALF)
    cc = jnp.stack([c_re, -c_im], axis=0).reshape(2, nb, 2, 4, S5_GROUP, S5_STATE)
    cmat = jnp.einsum("rbhgcp,kg,jh->bhrgpjkc", cc, eye4, eye2).reshape(nb, 2, 2 * S5_HALF, 128)
    return lam, bmat, cmat


def _time_to_scan_order(a):
    s, d = a.shape
    return a.reshape(SCAN_CHUNKS, s // SCAN_CHUNKS, d).transpose(1, 0, 2).reshape(s, d)


def _scan_to_time_order(a):
    s, d = a.shape
    return a.reshape(s // SCAN_CHUNKS, SCAN_CHUNKS, d).transpose(1, 0, 2).reshape(s, d)


def _scan_chunks(xr_ref, xi_ref, lr, li, nst, reverse, after_step=None, state=None):
    lanes = lr.shape[1]
    lr8, li8 = jnp.broadcast_to(lr, (SUBLANES, lanes)), jnp.broadcast_to(li, (SUBLANES, lanes))
    zero8 = jnp.zeros((SUBLANES, lanes), F32)

    def rows_of(n):
        s = (nst - 1 - n) if reverse else n
        return s, pl.ds(pl.multiple_of(s * SUBLANES, SUBLANES), SUBLANES)

    def local(n, carry):
        pr, pi = carry
        _, rows = rows_of(n)
        nr = lr8 * pr - li8 * pi + xr_ref[rows, :]
        ni = lr8 * pi + li8 * pr + xi_ref[rows, :]
        xr_ref[rows, :] = nr
        xi_ref[rows, :] = ni
        return nr, ni

    er, ei = lax.fori_loop(0, nst, local, (zero8, zero8))
    pr, pi = lr, li
    for _ in range(int(math.log2(nst))):
        pr, pi = pr * pr - pi * pi, 2.0 * pr * pi
    tr = ti = jnp.zeros((1, lanes), F32)
    ent_r, ent_i = [None] * SCAN_CHUNKS, [None] * SCAN_CHUNKS
    for k in (reversed(range(SCAN_CHUNKS)) if reverse else range(SCAN_CHUNKS)):
        ent_r[k], ent_i[k] = tr, ti
        tr, ti = er[k:k + 1] + (pr * tr - pi * ti), ei[k:k + 1] + (pr * ti + pi * tr)
    in_r, in_i = jnp.concatenate(ent_r, axis=0), jnp.concatenate(ent_i, axis=0)

    def fix(n, carry):
        wr, wi, st = carry
        s, rows = rows_of(n)
        nr = xr_ref[rows, :] + (wr * in_r - wi * in_i)
        ni = xi_ref[rows, :] + (wr * in_i + wi * in_r)
        xr_ref[rows, :] = nr
        xi_ref[rows, :] = ni
        if after_step is not None:
            st = after_step(s, nr, ni, st)
        return wr * lr8 - wi * li8, wr * li8 + wi * lr8, st

    _, _, state = lax.fori_loop(0, nst, fix, (lr8, li8, state))
    return in_r, in_i, state


def _s5_fill_states(u_ref, bm, xr_ref, xi_ref, s):
    rc = min(S5_ROWS, s)

    def fill(r, _):
        rows = pl.ds(pl.multiple_of(r * rc, rc), rc)
        bu = jnp.dot(u_ref[rows, :].astype(MXU_DTYPE), bm, preferred_element_type=F32)
        xr_ref[rows, :] = bu[:, :S5_HALF]
        xi_ref[rows, :] = bu[:, S5_HALF:]
        return 0
    lax.fori_loop(0, s // rc, fill, 0)


def _s5_specs():
    return [pl.BlockSpec((1, 2, 2, S5_HALF), lambda b: (b, 0, 0, 0)),
            pl.BlockSpec((1, 2, 128, 2 * S5_HALF), lambda b: (b, 0, 0, 0)),
            pl.BlockSpec((1, 2, 2 * S5_HALF, 128), lambda b: (b, 0, 0, 0)),
            pl.BlockSpec((1, LANES), lambda b: (0, b))]


def _s5_scan_fwd(u, lam, bmat, cmat, dvec):
    s = u.shape[0]
    nst = s // SCAN_CHUNKS
    rc = min(S5_ROWS, s)

    def body(u_ref, lam_ref, b_ref, c_ref, d_ref, y_ref, xr_ref, xi_ref):
        y_ref[...] = u_ref[...] * d_ref[...]
        for hb in range(2):
            _s5_fill_states(u_ref, b_ref[0, hb], xr_ref, xi_ref, s)
            _scan_chunks(xr_ref, xi_ref, lam_ref[0, hb, 0:1, :], lam_ref[0, hb, 1:2, :], nst, False)
            cm = c_ref[0, hb]

            def emit(r, _, cm=cm):
                rows = pl.ds(pl.multiple_of(r * rc, rc), rc)
                y_ref[rows, :] += (jnp.dot(xr_ref[rows, :].astype(MXU_DTYPE), cm[:S5_HALF], preferred_element_type=F32)
                                   + jnp.dot(xi_ref[rows, :].astype(MXU_DTYPE), cm[S5_HALF:], preferred_element_type=F32))
                return 0
            lax.fori_loop(0, s // rc, emit, 0)

    blk = pl.BlockSpec((s, LANES), lambda b: (0, b))
    return pl.pallas_call(
        body, name="s5_scan_fwd", grid=(S5_BLOCKS,), in_specs=[blk] + _s5_specs(), out_specs=blk,
        out_shape=jax.ShapeDtypeStruct(u.shape, F32),
        scratch_shapes=[pltpu.VMEM((s, S5_HALF), F32)] * 2,
        compiler_params=_params(("parallel",)))(u, lam, bmat, cmat, dvec)


def _s5_scan_bwd(u, dy, lam, bmat, cmat, dvec):
    s = u.shape[0]
    nst = s // SCAN_CHUNKS
    rc = min(S5_ROWS, s)
    nt = (((1,), (1,)), ((), ()))
    tn = (((0,), (0,)), ((), ()))

    def body(u_ref, dy_ref, lam_ref, b_ref, c_ref, d_ref, du_ref, db_ref, dc_ref, dl_ref, dd_ref,
             xr_ref, xi_ref, gr_ref, gi_ref):
        du_ref[...] = dy_ref[...] * d_ref[...]
        dd_ref[...] = jnp.sum(dy_ref[...] * u_ref[...], axis=0, keepdims=True)
        db_ref[...] = jnp.zeros_like(db_ref)
        dc_ref[...] = jnp.zeros_like(dc_ref)
        for hb in range(2):
            bm, cm = b_ref[0, hb], c_ref[0, hb]
            lr, li = lam_ref[0, hb, 0:1, :], lam_ref[0, hb, 1:2, :]
            _s5_fill_states(u_ref, bm, xr_ref, xi_ref, s)
            xin_r, xin_i, _ = _scan_chunks(xr_ref, xi_ref, lr, li, nst, False)

            def fill_g(r, _, cm=cm):
                rows = pl.ds(pl.multiple_of(r * rc, rc), rc)
                g = lax.dot_general(dy_ref[rows, :].astype(MXU_DTYPE), cm, nt, preferred_element_type=F32)
                gr_ref[rows, :] = g[:, :S5_HALF]
                gi_ref[rows, :] = g[:, S5_HALF:]
                return 0
            lax.fori_loop(0, s // rc, fill_g, 0)
            def lam_grad(st, g_r, g_i, acc, xin_r=xin_r, xin_i=xin_i):
                prev = pl.ds(pl.multiple_of(jnp.maximum(st - 1, 0) * SUBLANES, SUBLANES), SUBLANES)
                x_r = jnp.where(st > 0, xr_ref[prev, :], xin_r)
                x_i = jnp.where(st > 0, xi_ref[prev, :], xin_i)
                return acc[0] + (g_r * x_r + g_i * x_i), acc[1] + (g_i * x_r - g_r * x_i)

            zero8 = jnp.zeros((SUBLANES, S5_HALF), F32)
            _, _, (a_r, a_i) = _scan_chunks(gr_ref, gi_ref, lr, -li, nst, True, after_step=lam_grad, state=(zero8, zero8))
            dl_ref[0, hb] = jnp.concatenate([jnp.sum(a_r, axis=0, keepdims=True),
                                             jnp.sum(a_i, axis=0, keepdims=True)], axis=0)

            def emit(r, _, bm=bm, hb=hb):
                rows = pl.ds(pl.multiple_of(r * rc, rc), rc)
                g = jnp.concatenate([gr_ref[rows, :], gi_ref[rows, :]], axis=1).astype(MXU_DTYPE)
                x = jnp.concatenate([xr_ref[rows, :], xi_ref[rows, :]], axis=1).astype(MXU_DTYPE)
                du_ref[rows, :] += lax.dot_general(g, bm, nt, preferred_element_type=F32)
                db_ref[0, hb] += lax.dot_general(u_ref[rows, :].astype(MXU_DTYPE), g, tn, preferred_element_type=F32)
                dc_ref[0, hb] += lax.dot_general(dy_ref[rows, :].astype(MXU_DTYPE), x, tn, preferred_element_type=F32)
                return 0
            lax.fori_loop(0, s // rc, emit, 0)

    blk = pl.BlockSpec((s, LANES), lambda b: (0, b))
    mat = pl.BlockSpec((1, 2, 128, 2 * S5_HALF), lambda b: (b, 0, 0, 0))
    return pl.pallas_call(
        body, name="s5_scan_bwd", grid=(S5_BLOCKS,), in_specs=[blk, blk] + _s5_specs(),
        out_specs=[blk, mat, mat, pl.BlockSpec((1, 2, 2, S5_HALF), lambda b: (b, 0, 0, 0)),
                   pl.BlockSpec((1, LANES), lambda b: (0, b))],
        out_shape=[jax.ShapeDtypeStruct(u.shape, F32),
                   jax.ShapeDtypeStruct((S5_BLOCKS, 2, 128, 2 * S5_HALF), F32),
                   jax.ShapeDtypeStruct((S5_BLOCKS, 2, 128, 2 * S5_HALF), F32),
                   jax.ShapeDtypeStruct((S5_BLOCKS, 2, 2, S5_HALF), F32),
                   jax.ShapeDtypeStruct((1, D_MODEL), F32)],
        scratch_shapes=[pltpu.VMEM((s, S5_HALF), F32)] * 4,
        compiler_params=_params(("parallel",)))(u, dy, lam, bmat, cmat, dvec)


_GELU_C = math.sqrt(2.0 / math.pi)


def _gelu_parts(y):
    inner = _GELU_C * (y + 0.044715 * y * y * y)
    th = jnp.tanh(inner)
    return 0.5 * y * (1.0 + th), th


def _s5_fwd(x, gain, w_in, ssm, dvec, w_glu):
    lam, bmat, cmat = ssm
    h = _rms_fwd(x, gain, "mix_norm")
    u = _mm(h, w_in, name="s5_in")
    y = _scan_to_time_order(_s5_scan_fwd(_time_to_scan_order(u), lam, bmat.astype(MXU_DTYPE), cmat.astype(MXU_DTYPE), dvec))
    g = _ew(lambda yb: ((_gelu_parts(yb)[0],), ()), [y], outs=[(D_MODEL, BF16)], name="s5_gelu")[0]
    vg = _mm(g, w_glu, name="s5_glu", out_dtype=BF16)

    def glu_fn(vb, gb, xb):
        return (xb + vb.astype(F32) * _sigmoid(gb.astype(F32)),), ()
    x1 = _ew(glu_fn, [(vg, D_MODEL, 0), (vg, D_MODEL, 1), x], outs=[(D_MODEL, F32)], name="s5_gate")[0]
    return x1, (x, h, u, y, g, vg)


def _s5_bwd(dx1, saved, gain, w_in, ssm, dvec, w_glu):
    x, h, u, y, g, vg = saved
    lam, bmat, cmat = ssm

    def dglu_fn(db, vb, gb):
        vf, sg = vb.astype(F32), _sigmoid(gb.astype(F32))
        return (jnp.concatenate([db * sg, db * vf * sg * (1.0 - sg)], axis=1),), ()
    dvg = _ew(dglu_fn, [dx1, (vg, D_MODEL, 0), (vg, D_MODEL, 1)], outs=[(2 * D_MODEL, BF16)], name="s5_dgate")[0]
    dw_glu = _mm(g, dvg, ta=True, name="s5_dw_glu", out_dtype=BF16)
    dg = _mm(dvg, w_glu, tb=True, name="s5_dg")

    def dgelu_fn(dgb, yb):
        _, th = _gelu_parts(yb)
        dinner = _GELU_C * (1.0 + 3.0 * 0.044715 * yb * yb)
        return (dgb * (0.5 * (1.0 + th) + 0.5 * yb * (1.0 - th * th) * dinner),), ()
    dy = _ew(dgelu_fn, [dg, y], outs=[(D_MODEL, F32)], name="s5_dgelu")[0]
    du_s, dbm, dct, dlam, ddvec = _s5_scan_bwd(_time_to_scan_order(u), _time_to_scan_order(dy), lam,
                                               bmat.astype(MXU_DTYPE), cmat.astype(MXU_DTYPE), dvec)
    du = _scan_to_time_order(du_s)
    dw_in = _mm(h, du, ta=True, name="s5_dw_in", out_dtype=BF16)
    dx, dgain = _mm_rms_bwd(du, w_in, x, gain, dx1, name="s5_dh")
    return dx, dgain, dw_in, (dlam, dbm, jnp.swapaxes(dct, 2, 3)), ddvec, dw_glu


POOL_BLOCK = 256
N_POOL_GROUPS = len(POOL_WINDOWS)


def _pool_bands(gi, i, t):
    w = jnp.left_shift(2, gi)
    r = lax.broadcasted_iota(jnp.int32, (t, t), 0)
    c = lax.broadcasted_iota(jnp.int32, (t, t), 1)
    inside = ((c <= r) & (c > r - w)).astype(MXU_DTYPE)
    before = (c > r - w + t).astype(MXU_DTYPE)

    def inv_count(block):
        pos = block * t + lax.broadcasted_iota(jnp.int32, (t, 1), 0)
        return 1.0 / jnp.minimum(pos + 1, w).astype(F32)
    return inside, before, inv_count


def _pool_fwd(x, gain, w_grp, b_grp, scale):
    s = x.shape[0]
    t = min(POOL_BLOCK, s)
    h = _rms_fwd(x, gain, "mix_norm")

    def body(h_ref, hp_ref, w_ref, b_ref, sc_ref, x_ref, x1_ref, diff_ref):
        gi, i = pl.program_id(0), pl.program_id(1)
        inside, before, inv_count = _pool_bands(gi, i, t)
        hc = h_ref[...]
        tot = jnp.dot(inside, hc.astype(MXU_DTYPE), preferred_element_type=F32)
        prev = jnp.dot(before, hp_ref[...].astype(MXU_DTYPE), preferred_element_type=F32)
        tot = tot + jnp.where(i > 0, prev, 0.0)
        diff = (tot * inv_count(i) - hc.astype(F32)).astype(diff_ref.dtype)
        y = (jnp.dot(diff.astype(MXU_DTYPE), w_ref[0], preferred_element_type=F32) + b_ref[...]) * sc_ref[...]
        diff_ref[...] = diff
        x1_ref[...] = x_ref[...] + y

    blk = pl.BlockSpec((t, POOL_WIDTH), lambda gi, i: (i, gi))
    vec = pl.BlockSpec((1, POOL_WIDTH), lambda gi, i: (0, gi))
    x1, diff = pl.pallas_call(
        body, name="pool_fwd", grid=(N_POOL_GROUPS, s // t),
        in_specs=[blk, pl.BlockSpec((t, POOL_WIDTH), lambda gi, i: (jnp.maximum(i - 1, 0), gi)),
                  pl.BlockSpec((1, POOL_WIDTH, POOL_WIDTH), lambda gi, i: (gi, 0, 0)), vec, vec, blk],
        out_specs=[blk, blk],
        out_shape=[jax.ShapeDtypeStruct(x.shape, F32), jax.ShapeDtypeStruct(x.shape, BF16)],
        compiler_params=_params(("parallel", "arbitrary")))(h, h, w_grp, b_grp, scale, x)
    return x1, (x, diff)


def _pool_bwd(dx1, saved, gain, w_grp, b_grp, scale):
    x, diff = saved
    s = x.shape[0]
    t = min(POOL_BLOCK, s)
    nb = s // t

    def body1(dx_ref, diff_ref, w_ref, b_ref, sc_ref, dd_ref, dw_ref, db_ref, dsc_ref):
        i = pl.program_id(1)

        @pl.when(i == 0)
        def _():
            dw_ref[...] = jnp.zeros_like(dw_ref)
            db_ref[...] = jnp.zeros_like(db_ref)
            dsc_ref[...] = jnp.zeros_like(dsc_ref)

        dfb = diff_ref[...].astype(MXU_DTYPE)
        ypre = jnp.dot(dfb, w_ref[0], preferred_element_type=F32) + b_ref[...]
        dxb = dx_ref[...]
        dy = dxb * sc_ref[...]
        dsc_ref[...] += jnp.sum(dxb * ypre, axis=0, keepdims=True)
        db_ref[...] += jnp.sum(dy, axis=0, keepdims=True)
        dyb = dy.astype(MXU_DTYPE)
        dw_ref[0] += lax.dot_general(dfb, dyb, (((0,), (0,)), ((), ())), preferred_element_type=F32)
        dd_ref[...] = lax.dot_general(dyb, w_ref[0], (((1,), (1,)), ((), ())), preferred_element_type=F32)

    blk = pl.BlockSpec((t, POOL_WIDTH), lambda gi, i: (i, gi))
    vec = pl.BlockSpec((1, POOL_WIDTH), lambda gi, i: (0, gi))
    mat = pl.BlockSpec((1, POOL_WIDTH, POOL_WIDTH), lambda gi, i: (gi, 0, 0))
    ddiff, dw, db, dsc = pl.pallas_call(
        body1, name="pool_bwd_map", grid=(N_POOL_GROUPS, nb), in_specs=[blk, blk, mat, vec, vec],
        out_specs=[blk, mat, vec, vec],
        out_shape=[jax.ShapeDtypeStruct(x.shape, F32), jax.ShapeDtypeStruct(w_grp.shape, F32),
                   jax.ShapeDtypeStruct((1, D_MODEL), F32), jax.ShapeDtypeStruct((1, D_MODEL), F32)],
        compiler_params=_params(("parallel", "arbitrary")))(dx1, diff, w_grp, b_grp, scale)

    def body2(dc_ref, dn_ref, dh_ref):
        gi, i = pl.program_id(0), pl.program_id(1)
        inside, before, inv_count = _pool_bands(gi, i, t)
        tn = (((0,), (0,)), ((), ()))
        dc = dc_ref[...]
        tot = lax.dot_general(inside, (dc * inv_count(i)).astype(MXU_DTYPE), tn, preferred_element_type=F32)
        nxt = lax.dot_general(before, (dn_ref[...] * inv_count(i + 1)).astype(MXU_DTYPE), tn, preferred_element_type=F32)
        dh_ref[...] = tot + jnp.where(i < nb - 1, nxt, 0.0) - dc

    dh = pl.pallas_call(
        body2, name="pool_bwd_window", grid=(N_POOL_GROUPS, nb),
        in_specs=[blk, pl.BlockSpec((t, POOL_WIDTH), lambda gi, i: (jnp.minimum(i + 1, nb - 1), gi))],
        out_specs=blk, out_shape=jax.ShapeDtypeStruct(x.shape, F32),
        compiler_params=_params(("parallel", "parallel")))(ddiff, ddiff)
    dx, dgain = _rms_bwd(x, gain, dh, dx1, "mix_norm_bwd")
    return dx, dgain, dw, db, dsc


MESH_ID = pl.DeviceIdType.MESH
ANY = pl.BlockSpec(memory_space=pl.ANY)


def _place():
    x, y, c = lax.axis_index("x"), lax.axis_index("y"), lax.axis_index("c")
    other_chips = [(1 - x, y), (x, 1 - y), (1 - x, 1 - y)]
    return x, y, c, other_chips


def _chip_index(chip):
    return 2 * chip[0] + chip[1]


def _remote(src, dst, send_sems, recv_sems, n, to):
    return pltpu.make_async_remote_copy(src_ref=src, dst_ref=dst, send_sem=send_sems.at[n], recv_sem=recv_sems.at[n],
                                        device_id=to, device_id_type=MESH_ID)


def _gather_weights(ws):
    n = len(ws)
    from_x, relay_x, from_y, relay_y, sib_x, sib_y, sib_d0, sib_d1, sib_own = range(9)
    slots = 9

    def body(*refs):
        w_refs, out_refs = refs[:n], refs[n:2 * n]
        send_sems, recv_sems = refs[2 * n:]
        x, y, c, (xn, yn, dn) = _place()
        k = _chip_index((x, y))
        me, sibling = (x, y, c), (x, y, 1 - c)

        def copy(ref, t, slot, to):
            return _remote(ref, ref, send_sems, recv_sems, slots * t + slot, to)

        def quarter(ref, q):
            rows = ref.shape[0] // 2
            return ref.at[pl.ds(q * rows, rows)]

        started = []

        def start(cp):
            cp.start()
            started.append(cp)

        for t in range(n):
            for slot, chip in ((from_x, xn), (from_y, yn)):
                start(_remote(w_refs[t].at[c], out_refs[t].at[k, c], send_sems, recv_sems, slots * t + slot, (*chip, c)))
            start(_remote(w_refs[t], out_refs[t].at[k], send_sems, recv_sems, slots * t + sib_own, sibling))
        for t in range(n):
            got = out_refs[t].at[_chip_index(xn), c]
            copy(got, t, from_x, me).wait_recv()
            start(copy(quarter(got, 0), t, relay_y, (*yn, c)))
            start(copy(got, t, sib_x, sibling))
        for t in range(n):
            got = out_refs[t].at[_chip_index(yn), c]
            copy(got, t, from_y, me).wait_recv()
            start(copy(quarter(got, 1), t, relay_x, (*xn, c)))
            start(copy(got, t, sib_y, sibling))
        for t in range(n):
            got = out_refs[t].at[_chip_index(dn), c]
            for q, slot, sib_slot in ((0, relay_y, sib_d0), (1, relay_x, sib_d1)):
                copy(quarter(got, q), t, slot, me).wait_recv()
                start(copy(quarter(got, q), t, sib_slot, sibling))
        for t in range(n):
            for chip, slot in ((xn, sib_x), (yn, sib_y)):
                copy(out_refs[t].at[_chip_index(chip), 1 - c], t, slot, me).wait_recv()
            theirs = out_refs[t].at[_chip_index(dn), 1 - c]
            copy(quarter(theirs, 0), t, sib_d0, me).wait_recv()
            copy(quarter(theirs, 1), t, sib_d1, me).wait_recv()
            copy(out_refs[t].at[k], t, sib_own, me).wait_recv()
        for cp in started:
            cp.wait_send()

    return pl.pallas_call(
        body, name="gather_weights", in_specs=[ANY] * n, out_specs=[ANY] * n,
        out_shape=[jax.ShapeDtypeStruct((N_CHIPS,) + w.shape, w.dtype) for w in ws],
        scratch_shapes=[pltpu.SemaphoreType.DMA((slots * n,)), pltpu.SemaphoreType.DMA((slots * n,))],
    )(*ws)


def _swap_halves(gs):
    n = len(gs)

    def body(*refs):
        g_refs, out_refs = refs[:n], refs[n:2 * n]
        send_sems, recv_sems = refs[2 * n:]
        x, y, c, _ = _place()
        copies = [_remote(g_refs[t].at[s, 1 - c], out_refs[t].at[s], send_sems, recv_sems, N_CHIPS * t + s, (x, y, 1 - c))
                  for t in range(n) for s in range(N_CHIPS)]
        for cp in copies:
            cp.start()
        for cp in copies:
            cp.wait_recv()
        for cp in copies:
            cp.wait_send()

    return pl.pallas_call(
        body, name="swap_halves", in_specs=[ANY] * n, out_specs=[ANY] * n,
        out_shape=[jax.ShapeDtypeStruct((N_CHIPS,) + g.shape[2:], g.dtype) for g in gs],
        scratch_shapes=[pltpu.SemaphoreType.DMA((N_CHIPS * n,)), pltpu.SemaphoreType.DMA((N_CHIPS * n,))],
    )(*gs)


def _neighbour_exchange(srcs, out_shapes, name):
    n = len(out_shapes)

    def body(*refs):
        in_refs, out_refs = refs[:n], refs[n:2 * n]
        send_sems, recv_sems = refs[2 * n:]
        x, y, c, chips = _place()
        sends = []
        for t in range(n):
            for slot, (src, chip) in enumerate(zip(srcs(in_refs[t], chips), chips[:2])):
                sends.append(_remote(src, out_refs[t].at[slot], send_sems, recv_sems, 2 * t + slot, (*chip, c)))
        for cp in sends:
            cp.start()
        for t in range(n):
            for slot in range(2):
                landed = out_refs[t].at[slot]
                _remote(landed, landed, send_sems, recv_sems, 2 * t + slot, (x, y, c)).wait_recv()
        for cp in sends:
            cp.wait_send()

    return pl.pallas_call(
        body, name=name, in_specs=[ANY] * n, out_specs=[ANY] * n, out_shape=out_shapes,
        scratch_shapes=[pltpu.SemaphoreType.DMA((2 * n,)), pltpu.SemaphoreType.DMA((2 * n,))])


def _relay_partials(ps):
    def srcs(p_ref, chips):
        half = p_ref.shape[1] // 2
        diagonal = p_ref.at[_chip_index(chips[2])]
        return diagonal.at[pl.ds(0, half)], diagonal.at[pl.ds(half, half)]

    shapes = [jax.ShapeDtypeStruct((2, p.shape[1] // 2, p.shape[2]), p.dtype) for p in ps]
    return _neighbour_exchange(srcs, shapes, "relay_partials")(*ps)


def _merge_relayed(p, relayed, targets, name):
    rows, cols = p.shape[1:]
    tr = _row_tile(rows // 2, SUM_ROWS)
    per = rows // 2 // tr

    def body(x_ref, y_ref, p_ref, r_ref, o_ref):
        to, q = pl.program_id(0), pl.program_id(1)
        extra = jnp.where(q == 1 - to, r_ref[0].astype(F32), 0.0)
        o_ref[0] = (p_ref[0].astype(F32) + extra).astype(o_ref.dtype)

    return pl.pallas_call(
        body, name=name,
        grid_spec=pltpu.PrefetchScalarGridSpec(
            num_scalar_prefetch=2, grid=(2, 2, per),
            in_specs=[pl.BlockSpec((1, tr, cols), lambda to, q, i, xs, ys: (jnp.where(to == 0, xs[0], ys[0]), q * per + i, 0)),
                      pl.BlockSpec((1, tr, cols), lambda to, q, i, xs, ys: (1 - to, i, 0))],
            out_specs=pl.BlockSpec((1, tr, cols), lambda to, q, i, xs, ys: (to, q * per + i, 0))),
        out_shape=jax.ShapeDtypeStruct((2, rows, cols), p.dtype),
        compiler_params=_params(("parallel", "parallel", "parallel")))(targets[0], targets[1], p, relayed)


def _scatter_partials(ms):
    shapes = [jax.ShapeDtypeStruct(m.shape, m.dtype) for m in ms]
    return _neighbour_exchange(lambda m_ref, chips: (m_ref.at[0], m_ref.at[1]), shapes, "scatter_partials")(*ms)


def _share_half(fs):
    n = len(fs)

    def body(*refs):
        f_refs, out_refs = refs[:n], refs[n:2 * n]
        send_sems, recv_sems = refs[2 * n:]
        x, y, c, _ = _place()
        sends = [_remote(f_refs[t], out_refs[t].at[c], send_sems, recv_sems, t, (x, y, 1 - c)) for t in range(n)]
        for cp in sends:
            cp.start()
        for t in range(n):
            theirs = out_refs[t].at[1 - c]
            _remote(theirs, theirs, send_sems, recv_sems, t, (x, y, c)).wait_recv()
        for cp in sends:
            cp.wait_send()

    return pl.pallas_call(
        body, name="share_half", in_specs=[ANY] * n, out_specs=[ANY] * n,
        out_shape=[jax.ShapeDtypeStruct((2,) + f.shape, f.dtype) for f in fs],
        scratch_shapes=[pltpu.SemaphoreType.DMA((n,)), pltpu.SemaphoreType.DMA((n,))],
    )(*fs)


def _gather_small(v):
    def body(v_ref, out_ref, send_sems, recv_sems):
        x, y, c, chips = _place()
        me, sibling = (x, y, c), (x, y, 1 - c)

        def slot(px, py, pc):
            return out_ref.at[4 * px + 2 * py + pc]

        first = [_remote(v_ref, slot(*me), send_sems, recv_sems, 0, sibling)]
        first += [_remote(v_ref, slot(*me), send_sems, recv_sems, 1 + j, (*chip, c)) for j, chip in enumerate(chips)]
        for cp in first:
            cp.start()
        passed = [_remote(slot(*chip, c), slot(*chip, c), send_sems, recv_sems, 4 + j, sibling)
                  for j, chip in enumerate(chips)]
        for j, chip in enumerate(chips):
            _remote(slot(*chip, c), slot(*chip, c), send_sems, recv_sems, 1 + j, me).wait_recv()
            passed[j].start()
        _remote(slot(*sibling), slot(*sibling), send_sems, recv_sems, 0, me).wait_recv()
        for j, chip in enumerate(chips):
            _remote(slot(*chip, 1 - c), slot(*chip, 1 - c), send_sems, recv_sems, 4 + j, me).wait_recv()
        for cp in first + passed:
            cp.wait_send()

    gathered = pl.pallas_call(
        body, name="gather_small", in_specs=[ANY], out_specs=ANY,
        out_shape=jax.ShapeDtypeStruct((N_DEV,) + v.shape, v.dtype),
        scratch_shapes=[pltpu.SemaphoreType.DMA((7,)), pltpu.SemaphoreType.DMA((7,))],
    )(v)
    device = 4 * lax.axis_index("x") + 2 * lax.axis_index("y") + lax.axis_index("c")
    return lax.dynamic_update_index_in_dim(gathered, v, device, 0)


SMALL_ROWS = 256
SUM_ROWS = 256
BF16_ROWS = 16


def _row_tile(rows, want):
    for t in range(min(rows, want) // BF16_ROWS * BF16_ROWS, 0, -BF16_ROWS):
        if rows % t == 0:
            return t
    return rows


def _pair_sum(g, r, core, name):
    rows, cols = g.shape[2:]
    tr = _row_tile(rows, SUM_ROWS)

    def body(c_ref, g_ref, r_ref, o_ref):
        o_ref[0] = (g_ref[0, 0].astype(F32) + r_ref[0].astype(F32)).astype(o_ref.dtype)

    return pl.pallas_call(
        body, name=name,
        grid_spec=pltpu.PrefetchScalarGridSpec(
            num_scalar_prefetch=1, grid=(N_CHIPS, rows // tr),
            in_specs=[pl.BlockSpec((1, 1, tr, cols), lambda s, i, c_ref: (s, c_ref[0], i, 0)),
                      pl.BlockSpec((1, tr, cols), lambda s, i, c_ref: (s, i, 0))],
            out_specs=pl.BlockSpec((1, tr, cols), lambda s, i, c_ref: (s, i, 0))),
        out_shape=jax.ShapeDtypeStruct(r.shape, BF16),
        compiler_params=_params(("parallel", "parallel")))(core, g, r)


def _chip_sum(p, recv, chip, name):
    rows, cols = p.shape[1:]
    tr = _row_tile(rows, SUM_ROWS)
    n_recv = recv.shape[0]

    def body(k_ref, p_ref, r_ref, o_ref):
        acc = p_ref[0].astype(F32)
        for j in range(n_recv):
            acc = acc + r_ref[j].astype(F32)
        o_ref[...] = acc

    return pl.pallas_call(
        body, name=name,
        grid_spec=pltpu.PrefetchScalarGridSpec(
            num_scalar_prefetch=1, grid=(rows // tr,),
            in_specs=[pl.BlockSpec((1, tr, cols), lambda i, k_ref: (k_ref[0], i, 0)),
                      pl.BlockSpec((n_recv, tr, cols), lambda i, k_ref: (0, i, 0))],
            out_specs=pl.BlockSpec((tr, cols), lambda i, k_ref: (i, 0))),
        out_shape=jax.ShapeDtypeStruct((rows, cols), F32),
        compiler_params=_params(("parallel",)))(chip, p, recv)


def _sum_blocks(a, name):
    n, rows, cols = a.shape
    tr = _row_tile(rows, SUM_ROWS)

    def body(a_ref, o_ref):
        acc = a_ref[0].astype(F32)
        for s in range(1, n):
            acc = acc + a_ref[s].astype(F32)
        o_ref[...] = acc

    return pl.pallas_call(
        body, name=name, grid=(rows // tr,),
        in_specs=[pl.BlockSpec((n, tr, cols), lambda i: (0, i, 0))],
        out_specs=pl.BlockSpec((tr, cols), lambda i: (i, 0)),
        out_shape=jax.ShapeDtypeStruct((rows, cols), F32),
        compiler_params=_params(("parallel",)))(a)


def _adamw(w, g, m, v, name):
    def fn(wb, gb, mb, vb):
        m2 = ADAM_B1 * mb + (1.0 - ADAM_B1) * gb
        v2 = ADAM_B2 * vb + (1.0 - ADAM_B2) * (gb * gb)
        m_hat = m2 / (1.0 - ADAM_B1 ** ADAM_STEP)
        v_hat = v2 / (1.0 - ADAM_B2 ** ADAM_STEP)
        delta = -ADAM_LR * (m_hat / (jnp.sqrt(v_hat) + ADAM_EPS) + ADAM_WD * wb)
        return (delta, m2, v2), ()
    c = w.shape[1]
    return _ew(fn, [w, g, m, v], outs=[(c, F32)] * 3, name=name)


WEIGHTS = ["mix_norm_g", "ffn_norm_g", "final_norm_g", "fox_w_in", "fox_b_f", "fox_w_out", "s5_w_in", "s5_a_re",
           "s5_a_im", "s5_log_dt", "s5_b_re", "s5_b_im", "s5_c_re", "s5_c_im", "s5_d", "s5_w_glu", "pool_w",
           "pool_b", "pool_scale", "ffn_w_gate_up", "ffn_w_down"]
BIG = {"fox_w_in": 2, "fox_w_out": 1, "s5_w_in": 1, "s5_w_glu": 2, "pool_w": 2, "ffn_w_gate_up": 2, "ffn_w_down": 1}
BY_CHIP = ("ffn_w_gate_up", "ffn_w_down")
SLICED = ("pool_b", "pool_scale")
SMALL = [n for n in WEIGHTS if n not in BIG]


def _to_natural(cm, axis):
    moved = jnp.moveaxis(cm, 0, axis)
    shape = moved.shape[:axis] + (moved.shape[axis] * moved.shape[axis + 1],) + moved.shape[axis + 2:]
    return moved.reshape(shape)


def _to_chip_major(nat, axis):
    shape = nat.shape[:axis] + (N_CHIPS, nat.shape[axis] // N_CHIPS) + nat.shape[axis + 1:]
    return jnp.moveaxis(nat.reshape(shape), axis, 0)


def _halves_view(shape):
    return (2, int(np.prod(shape[:-1])) // 2, shape[-1])


def _pack_small(parts):
    flat = jnp.concatenate([p.reshape(-1).astype(F32) for p in parts])
    pad = (-flat.shape[0]) % (SMALL_ROWS * LANES)
    return jnp.pad(flat, (0, pad)).reshape(-1, LANES)


def _unpack_small(buf, shapes):
    flat = buf.reshape(-1)
    out, off = [], 0
    for shp in shapes:
        n = int(np.prod(shp))
        out.append(flat[off:off + n].reshape(shp))
        off += n
    return out


def _local_step(x, target, w):
    grads = {}
    mixers = ("fox", "s5", "pool")
    saved = []
    ssm, ssm_pull = jax.vjp(_s5_operands, w["s5_a_re"][0], w["s5_a_im"][0], w["s5_log_dt"][0], w["s5_b_re"][0],
                            w["s5_b_im"][0], w["s5_c_re"][0], w["s5_c_im"][0])
    fox_w = []
    for j in range(w["fox_w_in"].shape[0]):
        w_in = w["fox_w_in"][j]
        w_f = jnp.pad(w_in[:, 3 * D_MODEL:], ((0, 0), (0, LANES - FOX_HEADS)))
        fox_w.append((w_in[:, :3 * D_MODEL], w_f, w["fox_w_out"][j]))
    for i in range(DEPTH):
        kind, j = mixers[i % 3], i // 3
        gain = w["mix_norm_g"][i]
        if kind == "fox":
            x1, sv = _fox_fwd(x, gain, fox_w[j][0], fox_w[j][1], w["fox_b_f"][j], fox_w[j][2])
        elif kind == "s5":
            x1, sv = _s5_fwd(x, gain, w["s5_w_in"][j], ssm, w["s5_d"], w["s5_w_glu"][j])
        else:
            x1, sv = _pool_fwd(x, gain, w["pool_w"][j], w["pool_b"], w["pool_scale"])
        x, sf = _ffn_fwd(x1, w["ffn_norm_g"][i], w["ffn_w_gate_up"], w["ffn_w_down"], i)
        saved.append((sv, sf))
    loss, dx, grads["final_norm_g"] = _loss_head(x, w["final_norm_g"], target)
    per_layer = {n: [None] * DEPTH for n in ("mix_norm_g", "ffn_norm_g")}
    fox_g = {n: [None] * len(fox_w) for n in ("fox_w_in", "fox_b_f", "fox_w_out")}
    for n in BY_CHIP:
        grads[n] = lax.empty(w[n].shape, BF16)
    for i in reversed(range(DEPTH)):
        kind, j = mixers[i % 3], i // 3
        sv, sf = saved[i]
        dx, per_layer["ffn_norm_g"][i], grads["ffn_w_gate_up"], grads["ffn_w_down"] = _ffn_bwd(
            dx, sf, w["ffn_norm_g"][i], w["ffn_w_gate_up"], w["ffn_w_down"], i, grads["ffn_w_gate_up"], grads["ffn_w_down"])
        gain = w["mix_norm_g"][i]
        if kind == "fox":
            dx, per_layer["mix_norm_g"][i], fox_g["fox_w_in"][j], fox_g["fox_b_f"][j], fox_g["fox_w_out"][j] = _fox_bwd(
                dx, sv, gain, fox_w[j][0], fox_w[j][1], fox_w[j][2])
        elif kind == "s5":
            dx, per_layer["mix_norm_g"][i], dw_in, dssm, dd, dw_glu = _s5_bwd(
                dx, sv, gain, w["s5_w_in"][j], ssm, w["s5_d"], w["s5_w_glu"][j])
            grads["s5_w_in"], grads["s5_w_glu"], grads["s5_d"] = dw_in[None], dw_glu[None], dd
            for n, g in zip(("s5_a_re", "s5_a_im", "s5_log_dt", "s5_b_re", "s5_b_im", "s5_c_re", "s5_c_im"), ssm_pull(dssm)):
                grads[n] = g[None]
        else:
            dx, per_layer["mix_norm_g"][i], dw, db, dsc = _pool_bwd(dx, sv, gain, w["pool_w"][j], w["pool_b"], w["pool_scale"])
            grads["pool_w"], grads["pool_b"], grads["pool_scale"] = dw[None].astype(BF16), db, dsc
    for n, parts in {**per_layer, **fox_g}.items():
        grads[n] = jnp.stack(parts)
    return loss, dx, grads


def kernel(x, mix_norm_g, ffn_norm_g, final_norm_g, fox_w_in, fox_b_f, fox_w_out, s5_w_in, s5_a_re, s5_a_im, s5_log_dt, s5_b_re, s5_b_im, s5_c_re, s5_c_im, s5_d, s5_w_glu, pool_w, pool_b, pool_scale, ffn_w_gate_up, ffn_w_down, loss_target, m_mix_norm_g, m_ffn_norm_g, m_final_norm_g, m_fox_w_in, m_fox_b_f, m_fox_w_out, m_s5_w_in, m_s5_a_re, m_s5_a_im, m_s5_log_dt, m_s5_b_re, m_s5_b_im, m_s5_c_re, m_s5_c_im, m_s5_d, m_s5_w_glu, m_pool_w, m_pool_b, m_pool_scale, m_ffn_w_gate_up, m_ffn_w_down, v_mix_norm_g, v_ffn_norm_g, v_final_norm_g, v_fox_w_in, v_fox_b_f, v_fox_w_out, v_s5_w_in, v_s5_a_re, v_s5_a_im, v_s5_log_dt, v_s5_b_re, v_s5_b_im, v_s5_c_re, v_s5_c_im, v_s5_d, v_s5_w_glu, v_pool_w, v_pool_b, v_pool_scale, v_ffn_w_gate_up, v_ffn_w_down):
    given = dict(locals())
    shard = {n: given[n] for n in WEIGHTS}
    chip = 2 * lax.axis_index("x") + lax.axis_index("y")
    core = lax.axis_index("c")

    views = {n: _halves_view(shard[n].shape) for n in BIG}
    own = [shard[n].astype(MXU_DTYPE).reshape(views[n]) for n in BIG]
    whole = {}
    for n, by_chip in zip(BIG, _gather_weights(own)):
        by_chip = by_chip.reshape((N_CHIPS,) + shard[n].shape)
        whole[n] = by_chip if n in BY_CHIP else _to_natural(by_chip, BIG[n])
    for n in SMALL:
        whole[n] = shard[n]
    sliced_shapes = [shard[n].shape for n in SLICED]
    by_chip = _gather_small(_pack_small([shard[n] for n in SLICED]))[0::2]
    slices = [_unpack_small(by_chip[k], sliced_shapes) for k in range(N_CHIPS)]
    for idx, n in enumerate(SLICED):
        whole[n] = jnp.concatenate([slices[k][idx] for k in range(N_CHIPS)], axis=-1)

    loss_part, dx, grads = _local_step(x[0], loss_target[0], whole)
    loss = lax.psum(loss_part, MESH_AXES)

    gs = [(grads[n] if n in BY_CHIP else _to_chip_major(grads[n].astype(BF16), BIG[n])).reshape((N_CHIPS,) + views[n])
          for n in BIG]
    core_id, chip_id = core.reshape(1).astype(jnp.int32), chip.reshape(1).astype(jnp.int32)
    partial = [_pair_sum(g, r, core_id, "pair_sum_" + n) for n, g, r in zip(BIG, gs, _swap_halves(gs))]
    x_chip, y_chip = 2 * (1 - lax.axis_index("x")) + lax.axis_index("y"), 2 * lax.axis_index("x") + 1 - lax.axis_index("y")
    neighbours = (x_chip.reshape(1).astype(jnp.int32), y_chip.reshape(1).astype(jnp.int32))
    merged = [_merge_relayed(p, r, neighbours, "merge_relayed_" + n) for n, p, r in zip(BIG, partial, _relay_partials(partial))]
    half = [_chip_sum(p, r, chip_id, "chip_sum_" + n) for n, p, r in zip(BIG, partial, _scatter_partials(merged))]
    grad = {n: lax.dynamic_update_index_in_dim(both, mine, core, 0).reshape(shard[n].shape)
            for n, mine, both in zip(BIG, half, _share_half(half))}

    small_sum = _sum_blocks(_gather_small(_pack_small([grads[n] for n in SMALL])), "small_sum")
    for n, g in zip(SMALL, _unpack_small(small_sum, [whole[n].shape for n in SMALL])):
        grad[n] = g
    for n in SLICED:
        width = shard[n].shape[-1]
        grad[n] = lax.dynamic_slice_in_dim(grad[n], chip * width, width, axis=-1)

    delta, new_m, new_v = {}, {}, {}
    for n in BIG:
        view = (-1, shard[n].shape[-1])
        res = _adamw(shard[n].reshape(view), grad[n].reshape(view), given["m_" + n].reshape(view),
                     given["v_" + n].reshape(view), "adamw_" + n)
        delta[n], new_m[n], new_v[n] = (r.reshape(shard[n].shape) for r in res)
    small_shapes = [shard[n].shape for n in SMALL]
    res = _adamw(_pack_small([shard[n] for n in SMALL]), _pack_small([grad[n] for n in SMALL]),
                 _pack_small([given["m_" + n] for n in SMALL]), _pack_small([given["v_" + n] for n in SMALL]), "adamw_small")
    for out, buf in zip((delta, new_m, new_v), res):
        for n, a in zip(SMALL, _unpack_small(buf, small_shapes)):
            out[n] = a
    return (loss, dx[None], *[grad[n] for n in WEIGHTS], *[delta[n] for n in WEIGHTS],
            *[new_m[n] for n in WEIGHTS], *[new_v[n] for n in WEIGHTS])
```

```python
import functools
import math

import jax
import jax.numpy as jnp
import numpy as np
from jax import lax
from jax.experimental import pallas as pl
from jax.experimental.pallas import tpu as pltpu

F32 = jnp.float32
BF16 = jnp.bfloat16
MXU_DTYPE = jnp.bfloat16

D_MODEL = 1024
DEPTH = 4
EPS = 1e-6
FOX_HEADS = 16
FOX_HEAD_DIM = 64
HEAD_PAIRS = FOX_HEADS // 2
S5_GROUPS = 64
S5_GROUP = 16
S5_STATE = 64
S5_BLOCKS = 8
S5_HALF = 256
POOL_WINDOWS = (2, 4, 8, 16)
POOL_WIDTH = 256
D_FF = 2816
N_CHIPS = 4
N_DEV = 8
LANES = 128
SUBLANES = 8
VMEM_LIMIT = 56 * 1024 * 1024

ADAM_LR = 0.001
ADAM_B1 = 0.9
ADAM_B2 = 0.999
ADAM_EPS = 1e-08
ADAM_WD = 0.01
ADAM_STEP = 10

MESH_AXES = ("x", "y", "c")


def _tile(n, want):
    t = (min(n, want) // LANES) * LANES
    while t >= LANES:
        if n % t == 0:
            return t
        t -= LANES
    return n


def _params(sem=None):
    return pltpu.CompilerParams(dimension_semantics=sem, vmem_limit_bytes=VMEM_LIMIT)


def _mm(a, b, *, name, ta=False, tb=False, out_dtype=F32, add=None, tm=1024, tn=1024, tk=1024,
        b_tiles=None, out_tiles=None, into=None):
    m, k = (a.shape[1], a.shape[0]) if ta else a.shape
    if b_tiles is None:
        n = b.shape[0] if tb else b.shape[1]
        assert (b.shape[1] if tb else b.shape[0]) == k, (a.shape, b.shape, ta, tb)
    else:
        assert b_tiles[0] == k, (a.shape, b_tiles[0])
        n = b_tiles[1]
    tm, tn, tk = _tile(m, tm), _tile(n, tn), _tile(k, tk)
    nk = k // tk
    a_spec = pl.BlockSpec((tk, tm), lambda i, j, kk: (kk, i)) if ta else pl.BlockSpec((tm, tk), lambda i, j, kk: (i, kk))
    b_shape = (tn, tk) if tb else (tk, tn)
    if b_tiles is None:
        b_spec = pl.BlockSpec(b_shape, (lambda i, j, kk: (j, kk)) if tb else (lambda i, j, kk: (kk, j)))
    else:
        b_spec = pl.BlockSpec(b_tiles[2], b_tiles[3])
    add_spec = pl.BlockSpec((tm, tn), lambda i, j, kk: (i, j))
    if out_tiles is None:
        o_spec, o_struct = add_spec, jax.ShapeDtypeStruct((m, n), out_dtype)
    else:
        o_spec, o_struct = pl.BlockSpec(out_tiles[1], out_tiles[2]), jax.ShapeDtypeStruct(out_tiles[0], out_dtype)
    dims = (((0 if ta else 1,), (1 if tb else 0,)), ((), ()))
    has_add, has_into = add is not None, into is not None

    def body(*refs):
        a_ref, b_ref = refs[:2]
        add_ref = refs[2] if has_add else None
        o_ref, acc_ref = refs[-2:]
        kk = pl.program_id(2)

        @pl.when(kk == 0)
        def _():
            acc_ref[...] = jnp.zeros_like(acc_ref)

        acc_ref[...] += lax.dot_general(a_ref[...].astype(MXU_DTYPE), b_ref[...].reshape(b_shape).astype(MXU_DTYPE), dims,
                                        preferred_element_type=F32)

        @pl.when(kk == nk - 1)
        def _():
            r = acc_ref[...]
            if has_add:
                r = r + add_ref[...].astype(F32)
            o_ref[...] = r.astype(out_dtype).reshape(o_ref.shape)

    ins = [a, b] + ([add] if has_add else []) + ([into] if has_into else [])
    specs = [a_spec, b_spec] + ([add_spec] if has_add else []) + ([pl.BlockSpec(memory_space=pl.ANY)] if has_into else [])
    return pl.pallas_call(
        body, name=name, grid=(m // tm, n // tn, nk), in_specs=specs, out_specs=o_spec,
        out_shape=o_struct, scratch_shapes=[pltpu.VMEM((tm, tn), F32)],
        input_output_aliases={len(ins) - 1: 0} if has_into else {},
        compiler_params=_params(("parallel", "parallel", "arbitrary")))(*ins)


def _ew(fn, tens, vecs=(), *, outs=(), sums=(), name, tr=256):
    tens = [t if isinstance(t, tuple) else (t, t.shape[1], 0) for t in tens]
    rows = tens[0][0].shape[0]
    tr = min(tr, rows)
    n_t, n_v, n_o, n_s = len(tens), len(vecs), len(outs), len(sums)

    def body(*refs):
        i = pl.program_id(0)
        t_blocks = [r[...] for r in refs[:n_t]]
        v_blocks = [r[...] for r in refs[n_t:n_t + n_v]]
        o_refs = refs[n_t + n_v:n_t + n_v + n_o]
        s_refs = refs[n_t + n_v + n_o:]
        o_vals, s_vals = fn(*t_blocks, *v_blocks)
        for r, v in zip(o_refs, o_vals):
            r[...] = v.astype(r.dtype)
        if n_s:
            @pl.when(i == 0)
            def _():
                for r in s_refs:
                    r[...] = jnp.zeros_like(r)
            for r, v in zip(s_refs, s_vals):
                r[...] += jnp.sum(v.astype(F32), axis=0, keepdims=True)

    in_specs = [pl.BlockSpec((tr, w), functools.partial(lambda i, cb: (i, cb), cb=cb)) for _, w, cb in tens]
    in_specs += [pl.BlockSpec(v.shape, functools.partial(lambda i, nd: (0,) * nd, nd=v.ndim)) for v in vecs]
    out_specs = [pl.BlockSpec((tr, c), lambda i: (i, 0)) for c, _ in outs]
    out_specs += [pl.BlockSpec((1, c), lambda i: (0, 0)) for c in sums]
    out_shape = [jax.ShapeDtypeStruct((rows, c), dt) for c, dt in outs]
    out_shape += [jax.ShapeDtypeStruct((1, c), F32) for c in sums]
    res = pl.pallas_call(
        body, name=name, grid=(rows // tr,), in_specs=in_specs, out_specs=out_specs, out_shape=out_shape,
        compiler_params=_params(("arbitrary",)))(*[t[0] for t in tens], *vecs)
    return res


def _sigmoid(z):
    return 1.0 / (1.0 + jnp.exp(-z))


def _rms_fwd(x, g, name):
    def fn(xb, gb):
        r = lax.rsqrt(jnp.mean(xb * xb, axis=-1, keepdims=True) + EPS)
        return ((xb * r) * gb,), ()
    return _ew(fn, [x], [g.reshape(1, -1)], outs=[(x.shape[1], BF16)], name=name)[0]


def _rms_bwd(x, g, dh, dres, name):
    def fn(xb, dhb, drb, gb):
        r = lax.rsqrt(jnp.mean(xb * xb, axis=-1, keepdims=True) + EPS)
        xh = xb * r
        dhf = dhb.astype(F32)
        dy = dhf * gb
        dx = r * (dy - xh * jnp.mean(dy * xh, axis=-1, keepdims=True))
        return (drb + dx,), (dhf * xh,)
    dx, dg = _ew(fn, [x, dh, dres], [g.reshape(1, -1)], outs=[(x.shape[1], F32)], sums=[x.shape[1]], name=name)
    return dx, dg[0]


def _mm_rms_bwd(a, b, x, g, dres, *, name, add=None, tm=512, tk=1024, b_tiles=None):
    m, k = a.shape
    n = x.shape[1]
    assert x.shape == (m, n) and (b_tiles is not None or b.shape == (n, k))
    tm, tk = _tile(m, tm), _tile(k, tk)
    nk = k // tk
    has_add = add is not None

    def body(*refs):
        a_ref, b_ref, x_ref, dr_ref, g_ref = refs[:5]
        add_ref = refs[5] if has_add else None
        dx_ref, dg_ref, acc_ref = refs[-3:]
        i, kk = pl.program_id(0), pl.program_id(1)

        @pl.when(kk == 0)
        def _():
            acc_ref[...] = jnp.zeros_like(acc_ref)

        @pl.when((kk == 0) & (i == 0))
        def _():
            dg_ref[...] = jnp.zeros_like(dg_ref)

        acc_ref[...] += lax.dot_general(a_ref[...].astype(MXU_DTYPE), b_ref[...].reshape((n, tk)).astype(MXU_DTYPE),
                                        (((1,), (1,)), ((), ())), preferred_element_type=F32)

        @pl.when(kk == nk - 1)
        def _():
            dh = acc_ref[...]
            if has_add:
                dh = dh + add_ref[...]
            xb = x_ref[...]
            r = lax.rsqrt(jnp.mean(xb * xb, axis=-1, keepdims=True) + EPS)
            xh = xb * r
            dy = dh * g_ref[...]
            dx_ref[...] = dr_ref[...] + r * (dy - xh * jnp.mean(dy * xh, axis=-1, keepdims=True))
            dg_ref[...] += jnp.sum(dh * xh, axis=0, keepdims=True)

    row = pl.BlockSpec((tm, n), lambda i, kk: (i, 0))
    vec = pl.BlockSpec((1, n), lambda i, kk: (0, 0))
    ins = [a, b, x, dres, g.reshape(1, n)] + ([add] if has_add else [])
    b_spec = pl.BlockSpec((n, tk), lambda i, kk: (0, kk)) if b_tiles is None else pl.BlockSpec(*b_tiles)
    specs = [pl.BlockSpec((tm, tk), lambda i, kk: (i, kk)), b_spec, row, row, vec]
    specs += [row] if has_add else []
    dx, dg = pl.pallas_call(
        body, name=name, grid=(m // tm, nk), in_specs=specs, out_specs=[row, vec],
        out_shape=[jax.ShapeDtypeStruct((m, n), F32), jax.ShapeDtypeStruct((1, n), F32)],
        scratch_shapes=[pltpu.VMEM((tm, n), F32)],
        compiler_params=_params(("arbitrary", "arbitrary")))(*ins)
    return dx, dg[0]


FFN_ROWS = 512
FFN_COLS = D_FF // 2


def _ffn_gate_up(x, gain, w_gu, layer):
    s, d = x.shape
    tm = min(FFN_ROWS, s)
    halves = D_FF // FFN_COLS

    def body(x_ref, gain_ref, wg_ref, wu_ref, h_ref, g_ref, u_ref, a_ref):
        @pl.when(pl.program_id(1) == 0)
        def _():
            xb = x_ref[...]
            h_ref[...] = ((xb * lax.rsqrt(jnp.mean(xb * xb, axis=-1, keepdims=True) + EPS)) * gain_ref[...]).astype(h_ref.dtype)

        hb = h_ref[...].astype(MXU_DTYPE)
        g = jnp.dot(hb, wg_ref[0, 0], preferred_element_type=F32)
        u = jnp.dot(hb, wu_ref[0, 0], preferred_element_type=F32)
        g_ref[...] = g.astype(g_ref.dtype)
        u_ref[...] = u.astype(u_ref.dtype)
        a_ref[...] = (g * _sigmoid(g) * u).astype(a_ref.dtype)

    row = pl.BlockSpec((tm, d), lambda i, jj: (i, 0))
    tile = pl.BlockSpec((tm, FFN_COLS), lambda i, jj: (i, jj))
    return pl.pallas_call(
        body, name="ffn_gate_up", grid=(s // tm, halves),
        in_specs=[row, pl.BlockSpec((1, d), lambda i, jj: (0, 0)),
                  pl.BlockSpec((1, 1, d, FFN_COLS), lambda i, jj: (jj, layer, 0, 0)),
                  pl.BlockSpec((1, 1, d, FFN_COLS), lambda i, jj: (halves + jj, layer, 0, 0))],
        out_specs=[row, tile, tile, tile],
        out_shape=[jax.ShapeDtypeStruct((s, d), BF16)] + [jax.ShapeDtypeStruct((s, D_FF), BF16)] * 3,
        compiler_params=_params(("parallel", "arbitrary")))(x, gain.reshape(1, d), w_gu, w_gu)


def _ffn_dgate_up(dx2, w_down, layer, g, u):
    s, d = dx2.shape
    tm = min(FFN_ROWS, s)
    halves = D_FF // FFN_COLS

    def body(dx_ref, w_ref, g_ref, u_ref, o_ref):
        jj = pl.program_id(1)
        df = lax.dot_general(dx_ref[...].astype(MXU_DTYPE), w_ref[...].reshape((FFN_COLS, d)), (((1,), (1,)), ((), ())),
                             preferred_element_type=F32)
        gf, uf = g_ref[...].astype(F32), u_ref[...].astype(F32)
        sg = _sigmoid(gf)
        dg = df * uf * (sg * (1.0 + gf * (1.0 - sg)))
        du = df * (gf * sg)
        o_ref[:, pl.ds(pl.multiple_of(jj * FFN_COLS, LANES), FFN_COLS)] = dg.astype(o_ref.dtype)
        o_ref[:, pl.ds(pl.multiple_of(D_FF + jj * FFN_COLS, LANES), FFN_COLS)] = du.astype(o_ref.dtype)

    tile = pl.BlockSpec((tm, FFN_COLS), lambda i, jj: (i, jj))
    return pl.pallas_call(
        body, name="ffn_dgate_up", grid=(s // tm, halves),
        in_specs=[pl.BlockSpec((tm, d), lambda i, jj: (i, 0)),
                  pl.BlockSpec((2, 1, FFN_COLS // 2, d), lambda i, jj: (jj, layer, 0, 0)), tile, tile],
        out_specs=pl.BlockSpec((tm, 2 * D_FF), lambda i, jj: (i, 0)),
        out_shape=jax.ShapeDtypeStruct((s, 2 * D_FF), BF16),
        compiler_params=_params(("parallel", "arbitrary")))(dx2, w_down, g, u)


def _ffn_fwd(x1, gain, w_gu, w_down, layer):
    d = x1.shape[1]
    h, g, u, act = _ffn_gate_up(x1, gain, w_gu, layer)
    x2 = _mm(act, w_down, name="ffn_down", add=x1, tk=FFN_COLS,
             b_tiles=(D_FF, d, (2, 1, FFN_COLS // 2, d), lambda i, j, kk: (kk, layer, 0, 0)))
    return x2, (x1, h, g, u, act)


def _ffn_bwd(dx2, saved, gain, w_gu, w_down, layer, dw_gu, dw_down):
    x1, h, g, u, act = saved
    d = x1.shape[1]
    dw_down = _mm(act, dx2, ta=True, name="ffn_dw_down", out_dtype=BF16, tm=FFN_COLS, into=dw_down,
                  out_tiles=(dw_down.shape, (2, 1, FFN_COLS // 2, d), lambda i, j, kk: (i, layer, 0, 0)))
    dgu = _ffn_dgate_up(dx2, w_down, layer, g, u)
    dw_gu = _mm(h, dgu, ta=True, name="ffn_dw_gu", out_dtype=BF16, tn=FFN_COLS, into=dw_gu,
                out_tiles=(dw_gu.shape, (1, 1, d, FFN_COLS), lambda i, j, kk: (j, layer, i, 0)))
    dx1, dgain = _mm_rms_bwd(dgu, w_gu, x1, gain, dx2, name="ffn_dh", tm=1024, tk=FFN_COLS,
                             b_tiles=((1, 1, d, FFN_COLS), lambda i, kk: (kk, layer, 0, 0)))
    return dx1, dgain, dw_gu, dw_down


def _loss_head(x, gain, target):
    d = x.shape[1]

    def fn(xb, tb, gb):
        r = lax.rsqrt(jnp.mean(xb * xb, axis=-1, keepdims=True) + EPS)
        xh = xb * r
        y = xh * gb
        err = y - tb
        dyv = err * (1.0 / d)
        dyg = dyv * gb
        dx = r * (dyg - xh * jnp.mean(dyg * xh, axis=-1, keepdims=True))
        return (dx,), (0.5 * err * err * (1.0 / d), dyv * xh)
    dx, lsum, dg = _ew(fn, [x, target], [gain.reshape(1, -1)], outs=[(d, F32)], sums=[d, d], name="loss_head")
    return jnp.sum(lsum), dx, dg[0]


ATT_BLOCK = 256
CUM_BLOCK = 512
NEG_INF = -1e30


def _fox_gate_fwd(fl_row, b_col):
    nh, s = fl_row.shape
    tb = min(CUM_BLOCK, s)

    def body(fl_ref, b_ref, z_ref, c_ref):
        upper = (lax.broadcasted_iota(jnp.int32, (tb, tb), 0) <= lax.broadcasted_iota(jnp.int32, (tb, tb), 1)).astype(F32)
        carry = jnp.zeros((nh, 1), F32)
        for blk in range(s // tb):
            z = fl_ref[:, blk * tb:(blk + 1) * tb] + b_ref[...]
            logf = jnp.minimum(z, 0.0) - jnp.log(1.0 + jnp.exp(-jnp.abs(z)))
            cs = jnp.dot(logf, upper, precision=lax.Precision.HIGHEST, preferred_element_type=F32) + carry
            z_ref[:, blk * tb:(blk + 1) * tb] = z
            c_ref[:, blk * tb:(blk + 1) * tb] = cs
            carry = cs[:, tb - 1:tb]

    return pl.pallas_call(body, name="fox_gate_fwd", out_shape=[jax.ShapeDtypeStruct((nh, s), F32)] * 2,
                          compiler_params=_params())(fl_row, b_col)


def _fox_gate_bwd(dc_row, z_row):
    nh, s = dc_row.shape
    tb = min(CUM_BLOCK, s)

    def body(dc_ref, z_ref, dz_ref, db_ref):
        lower = (lax.broadcasted_iota(jnp.int32, (tb, tb), 0) >= lax.broadcasted_iota(jnp.int32, (tb, tb), 1)).astype(F32)
        carry = jnp.zeros((nh, 1), F32)
        db = jnp.zeros((nh, 1), F32)
        for blk in reversed(range(s // tb)):
            dc = dc_ref[:, blk * tb:(blk + 1) * tb]
            rs = jnp.dot(dc, lower, precision=lax.Precision.HIGHEST, preferred_element_type=F32) + carry
            dz = rs * _sigmoid(-z_ref[:, blk * tb:(blk + 1) * tb])
            dz_ref[:, blk * tb:(blk + 1) * tb] = dz
            db = db + jnp.sum(dz, axis=1, keepdims=True)
            carry = rs[:, 0:1]
        db_ref[...] = db

    return pl.pallas_call(body, name="fox_gate_bwd",
                          out_shape=[jax.ShapeDtypeStruct((nh, s), F32), jax.ShapeDtypeStruct((nh, 1), F32)],
                          compiler_params=_params())(dc_row, z_row)


def _head_masks(rows):
    lane = lax.broadcasted_iota(jnp.int32, (rows, LANES), 1)
    return lane < FOX_HEAD_DIM


ATT_QUERIES = 512


def _lanes(a, width):
    return jnp.concatenate([a] * (width // LANES), axis=1)


def _attn_fwd_t(qkv, c_lanes):
    s = qkv.shape[0]
    t, tq = min(ATT_BLOCK, s), min(2 * ATT_QUERIES, s)
    nq, per = s // tq, tq // t
    scale = FOX_HEAD_DIM ** -0.5
    np_ = HEAD_PAIRS
    nt = (((1,), (1,)), ((), ()))

    def body(q_ref, k_ref, v_ref, c_ref, o_ref, lse_ref, vt_ref):
        i = pl.program_id(1)

        @pl.when(i == 0)
        def _():
            for r in range(s // t):
                vt_ref[:, r * t:(r + 1) * t] = v_ref[r * t:(r + 1) * t, :].astype(F32).T.astype(vt_ref.dtype)

        first = _head_masks(tq)
        upper = lax.broadcasted_iota(jnp.int32, (LANES, tq), 0) < FOX_HEAD_DIM
        qs = q_ref[...] * scale
        zero = jnp.zeros_like(qs)
        qh = (jnp.where(first, qs, zero), jnp.where(first, zero, qs))
        def block(j, carry, diagonal=None):
            skip = 0 if diagonal is None else diagonal * t
            wide = tq - skip

            def join(old, new):
                return new if skip == 0 else jnp.concatenate([old[:, :skip], new], axis=1)

            ms, ls, acc = carry
            start = pl.multiple_of(j * t, t)
            kb, vt = k_ref[pl.ds(start, t), :], vt_ref[:, pl.ds(start, t)]
            new_m, new_l, alphas, pvs = [], [], [], []
            for hh in range(2):
                sc = lax.dot_general(kb, qh[hh][skip:], nt, preferred_element_type=F32) - _lanes(c_ref[0, hh, pl.ds(start, t), :], wide)
                if diagonal is not None:
                    visible = lax.broadcasted_iota(jnp.int32, (t, wide), 0) <= lax.broadcasted_iota(jnp.int32, (t, wide), 1)
                    sc = jnp.where(visible, sc, NEG_INF)
                m_old, l_old = ms[hh][:, skip:], ls[hh][:, skip:]
                m_new = jnp.maximum(m_old, jnp.max(sc, axis=0, keepdims=True))
                p = jnp.exp(sc - m_new)
                alpha = jnp.exp(m_old - m_new)
                new_m.append(join(ms[hh], m_new))
                new_l.append(join(ls[hh], alpha * l_old + jnp.sum(p, axis=0, keepdims=True)))
                alphas.append(alpha)
                p_hi = p.astype(MXU_DTYPE)
                p_lo = (p - p_hi.astype(F32)).astype(MXU_DTYPE)
                pvs.append(jnp.dot(vt, p_hi, preferred_element_type=F32) + jnp.dot(vt, p_lo, preferred_element_type=F32))
            seen = jnp.where(upper[:, skip:], alphas[0], alphas[1]) * acc[:, skip:] + jnp.where(upper[:, skip:], pvs[0], pvs[1])
            return tuple(new_m), tuple(new_l), join(acc, seen)

        neg, nil = jnp.full((1, tq), NEG_INF, F32), jnp.zeros((1, tq), F32)
        carry = lax.fori_loop(0, per * i, block, ((neg, neg), (nil, nil), jnp.zeros((LANES, tq), F32)))
        for d in range(per):
            carry = block(per * i + d, carry, diagonal=d)
        ms, ls, acc = carry
        o_ref[...] = (acc / jnp.where(upper, ls[0], ls[1])).T.astype(o_ref.dtype)
        lse_ref[0] = jnp.concatenate([ms[0] + jnp.log(ls[0]), ms[1] + jnp.log(ls[1])], axis=0)

    return pl.pallas_call(
        body, name="fox_attn_fwd", grid=(np_, nq),
        in_specs=[pl.BlockSpec((tq, LANES), lambda p, i: (i, p)),
                  pl.BlockSpec((s, LANES), lambda p, i: (0, np_ + p)),
                  pl.BlockSpec((s, LANES), lambda p, i: (0, 2 * np_ + p)),
                  pl.BlockSpec((1, 2, s, LANES), lambda p, i: (p, 0, 0, 0))],
        out_specs=[pl.BlockSpec((tq, LANES), lambda p, i: (i, p)),
                   pl.BlockSpec((1, 2, tq), lambda p, i: (p, 0, i))],
        out_shape=[jax.ShapeDtypeStruct((s, D_MODEL), F32), jax.ShapeDtypeStruct((np_, 2, s), F32)],
        scratch_shapes=[pltpu.VMEM((LANES, s), MXU_DTYPE)],
        compiler_params=_params(("parallel", "arbitrary")))(qkv, qkv, qkv, c_lanes)


def _attn_bwd_t(qkv, do, lse, delta, c_lanes):
    s = qkv.shape[0]
    t, tq = min(ATT_BLOCK, s), min(ATT_QUERIES, s)
    nb, nq, per = s // t, s // tq, tq // t
    scale = FOX_HEAD_DIM ** -0.5
    np_ = HEAD_PAIRS
    nt = (((1,), (1,)), ((), ()))

    def body(q_ref, k_ref, v_ref, do_ref, lse_ref, dl_ref, c_ref, dq_ref, dk_ref, dv_ref, dc_ref, dqt_ref):
        j = pl.program_id(1)
        first_q, first = _head_masks(tq), _head_masks(t)
        upper = lax.broadcasted_iota(jnp.int32, (LANES, tq), 0) < FOX_HEAD_DIM
        causal = (lax.broadcasted_iota(jnp.int32, (t, tq), 0) + (j % per) * t) <= lax.broadcasted_iota(jnp.int32, (t, tq), 1)
        kb, vb = k_ref[...], v_ref[...]
        kt = kb.astype(F32).T.astype(MXU_DTYPE)
        cb = (_lanes(c_ref[0, 0], tq), _lanes(c_ref[0, 1], tq))

        @pl.when(j == 0)
        def _():
            dqt_ref[...] = jnp.zeros_like(dqt_ref)

        def step(i, carry, masked):
            dk_acc, dv_acc, dc_accs = carry
            start = pl.multiple_of(i * tq, tq)
            qs = q_ref[pl.ds(start, tq), :] * scale
            dob = do_ref[pl.ds(start, tq), :]
            zero = jnp.zeros_like(qs)
            dks, dvs, dqs, dcs = [], [], [], []
            for hh in range(2):
                qh = jnp.where(first_q, qs, zero) if hh == 0 else jnp.where(first_q, zero, qs)
                doh = jnp.where(first_q, dob, zero) if hh == 0 else jnp.where(first_q, zero, dob)
                sc = lax.dot_general(kb, qh, nt, preferred_element_type=F32)
                p = jnp.exp(sc - cb[hh] - lse_ref[0, hh:hh + 1, pl.ds(start, tq)])
                if masked:
                    p = jnp.where(causal, p, 0.0)
                dp = lax.dot_general(vb, doh, nt, preferred_element_type=F32)
                ds = p * (dp - dl_ref[0, hh:hh + 1, pl.ds(start, tq)])
                pb, dsb = p.astype(MXU_DTYPE), ds.astype(MXU_DTYPE)
                dvs.append(jnp.dot(pb, dob, preferred_element_type=F32))
                dks.append(jnp.dot(dsb, qs, preferred_element_type=F32))
                dqs.append(jnp.dot(kt, dsb, preferred_element_type=F32))
                dcs.append(dc_accs[hh] - jnp.sum(ds, axis=1, keepdims=True))
            dqt_ref[:, pl.ds(start, tq)] += jnp.where(upper, dqs[0], dqs[1]) * scale
            return (dk_acc + jnp.where(first, dks[0], dks[1]), dv_acc + jnp.where(first, dvs[0], dvs[1]), tuple(dcs))

        nil, col = jnp.zeros((t, LANES), F32), jnp.zeros((t, 1), F32)
        carry = step(j // per, (nil, nil, (col, col)), True)
        dk_acc, dv_acc, dc_accs = lax.fori_loop(j // per + 1, nq, functools.partial(step, masked=False), carry)
        dk_ref[...] = dk_acc.astype(dk_ref.dtype)
        dv_ref[...] = dv_acc.astype(dv_ref.dtype)
        dc_ref[0, 0] = jnp.broadcast_to(dc_accs[0], (t, LANES))
        dc_ref[0, 1] = jnp.broadcast_to(dc_accs[1], (t, LANES))

        @pl.when(j == nb - 1)
        def _():
            for r in range(nb):
                dq_ref[r * t:(r + 1) * t, :] = dqt_ref[:, r * t:(r + 1) * t].T

    row = pl.BlockSpec((1, 2, s), lambda p, j: (p, 0, 0))
    return pl.pallas_call(
        body, name="fox_attn_bwd", grid=(np_, nb),
        in_specs=[pl.BlockSpec((s, LANES), lambda p, j: (0, p)),
                  pl.BlockSpec((t, LANES), lambda p, j: (j, np_ + p)),
                  pl.BlockSpec((t, LANES), lambda p, j: (j, 2 * np_ + p)),
                  pl.BlockSpec((s, LANES), lambda p, j: (0, p)), row, row,
                  pl.BlockSpec((1, 2, t, LANES), lambda p, j: (p, 0, j, 0))],
        out_specs=[pl.BlockSpec((s, LANES), lambda p, j: (0, p)),
                   pl.BlockSpec((t, LANES), lambda p, j: (j, p)),
                   pl.BlockSpec((t, LANES), lambda p, j: (j, p)),
                   pl.BlockSpec((1, 2, t, LANES), lambda p, j: (p, 0, j, 0))],
        out_shape=[jax.ShapeDtypeStruct((s, D_MODEL), F32), jax.ShapeDtypeStruct((s, D_MODEL), BF16),
                   jax.ShapeDtypeStruct((s, D_MODEL), BF16), jax.ShapeDtypeStruct((np_, 2, s, LANES), F32)],
        scratch_shapes=[pltpu.VMEM((LANES, s), F32)],
        compiler_params=_params(("parallel", "arbitrary")))(qkv, qkv, qkv, do, lse, delta, c_lanes)


def _head_sums(a, b, name):
    d = a.shape[1]
    sel = (jnp.arange(d)[:, None] // FOX_HEAD_DIM == jnp.arange(LANES)[None, :]).astype(F32)

    def fn(ab, bb, selb):
        prod = ab.astype(F32) * bb.astype(F32)
        return (jnp.dot(prod, selb, precision=lax.Precision.HIGHEST, preferred_element_type=F32),), ()
    return _ew(fn, [a, b], [sel], outs=[(LANES, F32)], name=name)[0]


def _fox_fwd(x, gain, w_qkv, w_f, b_f, w_out):
    s = x.shape[0]
    h = _rms_fwd(x, gain, "mix_norm")
    qkv = _mm(h, w_qkv, name="fox_qkv", out_dtype=BF16)
    fl = _mm(h, w_f, name="fox_f", tn=LANES)
    z_row, c_rowf = _fox_gate_fwd(fl[:, :FOX_HEADS].T, b_f.reshape(FOX_HEADS, 1))
    c_lanes = jnp.broadcast_to(c_rowf.reshape(HEAD_PAIRS, 2, s, 1), (HEAD_PAIRS, 2, s, LANES))
    o, lse = _attn_fwd_t(qkv, c_lanes)
    x1 = _mm(o, w_out, name="fox_out", add=x)
    return x1, (x, h, qkv, z_row, c_lanes, o, lse)


def _fox_bwd(dx1, saved, gain, w_qkv, w_f, w_out):
    x, h, qkv, z_row, c_lanes, o, lse = saved
    s = x.shape[0]
    do = _mm(dx1, w_out, tb=True, name="fox_do", out_dtype=BF16)
    dw_out = _mm(o, dx1, ta=True, name="fox_dw_out", out_dtype=BF16)
    delta = _head_sums(do, o, "fox_delta")[:, :FOX_HEADS].T.reshape(HEAD_PAIRS, 2, s)
    dq, dk, dv, dc = _attn_bwd_t(qkv, do, lse, delta, c_lanes)
    dz_row, db = _fox_gate_bwd(dc[..., 0].reshape(FOX_HEADS, s), z_row)
    dqkv = jnp.concatenate([dq.astype(BF16), dk, dv], axis=1)
    dfl = jnp.pad(dz_row.T, ((0, 0), (0, LANES - FOX_HEADS))).astype(BF16)
    dw_qkv = _mm(h, dqkv, ta=True, name="fox_dw_qkv", out_dtype=BF16)
    dw_f = _mm(h, dfl, ta=True, name="fox_dw_f", out_dtype=BF16, tn=LANES)
    dh = _mm(dqkv, w_qkv, tb=True, name="fox_dh_qkv")
    dx, dgain = _mm_rms_bwd(dfl, w_f, x, gain, dx1, name="fox_dh_f", add=dh)
    dw_in = jnp.concatenate([dw_qkv, dw_f[:, :FOX_HEADS]], axis=1)
    return dx, dgain, dw_in, db.reshape(FOX_HEADS), dw_out


S5_ROWS = 2048
SCAN_CHUNKS = SUBLANES


def _s5_operands(a_re, a_im, log_dt, b_re, b_im, c_re, c_im):
    dt = jnp.exp(log_dt)[:, None]
    mag, ang = jnp.exp(a_re * dt), a_im * dt
    lr, li = mag * jnp.cos(ang), mag * jnp.sin(ang)
    den = a_re * a_re + a_im * a_im
    cr = ((lr - 1.0) * a_re + li * a_im) / den
    ci = (li * a_re - (lr - 1.0) * a_im) / den
    bbr = cr[..., None] * b_re - ci[..., None] * b_im
    bbi = cr[..., None] * b_im + ci[..., None] * b_re
    nb = S5_BLOCKS
    lam = jnp.stack([lr.reshape(nb, 2, S5_HALF), li.reshape(nb, 2, S5_HALF)], axis=2)
    eye4, eye2 = jnp.eye(4, dtype=F32), jnp.eye(2, dtype=F32)
    bb = jnp.stack([bbr, bbi], axis=0).reshape(2, nb, 2, 4, S5_STATE, S5_GROUP)
    bmat = jnp.einsum("rbhgpc,kg,jh->bhjkcrgp", bb, eye4, eye2).reshape(nb, 2, 128, 2 * S5_HALF)
    cc = jnp.stack([c_re, -c_im], axis=0).reshape(2, nb, 2, 4, S5_GROUP, S5_STATE)
    cmat = jnp.einsum("rbhgcp,kg,jh->bhrgpjkc", cc, eye4, eye2).reshape(nb, 2, 2 * S5_HALF, 128)
    return lam, bmat, cmat


def _time_to_scan_order(a):
    s, d = a.shape
    return a.reshape(SCAN_CHUNKS, s // SCAN_CHUNKS, d).transpose(1, 0, 2).reshape(s, d)


def _scan_to_time_order(a):
    s, d = a.shape
    return a.reshape(s // SCAN_CHUNKS, SCAN_CHUNKS, d).transpose(1, 0, 2).reshape(s, d)


def _scan_chunks(xr_ref, xi_ref, lr, li, nst, reverse, after_step=None, state=None):
    lanes = lr.shape[1]
    lr8, li8 = jnp.broadcast_to(lr, (SUBLANES, lanes)), jnp.broadcast_to(li, (SUBLANES, lanes))
    zero8 = jnp.zeros((SUBLANES, lanes), F32)

    def rows_of(n):
        s = (nst - 1 - n) if reverse else n
        return s, pl.ds(pl.multiple_of(s * SUBLANES, SUBLANES), SUBLANES)

    def local(n, carry):
        pr, pi = carry
        _, rows = rows_of(n)
        nr = lr8 * pr - li8 * pi + xr_ref[rows, :]
        ni = lr8 * pi + li8 * pr + xi_ref[rows, :]
        xr_ref[rows, :] = nr
        xi_ref[rows, :] = ni
        return nr, ni

    er, ei = lax.fori_loop(0, nst, local, (zero8, zero8))
    pr, pi = lr, li
    for _ in range(int(math.log2(nst))):
        pr, pi = pr * pr - pi * pi, 2.0 * pr * pi
    tr = ti = jnp.zeros((1, lanes), F32)
    ent_r, ent_i = [None] * SCAN_CHUNKS, [None] * SCAN_CHUNKS
    for k in (reversed(range(SCAN_CHUNKS)) if reverse else range(SCAN_CHUNKS)):
        ent_r[k], ent_i[k] = tr, ti
        tr, ti = er[k:k + 1] + (pr * tr - pi * ti), ei[k:k + 1] + (pr * ti + pi * tr)
    in_r, in_i = jnp.concatenate(ent_r, axis=0), jnp.concatenate(ent_i, axis=0)

    def fix(n, carry):
        wr, wi, st = carry
        s, rows = rows_of(n)
        nr = xr_ref[rows, :] + (wr * in_r - wi * in_i)
        ni = xi_ref[rows, :] + (wr * in_i + wi * in_r)
        xr_ref[rows, :] = nr
        xi_ref[rows, :] = ni
        if after_step is not None:
            st = after_step(s, nr, ni, st)
        return wr * lr8 - wi * li8, wr * li8 + wi * lr8, st

    _, _, state = lax.fori_loop(0, nst, fix, (lr8, li8, state))
    return in_r, in_i, state


def _s5_fill_states(u_ref, bm, xr_ref, xi_ref, s):
    rc = min(S5_ROWS, s)

    def fill(r, _):
        rows = pl.ds(pl.multiple_of(r * rc, rc), rc)
        bu = jnp.dot(u_ref[rows, :].astype(MXU_DTYPE), bm, preferred_element_type=F32)
        xr_ref[rows, :] = bu[:, :S5_HALF]
        xi_ref[rows, :] = bu[:, S5_HALF:]
        return 0
    lax.fori_loop(0, s // rc, fill, 0)


def _s5_specs():
    return [pl.BlockSpec((1, 2, 2, S5_HALF), lambda b: (b, 0, 0, 0)),
            pl.BlockSpec((1, 2, 128, 2 * S5_HALF), lambda b: (b, 0, 0, 0)),
            pl.BlockSpec((1, 2, 2 * S5_HALF, 128), lambda b: (b, 0, 0, 0)),
            pl.BlockSpec((1, LANES), lambda b: (0, b))]


def _s5_scan_fwd(u, lam, bmat, cmat, dvec):
    s = u.shape[0]
    nst = s // SCAN_CHUNKS
    rc = min(S5_ROWS, s)

    def body(u_ref, lam_ref, b_ref, c_ref, d_ref, y_ref, xr_ref, xi_ref):
        y_ref[...] = u_ref[...] * d_ref[...]
        for hb in range(2):
            _s5_fill_states(u_ref, b_ref[0, hb], xr_ref, xi_ref, s)
            _scan_chunks(xr_ref, xi_ref, lam_ref[0, hb, 0:1, :], lam_ref[0, hb, 1:2, :], nst, False)
            cm = c_ref[0, hb]

            def emit(r, _, cm=cm):
                rows = pl.ds(pl.multiple_of(r * rc, rc), rc)
                y_ref[rows, :] += (jnp.dot(xr_ref[rows, :].astype(MXU_DTYPE), cm[:S5_HALF], preferred_element_type=F32)
                                   + jnp.dot(xi_ref[rows, :].astype(MXU_DTYPE), cm[S5_HALF:], preferred_element_type=F32))
                return 0
            lax.fori_loop(0, s // rc, emit, 0)

    blk = pl.BlockSpec((s, LANES), lambda b: (0, b))
    return pl.pallas_call(
        body, name="s5_scan_fwd", grid=(S5_BLOCKS,), in_specs=[blk] + _s5_specs(), out_specs=blk,
        out_shape=jax.ShapeDtypeStruct(u.shape, F32),
        scratch_shapes=[pltpu.VMEM((s, S5_HALF), F32)] * 2,
        compiler_params=_params(("parallel",)))(u, lam, bmat, cmat, dvec)


def _s5_scan_bwd(u, dy, lam, bmat, cmat, dvec):
    s = u.shape[0]
    nst = s // SCAN_CHUNKS
    rc = min(S5_ROWS, s)
    nt = (((1,), (1,)), ((), ()))
    tn = (((0,), (0,)), ((), ()))

    def body(u_ref, dy_ref, lam_ref, b_ref, c_ref, d_ref, du_ref, db_ref, dc_ref, dl_ref, dd_ref,
             xr_ref, xi_ref, gr_ref, gi_ref):
        du_ref[...] = dy_ref[...] * d_ref[...]
        dd_ref[...] = jnp.sum(dy_ref[...] * u_ref[...], axis=0, keepdims=True)
        db_ref[...] = jnp.zeros_like(db_ref)
        dc_ref[...] = jnp.zeros_like(dc_ref)
        for hb in range(2):
            bm, cm = b_ref[0, hb], c_ref[0, hb]
            lr, li = lam_ref[0, hb, 0:1, :], lam_ref[0, hb, 1:2, :]
            _s5_fill_states(u_ref, bm, xr_ref, xi_ref, s)
            xin_r, xin_i, _ = _scan_chunks(xr_ref, xi_ref, lr, li, nst, False)

            def fill_g(r, _, cm=cm):
                rows = pl.ds(pl.multiple_of(r * rc, rc), rc)
                g = lax.dot_general(dy_ref[rows, :].astype(MXU_DTYPE), cm, nt, preferred_element_type=F32)
                gr_ref[rows, :] = g[:, :S5_HALF]
                gi_ref[rows, :] = g[:, S5_HALF:]
                return 0
            lax.fori_loop(0, s // rc, fill_g, 0)
            def lam_grad(st, g_r, g_i, acc, xin_r=xin_r, xin_i=xin_i):
                prev = pl.ds(pl.multiple_of(jnp.maximum(st - 1, 0) * SUBLANES, SUBLANES), SUBLANES)
                x_r = jnp.where(st > 0, xr_ref[prev, :], xin_r)
                x_i = jnp.where(st > 0, xi_ref[prev, :], xin_i)
                return acc[0] + (g_r * x_r + g_i * x_i), acc[1] + (g_i * x_r - g_r * x_i)

            zero8 = jnp.zeros((SUBLANES, S5_HALF), F32)
            _, _, (a_r, a_i) = _scan_chunks(gr_ref, gi_ref, lr, -li, nst, True, after_step=lam_grad, state=(zero8, zero8))
            dl_ref[0, hb] = jnp.concatenate([jnp.sum(a_r, axis=0, keepdims=True),
                                             jnp.sum(a_i, axis=0, keepdims=True)], axis=0)

            def emit(r, _, bm=bm, hb=hb):
                rows = pl.ds(pl.multiple_of(r * rc, rc), rc)
                g = jnp.concatenate([gr_ref[rows, :], gi_ref[rows, :]], axis=1).astype(MXU_DTYPE)
                x = jnp.concatenate([xr_ref[rows, :], xi_ref[rows, :]], axis=1).astype(MXU_DTYPE)
                du_ref[rows, :] += lax.dot_general(g, bm, nt, preferred_element_type=F32)
                db_ref[0, hb] += lax.dot_general(u_ref[rows, :].astype(MXU_DTYPE), g, tn, preferred_element_type=F32)
                dc_ref[0, hb] += lax.dot_general(dy_ref[rows, :].astype(MXU_DTYPE), x, tn, preferred_element_type=F32)
                return 0
            lax.fori_loop(0, s // rc, emit, 0)

    blk = pl.BlockSpec((s, LANES), lambda b: (0, b))
    mat = pl.BlockSpec((1, 2, 128, 2 * S5_HALF), lambda b: (b, 0, 0, 0))
    return pl.pallas_call(
        body, name="s5_scan_bwd", grid=(S5_BLOCKS,), in_specs=[blk, blk] + _s5_specs(),
        out_specs=[blk, mat, mat, pl.BlockSpec((1, 2, 2, S5_HALF), lambda b: (b, 0, 0, 0)),
                   pl.BlockSpec((1, LANES), lambda b: (0, b))],
        out_shape=[jax.ShapeDtypeStruct(u.shape, F32),
                   jax.ShapeDtypeStruct((S5_BLOCKS, 2, 128, 2 * S5_HALF), F32),
                   jax.ShapeDtypeStruct((S5_BLOCKS, 2, 128, 2 * S5_HALF), F32),
                   jax.ShapeDtypeStruct((S5_BLOCKS, 2, 2, S5_HALF), F32),
                   jax.ShapeDtypeStruct((1, D_MODEL), F32)],
        scratch_shapes=[pltpu.VMEM((s, S5_HALF), F32)] * 4,
        compiler_params=_params(("parallel",)))(u, dy, lam, bmat, cmat, dvec)


_GELU_C = math.sqrt(2.0 / math.pi)


def _gelu_parts(y):
    inner = _GELU_C * (y + 0.044715 * y * y * y)
    th = jnp.tanh(inner)
    return 0.5 * y * (1.0 + th), th


def _s5_fwd(x, gain, w_in, ssm, dvec, w_glu):
    lam, bmat, cmat = ssm
    h = _rms_fwd(x, gain, "mix_norm")
    u = _mm(h, w_in, name="s5_in")
    y = _scan_to_time_order(_s5_scan_fwd(_time_to_scan_order(u), lam, bmat.astype(MXU_DTYPE), cmat.astype(MXU_DTYPE), dvec))
    g = _ew(lambda yb: ((_gelu_parts(yb)[0],), ()), [y], outs=[(D_MODEL, BF16)], name="s5_gelu")[0]
    vg = _mm(g, w_glu, name="s5_glu", out_dtype=BF16)

    def glu_fn(vb, gb, xb):
        return (xb + vb.astype(F32) * _sigmoid(gb.astype(F32)),), ()
    x1 = _ew(glu_fn, [(vg, D_MODEL, 0), (vg, D_MODEL, 1), x], outs=[(D_MODEL, F32)], name="s5_gate")[0]
    return x1, (x, h, u, y, g, vg)


def _s5_bwd(dx1, saved, gain, w_in, ssm, dvec, w_glu):
    x, h, u, y, g, vg = saved
    lam, bmat, cmat = ssm

    def dglu_fn(db, vb, gb):
        vf, sg = vb.astype(F32), _sigmoid(gb.astype(F32))
        return (jnp.concatenate([db * sg, db * vf * sg * (1.0 - sg)], axis=1),), ()
    dvg = _ew(dglu_fn, [dx1, (vg, D_MODEL, 0), (vg, D_MODEL, 1)], outs=[(2 * D_MODEL, BF16)], name="s5_dgate")[0]
    dw_glu = _mm(g, dvg, ta=True, name="s5_dw_glu", out_dtype=BF16)
    dg = _mm(dvg, w_glu, tb=True, name="s5_dg")

    def dgelu_fn(dgb, yb):
        _, th = _gelu_parts(yb)
        dinner = _GELU_C * (1.0 + 3.0 * 0.044715 * yb * yb)
        return (dgb * (0.5 * (1.0 + th) + 0.5 * yb * (1.0 - th * th) * dinner),), ()
    dy = _ew(dgelu_fn, [dg, y], outs=[(D_MODEL, F32)], name="s5_dgelu")[0]
    du_s, dbm, dct, dlam, ddvec = _s5_scan_bwd(_time_to_scan_order(u), _time_to_scan_order(dy), lam,
                                               bmat.astype(MXU_DTYPE), cmat.astype(MXU_DTYPE), dvec)
    du = _scan_to_time_order(du_s)
    dw_in = _mm(h, du, ta=True, name="s5_dw_in", out_dtype=BF16)
    dx, dgain = _mm_rms_bwd(du, w_in, x, gain, dx1, name="s5_dh")
    return dx, dgain, dw_in, (dlam, dbm, jnp.swapaxes(dct, 2, 3)), ddvec, dw_glu


POOL_BLOCK = 256
N_POOL_GROUPS = len(POOL_WINDOWS)


def _pool_bands(gi, i, t):
    w = jnp.left_shift(2, gi)
    r = lax.broadcasted_iota(jnp.int32, (t, t), 0)
    c = lax.broadcasted_iota(jnp.int32, (t, t), 1)
    inside = ((c <= r) & (c > r - w)).astype(MXU_DTYPE)
    before = (c > r - w + t).astype(MXU_DTYPE)

    def inv_count(block):
        pos = block * t + lax.broadcasted_iota(jnp.int32, (t, 1), 0)
        return 1.0 / jnp.minimum(pos + 1, w).astype(F32)
    return inside, before, inv_count


def _pool_fwd(x, gain, w_grp, b_grp, scale):
    s = x.shape[0]
    t = min(POOL_BLOCK, s)
    h = _rms_fwd(x, gain, "mix_norm")

    def body(h_ref, hp_ref, w_ref, b_ref, sc_ref, x_ref, x1_ref, diff_ref):
        gi, i = pl.program_id(0), pl.program_id(1)
        inside, before, inv_count = _pool_bands(gi, i, t)
        hc = h_ref[...]
        tot = jnp.dot(inside, hc.astype(MXU_DTYPE), preferred_element_type=F32)
        prev = jnp.dot(before, hp_ref[...].astype(MXU_DTYPE), preferred_element_type=F32)
        tot = tot + jnp.where(i > 0, prev, 0.0)
        diff = (tot * inv_count(i) - hc.astype(F32)).astype(diff_ref.dtype)
        y = (jnp.dot(diff.astype(MXU_DTYPE), w_ref[0], preferred_element_type=F32) + b_ref[...]) * sc_ref[...]
        diff_ref[...] = diff
        x1_ref[...] = x_ref[...] + y

    blk = pl.BlockSpec((t, POOL_WIDTH), lambda gi, i: (i, gi))
    vec = pl.BlockSpec((1, POOL_WIDTH), lambda gi, i: (0, gi))
    x1, diff = pl.pallas_call(
        body, name="pool_fwd", grid=(N_POOL_GROUPS, s // t),
        in_specs=[blk, pl.BlockSpec((t, POOL_WIDTH), lambda gi, i: (jnp.maximum(i - 1, 0), gi)),
                  pl.BlockSpec((1, POOL_WIDTH, POOL_WIDTH), lambda gi, i: (gi, 0, 0)), vec, vec, blk],
        out_specs=[blk, blk],
        out_shape=[jax.ShapeDtypeStruct(x.shape, F32), jax.ShapeDtypeStruct(x.shape, BF16)],
        compiler_params=_params(("parallel", "arbitrary")))(h, h, w_grp, b_grp, scale, x)
    return x1, (x, diff)


def _pool_bwd(dx1, saved, gain, w_grp, b_grp, scale):
    x, diff = saved
    s = x.shape[0]
    t = min(POOL_BLOCK, s)
    nb = s // t

    def body1(dx_ref, diff_ref, w_ref, b_ref, sc_ref, dd_ref, dw_ref, db_ref, dsc_ref):
        i = pl.program_id(1)

        @pl.when(i == 0)
        def _():
            dw_ref[...] = jnp.zeros_like(dw_ref)
            db_ref[...] = jnp.zeros_like(db_ref)
            dsc_ref[...] = jnp.zeros_like(dsc_ref)

        dfb = diff_ref[...].astype(MXU_DTYPE)
        ypre = jnp.dot(dfb, w_ref[0], preferred_element_type=F32) + b_ref[...]
        dxb = dx_ref[...]
        dy = dxb * sc_ref[...]
        dsc_ref[...] += jnp.sum(dxb * ypre, axis=0, keepdims=True)
        db_ref[...] += jnp.sum(dy, axis=0, keepdims=True)
        dyb = dy.astype(MXU_DTYPE)
        dw_ref[0] += lax.dot_general(dfb, dyb, (((0,), (0,)), ((), ())), preferred_element_type=F32)
        dd_ref[...] = lax.dot_general(dyb, w_ref[0], (((1,), (1,)), ((), ())), preferred_element_type=F32)

    blk = pl.BlockSpec((t, POOL_WIDTH), lambda gi, i: (i, gi))
    vec = pl.BlockSpec((1, POOL_WIDTH), lambda gi, i: (0, gi))
    mat = pl.BlockSpec((1, POOL_WIDTH, POOL_WIDTH), lambda gi, i: (gi, 0, 0))
    ddiff, dw, db, dsc = pl.pallas_call(
        body1, name="pool_bwd_map", grid=(N_POOL_GROUPS, nb), in_specs=[blk, blk, mat, vec, vec],
        out_specs=[blk, mat, vec, vec],
        out_shape=[jax.ShapeDtypeStruct(x.shape, F32), jax.ShapeDtypeStruct(w_grp.shape, F32),
                   jax.ShapeDtypeStruct((1, D_MODEL), F32), jax.ShapeDtypeStruct((1, D_MODEL), F32)],
        compiler_params=_params(("parallel", "arbitrary")))(dx1, diff, w_grp, b_grp, scale)

    def body2(dc_ref, dn_ref, dh_ref):
        gi, i = pl.program_id(0), pl.program_id(1)
        inside, before, inv_count = _pool_bands(gi, i, t)
        tn = (((0,), (0,)), ((), ()))
        dc = dc_ref[...]
        tot = lax.dot_general(inside, (dc * inv_count(i)).astype(MXU_DTYPE), tn, preferred_element_type=F32)
        nxt = lax.dot_general(before, (dn_ref[...] * inv_count(i + 1)).astype(MXU_DTYPE), tn, preferred_element_type=F32)
        dh_ref[...] = tot + jnp.where(i < nb - 1, nxt, 0.0) - dc

    dh = pl.pallas_call(
        body2, name="pool_bwd_window", grid=(N_POOL_GROUPS, nb),
        in_specs=[blk, pl.BlockSpec((t, POOL_WIDTH), lambda gi, i: (jnp.minimum(i + 1, nb - 1), gi))],
        out_specs=blk, out_shape=jax.ShapeDtypeStruct(x.shape, F32),
        compiler_params=_params(("parallel", "parallel")))(ddiff, ddiff)
    dx, dgain = _rms_bwd(x, gain, dh, dx1, "mix_norm_bwd")
    return dx, dgain, dw, db, dsc


MESH_ID = pl.DeviceIdType.MESH
ANY = pl.BlockSpec(memory_space=pl.ANY)


def _place():
    x, y, c = lax.axis_index("x"), lax.axis_index("y"), lax.axis_index("c")
    other_chips = [(1 - x, y), (x, 1 - y), (1 - x, 1 - y)]
    return x, y, c, other_chips


def _chip_index(chip):
    return 2 * chip[0] + chip[1]


def _remote(src, dst, send_sems, recv_sems, n, to):
    return pltpu.make_async_remote_copy(src_ref=src, dst_ref=dst, send_sem=send_sems.at[n], recv_sem=recv_sems.at[n],
                                        device_id=to, device_id_type=MESH_ID)


def _gather_weights(ws):
    n = len(ws)
    from_x, relay_x, from_y, relay_y, sib_x, sib_y, sib_d0, sib_d1, sib_own = range(9)
    slots = 9

    def body(*refs):
        w_refs, out_refs = refs[:n], refs[n:2 * n]
        send_sems, recv_sems = refs[2 * n:]
        x, y, c, (xn, yn, dn) = _place()
        k = _chip_index((x, y))
        me, sibling = (x, y, c), (x, y, 1 - c)

        def copy(ref, t, slot, to):
            return _remote(ref, ref, send_sems, recv_sems, slots * t + slot, to)

        def quarter(ref, q):
            rows = ref.shape[0] // 2
            return ref.at[pl.ds(q * rows, rows)]

        started = []

        def start(cp):
            cp.start()
            started.append(cp)

        for t in range(n):
            for slot, chip in ((from_x, xn), (from_y, yn)):
                start(_remote(w_refs[t].at[c], out_refs[t].at[k, c], send_sems, recv_sems, slots * t + slot, (*chip, c)))
            start(_remote(w_refs[t], out_refs[t].at[k], send_sems, recv_sems, slots * t + sib_own, sibling))
        for t in range(n):
            got = out_refs[t].at[_chip_index(xn), c]
            copy(got, t, from_x, me).wait_recv()
            start(copy(quarter(got, 0), t, relay_y, (*yn, c)))
            start(copy(got, t, sib_x, sibling))
        for t in range(n):
            got = out_refs[t].at[_chip_index(yn), c]
            copy(got, t, from_y, me).wait_recv()
            start(copy(quarter(got, 1), t, relay_x, (*xn, c)))
            start(copy(got, t, sib_y, sibling))
        for t in range(n):
            got = out_refs[t].at[_chip_index(dn), c]
            for q, slot, sib_slot in ((0, relay_y, sib_d0), (1, relay_x, sib_d1)):
                copy(quarter(got, q), t, slot, me).wait_recv()
                start(copy(quarter(got, q), t, sib_slot, sibling))
        for t in range(n):
            for chip, slot in ((xn, sib_x), (yn, sib_y)):
                copy(out_refs[t].at[_chip_index(chip), 1 - c], t, slot, me).wait_recv()
            theirs = out_refs[t].at[_chip_index(dn), 1 - c]
            copy(quarter(theirs, 0), t, sib_d0, me).wait_recv()
            copy(quarter(theirs, 1), t, sib_d1, me).wait_recv()
            copy(out_refs[t].at[k], t, sib_own, me).wait_recv()
        for cp in started:
            cp.wait_send()

    return pl.pallas_call(
        body, name="gather_weights", in_specs=[ANY] * n, out_specs=[ANY] * n,
        out_shape=[jax.ShapeDtypeStruct((N_CHIPS,) + w.shape, w.dtype) for w in ws],
        scratch_shapes=[pltpu.SemaphoreType.DMA((slots * n,)), pltpu.SemaphoreType.DMA((slots * n,))],
    )(*ws)


def _swap_halves(gs):
    n = len(gs)

    def body(*refs):
        g_refs, out_refs = refs[:n], refs[n:2 * n]
        send_sems, recv_sems = refs[2 * n:]
        x, y, c, _ = _place()
        copies = [_remote(g_refs[t].at[s, 1 - c], out_refs[t].at[s], send_sems, recv_sems, N_CHIPS * t + s, (x, y, 1 - c))
                  for t in range(n) for s in range(N_CHIPS)]
        for cp in copies:
            cp.start()
        for cp in copies:
            cp.wait_recv()
        for cp in copies:
            cp.wait_send()

    return pl.pallas_call(
        body, name="swap_halves", in_specs=[ANY] * n, out_specs=[ANY] * n,
        out_shape=[jax.ShapeDtypeStruct((N_CHIPS,) + g.shape[2:], g.dtype) for g in gs],
        scratch_shapes=[pltpu.SemaphoreType.DMA((N_CHIPS * n,)), pltpu.SemaphoreType.DMA((N_CHIPS * n,))],
    )(*gs)


def _neighbour_exchange(srcs, out_shapes, name):
    n = len(out_shapes)

    def body(*refs):
        in_refs, out_refs = refs[:n], refs[n:2 * n]
        send_sems, recv_sems = refs[2 * n:]
        x, y, c, chips = _place()
        sends = []
        for t in range(n):
            for slot, (src, chip) in enumerate(zip(srcs(in_refs[t], chips), chips[:2])):
                sends.append(_remote(src, out_refs[t].at[slot], send_sems, recv_sems, 2 * t + slot, (*chip, c)))
        for cp in sends:
            cp.start()
        for t in range(n):
            for slot in range(2):
                landed = out_refs[t].at[slot]
                _remote(landed, landed, send_sems, recv_sems, 2 * t + slot, (x, y, c)).wait_recv()
        for cp in sends:
            cp.wait_send()

    return pl.pallas_call(
        body, name=name, in_specs=[ANY] * n, out_specs=[ANY] * n, out_shape=out_shapes,
        scratch_shapes=[pltpu.SemaphoreType.DMA((2 * n,)), pltpu.SemaphoreType.DMA((2 * n,))])


def _relay_partials(ps):
    def srcs(p_ref, chips):
        half = p_ref.shape[1] // 2
        diagonal = p_ref.at[_chip_index(chips[2])]
        return diagonal.at[pl.ds(0, half)], diagonal.at[pl.ds(half, half)]

    shapes = [jax.ShapeDtypeStruct((2, p.shape[1] // 2, p.shape[2]), p.dtype) for p in ps]
    return _neighbour_exchange(srcs, shapes, "relay_partials")(*ps)


def _merge_relayed(p, relayed, targets, name):
    rows, cols = p.shape[1:]
    tr = _row_tile(rows // 2, SUM_ROWS)
    per = rows // 2 // tr

    def body(x_ref, y_ref, p_ref, r_ref, o_ref):
        to, q = pl.program_id(0), pl.program_id(1)
        extra = jnp.where(q == 1 - to, r_ref[0].astype(F32), 0.0)
        o_ref[0] = (p_ref[0].astype(F32) + extra).astype(o_ref.dtype)

    return pl.pallas_call(
        body, name=name,
        grid_spec=pltpu.PrefetchScalarGridSpec(
            num_scalar_prefetch=2, grid=(2, 2, per),
            in_specs=[pl.BlockSpec((1, tr, cols), lambda to, q, i, xs, ys: (jnp.where(to == 0, xs[0], ys[0]), q * per + i, 0)),
                      pl.BlockSpec((1, tr, cols), lambda to, q, i, xs, ys: (1 - to, i, 0))],
            out_specs=pl.BlockSpec((1, tr, cols), lambda to, q, i, xs, ys: (to, q * per + i, 0))),
        out_shape=jax.ShapeDtypeStruct((2, rows, cols), p.dtype),
        compiler_params=_params(("parallel", "parallel", "parallel")))(targets[0], targets[1], p, relayed)


def _scatter_partials(ms):
    shapes = [jax.ShapeDtypeStruct(m.shape, m.dtype) for m in ms]
    return _neighbour_exchange(lambda m_ref, chips: (m_ref.at[0], m_ref.at[1]), shapes, "scatter_partials")(*ms)


def _share_half(fs):
    n = len(fs)

    def body(*refs):
        f_refs, out_refs = refs[:n], refs[n:2 * n]
        send_sems, recv_sems = refs[2 * n:]
        x, y, c, _ = _place()
        sends = [_remote(f_refs[t], out_refs[t].at[c], send_sems, recv_sems, t, (x, y, 1 - c)) for t in range(n)]
        for cp in sends:
            cp.start()
        for t in range(n):
            theirs = out_refs[t].at[1 - c]
            _remote(theirs, theirs, send_sems, recv_sems, t, (x, y, c)).wait_recv()
        for cp in sends:
            cp.wait_send()

    return pl.pallas_call(
        body, name="share_half", in_specs=[ANY] * n, out_specs=[ANY] * n,
        out_shape=[jax.ShapeDtypeStruct((2,) + f.shape, f.dtype) for f in fs],
        scratch_shapes=[pltpu.SemaphoreType.DMA((n,)), pltpu.SemaphoreType.DMA((n,))],
    )(*fs)


def _gather_small(v):
    def body(v_ref, out_ref, send_sems, recv_sems):
        x, y, c, chips = _place()
        me, sibling = (x, y, c), (x, y, 1 - c)

        def slot(px, py, pc):
            return out_ref.at[4 * px + 2 * py + pc]

        first = [_remote(v_ref, slot(*me), send_sems, recv_sems, 0, sibling)]
        first += [_remote(v_ref, slot(*me), send_sems, recv_sems, 1 + j, (*chip, c)) for j, chip in enumerate(chips)]
        for cp in first:
            cp.start()
        passed = [_remote(slot(*chip, c), slot(*chip, c), send_sems, recv_sems, 4 + j, sibling)
                  for j, chip in enumerate(chips)]
        for j, chip in enumerate(chips):
            _remote(slot(*chip, c), slot(*chip, c), send_sems, recv_sems, 1 + j, me).wait_recv()
            passed[j].start()
        _remote(slot(*sibling), slot(*sibling), send_sems, recv_sems, 0, me).wait_recv()
        for j, chip in enumerate(chips):
            _remote(slot(*chip, 1 - c), slot(*chip, 1 - c), send_sems, recv_sems, 4 + j, me).wait_recv()
        for cp in first + passed:
            cp.wait_send()

    gathered = pl.pallas_call(
        body, name="gather_small", in_specs=[ANY], out_specs=ANY,
        out_shape=jax.ShapeDtypeStruct((N_DEV,) + v.shape, v.dtype),
        scratch_shapes=[pltpu.SemaphoreType.DMA((7,)), pltpu.SemaphoreType.DMA((7,))],
    )(v)
    device = 4 * lax.axis_index("x") + 2 * lax.axis_index("y") + lax.axis_index("c")
    return lax.dynamic_update_index_in_dim(gathered, v, device, 0)


SMALL_ROWS = 256
SUM_ROWS = 256
BF16_ROWS = 16


def _row_tile(rows, want):
    for t in range(min(rows, want) // BF16_ROWS * BF16_ROWS, 0, -BF16_ROWS):
        if rows % t == 0:
            return t
    return rows


def _pair_sum(g, r, core, name):
    rows, cols = g.shape[2:]
    tr = _row_tile(rows, SUM_ROWS)

    def body(c_ref, g_ref, r_ref, o_ref):
        o_ref[0] = (g_ref[0, 0].astype(F32) + r_ref[0].astype(F32)).astype(o_ref.dtype)

    return pl.pallas_call(
        body, name=name,
        grid_spec=pltpu.PrefetchScalarGridSpec(
            num_scalar_prefetch=1, grid=(N_CHIPS, rows // tr),
            in_specs=[pl.BlockSpec((1, 1, tr, cols), lambda s, i, c_ref: (s, c_ref[0], i, 0)),
                      pl.BlockSpec((1, tr, cols), lambda s, i, c_ref: (s, i, 0))],
            out_specs=pl.BlockSpec((1, tr, cols), lambda s, i, c_ref: (s, i, 0))),
        out_shape=jax.ShapeDtypeStruct(r.shape, BF16),
        compiler_params=_params(("parallel", "parallel")))(core, g, r)


def _chip_sum(p, recv, chip, name):
    rows, cols = p.shape[1:]
    tr = _row_tile(rows, SUM_ROWS)
    n_recv = recv.shape[0]

    def body(k_ref, p_ref, r_ref, o_ref):
        acc = p_ref[0].astype(F32)
        for j in range(n_recv):
            acc = acc + r_ref[j].astype(F32)
        o_ref[...] = acc

    return pl.pallas_call(
        body, name=name,
        grid_spec=pltpu.PrefetchScalarGridSpec(
            num_scalar_prefetch=1, grid=(rows // tr,),
            in_specs=[pl.BlockSpec((1, tr, cols), lambda i, k_ref: (k_ref[0], i, 0)),
                      pl.BlockSpec((n_recv, tr, cols), lambda i, k_ref: (0, i, 0))],
            out_specs=pl.BlockSpec((tr, cols), lambda i, k_ref: (i, 0))),
        out_shape=jax.ShapeDtypeStruct((rows, cols), F32),
        compiler_params=_params(("parallel",)))(chip, p, recv)


def _sum_blocks(a, name):
    n, rows, cols = a.shape
    tr = _row_tile(rows, SUM_ROWS)

    def body(a_ref, o_ref):
        acc = a_ref[0].astype(F32)
        for s in range(1, n):
            acc = acc + a_ref[s].astype(F32)
        o_ref[...] = acc

    return pl.pallas_call(
        body, name=name, grid=(rows // tr,),
        in_specs=[pl.BlockSpec((n, tr, cols), lambda i: (0, i, 0))],
        out_specs=pl.BlockSpec((tr, cols), lambda i: (i, 0)),
        out_shape=jax.ShapeDtypeStruct((rows, cols), F32),
        compiler_params=_params(("parallel",)))(a)


def _adamw(w, g, m, v, name):
    def fn(wb, gb, mb, vb):
        m2 = ADAM_B1 * mb + (1.0 - ADAM_B1) * gb
        v2 = ADAM_B2 * vb + (1.0 - ADAM_B2) * (gb * gb)
        m_hat = m2 / (1.0 - ADAM_B1 ** ADAM_STEP)
        v_hat = v2 / (1.0 - ADAM_B2 ** ADAM_STEP)
        delta = -ADAM_LR * (m_hat / (jnp.sqrt(v_hat) + ADAM_EPS) + ADAM_WD * wb)
        return (delta, m2, v2), ()
    c = w.shape[1]
    return _ew(fn, [w, g, m, v], outs=[(c, F32)] * 3, name=name)


WEIGHTS = ["mix_norm_g", "ffn_norm_g", "final_norm_g", "fox_w_in", "fox_b_f", "fox_w_out", "s5_w_in", "s5_a_re",
           "s5_a_im", "s5_log_dt", "s5_b_re", "s5_b_im", "s5_c_re", "s5_c_im", "s5_d", "s5_w_glu", "pool_w",
           "pool_b", "pool_scale", "ffn_w_gate_up", "ffn_w_down"]
BIG = {"fox_w_in": 2, "fox_w_out": 1, "s5_w_in": 1, "s5_w_glu": 2, "pool_w": 2, "ffn_w_gate_up": 2, "ffn_w_down": 1}
BY_CHIP = ("ffn_w_gate_up", "ffn_w_down")
SLICED = ("pool_b", "pool_scale")
SMALL = [n for n in WEIGHTS if n not in BIG]


def _to_natural(cm, axis):
    moved = jnp.moveaxis(cm, 0, axis)
    shape = moved.shape[:axis] + (moved.shape[axis] * moved.shape[axis + 1],) + moved.shape[axis + 2:]
    return moved.reshape(shape)


def _to_chip_major(nat, axis):
    shape = nat.shape[:axis] + (N_CHIPS, nat.shape[axis] // N_CHIPS) + nat.shape[axis + 1:]
    return jnp.moveaxis(nat.reshape(shape), axis, 0)


def _halves_view(shape):
    return (2, int(np.prod(shape[:-1])) // 2, shape[-1])


def _pack_small(parts):
    flat = jnp.concatenate([p.reshape(-1).astype(F32) for p in parts])
    pad = (-flat.shape[0]) % (SMALL_ROWS * LANES)
    return jnp.pad(flat, (0, pad)).reshape(-1, LANES)


def _unpack_small(buf, shapes):
    flat = buf.reshape(-1)
    out, off = [], 0
    for shp in shapes:
        n = int(np.prod(shp))
        out.append(flat[off:off + n].reshape(shp))
        off += n
    return out


def _local_step(x, target, w):
    grads = {}
    mixers = ("fox", "s5", "pool")
    saved = []
    ssm, ssm_pull = jax.vjp(_s5_operands, w["s5_a_re"][0], w["s5_a_im"][0], w["s5_log_dt"][0], w["s5_b_re"][0],
                            w["s5_b_im"][0], w["s5_c_re"][0], w["s5_c_im"][0])
    fox_w = []
    for j in range(w["fox_w_in"].shape[0]):
        w_in = w["fox_w_in"][j]
        w_f = jnp.pad(w_in[:, 3 * D_MODEL:], ((0, 0), (0, LANES - FOX_HEADS)))
        fox_w.append((w_in[:, :3 * D_MODEL], w_f, w["fox_w_out"][j]))
    for i in range(DEPTH):
        kind, j = mixers[i % 3], i // 3
        gain = w["mix_norm_g"][i]
        if kind == "fox":
            x1, sv = _fox_fwd(x, gain, fox_w[j][0], fox_w[j][1], w["fox_b_f"][j], fox_w[j][2])
        elif kind == "s5":
            x1, sv = _s5_fwd(x, gain, w["s5_w_in"][j], ssm, w["s5_d"], w["s5_w_glu"][j])
        else:
            x1, sv = _pool_fwd(x, gain, w["pool_w"][j], w["pool_b"], w["pool_scale"])
        x, sf = _ffn_fwd(x1, w["ffn_norm_g"][i], w["ffn_w_gate_up"], w["ffn_w_down"], i)
        saved.append((sv, sf))
    loss, dx, grads["final_norm_g"] = _loss_head(x, w["final_norm_g"], target)
    per_layer = {n: [None] * DEPTH for n in ("mix_norm_g", "ffn_norm_g")}
    fox_g = {n: [None] * len(fox_w) for n in ("fox_w_in", "fox_b_f", "fox_w_out")}
    for n in BY_CHIP:
        grads[n] = lax.empty(w[n].shape, BF16)
    for i in reversed(range(DEPTH)):
        kind, j = mixers[i % 3], i // 3
        sv, sf = saved[i]
        dx, per_layer["ffn_norm_g"][i], grads["ffn_w_gate_up"], grads["ffn_w_down"] = _ffn_bwd(
            dx, sf, w["ffn_norm_g"][i], w["ffn_w_gate_up"], w["ffn_w_down"], i, grads["ffn_w_gate_up"], grads["ffn_w_down"])
        gain = w["mix_norm_g"][i]
        if kind == "fox":
            dx, per_layer["mix_norm_g"][i], fox_g["fox_w_in"][j], fox_g["fox_b_f"][j], fox_g["fox_w_out"][j] = _fox_bwd(
                dx, sv, gain, fox_w[j][0], fox_w[j][1], fox_w[j][2])
        elif kind == "s5":
            dx, per_layer["mix_norm_g"][i], dw_in, dssm, dd, dw_glu = _s5_bwd(
                dx, sv, gain, w["s5_w_in"][j], ssm, w["s5_d"], w["s5_w_glu"][j])
            grads["s5_w_in"], grads["s5_w_glu"], grads["s5_d"] = dw_in[None], dw_glu[None], dd
            for n, g in zip(("s5_a_re", "s5_a_im", "s5_log_dt", "s5_b_re", "s5_b_im", "s5_c_re", "s5_c_im"), ssm_pull(dssm)):
                grads[n] = g[None]
        else:
            dx, per_layer["mix_norm_g"][i], dw, db, dsc = _pool_bwd(dx, sv, gain, w["pool_w"][j], w["pool_b"], w["pool_scale"])
            grads["pool_w"], grads["pool_b"], grads["pool_scale"] = dw[None].astype(BF16), db, dsc
    for n, parts in {**per_layer, **fox_g}.items():
        grads[n] = jnp.stack(parts)
    return loss, dx, grads


def kernel(x, mix_norm_g, ffn_norm_g, final_norm_g, fox_w_in, fox_b_f, fox_w_out, s5_w_in, s5_a_re, s5_a_im, s5_log_dt, s5_b_re, s5_b_im, s5_c_re, s5_c_im, s5_d, s5_w_glu, pool_w, pool_b, pool_scale, ffn_w_gate_up, ffn_w_down, loss_target, m_mix_norm_g, m_ffn_norm_g, m_final_norm_g, m_fox_w_in, m_fox_b_f, m_fox_w_out, m_s5_w_in, m_s5_a_re, m_s5_a_im, m_s5_log_dt, m_s5_b_re, m_s5_b_im, m_s5_c_re, m_s5_c_im, m_s5_d, m_s5_w_glu, m_pool_w, m_pool_b, m_pool_scale, m_ffn_w_gate_up, m_ffn_w_down, v_mix_norm_g, v_ffn_norm_g, v_final_norm_g, v_fox_w_in, v_fox_b_f, v_fox_w_out, v_s5_w_in, v_s5_a_re, v_s5_a_im, v_s5_log_dt, v_s5_b_re, v_s5_b_im, v_s5_c_re, v_s5_c_im, v_s5_d, v_s5_w_glu, v_pool_w, v_pool_b, v_pool_scale, v_ffn_w_gate_up, v_ffn_w_down):
    given = dict(locals())
    shard = {n: given[n] for n in WEIGHTS}
    chip = 2 * lax.axis_index("x") + lax.axis_index("y")
    core = lax.axis_index("c")

    views = {n: _halves_view(shard[n].shape) for n in BIG}
    own = [shard[n].astype(MXU_DTYPE).reshape(views[n]) for n in BIG]
    whole = {}
    for n, by_chip in zip(BIG, _gather_weights(own)):
        by_chip = by_chip.reshape((N_CHIPS,) + shard[n].shape)
        whole[n] = by_chip if n in BY_CHIP else _to_natural(by_chip, BIG[n])
    for n in SMALL:
        whole[n] = shard[n]
    sliced_shapes = [shard[n].shape for n in SLICED]
    by_chip = _gather_small(_pack_small([shard[n] for n in SLICED]))[0::2]
    slices = [_unpack_small(by_chip[k], sliced_shapes) for k in range(N_CHIPS)]
    for idx, n in enumerate(SLICED):
        whole[n] = jnp.concatenate([slices[k][idx] for k in range(N_CHIPS)], axis=-1)

    loss_part, dx, grads = _local_step(x[0], loss_target[0], whole)
    loss = lax.psum(loss_part, MESH_AXES)

    gs = [(grads[n] if n in BY_CHIP else _to_chip_major(grads[n].astype(BF16), BIG[n])).reshape((N_CHIPS,) + views[n])
          for n in BIG]
    core_id, chip_id = core.reshape(1).astype(jnp.int32), chip.reshape(1).astype(jnp.int32)
    partial = [_pair_sum(g, r, core_id, "pair_sum_" + n) for n, g, r in zip(BIG, gs, _swap_halves(gs))]
    x_chip, y_chip = 2 * (1 - lax.axis_index("x")) + lax.axis_index("y"), 2 * lax.axis_index("x") + 1 - lax.axis_index("y")
    neighbours = (x_chip.reshape(1).astype(jnp.int32), y_chip.reshape(1).astype(jnp.int32))
    merged = [_merge_relayed(p, r, neighbours, "merge_relayed_" + n) for n, p, r in zip(BIG, partial, _relay_partials(partial))]
    half = [_chip_sum(p, r, chip_id, "chip_sum_" + n) for n, p, r in zip(BIG, partial, _scatter_partials(merged))]
    grad = {n: lax.dynamic_update_index_in_dim(both, mine, core, 0).reshape(shard[n].shape)
            for n, mine, both in zip(BIG, half, _share_half(half))}

    small_sum = _sum_blocks(_gather_small(_pack_small([grads[n] for n in SMALL])), "small_sum")
    for n, g in zip(SMALL, _unpack_small(small_sum, [whole[n].shape for n in SMALL])):
        grad[n] = g
    for n in SLICED:
        width = shard[n].shape[-1]
        grad[n] = lax.dynamic_slice_in_dim(grad[n], chip * width, width, axis=-1)

    delta, new_m, new_v = {}, {}, {}
    for n in BIG:
        view = (-1, shard[n].shape[-1])
        res = _adamw(shard[n].reshape(view), grad[n].reshape(view), given["m_" + n].reshape(view),
                     given["v_" + n].reshape(view), "adamw_" + n)
        delta[n], new_m[n], new_v[n] = (r.reshape(shard[n].shape) for r in res)
    small_shapes = [shard[n].shape for n in SMALL]
    res = _adamw(_pack_small([shard[n] for n in SMALL]), _pack_small([grad[n] for n in SMALL]),
                 _pack_small([given["m_" + n] for n in SMALL]), _pack_small([given["v_" + n] for n in SMALL]), "adamw_small")
    for out, buf in zip((delta, new_m, new_v), res):
        for n, a in zip(SMALL, _unpack_small(buf, small_shapes)):
            out[n] = a
    return (loss, dx[None], *[grad[n] for n in WEIGHTS], *[delta[n] for n in WEIGHTS],
            *[new_m[n] for n in WEIGHTS], *[new_v[n] for n in WEIGHTS])
```

```python
import functools
import math

import jax
import jax.numpy as jnp
import numpy as np
from jax import lax
from jax.experimental import pallas as pl
from jax.experimental.pallas import tpu as pltpu

F32 = jnp.float32
BF16 = jnp.bfloat16
MXU_DTYPE = jnp.bfloat16

D_MODEL = 1024
DEPTH = 4
EPS = 1e-6
FOX_HEADS = 16
FOX_HEAD_DIM = 64
HEAD_PAIRS = FOX_HEADS // 2
S5_GROUPS = 64
S5_GROUP = 16
S5_STATE = 64
S5_BLOCKS = 8
S5_HALF = 256
POOL_WINDOWS = (2, 4, 8, 16)
POOL_WIDTH = 256
D_FF = 2816
N_CHIPS = 4
N_DEV = 8
LANES = 128
SUBLANES = 8
VMEM_LIMIT = 56 * 1024 * 1024

ADAM_LR = 0.001
ADAM_B1 = 0.9
ADAM_B2 = 0.999
ADAM_EPS = 1e-08
ADAM_WD = 0.01
ADAM_STEP = 10

MESH_AXES = ("x", "y", "c")


def _tile(n, want):
    t = (min(n, want) // LANES) * LANES
    while t >= LANES:
        if n % t == 0:
            return t
        t -= LANES
    return n


def _params(sem=None):
    return pltpu.CompilerParams(dimension_semantics=sem, vmem_limit_bytes=VMEM_LIMIT)


def _mm(a, b, *, name, ta=False, tb=False, out_dtype=F32, add=None, tm=1024, tn=1024, tk=1024,
        b_tiles=None, out_tiles=None, into=None):
    m, k = (a.shape[1], a.shape[0]) if ta else a.shape
    if b_tiles is None:
        n = b.shape[0] if tb else b.shape[1]
        assert (b.shape[1] if tb else b.shape[0]) == k, (a.shape, b.shape, ta, tb)
    else:
        assert b_tiles[0] == k, (a.shape, b_tiles[0])
        n = b_tiles[1]
    tm, tn, tk = _tile(m, tm), _tile(n, tn), _tile(k, tk)
    nk = k // tk
    a_spec = pl.BlockSpec((tk, tm), lambda i, j, kk: (kk, i)) if ta else pl.BlockSpec((tm, tk), lambda i, j, kk: (i, kk))
    b_shape = (tn, tk) if tb else (tk, tn)
    if b_tiles is None:
        b_spec = pl.BlockSpec(b_shape, (lambda i, j, kk: (j, kk)) if tb else (lambda i, j, kk: (kk, j)))
    else:
        b_spec = pl.BlockSpec(b_tiles[2], b_tiles[3])
    add_spec = pl.BlockSpec((tm, tn), lambda i, j, kk: (i, j))
    if out_tiles is None:
        o_spec, o_struct = add_spec, jax.ShapeDtypeStruct((m, n), out_dtype)
    else:
        o_spec, o_struct = pl.BlockSpec(out_tiles[1], out_tiles[2]), jax.ShapeDtypeStruct(out_tiles[0], out_dtype)
    dims = (((0 if ta else 1,), (1 if tb else 0,)), ((), ()))
    has_add, has_into = add is not None, into is not None

    def body(*refs):
        a_ref, b_ref = refs[:2]
        add_ref = refs[2] if has_add else None
        o_ref, acc_ref = refs[-2:]
        kk = pl.program_id(2)

        @pl.when(kk == 0)
        def _():
            acc_ref[...] = jnp.zeros_like(acc_ref)

        acc_ref[...] += lax.dot_general(a_ref[...].astype(MXU_DTYPE), b_ref[...].reshape(b_shape).astype(MXU_DTYPE), dims,
                                        preferred_element_type=F32)

        @pl.when(kk == nk - 1)
        def _():
            r = acc_ref[...]
            if has_add:
                r = r + add_ref[...].astype(F32)
            o_ref[...] = r.astype(out_dtype).reshape(o_ref.shape)

    ins = [a, b] + ([add] if has_add else []) + ([into] if has_into else [])
    specs = [a_spec, b_spec] + ([add_spec] if has_add else []) + ([pl.BlockSpec(memory_space=pl.ANY)] if has_into else [])
    return pl.pallas_call(
        body, name=name, grid=(m // tm, n // tn, nk), in_specs=specs, out_specs=o_spec,
        out_shape=o_struct, scratch_shapes=[pltpu.VMEM((tm, tn), F32)],
        input_output_aliases={len(ins) - 1: 0} if has_into else {},
        compiler_params=_params(("parallel", "parallel", "arbitrary")))(*ins)


def _ew(fn, tens, vecs=(), *, outs=(), sums=(), name, tr=256):
    tens = [t if isinstance(t, tuple) else (t, t.shape[1], 0) for t in tens]
    rows = tens[0][0].shape[0]
    tr = min(tr, rows)
    n_t, n_v, n_o, n_s = len(tens), len(vecs), len(outs), len(sums)

    def body(*refs):
        i = pl.program_id(0)
        t_blocks = [r[...] for r in refs[:n_t]]
        v_blocks = [r[...] for r in refs[n_t:n_t + n_v]]
        o_refs = refs[n_t + n_v:n_t + n_v + n_o]
        s_refs = refs[n_t + n_v + n_o:]
        o_vals, s_vals = fn(*t_blocks, *v_blocks)
        for r, v in zip(o_refs, o_vals):
            r[...] = v.astype(r.dtype)
        if n_s:
            @pl.when(i == 0)
            def _():
                for r in s_refs:
                    r[...] = jnp.zeros_like(r)
            for r, v in zip(s_refs, s_vals):
                r[...] += jnp.sum(v.astype(F32), axis=0, keepdims=True)

    in_specs = [pl.BlockSpec((tr, w), functools.partial(lambda i, cb: (i, cb), cb=cb)) for _, w, cb in tens]
    in_specs += [pl.BlockSpec(v.shape, functools.partial(lambda i, nd: (0,) * nd, nd=v.ndim)) for v in vecs]
    out_specs = [pl.BlockSpec((tr, c), lambda i: (i, 0)) for c, _ in outs]
    out_specs += [pl.BlockSpec((1, c), lambda i: (0, 0)) for c in sums]
    out_shape = [jax.ShapeDtypeStruct((rows, c), dt) for c, dt in outs]
    out_shape += [jax.ShapeDtypeStruct((1, c), F32) for c in sums]
    res = pl.pallas_call(
        body, name=name, grid=(rows // tr,), in_specs=in_specs, out_specs=out_specs, out_shape=out_shape,
        compiler_params=_params(("arbitrary",)))(*[t[0] for t in tens], *vecs)
    return res


def _sigmoid(z):
    return 1.0 / (1.0 + jnp.exp(-z))


def _rms_fwd(x, g, name):
    def fn(xb, gb):
        r = lax.rsqrt(jnp.mean(xb * xb, axis=-1, keepdims=True) + EPS)
        return ((xb * r) * gb,), ()
    return _ew(fn, [x], [g.reshape(1, -1)], outs=[(x.shape[1], BF16)], name=name)[0]


def _rms_bwd(x, g, dh, dres, name):
    def fn(xb, dhb, drb, gb):
        r = lax.rsqrt(jnp.mean(xb * xb, axis=-1, keepdims=True) + EPS)
        xh = xb * r
        dhf = dhb.astype(F32)
        dy = dhf * gb
        dx = r * (dy - xh * jnp.mean(dy * xh, axis=-1, keepdims=True))
        return (drb + dx,), (dhf * xh,)
    dx, dg = _ew(fn, [x, dh, dres], [g.reshape(1, -1)], outs=[(x.shape[1], F32)], sums=[x.shape[1]], name=name)
    return dx, dg[0]


def _mm_rms_bwd(a, b, x, g, dres, *, name, add=None, tm=512, tk=1024, b_tiles=None):
    m, k = a.shape
    n = x.shape[1]
    assert x.shape == (m, n) and (b_tiles is not None or b.shape == (n, k))
    tm, tk = _tile(m, tm), _tile(k, tk)
    nk = k // tk
    has_add = add is not None

    def body(*refs):
        a_ref, b_ref, x_ref, dr_ref, g_ref = refs[:5]
        add_ref = refs[5] if has_add else None
        dx_ref, dg_ref, acc_ref = refs[-3:]
        i, kk = pl.program_id(0), pl.program_id(1)

        @pl.when(kk == 0)
        def _():
            acc_ref[...] = jnp.zeros_like(acc_ref)

        @pl.when((kk == 0) & (i == 0))
        def _():
            dg_ref[...] = jnp.zeros_like(dg_ref)

        acc_ref[...] += lax.dot_general(a_ref[...].astype(MXU_DTYPE), b_ref[...].reshape((n, tk)).astype(MXU_DTYPE),
                                        (((1,), (1,)), ((), ())), preferred_element_type=F32)

        @pl.when(kk == nk - 1)
        def _():
            dh = acc_ref[...]
            if has_add:
                dh = dh + add_ref[...]
            xb = x_ref[...]
            r = lax.rsqrt(jnp.mean(xb * xb, axis=-1, keepdims=True) + EPS)
            xh = xb * r
            dy = dh * g_ref[...]
            dx_ref[...] = dr_ref[...] + r * (dy - xh * jnp.mean(dy * xh, axis=-1, keepdims=True))
            dg_ref[...] += jnp.sum(dh * xh, axis=0, keepdims=True)

    row = pl.BlockSpec((tm, n), lambda i, kk: (i, 0))
    vec = pl.BlockSpec((1, n), lambda i, kk: (0, 0))
    ins = [a, b, x, dres, g.reshape(1, n)] + ([add] if has_add else [])
    b_spec = pl.BlockSpec((n, tk), lambda i, kk: (0, kk)) if b_tiles is None else pl.BlockSpec(*b_tiles)
    specs = [pl.BlockSpec((tm, tk), lambda i, kk: (i, kk)), b_spec, row, row, vec]
    specs += [row] if has_add else []
    dx, dg = pl.pallas_call(
        body, name=name, grid=(m // tm, nk), in_specs=specs, out_specs=[row, vec],
        out_shape=[jax.ShapeDtypeStruct((m, n), F32), jax.ShapeDtypeStruct((1, n), F32)],
        scratch_shapes=[pltpu.VMEM((tm, n), F32)],
        compiler_params=_params(("arbitrary", "arbitrary")))(*ins)
    return dx, dg[0]


FFN_ROWS = 512
FFN_COLS = D_FF // 2


def _ffn_gate_up(x, gain, w_gu, layer):
    s, d = x.shape
    tm = min(FFN_ROWS, s)
    halves = D_FF // FFN_COLS

    def body(x_ref, gain_ref, wg_ref, wu_ref, h_ref, g_ref, u_ref, a_ref):
        @pl.when(pl.program_id(1) == 0)
        def _():
            xb = x_ref[...]
            h_ref[...] = ((xb * lax.rsqrt(jnp.mean(xb * xb, axis=-1, keepdims=True) + EPS)) * gain_ref[...]).astype(h_ref.dtype)

        hb = h_ref[...].astype(MXU_DTYPE)
        g = jnp.dot(hb, wg_ref[0, 0], preferred_element_type=F32)
        u = jnp.dot(hb, wu_ref[0, 0], preferred_element_type=F32)
        g_ref[...] = g.astype(g_ref.dtype)
        u_ref[...] = u.astype(u_ref.dtype)
        a_ref[...] = (g * _sigmoid(g) * u).astype(a_ref.dtype)

    row = pl.BlockSpec((tm, d), lambda i, jj: (i, 0))
    tile = pl.BlockSpec((tm, FFN_COLS), lambda i, jj: (i, jj))
    return pl.pallas_call(
        body, name="ffn_gate_up", grid=(s // tm, halves),
        in_specs=[row, pl.BlockSpec((1, d), lambda i, jj: (0, 0)),
                  pl.BlockSpec((1, 1, d, FFN_COLS), lambda i, jj: (jj, layer, 0, 0)),
                  pl.BlockSpec((1, 1, d, FFN_COLS), lambda i, jj: (halves + jj, layer, 0, 0))],
        out_specs=[row, tile, tile, tile],
        out_shape=[jax.ShapeDtypeStruct((s, d), BF16)] + [jax.ShapeDtypeStruct((s, D_FF), BF16)] * 3,
        compiler_params=_params(("parallel", "arbitrary")))(x, gain.reshape(1, d), w_gu, w_gu)


def _ffn_dgate_up(dx2, w_down, layer, g, u):
    s, d = dx2.shape
    tm = min(FFN_ROWS, s)
    halves = D_FF // FFN_COLS

    def body(dx_ref, w_ref, g_ref, u_ref, o_ref):
        jj = pl.program_id(1)
        df = lax.dot_general(dx_ref[...].astype(MXU_DTYPE), w_ref[...].reshape((FFN_COLS, d)), (((1,), (1,)), ((), ())),
                             preferred_element_type=F32)
        gf, uf = g_ref[...].astype(F32), u_ref[...].astype(F32)
        sg = _sigmoid(gf)
        dg = df * uf * (sg * (1.0 + gf * (1.0 - sg)))
        du = df * (gf * sg)
        o_ref[:, pl.ds(pl.multiple_of(jj * FFN_COLS, LANES), FFN_COLS)] = dg.astype(o_ref.dtype)
        o_ref[:, pl.ds(pl.multiple_of(D_FF + jj * FFN_COLS, LANES), FFN_COLS)] = du.astype(o_ref.dtype)

    tile = pl.BlockSpec((tm, FFN_COLS), lambda i, jj: (i, jj))
    return pl.pallas_call(
        body, name="ffn_dgate_up", grid=(s // tm, halves),
        in_specs=[pl.BlockSpec((tm, d), lambda i, jj: (i, 0)),
                  pl.BlockSpec((2, 1, FFN_COLS // 2, d), lambda i, jj: (jj, layer, 0, 0)), tile, tile],
        out_specs=pl.BlockSpec((tm, 2 * D_FF), lambda i, jj: (i, 0)),
        out_shape=jax.ShapeDtypeStruct((s, 2 * D_FF), BF16),
        compiler_params=_params(("parallel", "arbitrary")))(dx2, w_down, g, u)


def _ffn_fwd(x1, gain, w_gu, w_down, layer):
    d = x1.shape[1]
    h, g, u, act = _ffn_gate_up(x1, gain, w_gu, layer)
    x2 = _mm(act, w_down, name="ffn_down", add=x1, tk=FFN_COLS,
             b_tiles=(D_FF, d, (2, 1, FFN_COLS // 2, d), lambda i, j, kk: (kk, layer, 0, 0)))
    return x2, (x1, h, g, u, act)


def _ffn_bwd(dx2, saved, gain, w_gu, w_down, layer, dw_gu, dw_down):
    x1, h, g, u, act = saved
    d = x1.shape[1]
    dw_down = _mm(act, dx2, ta=True, name="ffn_dw_down", out_dtype=BF16, tm=FFN_COLS, into=dw_down,
                  out_tiles=(dw_down.shape, (2, 1, FFN_COLS // 2, d), lambda i, j, kk: (i, layer, 0, 0)))
    dgu = _ffn_dgate_up(dx2, w_down, layer, g, u)
    dw_gu = _mm(h, dgu, ta=True, name="ffn_dw_gu", out_dtype=BF16, tn=FFN_COLS, into=dw_gu,
                out_tiles=(dw_gu.shape, (1, 1, d, FFN_COLS), lambda i, j, kk: (j, layer, i, 0)))
    dx1, dgain = _mm_rms_bwd(dgu, w_gu, x1, gain, dx2, name="ffn_dh", tm=1024, tk=FFN_COLS,
                             b_tiles=((1, 1, d, FFN_COLS), lambda i, kk: (kk, layer, 0, 0)))
    return dx1, dgain, dw_gu, dw_down


def _loss_head(x, gain, target):
    d = x.shape[1]

    def fn(xb, tb, gb):
        r = lax.rsqrt(jnp.mean(xb * xb, axis=-1, keepdims=True) + EPS)
        xh = xb * r
        y = xh * gb
        err = y - tb
        dyv = err * (1.0 / d)
        dyg = dyv * gb
        dx = r * (dyg - xh * jnp.mean(dyg * xh, axis=-1, keepdims=True))
        return (dx,), (0.5 * err * err * (1.0 / d), dyv * xh)
    dx, lsum, dg = _ew(fn, [x, target], [gain.reshape(1, -1)], outs=[(d, F32)], sums=[d, d], name="loss_head")
    return jnp.sum(lsum), dx, dg[0]


ATT_BLOCK = 256
CUM_BLOCK = 512
NEG_INF = -1e30


def _fox_gate_fwd(fl_row, b_col):
    nh, s = fl_row.shape
    tb = min(CUM_BLOCK, s)

    def body(fl_ref, b_ref, z_ref, c_ref):
        upper = (lax.broadcasted_iota(jnp.int32, (tb, tb), 0) <= lax.broadcasted_iota(jnp.int32, (tb, tb), 1)).astype(F32)
        carry = jnp.zeros((nh, 1), F32)
        for blk in range(s // tb):
            z = fl_ref[:, blk * tb:(blk + 1) * tb] + b_ref[...]
            logf = jnp.minimum(z, 0.0) - jnp.log(1.0 + jnp.exp(-jnp.abs(z)))
            cs = jnp.dot(logf, upper, precision=lax.Precision.HIGHEST, preferred_element_type=F32) + carry
            z_ref[:, blk * tb:(blk + 1) * tb] = z
            c_ref[:, blk * tb:(blk + 1) * tb] = cs
            carry = cs[:, tb - 1:tb]

    return pl.pallas_call(body, name="fox_gate_fwd", out_shape=[jax.ShapeDtypeStruct((nh, s), F32)] * 2,
                          compiler_params=_params())(fl_row, b_col)


def _fox_gate_bwd(dc_row, z_row):
    nh, s = dc_row.shape
    tb = min(CUM_BLOCK, s)

    def body(dc_ref, z_ref, dz_ref, db_ref):
        lower = (lax.broadcasted_iota(jnp.int32, (tb, tb), 0) >= lax.broadcasted_iota(jnp.int32, (tb, tb), 1)).astype(F32)
        carry = jnp.zeros((nh, 1), F32)
        db = jnp.zeros((nh, 1), F32)
        for blk in reversed(range(s // tb)):
            dc = dc_ref[:, blk * tb:(blk + 1) * tb]
            rs = jnp.dot(dc, lower, precision=lax.Precision.HIGHEST, preferred_element_type=F32) + carry
            dz = rs * _sigmoid(-z_ref[:, blk * tb:(blk + 1) * tb])
            dz_ref[:, blk * tb:(blk + 1) * tb] = dz
            db = db + jnp.sum(dz, axis=1, keepdims=True)
            carry = rs[:, 0:1]
        db_ref[...] = db

    return pl.pallas_call(body, name="fox_gate_bwd",
                          out_shape=[jax.ShapeDtypeStruct((nh, s), F32), jax.ShapeDtypeStruct((nh, 1), F32)],
                          compiler_params=_params())(dc_row, z_row)


def _head_masks(rows):
    lane = lax.broadcasted_iota(jnp.int32, (rows, LANES), 1)
    return lane < FOX_HEAD_DIM


ATT_QUERIES = 512


def _lanes(a, width):
    return jnp.concatenate([a] * (width // LANES), axis=1)


def _attn_fwd_t(qkv, c_lanes):
    s = qkv.shape[0]
    t, tq = min(ATT_BLOCK, s), min(2 * ATT_QUERIES, s)
    nq, per = s // tq, tq // t
    scale = FOX_HEAD_DIM ** -0.5
    np_ = HEAD_PAIRS
    nt = (((1,), (1,)), ((), ()))

    def body(q_ref, k_ref, v_ref, c_ref, o_ref, lse_ref, vt_ref):
        i = pl.program_id(1)

        @pl.when(i == 0)
        def _():
            for r in range(s // t):
                vt_ref[:, r * t:(r + 1) * t] = v_ref[r * t:(r + 1) * t, :].astype(F32).T.astype(vt_ref.dtype)

        first = _head_masks(tq)
        upper = lax.broadcasted_iota(jnp.int32, (LANES, tq), 0) < FOX_HEAD_DIM
        qs = q_ref[...] * scale
        zero = jnp.zeros_like(qs)
        qh = (jnp.where(first, qs, zero), jnp.where(first, zero, qs))
        def block(j, carry, diagonal=None):
            skip = 0 if diagonal is None else diagonal * t
            wide = tq - skip

            def join(old, new):
                return new if skip == 0 else jnp.concatenate([old[:, :skip], new], axis=1)

            ms, ls, acc = carry
            start = pl.multiple_of(j * t, t)
            kb, vt = k_ref[pl.ds(start, t), :], vt_ref[:, pl.ds(start, t)]
            new_m, new_l, alphas, pvs = [], [], [], []
            for hh in range(2):
                sc = lax.dot_general(kb, qh[hh][skip:], nt, preferred_element_type=F32) - _lanes(c_ref[0, hh, pl.ds(start, t), :], wide)
                if diagonal is not None:
                    visible = lax.broadcasted_iota(jnp.int32, (t, wide), 0) <= lax.broadcasted_iota(jnp.int32, (t, wide), 1)
                    sc = jnp.where(visible, sc, NEG_INF)
                m_old, l_old = ms[hh][:, skip:], ls[hh][:, skip:]
                m_new = jnp.maximum(m_old, jnp.max(sc, axis=0, keepdims=True))
                p = jnp.exp(sc - m_new)
                alpha = jnp.exp(m_old - m_new)
                new_m.append(join(ms[hh], m_new))
                new_l.append(join(ls[hh], alpha * l_old + jnp.sum(p, axis=0, keepdims=True)))
                alphas.append(alpha)
                p_hi = p.astype(MXU_DTYPE)
                p_lo = (p - p_hi.astype(F32)).astype(MXU_DTYPE)
                pvs.append(jnp.dot(vt, p_hi, preferred_element_type=F32) + jnp.dot(vt, p_lo, preferred_element_type=F32))
            seen = jnp.where(upper[:, skip:], alphas[0], alphas[1]) * acc[:, skip:] + jnp.where(upper[:, skip:], pvs[0], pvs[1])
            return tuple(new_m), tuple(new_l), join(acc, seen)

        neg, nil = jnp.full((1, tq), NEG_INF, F32), jnp.zeros((1, tq), F32)
        carry = lax.fori_loop(0, per * i, block, ((neg, neg), (nil, nil), jnp.zeros((LANES, tq), F32)))
        for d in range(per):
            carry = block(per * i + d, carry, diagonal=d)
        ms, ls, acc = carry
        o_ref[...] = (acc / jnp.where(upper, ls[0], ls[1])).T.astype(o_ref.dtype)
        lse_ref[0] = jnp.concatenate([ms[0] + jnp.log(ls[0]), ms[1] + jnp.log(ls[1])], axis=0)

    return pl.pallas_call(
        body, name="fox_attn_fwd", grid=(np_, nq),
        in_specs=[pl.BlockSpec((tq, LANES), lambda p, i: (i, p)),
                  pl.BlockSpec((s, LANES), lambda p, i: (0, np_ + p)),
                  pl.BlockSpec((s, LANES), lambda p, i: (0, 2 * np_ + p)),
                  pl.BlockSpec((1, 2, s, LANES), lambda p, i: (p, 0, 0, 0))],
        out_specs=[pl.BlockSpec((tq, LANES), lambda p, i: (i, p)),
                   pl.BlockSpec((1, 2, tq), lambda p, i: (p, 0, i))],
        out_shape=[jax.ShapeDtypeStruct((s, D_MODEL), F32), jax.ShapeDtypeStruct((np_, 2, s), F32)],
        scratch_shapes=[pltpu.VMEM((LANES, s), MXU_DTYPE)],
        compiler_params=_params(("parallel", "arbitrary")))(qkv, qkv, qkv, c_lanes)


def _attn_bwd_t(qkv, do, lse, delta, c_lanes):
    s = qkv.shape[0]
    t, tq = min(ATT_BLOCK, s), min(ATT_QUERIES, s)
    nb, nq, per = s // t, s // tq, tq // t
    scale = FOX_HEAD_DIM ** -0.5
    np_ = HEAD_PAIRS
    nt = (((1,), (1,)), ((), ()))

    def body(q_ref, k_ref, v_ref, do_ref, lse_ref, dl_ref, c_ref, dq_ref, dk_ref, dv_ref, dc_ref, dqt_ref):
        j = pl.program_id(1)
        first_q, first = _head_masks(tq), _head_masks(t)
        upper = lax.broadcasted_iota(jnp.int32, (LANES, tq), 0) < FOX_HEAD_DIM
        causal = (lax.broadcasted_iota(jnp.int32, (t, tq), 0) + (j % per) * t) <= lax.broadcasted_iota(jnp.int32, (t, tq), 1)
        kb, vb = k_ref[...], v_ref[...]
        kt = kb.astype(F32).T.astype(MXU_DTYPE)
        cb = (_lanes(c_ref[0, 0], tq), _lanes(c_ref[0, 1], tq))

        @pl.when(j == 0)
        def _():
            dqt_ref[...] = jnp.zeros_like(dqt_ref)

        def step(i, carry, masked):
            dk_acc, dv_acc, dc_accs = carry
            start = pl.multiple_of(i * tq, tq)
            qs = q_ref[pl.ds(start, tq), :] * scale
            dob = do_ref[pl.ds(start, tq), :]
            zero = jnp.zeros_like(qs)
            dks, dvs, dqs, dcs = [], [], [], []
            for hh in range(2):
                qh = jnp.where(first_q, qs, zero) if hh == 0 else jnp.where(first_q, zero, qs)
                doh = jnp.where(first_q, dob, zero) if hh == 0 else jnp.where(first_q, zero, dob)
                sc = lax.dot_general(kb, qh, nt, preferred_element_type=F32)
                p = jnp.exp(sc - cb[hh] - lse_ref[0, hh:hh + 1, pl.ds(start, tq)])
                if masked:
                    p = jnp.where(causal, p, 0.0)
                dp = lax.dot_general(vb, doh, nt, preferred_element_type=F32)
                ds = p * (dp - dl_ref[0, hh:hh + 1, pl.ds(start, tq)])
                pb, dsb = p.astype(MXU_DTYPE), ds.astype(MXU_DTYPE)
                dvs.append(jnp.dot(pb, dob, preferred_element_type=F32))
                dks.append(jnp.dot(dsb, qs, preferred_element_type=F32))
                dqs.append(jnp.dot(kt, dsb, preferred_element_type=F32))
                dcs.append(dc_accs[hh] - jnp.sum(ds, axis=1, keepdims=True))
            dqt_ref[:, pl.ds(start, tq)] += jnp.where(upper, dqs[0], dqs[1]) * scale
            return (dk_acc + jnp.where(first, dks[0], dks[1]), dv_acc + jnp.where(first, dvs[0], dvs[1]), tuple(dcs))

        nil, col = jnp.zeros((t, LANES), F32), jnp.zeros((t, 1), F32)
        carry = step(j // per, (nil, nil, (col, col)), True)
        dk_acc, dv_acc, dc_accs = lax.fori_loop(j // per + 1, nq, functools.partial(step, masked=False), carry)
        dk_ref[...] = dk_acc.astype(dk_ref.dtype)
        dv_ref[...] = dv_acc.astype(dv_ref.dtype)
        dc_ref[0, 0] = jnp.broadcast_to(dc_accs[0], (t, LANES))
        dc_ref[0, 1] = jnp.broadcast_to(dc_accs[1], (t, LANES))

        @pl.when(j == nb - 1)
        def _():
            for r in range(nb):
                dq_ref[r * t:(r + 1) * t, :] = dqt_ref[:, r * t:(r + 1) * t].T

    row = pl.BlockSpec((1, 2, s), lambda p, j: (p, 0, 0))
    return pl.pallas_call(
        body, name="fox_attn_bwd", grid=(np_, nb),
        in_specs=[pl.BlockSpec((s, LANES), lambda p, j: (0, p)),
                  pl.BlockSpec((t, LANES), lambda p, j: (j, np_ + p)),
                  pl.BlockSpec((t, LANES), lambda p, j: (j, 2 * np_ + p)),
                  pl.BlockSpec((s, LANES), lambda p, j: (0, p)), row, row,
                  pl.BlockSpec((1, 2, t, LANES), lambda p, j: (p, 0, j, 0))],
        out_specs=[pl.BlockSpec((s, LANES), lambda p, j: (0, p)),
                   pl.BlockSpec((t, LANES), lambda p, j: (j, p)),
                   pl.BlockSpec((t, LANES), lambda p, j: (j, p)),
                   pl.BlockSpec((1, 2, t, LANES), lambda p, j: (p, 0, j, 0))],
        out_shape=[jax.ShapeDtypeStruct((s, D_MODEL), F32), jax.ShapeDtypeStruct((s, D_MODEL), BF16),
                   jax.ShapeDtypeStruct((s, D_MODEL), BF16), jax.ShapeDtypeStruct((np_, 2, s, LANES), F32)],
        scratch_shapes=[pltpu.VMEM((LANES, s), F32)],
        compiler_params=_params(("parallel", "arbitrary")))(qkv, qkv, qkv, do, lse, delta, c_lanes)


def _head_sums(a, b, name):
    d = a.shape[1]
    sel = (jnp.arange(d)[:, None] // FOX_HEAD_DIM == jnp.arange(LANES)[None, :]).astype(F32)

    def fn(ab, bb, selb):
        prod = ab.astype(F32) * bb.astype(F32)
        return (jnp.dot(prod, selb, precision=lax.Precision.HIGHEST, preferred_element_type=F32),), ()
    return _ew(fn, [a, b], [sel], outs=[(LANES, F32)], name=name)[0]


def _fox_fwd(x, gain, w_qkv, w_f, b_f, w_out):
    s = x.shape[0]
    h = _rms_fwd(x, gain, "mix_norm")
    qkv = _mm(h, w_qkv, name="fox_qkv", out_dtype=BF16)
    fl = _mm(h, w_f, name="fox_f", tn=LANES)
    z_row, c_rowf = _fox_gate_fwd(fl[:, :FOX_HEADS].T, b_f.reshape(FOX_HEADS, 1))
    c_lanes = jnp.broadcast_to(c_rowf.reshape(HEAD_PAIRS, 2, s, 1), (HEAD_PAIRS, 2, s, LANES))
    o, lse = _attn_fwd_t(qkv, c_lanes)
    x1 = _mm(o, w_out, name="fox_out", add=x)
    return x1, (x, h, qkv, z_row, c_lanes, o, lse)


def _fox_bwd(dx1, saved, gain, w_qkv, w_f, w_out):
    x, h, qkv, z_row, c_lanes, o, lse = saved
    s = x.shape[0]
    do = _mm(dx1, w_out, tb=True, name="fox_do", out_dtype=BF16)
    dw_out = _mm(o, dx1, ta=True, name="fox_dw_out", out_dtype=BF16)
    delta = _head_sums(do, o, "fox_delta")[:, :FOX_HEADS].T.reshape(HEAD_PAIRS, 2, s)
    dq, dk, dv, dc = _attn_bwd_t(qkv, do, lse, delta, c_lanes)
    dz_row, db = _fox_gate_bwd(dc[..., 0].reshape(FOX_HEADS, s), z_row)
    dqkv = jnp.concatenate([dq.astype(BF16), dk, dv], axis=1)
    dfl = jnp.pad(dz_row.T, ((0, 0), (0, LANES - FOX_HEADS))).astype(BF16)
    dw_qkv = _mm(h, dqkv, ta=True, name="fox_dw_qkv", out_dtype=BF16)
    dw_f = _mm(h, dfl, ta=True, name="fox_dw_f", out_dtype=BF16, tn=LANES)
    dh = _mm(dqkv, w_qkv, tb=True, name="fox_dh_qkv")
    dx, dgain = _mm_rms_bwd(dfl, w_f, x, gain, dx1, name="fox_dh_f", add=dh)
    dw_in = jnp.concatenate([dw_qkv, dw_f[:, :FOX_HEADS]], axis=1)
    return dx, dgain, dw_in, db.reshape(FOX_HEADS), dw_out


S5_ROWS = 2048
SCAN_CHUNKS = SUBLANES


def _s5_operands(a_re, a_im, log_dt, b_re, b_im, c_re, c_im):
    dt = jnp.exp(log_dt)[:, None]
    mag, ang = jnp.exp(a_re * dt), a_im * dt
    lr, li = mag * jnp.cos(ang), mag * jnp.sin(ang)
    den = a_re * a_re + a_im * a_im
    cr = ((lr - 1.0) * a_re + li * a_im) / den
    ci = (li * a_re - (lr - 1.0) * a_im) / den
    bbr = cr[..., None] * b_re - ci[..., None] * b_im
    bbi = cr[..., None] * b_im + ci[..., None] * b_re
    nb = S5_BLOCKS
    lam = jnp.stack([lr.reshape(nb, 2, S5_HALF), li.reshape(nb, 2, S5_HALF)], axis=2)
    eye4, eye2 = jnp.eye(4, dtype=F32), jnp.eye(2, dtype=F32)
    bb = jnp.stack([bbr, bbi], axis=0).reshape(2, nb, 2, 4, S5_STATE, S5_GROUP)
    bmat = jnp.einsum("rbhgpc,kg,jh->bhjkcrgp", bb, eye4, eye2).reshape(nb, 2, 128, 2 * S5_HALF)
    cc = jnp.stack([c_re, -c_im], axis=0).reshape(2, nb, 2, 4, S5_GROUP, S5_STATE)
    cmat = jnp.einsum("rbhgcp,kg,jh->bhrgpjkc", cc, eye4, eye2).reshape(nb, 2, 2 * S5_HALF, 128)
    return lam, bmat, cmat


def _time_to_scan_order(a):
    s, d = a.shape
    return a.reshape(SCAN_CHUNKS, s // SCAN_CHUNKS, d).transpose(1, 0, 2).reshape(s, d)


def _scan_to_time_order(a):
    s, d = a.shape
    return a.reshape(s // SCAN_CHUNKS, SCAN_CHUNKS, d).transpose(1, 0, 2).reshape(s, d)


def _scan_chunks(xr_ref, xi_ref, lr, li, nst, reverse, after_step=None, state=None):
    lanes = lr.shape[1]
    lr8, li8 = jnp.broadcast_to(lr, (SUBLANES, lanes)), jnp.broadcast_to(li, (SUBLANES, lanes))
    zero8 = jnp.zeros((SUBLANES, lanes), F32)

    def rows_of(n):
        s = (nst - 1 - n) if reverse else n
        return s, pl.ds(pl.multiple_of(s * SUBLANES, SUBLANES), SUBLANES)

    def local(n, carry):
        pr, pi = carry
        _, rows = rows_of(n)
        nr = lr8 * pr - li8 * pi + xr_ref[rows, :]
        ni = lr8 * pi + li8 * pr + xi_ref[rows, :]
        xr_ref[rows, :] = nr
        xi_ref[rows, :] = ni
        return nr, ni

    er, ei = lax.fori_loop(0, nst, local, (zero8, zero8))
    pr, pi = lr, li
    for _ in range(int(math.log2(nst))):
        pr, pi = pr * pr - pi * pi, 2.0 * pr * pi
    tr = ti = jnp.zeros((1, lanes), F32)
    ent_r, ent_i = [None] * SCAN_CHUNKS, [None] * SCAN_CHUNKS
    for k in (reversed(range(SCAN_CHUNKS)) if reverse else range(SCAN_CHUNKS)):
        ent_r[k], ent_i[k] = tr, ti
        tr, ti = er[k:k + 1] + (pr * tr - pi * ti), ei[k:k + 1] + (pr * ti + pi * tr)
    in_r, in_i = jnp.concatenate(ent_r, axis=0), jnp.concatenate(ent_i, axis=0)

    def fix(n, carry):
        wr, wi, st = carry
        s, rows = rows_of(n)
        nr = xr_ref[rows, :] + (wr * in_r - wi * in_i)
        ni = xi_ref[rows, :] + (wr * in_i + wi * in_r)
        xr_ref[rows, :] = nr
        xi_ref[rows, :] = ni
        if after_step is not None:
            st = after_step(s, nr, ni, st)
        return wr * lr8 - wi * li8, wr * li8 + wi * lr8, st

    _, _, state = lax.fori_loop(0, nst, fix, (lr8, li8, state))
    return in_r, in_i, state


def _s5_fill_states(u_ref, bm, xr_ref, xi_ref, s):
    rc = min(S5_ROWS, s)

    def fill(r, _):
        rows = pl.ds(pl.multiple_of(r * rc, rc), rc)
        bu = jnp.dot(u_ref[rows, :].astype(MXU_DTYPE), bm, preferred_element_type=F32)
        xr_ref[rows, :] = bu[:, :S5_HALF]
        xi_ref[rows, :] = bu[:, S5_HALF:]
        return 0
    lax.fori_loop(0, s // rc, fill, 0)


def _s5_specs():
    return [pl.BlockSpec((1, 2, 2, S5_HALF), lambda b: (b, 0, 0, 0)),
            pl.BlockSpec((1, 2, 128, 2 * S5_HALF), lambda b: (b, 0, 0, 0)),
            pl.BlockSpec((1, 2, 2 * S5_HALF, 128), lambda b: (b, 0, 0, 0)),
            pl.BlockSpec((1, LANES), lambda b: (0, b))]


def _s5_scan_fwd(u, lam, bmat, cmat, dvec):
    s = u.shape[0]
    nst = s // SCAN_CHUNKS
    rc = min(S5_ROWS, s)

    def body(u_ref, lam_ref, b_ref, c_ref, d_ref, y_ref, xr_ref, xi_ref):
        y_ref[...] = u_ref[...] * d_ref[...]
        for hb in range(2):
            _s5_fill_states(u_ref, b_ref[0, hb], xr_ref, xi_ref, s)
            _scan_chunks(xr_ref, xi_ref, lam_ref[0, hb, 0:1, :], lam_ref[0, hb, 1:2, :], nst, False)
            cm = c_ref[0, hb]

            def emit(r, _, cm=cm):
                rows = pl.ds(pl.multiple_of(r * rc, rc), rc)
                y_ref[rows, :] += (jnp.dot(xr_ref[rows, :].astype(MXU_DTYPE), cm[:S5_HALF], preferred_element_type=F32)
                                   + jnp.dot(xi_ref[rows, :].astype(MXU_DTYPE), cm[S5_HALF:], preferred_element_type=F32))
                return 0
            lax.fori_loop(0, s // rc, emit, 0)

    blk = pl.BlockSpec((s, LANES), lambda b: (0, b))
    return pl.pallas_call(
        body, name="s5_scan_fwd", grid=(S5_BLOCKS,), in_specs=[blk] + _s5_specs(), out_specs=blk,
        out_shape=jax.ShapeDtypeStruct(u.shape, F32),
        scratch_shapes=[pltpu.VMEM((s, S5_HALF), F32)] * 2,
        compiler_params=_params(("parallel",)))(u, lam, bmat, cmat, dvec)


def _s5_scan_bwd(u, dy, lam, bmat, cmat, dvec):
    s = u.shape[0]
    nst = s // SCAN_CHUNKS
    rc = min(S5_ROWS, s)
    nt = (((1,), (1,)), ((), ()))
    tn = (((0,), (0,)), ((), ()))

    def body(u_ref, dy_ref, lam_ref, b_ref, c_ref, d_ref, du_ref, db_ref, dc_ref, dl_ref, dd_ref,
             xr_ref, xi_ref, gr_ref, gi_ref):
        du_ref[...] = dy_ref[...] * d_ref[...]
        dd_ref[...] = jnp.sum(dy_ref[...] * u_ref[...], axis=0, keepdims=True)
        db_ref[...] = jnp.zeros_like(db_ref)
        dc_ref[...] = jnp.zeros_like(dc_ref)
        for hb in range(2):
            bm, cm = b_ref[0, hb], c_ref[0, hb]
            lr, li = lam_ref[0, hb, 0:1, :], lam_ref[0, hb, 1:2, :]
            _s5_fill_states(u_ref, bm, xr_ref, xi_ref, s)
            xin_r, xin_i, _ = _scan_chunks(xr_ref, xi_ref, lr, li, nst, False)

            def fill_g(r, _, cm=cm):
                rows = pl.ds(pl.multiple_of(r * rc, rc), rc)
                g = lax.dot_general(dy_ref[rows, :].astype(MXU_DTYPE), cm, nt, preferred_element_type=F32)
                gr_ref[rows, :] = g[:, :S5_HALF]
                gi_ref[rows, :] = g[:, S5_HALF:]
                return 0
            lax.fori_loop(0, s // rc, fill_g, 0)
            def lam_grad(st, g_r, g_i, acc, xin_r=xin_r, xin_i=xin_i):
                prev = pl.ds(pl.multiple_of(jnp.maximum(st - 1, 0) * SUBLANES, SUBLANES), SUBLANES)
                x_r = jnp.where(st > 0, xr_ref[prev, :], xin_r)
                x_i = jnp.where(st > 0, xi_ref[prev, :], xin_i)
                return acc[0] + (g_r * x_r + g_i * x_i), acc[1] + (g_i * x_r - g_r * x_i)

            zero8 = jnp.zeros((SUBLANES, S5_HALF), F32)
            _, _, (a_r, a_i) = _scan_chunks(gr_ref, gi_ref, lr, -li, nst, True, after_step=lam_grad, state=(zero8, zero8))
            dl_ref[0, hb] = jnp.concatenate([jnp.sum(a_r, axis=0, keepdims=True),
                                             jnp.sum(a_i, axis=0, keepdims=True)], axis=0)

            def emit(r, _, bm=bm, hb=hb):
                rows = pl.ds(pl.multiple_of(r * rc, rc), rc)
                g = jnp.concatenate([gr_ref[rows, :], gi_ref[rows, :]], axis=1).astype(MXU_DTYPE)
                x = jnp.concatenate([xr_ref[rows, :], xi_ref[rows, :]], axis=1).astype(MXU_DTYPE)
                du_ref[rows, :] += lax.dot_general(g, bm, nt, preferred_element_type=F32)
                db_ref[0, hb] += lax.dot_general(u_ref[rows, :].astype(MXU_DTYPE), g, tn, preferred_element_type=F32)
                dc_ref[0, hb] += lax.dot_general(dy_ref[rows, :].astype(MXU_DTYPE), x, tn, preferred_element_type=F32)
                return 0
            lax.fori_loop(0, s // rc, emit, 0)

    blk = pl.BlockSpec((s, LANES), lambda b: (0, b))
    mat = pl.BlockSpec((1, 2, 128, 2 * S5_HALF), lambda b: (b, 0, 0, 0))
    return pl.pallas_call(
        body, name="s5_scan_bwd", grid=(S5_BLOCKS,), in_specs=[blk, blk] + _s5_specs(),
        out_specs=[blk, mat, mat, pl.BlockSpec((1, 2, 2, S5_HALF), lambda b: (b, 0, 0, 0)),
                   pl.BlockSpec((1, LANES), lambda b: (0, b))],
        out_shape=[jax.ShapeDtypeStruct(u.shape, F32),
                   jax.ShapeDtypeStruct((S5_BLOCKS, 2, 128, 2 * S5_HALF), F32),
                   jax.ShapeDtypeStruct((S5_BLOCKS, 2, 128, 2 * S5_HALF), F32),
                   jax.ShapeDtypeStruct((S5_BLOCKS, 2, 2, S5_HALF), F32),
                   jax.ShapeDtypeStruct((1, D_MODEL), F32)],
        scratch_shapes=[pltpu.VMEM((s, S5_HALF), F32)] * 4,
        compiler_params=_params(("parallel",)))(u, dy, lam, bmat, cmat, dvec)


_GELU_C = math.sqrt(2.0 / math.pi)


def _gelu_parts(y):
    inner = _GELU_C * (y + 0.044715 * y * y * y)
    th = jnp.tanh(inner)
    return 0.5 * y * (1.0 + th), th


def _s5_fwd(x, gain, w_in, ssm, dvec, w_glu):
    lam, bmat, cmat = ssm
    h = _rms_fwd(x, gain, "mix_norm")
    u = _mm(h, w_in, name="s5_in")
    y = _scan_to_time_order(_s5_scan_fwd(_time_to_scan_order(u), lam, bmat.astype(MXU_DTYPE), cmat.astype(MXU_DTYPE), dvec))
    g = _ew(lambda yb: ((_gelu_parts(yb)[0],), ()), [y], outs=[(D_MODEL, BF16)], name="s5_gelu")[0]
    vg = _mm(g, w_glu, name="s5_glu", out_dtype=BF16)

    def glu_fn(vb, gb, xb):
        return (xb + vb.astype(F32) * _sigmoid(gb.astype(F32)),), ()
    x1 = _ew(glu_fn, [(vg, D_MODEL, 0), (vg, D_MODEL, 1), x], outs=[(D_MODEL, F32)], name="s5_gate")[0]
    return x1, (x, h, u, y, g, vg)


def _s5_bwd(dx1, saved, gain, w_in, ssm, dvec, w_glu):
    x, h, u, y, g, vg = saved
    lam, bmat, cmat = ssm

    def dglu_fn(db, vb, gb):
        vf, sg = vb.astype(F32), _sigmoid(gb.astype(F32))
        return (jnp.concatenate([db * sg, db * vf * sg * (1.0 - sg)], axis=1),), ()
    dvg = _ew(dglu_fn, [dx1, (vg, D_MODEL, 0), (vg, D_MODEL, 1)], outs=[(2 * D_MODEL, BF16)], name="s5_dgate")[0]
    dw_glu = _mm(g, dvg, ta=True, name="s5_dw_glu", out_dtype=BF16)
    dg = _mm(dvg, w_glu, tb=True, name="s5_dg")

    def dgelu_fn(dgb, yb):
        _, th = _gelu_parts(yb)
        dinner = _GELU_C * (1.0 + 3.0 * 0.044715 * yb * yb)
        return (dgb * (0.5 * (1.0 + th) + 0.5 * yb * (1.0 - th * th) * dinner),), ()
    dy = _ew(dgelu_fn, [dg, y], outs=[(D_MODEL, F32)], name="s5_dgelu")[0]
    du_s, dbm, dct, dlam, ddvec = _s5_scan_bwd(_time_to_scan_order(u), _time_to_scan_order(dy), lam,
                                               bmat.astype(MXU_DTYPE), cmat.astype(MXU_DTYPE), dvec)
    du = _scan_to_time_order(du_s)
    dw_in = _mm(h, du, ta=True, name="s5_dw_in", out_dtype=BF16)
    dx, dgain = _mm_rms_bwd(du, w_in, x, gain, dx1, name="s5_dh")
    return dx, dgain, dw_in, (dlam, dbm, jnp.swapaxes(dct, 2, 3)), ddvec, dw_glu


POOL_BLOCK = 512
N_POOL_GROUPS = len(POOL_WINDOWS)


def _pool_bands(gi, i, t):
    w = jnp.left_shift(2, gi)
    r = lax.broadcasted_iota(jnp.int32, (t, t), 0)
    c = lax.broadcasted_iota(jnp.int32, (t, t), 1)
    inside = ((c <= r) & (c > r - w)).astype(MXU_DTYPE)
    before = (c > r - w + t).astype(MXU_DTYPE)

    def inv_count(block):
        pos = block * t + lax.broadcasted_iota(jnp.int32, (t, 1), 0)
        return 1.0 / jnp.minimum(pos + 1, w).astype(F32)
    return inside, before, inv_count


def _pool_fwd(x, gain, w_grp, b_grp, scale):
    s = x.shape[0]
    t = min(POOL_BLOCK, s)
    h = _rms_fwd(x, gain, "mix_norm")

    def body(h_ref, hp_ref, w_ref, b_ref, sc_ref, x_ref, x1_ref, diff_ref):
        gi, i = pl.program_id(0), pl.program_id(1)
        inside, before, inv_count = _pool_bands(gi, i, t)
        hc = h_ref[...]
        tot = jnp.dot(inside, hc.astype(MXU_DTYPE), preferred_element_type=F32)
        prev = jnp.dot(before, hp_ref[...].astype(MXU_DTYPE), preferred_element_type=F32)
        tot = tot + jnp.where(i > 0, prev, 0.0)
        diff = (tot * inv_count(i) - hc.astype(F32)).astype(diff_ref.dtype)
        y = (jnp.dot(diff.astype(MXU_DTYPE), w_ref[0], preferred_element_type=F32) + b_ref[...]) * sc_ref[...]
        diff_ref[...] = diff
        x1_ref[...] = x_ref[...] + y

    blk = pl.BlockSpec((t, POOL_WIDTH), lambda gi, i: (i, gi))
    vec = pl.BlockSpec((1, POOL_WIDTH), lambda gi, i: (0, gi))
    x1, diff = pl.pallas_call(
        body, name="pool_fwd", grid=(N_POOL_GROUPS, s // t),
        in_specs=[blk, pl.BlockSpec((t, POOL_WIDTH), lambda gi, i: (jnp.maximum(i - 1, 0), gi)),
                  pl.BlockSpec((1, POOL_WIDTH, POOL_WIDTH), lambda gi, i: (gi, 0, 0)), vec, vec, blk],
        out_specs=[blk, blk],
        out_shape=[jax.ShapeDtypeStruct(x.shape, F32), jax.ShapeDtypeStruct(x.shape, BF16)],
        compiler_params=_params(("parallel", "arbitrary")))(h, h, w_grp, b_grp, scale, x)
    return x1, (x, diff)


def _pool_bwd(dx1, saved, gain, w_grp, b_grp, scale):
    x, diff = saved
    s = x.shape[0]
    t = min(POOL_BLOCK, s)
    nb = s // t

    def body1(dx_ref, diff_ref, w_ref, b_ref, sc_ref, dd_ref, dw_ref, db_ref, dsc_ref):
        i = pl.program_id(1)

        @pl.when(i == 0)
        def _():
            dw_ref[...] = jnp.zeros_like(dw_ref)
            db_ref[...] = jnp.zeros_like(db_ref)
            dsc_ref[...] = jnp.zeros_like(dsc_ref)

        dfb = diff_ref[...].astype(MXU_DTYPE)
        ypre = jnp.dot(dfb, w_ref[0], preferred_element_type=F32) + b_ref[...]
        dxb = dx_ref[...]
        dy = dxb * sc_ref[...]
        dsc_ref[...] += jnp.sum(dxb * ypre, axis=0, keepdims=True)
        db_ref[...] += jnp.sum(dy, axis=0, keepdims=True)
        dyb = dy.astype(MXU_DTYPE)
        dw_ref[0] += lax.dot_general(dfb, dyb, (((0,), (0,)), ((), ())), preferred_element_type=F32)
        dd_ref[...] = lax.dot_general(dyb, w_ref[0], (((1,), (1,)), ((), ())), preferred_element_type=F32)

    blk = pl.BlockSpec((t, POOL_WIDTH), lambda gi, i: (i, gi))
    vec = pl.BlockSpec((1, POOL_WIDTH), lambda gi, i: (0, gi))
    mat = pl.BlockSpec((1, POOL_WIDTH, POOL_WIDTH), lambda gi, i: (gi, 0, 0))
    ddiff, dw, db, dsc = pl.pallas_call(
        body1, name="pool_bwd_map", grid=(N_POOL_GROUPS, nb), in_specs=[blk, blk, mat, vec, vec],
        out_specs=[blk, mat, vec, vec],
        out_shape=[jax.ShapeDtypeStruct(x.shape, F32), jax.ShapeDtypeStruct(w_grp.shape, F32),
                   jax.ShapeDtypeStruct((1, D_MODEL), F32), jax.ShapeDtypeStruct((1, D_MODEL), F32)],
        compiler_params=_params(("parallel", "arbitrary")))(dx1, diff, w_grp, b_grp, scale)

    def body2(dc_ref, dn_ref, dh_ref):
        gi, i = pl.program_id(0), pl.program_id(1)
        inside, before, inv_count = _pool_bands(gi, i, t)
        tn = (((0,), (0,)), ((), ()))
        dc = dc_ref[...]
        tot = lax.dot_general(inside, (dc * inv_count(i)).astype(MXU_DTYPE), tn, preferred_element_type=F32)
        nxt = lax.dot_general(before, (dn_ref[...] * inv_count(i + 1)).astype(MXU_DTYPE), tn, preferred_element_type=F32)
        dh_ref[...] = tot + jnp.where(i < nb - 1, nxt, 0.0) - dc

    dh = pl.pallas_call(
        body2, name="pool_bwd_window", grid=(N_POOL_GROUPS, nb),
        in_specs=[blk, pl.BlockSpec((t, POOL_WIDTH), lambda gi, i: (jnp.minimum(i + 1, nb - 1), gi))],
        out_specs=blk, out_shape=jax.ShapeDtypeStruct(x.shape, F32),
        compiler_params=_params(("parallel", "parallel")))(ddiff, ddiff)
    dx, dgain = _rms_bwd(x, gain, dh, dx1, "mix_norm_bwd")
    return dx, dgain, dw, db, dsc


MESH_ID = pl.DeviceIdType.MESH
ANY = pl.BlockSpec(memory_space=pl.ANY)


def _place():
    x, y, c = lax.axis_index("x"), lax.axis_index("y"), lax.axis_index("c")
    other_chips = [(1 - x, y), (x, 1 - y), (1 - x, 1 - y)]
    return x, y, c, other_chips


def _chip_index(chip):
    return 2 * chip[0] + chip[1]


def _remote(src, dst, send_sems, recv_sems, n, to):
    return pltpu.make_async_remote_copy(src_ref=src, dst_ref=dst, send_sem=send_sems.at[n], recv_sem=recv_sems.at[n],
                                        device_id=to, device_id_type=MESH_ID)


def _gather_weights(ws):
    n = len(ws)
    from_x, relay_x, from_y, relay_y, sib_x, sib_y, sib_d0, sib_d1, sib_own = range(9)
    slots = 9

    def body(*refs):
        w_refs, out_refs = refs[:n], refs[n:2 * n]
        send_sems, recv_sems = refs[2 * n:]
        x, y, c, (xn, yn, dn) = _place()
        k = _chip_index((x, y))
        me, sibling = (x, y, c), (x, y, 1 - c)

        def copy(ref, t, slot, to):
            return _remote(ref, ref, send_sems, recv_sems, slots * t + slot, to)

        def quarter(ref, q):
            rows = ref.shape[0] // 2
            return ref.at[pl.ds(q * rows, rows)]

        started = []

        def start(cp):
            cp.start()
            started.append(cp)

        for t in range(n):
            for slot, chip in ((from_x, xn), (from_y, yn)):
                start(_remote(w_refs[t].at[c], out_refs[t].at[k, c], send_sems, recv_sems, slots * t + slot, (*chip, c)))
            start(_remote(w_refs[t], out_refs[t].at[k], send_sems, recv_sems, slots * t + sib_own, sibling))
        for t in range(n):
            got = out_refs[t].at[_chip_index(xn), c]
            copy(got, t, from_x, me).wait_recv()
            start(copy(quarter(got, 0), t, relay_y, (*yn, c)))
            start(copy(got, t, sib_x, sibling))
        for t in range(n):
            got = out_refs[t].at[_chip_index(yn), c]
            copy(got, t, from_y, me).wait_recv()
            start(copy(quarter(got, 1), t, relay_x, (*xn, c)))
            start(copy(got, t, sib_y, sibling))
        for t in range(n):
            got = out_refs[t].at[_chip_index(dn), c]
            for q, slot, sib_slot in ((0, relay_y, sib_d0), (1, relay_x, sib_d1)):
                copy(quarter(got, q), t, slot, me).wait_recv()
                start(copy(quarter(got, q), t, sib_slot, sibling))
        for t in range(n):
            for chip, slot in ((xn, sib_x), (yn, sib_y)):
                copy(out_refs[t].at[_chip_index(chip), 1 - c], t, slot, me).wait_recv()
            theirs = out_refs[t].at[_chip_index(dn), 1 - c]
            copy(quarter(theirs, 0), t, sib_d0, me).wait_recv()
            copy(quarter(theirs, 1), t, sib_d1, me).wait_recv()
            copy(out_refs[t].at[k], t, sib_own, me).wait_recv()
        for cp in started:
            cp.wait_send()

    return pl.pallas_call(
        body, name="gather_weights", in_specs=[ANY] * n, out_specs=[ANY] * n,
        out_shape=[jax.ShapeDtypeStruct((N_CHIPS,) + w.shape, w.dtype) for w in ws],
        scratch_shapes=[pltpu.SemaphoreType.DMA((slots * n,)), pltpu.SemaphoreType.DMA((slots * n,))],
    )(*ws)


def _swap_halves(gs):
    n = len(gs)

    def body(*refs):
        g_refs, out_refs = refs[:n], refs[n:2 * n]
        send_sems, recv_sems = refs[2 * n:]
        x, y, c, _ = _place()
        copies = [_remote(g_refs[t].at[s, 1 - c], out_refs[t].at[s], send_sems, recv_sems, N_CHIPS * t + s, (x, y, 1 - c))
                  for t in range(n) for s in range(N_CHIPS)]
        for cp in copies:
            cp.start()
        for cp in copies:
            cp.wait_recv()
        for cp in copies:
            cp.wait_send()

    return pl.pallas_call(
        body, name="swap_halves", in_specs=[ANY] * n, out_specs=[ANY] * n,
        out_shape=[jax.ShapeDtypeStruct((N_CHIPS,) + g.shape[2:], g.dtype) for g in gs],
        scratch_shapes=[pltpu.SemaphoreType.DMA((N_CHIPS * n,)), pltpu.SemaphoreType.DMA((N_CHIPS * n,))],
    )(*gs)


def _neighbour_exchange(srcs, out_shapes, name):
    n = len(out_shapes)

    def body(*refs):
        in_refs, out_refs = refs[:n], refs[n:2 * n]
        send_sems, recv_sems = refs[2 * n:]
        x, y, c, chips = _place()
        sends = []
        for t in range(n):
            for slot, (src, chip) in enumerate(zip(srcs(in_refs[t], chips), chips[:2])):
                sends.append(_remote(src, out_refs[t].at[slot], send_sems, recv_sems, 2 * t + slot, (*chip, c)))
        for cp in sends:
            cp.start()
        for t in range(n):
            for slot in range(2):
                landed = out_refs[t].at[slot]
                _remote(landed, landed, send_sems, recv_sems, 2 * t + slot, (x, y, c)).wait_recv()
        for cp in sends:
            cp.wait_send()

    return pl.pallas_call(
        body, name=name, in_specs=[ANY] * n, out_specs=[ANY] * n, out_shape=out_shapes,
        scratch_shapes=[pltpu.SemaphoreType.DMA((2 * n,)), pltpu.SemaphoreType.DMA((2 * n,))])


def _relay_partials(ps):
    def srcs(p_ref, chips):
        half = p_ref.shape[1] // 2
        diagonal = p_ref.at[_chip_index(chips[2])]
        return diagonal.at[pl.ds(0, half)], diagonal.at[pl.ds(half, half)]

    shapes = [jax.ShapeDtypeStruct((2, p.shape[1] // 2, p.shape[2]), p.dtype) for p in ps]
    return _neighbour_exchange(srcs, shapes, "relay_partials")(*ps)


def _merge_relayed(p, relayed, targets, name):
    rows, cols = p.shape[1:]
    tr = _row_tile(rows // 2, SUM_ROWS)
    per = rows // 2 // tr

    def body(x_ref, y_ref, p_ref, r_ref, o_ref):
        to, q = pl.program_id(0), pl.program_id(1)
        extra = jnp.where(q == 1 - to, r_ref[0].astype(F32), 0.0)
        o_ref[0] = (p_ref[0].astype(F32) + extra).astype(o_ref.dtype)

    return pl.pallas_call(
        body, name=name,
        grid_spec=pltpu.PrefetchScalarGridSpec(
            num_scalar_prefetch=2, grid=(2, 2, per),
            in_specs=[pl.BlockSpec((1, tr, cols), lambda to, q, i, xs, ys: (jnp.where(to == 0, xs[0], ys[0]), q * per + i, 0)),
                      pl.BlockSpec((1, tr, cols), lambda to, q, i, xs, ys: (1 - to, i, 0))],
            out_specs=pl.BlockSpec((1, tr, cols), lambda to, q, i, xs, ys: (to, q * per + i, 0))),
        out_shape=jax.ShapeDtypeStruct((2, rows, cols), p.dtype),
        compiler_params=_params(("parallel", "parallel", "parallel")))(targets[0], targets[1], p, relayed)


def _scatter_partials(ms):
    shapes = [jax.ShapeDtypeStruct(m.shape, m.dtype) for m in ms]
    return _neighbour_exchange(lambda m_ref, chips: (m_ref.at[0], m_ref.at[1]), shapes, "scatter_partials")(*ms)


def _share_half(fs):
    n = len(fs)

    def body(*refs):
        f_refs, out_refs = refs[:n], refs[n:2 * n]
        send_sems, recv_sems = refs[2 * n:]
        x, y, c, _ = _place()
        sends = [_remote(f_refs[t], out_refs[t].at[c], send_sems, recv_sems, t, (x, y, 1 - c)) for t in range(n)]
        for cp in sends:
            cp.start()
        for t in range(n):
            theirs = out_refs[t].at[1 - c]
            _remote(theirs, theirs, send_sems, recv_sems, t, (x, y, c)).wait_recv()
        for cp in sends:
            cp.wait_send()

    return pl.pallas_call(
        body, name="share_half", in_specs=[ANY] * n, out_specs=[ANY] * n,
        out_shape=[jax.ShapeDtypeStruct((2,) + f.shape, f.dtype) for f in fs],
        scratch_shapes=[pltpu.SemaphoreType.DMA((n,)), pltpu.SemaphoreType.DMA((n,))],
    )(*fs)


def _gather_small(v):
    def body(v_ref, out_ref, send_sems, recv_sems):
        x, y, c, chips = _place()
        me, sibling = (x, y, c), (x, y, 1 - c)

        def slot(px, py, pc):
            return out_ref.at[4 * px + 2 * py + pc]

        first = [_remote(v_ref, slot(*me), send_sems, recv_sems, 0, sibling)]
        first += [_remote(v_ref, slot(*me), send_sems, recv_sems, 1 + j, (*chip, c)) for j, chip in enumerate(chips)]
        for cp in first:
            cp.start()
        passed = [_remote(slot(*chip, c), slot(*chip, c), send_sems, recv_sems, 4 + j, sibling)
                  for j, chip in enumerate(chips)]
        for j, chip in enumerate(chips):
            _remote(slot(*chip, c), slot(*chip, c), send_sems, recv_sems, 1 + j, me).wait_recv()
            passed[j].start()
        _remote(slot(*sibling), slot(*sibling), send_sems, recv_sems, 0, me).wait_recv()
        for j, chip in enumerate(chips):
            _remote(slot(*chip, 1 - c), slot(*chip, 1 - c), send_sems, recv_sems, 4 + j, me).wait_recv()
        for cp in first + passed:
            cp.wait_send()

    gathered = pl.pallas_call(
        body, name="gather_small", in_specs=[ANY], out_specs=ANY,
        out_shape=jax.ShapeDtypeStruct((N_DEV,) + v.shape, v.dtype),
        scratch_shapes=[pltpu.SemaphoreType.DMA((7,)), pltpu.SemaphoreType.DMA((7,))],
    )(v)
    device = 4 * lax.axis_index("x") + 2 * lax.axis_index("y") + lax.axis_index("c")
    return lax.dynamic_update_index_in_dim(gathered, v, device, 0)


SMALL_ROWS = 256
SUM_ROWS = 256
BF16_ROWS = 16


def _row_tile(rows, want):
    for t in range(min(rows, want) // BF16_ROWS * BF16_ROWS, 0, -BF16_ROWS):
        if rows % t == 0:
            return t
    return rows


def _pair_sum(g, r, core, name):
    rows, cols = g.shape[2:]
    tr = _row_tile(rows, SUM_ROWS)

    def body(c_ref, g_ref, r_ref, o_ref):
        o_ref[0] = (g_ref[0, 0].astype(F32) + r_ref[0].astype(F32)).astype(o_ref.dtype)

    return pl.pallas_call(
        body, name=name,
        grid_spec=pltpu.PrefetchScalarGridSpec(
            num_scalar_prefetch=1, grid=(N_CHIPS, rows // tr),
            in_specs=[pl.BlockSpec((1, 1, tr, cols), lambda s, i, c_ref: (s, c_ref[0], i, 0)),
                      pl.BlockSpec((1, tr, cols), lambda s, i, c_ref: (s, i, 0))],
            out_specs=pl.BlockSpec((1, tr, cols), lambda s, i, c_ref: (s, i, 0))),
        out_shape=jax.ShapeDtypeStruct(r.shape, BF16),
        compiler_params=_params(("parallel", "parallel")))(core, g, r)


def _chip_sum(p, recv, chip, name):
    rows, cols = p.shape[1:]
    tr = _row_tile(rows, SUM_ROWS)
    n_recv = recv.shape[0]

    def body(k_ref, p_ref, r_ref, o_ref):
        acc = p_ref[0].astype(F32)
        for j in range(n_recv):
            acc = acc + r_ref[j].astype(F32)
        o_ref[...] = acc

    return pl.pallas_call(
        body, name=name,
        grid_spec=pltpu.PrefetchScalarGridSpec(
            num_scalar_prefetch=1, grid=(rows // tr,),
            in_specs=[pl.BlockSpec((1, tr, cols), lambda i, k_ref: (k_ref[0], i, 0)),
                      pl.BlockSpec((n_recv, tr, cols), lambda i, k_ref: (0, i, 0))],
            out_specs=pl.BlockSpec((tr, cols), lambda i, k_ref: (i, 0))),
        out_shape=jax.ShapeDtypeStruct((rows, cols), F32),
        compiler_params=_params(("parallel",)))(chip, p, recv)


def _sum_blocks(a, name):
    n, rows, cols = a.shape
    tr = _row_tile(rows, SUM_ROWS)

    def body(a_ref, o_ref):
        acc = a_ref[0].astype(F32)
        for s in range(1, n):
            acc = acc + a_ref[s].astype(F32)
        o_ref[...] = acc

    return pl.pallas_call(
        body, name=name, grid=(rows // tr,),
        in_specs=[pl.BlockSpec((n, tr, cols), lambda i: (0, i, 0))],
        out_specs=pl.BlockSpec((tr, cols), lambda i: (i, 0)),
        out_shape=jax.ShapeDtypeStruct((rows, cols), F32),
        compiler_params=_params(("parallel",)))(a)


def _adamw(w, g, m, v, name):
    def fn(wb, gb, mb, vb):
        m2 = ADAM_B1 * mb + (1.0 - ADAM_B1) * gb
        v2 = ADAM_B2 * vb + (1.0 - ADAM_B2) * (gb * gb)
        m_hat = m2 / (1.0 - ADAM_B1 ** ADAM_STEP)
        v_hat = v2 / (1.0 - ADAM_B2 ** ADAM_STEP)
        delta = -ADAM_LR * (m_hat / (jnp.sqrt(v_hat) + ADAM_EPS) + ADAM_WD * wb)
        return (delta, m2, v2), ()
    c = w.shape[1]
    return _ew(fn, [w, g, m, v], outs=[(c, F32)] * 3, name=name)


WEIGHTS = ["mix_norm_g", "ffn_norm_g", "final_norm_g", "fox_w_in", "fox_b_f", "fox_w_out", "s5_w_in", "s5_a_re",
           "s5_a_im", "s5_log_dt", "s5_b_re", "s5_b_im", "s5_c_re", "s5_c_im", "s5_d", "s5_w_glu", "pool_w",
           "pool_b", "pool_scale", "ffn_w_gate_up", "ffn_w_down"]
BIG = {"fox_w_in": 2, "fox_w_out": 1, "s5_w_in": 1, "s5_w_glu": 2, "pool_w": 2, "ffn_w_gate_up": 2, "ffn_w_down": 1}
BY_CHIP = ("ffn_w_gate_up", "ffn_w_down")
SLICED = ("pool_b", "pool_scale")
SMALL = [n for n in WEIGHTS if n not in BIG]


def _to_natural(cm, axis):
    moved = jnp.moveaxis(cm, 0, axis)
    shape = moved.shape[:axis] + (moved.shape[axis] * moved.shape[axis + 1],) + moved.shape[axis + 2:]
    return moved.reshape(shape)


def _to_chip_major(nat, axis):
    shape = nat.shape[:axis] + (N_CHIPS, nat.shape[axis] // N_CHIPS) + nat.shape[axis + 1:]
    return jnp.moveaxis(nat.reshape(shape), axis, 0)


def _halves_view(shape):
    return (2, int(np.prod(shape[:-1])) // 2, shape[-1])


def _pack_small(parts):
    flat = jnp.concatenate([p.reshape(-1).astype(F32) for p in parts])
    pad = (-flat.shape[0]) % (SMALL_ROWS * LANES)
    return jnp.pad(flat, (0, pad)).reshape(-1, LANES)


def _unpack_small(buf, shapes):
    flat = buf.reshape(-1)
    out, off = [], 0
    for shp in shapes:
        n = int(np.prod(shp))
        out.append(flat[off:off + n].reshape(shp))
        off += n
    return out


def _local_step(x, target, w):
    grads = {}
    mixers = ("fox", "s5", "pool")
    saved = []
    ssm, ssm_pull = jax.vjp(_s5_operands, w["s5_a_re"][0], w["s5_a_im"][0], w["s5_log_dt"][0], w["s5_b_re"][0],
                            w["s5_b_im"][0], w["s5_c_re"][0], w["s5_c_im"][0])
    fox_w = []
    for j in range(w["fox_w_in"].shape[0]):
        w_in = w["fox_w_in"][j]
        w_f = jnp.pad(w_in[:, 3 * D_MODEL:], ((0, 0), (0, LANES - FOX_HEADS)))
        fox_w.append((w_in[:, :3 * D_MODEL], w_f, w["fox_w_out"][j]))
    for i in range(DEPTH):
        kind, j = mixers[i % 3], i // 3
        gain = w["mix_norm_g"][i]
        if kind == "fox":
            x1, sv = _fox_fwd(x, gain, fox_w[j][0], fox_w[j][1], w["fox_b_f"][j], fox_w[j][2])
        elif kind == "s5":
            x1, sv = _s5_fwd(x, gain, w["s5_w_in"][j], ssm, w["s5_d"], w["s5_w_glu"][j])
        else:
            x1, sv = _pool_fwd(x, gain, w["pool_w"][j], w["pool_b"], w["pool_scale"])
        x, sf = _ffn_fwd(x1, w["ffn_norm_g"][i], w["ffn_w_gate_up"], w["ffn_w_down"], i)
        saved.append((sv, sf))
    loss, dx, grads["final_norm_g"] = _loss_head(x, w["final_norm_g"], target)
    per_layer = {n: [None] * DEPTH for n in ("mix_norm_g", "ffn_norm_g")}
    fox_g = {n: [None] * len(fox_w) for n in ("fox_w_in", "fox_b_f", "fox_w_out")}
    for n in BY_CHIP:
        grads[n] = lax.empty(w[n].shape, BF16)
    for i in reversed(range(DEPTH)):
        kind, j = mixers[i % 3], i // 3
        sv, sf = saved[i]
        dx, per_layer["ffn_norm_g"][i], grads["ffn_w_gate_up"], grads["ffn_w_down"] = _ffn_bwd(
            dx, sf, w["ffn_norm_g"][i], w["ffn_w_gate_up"], w["ffn_w_down"], i, grads["ffn_w_gate_up"], grads["ffn_w_down"])
        gain = w["mix_norm_g"][i]
        if kind == "fox":
            dx, per_layer["mix_norm_g"][i], fox_g["fox_w_in"][j], fox_g["fox_b_f"][j], fox_g["fox_w_out"][j] = _fox_bwd(
                dx, sv, gain, fox_w[j][0], fox_w[j][1], fox_w[j][2])
        elif kind == "s5":
            dx, per_layer["mix_norm_g"][i], dw_in, dssm, dd, dw_glu = _s5_bwd(
                dx, sv, gain, w["s5_w_in"][j], ssm, w["s5_d"], w["s5_w_glu"][j])
            grads["s5_w_in"], grads["s5_w_glu"], grads["s5_d"] = dw_in[None], dw_glu[None], dd
            for n, g in zip(("s5_a_re", "s5_a_im", "s5_log_dt", "s5_b_re", "s5_b_im", "s5_c_re", "s5_c_im"), ssm_pull(dssm)):
                grads[n] = g[None]
        else:
            dx, per_layer["mix_norm_g"][i], dw, db, dsc = _pool_bwd(dx, sv, gain, w["pool_w"][j], w["pool_b"], w["pool_scale"])
            grads["pool_w"], grads["pool_b"], grads["pool_scale"] = dw[None].astype(BF16), db, dsc
    for n, parts in {**per_layer, **fox_g}.items():
        grads[n] = jnp.stack(parts)
    return loss, dx, grads


def kernel(x, mix_norm_g, ffn_norm_g, final_norm_g, fox_w_in, fox_b_f, fox_w_out, s5_w_in, s5_a_re, s5_a_im, s5_log_dt, s5_b_re, s5_b_im, s5_c_re, s5_c_im, s5_d, s5_w_glu, pool_w, pool_b, pool_scale, ffn_w_gate_up, ffn_w_down, loss_target, m_mix_norm_g, m_ffn_norm_g, m_final_norm_g, m_fox_w_in, m_fox_b_f, m_fox_w_out, m_s5_w_in, m_s5_a_re, m_s5_a_im, m_s5_log_dt, m_s5_b_re, m_s5_b_im, m_s5_c_re, m_s5_c_im, m_s5_d, m_s5_w_glu, m_pool_w, m_pool_b, m_pool_scale, m_ffn_w_gate_up, m_ffn_w_down, v_mix_norm_g, v_ffn_norm_g, v_final_norm_g, v_fox_w_in, v_fox_b_f, v_fox_w_out, v_s5_w_in, v_s5_a_re, v_s5_a_im, v_s5_log_dt, v_s5_b_re, v_s5_b_im, v_s5_c_re, v_s5_c_im, v_s5_d, v_s5_w_glu, v_pool_w, v_pool_b, v_pool_scale, v_ffn_w_gate_up, v_ffn_w_down):
    given = dict(locals())
    shard = {n: given[n] for n in WEIGHTS}
    chip = 2 * lax.axis_index("x") + lax.axis_index("y")
    core = lax.axis_index("c")

    views = {n: _halves_view(shard[n].shape) for n in BIG}
    own = [shard[n].astype(MXU_DTYPE).reshape(views[n]) for n in BIG]
    whole = {}
    for n, by_chip in zip(BIG, _gather_weights(own)):
        by_chip = by_chip.reshape((N_CHIPS,) + shard[n].shape)
        whole[n] = by_chip if n in BY_CHIP else _to_natural(by_chip, BIG[n])
    for n in SMALL:
        whole[n] = shard[n]
    sliced_shapes = [shard[n].shape for n in SLICED]
    by_chip = _gather_small(_pack_small([shard[n] for n in SLICED]))[0::2]
    slices = [_unpack_small(by_chip[k], sliced_shapes) for k in range(N_CHIPS)]
    for idx, n in enumerate(SLICED):
        whole[n] = jnp.concatenate([slices[k][idx] for k in range(N_CHIPS)], axis=-1)

    loss_part, dx, grads = _local_step(x[0], loss_target[0], whole)
    loss = lax.psum(loss_part, MESH_AXES)

    gs = [(grads[n] if n in BY_CHIP else _to_chip_major(grads[n].astype(BF16), BIG[n])).reshape((N_CHIPS,) + views[n])
          for n in BIG]
    core_id, chip_id = core.reshape(1).astype(jnp.int32), chip.reshape(1).astype(jnp.int32)
    partial = [_pair_sum(g, r, core_id, "pair_sum_" + n) for n, g, r in zip(BIG, gs, _swap_halves(gs))]
    x_chip, y_chip = 2 * (1 - lax.axis_index("x")) + lax.axis_index("y"), 2 * lax.axis_index("x") + 1 - lax.axis_index("y")
    neighbours = (x_chip.reshape(1).astype(jnp.int32), y_chip.reshape(1).astype(jnp.int32))
    merged = [_merge_relayed(p, r, neighbours, "merge_relayed_" + n) for n, p, r in zip(BIG, partial, _relay_partials(partial))]
    half = [_chip_sum(p, r, chip_id, "chip_sum_" + n) for n, p, r in zip(BIG, partial, _scatter_partials(merged))]
    grad = {n: lax.dynamic_update_index_in_dim(both, mine, core, 0).reshape(shard[n].shape)
            for n, mine, both in zip(BIG, half, _share_half(half))}

    small_sum = _sum_blocks(_gather_small(_pack_small([grads[n] for n in SMALL])), "small_sum")
    for n, g in zip(SMALL, _unpack_small(small_sum, [whole[n].shape for n in SMALL])):
        grad[n] = g
    for n in SLICED:
        width = shard[n].shape[-1]
        grad[n] = lax.dynamic_slice_in_dim(grad[n], chip * width, width, axis=-1)

    delta, new_m, new_v = {}, {}, {}
    for n in BIG:
        view = (-1, shard[n].shape[-1])
        res = _adamw(shard[n].reshape(view), grad[n].reshape(view), given["m_" + n].reshape(view),
                     given["v_" + n].reshape(view), "adamw_" + n)
        delta[n], new_m[n], new_v[n] = (r.reshape(shard[n].shape) for r in res)
    small_shapes = [shard[n].shape for n in SMALL]
    res = _adamw(_pack_small([shard[n] for n in SMALL]), _pack_small([grad[n] for n in SMALL]),
                 _pack_small([given["m_" + n] for n in SMALL]), _pack_small([given["v_" + n] for n in SMALL]), "adamw_small")
    for out, buf in zip((delta, new_m, new_v), res):
        for n, a in zip(SMALL, _unpack_small(buf, small_shapes)):
            out[n] = a
    return (loss, dx[None], *[grad[n] for n in WEIGHTS], *[delta[n] for n in WEIGHTS],
            *[new_m[n] for n in WEIGHTS], *[new_v[n] for n in WEIGHTS])
```

```python
import functools
import math

import jax
import jax.numpy as jnp
import numpy as np
from jax import lax
from jax.experimental import pallas as pl
from jax.experimental.pallas import tpu as pltpu

F32 = jnp.float32
BF16 = jnp.bfloat16
MXU_DTYPE = jnp.bfloat16

D_MODEL = 1024
DEPTH = 4
EPS = 1e-6
FOX_HEADS = 16
FOX_HEAD_DIM = 64
HEAD_PAIRS = FOX_HEADS // 2
S5_GROUPS = 64
S5_GROUP = 16
S5_STATE = 64
S5_BLOCKS = 8
S5_HALF = 256
POOL_WINDOWS = (2, 4, 8, 16)
POOL_WIDTH = 256
D_FF = 2816
N_CHIPS = 4
N_DEV = 8
LANES = 128
SUBLANES = 8
VMEM_LIMIT = 56 * 1024 * 1024

ADAM_LR = 0.001
ADAM_B1 = 0.9
ADAM_B2 = 0.999
ADAM_EPS = 1e-08
ADAM_WD = 0.01
ADAM_STEP = 10

MESH_AXES = ("x", "y", "c")


def _tile(n, want):
    t = (min(n, want) // LANES) * LANES
    while t >= LANES:
        if n % t == 0:
            return t
        t -= LANES
    return n


def _params(sem=None):
    return pltpu.CompilerParams(dimension_semantics=sem, vmem_limit_bytes=VMEM_LIMIT)


def _mm(a, b, *, name, ta=False, tb=False, out_dtype=F32, add=None, tm=1024, tn=1024, tk=1024,
        b_tiles=None, out_tiles=None, into=None):
    m, k = (a.shape[1], a.shape[0]) if ta else a.shape
    if b_tiles is None:
        n = b.shape[0] if tb else b.shape[1]
        assert (b.shape[1] if tb else b.shape[0]) == k, (a.shape, b.shape, ta, tb)
    else:
        assert b_tiles[0] == k, (a.shape, b_tiles[0])
        n = b_tiles[1]
    tm, tn, tk = _tile(m, tm), _tile(n, tn), _tile(k, tk)
    nk = k // tk
    a_spec = pl.BlockSpec((tk, tm), lambda i, j, kk: (kk, i)) if ta else pl.BlockSpec((tm, tk), lambda i, j, kk: (i, kk))
    b_shape = (tn, tk) if tb else (tk, tn)
    if b_tiles is None:
        b_spec = pl.BlockSpec(b_shape, (lambda i, j, kk: (j, kk)) if tb else (lambda i, j, kk: (kk, j)))
    else:
        b_spec = pl.BlockSpec(b_tiles[2], b_tiles[3])
    add_spec = pl.BlockSpec((tm, tn), lambda i, j, kk: (i, j))
    if out_tiles is None:
        o_spec, o_struct = add_spec, jax.ShapeDtypeStruct((m, n), out_dtype)
    else:
        o_spec, o_struct = pl.BlockSpec(out_tiles[1], out_tiles[2]), jax.ShapeDtypeStruct(out_tiles[0], out_dtype)
    dims = (((0 if ta else 1,), (1 if tb else 0,)), ((), ()))
    has_add, has_into = add is not None, into is not None

    def body(*refs):
        a_ref, b_ref = refs[:2]
        add_ref = refs[2] if has_add else None
        o_ref, acc_ref = refs[-2:]
        kk = pl.program_id(2)

        @pl.when(kk == 0)
        def _():
            acc_ref[...] = jnp.zeros_like(acc_ref)

        acc_ref[...] += lax.dot_general(a_ref[...].astype(MXU_DTYPE), b_ref[...].reshape(b_shape).astype(MXU_DTYPE), dims,
                                        preferred_element_type=F32)

        @pl.when(kk == nk - 1)
        def _():
            r = acc_ref[...]
            if has_add:
                r = r + add_ref[...].astype(F32)
            o_ref[...] = r.astype(out_dtype).reshape(o_ref.shape)

    ins = [a, b] + ([add] if has_add else []) + ([into] if has_into else [])
    specs = [a_spec, b_spec] + ([add_spec] if has_add else []) + ([pl.BlockSpec(memory_space=pl.ANY)] if has_into else [])
    return pl.pallas_call(
        body, name=name, grid=(m // tm, n // tn, nk), in_specs=specs, out_specs=o_spec,
        out_shape=o_struct, scratch_shapes=[pltpu.VMEM((tm, tn), F32)],
        input_output_aliases={len(ins) - 1: 0} if has_into else {},
        compiler_params=_params(("parallel", "parallel", "arbitrary")))(*ins)


def _ew(fn, tens, vecs=(), *, outs=(), sums=(), name, tr=256):
    tens = [t if isinstance(t, tuple) else (t, t.shape[1], 0) for t in tens]
    rows = tens[0][0].shape[0]
    tr = min(tr, rows)
    n_t, n_v, n_o, n_s = len(tens), len(vecs), len(outs), len(sums)

    def body(*refs):
        i = pl.program_id(0)
        t_blocks = [r[...] for r in refs[:n_t]]
        v_blocks = [r[...] for r in refs[n_t:n_t + n_v]]
        o_refs = refs[n_t + n_v:n_t + n_v + n_o]
        s_refs = refs[n_t + n_v + n_o:]
        o_vals, s_vals = fn(*t_blocks, *v_blocks)
        for r, v in zip(o_refs, o_vals):
            r[...] = v.astype(r.dtype)
        if n_s:
            @pl.when(i == 0)
            def _():
                for r in s_refs:
                    r[...] = jnp.zeros_like(r)
            for r, v in zip(s_refs, s_vals):
                r[...] += jnp.sum(v.astype(F32), axis=0, keepdims=True)

    in_specs = [pl.BlockSpec((tr, w), functools.partial(lambda i, cb: (i, cb), cb=cb)) for _, w, cb in tens]
    in_specs += [pl.BlockSpec(v.shape, functools.partial(lambda i, nd: (0,) * nd, nd=v.ndim)) for v in vecs]
    out_specs = [pl.BlockSpec((tr, c), lambda i: (i, 0)) for c, _ in outs]
    out_specs += [pl.BlockSpec((1, c), lambda i: (0, 0)) for c in sums]
    out_shape = [jax.ShapeDtypeStruct((rows, c), dt) for c, dt in outs]
    out_shape += [jax.ShapeDtypeStruct((1, c), F32) for c in sums]
    res = pl.pallas_call(
        body, name=name, grid=(rows // tr,), in_specs=in_specs, out_specs=out_specs, out_shape=out_shape,
        compiler_params=_params(("arbitrary",)))(*[t[0] for t in tens], *vecs)
    return res


def _sigmoid(z):
    return 1.0 / (1.0 + jnp.exp(-z))


def _rms_fwd(x, g, name):
    def fn(xb, gb):
        r = lax.rsqrt(jnp.mean(xb * xb, axis=-1, keepdims=True) + EPS)
        return ((xb * r) * gb,), ()
    return _ew(fn, [x], [g.reshape(1, -1)], outs=[(x.shape[1], BF16)], name=name)[0]


def _rms_bwd(x, g, dh, dres, name):
    def fn(xb, dhb, drb, gb):
        r = lax.rsqrt(jnp.mean(xb * xb, axis=-1, keepdims=True) + EPS)
        xh = xb * r
        dhf = dhb.astype(F32)
        dy = dhf * gb
        dx = r * (dy - xh * jnp.mean(dy * xh, axis=-1, keepdims=True))
        return (drb + dx,), (dhf * xh,)
    dx, dg = _ew(fn, [x, dh, dres], [g.reshape(1, -1)], outs=[(x.shape[1], F32)], sums=[x.shape[1]], name=name)
    return dx, dg[0]


def _mm_rms_bwd(a, b, x, g, dres, *, name, add=None, tm=512, tk=1024, b_tiles=None):
    m, k = a.shape
    n = x.shape[1]
    assert x.shape == (m, n) and (b_tiles is not None or b.shape == (n, k))
    tm, tk = _tile(m, tm), _tile(k, tk)
    nk = k // tk
    has_add = add is not None

    def body(*refs):
        a_ref, b_ref, x_ref, dr_ref, g_ref = refs[:5]
        add_ref = refs[5] if has_add else None
        dx_ref, dg_ref, acc_ref = refs[-3:]
        i, kk = pl.program_id(0), pl.program_id(1)

        @pl.when(kk == 0)
        def _():
            acc_ref[...] = jnp.zeros_like(acc_ref)

        @pl.when((kk == 0) & (i == 0))
        def _():
            dg_ref[...] = jnp.zeros_like(dg_ref)

        acc_ref[...] += lax.dot_general(a_ref[...].astype(MXU_DTYPE), b_ref[...].reshape((n, tk)).astype(MXU_DTYPE),
                                        (((1,), (1,)), ((), ())), preferred_element_type=F32)

        @pl.when(kk == nk - 1)
        def _():
            dh = acc_ref[...]
            if has_add:
                dh = dh + add_ref[...]
            xb = x_ref[...]
            r = lax.rsqrt(jnp.mean(xb * xb, axis=-1, keepdims=True) + EPS)
            xh = xb * r
            dy = dh * g_ref[...]
            dx_ref[...] = dr_ref[...] + r * (dy - xh * jnp.mean(dy * xh, axis=-1, keepdims=True))
            dg_ref[...] += jnp.sum(dh * xh, axis=0, keepdims=True)

    row = pl.BlockSpec((tm, n), lambda i, kk: (i, 0))
    vec = pl.BlockSpec((1, n), lambda i, kk: (0, 0))
    ins = [a, b, x, dres, g.reshape(1, n)] + ([add] if has_add else [])
    b_spec = pl.BlockSpec((n, tk), lambda i, kk: (0, kk)) if b_tiles is None else pl.BlockSpec(*b_tiles)
    specs = [pl.BlockSpec((tm, tk), lambda i, kk: (i, kk)), b_spec, row, row, vec]
    specs += [row] if has_add else []
    dx, dg = pl.pallas_call(
        body, name=name, grid=(m // tm, nk), in_specs=specs, out_specs=[row, vec],
        out_shape=[jax.ShapeDtypeStruct((m, n), F32), jax.ShapeDtypeStruct((1, n), F32)],
        scratch_shapes=[pltpu.VMEM((tm, n), F32)],
        compiler_params=_params(("arbitrary", "arbitrary")))(*ins)
    return dx, dg[0]


FFN_ROWS = 512
FFN_COLS = D_FF // 2


def _ffn_gate_up(x, gain, w_gu, layer):
    s, d = x.shape
    tm = min(FFN_ROWS, s)
    halves = D_FF // FFN_COLS

    def body(x_ref, gain_ref, wg_ref, wu_ref, h_ref, g_ref, u_ref, a_ref):
        @pl.when(pl.program_id(1) == 0)
        def _():
            xb = x_ref[...]
            h_ref[...] = ((xb * lax.rsqrt(jnp.mean(xb * xb, axis=-1, keepdims=True) + EPS)) * gain_ref[...]).astype(h_ref.dtype)

        hb = h_ref[...].astype(MXU_DTYPE)
        g = jnp.dot(hb, wg_ref[0, 0], preferred_element_type=F32)
        u = jnp.dot(hb, wu_ref[0, 0], preferred_element_type=F32)
        g_ref[...] = g.astype(g_ref.dtype)
        u_ref[...] = u.astype(u_ref.dtype)
        a_ref[...] = (g * _sigmoid(g) * u).astype(a_ref.dtype)

    row = pl.BlockSpec((tm, d), lambda i, jj: (i, 0))
    tile = pl.BlockSpec((tm, FFN_COLS), lambda i, jj: (i, jj))
    return pl.pallas_call(
        body, name="ffn_gate_up", grid=(s // tm, halves),
        in_specs=[row, pl.BlockSpec((1, d), lambda i, jj: (0, 0)),
                  pl.BlockSpec((1, 1, d, FFN_COLS), lambda i, jj: (jj, layer, 0, 0)),
                  pl.BlockSpec((1, 1, d, FFN_COLS), lambda i, jj: (halves + jj, layer, 0, 0))],
        out_specs=[row, tile, tile, tile],
        out_shape=[jax.ShapeDtypeStruct((s, d), BF16)] + [jax.ShapeDtypeStruct((s, D_FF), BF16)] * 3,
        compiler_params=_params(("parallel", "arbitrary")))(x, gain.reshape(1, d), w_gu, w_gu)


def _ffn_dgate_up(dx2, w_down, layer, g, u):
    s, d = dx2.shape
    tm = min(FFN_ROWS, s)
    halves = D_FF // FFN_COLS

    def body(dx_ref, w_ref, g_ref, u_ref, o_ref):
        jj = pl.program_id(1)
        df = lax.dot_general(dx_ref[...].astype(MXU_DTYPE), w_ref[...].reshape((FFN_COLS, d)), (((1,), (1,)), ((), ())),
                             preferred_element_type=F32)
        gf, uf = g_ref[...].astype(F32), u_ref[...].astype(F32)
        sg = _sigmoid(gf)
        dg = df * uf * (sg * (1.0 + gf * (1.0 - sg)))
        du = df * (gf * sg)
        o_ref[:, pl.ds(pl.multiple_of(jj * FFN_COLS, LANES), FFN_COLS)] = dg.astype(o_ref.dtype)
        o_ref[:, pl.ds(pl.multiple_of(D_FF + jj * FFN_COLS, LANES), FFN_COLS)] = du.astype(o_ref.dtype)

    tile = pl.BlockSpec((tm, FFN_COLS), lambda i, jj: (i, jj))
    return pl.pallas_call(
        body, name="ffn_dgate_up", grid=(s // tm, halves),
        in_specs=[pl.BlockSpec((tm, d), lambda i, jj: (i, 0)),
                  pl.BlockSpec((2, 1, FFN_COLS // 2, d), lambda i, jj: (jj, layer, 0, 0)), tile, tile],
        out_specs=pl.BlockSpec((tm, 2 * D_FF), lambda i, jj: (i, 0)),
        out_shape=jax.ShapeDtypeStruct((s, 2 * D_FF), BF16),
        compiler_params=_params(("parallel", "arbitrary")))(dx2, w_down, g, u)


def _ffn_fwd(x1, gain, w_gu, w_down, layer):
    d = x1.shape[1]
    h, g, u, act = _ffn_gate_up(x1, gain, w_gu, layer)
    x2 = _mm(act, w_down, name="ffn_down", add=x1, tk=FFN_COLS,
             b_tiles=(D_FF, d, (2, 1, FFN_COLS // 2, d), lambda i, j, kk: (kk, layer, 0, 0)))
    return x2, (x1, h, g, u, act)


def _ffn_bwd(dx2, saved, gain, w_gu, w_down, layer, dw_gu, dw_down):
    x1, h, g, u, act = saved
    d = x1.shape[1]
    dw_down = _mm(act, dx2, ta=True, name="ffn_dw_down", out_dtype=BF16, tm=FFN_COLS, into=dw_down,
                  out_tiles=(dw_down.shape, (2, 1, FFN_COLS // 2, d), lambda i, j, kk: (i, layer, 0, 0)))
    dgu = _ffn_dgate_up(dx2, w_down, layer, g, u)
    dw_gu = _mm(h, dgu, ta=True, name="ffn_dw_gu", out_dtype=BF16, tn=FFN_COLS, into=dw_gu,
                out_tiles=(dw_gu.shape, (1, 1, d, FFN_COLS), lambda i, j, kk: (j, layer, i, 0)))
    dx1, dgain = _mm_rms_bwd(dgu, w_gu, x1, gain, dx2, name="ffn_dh", tm=1024, tk=FFN_COLS,
                             b_tiles=((1, 1, d, FFN_COLS), lambda i, kk: (kk, layer, 0, 0)))
    return dx1, dgain, dw_gu, dw_down


def _loss_head(x, gain, target):
    d = x.shape[1]

    def fn(xb, tb, gb):
        r = lax.rsqrt(jnp.mean(xb * xb, axis=-1, keepdims=True) + EPS)
        xh = xb * r
        y = xh * gb
        err = y - tb
        dyv = err * (1.0 / d)
        dyg = dyv * gb
        dx = r * (dyg - xh * jnp.mean(dyg * xh, axis=-1, keepdims=True))
        return (dx,), (0.5 * err * err * (1.0 / d), dyv * xh)
    dx, lsum, dg = _ew(fn, [x, target], [gain.reshape(1, -1)], outs=[(d, F32)], sums=[d, d], name="loss_head")
    return jnp.sum(lsum), dx, dg[0]


ATT_BLOCK = 256
CUM_BLOCK = 512
NEG_INF = -1e30


def _fox_gate_fwd(fl_row, b_col):
    nh, s = fl_row.shape
    tb = min(CUM_BLOCK, s)

    def body(fl_ref, b_ref, z_ref, c_ref):
        upper = (lax.broadcasted_iota(jnp.int32, (tb, tb), 0) <= lax.broadcasted_iota(jnp.int32, (tb, tb), 1)).astype(F32)
        carry = jnp.zeros((nh, 1), F32)
        for blk in range(s // tb):
            z = fl_ref[:, blk * tb:(blk + 1) * tb] + b_ref[...]
            logf = jnp.minimum(z, 0.0) - jnp.log(1.0 + jnp.exp(-jnp.abs(z)))
            cs = jnp.dot(logf, upper, precision=lax.Precision.HIGHEST, preferred_element_type=F32) + carry
            z_ref[:, blk * tb:(blk + 1) * tb] = z
            c_ref[:, blk * tb:(blk + 1) * tb] = cs
            carry = cs[:, tb - 1:tb]

    return pl.pallas_call(body, name="fox_gate_fwd", out_shape=[jax.ShapeDtypeStruct((nh, s), F32)] * 2,
                          compiler_params=_params())(fl_row, b_col)


def _fox_gate_bwd(dc_row, z_row):
    nh, s = dc_row.shape
    tb = min(CUM_BLOCK, s)

    def body(dc_ref, z_ref, dz_ref, db_ref):
        lower = (lax.broadcasted_iota(jnp.int32, (tb, tb), 0) >= lax.broadcasted_iota(jnp.int32, (tb, tb), 1)).astype(F32)
        carry = jnp.zeros((nh, 1), F32)
        db = jnp.zeros((nh, 1), F32)
        for blk in reversed(range(s // tb)):
            dc = dc_ref[:, blk * tb:(blk + 1) * tb]
            rs = jnp.dot(dc, lower, precision=lax.Precision.HIGHEST, preferred_element_type=F32) + carry
            dz = rs * _sigmoid(-z_ref[:, blk * tb:(blk + 1) * tb])
            dz_ref[:, blk * tb:(blk + 1) * tb] = dz
            db = db + jnp.sum(dz, axis=1, keepdims=True)
            carry = rs[:, 0:1]
        db_ref[...] = db

    return pl.pallas_call(body, name="fox_gate_bwd",
                          out_shape=[jax.ShapeDtypeStruct((nh, s), F32), jax.ShapeDtypeStruct((nh, 1), F32)],
                          compiler_params=_params())(dc_row, z_row)


def _head_masks(rows):
    lane = lax.broadcasted_iota(jnp.int32, (rows, LANES), 1)
    return lane < FOX_HEAD_DIM


ATT_QUERIES = 512


def _lanes(a, width):
    return jnp.concatenate([a] * (width // LANES), axis=1)


def _attn_fwd_t(qkv, c_lanes):
    s = qkv.shape[0]
    t, tq = min(ATT_BLOCK, s), min(2 * ATT_QUERIES, s)
    nq, per = s // tq, tq // t
    scale = FOX_HEAD_DIM ** -0.5
    np_ = HEAD_PAIRS
    nt = (((1,), (1,)), ((), ()))

    def body(q_ref, k_ref, v_ref, c_ref, o_ref, lse_ref, vt_ref):
        i = pl.program_id(1)

        @pl.when(i == 0)
        def _():
            for r in range(s // t):
                vt_ref[:, r * t:(r + 1) * t] = v_ref[r * t:(r + 1) * t, :].astype(F32).T.astype(vt_ref.dtype)

        first = _head_masks(tq)
        upper = lax.broadcasted_iota(jnp.int32, (LANES, tq), 0) < FOX_HEAD_DIM
        qs = q_ref[...] * scale
        zero = jnp.zeros_like(qs)
        qh = (jnp.where(first, qs, zero), jnp.where(first, zero, qs))
        def block(j, carry, diagonal=None):
            skip = 0 if diagonal is None else diagonal * t
            wide = tq - skip

            def join(old, new):
                return new if skip == 0 else jnp.concatenate([old[:, :skip], new], axis=1)

            ms, ls, acc = carry
            start = pl.multiple_of(j * t, t)
            kb, vt = k_ref[pl.ds(start, t), :], vt_ref[:, pl.ds(start, t)]
            new_m, new_l, alphas, pvs = [], [], [], []
            for hh in range(2):
                sc = lax.dot_general(kb, qh[hh][skip:], nt, preferred_element_type=F32) - _lanes(c_ref[0, hh, pl.ds(start, t), :], wide)
                if diagonal is not None:
                    visible = lax.broadcasted_iota(jnp.int32, (t, wide), 0) <= lax.broadcasted_iota(jnp.int32, (t, wide), 1)
                    sc = jnp.where(visible, sc, NEG_INF)
                m_old, l_old = ms[hh][:, skip:], ls[hh][:, skip:]
                m_new = jnp.maximum(m_old, jnp.max(sc, axis=0, keepdims=True))
                p = jnp.exp(sc - m_new)
                alpha = jnp.exp(m_old - m_new)
                new_m.append(join(ms[hh], m_new))
                new_l.append(join(ls[hh], alpha * l_old + jnp.sum(p, axis=0, keepdims=True)))
                alphas.append(alpha)
                p_hi = p.astype(MXU_DTYPE)
                p_lo = (p - p_hi.astype(F32)).astype(MXU_DTYPE)
                pvs.append(jnp.dot(vt, p_hi, preferred_element_type=F32) + jnp.dot(vt, p_lo, preferred_element_type=F32))
            seen = jnp.where(upper[:, skip:], alphas[0], alphas[1]) * acc[:, skip:] + jnp.where(upper[:, skip:], pvs[0], pvs[1])
            return tuple(new_m), tuple(new_l), join(acc, seen)

        neg, nil = jnp.full((1, tq), NEG_INF, F32), jnp.zeros((1, tq), F32)
        carry = lax.fori_loop(0, per * i, block, ((neg, neg), (nil, nil), jnp.zeros((LANES, tq), F32)))
        for d in range(per):
            carry = block(per * i + d, carry, diagonal=d)
        ms, ls, acc = carry
        o_ref[...] = (acc / jnp.where(upper, ls[0], ls[1])).T.astype(o_ref.dtype)
        lse_ref[0] = jnp.concatenate([ms[0] + jnp.log(ls[0]), ms[1] + jnp.log(ls[1])], axis=0)

    return pl.pallas_call(
        body, name="fox_attn_fwd", grid=(np_, nq),
        in_specs=[pl.BlockSpec((tq, LANES), lambda p, i: (i, p)),
                  pl.BlockSpec((s, LANES), lambda p, i: (0, np_ + p)),
                  pl.BlockSpec((s, LANES), lambda p, i: (0, 2 * np_ + p)),
                  pl.BlockSpec((1, 2, s, LANES), lambda p, i: (p, 0, 0, 0))],
        out_specs=[pl.BlockSpec((tq, LANES), lambda p, i: (i, p)),
                   pl.BlockSpec((1, 2, tq), lambda p, i: (p, 0, i))],
        out_shape=[jax.ShapeDtypeStruct((s, D_MODEL), F32), jax.ShapeDtypeStruct((np_, 2, s), F32)],
        scratch_shapes=[pltpu.VMEM((LANES, s), MXU_DTYPE)],
        compiler_params=_params(("parallel", "arbitrary")))(qkv, qkv, qkv, c_lanes)


def _attn_bwd_t(qkv, do, lse, delta, c_lanes):
    s = qkv.shape[0]
    t, tq = min(ATT_BLOCK, s), min(ATT_QUERIES, s)
    nb, nq, per = s // t, s // tq, tq // t
    scale = FOX_HEAD_DIM ** -0.5
    np_ = HEAD_PAIRS
    nt = (((1,), (1,)), ((), ()))

    def body(q_ref, k_ref, v_ref, do_ref, lse_ref, dl_ref, c_ref, dq_ref, dk_ref, dv_ref, dc_ref, dqt_ref):
        j = pl.program_id(1)
        first_q, first = _head_masks(tq), _head_masks(t)
        upper = lax.broadcasted_iota(jnp.int32, (LANES, tq), 0) < FOX_HEAD_DIM
        causal = (lax.broadcasted_iota(jnp.int32, (t, tq), 0) + (j % per) * t) <= lax.broadcasted_iota(jnp.int32, (t, tq), 1)
        kb, vb = k_ref[...], v_ref[...]
        kt = kb.astype(F32).T.astype(MXU_DTYPE)
        cb = (_lanes(c_ref[0, 0], tq), _lanes(c_ref[0, 1], tq))

        @pl.when(j == 0)
        def _():
            dqt_ref[...] = jnp.zeros_like(dqt_ref)

        def step(i, carry, masked):
            dk_acc, dv_acc, dc_accs = carry
            start = pl.multiple_of(i * tq, tq)
            qs = q_ref[pl.ds(start, tq), :] * scale
            dob = do_ref[pl.ds(start, tq), :]
            zero = jnp.zeros_like(qs)
            dks, dvs, dqs, dcs = [], [], [], []
            for hh in range(2):
                qh = jnp.where(first_q, qs, zero) if hh == 0 else jnp.where(first_q, zero, qs)
                doh = jnp.where(first_q, dob, zero) if hh == 0 else jnp.where(first_q, zero, dob)
                sc = lax.dot_general(kb, qh, nt, preferred_element_type=F32)
                p = jnp.exp(sc - cb[hh] - lse_ref[0, hh:hh + 1, pl.ds(start, tq)])
                if masked:
                    p = jnp.where(causal, p, 0.0)
                dp = lax.dot_general(vb, doh, nt, preferred_element_type=F32)
                ds = p * (dp - dl_ref[0, hh:hh + 1, pl.ds(start, tq)])
                pb, dsb = p.astype(MXU_DTYPE), ds.astype(MXU_DTYPE)
                dvs.append(jnp.dot(pb, dob, preferred_element_type=F32))
                dks.append(jnp.dot(dsb, qs, preferred_element_type=F32))
                dqs.append(jnp.dot(kt, dsb, preferred_element_type=F32))
                dcs.append(dc_accs[hh] - jnp.sum(ds, axis=1, keepdims=True))
            dqt_ref[:, pl.ds(start, tq)] += jnp.where(upper, dqs[0], dqs[1]) * scale
            return (dk_acc + jnp.where(first, dks[0], dks[1]), dv_acc + jnp.where(first, dvs[0], dvs[1]), tuple(dcs))

        nil, col = jnp.zeros((t, LANES), F32), jnp.zeros((t, 1), F32)
        carry = step(j // per, (nil, nil, (col, col)), True)
        dk_acc, dv_acc, dc_accs = lax.fori_loop(j // per + 1, nq, functools.partial(step, masked=False), carry)
        dk_ref[...] = dk_acc.astype(dk_ref.dtype)
        dv_ref[...] = dv_acc.astype(dv_ref.dtype)
        dc_ref[0, 0] = jnp.broadcast_to(dc_accs[0], (t, LANES))
        dc_ref[0, 1] = jnp.broadcast_to(dc_accs[1], (t, LANES))

        @pl.when(j == nb - 1)
        def _():
            for r in range(nb):
                dq_ref[r * t:(r + 1) * t, :] = dqt_ref[:, r * t:(r + 1) * t].T

    row = pl.BlockSpec((1, 2, s), lambda p, j: (p, 0, 0))
    return pl.pallas_call(
        body, name="fox_attn_bwd", grid=(np_, nb),
        in_specs=[pl.BlockSpec((s, LANES), lambda p, j: (0, p)),
                  pl.BlockSpec((t, LANES), lambda p, j: (j, np_ + p)),
                  pl.BlockSpec((t, LANES), lambda p, j: (j, 2 * np_ + p)),
                  pl.BlockSpec((s, LANES), lambda p, j: (0, p)), row, row,
                  pl.BlockSpec((1, 2, t, LANES), lambda p, j: (p, 0, j, 0))],
        out_specs=[pl.BlockSpec((s, LANES), lambda p, j: (0, p)),
                   pl.BlockSpec((t, LANES), lambda p, j: (j, p)),
                   pl.BlockSpec((t, LANES), lambda p, j: (j, p)),
                   pl.BlockSpec((1, 2, t, LANES), lambda p, j: (p, 0, j, 0))],
        out_shape=[jax.ShapeDtypeStruct((s, D_MODEL), F32), jax.ShapeDtypeStruct((s, D_MODEL), BF16),
                   jax.ShapeDtypeStruct((s, D_MODEL), BF16), jax.ShapeDtypeStruct((np_, 2, s, LANES), F32)],
        scratch_shapes=[pltpu.VMEM((LANES, s), F32)],
        compiler_params=_params(("parallel", "arbitrary")))(qkv, qkv, qkv, do, lse, delta, c_lanes)


def _head_sums(a, b, name):
    d = a.shape[1]
    sel = (jnp.arange(d)[:, None] // FOX_HEAD_DIM == jnp.arange(LANES)[None, :]).astype(F32)

    def fn(ab, bb, selb):
        prod = ab.astype(F32) * bb.astype(F32)
        return (jnp.dot(prod, selb, precision=lax.Precision.HIGHEST, preferred_element_type=F32),), ()
    return _ew(fn, [a, b], [sel], outs=[(LANES, F32)], name=name)[0]


def _fox_fwd(x, gain, w_qkv, w_f, b_f, w_out):
    s = x.shape[0]
    h = _rms_fwd(x, gain, "mix_norm")
    qkv = _mm(h, w_qkv, name="fox_qkv", out_dtype=BF16)
    fl = _mm(h, w_f, name="fox_f", tn=LANES)
    z_row, c_rowf = _fox_gate_fwd(fl[:, :FOX_HEADS].T, b_f.reshape(FOX_HEADS, 1))
    c_lanes = jnp.broadcast_to(c_rowf.reshape(HEAD_PAIRS, 2, s, 1), (HEAD_PAIRS, 2, s, LANES))
    o, lse = _attn_fwd_t(qkv, c_lanes)
    x1 = _mm(o, w_out, name="fox_out", add=x)
    return x1, (x, h, qkv, z_row, c_lanes, o, lse)


def _fox_bwd(dx1, saved, gain, w_qkv, w_f, w_out):
    x, h, qkv, z_row, c_lanes, o, lse = saved
    s = x.shape[0]
    do = _mm(dx1, w_out, tb=True, name="fox_do", out_dtype=BF16)
    dw_out = _mm(o, dx1, ta=True, name="fox_dw_out", out_dtype=BF16)
    delta = _head_sums(do, o, "fox_delta")[:, :FOX_HEADS].T.reshape(HEAD_PAIRS, 2, s)
    dq, dk, dv, dc = _attn_bwd_t(qkv, do, lse, delta, c_lanes)
    dz_row, db = _fox_gate_bwd(dc[..., 0].reshape(FOX_HEADS, s), z_row)
    dqkv = jnp.concatenate([dq.astype(BF16), dk, dv], axis=1)
    dfl = jnp.pad(dz_row.T, ((0, 0), (0, LANES - FOX_HEADS))).astype(BF16)
    dw_qkv = _mm(h, dqkv, ta=True, name="fox_dw_qkv", out_dtype=BF16)
    dw_f = _mm(h, dfl, ta=True, name="fox_dw_f", out_dtype=BF16, tn=LANES)
    dh = _mm(dqkv, w_qkv, tb=True, name="fox_dh_qkv")
    dx, dgain = _mm_rms_bwd(dfl, w_f, x, gain, dx1, name="fox_dh_f", add=dh)
    dw_in = jnp.concatenate([dw_qkv, dw_f[:, :FOX_HEADS]], axis=1)
    return dx, dgain, dw_in, db.reshape(FOX_HEADS), dw_out


S5_ROWS = 2048
SCAN_CHUNKS = SUBLANES


def _s5_operands(a_re, a_im, log_dt, b_re, b_im, c_re, c_im):
    dt = jnp.exp(log_dt)[:, None]
    mag, ang = jnp.exp(a_re * dt), a_im * dt
    lr, li = mag * jnp.cos(ang), mag * jnp.sin(ang)
    den = a_re * a_re + a_im * a_im
    cr = ((lr - 1.0) * a_re + li * a_im) / den
    ci = (li * a_re - (lr - 1.0) * a_im) / den
    bbr = cr[..., None] * b_re - ci[..., None] * b_im
    bbi = cr[..., None] * b_im + ci[..., None] * b_re
    nb = S5_BLOCKS
    lam = jnp.stack([lr.reshape(nb, 2, S5_HALF), li.reshape(nb, 2, S5_HALF)], axis=2)
    eye4, eye2 = jnp.eye(4, dtype=F32), jnp.eye(2, dtype=F32)
    bb = jnp.stack([bbr, bbi], axis=0).reshape(2, nb, 2, 4, S5_STATE, S5_GROUP)
    bmat = jnp.einsum("rbhgpc,kg,jh->bhjkcrgp", bb, eye4, eye2).reshape(nb, 2, 128, 2 * S5_HALF)
    cc = jnp.stack([c_re, -c_im], axis=0).reshape(2, nb, 2, 4, S5_GROUP, S5_STATE)
    cmat = jnp.einsum("rbhgcp,kg,jh->bhrgpjkc", cc, eye4, eye2).reshape(nb, 2, 2 * S5_HALF, 128)
    return lam, bmat, cmat


def _time_to_scan_order(a):
    s, d = a.shape
    return a.reshape(SCAN_CHUNKS, s // SCAN_CHUNKS, d).transpose(1, 0, 2).reshape(s, d)


def _scan_to_time_order(a):
    s, d = a.shape
    return a.reshape(s // SCAN_CHUNKS, SCAN_CHUNKS, d).transpose(1, 0, 2).reshape(s, d)


def _scan_chunks(xr_ref, xi_ref, lr, li, nst, reverse, after_step=None, state=None):
    lanes = lr.shape[1]
    lr8, li8 = jnp.broadcast_to(lr, (SUBLANES, lanes)), jnp.broadcast_to(li, (SUBLANES, lanes))
    zero8 = jnp.zeros((SUBLANES, lanes), F32)

    def rows_of(n):
        s = (nst - 1 - n) if reverse else n
        return s, pl.ds(pl.multiple_of(s * SUBLANES, SUBLANES), SUBLANES)

    def local(n, carry):
        pr, pi = carry
        _, rows = rows_of(n)
        nr = lr8 * pr - li8 * pi + xr_ref[rows, :]
        ni = lr8 * pi + li8 * pr + xi_ref[rows, :]
        xr_ref[rows, :] = nr
        xi_ref[rows, :] = ni
        return nr, ni

    er, ei = lax.fori_loop(0, nst, local, (zero8, zero8))
    pr, pi = lr, li
    for _ in range(int(math.log2(nst))):
        pr, pi = pr * pr - pi * pi, 2.0 * pr * pi
    tr = ti = jnp.zeros((1, lanes), F32)
    ent_r, ent_i = [None] * SCAN_CHUNKS, [None] * SCAN_CHUNKS
    for k in (reversed(range(SCAN_CHUNKS)) if reverse else range(SCAN_CHUNKS)):
        ent_r[k], ent_i[k] = tr, ti
        tr, ti = er[k:k + 1] + (pr * tr - pi * ti), ei[k:k + 1] + (pr * ti + pi * tr)
    in_r, in_i = jnp.concatenate(ent_r, axis=0), jnp.concatenate(ent_i, axis=0)

    def fix(n, carry):
        wr, wi, st = carry
        s, rows = rows_of(n)
        nr = xr_ref[rows, :] + (wr * in_r - wi * in_i)
        ni = xi_ref[rows, :] + (wr * in_i + wi * in_r)
        xr_ref[rows, :] = nr
        xi_ref[rows, :] = ni
        if after_step is not None:
            st = after_step(s, nr, ni, st)
        return wr * lr8 - wi * li8, wr * li8 + wi * lr8, st

    _, _, state = lax.fori_loop(0, nst, fix, (lr8, li8, state))
    return in_r, in_i, state


def _s5_fill_states(u_ref, bm, xr_ref, xi_ref, s):
    rc = min(S5_ROWS, s)

    def fill(r, _):
        rows = pl.ds(pl.multiple_of(r * rc, rc), rc)
        bu = jnp.dot(u_ref[rows, :].astype(MXU_DTYPE), bm, preferred_element_type=F32)
        xr_ref[rows, :] = bu[:, :S5_HALF]
        xi_ref[rows, :] = bu[:, S5_HALF:]
        return 0
    lax.fori_loop(0, s // rc, fill, 0)


def _s5_specs():
    return [pl.BlockSpec((1, 2, 2, S5_HALF), lambda b: (b, 0, 0, 0)),
            pl.BlockSpec((1, 2, 128, 2 * S5_HALF), lambda b: (b, 0, 0, 0)),
            pl.BlockSpec((1, 2, 2 * S5_HALF, 128), lambda b: (b, 0, 0, 0)),
            pl.BlockSpec((1, LANES), lambda b: (0, b))]


def _s5_scan_fwd(u, lam, bmat, cmat, dvec):
    s = u.shape[0]
    nst = s // SCAN_CHUNKS
    rc = min(S5_ROWS, s)

    def body(u_ref, lam_ref, b_ref, c_ref, d_ref, y_ref, xr_ref, xi_ref):
        y_ref[...] = u_ref[...] * d_ref[...]
        for hb in range(2):
            _s5_fill_states(u_ref, b_ref[0, hb], xr_ref, xi_ref, s)
            _scan_chunks(xr_ref, xi_ref, lam_ref[0, hb, 0:1, :], lam_ref[0, hb, 1:2, :], nst, False)
            cm = c_ref[0, hb]

            def emit(r, _, cm=cm):
                rows = pl.ds(pl.multiple_of(r * rc, rc), rc)
                y_ref[rows, :] += (jnp.dot(xr_ref[rows, :].astype(MXU_DTYPE), cm[:S5_HALF], preferred_element_type=F32)
                                   + jnp.dot(xi_ref[rows, :].astype(MXU_DTYPE), cm[S5_HALF:], preferred_element_type=F32))
                return 0
            lax.fori_loop(0, s // rc, emit, 0)

    blk = pl.BlockSpec((s, LANES), lambda b: (0, b))
    return pl.pallas_call(
        body, name="s5_scan_fwd", grid=(S5_BLOCKS,), in_specs=[blk] + _s5_specs(), out_specs=blk,
        out_shape=jax.ShapeDtypeStruct(u.shape, F32),
        scratch_shapes=[pltpu.VMEM((s, S5_HALF), F32)] * 2,
        compiler_params=_params(("parallel",)))(u, lam, bmat, cmat, dvec)


def _s5_scan_bwd(u, dy, lam, bmat, cmat, dvec):
    s = u.shape[0]
    nst = s // SCAN_CHUNKS
    rc = min(S5_ROWS, s)
    nt = (((1,), (1,)), ((), ()))
    tn = (((0,), (0,)), ((), ()))

    def body(u_ref, dy_ref, lam_ref, b_ref, c_ref, d_ref, du_ref, db_ref, dc_ref, dl_ref, dd_ref,
             xr_ref, xi_ref, gr_ref, gi_ref):
        du_ref[...] = dy_ref[...] * d_ref[...]
        dd_ref[...] = jnp.sum(dy_ref[...] * u_ref[...], axis=0, keepdims=True)
        db_ref[...] = jnp.zeros_like(db_ref)
        dc_ref[...] = jnp.zeros_like(dc_ref)
        for hb in range(2):
            bm, cm = b_ref[0, hb], c_ref[0, hb]
            lr, li = lam_ref[0, hb, 0:1, :], lam_ref[0, hb, 1:2, :]
            _s5_fill_states(u_ref, bm, xr_ref, xi_ref, s)
            xin_r, xin_i, _ = _scan_chunks(xr_ref, xi_ref, lr, li, nst, False)

            def fill_g(r, _, cm=cm):
                rows = pl.ds(pl.multiple_of(r * rc, rc), rc)
                g = lax.dot_general(dy_ref[rows, :].astype(MXU_DTYPE), cm, nt, preferred_element_type=F32)
                gr_ref[rows, :] = g[:, :S5_HALF]
                gi_ref[rows, :] = g[:, S5_HALF:]
                return 0
            lax.fori_loop(0, s // rc, fill_g, 0)
            def lam_grad(st, g_r, g_i, acc, xin_r=xin_r, xin_i=xin_i):
                prev = pl.ds(pl.multiple_of(jnp.maximum(st - 1, 0) * SUBLANES, SUBLANES), SUBLANES)
                x_r = jnp.where(st > 0, xr_ref[prev, :], xin_r)
                x_i = jnp.where(st > 0, xi_ref[prev, :], xin_i)
                return acc[0] + (g_r * x_r + g_i * x_i), acc[1] + (g_i * x_r - g_r * x_i)

            zero8 = jnp.zeros((SUBLANES, S5_HALF), F32)
            _, _, (a_r, a_i) = _scan_chunks(gr_ref, gi_ref, lr, -li, nst, True, after_step=lam_grad, state=(zero8, zero8))
            dl_ref[0, hb] = jnp.concatenate([jnp.sum(a_r, axis=0, keepdims=True),
                                             jnp.sum(a_i, axis=0, keepdims=True)], axis=0)

            def emit(r, _, bm=bm, hb=hb):
                rows = pl.ds(pl.multiple_of(r * rc, rc), rc)
                g = jnp.concatenate([gr_ref[rows, :], gi_ref[rows, :]], axis=1).astype(MXU_DTYPE)
                x = jnp.concatenate([xr_ref[rows, :], xi_ref[rows, :]], axis=1).astype(MXU_DTYPE)
                du_ref[rows, :] += lax.dot_general(g, bm, nt, preferred_element_type=F32)
                db_ref[0, hb] += lax.dot_general(u_ref[rows, :].astype(MXU_DTYPE), g, tn, preferred_element_type=F32)
                dc_ref[0, hb] += lax.dot_general(dy_ref[rows, :].astype(MXU_DTYPE), x, tn, preferred_element_type=F32)
                return 0
            lax.fori_loop(0, s // rc, emit, 0)

    blk = pl.BlockSpec((s, LANES), lambda b: (0, b))
    mat = pl.BlockSpec((1, 2, 128, 2 * S5_HALF), lambda b: (b, 0, 0, 0))
    return pl.pallas_call(
        body, name="s5_scan_bwd", grid=(S5_BLOCKS,), in_specs=[blk, blk] + _s5_specs(),
        out_specs=[blk, mat, mat, pl.BlockSpec((1, 2, 2, S5_HALF), lambda b: (b, 0, 0, 0)),
                   pl.BlockSpec((1, LANES), lambda b: (0, b))],
        out_shape=[jax.ShapeDtypeStruct(u.shape, F32),
                   jax.ShapeDtypeStruct((S5_BLOCKS, 2, 128, 2 * S5_HALF), F32),
                   jax.ShapeDtypeStruct((S5_BLOCKS, 2, 128, 2 * S5_HALF), F32),
                   jax.ShapeDtypeStruct((S5_BLOCKS, 2, 2, S5_HALF), F32),
                   jax.ShapeDtypeStruct((1, D_MODEL), F32)],
        scratch_shapes=[pltpu.VMEM((s, S5_HALF), F32)] * 4,
        compiler_params=_params(("parallel",)))(u, dy, lam, bmat, cmat, dvec)


_GELU_C = math.sqrt(2.0 / math.pi)


def _gelu_parts(y):
    inner = _GELU_C * (y + 0.044715 * y * y * y)
    th = jnp.tanh(inner)
    return 0.5 * y * (1.0 + th), th


def _s5_fwd(x, gain, w_in, ssm, dvec, w_glu):
    lam, bmat, cmat = ssm
    h = _rms_fwd(x, gain, "mix_norm")
    u = _mm(h, w_in, name="s5_in")
    y = _scan_to_time_order(_s5_scan_fwd(_time_to_scan_order(u), lam, bmat.astype(MXU_DTYPE), cmat.astype(MXU_DTYPE), dvec))
    g = _ew(lambda yb: ((_gelu_parts(yb)[0],), ()), [y], outs=[(D_MODEL, BF16)], name="s5_gelu")[0]
    vg = _mm(g, w_glu, name="s5_glu", out_dtype=BF16)

    def glu_fn(vb, gb, xb):
        return (xb + vb.astype(F32) * _sigmoid(gb.astype(F32)),), ()
    x1 = _ew(glu_fn, [(vg, D_MODEL, 0), (vg, D_MODEL, 1), x], outs=[(D_MODEL, F32)], name="s5_gate")[0]
    return x1, (x, h, u, y, g, vg)


def _s5_bwd(dx1, saved, gain, w_in, ssm, dvec, w_glu):
    x, h, u, y, g, vg = saved
    lam, bmat, cmat = ssm

    def dglu_fn(db, vb, gb):
        vf, sg = vb.astype(F32), _sigmoid(gb.astype(F32))
        return (jnp.concatenate([db * sg, db * vf * sg * (1.0 - sg)], axis=1),), ()
    dvg = _ew(dglu_fn, [dx1, (vg, D_MODEL, 0), (vg, D_MODEL, 1)], outs=[(2 * D_MODEL, BF16)], name="s5_dgate")[0]
    dw_glu = _mm(g, dvg, ta=True, name="s5_dw_glu", out_dtype=BF16)
    dg = _mm(dvg, w_glu, tb=True, name="s5_dg")

    def dgelu_fn(dgb, yb):
        _, th = _gelu_parts(yb)
        dinner = _GELU_C * (1.0 + 3.0 * 0.044715 * yb * yb)
        return (dgb * (0.5 * (1.0 + th) + 0.5 * yb * (1.0 - th * th) * dinner),), ()
    dy = _ew(dgelu_fn, [dg, y], outs=[(D_MODEL, F32)], name="s5_dgelu")[0]
    du_s, dbm, dct, dlam, ddvec = _s5_scan_bwd(_time_to_scan_order(u), _time_to_scan_order(dy), lam,
                                               bmat.astype(MXU_DTYPE), cmat.astype(MXU_DTYPE), dvec)
    du = _scan_to_time_order(du_s)
    dw_in = _mm(h, du, ta=True, name="s5_dw_in", out_dtype=BF16)
    dx, dgain = _mm_rms_bwd(du, w_in, x, gain, dx1, name="s5_dh")
    return dx, dgain, dw_in, (dlam, dbm, jnp.swapaxes(dct, 2, 3)), ddvec, dw_glu


POOL_BLOCK = 512
N_POOL_GROUPS = len(POOL_WINDOWS)


def _pool_bands(gi, i, t):
    w = jnp.left_shift(2, gi)
    r = lax.broadcasted_iota(jnp.int32, (t, t), 0)
    c = lax.broadcasted_iota(jnp.int32, (t, t), 1)
    inside = ((c <= r) & (c > r - w)).astype(MXU_DTYPE)
    before = (c > r - w + t).astype(MXU_DTYPE)

    def inv_count(block):
        pos = block * t + lax.broadcasted_iota(jnp.int32, (t, 1), 0)
        return 1.0 / jnp.minimum(pos + 1, w).astype(F32)
    return inside, before, inv_count


def _pool_fwd(x, gain, w_grp, b_grp, scale):
    s = x.shape[0]
    t = min(POOL_BLOCK, s)
    h = _rms_fwd(x, gain, "mix_norm")

    def body(h_ref, hp_ref, w_ref, b_ref, sc_ref, x_ref, x1_ref, diff_ref):
        gi, i = pl.program_id(0), pl.program_id(1)
        inside, before, inv_count = _pool_bands(gi, i, t)
        hc = h_ref[...]
        tot = jnp.dot(inside, hc.astype(MXU_DTYPE), preferred_element_type=F32)
        prev = jnp.dot(before, hp_ref[...].astype(MXU_DTYPE), preferred_element_type=F32)
        tot = tot + jnp.where(i > 0, prev, 0.0)
        diff = (tot * inv_count(i) - hc.astype(F32)).astype(diff_ref.dtype)
        y = (jnp.dot(diff.astype(MXU_DTYPE), w_ref[0], preferred_element_type=F32) + b_ref[...]) * sc_ref[...]
        diff_ref[...] = diff
        x1_ref[...] = x_ref[...] + y

    blk = pl.BlockSpec((t, POOL_WIDTH), lambda gi, i: (i, gi))
    vec = pl.BlockSpec((1, POOL_WIDTH), lambda gi, i: (0, gi))
    x1, diff = pl.pallas_call(
        body, name="pool_fwd", grid=(N_POOL_GROUPS, s // t),
        in_specs=[blk, pl.BlockSpec((t, POOL_WIDTH), lambda gi, i: (jnp.maximum(i - 1, 0), gi)),
                  pl.BlockSpec((1, POOL_WIDTH, POOL_WIDTH), lambda gi, i: (gi, 0, 0)), vec, vec, blk],
        out_specs=[blk, blk],
        out_shape=[jax.ShapeDtypeStruct(x.shape, F32), jax.ShapeDtypeStruct(x.shape, BF16)],
        compiler_params=_params(("parallel", "arbitrary")))(h, h, w_grp, b_grp, scale, x)
    return x1, (x, diff)


def _pool_bwd(dx1, saved, gain, w_grp, b_grp, scale):
    x, diff = saved
    s = x.shape[0]
    t = min(POOL_BLOCK, s)
    nb = s // t

    def body1(dx_ref, diff_ref, w_ref, b_ref, sc_ref, dd_ref, dw_ref, db_ref, dsc_ref):
        i = pl.program_id(1)

        @pl.when(i == 0)
        def _():
            dw_ref[...] = jnp.zeros_like(dw_ref)
            db_ref[...] = jnp.zeros_like(db_ref)
            dsc_ref[...] = jnp.zeros_like(dsc_ref)

        dfb = diff_ref[...].astype(MXU_DTYPE)
        ypre = jnp.dot(dfb, w_ref[0], preferred_element_type=F32) + b_ref[...]
        dxb = dx_ref[...]
        dy = dxb * sc_ref[...]
        dsc_ref[...] += jnp.sum(dxb * ypre, axis=0, keepdims=True)
        db_ref[...] += jnp.sum(dy, axis=0, keepdims=True)
        dyb = dy.astype(MXU_DTYPE)
        dw_ref[0] += lax.dot_general(dfb, dyb, (((0,), (0,)), ((), ())), preferred_element_type=F32)
        dd_ref[...] = lax.dot_general(dyb, w_ref[0], (((1,), (1,)), ((), ())), preferred_element_type=F32)

    blk = pl.BlockSpec((t, POOL_WIDTH), lambda gi, i: (i, gi))
    vec = pl.BlockSpec((1, POOL_WIDTH), lambda gi, i: (0, gi))
    mat = pl.BlockSpec((1, POOL_WIDTH, POOL_WIDTH), lambda gi, i: (gi, 0, 0))
    ddiff, dw, db, dsc = pl.pallas_call(
        body1, name="pool_bwd_map", grid=(N_POOL_GROUPS, nb), in_specs=[blk, blk, mat, vec, vec],
        out_specs=[blk, mat, vec, vec],
        out_shape=[jax.ShapeDtypeStruct(x.shape, F32), jax.ShapeDtypeStruct(w_grp.shape, F32),
                   jax.ShapeDtypeStruct((1, D_MODEL), F32), jax.ShapeDtypeStruct((1, D_MODEL), F32)],
        compiler_params=_params(("parallel", "arbitrary")))(dx1, diff, w_grp, b_grp, scale)

    def body2(dc_ref, dn_ref, dh_ref):
        gi, i = pl.program_id(0), pl.program_id(1)
        inside, before, inv_count = _pool_bands(gi, i, t)
        tn = (((0,), (0,)), ((), ()))
        dc = dc_ref[...]
        tot = lax.dot_general(inside, (dc * inv_count(i)).astype(MXU_DTYPE), tn, preferred_element_type=F32)
        nxt = lax.dot_general(before, (dn_ref[...] * inv_count(i + 1)).astype(MXU_DTYPE), tn, preferred_element_type=F32)
        dh_ref[...] = tot + jnp.where(i < nb - 1, nxt, 0.0) - dc

    dh = pl.pallas_call(
        body2, name="pool_bwd_window", grid=(N_POOL_GROUPS, nb),
        in_specs=[blk, pl.BlockSpec((t, POOL_WIDTH), lambda gi, i: (jnp.minimum(i + 1, nb - 1), gi))],
        out_specs=blk, out_shape=jax.ShapeDtypeStruct(x.shape, F32),
        compiler_params=_params(("parallel", "parallel")))(ddiff, ddiff)
    dx, dgain = _rms_bwd(x, gain, dh, dx1, "mix_norm_bwd")
    return dx, dgain, dw, db, dsc


MESH_ID = pl.DeviceIdType.MESH
ANY = pl.BlockSpec(memory_space=pl.ANY)


def _place():
    x, y, c = lax.axis_index("x"), lax.axis_index("y"), lax.axis_index("c")
    other_chips = [(1 - x, y), (x, 1 - y), (1 - x, 1 - y)]
    return x, y, c, other_chips


def _chip_index(chip):
    return 2 * chip[0] + chip[1]


def _remote(src, dst, send_sems, recv_sems, n, to):
    return pltpu.make_async_remote_copy(src_ref=src, dst_ref=dst, send_sem=send_sems.at[n], recv_sem=recv_sems.at[n],
                                        device_id=to, device_id_type=MESH_ID)


def _gather_weights(ws):
    n = len(ws)
    from_x, relay_x, from_y, relay_y, sib_x, sib_y, sib_d0, sib_d1, sib_own = range(9)
    slots = 9

    def body(*refs):
        w_refs, out_refs = refs[:n], refs[n:2 * n]
        send_sems, recv_sems = refs[2 * n:]
        x, y, c, (xn, yn, dn) = _place()
        k = _chip_index((x, y))
        me, sibling = (x, y, c), (x, y, 1 - c)

        def copy(ref, t, slot, to):
            return _remote(ref, ref, send_sems, recv_sems, slots * t + slot, to)

        def quarter(ref, q):
            rows = ref.shape[0] // 2
            return ref.at[pl.ds(q * rows, rows)]

        started = []

        def start(cp):
            cp.start()
            started.append(cp)

        for t in range(n):
            for slot, chip in ((from_x, xn), (from_y, yn)):
                start(_remote(w_refs[t].at[c], out_refs[t].at[k, c], send_sems, recv_sems, slots * t + slot, (*chip, c)))
            start(_remote(w_refs[t], out_refs[t].at[k], send_sems, recv_sems, slots * t + sib_own, sibling))
        for t in range(n):
            got = out_refs[t].at[_chip_index(xn), c]
            copy(got, t, from_x, me).wait_recv()
            start(copy(quarter(got, 0), t, relay_y, (*yn, c)))
            start(copy(got, t, sib_x, sibling))
        for t in range(n):
            got = out_refs[t].at[_chip_index(yn), c]
            copy(got, t, from_y, me).wait_recv()
            start(copy(quarter(got, 1), t, relay_x, (*xn, c)))
            start(copy(got, t, sib_y, sibling))
        for t in range(n):
            got = out_refs[t].at[_chip_index(dn), c]
            for q, slot, sib_slot in ((0, relay_y, sib_d0), (1, relay_x, sib_d1)):
                copy(quarter(got, q), t, slot, me).wait_recv()
                start(copy(quarter(got, q), t, sib_slot, sibling))
        for t in range(n):
            for chip, slot in ((xn, sib_x), (yn, sib_y)):
                copy(out_refs[t].at[_chip_index(chip), 1 - c], t, slot, me).wait_recv()
            theirs = out_refs[t].at[_chip_index(dn), 1 - c]
            copy(quarter(theirs, 0), t, sib_d0, me).wait_recv()
            copy(quarter(theirs, 1), t, sib_d1, me).wait_recv()
            copy(out_refs[t].at[k], t, sib_own, me).wait_recv()
        for cp in started:
            cp.wait_send()

    return pl.pallas_call(
        body, name="gather_weights", in_specs=[ANY] * n, out_specs=[ANY] * n,
        out_shape=[jax.ShapeDtypeStruct((N_CHIPS,) + w.shape, w.dtype) for w in ws],
        scratch_shapes=[pltpu.SemaphoreType.DMA((slots * n,)), pltpu.SemaphoreType.DMA((slots * n,))],
    )(*ws)


def _swap_halves(gs):
    n = len(gs)

    def body(*refs):
        g_refs, out_refs = refs[:n], refs[n:2 * n]
        send_sems, recv_sems = refs[2 * n:]
        x, y, c, _ = _place()
        copies = [_remote(g_refs[t].at[s, 1 - c], out_refs[t].at[s], send_sems, recv_sems, N_CHIPS * t + s, (x, y, 1 - c))
                  for t in range(n) for s in range(N_CHIPS)]
        for cp in copies:
            cp.start()
        for cp in copies:
            cp.wait_recv()
        for cp in copies:
            cp.wait_send()

    return pl.pallas_call(
        body, name="swap_halves", in_specs=[ANY] * n, out_specs=[ANY] * n,
        out_shape=[jax.ShapeDtypeStruct((N_CHIPS,) + g.shape[2:], g.dtype) for g in gs],
        scratch_shapes=[pltpu.SemaphoreType.DMA((N_CHIPS * n,)), pltpu.SemaphoreType.DMA((N_CHIPS * n,))],
    )(*gs)


def _neighbour_exchange(srcs, out_shapes, name):
    n = len(out_shapes)

    def body(*refs):
        in_refs, out_refs = refs[:n], refs[n:2 * n]
        send_sems, recv_sems = refs[2 * n:]
        x, y, c, chips = _place()
        sends = []
        for t in range(n):
            for slot, (src, chip) in enumerate(zip(srcs(in_refs[t], chips), chips[:2])):
                sends.append(_remote(src, out_refs[t].at[slot], send_sems, recv_sems, 2 * t + slot, (*chip, c)))
        for cp in sends:
            cp.start()
        for t in range(n):
            for slot in range(2):
                landed = out_refs[t].at[slot]
                _remote(landed, landed, send_sems, recv_sems, 2 * t + slot, (x, y, c)).wait_recv()
        for cp in sends:
            cp.wait_send()

    return pl.pallas_call(
        body, name=name, in_specs=[ANY] * n, out_specs=[ANY] * n, out_shape=out_shapes,
        scratch_shapes=[pltpu.SemaphoreType.DMA((2 * n,)), pltpu.SemaphoreType.DMA((2 * n,))])


def _relay_partials(ps):
    def srcs(p_ref, chips):
        half = p_ref.shape[1] // 2
        diagonal = p_ref.at[_chip_index(chips[2])]
        return diagonal.at[pl.ds(0, half)], diagonal.at[pl.ds(half, half)]

    shapes = [jax.ShapeDtypeStruct((2, p.shape[1] // 2, p.shape[2]), p.dtype) for p in ps]
    return _neighbour_exchange(srcs, shapes, "relay_partials")(*ps)


def _merge_relayed(p, relayed, targets, name):
    rows, cols = p.shape[1:]
    tr = _row_tile(rows // 2, SUM_ROWS)
    per = rows // 2 // tr

    def body(x_ref, y_ref, p_ref, r_ref, o_ref):
        to, q = pl.program_id(0), pl.program_id(1)
        extra = jnp.where(q == 1 - to, r_ref[0].astype(F32), 0.0)
        o_ref[0] = (p_ref[0].astype(F32) + extra).astype(o_ref.dtype)

    return pl.pallas_call(
        body, name=name,
        grid_spec=pltpu.PrefetchScalarGridSpec(
            num_scalar_prefetch=2, grid=(2, 2, per),
            in_specs=[pl.BlockSpec((1, tr, cols), lambda to, q, i, xs, ys: (jnp.where(to == 0, xs[0], ys[0]), q * per + i, 0)),
                      pl.BlockSpec((1, tr, cols), lambda to, q, i, xs, ys: (1 - to, i, 0))],
            out_specs=pl.BlockSpec((1, tr, cols), lambda to, q, i, xs, ys: (to, q * per + i, 0))),
        out_shape=jax.ShapeDtypeStruct((2, rows, cols), p.dtype),
        compiler_params=_params(("parallel", "parallel", "parallel")))(targets[0], targets[1], p, relayed)


def _scatter_partials(ms):
    shapes = [jax.ShapeDtypeStruct(m.shape, m.dtype) for m in ms]
    return _neighbour_exchange(lambda m_ref, chips: (m_ref.at[0], m_ref.at[1]), shapes, "scatter_partials")(*ms)


def _share_half(fs):
    n = len(fs)

    def body(*refs):
        f_refs, out_refs = refs[:n], refs[n:2 * n]
        send_sems, recv_sems = refs[2 * n:]
        x, y, c, _ = _place()
        sends = [_remote(f_refs[t], out_refs[t].at[c], send_sems, recv_sems, t, (x, y, 1 - c)) for t in range(n)]
        for cp in sends:
            cp.start()
        for t in range(n):
            theirs = out_refs[t].at[1 - c]
            _remote(theirs, theirs, send_sems, recv_sems, t, (x, y, c)).wait_recv()
        for cp in sends:
            cp.wait_send()

    return pl.pallas_call(
        body, name="share_half", in_specs=[ANY] * n, out_specs=[ANY] * n,
        out_shape=[jax.ShapeDtypeStruct((2,) + f.shape, f.dtype) for f in fs],
        scratch_shapes=[pltpu.SemaphoreType.DMA((n,)), pltpu.SemaphoreType.DMA((n,))],
    )(*fs)


def _gather_small(v):
    def body(v_ref, out_ref, send_sems, recv_sems):
        x, y, c, chips = _place()
        me, sibling = (x, y, c), (x, y, 1 - c)

        def slot(px, py, pc):
            return out_ref.at[4 * px + 2 * py + pc]

        first = [_remote(v_ref, slot(*me), send_sems, recv_sems, 0, sibling)]
        first += [_remote(v_ref, slot(*me), send_sems, recv_sems, 1 + j, (*chip, c)) for j, chip in enumerate(chips)]
        for cp in first:
            cp.start()
        passed = [_remote(slot(*chip, c), slot(*chip, c), send_sems, recv_sems, 4 + j, sibling)
                  for j, chip in enumerate(chips)]
        for j, chip in enumerate(chips):
            _remote(slot(*chip, c), slot(*chip, c), send_sems, recv_sems, 1 + j, me).wait_recv()
            passed[j].start()
        _remote(slot(*sibling), slot(*sibling), send_sems, recv_sems, 0, me).wait_recv()
        for j, chip in enumerate(chips):
            _remote(slot(*chip, 1 - c), slot(*chip, 1 - c), send_sems, recv_sems, 4 + j, me).wait_recv()
        for cp in first + passed:
            cp.wait_send()

    gathered = pl.pallas_call(
        body, name="gather_small", in_specs=[ANY], out_specs=ANY,
        out_shape=jax.ShapeDtypeStruct((N_DEV,) + v.shape, v.dtype),
        scratch_shapes=[pltpu.SemaphoreType.DMA((7,)), pltpu.SemaphoreType.DMA((7,))],
    )(v)
    device = 4 * lax.axis_index("x") + 2 * lax.axis_index("y") + lax.axis_index("c")
    return lax.dynamic_update_index_in_dim(gathered, v, device, 0)


SMALL_ROWS = 256
SUM_ROWS = 512
BF16_ROWS = 16


def _row_tile(rows, want):
    for t in range(min(rows, want) // BF16_ROWS * BF16_ROWS, 0, -BF16_ROWS):
        if rows % t == 0:
            return t
    return rows


def _pair_sum(g, r, core, name):
    rows, cols = g.shape[2:]
    tr = _row_tile(rows, SUM_ROWS)

    def body(c_ref, g_ref, r_ref, o_ref):
        o_ref[0] = (g_ref[0, 0].astype(F32) + r_ref[0].astype(F32)).astype(o_ref.dtype)

    return pl.pallas_call(
        body, name=name,
        grid_spec=pltpu.PrefetchScalarGridSpec(
            num_scalar_prefetch=1, grid=(N_CHIPS, rows // tr),
            in_specs=[pl.BlockSpec((1, 1, tr, cols), lambda s, i, c_ref: (s, c_ref[0], i, 0)),
                      pl.BlockSpec((1, tr, cols), lambda s, i, c_ref: (s, i, 0))],
            out_specs=pl.BlockSpec((1, tr, cols), lambda s, i, c_ref: (s, i, 0))),
        out_shape=jax.ShapeDtypeStruct(r.shape, BF16),
        compiler_params=_params(("parallel", "parallel")))(core, g, r)


def _chip_sum(p, recv, chip, name):
    rows, cols = p.shape[1:]
    tr = _row_tile(rows, SUM_ROWS)
    n_recv = recv.shape[0]

    def body(k_ref, p_ref, r_ref, o_ref):
        acc = p_ref[0].astype(F32)
        for j in range(n_recv):
            acc = acc + r_ref[j].astype(F32)
        o_ref[...] = acc

    return pl.pallas_call(
        body, name=name,
        grid_spec=pltpu.PrefetchScalarGridSpec(
            num_scalar_prefetch=1, grid=(rows // tr,),
            in_specs=[pl.BlockSpec((1, tr, cols), lambda i, k_ref: (k_ref[0], i, 0)),
                      pl.BlockSpec((n_recv, tr, cols), lambda i, k_ref: (0, i, 0))],
            out_specs=pl.BlockSpec((tr, cols), lambda i, k_ref: (i, 0))),
        out_shape=jax.ShapeDtypeStruct((rows, cols), F32),
        compiler_params=_params(("parallel",)))(chip, p, recv)


def _sum_blocks(a, name):
    n, rows, cols = a.shape
    tr = _row_tile(rows, SUM_ROWS)

    def body(a_ref, o_ref):
        acc = a_ref[0].astype(F32)
        for s in range(1, n):
            acc = acc + a_ref[s].astype(F32)
        o_ref[...] = acc

    return pl.pallas_call(
        body, name=name, grid=(rows // tr,),
        in_specs=[pl.BlockSpec((n, tr, cols), lambda i: (0, i, 0))],
        out_specs=pl.BlockSpec((tr, cols), lambda i: (i, 0)),
        out_shape=jax.ShapeDtypeStruct((rows, cols), F32),
        compiler_params=_params(("parallel",)))(a)


def _adamw(w, g, m, v, name):
    def fn(wb, gb, mb, vb):
        m2 = ADAM_B1 * mb + (1.0 - ADAM_B1) * gb
        v2 = ADAM_B2 * vb + (1.0 - ADAM_B2) * (gb * gb)
        m_hat = m2 / (1.0 - ADAM_B1 ** ADAM_STEP)
        v_hat = v2 / (1.0 - ADAM_B2 ** ADAM_STEP)
        delta = -ADAM_LR * (m_hat / (jnp.sqrt(v_hat) + ADAM_EPS) + ADAM_WD * wb)
        return (delta, m2, v2), ()
    c = w.shape[1]
    return _ew(fn, [w, g, m, v], outs=[(c, F32)] * 3, name=name)


WEIGHTS = ["mix_norm_g", "ffn_norm_g", "final_norm_g", "fox_w_in", "fox_b_f", "fox_w_out", "s5_w_in", "s5_a_re",
           "s5_a_im", "s5_log_dt", "s5_b_re", "s5_b_im", "s5_c_re", "s5_c_im", "s5_d", "s5_w_glu", "pool_w",
           "pool_b", "pool_scale", "ffn_w_gate_up", "ffn_w_down"]
BIG = {"fox_w_in": 2, "fox_w_out": 1, "s5_w_in": 1, "s5_w_glu": 2, "pool_w": 2, "ffn_w_gate_up": 2, "ffn_w_down": 1}
BY_CHIP = ("ffn_w_gate_up", "ffn_w_down")
SLICED = ("pool_b", "pool_scale")
SMALL = [n for n in WEIGHTS if n not in BIG]


def _to_natural(cm, axis):
    moved = jnp.moveaxis(cm, 0, axis)
    shape = moved.shape[:axis] + (moved.shape[axis] * moved.shape[axis + 1],) + moved.shape[axis + 2:]
    return moved.reshape(shape)


def _to_chip_major(nat, axis):
    shape = nat.shape[:axis] + (N_CHIPS, nat.shape[axis] // N_CHIPS) + nat.shape[axis + 1:]
    return jnp.moveaxis(nat.reshape(shape), axis, 0)


def _halves_view(shape):
    return (2, int(np.prod(shape[:-1])) // 2, shape[-1])


def _pack_small(parts):
    flat = jnp.concatenate([p.reshape(-1).astype(F32) for p in parts])
    pad = (-flat.shape[0]) % (SMALL_ROWS * LANES)
    return jnp.pad(flat, (0, pad)).reshape(-1, LANES)


def _unpack_small(buf, shapes):
    flat = buf.reshape(-1)
    out, off = [], 0
    for shp in shapes:
        n = int(np.prod(shp))
        out.append(flat[off:off + n].reshape(shp))
        off += n
    return out


def _local_step(x, target, w):
    grads = {}
    mixers = ("fox", "s5", "pool")
    saved = []
    ssm, ssm_pull = jax.vjp(_s5_operands, w["s5_a_re"][0], w["s5_a_im"][0], w["s5_log_dt"][0], w["s5_b_re"][0],
                            w["s5_b_im"][0], w["s5_c_re"][0], w["s5_c_im"][0])
    fox_w = []
    for j in range(w["fox_w_in"].shape[0]):
        w_in = w["fox_w_in"][j]
        w_f = jnp.pad(w_in[:, 3 * D_MODEL:], ((0, 0), (0, LANES - FOX_HEADS)))
        fox_w.append((w_in[:, :3 * D_MODEL], w_f, w["fox_w_out"][j]))
    for i in range(DEPTH):
        kind, j = mixers[i % 3], i // 3
        gain = w["mix_norm_g"][i]
        if kind == "fox":
            x1, sv = _fox_fwd(x, gain, fox_w[j][0], fox_w[j][1], w["fox_b_f"][j], fox_w[j][2])
        elif kind == "s5":
            x1, sv = _s5_fwd(x, gain, w["s5_w_in"][j], ssm, w["s5_d"], w["s5_w_glu"][j])
        else:
            x1, sv = _pool_fwd(x, gain, w["pool_w"][j], w["pool_b"], w["pool_scale"])
        x, sf = _ffn_fwd(x1, w["ffn_norm_g"][i], w["ffn_w_gate_up"], w["ffn_w_down"], i)
        saved.append((sv, sf))
    loss, dx, grads["final_norm_g"] = _loss_head(x, w["final_norm_g"], target)
    per_layer = {n: [None] * DEPTH for n in ("mix_norm_g", "ffn_norm_g")}
    fox_g = {n: [None] * len(fox_w) for n in ("fox_w_in", "fox_b_f", "fox_w_out")}
    for n in BY_CHIP:
        grads[n] = lax.empty(w[n].shape, BF16)
    for i in reversed(range(DEPTH)):
        kind, j = mixers[i % 3], i // 3
        sv, sf = saved[i]
        dx, per_layer["ffn_norm_g"][i], grads["ffn_w_gate_up"], grads["ffn_w_down"] = _ffn_bwd(
            dx, sf, w["ffn_norm_g"][i], w["ffn_w_gate_up"], w["ffn_w_down"], i, grads["ffn_w_gate_up"], grads["ffn_w_down"])
        gain = w["mix_norm_g"][i]
        if kind == "fox":
            dx, per_layer["mix_norm_g"][i], fox_g["fox_w_in"][j], fox_g["fox_b_f"][j], fox_g["fox_w_out"][j] = _fox_bwd(
                dx, sv, gain, fox_w[j][0], fox_w[j][1], fox_w[j][2])
        elif kind == "s5":
            dx, per_layer["mix_norm_g"][i], dw_in, dssm, dd, dw_glu = _s5_bwd(
                dx, sv, gain, w["s5_w_in"][j], ssm, w["s5_d"], w["s5_w_glu"][j])
            grads["s5_w_in"], grads["s5_w_glu"], grads["s5_d"] = dw_in[None], dw_glu[None], dd
            for n, g in zip(("s5_a_re", "s5_a_im", "s5_log_dt", "s5_b_re", "s5_b_im", "s5_c_re", "s5_c_im"), ssm_pull(dssm)):
                grads[n] = g[None]
        else:
            dx, per_layer["mix_norm_g"][i], dw, db, dsc = _pool_bwd(dx, sv, gain, w["pool_w"][j], w["pool_b"], w["pool_scale"])
            grads["pool_w"], grads["pool_b"], grads["pool_scale"] = dw[None].astype(BF16), db, dsc
    for n, parts in {**per_layer, **fox_g}.items():
        grads[n] = jnp.stack(parts)
    return loss, dx, grads


def kernel(x, mix_norm_g, ffn_norm_g, final_norm_g, fox_w_in, fox_b_f, fox_w_out, s5_w_in, s5_a_re, s5_a_im, s5_log_dt, s5_b_re, s5_b_im, s5_c_re, s5_c_im, s5_d, s5_w_glu, pool_w, pool_b, pool_scale, ffn_w_gate_up, ffn_w_down, loss_target, m_mix_norm_g, m_ffn_norm_g, m_final_norm_g, m_fox_w_in, m_fox_b_f, m_fox_w_out, m_s5_w_in, m_s5_a_re, m_s5_a_im, m_s5_log_dt, m_s5_b_re, m_s5_b_im, m_s5_c_re, m_s5_c_im, m_s5_d, m_s5_w_glu, m_pool_w, m_pool_b, m_pool_scale, m_ffn_w_gate_up, m_ffn_w_down, v_mix_norm_g, v_ffn_norm_g, v_final_norm_g, v_fox_w_in, v_fox_b_f, v_fox_w_out, v_s5_w_in, v_s5_a_re, v_s5_a_im, v_s5_log_dt, v_s5_b_re, v_s5_b_im, v_s5_c_re, v_s5_c_im, v_s5_d, v_s5_w_glu, v_pool_w, v_pool_b, v_pool_scale, v_ffn_w_gate_up, v_ffn_w_down):
    given = dict(locals())
    shard = {n: given[n] for n in WEIGHTS}
    chip = 2 * lax.axis_index("x") + lax.axis_index("y")
    core = lax.axis_index("c")

    views = {n: _halves_view(shard[n].shape) for n in BIG}
    own = [shard[n].astype(MXU_DTYPE).reshape(views[n]) for n in BIG]
    whole = {}
    for n, by_chip in zip(BIG, _gather_weights(own)):
        by_chip = by_chip.reshape((N_CHIPS,) + shard[n].shape)
        whole[n] = by_chip if n in BY_CHIP else _to_natural(by_chip, BIG[n])
    for n in SMALL:
        whole[n] = shard[n]
    sliced_shapes = [shard[n].shape for n in SLICED]
    by_chip = _gather_small(_pack_small([shard[n] for n in SLICED]))[0::2]
    slices = [_unpack_small(by_chip[k], sliced_shapes) for k in range(N_CHIPS)]
    for idx, n in enumerate(SLICED):
        whole[n] = jnp.concatenate([slices[k][idx] for k in range(N_CHIPS)], axis=-1)

    loss_part, dx, grads = _local_step(x[0], loss_target[0], whole)
    loss = lax.psum(loss_part, MESH_AXES)

    gs = [(grads[n] if n in BY_CHIP else _to_chip_major(grads[n].astype(BF16), BIG[n])).reshape((N_CHIPS,) + views[n])
          for n in BIG]
    core_id, chip_id = core.reshape(1).astype(jnp.int32), chip.reshape(1).astype(jnp.int32)
    partial = [_pair_sum(g, r, core_id, "pair_sum_" + n) for n, g, r in zip(BIG, gs, _swap_halves(gs))]
    x_chip, y_chip = 2 * (1 - lax.axis_index("x")) + lax.axis_index("y"), 2 * lax.axis_index("x") + 1 - lax.axis_index("y")
    neighbours = (x_chip.reshape(1).astype(jnp.int32), y_chip.reshape(1).astype(jnp.int32))
    merged = [_merge_relayed(p, r, neighbours, "merge_relayed_" + n) for n, p, r in zip(BIG, partial, _relay_partials(partial))]
    half = [_chip_sum(p, r, chip_id, "chip_sum_" + n) for n, p, r in zip(BIG, partial, _scatter_partials(merged))]
    grad = {n: lax.dynamic_update_index_in_dim(both, mine, core, 0).reshape(shard[n].shape)
            for n, mine, both in zip(BIG, half, _share_half(half))}

    small_sum = _sum_blocks(_gather_small(_pack_small([grads[n] for n in SMALL])), "small_sum")
    for n, g in zip(SMALL, _unpack_small(small_sum, [whole[n].shape for n in SMALL])):
        grad[n] = g
    for n in SLICED:
        width = shard[n].shape[-1]
        grad[n] = lax.dynamic_slice_in_dim(grad[n], chip * width, width, axis=-1)

    delta, new_m, new_v = {}, {}, {}
    for n in BIG:
        view = (-1, shard[n].shape[-1])
        res = _adamw(shard[n].reshape(view), grad[n].reshape(view), given["m_" + n].reshape(view),
                     given["v_" + n].reshape(view), "adamw_" + n)
        delta[n], new_m[n], new_v[n] = (r.reshape(shard[n].shape) for r in res)
    small_shapes = [shard[n].shape for n in SMALL]
    res = _adamw(_pack_small([shard[n] for n in SMALL]), _pack_small([grad[n] for n in SMALL]),
                 _pack_small([given["m_" + n] for n in SMALL]), _pack_small([given["v_" + n] for n in SMALL]), "adamw_small")
    for out, buf in zip((delta, new_m, new_v), res):
        for n, a in zip(SMALL, _unpack_small(buf, small_shapes)):
            out[n] = a
    return (loss, dx[None], *[grad[n] for n in WEIGHTS], *[delta[n] for n in WEIGHTS],
            *[new_m[n] for n in WEIGHTS], *[new_v[n] for n in WEIGHTS])
```
